```python
import jax, jax.numpy as jnp
from jax import lax
import numpy as np

D_MODEL = 1024
BATCH = 8
SEQ = 4096
DEPTH = 1

HEAD_DIM = 64
N_ATTN_HEADS = 16
ATTN_WIDTH = N_ATTN_HEADS * HEAD_DIM
ROPE_DIM = HEAD_DIM // 4
ROPE_THETA = 500000.0
DILATED_PATTERNS = ((128, 1), (512, 4), (2048, 16))

D_INNER = 1024
SSM_HEAD_DIM = 64
N_SSM_HEADS = D_INNER // SSM_HEAD_DIM
N_SSM_GROUPS = 4
HEADS_PER_GROUP = N_SSM_HEADS // N_SSM_GROUPS
D_STATE = 128
SSM_CONV = 3
CHUNK = 128
XBC_WIDTH = D_INNER + 2 * N_SSM_GROUPS * D_STATE

MIX_WIDTH = ATTN_WIDTH + D_INNER
IN_WIDTH = 3 * ATTN_WIDTH + D_INNER + XBC_WIDTH + 2 * N_SSM_HEADS

D_FF = 2816
FFN_CONV = 3
EPS = 1e-6

kernel_name = "hymba_dilated_ssd_convffn_encoder"


def rmsnorm(x, w):
    xf = x.astype(jnp.float32)
    y = xf * lax.rsqrt(jnp.mean(xf * xf, axis=-1, keepdims=True) + EPS)
    return (y * w.astype(jnp.float32)).astype(x.dtype)


def dwconv_centered(x, w, b):
    k = w.shape[1]
    rhs = w.T[:, None, :].astype(x.dtype)
    y = lax.conv_general_dilated(x, rhs, window_strides=(1,), padding=[(k // 2, k // 2)],
                                 dimension_numbers=("NWC", "WIO", "NWC"),
                                 feature_group_count=x.shape[-1])
    return y + b.astype(x.dtype)


def partial_rope(t, pos):
    half = ROPE_DIM // 2
    inv_freq = jnp.power(ROPE_THETA, -jnp.arange(half, dtype=jnp.float32) * 2.0 / ROPE_DIM)
    ang = pos[:, None] * inv_freq[None, :]
    cos = jnp.cos(ang)[None, :, None, :]
    sin = jnp.sin(ang)[None, :, None, :]
    tf = t.astype(jnp.float32)
    x1 = tf[..., :half]
    x2 = tf[..., half:ROPE_DIM]
    out = jnp.concatenate([x1 * cos - x2 * sin, x2 * cos + x1 * sin, tf[..., ROPE_DIM:]], axis=-1)
    return out.astype(t.dtype)


def band_attention(q, k, v, half):
    n, length, h, dh = q.shape
    blk = half
    nb = -(-length // blk)
    lp = nb * blk
    qb = jnp.pad(q, [(0, 0), (0, lp - length), (0, 0), (0, 0)]).reshape(n, nb, blk, h, dh)
    pad_kv = [(0, 0), (blk, lp - length + blk), (0, 0), (0, 0)]
    kb = jnp.pad(k, pad_kv).reshape(n, nb + 2, blk, h, dh)
    vb = jnp.pad(v, pad_kv).reshape(n, nb + 2, blk, h, dh)
    kw = jnp.concatenate([kb[:, :-2], kb[:, 1:-1], kb[:, 2:]], axis=2)
    vw = jnp.concatenate([vb[:, :-2], vb[:, 1:-1], vb[:, 2:]], axis=2)
    s = jnp.einsum("nbqhd,nbkhd->nbhqk", qb, kw, preferred_element_type=jnp.float32) * (dh ** -0.5)
    qpos = jnp.arange(nb)[:, None] * blk + jnp.arange(blk)[None, :]
    kpos = jnp.arange(nb)[:, None] * blk - blk + jnp.arange(3 * blk)[None, :]
    valid = ((jnp.abs(kpos[:, None, :] - qpos[:, :, None]) <= half)
             & (kpos >= 0)[:, None, :] & (kpos < length)[:, None, :])
    s = jnp.where(valid[None, :, None], s, -jnp.inf)
    m = jnp.max(s, axis=-1, keepdims=True)
    p = jnp.exp(s - m)
    den = jnp.sum(p, axis=-1)
    o = jnp.einsum("nbhqk,nbkhd->nbqhd", p, vw.astype(jnp.float32))
    o = o / jnp.transpose(den, (0, 1, 3, 2))[..., None]
    lse = jnp.transpose(m[..., 0] + jnp.log(den), (0, 1, 3, 2))
    o = o.reshape(n, lp, h, dh)[:, :length]
    lse = lse.reshape(n, lp, h)[:, :length]
    return o, lse


def dilated_attention(q, k, v):
    b, s, h, dh = q.shape
    outs, lses = [], []
    for window, dil in DILATED_PATTERNS:
        half = (window // 2) // dil
        length = s // dil

        def gather(t):
            t = t.reshape(b, length, dil, h, dh).transpose(0, 2, 1, 3, 4)
            return t.reshape(b * dil, length, h, dh)

        o, lse = band_attention(gather(q), gather(k), gather(v), half)
        o = o.reshape(b, dil, length, h, dh).transpose(0, 2, 1, 3, 4).reshape(b, s, h, dh)
        lse = lse.reshape(b, dil, length, h).transpose(0, 2, 1, 3).reshape(b, s, h)
        outs.append(o)
        lses.append(lse)
    wts = jax.nn.softmax(jnp.stack(lses, axis=0), axis=0)[..., None]
    out = jnp.sum(wts * jnp.stack(outs, axis=0), axis=0)
    return out.astype(q.dtype)


def segsum(a):
    t = a.shape[-1]
    a_rep = jnp.broadcast_to(a[..., :, None], a.shape + (t,))
    a_rep = jnp.where(jnp.tril(jnp.ones((t, t), dtype=bool), -1), a_rep, 0.0)
    cs = jnp.cumsum(a_rep, axis=-2)
    return jnp.where(jnp.tril(jnp.ones((t, t), dtype=bool)), cs, -jnp.inf)


def ssd_scan(x, dt, a_head, bm, cm):
    b, l, g, r, p = x.shape
    c = l // CHUNK
    xc = (x * dt[..., None]).reshape(b, c, CHUNK, g, r, p)
    a = (dt * a_head).reshape(b, c, CHUNK, g, r).transpose(0, 3, 4, 1, 2)
    bc = bm.reshape(b, c, CHUNK, g, -1)
    cc = cm.reshape(b, c, CHUNK, g, -1)
    a_cs = jnp.cumsum(a, axis=-1)
    lmat = jnp.exp(segsum(a))
    cb = jnp.einsum("bclgn,bcsgn->bcgls", cc, bc)
    y_diag = jnp.einsum("bcgls,bgrcls,bcsgrp->bclgrp", cb, lmat, xc)
    decay_states = jnp.exp(a_cs[..., -1:] - a_cs)
    states = jnp.einsum("bclgn,bgrcl,bclgrp->bcgrpn", bc, decay_states, xc)
    states = jnp.concatenate([jnp.zeros_like(states[:, :1]), states], axis=1)
    chunk_tot = jnp.pad(a_cs[..., -1], [(0, 0), (0, 0), (0, 0), (1, 0)])
    decay_chunk = jnp.exp(segsum(chunk_tot))
    states = jnp.einsum("bgrzc,bcgrpn->bzgrpn", decay_chunk, states)[:, :-1]
    y_off = jnp.einsum("bclgn,bcgrpn,bgrcl->bclgrp", cc, states, jnp.exp(a_cs))
    return (y_diag + y_off).reshape(b, l, g, r, p)


def ssm_mixer(z, xbc, dt_f_raw, dt_b_raw, conv_w, conv_b, a_log_f, a_log_b,
              dt_bias_f, dt_bias_b, d_skip, norm_w):
    b, l, _ = z.shape
    xbc = jax.nn.silu(dwconv_centered(xbc, conv_w, conv_b))
    gn = N_SSM_GROUPS * D_STATE
    xs = xbc[..., :D_INNER].astype(jnp.float32).reshape(b, l, N_SSM_GROUPS, HEADS_PER_GROUP, SSM_HEAD_DIM)
    bm = xbc[..., D_INNER:D_INNER + gn].astype(jnp.float32).reshape(b, l, N_SSM_GROUPS, D_STATE)
    cm = xbc[..., D_INNER + gn:].astype(jnp.float32).reshape(b, l, N_SSM_GROUPS, D_STATE)

    def direction(xs_, bm_, cm_, dt_raw, a_log, dt_bias):
        dt = jax.nn.softplus(dt_raw.astype(jnp.float32) + dt_bias.astype(jnp.float32))
        dt = dt.reshape(b, l, N_SSM_GROUPS, HEADS_PER_GROUP)
        a_head = -jnp.exp(a_log.astype(jnp.float32)).reshape(N_SSM_GROUPS, HEADS_PER_GROUP)
        return ssd_scan(xs_, dt, a_head, bm_, cm_)

    flip = lambda t: jnp.flip(t, axis=1)
    y_f = direction(xs, bm, cm, dt_f_raw, a_log_f, dt_bias_f)
    y_b = flip(direction(flip(xs), flip(bm), flip(cm), flip(dt_b_raw), a_log_b, dt_bias_b))
    d = d_skip.astype(jnp.float32).reshape(N_SSM_GROUPS, HEADS_PER_GROUP)[..., None]
    y = (y_f + y_b + d * xs).reshape(b, l, D_INNER)
    gated = (y * jax.nn.silu(z.astype(jnp.float32))).reshape(b, l, N_SSM_GROUPS, D_INNER // N_SSM_GROUPS)
    gated = gated * lax.rsqrt(jnp.mean(gated * gated, axis=-1, keepdims=True) + EPS)
    return (gated.reshape(b, l, D_INNER) * norm_w.astype(jnp.float32)).astype(z.dtype)


def conv_gated_mlp(h, w_up, conv_w, conv_b, w_down):
    u = dwconv_centered(h @ w_up, conv_w, conv_b)
    gate = u[..., :D_FF]
    up = u[..., D_FF:]
    return (jax.nn.silu(gate) * up) @ w_down


def _fwd_setup_inputs(seed: int = 0) -> dict:
    key = jax.random.key(seed)
    ks = jax.random.split(key, 20)
    f32 = jnp.float32

    def gain(k, shape):
        return 1.0 + 0.02 * jax.random.normal(k, shape, f32)

    def dt_bias(k):
        dt = jnp.exp(jax.random.uniform(k, (DEPTH, N_SSM_HEADS), f32, np.log(1e-3), np.log(1e-1)))
        return dt + jnp.log(-jnp.expm1(-dt))

    return {
        "x": jax.random.normal(ks[0], (BATCH, SEQ, D_MODEL), f32),
        "norm1_w": gain(ks[1], (DEPTH, D_MODEL)),
        "w_in": jax.random.normal(ks[2], (DEPTH, D_MODEL, IN_WIDTH), f32) * D_MODEL ** -0.5,
        "ssm_conv_w": jax.random.normal(ks[3], (DEPTH, XBC_WIDTH, SSM_CONV), f32) * SSM_CONV ** -0.5,
        "ssm_conv_b": 0.02 * jax.random.normal(ks[4], (DEPTH, XBC_WIDTH), f32),
        "a_log_f": jnp.log(jax.random.uniform(ks[5], (DEPTH, N_SSM_HEADS), f32, 1.0, 16.0)),
        "a_log_b": jnp.log(jax.random.uniform(ks[6], (DEPTH, N_SSM_HEADS), f32, 1.0, 16.0)),
        "dt_bias_f": dt_bias(ks[7]),
        "dt_bias_b": dt_bias(ks[8]),
        "d_skip": gain(ks[9], (DEPTH, N_SSM_HEADS)),
        "ssm_norm_w": gain(ks[10], (DEPTH, D_INNER)),
        "w_out": jax.random.normal(ks[11], (DEPTH, MIX_WIDTH, D_MODEL), f32) * MIX_WIDTH ** -0.5,
        "norm2_w": gain(ks[12], (DEPTH, D_MODEL)),
        "w_up": jax.random.normal(ks[13], (DEPTH, D_MODEL, 2 * D_FF), f32) * D_MODEL ** -0.5,
        "ffn_conv_w": jax.random.normal(ks[14], (DEPTH, 2 * D_FF, FFN_CONV), f32) * FFN_CONV ** -0.5,
        "ffn_conv_b": 0.02 * jax.random.normal(ks[15], (DEPTH, 2 * D_FF), f32),
        "w_down": jax.random.normal(ks[16], (DEPTH, D_FF, D_MODEL), f32) * D_FF ** -0.5,
        "final_norm_w": gain(ks[17], (D_MODEL,)),
    }


def _fwd_reference(x, norm1_w, w_in, ssm_conv_w, ssm_conv_b, a_log_f, a_log_b, dt_bias_f, dt_bias_b,
              d_skip, ssm_norm_w, w_out, norm2_w, w_up, ffn_conv_w, ffn_conv_b, w_down, final_norm_w):
    b, s, _ = x.shape
    pos = jnp.arange(s, dtype=jnp.float32)
    sizes = [ATTN_WIDTH, ATTN_WIDTH, ATTN_WIDTH, D_INNER, XBC_WIDTH, N_SSM_HEADS, N_SSM_HEADS]
    splits = [int(v) for v in np.cumsum(sizes)[:-1]]
    for layer in range(DEPTH):
        h = rmsnorm(x, norm1_w[layer])
        proj = h @ w_in[layer]
        q, k, v, z, xbc, dt_f, dt_b = jnp.split(proj, splits, axis=-1)
        q = partial_rope(q.reshape(b, s, N_ATTN_HEADS, HEAD_DIM), pos)
        k = partial_rope(k.reshape(b, s, N_ATTN_HEADS, HEAD_DIM), pos)
        v = v.reshape(b, s, N_ATTN_HEADS, HEAD_DIM)
        attn = dilated_attention(q, k, v).reshape(b, s, ATTN_WIDTH)
        ssm = ssm_mixer(z, xbc, dt_f, dt_b, ssm_conv_w[layer], ssm_conv_b[layer], a_log_f[layer],
                        a_log_b[layer], dt_bias_f[layer], dt_bias_b[layer], d_skip[layer], ssm_norm_w[layer])
        x = x + jnp.concatenate([attn, ssm], axis=-1) @ w_out[layer]
        h = rmsnorm(x, norm2_w[layer])
        x = x + conv_gated_mlp(h, w_up[layer], ffn_conv_w[layer], ffn_conv_b[layer], w_down[layer])
    return rmsnorm(x, final_norm_w)


import jax as _jax
import jax.numpy as _jnp

TWIN_FORMAT = 'train_step'
FWD_PARAMS = ['x', 'norm1_w', 'w_in', 'ssm_conv_w', 'ssm_conv_b', 'a_log_f', 'a_log_b', 'dt_bias_f', 'dt_bias_b', 'd_skip', 'ssm_norm_w', 'w_out', 'norm2_w', 'w_up', 'ffn_conv_w', 'ffn_conv_b', 'w_down', 'final_norm_w']
TWIN_WEIGHTS = ['norm1_w', 'w_in', 'ssm_conv_w', 'ssm_conv_b', 'a_log_f', 'a_log_b', 'dt_bias_f', 'dt_bias_b', 'd_skip', 'ssm_norm_w', 'w_out', 'norm2_w', 'w_up', 'ffn_conv_w', 'ffn_conv_b', 'w_down', 'final_norm_w']
TWIN_DIFF_INPUT = 'x'
TWIN_INPUTS = ['x', 'norm1_w', 'w_in', 'ssm_conv_w', 'ssm_conv_b', 'a_log_f', 'a_log_b', 'dt_bias_f', 'dt_bias_b', 'd_skip', 'ssm_norm_w', 'w_out', 'norm2_w', 'w_up', 'ffn_conv_w', 'ffn_conv_b', 'w_down', 'final_norm_w', 'loss_target', 'm_norm1_w', 'm_w_in', 'm_ssm_conv_w', 'm_ssm_conv_b', 'm_a_log_f', 'm_a_log_b', 'm_dt_bias_f', 'm_dt_bias_b', 'm_d_skip', 'm_ssm_norm_w', 'm_w_out', 'm_norm2_w', 'm_w_up', 'm_ffn_conv_w', 'm_ffn_conv_b', 'm_w_down', 'm_final_norm_w', 'v_norm1_w', 'v_w_in', 'v_ssm_conv_w', 'v_ssm_conv_b', 'v_a_log_f', 'v_a_log_b', 'v_dt_bias_f', 'v_dt_bias_b', 'v_d_skip', 'v_ssm_norm_w', 'v_w_out', 'v_norm2_w', 'v_w_up', 'v_ffn_conv_w', 'v_ffn_conv_b', 'v_w_down', 'v_final_norm_w']
TWIN_OUTPUTS = ['loss', 'grad_x', 'grad_norm1_w', 'grad_w_in', 'grad_ssm_conv_w', 'grad_ssm_conv_b', 'grad_a_log_f', 'grad_a_log_b', 'grad_dt_bias_f', 'grad_dt_bias_b', 'grad_d_skip', 'grad_ssm_norm_w', 'grad_w_out', 'grad_norm2_w', 'grad_w_up', 'grad_ffn_conv_w', 'grad_ffn_conv_b', 'grad_w_down', 'grad_final_norm_w', 'delta_norm1_w', 'delta_w_in', 'delta_ssm_conv_w', 'delta_ssm_conv_b', 'delta_a_log_f', 'delta_a_log_b', 'delta_dt_bias_f', 'delta_dt_bias_b', 'delta_d_skip', 'delta_ssm_norm_w', 'delta_w_out', 'delta_norm2_w', 'delta_w_up', 'delta_ffn_conv_w', 'delta_ffn_conv_b', 'delta_w_down', 'delta_final_norm_w', 'new_m_norm1_w', 'new_m_w_in', 'new_m_ssm_conv_w', 'new_m_ssm_conv_b', 'new_m_a_log_f', 'new_m_a_log_b', 'new_m_dt_bias_f', 'new_m_dt_bias_b', 'new_m_d_skip', 'new_m_ssm_norm_w', 'new_m_w_out', 'new_m_norm2_w', 'new_m_w_up', 'new_m_ffn_conv_w', 'new_m_ffn_conv_b', 'new_m_w_down', 'new_m_final_norm_w', 'new_v_norm1_w', 'new_v_w_in', 'new_v_ssm_conv_w', 'new_v_ssm_conv_b', 'new_v_a_log_f', 'new_v_a_log_b', 'new_v_dt_bias_f', 'new_v_dt_bias_b', 'new_v_d_skip', 'new_v_ssm_norm_w', 'new_v_w_out', 'new_v_norm2_w', 'new_v_w_up', 'new_v_ffn_conv_w', 'new_v_ffn_conv_b', 'new_v_w_down', 'new_v_final_norm_w']
TWIN_LEAF_KINDS = {'loss': 'loss', 'grad_x': 'grad_x', 'grad_norm1_w': 'grad_w', 'grad_w_in': 'grad_w', 'grad_ssm_conv_w': 'grad_w', 'grad_ssm_conv_b': 'grad_w', 'grad_a_log_f': 'grad_w', 'grad_a_log_b': 'grad_w', 'grad_dt_bias_f': 'grad_w', 'grad_dt_bias_b': 'grad_w', 'grad_d_skip': 'grad_w', 'grad_ssm_norm_w': 'grad_w', 'grad_w_out': 'grad_w', 'grad_norm2_w': 'grad_w', 'grad_w_up': 'grad_w', 'grad_ffn_conv_w': 'grad_w', 'grad_ffn_conv_b': 'grad_w', 'grad_w_down': 'grad_w', 'grad_final_norm_w': 'grad_w', 'delta_norm1_w': 'delta_w', 'delta_w_in': 'delta_w', 'delta_ssm_conv_w': 'delta_w', 'delta_ssm_conv_b': 'delta_w', 'delta_a_log_f': 'delta_w', 'delta_a_log_b': 'delta_w', 'delta_dt_bias_f': 'delta_w', 'delta_dt_bias_b': 'delta_w', 'delta_d_skip': 'delta_w', 'delta_ssm_norm_w': 'delta_w', 'delta_w_out': 'delta_w', 'delta_norm2_w': 'delta_w', 'delta_w_up': 'delta_w', 'delta_ffn_conv_w': 'delta_w', 'delta_ffn_conv_b': 'delta_w', 'delta_w_down': 'delta_w', 'delta_final_norm_w': 'delta_w', 'new_m_norm1_w': 'new_m', 'new_m_w_in': 'new_m', 'new_m_ssm_conv_w': 'new_m', 'new_m_ssm_conv_b': 'new_m', 'new_m_a_log_f': 'new_m', 'new_m_a_log_b': 'new_m', 'new_m_dt_bias_f': 'new_m', 'new_m_dt_bias_b': 'new_m', 'new_m_d_skip': 'new_m', 'new_m_ssm_norm_w': 'new_m', 'new_m_w_out': 'new_m', 'new_m_norm2_w': 'new_m', 'new_m_w_up': 'new_m', 'new_m_ffn_conv_w': 'new_m', 'new_m_ffn_conv_b': 'new_m', 'new_m_w_down': 'new_m', 'new_m_final_norm_w': 'new_m', 'new_v_norm1_w': 'new_v', 'new_v_w_in': 'new_v', 'new_v_ssm_conv_w': 'new_v', 'new_v_ssm_conv_b': 'new_v', 'new_v_a_log_f': 'new_v', 'new_v_a_log_b': 'new_v', 'new_v_dt_bias_f': 'new_v', 'new_v_dt_bias_b': 'new_v', 'new_v_d_skip': 'new_v', 'new_v_ssm_norm_w': 'new_v', 'new_v_w_out': 'new_v', 'new_v_norm2_w': 'new_v', 'new_v_w_up': 'new_v', 'new_v_ffn_conv_w': 'new_v', 'new_v_ffn_conv_b': 'new_v', 'new_v_w_down': 'new_v', 'new_v_final_norm_w': 'new_v'}


def _forward(args):
    return _fwd_reference(*[args[k] for k in FWD_PARAMS])


def _output_shape():
    def fwd():
        inp = _fwd_setup_inputs(0)
        return _fwd_reference(*[inp[k] for k in FWD_PARAMS])
    out = _jax.eval_shape(fwd)
    return out.shape, out.dtype

N_MICROBATCH = 1
ADAM_LR = 0.001
ADAM_B1 = 0.9
ADAM_B2 = 0.999
ADAM_EPS = 1e-08
ADAM_WD = 0.01
ADAM_STEP = 10
PER_EXAMPLE_BATCH_AXIS = {'x': 0, 'loss_target': 0}
SHARED_INPUTS = []
_WEIGHT_DTYPES = {'norm1_w': _jnp.float32, 'w_in': _jnp.float32, 'ssm_conv_w': _jnp.float32, 'ssm_conv_b': _jnp.float32, 'a_log_f': _jnp.float32, 'a_log_b': _jnp.float32, 'dt_bias_f': _jnp.float32, 'dt_bias_b': _jnp.float32, 'd_skip': _jnp.float32, 'ssm_norm_w': _jnp.float32, 'w_out': _jnp.float32, 'norm2_w': _jnp.float32, 'w_up': _jnp.float32, 'ffn_conv_w': _jnp.float32, 'ffn_conv_b': _jnp.float32, 'w_down': _jnp.float32, 'final_norm_w': _jnp.float32}
MOMENT_SCALE = {'norm1_w': 1.736827e-01, 'w_in': 6.850165e-02, 'ssm_conv_w': 8.632274e-02, 'ssm_conv_b': 1.397924e-01, 'a_log_f': 2.469335e-01, 'a_log_b': 1.778908e-01, 'dt_bias_f': 1.912720e-01, 'dt_bias_b': 2.159082e-01, 'd_skip': 4.505979e-01, 'ssm_norm_w': 1.179545e-01, 'w_out': 1.138475e-01, 'norm2_w': 1.203999e-01, 'w_up': 4.805084e-02, 'ffn_conv_w': 4.968851e-02, 'ffn_conv_b': 4.810365e-02, 'w_down': 7.875201e-02, 'final_norm_w': 3.203970e+01}


def _to_microbatches(a, axis):
    t = _jnp.moveaxis(a, axis, 0)
    t = t.reshape((N_MICROBATCH, t.shape[0] // N_MICROBATCH) + t.shape[1:])
    return _jnp.moveaxis(t, 1, axis + 1)


def setup_inputs(seed: int = 0) -> dict:
    inp = _fwd_setup_inputs(seed)
    key = _jax.random.fold_in(_jax.random.key(seed), 7919)
    shape, _ = _output_shape()
    out = dict(inp)
    out["loss_target"] = _jax.random.normal(_jax.random.fold_in(key, 0), shape, _jnp.float32)
    for i, name in enumerate(TWIN_WEIGHTS):
        w = inp[name].astype(_jnp.float32)
        if MOMENT_SCALE is None:
            s = _jnp.sqrt(_jnp.mean(_jnp.square(w)) + 1e-30)
        else:
            s = MOMENT_SCALE[name]
        km, kv = _jax.random.split(_jax.random.fold_in(key, i + 1))
        out[name] = w
        out["m_" + name] = s * _jax.random.normal(km, w.shape, _jnp.float32)
        out["v_" + name] = (s * s) * _jax.random.uniform(kv, w.shape, _jnp.float32, 0.5, 1.5)
    if N_MICROBATCH > 1:
        for name, axis in PER_EXAMPLE_BATCH_AXIS.items():
            out[name] = _to_microbatches(out[name], axis)
    return {'x': out['x'], 'norm1_w': out['norm1_w'], 'w_in': out['w_in'], 'ssm_conv_w': out['ssm_conv_w'], 'ssm_conv_b': out['ssm_conv_b'], 'a_log_f': out['a_log_f'], 'a_log_b': out['a_log_b'], 'dt_bias_f': out['dt_bias_f'], 'dt_bias_b': out['dt_bias_b'], 'd_skip': out['d_skip'], 'ssm_norm_w': out['ssm_norm_w'], 'w_out': out['w_out'], 'norm2_w': out['norm2_w'], 'w_up': out['w_up'], 'ffn_conv_w': out['ffn_conv_w'], 'ffn_conv_b': out['ffn_conv_b'], 'w_down': out['w_down'], 'final_norm_w': out['final_norm_w'], 'loss_target': out['loss_target'], 'm_norm1_w': out['m_norm1_w'], 'm_w_in': out['m_w_in'], 'm_ssm_conv_w': out['m_ssm_conv_w'], 'm_ssm_conv_b': out['m_ssm_conv_b'], 'm_a_log_f': out['m_a_log_f'], 'm_a_log_b': out['m_a_log_b'], 'm_dt_bias_f': out['m_dt_bias_f'], 'm_dt_bias_b': out['m_dt_bias_b'], 'm_d_skip': out['m_d_skip'], 'm_ssm_norm_w': out['m_ssm_norm_w'], 'm_w_out': out['m_w_out'], 'm_norm2_w': out['m_norm2_w'], 'm_w_up': out['m_w_up'], 'm_ffn_conv_w': out['m_ffn_conv_w'], 'm_ffn_conv_b': out['m_ffn_conv_b'], 'm_w_down': out['m_w_down'], 'm_final_norm_w': out['m_final_norm_w'], 'v_norm1_w': out['v_norm1_w'], 'v_w_in': out['v_w_in'], 'v_ssm_conv_w': out['v_ssm_conv_w'], 'v_ssm_conv_b': out['v_ssm_conv_b'], 'v_a_log_f': out['v_a_log_f'], 'v_a_log_b': out['v_a_log_b'], 'v_dt_bias_f': out['v_dt_bias_f'], 'v_dt_bias_b': out['v_dt_bias_b'], 'v_d_skip': out['v_d_skip'], 'v_ssm_norm_w': out['v_ssm_norm_w'], 'v_w_out': out['v_w_out'], 'v_norm2_w': out['v_norm2_w'], 'v_w_up': out['v_w_up'], 'v_ffn_conv_w': out['v_ffn_conv_w'], 'v_ffn_conv_b': out['v_ffn_conv_b'], 'v_w_down': out['v_w_down'], 'v_final_norm_w': out['v_final_norm_w']}


def _loss(weights, diff, rest, loss_target):
    with _jax.named_scope("forward"):
        args = {**rest, TWIN_DIFF_INPUT: diff, **{k: w.astype(_WEIGHT_DTYPES[k]) for k, w in weights.items()}}
        y = _forward(args)
    with _jax.named_scope("loss_head"):
        err = _jnp.square(y.astype(_jnp.float32) - loss_target)
        return 0.5 * _jnp.sum(_jnp.mean(err, axis=-1)) if err.ndim else 0.5 * err


def _adamw(w, g, m, v):
    m = ADAM_B1 * m + (1.0 - ADAM_B1) * g
    v = ADAM_B2 * v + (1.0 - ADAM_B2) * _jnp.square(g)
    m_hat = m / (1.0 - ADAM_B1 ** ADAM_STEP)
    v_hat = v / (1.0 - ADAM_B2 ** ADAM_STEP)
    delta = -ADAM_LR * (m_hat / (_jnp.sqrt(v_hat) + ADAM_EPS) + ADAM_WD * w)
    return delta, m, v


def reference(x, norm1_w, w_in, ssm_conv_w, ssm_conv_b, a_log_f, a_log_b, dt_bias_f, dt_bias_b, d_skip, ssm_norm_w, w_out, norm2_w, w_up, ffn_conv_w, ffn_conv_b, w_down, final_norm_w, loss_target, m_norm1_w, m_w_in, m_ssm_conv_w, m_ssm_conv_b, m_a_log_f, m_a_log_b, m_dt_bias_f, m_dt_bias_b, m_d_skip, m_ssm_norm_w, m_w_out, m_norm2_w, m_w_up, m_ffn_conv_w, m_ffn_conv_b, m_w_down, m_final_norm_w, v_norm1_w, v_w_in, v_ssm_conv_w, v_ssm_conv_b, v_a_log_f, v_a_log_b, v_dt_bias_f, v_dt_bias_b, v_d_skip, v_ssm_norm_w, v_w_out, v_norm2_w, v_w_up, v_ffn_conv_w, v_ffn_conv_b, v_w_down, v_final_norm_w):
    given = dict(x=x, norm1_w=norm1_w, w_in=w_in, ssm_conv_w=ssm_conv_w, ssm_conv_b=ssm_conv_b, a_log_f=a_log_f, a_log_b=a_log_b, dt_bias_f=dt_bias_f, dt_bias_b=dt_bias_b, d_skip=d_skip, ssm_norm_w=ssm_norm_w, w_out=w_out, norm2_w=norm2_w, w_up=w_up, ffn_conv_w=ffn_conv_w, ffn_conv_b=ffn_conv_b, w_down=w_down, final_norm_w=final_norm_w, loss_target=loss_target, m_norm1_w=m_norm1_w, m_w_in=m_w_in, m_ssm_conv_w=m_ssm_conv_w, m_ssm_conv_b=m_ssm_conv_b, m_a_log_f=m_a_log_f, m_a_log_b=m_a_log_b, m_dt_bias_f=m_dt_bias_f, m_dt_bias_b=m_dt_bias_b, m_d_skip=m_d_skip, m_ssm_norm_w=m_ssm_norm_w, m_w_out=m_w_out, m_norm2_w=m_norm2_w, m_w_up=m_w_up, m_ffn_conv_w=m_ffn_conv_w, m_ffn_conv_b=m_ffn_conv_b, m_w_down=m_w_down, m_final_norm_w=m_final_norm_w, v_norm1_w=v_norm1_w, v_w_in=v_w_in, v_ssm_conv_w=v_ssm_conv_w, v_ssm_conv_b=v_ssm_conv_b, v_a_log_f=v_a_log_f, v_a_log_b=v_a_log_b, v_dt_bias_f=v_dt_bias_f, v_dt_bias_b=v_dt_bias_b, v_d_skip=v_d_skip, v_ssm_norm_w=v_ssm_norm_w, v_w_out=v_w_out, v_norm2_w=v_norm2_w, v_w_up=v_w_up, v_ffn_conv_w=v_ffn_conv_w, v_ffn_conv_b=v_ffn_conv_b, v_w_down=v_w_down, v_final_norm_w=v_final_norm_w)
    weights = {n: given[n] for n in TWIN_WEIGHTS}
    shared = {n: given[n] for n in SHARED_INPUTS}
    per_example = {n: given[n] for n in ['x']}
    grad_fn = _jax.value_and_grad(_loss, argnums=(0, 1))

    def one_microbatch(ex, loss_target):
        ex = dict(ex)
        diff = ex.pop(TWIN_DIFF_INPUT)
        return grad_fn(weights, diff, {**shared, **ex}, loss_target)

    if N_MICROBATCH == 1:
        loss, (grad_w, grad_x) = one_microbatch(per_example, given["loss_target"])
    else:
        def body(carry, xs):
            loss_sum, grad_sum = carry
            l_k, (gw_k, gx_k) = one_microbatch(xs[0], xs[1])
            with _jax.named_scope("update"):
                return (loss_sum + l_k, _jax.tree.map(_jnp.add, grad_sum, gw_k)), gx_k

        init = (_jnp.zeros((), _jnp.float32), _jax.tree.map(_jnp.zeros_like, weights))
        (loss, grad_w), grad_x = _jax.lax.scan(body, init, (per_example, given["loss_target"]))
    with _jax.named_scope("update"):
        delta_w, new_m, new_v = {}, {}, {}
        for n in TWIN_WEIGHTS:
            delta_w[n], new_m[n], new_v[n] = _adamw(weights[n], grad_w[n], given["m_" + n], given["v_" + n])
    return (loss, grad_x, *[grad_w[n] for n in TWIN_WEIGHTS], *[delta_w[n] for n in TWIN_WEIGHTS],
            *[new_m[n] for n in TWIN_WEIGHTS], *[new_v[n] for n in TWIN_WEIGHTS])
```

```python
import math

import jax
import jax.numpy as jnp
from jax import lax
from jax.experimental import pallas as pl
from jax.experimental.pallas import tpu as pltpu

F32 = jnp.float32
BF16 = jnp.bfloat16
SDS = jax.ShapeDtypeStruct

N_DEV = 8
D_MODEL = 1024
N_HEADS = 16
HEAD_DIM = 64
ROPE_DIM = 16
ROPE_THETA = 500000.0
DILATIONS = (1, 4, 16)
BAND_HALF = 64
D_INNER = 1024
N_GROUPS = 4
D_STATE = 128
CHUNK = 128
XBC = D_INNER + 2 * N_GROUPS * D_STATE
D_FF = 2816
MAIN_W = 3 * D_MODEL + D_INNER + XBC
EPS = 1e-6
LR, B1, B2, AEPS, WD, STEP = 0.001, 0.9, 0.999, 1e-08, 0.01, 10
NEG = -1e30
VMEM_LIMIT = 56 * 1024 * 1024
MESH = pl.DeviceIdType.MESH
HIGH = lax.Precision.HIGHEST
NT = (((1,), (1,)), ((), ()))
TN = (((0,), (0,)), ((), ()))


def _cp(*sem):
    return pltpu.CompilerParams(dimension_semantics=sem, vmem_limit_bytes=VMEM_LIMIT)


def _pick(n, cands):
    for c in cands:
        if n % c == 0:
            return c
    raise ValueError(f"no tile for {n}")


def _sigmoid(x):
    return 1.0 / (1.0 + jnp.exp(-x))


def _softplus(x):
    return jnp.maximum(x, 0.0) + jnp.log1p(jnp.exp(-jnp.abs(x)))


def _matmul(a, b, *, name, trans_b=False, out_dtype=BF16, residual=None):
    m, k = a.shape
    n = b.shape[0] if trans_b else b.shape[1]
    tm = _pick(m, (512, 256, 128, 32))
    tn = _pick(n, (512, 256, 128))
    tk = _pick(k, (2048, 1408, 1024, 512, 256, 128))
    nk = k // tk
    dn = NT if trans_b else (((1,), (0,)), ((), ()))

    def body(*refs):
        a_ref, b_ref = refs[0], refs[1]
        o_ref, acc = refs[-2], refs[-1]
        kk = pl.program_id(2)

        @pl.when(kk == 0)
        def _():
            acc[...] = jnp.zeros_like(acc)

        acc[...] += lax.dot_general(a_ref[...], b_ref[...], dn, preferred_element_type=F32)

        @pl.when(kk == nk - 1)
        def _():
            r = acc[...]
            if residual is not None:
                r = r + refs[2][...].astype(F32)
            o_ref[...] = r.astype(o_ref.dtype)

    in_specs = [pl.BlockSpec((tm, tk), lambda i, j, kk: (i, kk)),
                pl.BlockSpec((tn, tk), lambda i, j, kk: (j, kk)) if trans_b else pl.BlockSpec((tk, tn), lambda i, j, kk: (kk, j))]
    args = [a, b]
    if residual is not None:
        in_specs.append(pl.BlockSpec((tm, tn), lambda i, j, kk: (i, j)))
        args.append(residual)
    return pl.pallas_call(
        body, grid=(m // tm, n // tn, nk), in_specs=in_specs,
        out_specs=pl.BlockSpec((tm, tn), lambda i, j, kk: (i, j)),
        out_shape=SDS((m, n), out_dtype), scratch_shapes=[pltpu.VMEM((tm, tn), F32)],
        name=name, compiler_params=_cp("parallel", "parallel", "arbitrary"))(*args)


def _rmsnorm_fwd(x, w, name):
    s, d = x.shape
    tm = _pick(s, (512, 128))

    def body(x_ref, w_ref, o_ref):
        xf = x_ref[...]
        r = lax.rsqrt(jnp.mean(xf * xf, axis=-1, keepdims=True) + EPS)
        o_ref[...] = (xf * r * w_ref[...]).astype(o_ref.dtype)

    return pl.pallas_call(
        body, grid=(s // tm,), in_specs=[pl.BlockSpec((tm, d), lambda i: (i, 0)), pl.BlockSpec((1, d), lambda i: (0, 0))],
        out_specs=pl.BlockSpec((tm, d), lambda i: (i, 0)), out_shape=SDS((s, d), BF16),
        name=name, compiler_params=_cp("parallel"))(x, w)


def _rmsnorm_bwd(x, w, dh, dres, name):
    s, d = x.shape
    tm = _pick(s, (512, 128))

    def body(x_ref, w_ref, dh_ref, dres_ref, dx_ref, dw_ref):
        xf = x_ref[...]
        r = lax.rsqrt(jnp.mean(xf * xf, axis=-1, keepdims=True) + EPS)
        xhat = xf * r
        dhf = dh_ref[...].astype(F32)
        g = dhf * w_ref[...]
        dx_ref[...] = dres_ref[...] + r * (g - xhat * jnp.mean(g * xhat, axis=-1, keepdims=True))

        @pl.when(pl.program_id(0) == 0)
        def _():
            dw_ref[...] = jnp.zeros_like(dw_ref)

        dw_ref[...] += jnp.sum(dhf * xhat, axis=0, keepdims=True)

    row = pl.BlockSpec((tm, d), lambda i: (i, 0))
    vec = pl.BlockSpec((1, d), lambda i: (0, 0))
    return pl.pallas_call(
        body, grid=(s // tm,), in_specs=[row, vec, row, row], out_specs=[row, vec],
        out_shape=[SDS((s, d), F32), SDS((1, d), F32)], name=name, compiler_params=_cp("arbitrary"))(x, w, dh, dres)


def _final_norm_loss(x, w, target, name):
    s, d = x.shape
    tm = _pick(s, (512, 128))

    def body(x_ref, w_ref, t_ref, dx_ref, dw_ref, loss_ref):
        xf = x_ref[...]
        r = lax.rsqrt(jnp.mean(xf * xf, axis=-1, keepdims=True) + EPS)
        xhat = xf * r
        wv = w_ref[...]
        e = xhat * wv - t_ref[...]
        dy = e * (1.0 / d)
        g = dy * wv
        dx_ref[...] = r * (g - xhat * jnp.mean(g * xhat, axis=-1, keepdims=True))

        @pl.when(pl.program_id(0) == 0)
        def _():
            dw_ref[...] = jnp.zeros_like(dw_ref)
            loss_ref[...] = jnp.zeros_like(loss_ref)

        dw_ref[...] += jnp.sum(dy * xhat, axis=0, keepdims=True)
        loss_ref[...] += jnp.sum(jnp.sum(e * e, axis=1, keepdims=True), axis=0, keepdims=True) * (0.5 / d)

    row = pl.BlockSpec((tm, d), lambda i: (i, 0))
    vec = pl.BlockSpec((1, d), lambda i: (0, 0))
    return pl.pallas_call(
        body, grid=(s // tm,), in_specs=[row, vec, row], out_specs=[row, vec, pl.BlockSpec((1, 128), lambda i: (0, 0))],
        out_shape=[SDS((s, d), F32), SDS((1, d), F32), SDS((1, 128), F32)], name=name, compiler_params=_cp("arbitrary"))(x, w, target)


def _rope_tables(s):
    half = ROPE_DIM // 2
    inv_freq = jnp.power(ROPE_THETA, -jnp.arange(half, dtype=F32) * 2.0 / ROPE_DIM)
    ang = jnp.arange(s, dtype=F32)[:, None] * inv_freq[None, :]
    cos, sin = jnp.cos(ang), jnp.sin(ang)
    z = jnp.zeros((s, HEAD_DIM - ROPE_DIM), F32)
    zh = jnp.zeros((s, half), F32)
    c = jnp.concatenate([cos, cos, z + 1.0], axis=1)
    sa = jnp.concatenate([zh, sin, z], axis=1)
    sb = jnp.concatenate([-sin, zh, z], axis=1)
    two = lambda t: jnp.concatenate([t, t], axis=1)
    c, sa, sb = two(c), two(sa), two(sb)
    return (c, sa, sb), (c, jnp.roll(sb, half, axis=1), jnp.roll(sa, -half, axis=1))


def _rope_fwd(proj, tabs, name):
    s = proj.shape[0]
    tm = _pick(s, (512, 128))
    half = ROPE_DIM // 2

    def body(x_ref, c_ref, sa_ref, sb_ref, o_ref):
        x = x_ref[...].astype(F32)
        o_ref[...] = (x * c_ref[...] + pltpu.roll(x, half, 1) * sa_ref[...] + pltpu.roll(x, 128 - half, 1) * sb_ref[...]).astype(o_ref.dtype)

    blk = pl.BlockSpec((tm, 128), lambda i, j: (i, j))
    tab = pl.BlockSpec((tm, 128), lambda i, j: (i, 0))
    return pl.pallas_call(
        body, grid=(s // tm, 2 * D_MODEL // 128), in_specs=[blk, tab, tab, tab], out_specs=blk,
        out_shape=SDS((s, 2 * D_MODEL), BF16), name=name, compiler_params=_cp("parallel", "parallel"))(proj, *tabs)


def _sum3_rope(d1, d2, d3, tabs, name):
    s, w = d1.shape
    tm = _pick(s, (512, 128))
    half = ROPE_DIM // 2

    def body(*refs):
        x = refs[0][...].astype(F32) + refs[1][...].astype(F32) + refs[2][...].astype(F32)
        if tabs is not None:
            x = x * refs[3][...] + pltpu.roll(x, half, 1) * refs[4][...] + pltpu.roll(x, 128 - half, 1) * refs[5][...]
        refs[-1][...] = x.astype(refs[-1].dtype)

    blk = pl.BlockSpec((tm, 128), lambda i, j: (i, j))
    tab = pl.BlockSpec((tm, 128), lambda i, j: (i, 0))
    extra = [] if tabs is None else list(tabs)
    return pl.pallas_call(
        body, grid=(s // tm, w // 128), in_specs=[blk, blk, blk] + [tab] * len(extra), out_specs=blk,
        out_shape=SDS((s, w), BF16), name=name, compiler_params=_cp("parallel", "parallel"))(d1, d2, d3, *extra)


def _band_valid(t, nq, nk, qofs, kofs, seq_len):
    qpos = t * 128 + qofs + lax.broadcasted_iota(jnp.int32, (nq, nk), 0)
    kpos = t * 128 + kofs + lax.broadcasted_iota(jnp.int32, (nq, nk), 1)
    sh = int(math.log2(seq_len))
    same = lax.shift_right_arithmetic(qpos, sh) == lax.shift_right_arithmetic(kpos, sh)
    return same & (jnp.abs(kpos - qpos) <= BAND_HALF)


def _win_specs(width, col, nt):
    return [pl.BlockSpec((128, width), lambda t: (jnp.maximum(t - 1, 0), col)),
            pl.BlockSpec((128, width), lambda t: (t, col)),
            pl.BlockSpec((128, width), lambda t: (jnp.minimum(t + 1, nt - 1), col))]


def _attn_fwd(qk, v_src, v_col, seq_len, name):
    s = qk.shape[0]
    nt = s // 128
    dm = D_MODEL

    def body(q_ref, k0, k1, k2, v0, v1, v2, o_ref, lse_ref):
        t = pl.program_id(0)
        valid = _band_valid(t, 128, 384, 0, -128, seq_len)
        q = q_ref[...]
        kc = jnp.concatenate([k0[...], k1[...], k2[...]], axis=0)
        vc = jnp.concatenate([v0[...], v1[...], v2[...]], axis=0)
        outs, lses = [], []
        for h in range(N_HEADS):
            sl = slice(h * HEAD_DIM, (h + 1) * HEAD_DIM)
            sc = lax.dot_general(q[:, sl], kc[:, sl], NT, preferred_element_type=F32) * (HEAD_DIM ** -0.5)
            sc = jnp.where(valid, sc, NEG)
            m = jnp.max(sc, axis=1, keepdims=True)
            e = jnp.exp(sc - m)
            den = jnp.sum(e, axis=1, keepdims=True)
            o = jnp.dot(e.astype(BF16), vc[:, sl], preferred_element_type=F32) / den
            outs.append(o)
            lses.append(m + jnp.log(den))
        o_ref[...] = jnp.concatenate(outs, axis=1).astype(o_ref.dtype)
        lse_ref[...] = jnp.concatenate(lses, axis=1)

    in_specs = [pl.BlockSpec((128, dm), lambda t: (t, 0))] + _win_specs(dm, 1, nt) + _win_specs(dm, v_col, nt)
    return pl.pallas_call(
        body, grid=(nt,), in_specs=in_specs,
        out_specs=[pl.BlockSpec((128, dm), lambda t: (t, 0)), pl.BlockSpec((128, N_HEADS), lambda t: (t, 0))],
        out_shape=[SDS((s, dm), BF16), SDS((s, N_HEADS), F32)], name=name, compiler_params=_cp("parallel"))(
            qk, qk, qk, qk, v_src, v_src, v_src)


def _attn_combine(os_, lses, expand, name):
    s, dm = os_[0].shape
    tm = _pick(s, (256, 128))

    def body(o1, o2, o3, l1, l2, l3, e_ref, out_ref, lt_ref):
        ls = [l1[...], l2[...], l3[...]]
        m = jnp.maximum(jnp.maximum(ls[0], ls[1]), ls[2])
        es = [jnp.exp(l - m) for l in ls]
        tot = es[0] + es[1] + es[2]
        lt_ref[...] = m + jnp.log(tot)
        acc = jnp.zeros((tm, dm), F32)
        for e, o in zip(es, (o1, o2, o3)):
            acc = acc + jnp.dot(e / tot, e_ref[...], precision=HIGH, preferred_element_type=F32) * o[...].astype(F32)
        out_ref[...] = acc.astype(out_ref.dtype)

    row = pl.BlockSpec((tm, dm), lambda i: (i, 0))
    st = pl.BlockSpec((tm, N_HEADS), lambda i: (i, 0))
    return pl.pallas_call(
        body, grid=(s // tm,), in_specs=[row, row, row, st, st, st, pl.BlockSpec((N_HEADS, dm), lambda i: (0, 0))],
        out_specs=[row, st], out_shape=[SDS((s, dm), BF16), SDS((s, N_HEADS), F32)],
        name=name, compiler_params=_cp("parallel"))(*os_, *lses, expand)


def _attn_delta(dmix, attn, expand_t, name):
    s, dm = attn.shape
    tm = _pick(s, (256, 128))

    def body(d_ref, a_ref, e_ref, o_ref):
        prod = d_ref[...].astype(F32) * a_ref[...].astype(F32)
        o_ref[...] = jnp.dot(prod, e_ref[...], precision=HIGH, preferred_element_type=F32)

    row = pl.BlockSpec((tm, dm), lambda i: (i, 0))
    return pl.pallas_call(
        body, grid=(s // tm,), in_specs=[row, row, pl.BlockSpec((dm, N_HEADS), lambda i: (0, 0))],
        out_specs=pl.BlockSpec((tm, N_HEADS), lambda i: (i, 0)), out_shape=SDS((s, N_HEADS), F32),
        name=name, compiler_params=_cp("parallel"))(dmix, attn, expand_t)


def _attn_bwd_dq(qk, v_src, v_col, do_src, lse, delta, seq_len, name):
    s = qk.shape[0]
    nt = s // 128
    dm = D_MODEL

    def body(q_ref, k0, k1, k2, v0, v1, v2, do_ref, lse_ref, dl_ref, dq_ref):
        t = pl.program_id(0)
        valid = _band_valid(t, 128, 384, 0, -128, seq_len)
        q = q_ref[...]
        do = do_ref[...]
        kc = jnp.concatenate([k0[...], k1[...], k2[...]], axis=0)
        vc = jnp.concatenate([v0[...], v1[...], v2[...]], axis=0)
        lse_v, dl_v = lse_ref[...], dl_ref[...]
        outs = []
        for h in range(N_HEADS):
            sl = slice(h * HEAD_DIM, (h + 1) * HEAD_DIM)
            sc = lax.dot_general(q[:, sl], kc[:, sl], NT, preferred_element_type=F32) * (HEAD_DIM ** -0.5)
            p = jnp.exp(jnp.where(valid, sc - lse_v[:, h:h + 1], NEG))
            dp = lax.dot_general(do[:, sl], vc[:, sl], NT, preferred_element_type=F32)
            ds = p * (dp - dl_v[:, h:h + 1])
            outs.append(jnp.dot(ds.astype(BF16), kc[:, sl], preferred_element_type=F32) * (HEAD_DIM ** -0.5))
        dq_ref[...] = jnp.concatenate(outs, axis=1)

    row = pl.BlockSpec((128, dm), lambda t: (t, 0))
    st = pl.BlockSpec((128, N_HEADS), lambda t: (t, 0))
    in_specs = [row] + _win_specs(dm, 1, nt) + _win_specs(dm, v_col, nt) + [row, st, st]
    return pl.pallas_call(
        body, grid=(nt,), in_specs=in_specs, out_specs=row, out_shape=SDS((s, dm), F32),
        name=name, compiler_params=_cp("parallel"))(qk, qk, qk, qk, v_src, v_src, v_src, do_src, lse, delta)


def _attn_bwd_dkv(qk, v_src, v_col, do_src, lse, delta, seq_len, name):
    s = qk.shape[0]
    nt = s // 128
    dm = D_MODEL

    def body(k_ref, v_ref, q0, q1, q2, d0, d1, d2, l0, l1, l2, e0, e1, e2, dk_ref, dv_ref):
        t = pl.program_id(0)
        valid = _band_valid(t, 384, 128, -128, 0, seq_len)
        k = k_ref[...]
        v = v_ref[...]
        qc = jnp.concatenate([q0[...], q1[...], q2[...]], axis=0)
        dc = jnp.concatenate([d0[...], d1[...], d2[...]], axis=0)
        lse_v = jnp.concatenate([l0[...], l1[...], l2[...]], axis=0)
        dl_v = jnp.concatenate([e0[...], e1[...], e2[...]], axis=0)
        dks, dvs = [], []
        for h in range(N_HEADS):
            sl = slice(h * HEAD_DIM, (h + 1) * HEAD_DIM)
            sc = lax.dot_general(qc[:, sl], k[:, sl], NT, preferred_element_type=F32) * (HEAD_DIM ** -0.5)
            p = jnp.exp(jnp.where(valid, sc - lse_v[:, h:h + 1], NEG))
            dvs.append(lax.dot_general(p.astype(BF16), dc[:, sl], TN, preferred_element_type=F32))
            dp = lax.dot_general(dc[:, sl], v[:, sl], NT, preferred_element_type=F32)
            ds = p * (dp - dl_v[:, h:h + 1])
            dks.append(lax.dot_general(ds.astype(BF16), qc[:, sl], TN, preferred_element_type=F32) * (HEAD_DIM ** -0.5))
        dk_ref[...] = jnp.concatenate(dks, axis=1)
        dv_ref[...] = jnp.concatenate(dvs, axis=1)

    row = pl.BlockSpec((128, dm), lambda t: (t, 0))
    in_specs = ([pl.BlockSpec((128, dm), lambda t: (t, 1)), pl.BlockSpec((128, dm), lambda t: (t, v_col))]
                + _win_specs(dm, 0, nt) + _win_specs(dm, 0, nt) + _win_specs(N_HEADS, 0, nt) + _win_specs(N_HEADS, 0, nt))
    return pl.pallas_call(
        body, grid=(nt,), in_specs=in_specs, out_specs=[row, row], out_shape=[SDS((s, dm), F32), SDS((s, dm), F32)],
        name=name, compiler_params=_cp("parallel"))(qk, v_src, qk, qk, qk, do_src, do_src, do_src, lse, lse, lse, delta, delta, delta)


def _halo_specs(tm, tc, col0, nrow_blocks):
    r = tm // 16
    return [pl.BlockSpec((16, tc), lambda i, j: (jnp.maximum(i * r - 1, 0), col0 + j)),
            pl.BlockSpec((16, tc), lambda i, j: (jnp.minimum((i + 1) * r, nrow_blocks * r - 1), col0 + j))]


def _shifted(x_ref, hp_ref, hn_ref, i, last):
    x = x_ref[...].astype(F32)
    tm = x.shape[0]
    rows = lax.broadcasted_iota(jnp.int32, x.shape, 0)
    prev_row = jnp.where(i > 0, hp_ref[15:16, :].astype(F32), 0.0)
    next_row = jnp.where(i < last, hn_ref[0:1, :].astype(F32), 0.0)
    xp = jnp.where(rows == 0, prev_row, pltpu.roll(x, 1, 0))
    xn = jnp.where(rows == tm - 1, next_row, pltpu.roll(x, tm - 1, 0))
    return xp, x, xn


def _conv(x_src, col0, width, w3, bias, act, name, out_dtype=BF16):
    s = x_src.shape[0]
    tm = _pick(s, (512, 128))
    tc = 256
    nb = s // tm
    c0 = col0 // tc

    def body(x_ref, hp_ref, hn_ref, w_ref, b_ref, o_ref):
        i = pl.program_id(0)
        xp, x, xn = _shifted(x_ref, hp_ref, hn_ref, i, nb - 1)
        w = w_ref[...]
        y = w[0:1, :] * xp + w[1:2, :] * x + w[2:3, :] * xn + b_ref[...]
        if act:
            y = y * _sigmoid(y)
        o_ref[...] = y.astype(o_ref.dtype)

    in_specs = ([pl.BlockSpec((tm, tc), lambda i, j: (i, c0 + j))] + _halo_specs(tm, tc, c0, nb)
                + [pl.BlockSpec((3, tc), lambda i, j: (0, j)), pl.BlockSpec((1, tc), lambda i, j: (0, j))])
    return pl.pallas_call(
        body, grid=(nb, width // tc), in_specs=in_specs, out_specs=pl.BlockSpec((tm, tc), lambda i, j: (i, j)),
        out_shape=SDS((s, width), out_dtype), name=name, compiler_params=_cp("parallel", "parallel"))(x_src, x_src, x_src, w3, bias)


def _conv_silu_bwd(x_src, col0, width, w3, bias, addends, add_widths, name):
    s = x_src.shape[0]
    tm = _pick(s, (512, 128))
    tc = 256
    nb = s // tm
    c0 = col0 // tc
    na = len(addends)

    def body(*refs):
        x_ref, hp_ref, hn_ref, w_ref, b_ref = refs[:5]
        a_refs = refs[5:5 + na]
        dp_ref, dw_ref, db_ref = refs[5 + na:]
        i, j = pl.program_id(1), pl.program_id(0)
        xp, x, xn = _shifted(x_ref, hp_ref, hn_ref, i, nb - 1)
        w = w_ref[...]
        pre = w[0:1, :] * xp + w[1:2, :] * x + w[2:3, :] * xn + b_ref[...]
        g = jnp.zeros_like(pre)
        for a_ref, aw in zip(a_refs, add_widths):
            av = a_ref[...].astype(F32)
            g = g + (av if aw == width else jnp.where(j < aw // tc, av, 0.0))
        sg = _sigmoid(pre)
        dpre = g * (sg * (1.0 + pre * (1.0 - sg)))
        dp_ref[...] = dpre.astype(dp_ref.dtype)

        @pl.when(i == 0)
        def _():
            dw_ref[...] = jnp.zeros_like(dw_ref)
            db_ref[...] = jnp.zeros_like(db_ref)

        dw_ref[...] += jnp.concatenate([jnp.sum(dpre * xp, axis=0, keepdims=True), jnp.sum(dpre * x, axis=0, keepdims=True),
                                        jnp.sum(dpre * xn, axis=0, keepdims=True)], axis=0)
        db_ref[...] += jnp.sum(dpre, axis=0, keepdims=True)

    r = tm // 16
    in_specs = [pl.BlockSpec((tm, tc), lambda j, i: (i, c0 + j)),
                pl.BlockSpec((16, tc), lambda j, i: (jnp.maximum(i * r - 1, 0), c0 + j)),
                pl.BlockSpec((16, tc), lambda j, i: (jnp.minimum((i + 1) * r, nb * r - 1), c0 + j)),
                pl.BlockSpec((3, tc), lambda j, i: (0, j)), pl.BlockSpec((1, tc), lambda j, i: (0, j))]
    for aw in add_widths:
        nblk = aw // tc
        in_specs.append(pl.BlockSpec((tm, tc), lambda j, i, nblk=nblk: (i, jnp.minimum(j, nblk - 1))))
    return pl.pallas_call(
        body, grid=(width // tc, nb), in_specs=in_specs,
        out_specs=[pl.BlockSpec((tm, tc), lambda j, i: (i, j)), pl.BlockSpec((3, tc), lambda j, i: (0, j)), pl.BlockSpec((1, tc), lambda j, i: (0, j))],
        out_shape=[SDS((s, width), BF16), SDS((3, width), F32), SDS((1, width), F32)],
        name=name, compiler_params=_cp("parallel", "arbitrary"))(x_src, x_src, x_src, w3, bias, *addends)


def _ffn_gate_fwd(u, w3, bias, name):
    s = u.shape[0]
    tm = _pick(s, (512, 128))
    tc = 256
    nb = s // tm
    nj = D_FF // tc

    def body(g_ref, gp, gn, u_ref, up, un, wg_ref, wu_ref, bg_ref, bu_ref, o_ref):
        i = pl.program_id(0)
        outs = []
        for (x_ref, hp, hn, w_ref, b_ref) in ((g_ref, gp, gn, wg_ref, bg_ref), (u_ref, up, un, wu_ref, bu_ref)):
            xp, x, xn = _shifted(x_ref, hp, hn, i, nb - 1)
            w = w_ref[...]
            outs.append(w[0:1, :] * xp + w[1:2, :] * x + w[2:3, :] * xn + b_ref[...])
        gate, upv = outs
        o_ref[...] = (gate * _sigmoid(gate) * upv).astype(o_ref.dtype)

    def xspecs(c0):
        return [pl.BlockSpec((tm, tc), lambda i, j: (i, c0 + j))] + _halo_specs(tm, tc, c0, nb)

    in_specs = (xspecs(0) + xspecs(nj)
                + [pl.BlockSpec((3, tc), lambda i, j: (0, j)), pl.BlockSpec((3, tc), lambda i, j: (0, nj + j)),
                   pl.BlockSpec((1, tc), lambda i, j: (0, j)), pl.BlockSpec((1, tc), lambda i, j: (0, nj + j))])
    return pl.pallas_call(
        body, grid=(nb, nj), in_specs=in_specs, out_specs=pl.BlockSpec((tm, tc), lambda i, j: (i, j)),
        out_shape=SDS((s, D_FF), BF16), name=name, compiler_params=_cp("parallel", "parallel"))(u, u, u, u, u, u, w3, w3, bias, bias)


def _ffn_gate_bwd(u, w3, bias, dact, name):
    s = u.shape[0]
    tm = _pick(s, (512, 128))
    tc = 256
    nb = s // tm
    nj = D_FF // tc

    def body(g_ref, gp, gn, u_ref, up, un, wg_ref, wu_ref, bg_ref, bu_ref, da_ref, dg_ref, du_ref, dwg_ref, dwu_ref, dbg_ref, dbu_ref):
        i = pl.program_id(1)
        sh, pre = [], []
        for (x_ref, hp, hn, w_ref, b_ref) in ((g_ref, gp, gn, wg_ref, bg_ref), (u_ref, up, un, wu_ref, bu_ref)):
            xs3 = _shifted(x_ref, hp, hn, i, nb - 1)
            w = w_ref[...]
            sh.append(xs3)
            pre.append(w[0:1, :] * xs3[0] + w[1:2, :] * xs3[1] + w[2:3, :] * xs3[2] + b_ref[...])
        gate, upv = pre
        da = da_ref[...].astype(F32)
        sg = _sigmoid(gate)
        dgate = da * upv * (sg * (1.0 + gate * (1.0 - sg)))
        dup = da * gate * sg
        dg_ref[...] = dgate.astype(dg_ref.dtype)
        du_ref[...] = dup.astype(du_ref.dtype)

        @pl.when(i == 0)
        def _():
            for r in (dwg_ref, dwu_ref, dbg_ref, dbu_ref):
                r[...] = jnp.zeros_like(r)

        for d, xs3, dw_ref, db_ref in ((dgate, sh[0], dwg_ref, dbg_ref), (dup, sh[1], dwu_ref, dbu_ref)):
            dw_ref[...] += jnp.concatenate([jnp.sum(d * xs3[0], axis=0, keepdims=True), jnp.sum(d * xs3[1], axis=0, keepdims=True),
                                            jnp.sum(d * xs3[2], axis=0, keepdims=True)], axis=0)
            db_ref[...] += jnp.sum(d, axis=0, keepdims=True)

    r = tm // 16

    def xspecs(c0):
        return [pl.BlockSpec((tm, tc), lambda j, i: (i, c0 + j)),
                pl.BlockSpec((16, tc), lambda j, i: (jnp.maximum(i * r - 1, 0), c0 + j)),
                pl.BlockSpec((16, tc), lambda j, i: (jnp.minimum((i + 1) * r, nb * r - 1), c0 + j))]

    in_specs = (xspecs(0) + xspecs(nj)
                + [pl.BlockSpec((3, tc), lambda j, i: (0, j)), pl.BlockSpec((3, tc), lambda j, i: (0, nj + j)),
                   pl.BlockSpec((1, tc), lambda j, i: (0, j)), pl.BlockSpec((1, tc), lambda j, i: (0, nj + j)),
                   pl.BlockSpec((tm, tc), lambda j, i: (i, j))])
    blk = pl.BlockSpec((tm, tc), lambda j, i: (i, j))
    w_o = pl.BlockSpec((3, tc), lambda j, i: (0, j))
    b_o = pl.BlockSpec((1, tc), lambda j, i: (0, j))
    return pl.pallas_call(
        body, grid=(nj, nb), in_specs=in_specs, out_specs=[blk, blk, w_o, w_o, b_o, b_o],
        out_shape=[SDS((s, D_FF), BF16), SDS((s, D_FF), BF16), SDS((3, D_FF), F32), SDS((3, D_FF), F32), SDS((1, D_FF), F32), SDS((1, D_FF), F32)],
        name=name, compiler_params=_cp("parallel", "arbitrary"))(u, u, u, u, u, u, w3, w3, bias, bias, dact)


def _ssd_common(dt_ref, dtt_ref, al_r, al_c, bi_r, bi_c, off, rev):
    li = lax.broadcasted_iota(jnp.int32, (CHUNK, CHUNK), 0)
    si = lax.broadcasted_iota(jnp.int32, (CHUNK, CHUNK), 1)
    mask = (li <= si) if rev else (li >= si)
    mask_t = (li >= si) if rev else (li <= si)
    a_r = -jnp.exp(al_r[...])
    a_c = -jnp.exp(al_c[...])
    pre = dt_ref[:, off:off + N_HEADS] + bi_r[...]
    dt = _softplus(pre)
    cs = jnp.dot(mask.astype(F32), dt * a_r, precision=HIGH, preferred_element_type=F32)
    dt_t = _softplus(dtt_ref[off:off + N_HEADS, :] + bi_c[...])
    cs_t = jnp.dot(dt_t * a_c, mask_t.astype(F32), precision=HIGH, preferred_element_type=F32)
    tot = cs[0:1, :] if rev else cs[CHUNK - 1:CHUNK, :]
    return mask, mask_t, a_r, pre, dt, cs, cs_t, tot


def _ssd_fwd(xbc, dt_raw, dt_t, al_r, al_c, bi_r, bi_c, off, rev, name):
    s = xbc.shape[0]
    nc = s // CHUNK
    hp = D_INNER // N_HEADS
    gs = N_GROUPS * D_STATE
    cm = (lambda c: nc - 1 - c) if rev else (lambda c: c)

    def body(x_ref, b_ref, c_ref, dt_ref, dtt_ref, alr, alc, bir, bic, y_ref, st_ref, h_scr):
        @pl.when(pl.program_id(0) == 0)
        def _():
            h_scr[...] = jnp.zeros_like(h_scr)

        mask, _, _, _, dt, cs, cs_t, tot = _ssd_common(dt_ref, dtt_ref, alr, alc, bir, bic, off, rev)
        xs = x_ref[...]
        ys = []
        for g in range(N_GROUPS):
            bg = b_ref[:, g * D_STATE:(g + 1) * D_STATE]
            cg = c_ref[:, g * D_STATE:(g + 1) * D_STATE]
            gm = lax.dot_general(cg, bg, NT, preferred_element_type=F32)
            for r in range(N_HEADS // N_GROUPS):
                h = g * (N_HEADS // N_GROUPS) + r
                csc, csr, tot_h = cs[:, h:h + 1], cs_t[h:h + 1, :], tot[:, h:h + 1]
                lm = jnp.exp(jnp.where(mask, csc - csr, NEG))
                xd = (xs[:, h * hp:(h + 1) * hp].astype(F32) * dt[:, h:h + 1]).astype(BF16)
                hh = h_scr[h]
                y = jnp.dot((gm * lm).astype(BF16), xd, preferred_element_type=F32)
                y = y + jnp.exp(csc) * lax.dot_general(cg, hh.astype(BF16), NT, preferred_element_type=F32)
                st_ref[0, h] = hh
                bd = (bg.astype(F32) * jnp.exp(tot_h - csc)).astype(BF16)
                h_scr[h] = jnp.exp(tot_h) * hh + lax.dot_general(xd, bd, TN, preferred_element_type=F32)
                ys.append(y)
        y_ref[...] = jnp.concatenate(ys, axis=1)

    small = lambda shape: pl.BlockSpec(shape, lambda c: (0, 0))
    in_specs = [pl.BlockSpec((CHUNK, D_INNER), lambda c: (cm(c), 0)),
                pl.BlockSpec((CHUNK, gs), lambda c: (cm(c), D_INNER // gs)),
                pl.BlockSpec((CHUNK, gs), lambda c: (cm(c), D_INNER // gs + 1)),
                pl.BlockSpec((CHUNK, 128), lambda c: (cm(c), 0)),
                pl.BlockSpec((2 * N_HEADS, CHUNK), lambda c: (0, cm(c))),
                small((1, N_HEADS)), small((N_HEADS, 1)), small((1, N_HEADS)), small((N_HEADS, 1))]
    return pl.pallas_call(
        body, grid=(nc,), in_specs=in_specs,
        out_specs=[pl.BlockSpec((CHUNK, D_INNER), lambda c: (cm(c), 0)), pl.BlockSpec((1, N_HEADS, hp, D_STATE), lambda c: (cm(c), 0, 0, 0))],
        out_shape=[SDS((s, D_INNER), F32), SDS((nc, N_HEADS, hp, D_STATE), F32)],
        scratch_shapes=[pltpu.VMEM((N_HEADS, hp, D_STATE), F32)], name=name, compiler_params=_cp("arbitrary"))(
            xbc, xbc, xbc, dt_raw, dt_t, al_r, al_c, bi_r, bi_c)


def _ssd_bwd(xbc, dt_raw, dt_t, al_r, al_c, bi_r, bi_c, states, dy, off, rev, name):
    s = xbc.shape[0]
    nc = s // CHUNK
    hp = D_INNER // N_HEADS
    gs = N_GROUPS * D_STATE
    hpg = N_HEADS // N_GROUPS
    cm = (lambda c: c) if rev else (lambda c: nc - 1 - c)

    def body(x_ref, b_ref, c_ref, dt_ref, dtt_ref, alr, alc, bir, bic, st_ref, dy_ref, dx_ref, ddt_ref, dal_ref, dbi_ref, dh_scr):
        @pl.when(pl.program_id(0) == 0)
        def _():
            dh_scr[...] = jnp.zeros_like(dh_scr)
            dal_ref[...] = jnp.zeros_like(dal_ref)
            dbi_ref[...] = jnp.zeros_like(dbi_ref)

        mask, mask_t, a_r, pre, dt, cs, cs_t, tot = _ssd_common(dt_ref, dtt_ref, alr, alc, bir, bic, off, rev)
        xs = x_ref[...]
        dyv = dy_ref[...]
        rows = lax.broadcasted_iota(jnp.int32, (CHUNK, 1), 0)
        end_row = (rows == 0) if rev else (rows == CHUNK - 1)
        lane_h = lax.broadcasted_iota(jnp.int32, (1, N_HEADS), 1)
        dcs_all = jnp.zeros((CHUNK, N_HEADS), F32)
        dxsum_all = jnp.zeros((CHUNK, N_HEADS), F32)
        dxs, dbs, dcs_out = [], [], []
        for g in range(N_GROUPS):
            bg = b_ref[:, g * D_STATE:(g + 1) * D_STATE]
            cg = c_ref[:, g * D_STATE:(g + 1) * D_STATE]
            bgf = bg.astype(F32)
            gm = lax.dot_general(cg, bg, NT, preferred_element_type=F32)
            db_g = jnp.zeros((CHUNK, D_STATE), F32)
            dc_g = jnp.zeros((CHUNK, D_STATE), F32)
            for r in range(hpg):
                h = g * hpg + r
                csc, csr, tot_h = cs[:, h:h + 1], cs_t[h:h + 1, :], tot[:, h:h + 1]
                lm = jnp.exp(jnp.where(mask, csc - csr, NEG))
                xh = xs[:, h * hp:(h + 1) * hp].astype(F32)
                dtc = dt[:, h:h + 1]
                xd = (xh * dtc).astype(BF16)
                dyh = dyv[:, h * hp:(h + 1) * hp]
                dyb = dyh.astype(BF16)
                hh = st_ref[0, h]
                hb = hh.astype(BF16)
                dhn = dh_scr[h]
                dhb = dhn.astype(BF16)
                m = gm * lm
                ecs = jnp.exp(csc)
                dec = jnp.exp(tot_h - csc)
                t_h = jnp.exp(tot_h)
                dxd = lax.dot_general(m.astype(BF16), dyb, TN, preferred_element_type=F32)
                dm = lax.dot_general(dyb, xd, NT, preferred_element_type=F32)
                dg = (dm * lm).astype(BF16)
                w = dm * m
                dc_g = dc_g + jnp.dot(dg, bg, preferred_element_type=F32)
                db_g = db_g + lax.dot_general(dg, cg, TN, preferred_element_type=F32)
                ch = lax.dot_general(cg, hb, NT, preferred_element_type=F32)
                dcs = jnp.sum(dyh * (ecs * ch), axis=1, keepdims=True)
                dch = (ecs * dyh).astype(BF16)
                dc_g = dc_g + jnp.dot(dch, hb, preferred_element_type=F32)
                dh_in = lax.dot_general(dch, cg, TN, preferred_element_type=F32)
                bd = (bgf * dec).astype(BF16)
                dxd = dxd + lax.dot_general(bd, dhb, NT, preferred_element_type=F32)
                xds = jnp.dot(xd, dhb, preferred_element_type=F32)
                db_g = db_g + dec * xds
                ddec = jnp.sum(bgf * xds, axis=1, keepdims=True) * dec
                dtot = jnp.sum(jnp.sum(dhn * hh, axis=1, keepdims=True), axis=0, keepdims=True) * t_h + jnp.sum(ddec, axis=0, keepdims=True)
                dh_scr[h] = dh_in + t_h * dhn
                dcs = dcs + jnp.sum(w, axis=1, keepdims=True) - jnp.sum(w.T, axis=1, keepdims=True) - ddec
                dcs = dcs + jnp.where(end_row, dtot, 0.0)
                onehot = (lane_h == h).astype(F32)
                dcs_all = dcs_all + dcs * onehot
                dxsum_all = dxsum_all + jnp.sum(dxd * xh, axis=1, keepdims=True) * onehot
                dxs.append(dxd * dtc)
            dbs.append(db_g)
            dcs_out.append(dc_g)
        dx_ref[...] = jnp.concatenate(dxs + dbs + dcs_out, axis=1)
        da = jnp.dot(mask_t.astype(F32), dcs_all, precision=HIGH, preferred_element_type=F32)
        dal_ref[...] += jnp.sum(da * dt, axis=0, keepdims=True) * a_r
        ddt_raw = (da * a_r + dxsum_all) * _sigmoid(pre)
        ddt_ref[...] = ddt_raw
        dbi_ref[...] += jnp.sum(ddt_raw, axis=0, keepdims=True)

    small = lambda shape: pl.BlockSpec(shape, lambda c: (0, 0))
    in_specs = [pl.BlockSpec((CHUNK, D_INNER), lambda c: (cm(c), 0)),
                pl.BlockSpec((CHUNK, gs), lambda c: (cm(c), D_INNER // gs)),
                pl.BlockSpec((CHUNK, gs), lambda c: (cm(c), D_INNER // gs + 1)),
                pl.BlockSpec((CHUNK, 128), lambda c: (cm(c), 0)),
                pl.BlockSpec((2 * N_HEADS, CHUNK), lambda c: (0, cm(c))),
                small((1, N_HEADS)), small((N_HEADS, 1)), small((1, N_HEADS)), small((N_HEADS, 1)),
                pl.BlockSpec((1, N_HEADS, hp, D_STATE), lambda c: (cm(c), 0, 0, 0)),
                pl.BlockSpec((CHUNK, D_INNER), lambda c: (cm(c), 0))]
    return pl.pallas_call(
        body, grid=(nc,), in_specs=in_specs,
        out_specs=[pl.BlockSpec((CHUNK, XBC), lambda c: (cm(c), 0)), pl.BlockSpec((CHUNK, N_HEADS), lambda c: (cm(c), 0)),
                   small((1, N_HEADS)), small((1, N_HEADS))],
        out_shape=[SDS((s, XBC), F32), SDS((s, N_HEADS), F32), SDS((1, N_HEADS), F32), SDS((1, N_HEADS), F32)],
        scratch_shapes=[pltpu.VMEM((N_HEADS, hp, D_STATE), F32)], name=name, compiler_params=_cp("arbitrary"))(
            xbc, xbc, xbc, dt_raw, dt_t, al_r, al_c, bi_r, bi_c, states, dy)


def _gate_fwd(yf, yb, xbc, proj, dskip_x, norm_w, name):
    s = yf.shape[0]
    tm = _pick(s, (256, 128))
    gw = D_INNER // N_GROUPS
    zc = 3 * D_MODEL // D_INNER

    def body(yf_ref, yb_ref, x_ref, z_ref, d_ref, w_ref, o_ref):
        y = yf_ref[...] + yb_ref[...] + d_ref[...] * x_ref[...].astype(F32)
        z = z_ref[...].astype(F32)
        gt = y * (z * _sigmoid(z))
        outs = []
        for g in range(N_GROUPS):
            gg = gt[:, g * gw:(g + 1) * gw]
            outs.append(gg * lax.rsqrt(jnp.mean(gg * gg, axis=-1, keepdims=True) + EPS))
        o_ref[...] = (jnp.concatenate(outs, axis=1) * w_ref[...]).astype(o_ref.dtype)

    row = pl.BlockSpec((tm, D_INNER), lambda i: (i, 0))
    vec = pl.BlockSpec((1, D_INNER), lambda i: (0, 0))
    return pl.pallas_call(
        body, grid=(s // tm,), in_specs=[row, row, row, pl.BlockSpec((tm, D_INNER), lambda i: (i, zc)), vec, vec],
        out_specs=row, out_shape=SDS((s, D_INNER), BF16), name=name, compiler_params=_cp("parallel"))(yf, yb, xbc, proj, dskip_x, norm_w)


def _gate_bwd(yf, yb, xbc, proj, dskip_x, norm_w, dmix, name):
    s = yf.shape[0]
    tm = _pick(s, (256, 128))
    gw = D_INNER // N_GROUPS
    zc = 3 * D_MODEL // D_INNER

    def body(yf_ref, yb_ref, x_ref, z_ref, d_ref, w_ref, do_ref, dy_ref, dz_ref, dxs_ref, dw_ref, dd_ref):
        xf = x_ref[...].astype(F32)
        y = yf_ref[...] + yb_ref[...] + d_ref[...] * xf
        z = z_ref[...].astype(F32)
        sg = _sigmoid(z)
        sz = z * sg
        gt = y * sz
        do = do_ref[...].astype(F32)
        dgh = do * w_ref[...]
        ghs, dgts = [], []
        for g in range(N_GROUPS):
            gg = gt[:, g * gw:(g + 1) * gw]
            r = lax.rsqrt(jnp.mean(gg * gg, axis=-1, keepdims=True) + EPS)
            gh = gg * r
            dg = dgh[:, g * gw:(g + 1) * gw]
            ghs.append(gh)
            dgts.append(r * (dg - gh * jnp.mean(dg * gh, axis=-1, keepdims=True)))
        ghat = jnp.concatenate(ghs, axis=1)
        dgt = jnp.concatenate(dgts, axis=1)
        dy = dgt * sz
        dy_ref[...] = dy
        dz_ref[...] = (dgt * y * (sg * (1.0 + z * (1.0 - sg)))).astype(dz_ref.dtype)
        dxs_ref[...] = dy * d_ref[...]

        @pl.when(pl.program_id(0) == 0)
        def _():
            dw_ref[...] = jnp.zeros_like(dw_ref)
            dd_ref[...] = jnp.zeros_like(dd_ref)

        dw_ref[...] += jnp.sum(do * ghat, axis=0, keepdims=True)
        dd_ref[...] += jnp.sum(dy * xf, axis=0, keepdims=True)

    row = pl.BlockSpec((tm, D_INNER), lambda i: (i, 0))
    vec = pl.BlockSpec((1, D_INNER), lambda i: (0, 0))
    return pl.pallas_call(
        body, grid=(s // tm,),
        in_specs=[row, row, row, pl.BlockSpec((tm, D_INNER), lambda i: (i, zc)), vec, vec, pl.BlockSpec((tm, D_INNER), lambda i: (i, 1))],
        out_specs=[row, row, row, vec, vec],
        out_shape=[SDS((s, D_INNER), F32), SDS((s, D_INNER), BF16), SDS((s, D_INNER), F32), SDS((1, D_INNER), F32), SDS((1, D_INNER), F32)],
        name=name, compiler_params=_cp("arbitrary"))(yf, yb, xbc, proj, dskip_x, norm_w, dmix)


def _adamw(parts, w, m, v, name):
    r, c = w.shape
    tr = _pick(r, (256, 352, 128))

    def body(p_ref, w_ref, m_ref, v_ref, g_ref, d_ref, nm_ref, nv_ref):
        g = p_ref[0].astype(F32)
        for i in range(1, N_DEV):
            g = g + p_ref[i].astype(F32)
        mn = B1 * m_ref[...] + (1.0 - B1) * g
        vn = B2 * v_ref[...] + (1.0 - B2) * (g * g)
        m_hat = mn / (1.0 - B1 ** STEP)
        v_hat = vn / (1.0 - B2 ** STEP)
        g_ref[...] = g
        d_ref[...] = -LR * (m_hat / (jnp.sqrt(v_hat) + AEPS) + WD * w_ref[...])
        nm_ref[...] = mn
        nv_ref[...] = vn

    blk = pl.BlockSpec((tr, c), lambda i: (i, 0))
    return pl.pallas_call(
        body, grid=(r // tr,), in_specs=[pl.BlockSpec((N_DEV, tr, c), lambda i: (0, i, 0)), blk, blk, blk],
        out_specs=[blk, blk, blk, blk], out_shape=[SDS((r, c), F32)] * 4, name=name, compiler_params=_cp("parallel"))(parts, w, m, v)


def _sum_parts(parts, name):
    _, r, c = parts.shape

    def body(p_ref, o_ref):
        g = p_ref[0]
        for i in range(1, N_DEV):
            g = g + p_ref[i]
        o_ref[...] = g

    return pl.pallas_call(body, out_shape=SDS((r, c), F32), name=name)(parts)


def _adamw_small(g, w, m, v, name):
    def body(g_ref, w_ref, m_ref, v_ref, d_ref, nm_ref, nv_ref):
        gv = g_ref[...]
        mn = B1 * m_ref[...] + (1.0 - B1) * gv
        vn = B2 * v_ref[...] + (1.0 - B2) * (gv * gv)
        m_hat = mn / (1.0 - B1 ** STEP)
        v_hat = vn / (1.0 - B2 ** STEP)
        d_ref[...] = -LR * (m_hat / (jnp.sqrt(v_hat) + AEPS) + WD * w_ref[...])
        nm_ref[...] = mn
        nv_ref[...] = vn

    return pl.pallas_call(body, out_shape=[SDS(g.shape, F32)] * 3, name=name)(g, w, m, v)


def _my_index():
    return 4 * lax.axis_index("x") + 2 * lax.axis_index("y") + lax.axis_index("c")


def _all_gather(shards, name):
    n = len(shards)

    def body(*refs):
        srcs, outs = refs[:n], refs[n:2 * n]
        send_sems, recv_sems, local_sems = refs[2 * n:]
        x, y, c = lax.axis_index("x"), lax.axis_index("y"), lax.axis_index("c")
        sibling = (x, y, 1 - c)
        chips = [(1 - x, y), (x, 1 - y), (1 - x, 1 - y)]

        def idx(px, py, pc):
            return 4 * px + 2 * py + pc

        def copy(a, k, block, to, src=None):
            dst = outs[a].at[idx(*block)]
            return pltpu.make_async_remote_copy(
                src_ref=dst if src is None else src, dst_ref=dst,
                send_sem=send_sems.at[a * 7 + k], recv_sem=recv_sems.at[a * 7 + k], device_id=to, device_id_type=MESH)

        me = (x, y, c)
        locals_, first, passed = [], [], []
        for a in range(n):
            lc = pltpu.make_async_copy(srcs[a], outs[a].at[idx(*me)], local_sems.at[a])
            lc.start()
            locals_.append(lc)
            first.append(copy(a, 0, me, sibling, src=srcs[a]))
            first += [copy(a, 1 + j, me, (*chip, c), src=srcs[a]) for j, chip in enumerate(chips)]
        for cp in first:
            cp.start()
        for j, chip in enumerate(chips):
            for a in range(n):
                copy(a, 1 + j, (*chip, c), me).wait_recv()
                fw = copy(a, 4 + j, (*chip, c), sibling)
                fw.start()
                passed.append(fw)
        for a in range(n):
            copy(a, 0, sibling, me).wait_recv()
            for j, chip in enumerate(chips):
                copy(a, 4 + j, (*chip, 1 - c), me).wait_recv()
        for cp in first + passed:
            cp.wait_send()
        for lc in locals_:
            lc.wait()

    any_spec = pl.BlockSpec(memory_space=pl.ANY)
    return pl.pallas_call(
        body, in_specs=[any_spec] * n, out_specs=[any_spec] * n,
        out_shape=[SDS((N_DEV,) + s.shape, s.dtype) for s in shards],
        scratch_shapes=[pltpu.SemaphoreType.DMA((7 * n,)), pltpu.SemaphoreType.DMA((7 * n,)), pltpu.SemaphoreType.DMA((n,))],
        name=name)(*shards)


def _exchange(arrays, name):
    n = len(arrays)

    def body(*refs):
        srcs, outs = refs[:n], refs[n:2 * n]
        send_sems, recv_sems, local_sems = refs[2 * n:]
        x, y, c = lax.axis_index("x"), lax.axis_index("y"), lax.axis_index("c")
        me = 4 * x + 2 * y + c
        copies, locals_ = [], []
        for a in range(n):
            lc = pltpu.make_async_copy(srcs[a].at[me], outs[a].at[me], local_sems.at[a])
            lc.start()
            locals_.append(lc)
        for k in range(1, N_DEV):
            px, py, pc = x ^ ((k >> 2) & 1), y ^ ((k >> 1) & 1), c ^ (k & 1)
            peer = 4 * px + 2 * py + pc
            for a in range(n):
                cp = pltpu.make_async_remote_copy(
                    src_ref=srcs[a].at[peer], dst_ref=outs[a].at[me],
                    send_sem=send_sems.at[a * 7 + k - 1], recv_sem=recv_sems.at[a * 7 + k - 1],
                    device_id=(px, py, pc), device_id_type=MESH)
                cp.start()
                copies.append((cp, a, peer, k))
        for cp, a, peer, k in copies:
            pltpu.make_async_remote_copy(
                src_ref=srcs[a].at[peer], dst_ref=outs[a].at[peer],
                send_sem=send_sems.at[a * 7 + k - 1], recv_sem=recv_sems.at[a * 7 + k - 1],
                device_id=(x, y, c), device_id_type=MESH).wait_recv()
        for cp, _, _, _ in copies:
            cp.wait_send()
        for lc in locals_:
            lc.wait()

    any_spec = pl.BlockSpec(memory_space=pl.ANY)
    return pl.pallas_call(
        body, in_specs=[any_spec] * n, out_specs=[any_spec] * n,
        out_shape=[SDS(a.shape, a.dtype) for a in arrays],
        scratch_shapes=[pltpu.SemaphoreType.DMA((7 * n,)), pltpu.SemaphoreType.DMA((7 * n,)), pltpu.SemaphoreType.DMA((n,))],
        name=name)(*arrays)


def _to_pattern(t, d):
    if d == 1:
        return t
    s, w = t.shape
    return t.reshape(s // d, d, w).transpose(1, 0, 2).reshape(s, w)


def _from_pattern(t, d):
    if d == 1:
        return t
    s, w = t.shape
    return t.reshape(d, s // d, w).transpose(1, 0, 2).reshape(s, w)


def _pad_lanes(t, n):
    return jnp.pad(t, ((0, 0), (0, n - t.shape[1])))


def _local_step(x, target, p):
    s = x.shape[0]
    tabs_f, tabs_b = _rope_tables(s)
    expand = jnp.repeat(jnp.eye(N_HEADS, dtype=F32), HEAD_DIM, axis=1)
    w_main, w_dt = p["w_in"][:, :MAIN_W], _pad_lanes(p["w_in"][:, MAIN_W:], 128)
    al_r = {"f": p["a_log_f"], "b": p["a_log_b"]}
    bi_r = {"f": p["dt_bias_f"], "b": p["dt_bias_b"]}
    dskip_x = jnp.repeat(p["d_skip"], D_INNER // N_HEADS, axis=1)
    ssm_w3, ffn_w3 = p["ssm_conv_w"].T, p["ffn_conv_w"].T

    h1 = _rmsnorm_fwd(x, p["norm1_w"], "norm1_fwd")
    proj = _matmul(h1, w_main, name="in_proj")
    dt_raw = _matmul(h1, w_dt, name="in_proj_dt", out_dtype=F32)
    dt_t = dt_raw[:, :2 * N_HEADS].T
    qk = _rope_fwd(proj, tabs_f, "rope_fwd")
    v_col = 2
    pat = []
    for d in DILATIONS:
        qk_p = _to_pattern(qk, d)
        v_p = proj if d == 1 else _to_pattern(proj[:, 2 * D_MODEL:3 * D_MODEL], d)
        pat.append((qk_p, v_p, v_col if d == 1 else 0))
    os_, lses = [], []
    for d, (qk_p, v_p, vc) in zip(DILATIONS, pat):
        o_p, lse_p = _attn_fwd(qk_p, v_p, vc, s // d, f"attn_fwd_d{d}")
        os_.append(_from_pattern(o_p, d))
        lses.append(_from_pattern(lse_p, d))
    attn, lse_tot = _attn_combine(os_, lses, expand, "attn_combine")

    xbc = _conv(proj, 3 * D_MODEL + D_INNER, XBC, ssm_w3, p["ssm_conv_b"], True, "ssm_conv_fwd")
    col = lambda r: r.reshape(N_HEADS, 1)
    ssd_args = {k: (al_r[k], col(al_r[k]), bi_r[k], col(bi_r[k])) for k in ("f", "b")}
    yf, st_f = _ssd_fwd(xbc, dt_raw, dt_t, *ssd_args["f"], 0, False, "ssd_fwd_f")
    yb, st_b = _ssd_fwd(xbc, dt_raw, dt_t, *ssd_args["b"], N_HEADS, True, "ssd_fwd_b")
    ssm = _gate_fwd(yf, yb, xbc, proj, dskip_x, p["ssm_norm_w"], "ssm_gate_fwd")

    mix = jnp.concatenate([attn, ssm], axis=1)
    x2 = _matmul(mix, p["w_out"], name="out_proj", out_dtype=F32, residual=x)
    h2 = _rmsnorm_fwd(x2, p["norm2_w"], "norm2_fwd")
    u = _matmul(h2, p["w_up"], name="ffn_up")
    act = _ffn_gate_fwd(u, ffn_w3, p["ffn_conv_b"], "ffn_gate_fwd")
    x3 = _matmul(act, p["w_down"], name="ffn_down", out_dtype=F32, residual=x2)

    dx3, g_final, loss = _final_norm_loss(x3, p["final_norm_w"].reshape(1, D_MODEL), target, "final_norm_loss")
    dx3b = dx3.astype(BF16)
    g_w_down = _matmul(act.T, dx3b, name="dw_down")
    dact = _matmul(dx3b, p["w_down"], name="d_act", trans_b=True)
    dug, duu, dwg, dwu, dbg, dbu = _ffn_gate_bwd(u, ffn_w3, p["ffn_conv_b"], dact, "ffn_gate_bwd")
    flip3 = lambda w3: w3[::-1]
    zero_b = jnp.zeros((1, D_FF), F32)
    du_g = _conv(dug, 0, D_FF, flip3(ffn_w3[:, :D_FF]), zero_b, False, "ffn_conv_bwd_gate")
    du_u = _conv(duu, 0, D_FF, flip3(ffn_w3[:, D_FF:]), zero_b, False, "ffn_conv_bwd_up")
    du = jnp.concatenate([du_g, du_u], axis=1)
    g_w_up = _matmul(h2.T, du, name="dw_up")
    dh2 = _matmul(du, p["w_up"], name="d_h2", trans_b=True)
    dx2, g_norm2 = _rmsnorm_bwd(x2, p["norm2_w"], dh2, dx3, "norm2_bwd")
    dx2b = dx2.astype(BF16)
    g_w_out = _matmul(mix.T, dx2b, name="dw_out")
    dmix = _matmul(dx2b, p["w_out"], name="d_mix", trans_b=True)

    dy, dz, dxs_skip, g_ssm_norm, g_dskip_lanes = _gate_bwd(yf, yb, xbc, proj, dskip_x, p["ssm_norm_w"], dmix, "ssm_gate_bwd")
    dxbc_f, ddt_f, g_al_f, g_bi_f = _ssd_bwd(xbc, dt_raw, dt_t, *ssd_args["f"], st_f, dy, 0, False, "ssd_bwd_f")
    dxbc_b, ddt_b, g_al_b, g_bi_b = _ssd_bwd(xbc, dt_raw, dt_t, *ssd_args["b"], st_b, dy, N_HEADS, True, "ssd_bwd_b")
    dpre, g_ssm_w3, g_ssm_cb = _conv_silu_bwd(proj, 3 * D_MODEL + D_INNER, XBC, ssm_w3, p["ssm_conv_b"],
                                              [dxbc_f, dxbc_b, dxs_skip], [XBC, XBC, D_INNER], "ssm_conv_bwd")
    dxbc_raw = _conv(dpre, 0, XBC, flip3(ssm_w3), jnp.zeros((1, XBC), F32), False, "ssm_conv_bwd_x")

    delta = _attn_delta(dmix, attn, expand.T, "attn_delta")
    dq_u, dk_u, dv_u = [], [], []
    for d, (qk_p, v_p, vc) in zip(DILATIONS, pat):
        do_p = dmix if d == 1 else _to_pattern(dmix[:, :D_MODEL], d)
        lse_p, dl_p = _to_pattern(lse_tot, d), _to_pattern(delta, d)
        dq = _attn_bwd_dq(qk_p, v_p, vc, do_p, lse_p, dl_p, s // d, f"attn_bwd_dq_d{d}")
        dk, dv = _attn_bwd_dkv(qk_p, v_p, vc, do_p, lse_p, dl_p, s // d, f"attn_bwd_dkv_d{d}")
        dq_u.append(_from_pattern(dq, d))
        dk_u.append(_from_pattern(dk, d))
        dv_u.append(_from_pattern(dv, d))
    dq_raw = _sum3_rope(*dq_u, tabs_b, "rope_bwd_q")
    dk_raw = _sum3_rope(*dk_u, tabs_b, "rope_bwd_k")
    dv_raw = _sum3_rope(*dv_u, None, "sum_dv")

    dproj = jnp.concatenate([dq_raw, dk_raw, dv_raw, dz, dxbc_raw], axis=1)
    ddt = _pad_lanes(jnp.concatenate([ddt_f, ddt_b], axis=1), 128).astype(BF16)
    h1t = h1.T
    g_w_main = _matmul(h1t, dproj, name="dw_in")
    g_w_dt = _matmul(h1t, ddt, name="dw_in_dt")
    dh1 = _matmul(dproj, w_main, name="d_h1", trans_b=True, out_dtype=F32)
    dh1 = _matmul(ddt, w_dt, name="d_h1_dt", trans_b=True, out_dtype=F32, residual=dh1)
    grad_x, g_norm1 = _rmsnorm_bwd(x, p["norm1_w"], dh1, dx2, "norm1_bwd")

    g_w_in = jnp.concatenate([g_w_main, g_w_dt[:, :2 * N_HEADS]], axis=1)
    g_dskip = jnp.sum(g_dskip_lanes.reshape(N_HEADS, D_INNER // N_HEADS), axis=1).reshape(1, N_HEADS)
    small = {
        "norm1_w": g_norm1, "ssm_conv_w": g_ssm_w3.T, "ssm_conv_b": g_ssm_cb, "a_log_f": g_al_f, "a_log_b": g_al_b,
        "dt_bias_f": g_bi_f, "dt_bias_b": g_bi_b, "d_skip": g_dskip, "ssm_norm_w": g_ssm_norm, "norm2_w": g_norm2,
        "ffn_conv_w": jnp.concatenate([dwg, dwu], axis=1).T, "ffn_conv_b": jnp.concatenate([dbg, dbu], axis=1), "final_norm_w": g_final,
    }
    big = {"w_in": g_w_in, "w_out": g_w_out, "w_up": g_w_up, "w_down": g_w_down}
    return loss[0, 0], grad_x, big, small


SMALL_ORDER = ("norm1_w", "ssm_conv_w", "ssm_conv_b", "a_log_f", "a_log_b", "dt_bias_f", "dt_bias_b", "d_skip",
               "ssm_norm_w", "norm2_w", "ffn_conv_w", "ffn_conv_b", "final_norm_w")
SHARDED_SMALL = ("ssm_conv_w", "ffn_conv_w")
BIG_ORDER = ("w_in", "w_out", "w_up", "w_down")


def _pack(vals):
    rows = []
    for v in vals:
        f = v.reshape(-1).astype(F32)
        n = -(-f.shape[0] // 128) * 128
        rows.append(jnp.pad(f, (0, n - f.shape[0])).reshape(-1, 128))
    out = jnp.concatenate(rows, axis=0)
    pad = -out.shape[0] % 8
    return jnp.pad(out, ((0, pad), (0, 0)))


def _unpack(packed, shapes):
    out, r = [], 0
    for shp in shapes:
        n = math.prod(shp)
        nr = -(-n // 128)
        out.append(packed[r:r + nr].reshape(-1)[:n].reshape(shp))
        r += nr
    return out


def kernel(x, norm1_w, w_in, ssm_conv_w, ssm_conv_b, a_log_f, a_log_b, dt_bias_f, dt_bias_b, d_skip, ssm_norm_w, w_out, norm2_w, w_up, ffn_conv_w, ffn_conv_b, w_down, final_norm_w, loss_target, m_norm1_w, m_w_in, m_ssm_conv_w, m_ssm_conv_b, m_a_log_f, m_a_log_b, m_dt_bias_f, m_dt_bias_b, m_d_skip, m_ssm_norm_w, m_w_out, m_norm2_w, m_w_up, m_ffn_conv_w, m_ffn_conv_b, m_w_down, m_final_norm_w, v_norm1_w, v_w_in, v_ssm_conv_w, v_ssm_conv_b, v_a_log_f, v_a_log_b, v_dt_bias_f, v_dt_bias_b, v_d_skip, v_ssm_norm_w, v_w_out, v_norm2_w, v_w_up, v_ffn_conv_w, v_ffn_conv_b, v_w_down, v_final_norm_w):
    w = dict(norm1_w=norm1_w, w_in=w_in, ssm_conv_w=ssm_conv_w, ssm_conv_b=ssm_conv_b, a_log_f=a_log_f, a_log_b=a_log_b,
             dt_bias_f=dt_bias_f, dt_bias_b=dt_bias_b, d_skip=d_skip, ssm_norm_w=ssm_norm_w, w_out=w_out, norm2_w=norm2_w,
             w_up=w_up, ffn_conv_w=ffn_conv_w, ffn_conv_b=ffn_conv_b, w_down=w_down, final_norm_w=final_norm_w)
    mo = dict(norm1_w=m_norm1_w, w_in=m_w_in, ssm_conv_w=m_ssm_conv_w, ssm_conv_b=m_ssm_conv_b, a_log_f=m_a_log_f, a_log_b=m_a_log_b,
              dt_bias_f=m_dt_bias_f, dt_bias_b=m_dt_bias_b, d_skip=m_d_skip, ssm_norm_w=m_ssm_norm_w, w_out=m_w_out, norm2_w=m_norm2_w,
              w_up=m_w_up, ffn_conv_w=m_ffn_conv_w, ffn_conv_b=m_ffn_conv_b, w_down=m_w_down, final_norm_w=m_final_norm_w)
    vo = dict(norm1_w=v_norm1_w, w_in=v_w_in, ssm_conv_w=v_ssm_conv_w, ssm_conv_b=v_ssm_conv_b, a_log_f=v_a_log_f, a_log_b=v_a_log_b,
              dt_bias_f=v_dt_bias_f, dt_bias_b=v_dt_bias_b, d_skip=v_d_skip, ssm_norm_w=v_ssm_norm_w, w_out=v_w_out, norm2_w=v_norm2_w,
              w_up=v_w_up, ffn_conv_w=v_ffn_conv_w, ffn_conv_b=v_ffn_conv_b, w_down=v_w_down, final_norm_w=v_final_norm_w)
    me = _my_index()

    shards = [w["w_in"][0].astype(BF16), w["w_out"][0].astype(BF16), w["w_up"][0].astype(BF16), w["w_down"][0].astype(BF16),
              _pack([w["ssm_conv_w"][0], w["ffn_conv_w"][0]])]
    g_in, g_out, g_up, g_down, g_conv = _all_gather(shards, "weights_all_gather")
    n_in, n_up = w_in.shape[2], w_up.shape[2]
    conv_rows = [_unpack(g_conv[i], [ssm_conv_w.shape[1:], ffn_conv_w.shape[1:]]) for i in range(N_DEV)]
    full = {
        "w_in": g_in.transpose(1, 0, 2).reshape(D_MODEL, N_DEV * n_in),
        "w_out": g_out.reshape(2 * D_MODEL, D_MODEL),
        "w_up": g_up.transpose(1, 0, 2).reshape(D_MODEL, N_DEV * n_up),
        "w_down": g_down.reshape(D_FF, D_MODEL),
        "ssm_conv_w": jnp.concatenate([c[0] for c in conv_rows], axis=0),
        "ffn_conv_w": jnp.concatenate([c[1] for c in conv_rows], axis=0),
    }
    for k in ("norm1_w", "ssm_conv_b", "a_log_f", "a_log_b", "dt_bias_f", "dt_bias_b", "d_skip", "ssm_norm_w", "norm2_w", "ffn_conv_b"):
        full[k] = w[k]
    full["final_norm_w"] = w["final_norm_w"]

    loss_part, grad_x, big, small = _local_step(x[0], loss_target[0], full)

    out_big = [big["w_in"].reshape(D_MODEL, N_DEV, n_in).transpose(1, 0, 2),
               big["w_out"].reshape(N_DEV, 2 * D_MODEL // N_DEV, D_MODEL),
               big["w_up"].reshape(D_MODEL, N_DEV, n_up).transpose(1, 0, 2),
               big["w_down"].reshape(N_DEV, D_FF // N_DEV, D_MODEL)]
    small_shapes = [(1,)] + [small[k].shape for k in SMALL_ORDER]
    packed = _pack([loss_part] + [small[k] for k in SMALL_ORDER])
    out_small = jnp.broadcast_to(packed[None], (N_DEV,) + packed.shape)
    r_in, r_out, r_up, r_down, r_small = _exchange(out_big + [out_small], "grads_exchange")

    outs_g, outs_d, outs_m, outs_v = {}, {}, {}, {}
    for k, parts in zip(BIG_ORDER, (r_in, r_out, r_up, r_down)):
        g, dlt, nm, nv = _adamw(parts, w[k][0], mo[k][0], vo[k][0], f"adamw_{k}")
        outs_g[k], outs_d[k], outs_m[k], outs_v[k] = g[None], dlt[None], nm[None], nv[None]
    tot = _unpack(_sum_parts(r_small, "small_grads_sum"), small_shapes)
    loss = tot[0][0]
    gs = dict(zip(SMALL_ORDER, tot[1:]))
    g_own = {}
    for k in SMALL_ORDER:
        if k in SHARDED_SMALL:
            rows = w[k].shape[1]
            g_own[k] = lax.dynamic_slice_in_dim(gs[k], me * rows, rows, axis=0)[None]
        else:
            g_own[k] = gs[k].reshape(w[k].shape)
    shapes = [w[k].shape for k in SMALL_ORDER]
    d_s, m_s, v_s = _adamw_small(_pack([g_own[k] for k in SMALL_ORDER]), _pack([w[k] for k in SMALL_ORDER]),
                                 _pack([mo[k] for k in SMALL_ORDER]), _pack([vo[k] for k in SMALL_ORDER]), "adamw_small")
    for k, a, b, c in zip(SMALL_ORDER, _unpack(d_s, shapes), _unpack(m_s, shapes), _unpack(v_s, shapes)):
        outs_g[k], outs_d[k], outs_m[k], outs_v[k] = g_own[k], a, b, c

    order = ("norm1_w", "w_in", "ssm_conv_w", "ssm_conv_b", "a_log_f", "a_log_b", "dt_bias_f", "dt_bias_b", "d_skip", "ssm_norm_w",
             "w_out", "norm2_w", "w_up", "ffn_conv_w", "ffn_conv_b", "w_down", "final_norm_w")
    return (loss, grad_x[None], *[outs_g[k] for k in order], *[outs_d[k] for k in order],
            *[outs_m[k] for k in order], *[outs_v[k] for k in order])
```

```python
import math

import jax
import jax.numpy as jnp
from jax import lax
from jax.experimental import pallas as pl
from jax.experimental.pallas import tpu as pltpu

F32 = jnp.float32
BF16 = jnp.bfloat16
SDS = jax.ShapeDtypeStruct

N_DEV = 8
D_MODEL = 1024
N_HEADS = 16
HEAD_DIM = 64
ROPE_DIM = 16
ROPE_THETA = 500000.0
DILATIONS = (1, 4, 16)
BAND_HALF = 64
D_INNER = 1024
N_GROUPS = 4
D_STATE = 128
CHUNK = 128
XBC = D_INNER + 2 * N_GROUPS * D_STATE
D_FF = 2816
MAIN_W = 3 * D_MODEL + D_INNER + XBC
EPS = 1e-6
LR, B1, B2, AEPS, WD, STEP = 0.001, 0.9, 0.999, 1e-08, 0.01, 10
NEG = -1e30
VMEM_LIMIT = 56 * 1024 * 1024
MESH = pl.DeviceIdType.MESH
HIGH = lax.Precision.HIGHEST
NT = (((1,), (1,)), ((), ()))
TN = (((0,), (0,)), ((), ()))


def _cp(*sem):
    return pltpu.CompilerParams(dimension_semantics=sem, vmem_limit_bytes=VMEM_LIMIT)


def _pick(n, cands):
    for c in cands:
        if n % c == 0:
            return c
    raise ValueError(f"no tile for {n}")


def _sigmoid(x):
    return 1.0 / (1.0 + jnp.exp(-x))


def _softplus(x):
    return jnp.maximum(x, 0.0) + jnp.log1p(jnp.exp(-jnp.abs(x)))


def _slab(s, width, dtype, slab, n_in, out_idx=0):
    if slab is None:
        return SDS((s, width), dtype), 0, [], [], {}
    total, col0, into = slab
    if into is None:
        return SDS((s, total), dtype), col0, [], [], {}
    return SDS((s, total), dtype), col0, [into], [pl.BlockSpec(memory_space=pl.ANY)], {n_in: out_idx}


def _matmul(a, b, *, name, trans_b=False, out_dtype=BF16, residual=None):
    m, k = a.shape
    n = b.shape[0] if trans_b else b.shape[1]
    tk = k if k <= 2048 else _pick(k, (2048, 1408, 1024, 512))
    nk = k // tk
    if nk == 1:
        tm = _pick(m, (2048, 1408, 1024, 512, 256, 128))
        tn = _pick(n, (512, 256, 128))
    else:
        tm = _pick(m, (1024, 1408, 512, 256, 128))
        tn = _pick(n, (1024, 1408, 512, 256, 128))
    dn = NT if trans_b else (((1,), (0,)), ((), ()))

    def body(*refs):
        a_ref, b_ref = refs[0], refs[1]
        o_ref, acc = refs[-2], refs[-1]
        kk = pl.program_id(2)

        @pl.when(kk == 0)
        def _():
            acc[...] = jnp.zeros_like(acc)

        acc[...] += lax.dot_general(a_ref[...], b_ref[...], dn, preferred_element_type=F32)

        @pl.when(kk == nk - 1)
        def _():
            r = acc[...]
            if residual is not None:
                r = r + refs[2][...].astype(F32)
            o_ref[...] = r.astype(o_ref.dtype)

    in_specs = [pl.BlockSpec((tm, tk), lambda i, j, kk: (i, kk)),
                pl.BlockSpec((tn, tk), lambda i, j, kk: (j, kk)) if trans_b else pl.BlockSpec((tk, tn), lambda i, j, kk: (kk, j))]
    args = [a, b]
    if residual is not None:
        in_specs.append(pl.BlockSpec((tm, tn), lambda i, j, kk: (i, j)))
        args.append(residual)
    return pl.pallas_call(
        body, grid=(m // tm, n // tn, nk), in_specs=in_specs,
        out_specs=pl.BlockSpec((tm, tn), lambda i, j, kk: (i, j)),
        out_shape=SDS((m, n), out_dtype), scratch_shapes=[pltpu.VMEM((tm, tn), F32)],
        name=name, compiler_params=_cp("parallel", "parallel", "arbitrary"))(*args)


def _rmsnorm_fwd(x, w, name):
    s, d = x.shape
    tm = _pick(s, (512, 128))

    def body(x_ref, w_ref, o_ref):
        xf = x_ref[...]
        r = lax.rsqrt(jnp.mean(xf * xf, axis=-1, keepdims=True) + EPS)
        o_ref[...] = (xf * r * w_ref[...]).astype(o_ref.dtype)

    return pl.pallas_call(
        body, grid=(s // tm,), in_specs=[pl.BlockSpec((tm, d), lambda i: (i, 0)), pl.BlockSpec((1, d), lambda i: (0, 0))],
        out_specs=pl.BlockSpec((tm, d), lambda i: (i, 0)), out_shape=SDS((s, d), BF16),
        name=name, compiler_params=_cp("parallel"))(x, w)


def _rmsnorm_bwd(x, w, dh, dres, name):
    s, d = x.shape
    tm = _pick(s, (512, 128))

    def body(x_ref, w_ref, dh_ref, dres_ref, dx_ref, dw_ref):
        xf = x_ref[...]
        r = lax.rsqrt(jnp.mean(xf * xf, axis=-1, keepdims=True) + EPS)
        xhat = xf * r
        dhf = dh_ref[...].astype(F32)
        g = dhf * w_ref[...]
        dx_ref[...] = dres_ref[...] + r * (g - xhat * jnp.mean(g * xhat, axis=-1, keepdims=True))

        @pl.when(pl.program_id(0) == 0)
        def _():
            dw_ref[...] = jnp.zeros_like(dw_ref)

        dw_ref[...] += jnp.sum(dhf * xhat, axis=0, keepdims=True)

    row = pl.BlockSpec((tm, d), lambda i: (i, 0))
    vec = pl.BlockSpec((1, d), lambda i: (0, 0))
    return pl.pallas_call(
        body, grid=(s // tm,), in_specs=[row, vec, row, row], out_specs=[row, vec],
        out_shape=[SDS((s, d), F32), SDS((1, d), F32)], name=name, compiler_params=_cp("arbitrary"))(x, w, dh, dres)


def _final_norm_loss(x, w, target, name):
    s, d = x.shape
    tm = _pick(s, (512, 128))

    def body(x_ref, w_ref, t_ref, dx_ref, dw_ref, loss_ref):
        xf = x_ref[...]
        r = lax.rsqrt(jnp.mean(xf * xf, axis=-1, keepdims=True) + EPS)
        xhat = xf * r
        wv = w_ref[...]
        e = xhat * wv - t_ref[...]
        dy = e * (1.0 / d)
        g = dy * wv
        dx_ref[...] = r * (g - xhat * jnp.mean(g * xhat, axis=-1, keepdims=True))

        @pl.when(pl.program_id(0) == 0)
        def _():
            dw_ref[...] = jnp.zeros_like(dw_ref)
            loss_ref[...] = jnp.zeros_like(loss_ref)

        dw_ref[...] += jnp.sum(dy * xhat, axis=0, keepdims=True)
        loss_ref[...] += jnp.sum(jnp.sum(e * e, axis=1, keepdims=True), axis=0, keepdims=True) * (0.5 / d)

    row = pl.BlockSpec((tm, d), lambda i: (i, 0))
    vec = pl.BlockSpec((1, d), lambda i: (0, 0))
    return pl.pallas_call(
        body, grid=(s // tm,), in_specs=[row, vec, row], out_specs=[row, vec, pl.BlockSpec((1, 128), lambda i: (0, 0))],
        out_shape=[SDS((s, d), F32), SDS((1, d), F32), SDS((1, 128), F32)], name=name, compiler_params=_cp("arbitrary"))(x, w, target)


def _rope_tables(s):
    half = ROPE_DIM // 2
    inv_freq = jnp.power(ROPE_THETA, -jnp.arange(half, dtype=F32) * 2.0 / ROPE_DIM)
    ang = jnp.arange(s, dtype=F32)[:, None] * inv_freq[None, :]
    cos, sin = jnp.cos(ang), jnp.sin(ang)
    z = jnp.zeros((s, HEAD_DIM - ROPE_DIM), F32)
    zh = jnp.zeros((s, half), F32)
    c = jnp.concatenate([cos, cos, z + 1.0], axis=1)
    sa = jnp.concatenate([zh, sin, z], axis=1)
    sb = jnp.concatenate([-sin, zh, z], axis=1)
    two = lambda t: jnp.concatenate([t, t], axis=1)
    c, sa, sb = two(c), two(sa), two(sb)
    return (c, sa, sb), (c, jnp.roll(sb, half, axis=1), jnp.roll(sa, -half, axis=1))


def _rope_fwd(proj, tabs, name):
    s = proj.shape[0]
    tm = _pick(s, (256, 128))
    half = ROPE_DIM // 2
    wb = D_MODEL

    def body(x_ref, c_ref, sa_ref, sb_ref, o_ref):
        c, sa, sb = c_ref[...], sa_ref[...], sb_ref[...]
        for j in range(wb // 128):
            x = x_ref[:, j * 128:(j + 1) * 128].astype(F32)
            o_ref[:, j * 128:(j + 1) * 128] = (x * c + pltpu.roll(x, half, 1) * sa + pltpu.roll(x, 128 - half, 1) * sb).astype(o_ref.dtype)

    blk = pl.BlockSpec((tm, wb), lambda i, j: (i, j))
    tab = pl.BlockSpec((tm, 128), lambda i, j: (i, 0))
    return pl.pallas_call(
        body, grid=(s // tm, 2 * D_MODEL // wb), in_specs=[blk, tab, tab, tab], out_specs=blk,
        out_shape=SDS((s, 2 * D_MODEL), BF16), name=name, compiler_params=_cp("parallel", "parallel"))(proj, *tabs)


def _sum3_rope(d1, d2, d3, tabs, name, slab=None):
    s, w = d1.shape
    tm = _pick(s, (256, 128))
    half = ROPE_DIM // 2

    def body(*refs):
        for j in range(w // 128):
            ls = slice(j * 128, (j + 1) * 128)
            x = refs[0][:, ls].astype(F32) + refs[1][:, ls].astype(F32) + refs[2][:, ls].astype(F32)
            if tabs is not None:
                x = x * refs[3][...] + pltpu.roll(x, half, 1) * refs[4][...] + pltpu.roll(x, 128 - half, 1) * refs[5][...]
            refs[-1][:, ls] = x.astype(refs[-1].dtype)

    blk = pl.BlockSpec((tm, w), lambda i: (i, 0))
    tab = pl.BlockSpec((tm, 128), lambda i: (i, 0))
    extra = [] if tabs is None else list(tabs)
    out_shape, col0, more, more_specs, alias = _slab(s, w, BF16, slab, 3 + len(extra))
    cb = col0 // w
    return pl.pallas_call(
        body, grid=(s // tm,), in_specs=[blk, blk, blk] + [tab] * len(extra) + more_specs,
        out_specs=pl.BlockSpec((tm, w), lambda i: (i, cb)), out_shape=out_shape, input_output_aliases=alias,
        name=name, compiler_params=_cp("parallel"))(d1, d2, d3, *extra, *more)


def _band_valid(t, nq, nk, qofs, kofs, seq_len):
    qpos = t * 128 + qofs + lax.broadcasted_iota(jnp.int32, (nq, nk), 0)
    kpos = t * 128 + kofs + lax.broadcasted_iota(jnp.int32, (nq, nk), 1)
    sh = int(math.log2(seq_len))
    same = lax.shift_right_arithmetic(qpos, sh) == lax.shift_right_arithmetic(kpos, sh)
    return same & (jnp.abs(kpos - qpos) <= BAND_HALF)


def _window(r0, r1, r2):
    return jnp.concatenate([r0[128 - BAND_HALF:128], r1[...], r2[0:BAND_HALF]], axis=0)


WIN = 128 + 2 * BAND_HALF


def _win_specs(width, col, nt):
    return [pl.BlockSpec((128, width), lambda t: (jnp.maximum(t - 1, 0), col)),
            pl.BlockSpec((128, width), lambda t: (t, col)),
            pl.BlockSpec((128, width), lambda t: (jnp.minimum(t + 1, nt - 1), col))]


def _attn_fwd(qk, v_src, v_col, seq_len, name):
    s = qk.shape[0]
    nt = s // 128
    dm = D_MODEL

    def body(q_ref, k0, k1, k2, v0, v1, v2, o_ref, lse_ref):
        t = pl.program_id(0)
        valid = _band_valid(t, 128, WIN, 0, -BAND_HALF, seq_len)
        q = q_ref[...]
        kc = _window(k0, k1, k2)
        vc = _window(v0, v1, v2)
        outs, lses = [], []
        for h in range(N_HEADS):
            sl = slice(h * HEAD_DIM, (h + 1) * HEAD_DIM)
            sc = lax.dot_general(q[:, sl], kc[:, sl], NT, preferred_element_type=F32) * (HEAD_DIM ** -0.5)
            sc = jnp.where(valid, sc, NEG)
            m = jnp.max(sc, axis=1, keepdims=True)
            e = jnp.exp(sc - m)
            den = jnp.sum(e, axis=1, keepdims=True)
            o = jnp.dot(e.astype(BF16), vc[:, sl], preferred_element_type=F32) / den
            outs.append(o)
            lses.append(m + jnp.log(den))
        o_ref[...] = jnp.concatenate(outs, axis=1).astype(o_ref.dtype)
        lse_ref[...] = jnp.concatenate(lses, axis=1)

    in_specs = [pl.BlockSpec((128, dm), lambda t: (t, 0))] + _win_specs(dm, 1, nt) + _win_specs(dm, v_col, nt)
    return pl.pallas_call(
        body, grid=(nt,), in_specs=in_specs,
        out_specs=[pl.BlockSpec((128, dm), lambda t: (t, 0)), pl.BlockSpec((128, N_HEADS), lambda t: (t, 0))],
        out_shape=[SDS((s, dm), BF16), SDS((s, N_HEADS), F32)], name=name, compiler_params=_cp("parallel"))(
            qk, qk, qk, qk, v_src, v_src, v_src)


def _attn_combine(os_, lses, expand, name):
    s, dm = os_[0].shape
    tm = _pick(s, (256, 128))

    def body(o1, o2, o3, l1, l2, l3, e_ref, out_ref, lt_ref):
        ls = [l1[...], l2[...], l3[...]]
        m = jnp.maximum(jnp.maximum(ls[0], ls[1]), ls[2])
        es = [jnp.exp(l - m) for l in ls]
        tot = es[0] + es[1] + es[2]
        lt_ref[...] = m + jnp.log(tot)
        acc = jnp.zeros((tm, dm), F32)
        for e, o in zip(es, (o1, o2, o3)):
            acc = acc + jnp.dot(e / tot, e_ref[...], precision=HIGH, preferred_element_type=F32) * o[...].astype(F32)
        out_ref[...] = acc.astype(out_ref.dtype)

    row = pl.BlockSpec((tm, dm), lambda i: (i, 0))
    st = pl.BlockSpec((tm, N_HEADS), lambda i: (i, 0))
    return pl.pallas_call(
        body, grid=(s // tm,), in_specs=[row, row, row, st, st, st, pl.BlockSpec((N_HEADS, dm), lambda i: (0, 0))],
        out_specs=[row, st], out_shape=[SDS((s, dm + D_INNER), BF16), SDS((s, N_HEADS), F32)],
        name=name, compiler_params=_cp("parallel"))(*os_, *lses, expand)


def _attn_delta(dmix, attn, expand_t, name):
    s, dm = attn.shape[0], D_MODEL
    tm = _pick(s, (256, 128))

    def body(d_ref, a_ref, e_ref, o_ref):
        prod = d_ref[...].astype(F32) * a_ref[...].astype(F32)
        o_ref[...] = jnp.dot(prod, e_ref[...], precision=HIGH, preferred_element_type=F32)

    row = pl.BlockSpec((tm, dm), lambda i: (i, 0))
    return pl.pallas_call(
        body, grid=(s // tm,), in_specs=[row, row, pl.BlockSpec((dm, N_HEADS), lambda i: (0, 0))],
        out_specs=pl.BlockSpec((tm, N_HEADS), lambda i: (i, 0)), out_shape=SDS((s, N_HEADS), F32),
        name=name, compiler_params=_cp("parallel"))(dmix, attn, expand_t)


def _attn_bwd_dq(qk, v_src, v_col, do_src, lse, delta, seq_len, name):
    s = qk.shape[0]
    nt = s // 128
    dm = D_MODEL

    def body(q_ref, k0, k1, k2, v0, v1, v2, do_ref, lse_ref, dl_ref, dq_ref):
        t = pl.program_id(0)
        valid = _band_valid(t, 128, WIN, 0, -BAND_HALF, seq_len)
        q = q_ref[...]
        do = do_ref[...]
        kc = _window(k0, k1, k2)
        vc = _window(v0, v1, v2)
        lse_v, dl_v = lse_ref[...], dl_ref[...]
        outs = []
        for h in range(N_HEADS):
            sl = slice(h * HEAD_DIM, (h + 1) * HEAD_DIM)
            sc = lax.dot_general(q[:, sl], kc[:, sl], NT, preferred_element_type=F32) * (HEAD_DIM ** -0.5)
            p = jnp.exp(jnp.where(valid, sc - lse_v[:, h:h + 1], NEG))
            dp = lax.dot_general(do[:, sl], vc[:, sl], NT, preferred_element_type=F32)
            ds = p * (dp - dl_v[:, h:h + 1])
            outs.append(jnp.dot(ds.astype(BF16), kc[:, sl], preferred_element_type=F32) * (HEAD_DIM ** -0.5))
        dq_ref[...] = jnp.concatenate(outs, axis=1).astype(dq_ref.dtype)

    row = pl.BlockSpec((128, dm), lambda t: (t, 0))
    st = pl.BlockSpec((128, N_HEADS), lambda t: (t, 0))
    in_specs = [row] + _win_specs(dm, 1, nt) + _win_specs(dm, v_col, nt) + [row, st, st]
    return pl.pallas_call(
        body, grid=(nt,), in_specs=in_specs, out_specs=row, out_shape=SDS((s, dm), BF16),
        name=name, compiler_params=_cp("parallel"))(qk, qk, qk, qk, v_src, v_src, v_src, do_src, lse, delta)


def _attn_bwd_dkv(qk, v_src, v_col, do_src, lse, delta, seq_len, name):
    s = qk.shape[0]
    nt = s // 128
    dm = D_MODEL

    def body(k_ref, v_ref, q0, q1, q2, d0, d1, d2, l0, l1, l2, e0, e1, e2, dk_ref, dv_ref):
        t = pl.program_id(0)
        valid = _band_valid(t, WIN, 128, -BAND_HALF, 0, seq_len)
        k = k_ref[...]
        v = v_ref[...]
        qc = _window(q0, q1, q2)
        dc = _window(d0, d1, d2)
        lse_v = _window(l0, l1, l2)
        dl_v = _window(e0, e1, e2)
        dks, dvs = [], []
        for h in range(N_HEADS):
            sl = slice(h * HEAD_DIM, (h + 1) * HEAD_DIM)
            sc = lax.dot_general(qc[:, sl], k[:, sl], NT, preferred_element_type=F32) * (HEAD_DIM ** -0.5)
            p = jnp.exp(jnp.where(valid, sc - lse_v[:, h:h + 1], NEG))
            dvs.append(lax.dot_general(p.astype(BF16), dc[:, sl], TN, preferred_element_type=F32))
            dp = lax.dot_general(dc[:, sl], v[:, sl], NT, preferred_element_type=F32)
            ds = p * (dp - dl_v[:, h:h + 1])
            dks.append(lax.dot_general(ds.astype(BF16), qc[:, sl], TN, preferred_element_type=F32) * (HEAD_DIM ** -0.5))
        dk_ref[...] = jnp.concatenate(dks, axis=1).astype(dk_ref.dtype)
        dv_ref[...] = jnp.concatenate(dvs, axis=1).astype(dv_ref.dtype)

    row = pl.BlockSpec((128, dm), lambda t: (t, 0))
    in_specs = ([pl.BlockSpec((128, dm), lambda t: (t, 1)), pl.BlockSpec((128, dm), lambda t: (t, v_col))]
                + _win_specs(dm, 0, nt) + _win_specs(dm, 0, nt) + _win_specs(N_HEADS, 0, nt) + _win_specs(N_HEADS, 0, nt))
    return pl.pallas_call(
        body, grid=(nt,), in_specs=in_specs, out_specs=[row, row], out_shape=[SDS((s, dm), BF16), SDS((s, dm), BF16)],
        name=name, compiler_params=_cp("parallel"))(qk, v_src, qk, qk, qk, do_src, do_src, do_src, lse, lse, lse, delta, delta, delta)


CONV_COLS = (1024, 1408, 512, 256)


def _halo_specs(tm, tc, col0, nrow_blocks):
    r = tm // 16
    return [pl.BlockSpec((16, tc), lambda i, j: (jnp.maximum(i * r - 1, 0), col0 + j)),
            pl.BlockSpec((16, tc), lambda i, j: (jnp.minimum((i + 1) * r, nrow_blocks * r - 1), col0 + j))]


def _shifted(x_ref, hp_ref, hn_ref, i, last):
    x = x_ref[...].astype(F32)
    tm = x.shape[0]
    rows = lax.broadcasted_iota(jnp.int32, x.shape, 0)
    prev_row = jnp.where(i > 0, hp_ref[15:16, :].astype(F32), 0.0)
    next_row = jnp.where(i < last, hn_ref[0:1, :].astype(F32), 0.0)
    xp = jnp.where(rows == 0, prev_row, pltpu.roll(x, 1, 0))
    xn = jnp.where(rows == tm - 1, next_row, pltpu.roll(x, tm - 1, 0))
    return xp, x, xn


def _conv(x_src, col0, width, w3, bias, act, name, out_dtype=BF16, slab=None):
    s = x_src.shape[0]
    tm = _pick(s, (256, 128))
    tc = _pick(width, CONV_COLS)
    nb = s // tm
    c0 = col0 // tc

    def body(*refs):
        x_ref, hp_ref, hn_ref, w_ref, b_ref = refs[:5]
        o_ref = refs[-1]
        i = pl.program_id(0)
        xp, x, xn = _shifted(x_ref, hp_ref, hn_ref, i, nb - 1)
        w = w_ref[...]
        y = w[0:1, :] * xp + w[1:2, :] * x + w[2:3, :] * xn + b_ref[...]
        if act:
            y = y * _sigmoid(y)
        o_ref[...] = y.astype(o_ref.dtype)

    in_specs = ([pl.BlockSpec((tm, tc), lambda i, j: (i, c0 + j))] + _halo_specs(tm, tc, c0, nb)
                + [pl.BlockSpec((3, tc), lambda i, j: (0, j)), pl.BlockSpec((1, tc), lambda i, j: (0, j))])
    out_shape, ocol, more, more_specs, alias = _slab(s, width, out_dtype, slab, 5)
    ob = ocol // tc
    return pl.pallas_call(
        body, grid=(nb, width // tc), in_specs=in_specs + more_specs, out_specs=pl.BlockSpec((tm, tc), lambda i, j: (i, ob + j)),
        out_shape=out_shape, input_output_aliases=alias, name=name, compiler_params=_cp("parallel", "parallel"))(
            x_src, x_src, x_src, w3, bias, *more)


def _conv_silu_bwd(x_src, col0, width, w3, bias, addends, add_widths, name):
    s = x_src.shape[0]
    tm = _pick(s, (256, 128))
    tc = _pick(width, CONV_COLS)
    nb = s // tm
    c0 = col0 // tc
    na = len(addends)

    def body(*refs):
        x_ref, hp_ref, hn_ref, w_ref, b_ref = refs[:5]
        a_refs = refs[5:5 + na]
        dp_ref, dw_ref, db_ref = refs[5 + na:]
        i, j = pl.program_id(1), pl.program_id(0)
        xp, x, xn = _shifted(x_ref, hp_ref, hn_ref, i, nb - 1)
        w = w_ref[...]
        pre = w[0:1, :] * xp + w[1:2, :] * x + w[2:3, :] * xn + b_ref[...]
        g = jnp.zeros_like(pre)
        for a_ref, aw in zip(a_refs, add_widths):
            av = a_ref[...].astype(F32)
            g = g + (av if aw == width else jnp.where(j < aw // tc, av, 0.0))
        sg = _sigmoid(pre)
        dpre = g * (sg * (1.0 + pre * (1.0 - sg)))
        dp_ref[...] = dpre.astype(dp_ref.dtype)

        @pl.when(i == 0)
        def _():
            dw_ref[...] = jnp.zeros_like(dw_ref)
            db_ref[...] = jnp.zeros_like(db_ref)

        dw_ref[...] += jnp.concatenate([jnp.sum(dpre * xp, axis=0, keepdims=True), jnp.sum(dpre * x, axis=0, keepdims=True),
                                        jnp.sum(dpre * xn, axis=0, keepdims=True)], axis=0)
        db_ref[...] += jnp.sum(dpre, axis=0, keepdims=True)

    r = tm // 16
    in_specs = [pl.BlockSpec((tm, tc), lambda j, i: (i, c0 + j)),
                pl.BlockSpec((16, tc), lambda j, i: (jnp.maximum(i * r - 1, 0), c0 + j)),
                pl.BlockSpec((16, tc), lambda j, i: (jnp.minimum((i + 1) * r, nb * r - 1), c0 + j)),
                pl.BlockSpec((3, tc), lambda j, i: (0, j)), pl.BlockSpec((1, tc), lambda j, i: (0, j))]
    for aw in add_widths:
        nblk = aw // tc
        in_specs.append(pl.BlockSpec((tm, tc), lambda j, i, nblk=nblk: (i, jnp.minimum(j, nblk - 1))))
    return pl.pallas_call(
        body, grid=(width // tc, nb), in_specs=in_specs,
        out_specs=[pl.BlockSpec((tm, tc), lambda j, i: (i, j)), pl.BlockSpec((3, tc), lambda j, i: (0, j)), pl.BlockSpec((1, tc), lambda j, i: (0, j))],
        out_shape=[SDS((s, width), BF16), SDS((3, width), F32), SDS((1, width), F32)],
        name=name, compiler_params=_cp("parallel", "arbitrary"))(x_src, x_src, x_src, w3, bias, *addends)


def _ffn_gate_fwd(u, w3, bias, name):
    s = u.shape[0]
    tm = _pick(s, (256, 128))
    tc = _pick(D_FF, CONV_COLS)
    nb = s // tm
    nj = D_FF // tc

    def body(g_ref, gp, gn, u_ref, up, un, wg_ref, wu_ref, bg_ref, bu_ref, o_ref):
        i = pl.program_id(0)
        outs = []
        for (x_ref, hp, hn, w_ref, b_ref) in ((g_ref, gp, gn, wg_ref, bg_ref), (u_ref, up, un, wu_ref, bu_ref)):
            xp, x, xn = _shifted(x_ref, hp, hn, i, nb - 1)
            w = w_ref[...]
            outs.append(w[0:1, :] * xp + w[1:2, :] * x + w[2:3, :] * xn + b_ref[...])
        gate, upv = outs
        o_ref[...] = (gate * _sigmoid(gate) * upv).astype(o_ref.dtype)

    def xspecs(c0):
        return [pl.BlockSpec((tm, tc), lambda i, j: (i, c0 + j))] + _halo_specs(tm, tc, c0, nb)

    in_specs = (xspecs(0) + xspecs(nj)
                + [pl.BlockSpec((3, tc), lambda i, j: (0, j)), pl.BlockSpec((3, tc), lambda i, j: (0, nj + j)),
                   pl.BlockSpec((1, tc), lambda i, j: (0, j)), pl.BlockSpec((1, tc), lambda i, j: (0, nj + j))])
    return pl.pallas_call(
        body, grid=(nb, nj), in_specs=in_specs, out_specs=pl.BlockSpec((tm, tc), lambda i, j: (i, j)),
        out_shape=SDS((s, D_FF), BF16), name=name, compiler_params=_cp("parallel", "parallel"))(u, u, u, u, u, u, w3, w3, bias, bias)


def _ffn_gate_bwd(u, w3, bias, dact, name):
    s = u.shape[0]
    tm = _pick(s, (256, 128))
    tc = _pick(D_FF, CONV_COLS)
    nb = s // tm
    nj = D_FF // tc

    def body(g_ref, gp, gn, u_ref, up, un, wg_ref, wu_ref, bg_ref, bu_ref, da_ref, dg_ref, du_ref, dwg_ref, dwu_ref, dbg_ref, dbu_ref):
        i = pl.program_id(1)
        sh, pre = [], []
        for (x_ref, hp, hn, w_ref, b_ref) in ((g_ref, gp, gn, wg_ref, bg_ref), (u_ref, up, un, wu_ref, bu_ref)):
            xs3 = _shifted(x_ref, hp, hn, i, nb - 1)
            w = w_ref[...]
            sh.append(xs3)
            pre.append(w[0:1, :] * xs3[0] + w[1:2, :] * xs3[1] + w[2:3, :] * xs3[2] + b_ref[...])
        gate, upv = pre
        da = da_ref[...].astype(F32)
        sg = _sigmoid(gate)
        dgate = da * upv * (sg * (1.0 + gate * (1.0 - sg)))
        dup = da * gate * sg
        dg_ref[...] = dgate.astype(dg_ref.dtype)
        du_ref[...] = dup.astype(du_ref.dtype)

        @pl.when(i == 0)
        def _():
            for r in (dwg_ref, dwu_ref, dbg_ref, dbu_ref):
                r[...] = jnp.zeros_like(r)

        for d, xs3, dw_ref, db_ref in ((dgate, sh[0], dwg_ref, dbg_ref), (dup, sh[1], dwu_ref, dbu_ref)):
            dw_ref[...] += jnp.concatenate([jnp.sum(d * xs3[0], axis=0, keepdims=True), jnp.sum(d * xs3[1], axis=0, keepdims=True),
                                            jnp.sum(d * xs3[2], axis=0, keepdims=True)], axis=0)
            db_ref[...] += jnp.sum(d, axis=0, keepdims=True)

    r = tm // 16

    def xspecs(c0):
        return [pl.BlockSpec((tm, tc), lambda j, i: (i, c0 + j)),
                pl.BlockSpec((16, tc), lambda j, i: (jnp.maximum(i * r - 1, 0), c0 + j)),
                pl.BlockSpec((16, tc), lambda j, i: (jnp.minimum((i + 1) * r, nb * r - 1), c0 + j))]

    in_specs = (xspecs(0) + xspecs(nj)
                + [pl.BlockSpec((3, tc), lambda j, i: (0, j)), pl.BlockSpec((3, tc), lambda j, i: (0, nj + j)),
                   pl.BlockSpec((1, tc), lambda j, i: (0, j)), pl.BlockSpec((1, tc), lambda j, i: (0, nj + j)),
                   pl.BlockSpec((tm, tc), lambda j, i: (i, j))])
    blk = pl.BlockSpec((tm, tc), lambda j, i: (i, j))
    w_o = pl.BlockSpec((3, tc), lambda j, i: (0, j))
    b_o = pl.BlockSpec((1, tc), lambda j, i: (0, j))
    return pl.pallas_call(
        body, grid=(nj, nb), in_specs=in_specs, out_specs=[blk, blk, w_o, w_o, b_o, b_o],
        out_shape=[SDS((s, D_FF), BF16), SDS((s, D_FF), BF16), SDS((3, D_FF), F32), SDS((3, D_FF), F32), SDS((1, D_FF), F32), SDS((1, D_FF), F32)],
        name=name, compiler_params=_cp("parallel", "arbitrary"))(u, u, u, u, u, u, w3, w3, bias, bias, dact)


def _ssd_common(dt_ref, dtt_ref, al_r, al_c, bi_r, bi_c, off, rev):
    li = lax.broadcasted_iota(jnp.int32, (CHUNK, CHUNK), 0)
    si = lax.broadcasted_iota(jnp.int32, (CHUNK, CHUNK), 1)
    mask = (li <= si) if rev else (li >= si)
    mask_t = (li >= si) if rev else (li <= si)
    a_r = -jnp.exp(al_r[...])
    a_c = -jnp.exp(al_c[...])
    pre = dt_ref[:, off:off + N_HEADS] + bi_r[...]
    dt = _softplus(pre)
    cs = jnp.dot(mask.astype(F32), dt * a_r, precision=HIGH, preferred_element_type=F32)
    dt_t = _softplus(dtt_ref[off:off + N_HEADS, :] + bi_c[...])
    cs_t = jnp.dot(dt_t * a_c, mask_t.astype(F32), precision=HIGH, preferred_element_type=F32)
    tot = cs[0:1, :] if rev else cs[CHUNK - 1:CHUNK, :]
    return mask, mask_t, a_r, pre, dt, cs, cs_t, tot


def _ssd_fwd(xbc, dt_raw, dt_t, al_r, al_c, bi_r, bi_c, off, rev, name):
    s = xbc.shape[0]
    nc = s // CHUNK
    hp = D_INNER // N_HEADS
    hpg = N_HEADS // N_GROUPS
    gs = N_GROUPS * D_STATE
    cm = (lambda c: nc - 1 - c) if rev else (lambda c: c)

    def body(x_ref, b_ref, c_ref, dt_ref, dtt_ref, alr, alc, bir, bic, y_ref, st_ref, h_scr):
        @pl.when(pl.program_id(0) == 0)
        def _():
            h_scr[...] = jnp.zeros_like(h_scr)

        mask, _, _, _, dt, cs, cs_t, tot = _ssd_common(dt_ref, dtt_ref, alr, alc, bir, bic, off, rev)
        xs = x_ref[...]
        ys = []
        for g in range(N_GROUPS):
            bg = b_ref[:, g * D_STATE:(g + 1) * D_STATE]
            cg = c_ref[:, g * D_STATE:(g + 1) * D_STATE]
            gm = lax.dot_general(cg, bg, NT, preferred_element_type=F32)
            hcat = h_scr[g]
            st_ref[0, g] = hcat
            ch = lax.dot_general(cg, hcat.astype(BF16), NT, preferred_element_type=F32)
            xdd = []
            for r in range(hpg):
                h = g * hpg + r
                csc, csr, tot_h = cs[:, h:h + 1], cs_t[h:h + 1, :], tot[:, h:h + 1]
                lm = jnp.exp(jnp.where(mask, csc - csr, NEG))
                xdf = xs[:, h * hp:(h + 1) * hp].astype(F32) * dt[:, h:h + 1]
                y = jnp.dot((gm * lm).astype(BF16), xdf.astype(BF16), preferred_element_type=F32)
                ys.append(y + jnp.exp(csc) * ch[:, r * hp:(r + 1) * hp])
                xdd.append((xdf * jnp.exp(tot_h - csc)).astype(BF16))
            snew = lax.dot_general(jnp.concatenate(xdd, axis=1), bg, TN, preferred_element_type=F32)
            for r in range(hpg):
                rs = slice(r * hp, (r + 1) * hp)
                h_scr[g, rs, :] = jnp.exp(tot[:, g * hpg + r:g * hpg + r + 1]) * hcat[rs] + snew[rs]
        y_ref[...] = jnp.concatenate(ys, axis=1)

    small = lambda shape: pl.BlockSpec(shape, lambda c: (0, 0))
    in_specs = [pl.BlockSpec((CHUNK, D_INNER), lambda c: (cm(c), 0)),
                pl.BlockSpec((CHUNK, gs), lambda c: (cm(c), D_INNER // gs)),
                pl.BlockSpec((CHUNK, gs), lambda c: (cm(c), D_INNER // gs + 1)),
                pl.BlockSpec((CHUNK, 128), lambda c: (cm(c), 0)),
                pl.BlockSpec((2 * N_HEADS, CHUNK), lambda c: (0, cm(c))),
                small((1, N_HEADS)), small((N_HEADS, 1)), small((1, N_HEADS)), small((N_HEADS, 1))]
    return pl.pallas_call(
        body, grid=(nc,), in_specs=in_specs,
        out_specs=[pl.BlockSpec((CHUNK, D_INNER), lambda c: (cm(c), 0)), pl.BlockSpec((1, N_GROUPS, hpg * hp, D_STATE), lambda c: (cm(c), 0, 0, 0))],
        out_shape=[SDS((s, D_INNER), F32), SDS((nc, N_GROUPS, hpg * hp, D_STATE), F32)],
        scratch_shapes=[pltpu.VMEM((N_GROUPS, hpg * hp, D_STATE), F32)], name=name, compiler_params=_cp("arbitrary"))(
            xbc, xbc, xbc, dt_raw, dt_t, al_r, al_c, bi_r, bi_c)


def _ssd_bwd(xbc, dt_raw, dt_t, al_r, al_c, bi_r, bi_c, states, dy, off, rev, name):
    s = xbc.shape[0]
    nc = s // CHUNK
    hp = D_INNER // N_HEADS
    gs = N_GROUPS * D_STATE
    hpg = N_HEADS // N_GROUPS
    cm = (lambda c: c) if rev else (lambda c: nc - 1 - c)

    def body(x_ref, b_ref, c_ref, dt_ref, dtt_ref, alr, alc, bir, bic, st_ref, dy_ref, dx_ref, ddt_ref, dal_ref, dbi_ref, dh_scr):
        @pl.when(pl.program_id(0) == 0)
        def _():
            dh_scr[...] = jnp.zeros_like(dh_scr)
            dal_ref[...] = jnp.zeros_like(dal_ref)
            dbi_ref[...] = jnp.zeros_like(dbi_ref)

        mask, mask_t, a_r, pre, dt, cs, cs_t, tot = _ssd_common(dt_ref, dtt_ref, alr, alc, bir, bic, off, rev)
        xs = x_ref[...]
        dyv = dy_ref[...]
        rows = lax.broadcasted_iota(jnp.int32, (CHUNK, 1), 0)
        end_row = (rows == 0) if rev else (rows == CHUNK - 1)
        lane_h = lax.broadcasted_iota(jnp.int32, (1, N_HEADS), 1)
        dcs_all = jnp.zeros((CHUNK, N_HEADS), F32)
        dxsum_all = jnp.zeros((CHUNK, N_HEADS), F32)
        dxs, dbs, dcs_out = [], [], []
        for g in range(N_GROUPS):
            bg = b_ref[:, g * D_STATE:(g + 1) * D_STATE]
            cg = c_ref[:, g * D_STATE:(g + 1) * D_STATE]
            gm = lax.dot_general(cg, bg, NT, preferred_element_type=F32)
            hcat = st_ref[0, g]
            dhcat = dh_scr[g]
            hb, dhb = hcat.astype(BF16), dhcat.astype(BF16)
            ch = lax.dot_general(cg, hb, NT, preferred_element_type=F32)
            z = lax.dot_general(bg, dhb, NT, preferred_element_type=F32)
            dg_sum = jnp.zeros((CHUNK, CHUNK), F32)
            dchs, xdds, t_hs = [], [], []
            for r in range(hpg):
                h = g * hpg + r
                rs = slice(r * hp, (r + 1) * hp)
                csc, csr, tot_h = cs[:, h:h + 1], cs_t[h:h + 1, :], tot[:, h:h + 1]
                lm = jnp.exp(jnp.where(mask, csc - csr, NEG))
                xh = xs[:, h * hp:(h + 1) * hp].astype(F32)
                dtc = dt[:, h:h + 1]
                xdf = xh * dtc
                xd = xdf.astype(BF16)
                dyh = dyv[:, h * hp:(h + 1) * hp]
                dyb = dyh.astype(BF16)
                m = gm * lm
                ecs = jnp.exp(csc)
                dec = jnp.exp(tot_h - csc)
                t_h = jnp.exp(tot_h)
                dxd = lax.dot_general(m.astype(BF16), dyb, TN, preferred_element_type=F32)
                dm = lax.dot_general(dyb, xd, NT, preferred_element_type=F32)
                dg_sum = dg_sum + dm * lm
                w = dm * m
                dcs = jnp.sum(dyh * (ecs * ch[:, rs]), axis=1, keepdims=True)
                dchs.append((ecs * dyh).astype(BF16))
                dxd = dxd + dec * z[:, rs]
                ddec = jnp.sum(xdf * z[:, rs], axis=1, keepdims=True) * dec
                xdds.append((xdf * dec).astype(BF16))
                dtot = (jnp.sum(jnp.sum(dhcat[rs] * hcat[rs], axis=1, keepdims=True), axis=0, keepdims=True) * t_h
                        + jnp.sum(ddec, axis=0, keepdims=True))
                t_hs.append(t_h)
                dcs = dcs + jnp.sum(w, axis=1, keepdims=True) - jnp.sum(w.T, axis=1, keepdims=True) - ddec
                dcs = dcs + jnp.where(end_row, dtot, 0.0)
                onehot = (lane_h == h).astype(F32)
                dcs_all = dcs_all + dcs * onehot
                dxsum_all = dxsum_all + jnp.sum(dxd * xh, axis=1, keepdims=True) * onehot
                dxs.append(dxd * dtc)
            dgs = dg_sum.astype(BF16)
            dchc = jnp.concatenate(dchs, axis=1)
            dcs_out.append(jnp.dot(dgs, bg, preferred_element_type=F32) + jnp.dot(dchc, hb, preferred_element_type=F32))
            dbs.append(lax.dot_general(dgs, cg, TN, preferred_element_type=F32)
                       + jnp.dot(jnp.concatenate(xdds, axis=1), dhb, preferred_element_type=F32))
            dh_in = lax.dot_general(dchc, cg, TN, preferred_element_type=F32)
            for r in range(hpg):
                rs = slice(r * hp, (r + 1) * hp)
                dh_scr[g, rs, :] = dh_in[rs] + t_hs[r] * dhcat[rs]
        dx_ref[...] = jnp.concatenate(dxs + dbs + dcs_out, axis=1)
        da = jnp.dot(mask_t.astype(F32), dcs_all, precision=HIGH, preferred_element_type=F32)
        dal_ref[...] += jnp.sum(da * dt, axis=0, keepdims=True) * a_r
        ddt_raw = (da * a_r + dxsum_all) * _sigmoid(pre)
        ddt_ref[...] = ddt_raw
        dbi_ref[...] += jnp.sum(ddt_raw, axis=0, keepdims=True)

    small = lambda shape: pl.BlockSpec(shape, lambda c: (0, 0))
    in_specs = [pl.BlockSpec((CHUNK, D_INNER), lambda c: (cm(c), 0)),
                pl.BlockSpec((CHUNK, gs), lambda c: (cm(c), D_INNER // gs)),
                pl.BlockSpec((CHUNK, gs), lambda c: (cm(c), D_INNER // gs + 1)),
                pl.BlockSpec((CHUNK, 128), lambda c: (cm(c), 0)),
                pl.BlockSpec((2 * N_HEADS, CHUNK), lambda c: (0, cm(c))),
                small((1, N_HEADS)), small((N_HEADS, 1)), small((1, N_HEADS)), small((N_HEADS, 1)),
                pl.BlockSpec((1, N_GROUPS, hpg * hp, D_STATE), lambda c: (cm(c), 0, 0, 0)),
                pl.BlockSpec((CHUNK, D_INNER), lambda c: (cm(c), 0))]
    return pl.pallas_call(
        body, grid=(nc,), in_specs=in_specs,
        out_specs=[pl.BlockSpec((CHUNK, XBC), lambda c: (cm(c), 0)), pl.BlockSpec((CHUNK, N_HEADS), lambda c: (cm(c), 0)),
                   small((1, N_HEADS)), small((1, N_HEADS))],
        out_shape=[SDS((s, XBC), F32), SDS((s, N_HEADS), F32), SDS((1, N_HEADS), F32), SDS((1, N_HEADS), F32)],
        scratch_shapes=[pltpu.VMEM((N_GROUPS, hpg * hp, D_STATE), F32)], name=name, compiler_params=_cp("arbitrary"))(
            xbc, xbc, xbc, dt_raw, dt_t, al_r, al_c, bi_r, bi_c, states, dy)


def _gate_fwd(yf, yb, xbc, proj, dskip_x, norm_w, mix, name):
    s = yf.shape[0]
    tm = _pick(s, (256, 128))
    gw = D_INNER // N_GROUPS
    zc = 3 * D_MODEL // D_INNER

    def body(yf_ref, yb_ref, x_ref, z_ref, d_ref, w_ref, _, o_ref):
        y = yf_ref[...] + yb_ref[...] + d_ref[...] * x_ref[...].astype(F32)
        z = z_ref[...].astype(F32)
        gt = y * (z * _sigmoid(z))
        outs = []
        for g in range(N_GROUPS):
            gg = gt[:, g * gw:(g + 1) * gw]
            outs.append(gg * lax.rsqrt(jnp.mean(gg * gg, axis=-1, keepdims=True) + EPS))
        o_ref[...] = (jnp.concatenate(outs, axis=1) * w_ref[...]).astype(o_ref.dtype)

    row = pl.BlockSpec((tm, D_INNER), lambda i: (i, 0))
    vec = pl.BlockSpec((1, D_INNER), lambda i: (0, 0))
    out_shape, ocol, more, more_specs, alias = _slab(s, D_INNER, BF16, (mix.shape[1], mix.shape[1] - D_INNER, mix), 6)
    return pl.pallas_call(
        body, grid=(s // tm,), in_specs=[row, row, row, pl.BlockSpec((tm, D_INNER), lambda i: (i, zc)), vec, vec] + more_specs,
        out_specs=pl.BlockSpec((tm, D_INNER), lambda i: (i, ocol // D_INNER)), out_shape=out_shape, input_output_aliases=alias,
        name=name, compiler_params=_cp("parallel"))(yf, yb, xbc, proj, dskip_x, norm_w, *more)


def _gate_bwd(yf, yb, xbc, proj, dskip_x, norm_w, dmix, dproj, name):
    s = yf.shape[0]
    tm = _pick(s, (256, 128))
    gw = D_INNER // N_GROUPS
    zc = 3 * D_MODEL // D_INNER

    def body(yf_ref, yb_ref, x_ref, z_ref, d_ref, w_ref, do_ref, _, dy_ref, dz_ref, dxs_ref, dw_ref, dd_ref):
        xf = x_ref[...].astype(F32)
        y = yf_ref[...] + yb_ref[...] + d_ref[...] * xf
        z = z_ref[...].astype(F32)
        sg = _sigmoid(z)
        sz = z * sg
        gt = y * sz
        do = do_ref[...].astype(F32)
        dgh = do * w_ref[...]
        ghs, dgts = [], []
        for g in range(N_GROUPS):
            gg = gt[:, g * gw:(g + 1) * gw]
            r = lax.rsqrt(jnp.mean(gg * gg, axis=-1, keepdims=True) + EPS)
            gh = gg * r
            dg = dgh[:, g * gw:(g + 1) * gw]
            ghs.append(gh)
            dgts.append(r * (dg - gh * jnp.mean(dg * gh, axis=-1, keepdims=True)))
        ghat = jnp.concatenate(ghs, axis=1)
        dgt = jnp.concatenate(dgts, axis=1)
        dy = dgt * sz
        dy_ref[...] = dy
        dz_ref[...] = (dgt * y * (sg * (1.0 + z * (1.0 - sg)))).astype(dz_ref.dtype)
        dxs_ref[...] = dy * d_ref[...]

        @pl.when(pl.program_id(0) == 0)
        def _():
            dw_ref[...] = jnp.zeros_like(dw_ref)
            dd_ref[...] = jnp.zeros_like(dd_ref)

        dw_ref[...] += jnp.sum(do * ghat, axis=0, keepdims=True)
        dd_ref[...] += jnp.sum(dy * xf, axis=0, keepdims=True)

    row = pl.BlockSpec((tm, D_INNER), lambda i: (i, 0))
    vec = pl.BlockSpec((1, D_INNER), lambda i: (0, 0))
    dz_shape, _, more, more_specs, alias = _slab(s, D_INNER, BF16, (dproj.shape[1], zc * D_INNER, dproj), 7, out_idx=1)
    return pl.pallas_call(
        body, grid=(s // tm,),
        in_specs=[row, row, row, pl.BlockSpec((tm, D_INNER), lambda i: (i, zc)), vec, vec, pl.BlockSpec((tm, D_INNER), lambda i: (i, 1))] + more_specs,
        out_specs=[row, pl.BlockSpec((tm, D_INNER), lambda i: (i, zc)), row, vec, vec],
        out_shape=[SDS((s, D_INNER), F32), dz_shape, SDS((s, D_INNER), F32), SDS((1, D_INNER), F32), SDS((1, D_INNER), F32)],
        input_output_aliases=alias, name=name, compiler_params=_cp("arbitrary"))(yf, yb, xbc, proj, dskip_x, norm_w, dmix, *more)


def _adamw(parts, w, m, v, name):
    r, c = w.shape
    tr = _pick(r, (256, 352, 128))

    def body(p_ref, w_ref, m_ref, v_ref, g_ref, d_ref, nm_ref, nv_ref):
        g = p_ref[0].astype(F32)
        for i in range(1, N_DEV):
            g = g + p_ref[i].astype(F32)
        mn = B1 * m_ref[...] + (1.0 - B1) * g
        vn = B2 * v_ref[...] + (1.0 - B2) * (g * g)
        m_hat = mn / (1.0 - B1 ** STEP)
        v_hat = vn / (1.0 - B2 ** STEP)
        g_ref[...] = g
        d_ref[...] = -LR * (m_hat / (jnp.sqrt(v_hat) + AEPS) + WD * w_ref[...])
        nm_ref[...] = mn
        nv_ref[...] = vn

    blk = pl.BlockSpec((tr, c), lambda i: (i, 0))
    return pl.pallas_call(
        body, grid=(r // tr,), in_specs=[pl.BlockSpec((N_DEV, tr, c), lambda i: (0, i, 0)), blk, blk, blk],
        out_specs=[blk, blk, blk, blk], out_shape=[SDS((r, c), F32)] * 4, name=name, compiler_params=_cp("parallel"))(parts, w, m, v)


def _sum_parts(parts, name):
    _, r, c = parts.shape

    def body(p_ref, o_ref):
        g = p_ref[0]
        for i in range(1, N_DEV):
            g = g + p_ref[i]
        o_ref[...] = g

    return pl.pallas_call(body, out_shape=SDS((r, c), F32), name=name)(parts)


def _adamw_small(g, w, m, v, name):
    def body(g_ref, w_ref, m_ref, v_ref, d_ref, nm_ref, nv_ref):
        gv = g_ref[...]
        mn = B1 * m_ref[...] + (1.0 - B1) * gv
        vn = B2 * v_ref[...] + (1.0 - B2) * (gv * gv)
        m_hat = mn / (1.0 - B1 ** STEP)
        v_hat = vn / (1.0 - B2 ** STEP)
        d_ref[...] = -LR * (m_hat / (jnp.sqrt(v_hat) + AEPS) + WD * w_ref[...])
        nm_ref[...] = mn
        nv_ref[...] = vn

    return pl.pallas_call(body, out_shape=[SDS(g.shape, F32)] * 3, name=name)(g, w, m, v)


def _my_index():
    return 4 * lax.axis_index("x") + 2 * lax.axis_index("y") + lax.axis_index("c")


def _all_gather(shards, name):
    n = len(shards)

    def body(*refs):
        srcs, outs = refs[:n], refs[n:2 * n]
        send_sems, recv_sems, local_sems = refs[2 * n:]
        x, y, c = lax.axis_index("x"), lax.axis_index("y"), lax.axis_index("c")
        sibling = (x, y, 1 - c)
        chips = [(1 - x, y), (x, 1 - y), (1 - x, 1 - y)]

        def idx(px, py, pc):
            return 4 * px + 2 * py + pc

        def copy(a, k, block, to, src=None):
            dst = outs[a].at[idx(*block)]
            return pltpu.make_async_remote_copy(
                src_ref=dst if src is None else src, dst_ref=dst,
                send_sem=send_sems.at[a * 7 + k], recv_sem=recv_sems.at[a * 7 + k], device_id=to, device_id_type=MESH)

        me = (x, y, c)
        locals_, first, passed = [], [], []
        for a in range(n):
            lc = pltpu.make_async_copy(srcs[a], outs[a].at[idx(*me)], local_sems.at[a])
            lc.start()
            locals_.append(lc)
            first.append(copy(a, 0, me, sibling, src=srcs[a]))
            first += [copy(a, 1 + j, me, (*chip, c), src=srcs[a]) for j, chip in enumerate(chips)]
        for cp in first:
            cp.start()
        for j, chip in enumerate(chips):
            for a in range(n):
                copy(a, 1 + j, (*chip, c), me).wait_recv()
                fw = copy(a, 4 + j, (*chip, c), sibling)
                fw.start()
                passed.append(fw)
        for a in range(n):
            copy(a, 0, sibling, me).wait_recv()
            for j, chip in enumerate(chips):
                copy(a, 4 + j, (*chip, 1 - c), me).wait_recv()
        for cp in first + passed:
            cp.wait_send()
        for lc in locals_:
            lc.wait()

    any_spec = pl.BlockSpec(memory_space=pl.ANY)
    return pl.pallas_call(
        body, in_specs=[any_spec] * n, out_specs=[any_spec] * n,
        out_shape=[SDS((N_DEV,) + s.shape, s.dtype) for s in shards],
        scratch_shapes=[pltpu.SemaphoreType.DMA((7 * n,)), pltpu.SemaphoreType.DMA((7 * n,)), pltpu.SemaphoreType.DMA((n,))],
        name=name)(*shards)


def _exchange(arrays, name):
    n = len(arrays)

    def body(*refs):
        srcs, outs = refs[:n], refs[n:2 * n]
        send_sems, recv_sems, local_sems = refs[2 * n:]
        x, y, c = lax.axis_index("x"), lax.axis_index("y"), lax.axis_index("c")
        me = 4 * x + 2 * y + c
        copies, locals_ = [], []
        for a in range(n):
            lc = pltpu.make_async_copy(srcs[a].at[me], outs[a].at[me], local_sems.at[a])
            lc.start()
            locals_.append(lc)
        for k in range(1, N_DEV):
            px, py, pc = x ^ ((k >> 2) & 1), y ^ ((k >> 1) & 1), c ^ (k & 1)
            peer = 4 * px + 2 * py + pc
            for a in range(n):
                cp = pltpu.make_async_remote_copy(
                    src_ref=srcs[a].at[peer], dst_ref=outs[a].at[me],
                    send_sem=send_sems.at[a * 7 + k - 1], recv_sem=recv_sems.at[a * 7 + k - 1],
                    device_id=(px, py, pc), device_id_type=MESH)
                cp.start()
                copies.append((cp, a, peer, k))
        for cp, a, peer, k in copies:
            pltpu.make_async_remote_copy(
                src_ref=srcs[a].at[peer], dst_ref=outs[a].at[peer],
                send_sem=send_sems.at[a * 7 + k - 1], recv_sem=recv_sems.at[a * 7 + k - 1],
                device_id=(x, y, c), device_id_type=MESH).wait_recv()
        for cp, _, _, _ in copies:
            cp.wait_send()
        for lc in locals_:
            lc.wait()

    any_spec = pl.BlockSpec(memory_space=pl.ANY)
    return pl.pallas_call(
        body, in_specs=[any_spec] * n, out_specs=[any_spec] * n,
        out_shape=[SDS(a.shape, a.dtype) for a in arrays],
        scratch_shapes=[pltpu.SemaphoreType.DMA((7 * n,)), pltpu.SemaphoreType.DMA((7 * n,)), pltpu.SemaphoreType.DMA((n,))],
        name=name)(*arrays)


def _to_pattern(t, d):
    if d == 1:
        return t
    s, w = t.shape
    return t.reshape(s // d, d, w).transpose(1, 0, 2).reshape(s, w)


def _from_pattern(t, d):
    if d == 1:
        return t
    s, w = t.shape
    return t.reshape(d, s // d, w).transpose(1, 0, 2).reshape(s, w)


def _pad_lanes(t, n):
    return jnp.pad(t, ((0, 0), (0, n - t.shape[1])))


def _local_step(x, target, p):
    s = x.shape[0]
    tabs_f, tabs_b = _rope_tables(s)
    expand = jnp.repeat(jnp.eye(N_HEADS, dtype=F32), HEAD_DIM, axis=1)
    w_main, w_dt = p["w_in"][:, :MAIN_W], _pad_lanes(p["w_in"][:, MAIN_W:], 128)
    al_r = {"f": p["a_log_f"], "b": p["a_log_b"]}
    bi_r = {"f": p["dt_bias_f"], "b": p["dt_bias_b"]}
    dskip_x = jnp.repeat(p["d_skip"], D_INNER // N_HEADS, axis=1)
    ssm_w3, ffn_w3 = p["ssm_conv_w"].T, p["ffn_conv_w"].T

    h1 = _rmsnorm_fwd(x, p["norm1_w"], "norm1_fwd")
    proj = _matmul(h1, w_main, name="in_proj")
    dt_raw = _matmul(h1, w_dt, name="in_proj_dt", out_dtype=F32)
    dt_t = dt_raw[:, :2 * N_HEADS].T
    qk = _rope_fwd(proj, tabs_f, "rope_fwd")
    v_col = 2
    pat = []
    for d in DILATIONS:
        qk_p = _to_pattern(qk, d)
        v_p = proj if d == 1 else _to_pattern(proj[:, 2 * D_MODEL:3 * D_MODEL], d)
        pat.append((qk_p, v_p, v_col if d == 1 else 0))
    os_, lses = [], []
    for d, (qk_p, v_p, vc) in zip(DILATIONS, pat):
        o_p, lse_p = _attn_fwd(qk_p, v_p, vc, s // d, f"attn_fwd_d{d}")
        os_.append(_from_pattern(o_p, d))
        lses.append(_from_pattern(lse_p, d))
    mix, lse_tot = _attn_combine(os_, lses, expand, "attn_combine")

    xbc = _conv(proj, 3 * D_MODEL + D_INNER, XBC, ssm_w3, p["ssm_conv_b"], True, "ssm_conv_fwd")
    col = lambda r: r.reshape(N_HEADS, 1)
    ssd_args = {k: (al_r[k], col(al_r[k]), bi_r[k], col(bi_r[k])) for k in ("f", "b")}
    yf, st_f = _ssd_fwd(xbc, dt_raw, dt_t, *ssd_args["f"], 0, False, "ssd_fwd_f")
    yb, st_b = _ssd_fwd(xbc, dt_raw, dt_t, *ssd_args["b"], N_HEADS, True, "ssd_fwd_b")
    mix = _gate_fwd(yf, yb, xbc, proj, dskip_x, p["ssm_norm_w"], mix, "ssm_gate_fwd")

    x2 =_matmul(mix, p["w_out"], name="out_proj", out_dtype=F32, residual=x)
    h2 = _rmsnorm_fwd(x2, p["norm2_w"], "norm2_fwd")
    u = _matmul(h2, p["w_up"], name="ffn_up")
    act = _ffn_gate_fwd(u, ffn_w3, p["ffn_conv_b"], "ffn_gate_fwd")
    x3 = _matmul(act, p["w_down"], name="ffn_down", out_dtype=F32, residual=x2)

    dx3, g_final, loss = _final_norm_loss(x3, p["final_norm_w"].reshape(1, D_MODEL), target, "final_norm_loss")
    dx3b = dx3.astype(BF16)
    g_w_down = _matmul(act.T, dx3b, name="dw_down")
    dact = _matmul(dx3b, p["w_down"], name="d_act", trans_b=True)
    dug, duu, dwg, dwu, dbg, dbu = _ffn_gate_bwd(u, ffn_w3, p["ffn_conv_b"], dact, "ffn_gate_bwd")
    flip3 = lambda w3: w3[::-1]
    zero_b = jnp.zeros((1, D_FF), F32)
    du = _conv(dug, 0, D_FF, flip3(ffn_w3[:, :D_FF]), zero_b, False, "ffn_conv_bwd_gate", slab=(2 * D_FF, 0, None))
    du = _conv(duu, 0, D_FF, flip3(ffn_w3[:, D_FF:]), zero_b, False, "ffn_conv_bwd_up", slab=(2 * D_FF, D_FF, du))
    g_w_up = _matmul(h2.T, du, name="dw_up")
    dh2 = _matmul(du, p["w_up"], name="d_h2", trans_b=True)
    dx2, g_norm2 = _rmsnorm_bwd(x2, p["norm2_w"], dh2, dx3, "norm2_bwd")
    dx2b = dx2.astype(BF16)
    g_w_out = _matmul(mix.T, dx2b, name="dw_out")
    dmix = _matmul(dx2b, p["w_out"], name="d_mix", trans_b=True)

    delta = _attn_delta(dmix, mix, expand.T, "attn_delta")
    dq_u, dk_u, dv_u = [], [], []
    for d, (qk_p, v_p, vc) in zip(DILATIONS, pat):
        do_p = dmix if d == 1 else _to_pattern(dmix[:, :D_MODEL], d)
        lse_p, dl_p = _to_pattern(lse_tot, d), _to_pattern(delta, d)
        dq = _attn_bwd_dq(qk_p, v_p, vc, do_p, lse_p, dl_p, s // d, f"attn_bwd_dq_d{d}")
        dk, dv = _attn_bwd_dkv(qk_p, v_p, vc, do_p, lse_p, dl_p, s // d, f"attn_bwd_dkv_d{d}")
        dq_u.append(_from_pattern(dq, d))
        dk_u.append(_from_pattern(dk, d))
        dv_u.append(_from_pattern(dv, d))
    dproj = _sum3_rope(*dq_u, tabs_b, "rope_bwd_q", slab=(MAIN_W, 0, None))
    dproj = _sum3_rope(*dk_u, tabs_b, "rope_bwd_k", slab=(MAIN_W, D_MODEL, dproj))
    dproj = _sum3_rope(*dv_u, None, "sum_dv", slab=(MAIN_W, 2 * D_MODEL, dproj))

    dy, dproj, dxs_skip, g_ssm_norm, g_dskip_lanes = _gate_bwd(yf, yb, xbc, proj, dskip_x, p["ssm_norm_w"], dmix, dproj, "ssm_gate_bwd")
    dxbc_f, ddt_f, g_al_f, g_bi_f = _ssd_bwd(xbc, dt_raw, dt_t, *ssd_args["f"], st_f, dy, 0, False, "ssd_bwd_f")
    dxbc_b, ddt_b, g_al_b, g_bi_b = _ssd_bwd(xbc, dt_raw, dt_t, *ssd_args["b"], st_b, dy, N_HEADS, True, "ssd_bwd_b")
    dpre, g_ssm_w3, g_ssm_cb = _conv_silu_bwd(proj, 3 * D_MODEL + D_INNER, XBC, ssm_w3, p["ssm_conv_b"],
                                              [dxbc_f, dxbc_b, dxs_skip], [XBC, XBC, D_INNER], "ssm_conv_bwd")
    dproj = _conv(dpre, 0, XBC, flip3(ssm_w3), jnp.zeros((1, XBC), F32), False, "ssm_conv_bwd_x",
                  slab=(MAIN_W, 3 * D_MODEL + D_INNER, dproj))

    ddt =_pad_lanes(jnp.concatenate([ddt_f, ddt_b], axis=1), 128).astype(BF16)
    h1t = h1.T
    g_w_main = _matmul(h1t, dproj, name="dw_in")
    g_w_dt = _matmul(h1t, ddt, name="dw_in_dt")
    dh1 = _matmul(dproj, w_main, name="d_h1", trans_b=True, out_dtype=F32)
    dh1 = _matmul(ddt, w_dt, name="d_h1_dt", trans_b=True, out_dtype=F32, residual=dh1)
    grad_x, g_norm1 = _rmsnorm_bwd(x, p["norm1_w"], dh1, dx2, "norm1_bwd")

    g_w_in = jnp.concatenate([g_w_main, g_w_dt[:, :2 * N_HEADS]], axis=1)
    g_dskip = jnp.sum(g_dskip_lanes.reshape(N_HEADS, D_INNER // N_HEADS), axis=1).reshape(1, N_HEADS)
    small = {
        "norm1_w": g_norm1, "ssm_conv_w": g_ssm_w3.T, "ssm_conv_b": g_ssm_cb, "a_log_f": g_al_f, "a_log_b": g_al_b,
        "dt_bias_f": g_bi_f, "dt_bias_b": g_bi_b, "d_skip": g_dskip, "ssm_norm_w": g_ssm_norm, "norm2_w": g_norm2,
        "ffn_conv_w": jnp.concatenate([dwg, dwu], axis=1).T, "ffn_conv_b": jnp.concatenate([dbg, dbu], axis=1), "final_norm_w": g_final,
    }
    big = {"w_in": g_w_in, "w_out": g_w_out, "w_up": g_w_up, "w_down": g_w_down}
    return loss[0, 0], grad_x, big, small


SMALL_ORDER = ("norm1_w", "ssm_conv_w", "ssm_conv_b", "a_log_f", "a_log_b", "dt_bias_f", "dt_bias_b", "d_skip",
               "ssm_norm_w", "norm2_w", "ffn_conv_w", "ffn_conv_b", "final_norm_w")
SHARDED_SMALL = ("ssm_conv_w", "ffn_conv_w")
BIG_ORDER = ("w_in", "w_out", "w_up", "w_down")


def _pack(vals):
    rows = []
    for v in vals:
        f = v.reshape(-1).astype(F32)
        n = -(-f.shape[0] // 128) * 128
        rows.append(jnp.pad(f, (0, n - f.shape[0])).reshape(-1, 128))
    out = jnp.concatenate(rows, axis=0)
    pad = -out.shape[0] % 8
    return jnp.pad(out, ((0, pad), (0, 0)))


def _unpack(packed, shapes):
    out, r = [], 0
    for shp in shapes:
        n = math.prod(shp)
        nr = -(-n // 128)
        out.append(packed[r:r + nr].reshape(-1)[:n].reshape(shp))
        r += nr
    return out


def kernel(x, norm1_w, w_in, ssm_conv_w, ssm_conv_b, a_log_f, a_log_b, dt_bias_f, dt_bias_b, d_skip, ssm_norm_w, w_out, norm2_w, w_up, ffn_conv_w, ffn_conv_b, w_down, final_norm_w, loss_target, m_norm1_w, m_w_in, m_ssm_conv_w, m_ssm_conv_b, m_a_log_f, m_a_log_b, m_dt_bias_f, m_dt_bias_b, m_d_skip, m_ssm_norm_w, m_w_out, m_norm2_w, m_w_up, m_ffn_conv_w, m_ffn_conv_b, m_w_down, m_final_norm_w, v_norm1_w, v_w_in, v_ssm_conv_w, v_ssm_conv_b, v_a_log_f, v_a_log_b, v_dt_bias_f, v_dt_bias_b, v_d_skip, v_ssm_norm_w, v_w_out, v_norm2_w, v_w_up, v_ffn_conv_w, v_ffn_conv_b, v_w_down, v_final_norm_w):
    w = dict(norm1_w=norm1_w, w_in=w_in, ssm_conv_w=ssm_conv_w, ssm_conv_b=ssm_conv_b, a_log_f=a_log_f, a_log_b=a_log_b,
             dt_bias_f=dt_bias_f, dt_bias_b=dt_bias_b, d_skip=d_skip, ssm_norm_w=ssm_norm_w, w_out=w_out, norm2_w=norm2_w,
             w_up=w_up, ffn_conv_w=ffn_conv_w, ffn_conv_b=ffn_conv_b, w_down=w_down, final_norm_w=final_norm_w)
    mo = dict(norm1_w=m_norm1_w, w_in=m_w_in, ssm_conv_w=m_ssm_conv_w, ssm_conv_b=m_ssm_conv_b, a_log_f=m_a_log_f, a_log_b=m_a_log_b,
              dt_bias_f=m_dt_bias_f, dt_bias_b=m_dt_bias_b, d_skip=m_d_skip, ssm_norm_w=m_ssm_norm_w, w_out=m_w_out, norm2_w=m_norm2_w,
              w_up=m_w_up, ffn_conv_w=m_ffn_conv_w, ffn_conv_b=m_ffn_conv_b, w_down=m_w_down, final_norm_w=m_final_norm_w)
    vo = dict(norm1_w=v_norm1_w, w_in=v_w_in, ssm_conv_w=v_ssm_conv_w, ssm_conv_b=v_ssm_conv_b, a_log_f=v_a_log_f, a_log_b=v_a_log_b,
              dt_bias_f=v_dt_bias_f, dt_bias_b=v_dt_bias_b, d_skip=v_d_skip, ssm_norm_w=v_ssm_norm_w, w_out=v_w_out, norm2_w=v_norm2_w,
              w_up=v_w_up, ffn_conv_w=v_ffn_conv_w, ffn_conv_b=v_ffn_conv_b, w_down=v_w_down, final_norm_w=v_final_norm_w)
    me = _my_index()

    shards = [w["w_in"][0].astype(BF16), w["w_out"][0].astype(BF16), w["w_up"][0].astype(BF16), w["w_down"][0].astype(BF16),
              _pack([w["ssm_conv_w"][0], w["ffn_conv_w"][0]])]
    g_in, g_out, g_up, g_down, g_conv = _all_gather(shards, "weights_all_gather")
    n_in, n_up = w_in.shape[2], w_up.shape[2]
    conv_rows = [_unpack(g_conv[i], [ssm_conv_w.shape[1:], ffn_conv_w.shape[1:]]) for i in range(N_DEV)]
    full = {
        "w_in": g_in.transpose(1, 0, 2).reshape(D_MODEL, N_DEV * n_in),
        "w_out": g_out.reshape(2 * D_MODEL, D_MODEL),
        "w_up": g_up.transpose(1, 0, 2).reshape(D_MODEL, N_DEV * n_up),
        "w_down": g_down.reshape(D_FF, D_MODEL),
        "ssm_conv_w": jnp.concatenate([c[0] for c in conv_rows], axis=0),
        "ffn_conv_w": jnp.concatenate([c[1] for c in conv_rows], axis=0),
    }
    for k in ("norm1_w", "ssm_conv_b", "a_log_f", "a_log_b", "dt_bias_f", "dt_bias_b", "d_skip", "ssm_norm_w", "norm2_w", "ffn_conv_b"):
        full[k] = w[k]
    full["final_norm_w"] = w["final_norm_w"]

    loss_part, grad_x, big, small = _local_step(x[0], loss_target[0], full)

    out_big = [big["w_in"].reshape(D_MODEL, N_DEV, n_in).transpose(1, 0, 2),
               big["w_out"].reshape(N_DEV, 2 * D_MODEL // N_DEV, D_MODEL),
               big["w_up"].reshape(D_MODEL, N_DEV, n_up).transpose(1, 0, 2),
               big["w_down"].reshape(N_DEV, D_FF // N_DEV, D_MODEL)]
    small_shapes = [(1,)] + [small[k].shape for k in SMALL_ORDER]
    packed = _pack([loss_part] + [small[k] for k in SMALL_ORDER])
    out_small = jnp.broadcast_to(packed[None], (N_DEV,) + packed.shape)
    r_in, r_out, r_up, r_down, r_small = _exchange(out_big + [out_small], "grads_exchange")

    outs_g, outs_d, outs_m, outs_v = {}, {}, {}, {}
    for k, parts in zip(BIG_ORDER, (r_in, r_out, r_up, r_down)):
        g, dlt, nm, nv = _adamw(parts, w[k][0], mo[k][0], vo[k][0], f"adamw_{k}")
        outs_g[k], outs_d[k], outs_m[k], outs_v[k] = g[None], dlt[None], nm[None], nv[None]
    tot = _unpack(_sum_parts(r_small, "small_grads_sum"), small_shapes)
    loss = tot[0][0]
    gs = dict(zip(SMALL_ORDER, tot[1:]))
    g_own = {}
    for k in SMALL_ORDER:
        if k in SHARDED_SMALL:
            rows = w[k].shape[1]
            g_own[k] = lax.dynamic_slice_in_dim(gs[k], me * rows, rows, axis=0)[None]
        else:
            g_own[k] = gs[k].reshape(w[k].shape)
    shapes = [w[k].shape for k in SMALL_ORDER]
    d_s, m_s, v_s = _adamw_small(_pack([g_own[k] for k in SMALL_ORDER]), _pack([w[k] for k in SMALL_ORDER]),
                                 _pack([mo[k] for k in SMALL_ORDER]), _pack([vo[k] for k in SMALL_ORDER]), "adamw_small")
    for k, a, b, c in zip(SMALL_ORDER, _unpack(d_s, shapes), _unpack(m_s, shapes), _unpack(v_s, shapes)):
        outs_g[k], outs_d[k], outs_m[k], outs_v[k] = g_own[k], a, b, c

    order = ("norm1_w", "w_in", "ssm_conv_w", "ssm_conv_b", "a_log_f", "a_log_b", "dt_bias_f", "dt_bias_b", "d_skip", "ssm_norm_w",
             "w_out", "norm2_w", "w_up", "ffn_conv_w", "ffn_conv_b", "w_down", "final_norm_w")
    return (loss, grad_x[None], *[outs_g[k] for k in order], *[outs_d[k] for k in order],
            *[outs_m[k] for k in order], *[outs_v[k] for k in order])
```

```python
import math

import jax
import jax.numpy as jnp
from jax import lax
from jax.experimental import pallas as pl
from jax.experimental.pallas import tpu as pltpu

F32 = jnp.float32
BF16 = jnp.bfloat16
SDS = jax.ShapeDtypeStruct

N_DEV = 8
D_MODEL = 1024
N_HEADS = 16
HEAD_DIM = 64
ROPE_DIM = 16
ROPE_THETA = 500000.0
DILATIONS = (1, 4, 16)
BAND_HALF = 64
D_INNER = 1024
N_GROUPS = 4
D_STATE = 128
CHUNK = 128
XBC = D_INNER + 2 * N_GROUPS * D_STATE
D_FF = 2816
MAIN_W = 3 * D_MODEL + D_INNER + XBC
EPS = 1e-6
LR, B1, B2, AEPS, WD, STEP = 0.001, 0.9, 0.999, 1e-08, 0.01, 10
NEG = -1e30
VMEM_LIMIT = 56 * 1024 * 1024
MESH = pl.DeviceIdType.MESH
HIGH = lax.Precision.HIGHEST
NT = (((1,), (1,)), ((), ()))
TN = (((0,), (0,)), ((), ()))


def _cp(*sem):
    return pltpu.CompilerParams(dimension_semantics=sem, vmem_limit_bytes=VMEM_LIMIT)


def _pick(n, cands):
    for c in cands:
        if n % c == 0:
            return c
    raise ValueError(f"no tile for {n}")


def _sigmoid(x):
    return 1.0 / (1.0 + jnp.exp(-x))


def _softplus(x):
    return jnp.maximum(x, 0.0) + jnp.log1p(jnp.exp(-jnp.abs(x)))


def _slab(s, width, dtype, slab, n_in, out_idx=0):
    if slab is None:
        return SDS((s, width), dtype), 0, [], [], {}
    total, col0, into = slab
    if into is None:
        return SDS((s, total), dtype), col0, [], [], {}
    return SDS((s, total), dtype), col0, [into], [pl.BlockSpec(memory_space=pl.ANY)], {n_in: out_idx}


def _matmul(a, b, *, name, trans_b=False, out_dtype=BF16, residual=None):
    m, k = a.shape
    n = b.shape[0] if trans_b else b.shape[1]
    tk = k if k <= 2048 else _pick(k, (2048, 1408, 1024, 512))
    nk = k // tk
    if nk == 1:
        tm = _pick(m, (2048, 1408, 1024, 512, 256, 128))
        tn = _pick(n, (512, 256, 128))
    else:
        tm = _pick(m, (1024, 1408, 512, 256, 128))
        tn = _pick(n, (1024, 1408, 512, 256, 128))
    dn = NT if trans_b else (((1,), (0,)), ((), ()))

    def body(*refs):
        a_ref, b_ref = refs[0], refs[1]
        o_ref, acc = refs[-2], refs[-1]
        kk = pl.program_id(2)

        @pl.when(kk == 0)
        def _():
            acc[...] = jnp.zeros_like(acc)

        acc[...] += lax.dot_general(a_ref[...], b_ref[...], dn, preferred_element_type=F32)

        @pl.when(kk == nk - 1)
        def _():
            r = acc[...]
            if residual is not None:
                r = r + refs[2][...].astype(F32)
            o_ref[...] = r.astype(o_ref.dtype)

    in_specs = [pl.BlockSpec((tm, tk), lambda i, j, kk: (i, kk)),
                pl.BlockSpec((tn, tk), lambda i, j, kk: (j, kk)) if trans_b else pl.BlockSpec((tk, tn), lambda i, j, kk: (kk, j))]
    args = [a, b]
    if residual is not None:
        in_specs.append(pl.BlockSpec((tm, tn), lambda i, j, kk: (i, j)))
        args.append(residual)
    return pl.pallas_call(
        body, grid=(m // tm, n // tn, nk), in_specs=in_specs,
        out_specs=pl.BlockSpec((tm, tn), lambda i, j, kk: (i, j)),
        out_shape=SDS((m, n), out_dtype), scratch_shapes=[pltpu.VMEM((tm, tn), F32)],
        name=name, compiler_params=_cp("parallel", "parallel", "arbitrary"))(*args)


def _rmsnorm_fwd(x, w, name):
    s, d = x.shape
    tm = _pick(s, (512, 128))

    def body(x_ref, w_ref, o_ref):
        xf = x_ref[...]
        r = lax.rsqrt(jnp.mean(xf * xf, axis=-1, keepdims=True) + EPS)
        o_ref[...] = (xf * r * w_ref[...]).astype(o_ref.dtype)

    return pl.pallas_call(
        body, grid=(s // tm,), in_specs=[pl.BlockSpec((tm, d), lambda i: (i, 0)), pl.BlockSpec((1, d), lambda i: (0, 0))],
        out_specs=pl.BlockSpec((tm, d), lambda i: (i, 0)), out_shape=SDS((s, d), BF16),
        name=name, compiler_params=_cp("parallel"))(x, w)


def _rmsnorm_bwd(x, w, dh, dres, name):
    s, d = x.shape
    tm = _pick(s, (512, 128))

    def body(x_ref, w_ref, dh_ref, dres_ref, dx_ref, dw_ref):
        xf = x_ref[...]
        r = lax.rsqrt(jnp.mean(xf * xf, axis=-1, keepdims=True) + EPS)
        xhat = xf * r
        dhf = dh_ref[...].astype(F32)
        g = dhf * w_ref[...]
        dx_ref[...] = dres_ref[...] + r * (g - xhat * jnp.mean(g * xhat, axis=-1, keepdims=True))

        @pl.when(pl.program_id(0) == 0)
        def _():
            dw_ref[...] = jnp.zeros_like(dw_ref)

        dw_ref[...] += jnp.sum(dhf * xhat, axis=0, keepdims=True)

    row = pl.BlockSpec((tm, d), lambda i: (i, 0))
    vec = pl.BlockSpec((1, d), lambda i: (0, 0))
    return pl.pallas_call(
        body, grid=(s // tm,), in_specs=[row, vec, row, row], out_specs=[row, vec],
        out_shape=[SDS((s, d), F32), SDS((1, d), F32)], name=name, compiler_params=_cp("arbitrary"))(x, w, dh, dres)


def _final_norm_loss(x, w, target, name):
    s, d = x.shape
    tm = _pick(s, (512, 128))

    def body(x_ref, w_ref, t_ref, dx_ref, dw_ref, loss_ref):
        xf = x_ref[...]
        r = lax.rsqrt(jnp.mean(xf * xf, axis=-1, keepdims=True) + EPS)
        xhat = xf * r
        wv = w_ref[...]
        e = xhat * wv - t_ref[...]
        dy = e * (1.0 / d)
        g = dy * wv
        dx_ref[...] = r * (g - xhat * jnp.mean(g * xhat, axis=-1, keepdims=True))

        @pl.when(pl.program_id(0) == 0)
        def _():
            dw_ref[...] = jnp.zeros_like(dw_ref)
            loss_ref[...] = jnp.zeros_like(loss_ref)

        dw_ref[...] += jnp.sum(dy * xhat, axis=0, keepdims=True)
        loss_ref[...] += jnp.sum(jnp.sum(e * e, axis=1, keepdims=True), axis=0, keepdims=True) * (0.5 / d)

    row = pl.BlockSpec((tm, d), lambda i: (i, 0))
    vec = pl.BlockSpec((1, d), lambda i: (0, 0))
    return pl.pallas_call(
        body, grid=(s // tm,), in_specs=[row, vec, row], out_specs=[row, vec, pl.BlockSpec((1, 128), lambda i: (0, 0))],
        out_shape=[SDS((s, d), F32), SDS((1, d), F32), SDS((1, 128), F32)], name=name, compiler_params=_cp("arbitrary"))(x, w, target)


def _rope_tables(s):
    half = ROPE_DIM // 2
    inv_freq = jnp.power(ROPE_THETA, -jnp.arange(half, dtype=F32) * 2.0 / ROPE_DIM)
    ang = jnp.arange(s, dtype=F32)[:, None] * inv_freq[None, :]
    cos, sin = jnp.cos(ang), jnp.sin(ang)
    z = jnp.zeros((s, HEAD_DIM - ROPE_DIM), F32)
    zh = jnp.zeros((s, half), F32)
    c = jnp.concatenate([cos, cos, z + 1.0], axis=1)
    sa = jnp.concatenate([zh, sin, z], axis=1)
    sb = jnp.concatenate([-sin, zh, z], axis=1)
    two = lambda t: jnp.concatenate([t, t], axis=1)
    c, sa, sb = two(c), two(sa), two(sb)
    return (c, sa, sb), (c, jnp.roll(sb, half, axis=1), jnp.roll(sa, -half, axis=1))


def _rope_fwd(proj, tabs, name):
    s = proj.shape[0]
    tm = _pick(s, (256, 128))
    half = ROPE_DIM // 2
    wb = D_MODEL

    def body(x_ref, c_ref, sa_ref, sb_ref, o_ref):
        c, sa, sb = c_ref[...], sa_ref[...], sb_ref[...]
        for j in range(wb // 128):
            x = x_ref[:, j * 128:(j + 1) * 128].astype(F32)
            o_ref[:, j * 128:(j + 1) * 128] = (x * c + pltpu.roll(x, half, 1) * sa + pltpu.roll(x, 128 - half, 1) * sb).astype(o_ref.dtype)

    blk = pl.BlockSpec((tm, wb), lambda i, j: (i, j))
    tab = pl.BlockSpec((tm, 128), lambda i, j: (i, 0))
    return pl.pallas_call(
        body, grid=(s // tm, 2 * D_MODEL // wb), in_specs=[blk, tab, tab, tab], out_specs=blk,
        out_shape=SDS((s, 2 * D_MODEL), BF16), name=name, compiler_params=_cp("parallel", "parallel"))(proj, *tabs)


def _sum3_rope(d1, d2, d3, tabs, name, slab=None):
    s, w = d1.shape
    tm = _pick(s, (256, 128))
    half = ROPE_DIM // 2

    def body(*refs):
        for j in range(w // 128):
            ls = slice(j * 128, (j + 1) * 128)
            x = refs[0][:, ls].astype(F32) + refs[1][:, ls].astype(F32) + refs[2][:, ls].astype(F32)
            if tabs is not None:
                x = x * refs[3][...] + pltpu.roll(x, half, 1) * refs[4][...] + pltpu.roll(x, 128 - half, 1) * refs[5][...]
            refs[-1][:, ls] = x.astype(refs[-1].dtype)

    blk = pl.BlockSpec((tm, w), lambda i: (i, 0))
    tab = pl.BlockSpec((tm, 128), lambda i: (i, 0))
    extra = [] if tabs is None else list(tabs)
    out_shape, col0, more, more_specs, alias = _slab(s, w, BF16, slab, 3 + len(extra))
    cb = col0 // w
    return pl.pallas_call(
        body, grid=(s // tm,), in_specs=[blk, blk, blk] + [tab] * len(extra) + more_specs,
        out_specs=pl.BlockSpec((tm, w), lambda i: (i, cb)), out_shape=out_shape, input_output_aliases=alias,
        name=name, compiler_params=_cp("parallel"))(d1, d2, d3, *extra, *more)


def _band_valid(t, nq, nk, qofs, kofs, seq_len):
    qpos = t * 128 + qofs + lax.broadcasted_iota(jnp.int32, (nq, nk), 0)
    kpos = t * 128 + kofs + lax.broadcasted_iota(jnp.int32, (nq, nk), 1)
    sh = int(math.log2(seq_len))
    same = lax.shift_right_arithmetic(qpos, sh) == lax.shift_right_arithmetic(kpos, sh)
    return same & (jnp.abs(kpos - qpos) <= BAND_HALF)


def _window(r0, r1, r2):
    return jnp.concatenate([r0[128 - BAND_HALF:128], r1[...], r2[0:BAND_HALF]], axis=0)


WIN = 128 + 2 * BAND_HALF


def _win_specs(width, col, nt):
    return [pl.BlockSpec((128, width), lambda t: (jnp.maximum(t - 1, 0), col)),
            pl.BlockSpec((128, width), lambda t: (t, col)),
            pl.BlockSpec((128, width), lambda t: (jnp.minimum(t + 1, nt - 1), col))]


def _attn_fwd(qk, v_src, v_col, seq_len, name):
    s = qk.shape[0]
    nt = s // 128
    dm = D_MODEL

    def body(q_ref, k0, k1, k2, v0, v1, v2, o_ref, lse_ref):
        t = pl.program_id(0)
        valid = _band_valid(t, 128, WIN, 0, -BAND_HALF, seq_len)
        q = q_ref[...]
        kc = _window(k0, k1, k2)
        vc = _window(v0, v1, v2)
        outs, lses = [], []
        for h in range(N_HEADS):
            sl = slice(h * HEAD_DIM, (h + 1) * HEAD_DIM)
            sc = lax.dot_general(q[:, sl], kc[:, sl], NT, preferred_element_type=F32) * (HEAD_DIM ** -0.5)
            sc = jnp.where(valid, sc, NEG)
            m = jnp.max(sc, axis=1, keepdims=True)
            e = jnp.exp(sc - m)
            den = jnp.sum(e, axis=1, keepdims=True)
            o = jnp.dot(e.astype(BF16), vc[:, sl], preferred_element_type=F32) / den
            outs.append(o)
            lses.append(m + jnp.log(den))
        o_ref[...] = jnp.concatenate(outs, axis=1).astype(o_ref.dtype)
        lse_ref[...] = jnp.concatenate(lses, axis=1)

    in_specs = [pl.BlockSpec((128, dm), lambda t: (t, 0))] + _win_specs(dm, 1, nt) + _win_specs(dm, v_col, nt)
    return pl.pallas_call(
        body, grid=(nt,), in_specs=in_specs,
        out_specs=[pl.BlockSpec((128, dm), lambda t: (t, 0)), pl.BlockSpec((128, N_HEADS), lambda t: (t, 0))],
        out_shape=[SDS((s, dm), BF16), SDS((s, N_HEADS), F32)], name=name, compiler_params=_cp("parallel"))(
            qk, qk, qk, qk, v_src, v_src, v_src)


def _attn_combine(os_, lses, expand, name):
    s, dm = os_[0].shape
    tm = _pick(s, (256, 128))

    def body(o1, o2, o3, l1, l2, l3, e_ref, out_ref, lt_ref):
        ls = [l1[...], l2[...], l3[...]]
        m = jnp.maximum(jnp.maximum(ls[0], ls[1]), ls[2])
        es = [jnp.exp(l - m) for l in ls]
        tot = es[0] + es[1] + es[2]
        lt_ref[...] = m + jnp.log(tot)
        acc = jnp.zeros((tm, dm), F32)
        for e, o in zip(es, (o1, o2, o3)):
            acc = acc + jnp.dot(e / tot, e_ref[...], precision=HIGH, preferred_element_type=F32) * o[...].astype(F32)
        out_ref[...] = acc.astype(out_ref.dtype)

    row = pl.BlockSpec((tm, dm), lambda i: (i, 0))
    st = pl.BlockSpec((tm, N_HEADS), lambda i: (i, 0))
    return pl.pallas_call(
        body, grid=(s // tm,), in_specs=[row, row, row, st, st, st, pl.BlockSpec((N_HEADS, dm), lambda i: (0, 0))],
        out_specs=[row, st], out_shape=[SDS((s, dm + D_INNER), BF16), SDS((s, N_HEADS), F32)],
        name=name, compiler_params=_cp("parallel"))(*os_, *lses, expand)


def _attn_delta(dmix, attn, expand_t, name):
    s, dm = attn.shape[0], D_MODEL
    tm = _pick(s, (256, 128))

    def body(d_ref, a_ref, e_ref, o_ref):
        prod = d_ref[...].astype(F32) * a_ref[...].astype(F32)
        o_ref[...] = jnp.dot(prod, e_ref[...], precision=HIGH, preferred_element_type=F32)

    row = pl.BlockSpec((tm, dm), lambda i: (i, 0))
    return pl.pallas_call(
        body, grid=(s // tm,), in_specs=[row, row, pl.BlockSpec((dm, N_HEADS), lambda i: (0, 0))],
        out_specs=pl.BlockSpec((tm, N_HEADS), lambda i: (i, 0)), out_shape=SDS((s, N_HEADS), F32),
        name=name, compiler_params=_cp("parallel"))(dmix, attn, expand_t)


def _attn_bwd_dq(qk, v_src, v_col, do_src, lse, delta, seq_len, name):
    s = qk.shape[0]
    nt = s // 128
    dm = D_MODEL

    def body(q_ref, k0, k1, k2, v0, v1, v2, do_ref, lse_ref, dl_ref, dq_ref):
        t = pl.program_id(0)
        valid = _band_valid(t, 128, WIN, 0, -BAND_HALF, seq_len)
        q = q_ref[...]
        do = do_ref[...]
        kc = _window(k0, k1, k2)
        vc = _window(v0, v1, v2)
        lse_v, dl_v = lse_ref[...], dl_ref[...]
        outs = []
        for h in range(N_HEADS):
            sl = slice(h * HEAD_DIM, (h + 1) * HEAD_DIM)
            sc = lax.dot_general(q[:, sl], kc[:, sl], NT, preferred_element_type=F32) * (HEAD_DIM ** -0.5)
            p = jnp.exp(jnp.where(valid, sc - lse_v[:, h:h + 1], NEG))
            dp = lax.dot_general(do[:, sl], vc[:, sl], NT, preferred_element_type=F32)
            ds = p * (dp - dl_v[:, h:h + 1])
            outs.append(jnp.dot(ds.astype(BF16), kc[:, sl], preferred_element_type=F32) * (HEAD_DIM ** -0.5))
        dq_ref[...] = jnp.concatenate(outs, axis=1).astype(dq_ref.dtype)

    row = pl.BlockSpec((128, dm), lambda t: (t, 0))
    st = pl.BlockSpec((128, N_HEADS), lambda t: (t, 0))
    in_specs = [row] + _win_specs(dm, 1, nt) + _win_specs(dm, v_col, nt) + [row, st, st]
    return pl.pallas_call(
        body, grid=(nt,), in_specs=in_specs, out_specs=row, out_shape=SDS((s, dm), BF16),
        name=name, compiler_params=_cp("parallel"))(qk, qk, qk, qk, v_src, v_src, v_src, do_src, lse, delta)


def _attn_bwd_dkv(qk, v_src, v_col, do_src, lse_t, delta_t, seq_len, name):
    s = qk.shape[0]
    nt = s // 128
    dm = D_MODEL

    def lane_window(r0, r1, r2):
        return jnp.concatenate([r0[:, 128 - BAND_HALF:128], r1[...], r2[:, 0:BAND_HALF]], axis=1)

    def body(k_ref, v_ref, q0, q1, q2, d0, d1, d2, l0, l1, l2, e0, e1, e2, dk_ref, dv_ref):
        t = pl.program_id(0)
        valid = _band_valid(t, 128, WIN, 0, -BAND_HALF, seq_len)
        k = k_ref[...]
        v = v_ref[...]
        qc = _window(q0, q1, q2)
        dc = _window(d0, d1, d2)
        lse_v = lane_window(l0, l1, l2)
        dl_v = lane_window(e0, e1, e2)
        dks, dvs = [], []
        for h in range(N_HEADS):
            sl = slice(h * HEAD_DIM, (h + 1) * HEAD_DIM)
            sc = lax.dot_general(k[:, sl], qc[:, sl], NT, preferred_element_type=F32) * (HEAD_DIM ** -0.5)
            p = jnp.exp(jnp.where(valid, sc - lse_v[h:h + 1, :], NEG))
            dvs.append(jnp.dot(p.astype(BF16), dc[:, sl], preferred_element_type=F32))
            dp = lax.dot_general(v[:, sl], dc[:, sl], NT, preferred_element_type=F32)
            ds = p * (dp - dl_v[h:h + 1, :])
            dks.append(jnp.dot(ds.astype(BF16), qc[:, sl], preferred_element_type=F32) * (HEAD_DIM ** -0.5))
        dk_ref[...] = jnp.concatenate(dks, axis=1).astype(dk_ref.dtype)
        dv_ref[...] = jnp.concatenate(dvs, axis=1).astype(dv_ref.dtype)

    row = pl.BlockSpec((128, dm), lambda t: (t, 0))
    stat = [pl.BlockSpec((N_HEADS, 128), lambda t: (0, jnp.maximum(t - 1, 0))), pl.BlockSpec((N_HEADS, 128), lambda t: (0, t)),
            pl.BlockSpec((N_HEADS, 128), lambda t: (0, jnp.minimum(t + 1, nt - 1)))]
    in_specs = ([pl.BlockSpec((128, dm), lambda t: (t, 1)), pl.BlockSpec((128, dm), lambda t: (t, v_col))]
                + _win_specs(dm, 0, nt) + _win_specs(dm, 0, nt) + stat + stat)
    return pl.pallas_call(
        body, grid=(nt,), in_specs=in_specs, out_specs=[row, row], out_shape=[SDS((s, dm), BF16), SDS((s, dm), BF16)],
        name=name, compiler_params=_cp("parallel"))(qk, v_src, qk, qk, qk, do_src, do_src, do_src, lse_t, lse_t, lse_t, delta_t, delta_t, delta_t)


CONV_COLS = (1024, 1408, 512, 256)


def _halo_specs(tm, tc, col0, nrow_blocks):
    r = tm // 16
    return [pl.BlockSpec((16, tc), lambda i, j: (jnp.maximum(i * r - 1, 0), col0 + j)),
            pl.BlockSpec((16, tc), lambda i, j: (jnp.minimum((i + 1) * r, nrow_blocks * r - 1), col0 + j))]


def _shifted(x_ref, hp_ref, hn_ref, i, last):
    x = x_ref[...].astype(F32)
    tm = x.shape[0]
    rows = lax.broadcasted_iota(jnp.int32, x.shape, 0)
    prev_row = jnp.where(i > 0, hp_ref[15:16, :].astype(F32), 0.0)
    next_row = jnp.where(i < last, hn_ref[0:1, :].astype(F32), 0.0)
    xp = jnp.where(rows == 0, prev_row, pltpu.roll(x, 1, 0))
    xn = jnp.where(rows == tm - 1, next_row, pltpu.roll(x, tm - 1, 0))
    return xp, x, xn


def _conv(x_src, col0, width, w3, bias, act, name, out_dtype=BF16, slab=None):
    s = x_src.shape[0]
    tm = _pick(s, (256, 128))
    tc = _pick(width, CONV_COLS)
    nb = s // tm
    c0 = col0 // tc

    def body(*refs):
        x_ref, hp_ref, hn_ref, w_ref, b_ref = refs[:5]
        o_ref = refs[-1]
        i = pl.program_id(0)
        xp, x, xn = _shifted(x_ref, hp_ref, hn_ref, i, nb - 1)
        w = w_ref[...]
        y = w[0:1, :] * xp + w[1:2, :] * x + w[2:3, :] * xn + b_ref[...]
        if act:
            y = y * _sigmoid(y)
        o_ref[...] = y.astype(o_ref.dtype)

    in_specs = ([pl.BlockSpec((tm, tc), lambda i, j: (i, c0 + j))] + _halo_specs(tm, tc, c0, nb)
                + [pl.BlockSpec((3, tc), lambda i, j: (0, j)), pl.BlockSpec((1, tc), lambda i, j: (0, j))])
    out_shape, ocol, more, more_specs, alias = _slab(s, width, out_dtype, slab, 5)
    ob = ocol // tc
    return pl.pallas_call(
        body, grid=(nb, width // tc), in_specs=in_specs + more_specs, out_specs=pl.BlockSpec((tm, tc), lambda i, j: (i, ob + j)),
        out_shape=out_shape, input_output_aliases=alias, name=name, compiler_params=_cp("parallel", "parallel"))(
            x_src, x_src, x_src, w3, bias, *more)


def _conv_silu_bwd(x_src, col0, width, w3, bias, addends, add_widths, name):
    s = x_src.shape[0]
    tm = _pick(s, (256, 128))
    tc = _pick(width, CONV_COLS)
    nb = s // tm
    c0 = col0 // tc
    na = len(addends)

    def body(*refs):
        x_ref, hp_ref, hn_ref, w_ref, b_ref = refs[:5]
        a_refs = refs[5:5 + na]
        dp_ref, dw_ref, db_ref = refs[5 + na:]
        i, j = pl.program_id(1), pl.program_id(0)
        xp, x, xn = _shifted(x_ref, hp_ref, hn_ref, i, nb - 1)
        w = w_ref[...]
        pre = w[0:1, :] * xp + w[1:2, :] * x + w[2:3, :] * xn + b_ref[...]
        g = jnp.zeros_like(pre)
        for a_ref, aw in zip(a_refs, add_widths):
            av = a_ref[...].astype(F32)
            g = g + (av if aw == width else jnp.where(j < aw // tc, av, 0.0))
        sg = _sigmoid(pre)
        dpre = g * (sg * (1.0 + pre * (1.0 - sg)))
        dp_ref[...] = dpre.astype(dp_ref.dtype)

        @pl.when(i == 0)
        def _():
            dw_ref[...] = jnp.zeros_like(dw_ref)
            db_ref[...] = jnp.zeros_like(db_ref)

        dw_ref[...] += jnp.concatenate([jnp.sum(dpre * xp, axis=0, keepdims=True), jnp.sum(dpre * x, axis=0, keepdims=True),
                                        jnp.sum(dpre * xn, axis=0, keepdims=True)], axis=0)
        db_ref[...] += jnp.sum(dpre, axis=0, keepdims=True)

    r = tm // 16
    in_specs = [pl.BlockSpec((tm, tc), lambda j, i: (i, c0 + j)),
                pl.BlockSpec((16, tc), lambda j, i: (jnp.maximum(i * r - 1, 0), c0 + j)),
                pl.BlockSpec((16, tc), lambda j, i: (jnp.minimum((i + 1) * r, nb * r - 1), c0 + j)),
                pl.BlockSpec((3, tc), lambda j, i: (0, j)), pl.BlockSpec((1, tc), lambda j, i: (0, j))]
    for aw in add_widths:
        nblk = aw // tc
        in_specs.append(pl.BlockSpec((tm, tc), lambda j, i, nblk=nblk: (i, jnp.minimum(j, nblk - 1))))
    return pl.pallas_call(
        body, grid=(width // tc, nb), in_specs=in_specs,
        out_specs=[pl.BlockSpec((tm, tc), lambda j, i: (i, j)), pl.BlockSpec((3, tc), lambda j, i: (0, j)), pl.BlockSpec((1, tc), lambda j, i: (0, j))],
        out_shape=[SDS((s, width), BF16), SDS((3, width), F32), SDS((1, width), F32)],
        name=name, compiler_params=_cp("parallel", "arbitrary"))(x_src, x_src, x_src, w3, bias, *addends)


def _ffn_gate_fwd(u, w3, bias, name):
    s = u.shape[0]
    tm = _pick(s, (256, 128))
    tc = _pick(D_FF, CONV_COLS)
    nb = s // tm
    nj = D_FF // tc

    def body(g_ref, gp, gn, u_ref, up, un, wg_ref, wu_ref, bg_ref, bu_ref, o_ref):
        i = pl.program_id(0)
        outs = []
        for (x_ref, hp, hn, w_ref, b_ref) in ((g_ref, gp, gn, wg_ref, bg_ref), (u_ref, up, un, wu_ref, bu_ref)):
            xp, x, xn = _shifted(x_ref, hp, hn, i, nb - 1)
            w = w_ref[...]
            outs.append(w[0:1, :] * xp + w[1:2, :] * x + w[2:3, :] * xn + b_ref[...])
        gate, upv = outs
        o_ref[...] = (gate * _sigmoid(gate) * upv).astype(o_ref.dtype)

    def xspecs(c0):
        return [pl.BlockSpec((tm, tc), lambda i, j: (i, c0 + j))] + _halo_specs(tm, tc, c0, nb)

    in_specs = (xspecs(0) + xspecs(nj)
                + [pl.BlockSpec((3, tc), lambda i, j: (0, j)), pl.BlockSpec((3, tc), lambda i, j: (0, nj + j)),
                   pl.BlockSpec((1, tc), lambda i, j: (0, j)), pl.BlockSpec((1, tc), lambda i, j: (0, nj + j))])
    return pl.pallas_call(
        body, grid=(nb, nj), in_specs=in_specs, out_specs=pl.BlockSpec((tm, tc), lambda i, j: (i, j)),
        out_shape=SDS((s, D_FF), BF16), name=name, compiler_params=_cp("parallel", "parallel"))(u, u, u, u, u, u, w3, w3, bias, bias)


def _ffn_gate_bwd(u, w3, bias, dact, name):
    s = u.shape[0]
    tm = _pick(s, (256, 128))
    tc = _pick(D_FF, CONV_COLS)
    nb = s // tm
    nj = D_FF // tc

    def body(g_ref, gp, gn, u_ref, up, un, wg_ref, wu_ref, bg_ref, bu_ref, da_ref, dg_ref, du_ref, dwg_ref, dwu_ref, dbg_ref, dbu_ref):
        i = pl.program_id(1)
        sh, pre = [], []
        for (x_ref, hp, hn, w_ref, b_ref) in ((g_ref, gp, gn, wg_ref, bg_ref), (u_ref, up, un, wu_ref, bu_ref)):
            xs3 = _shifted(x_ref, hp, hn, i, nb - 1)
            w = w_ref[...]
            sh.append(xs3)
            pre.append(w[0:1, :] * xs3[0] + w[1:2, :] * xs3[1] + w[2:3, :] * xs3[2] + b_ref[...])
        gate, upv = pre
        da = da_ref[...].astype(F32)
        sg = _sigmoid(gate)
        dgate = da * upv * (sg * (1.0 + gate * (1.0 - sg)))
        dup = da * gate * sg
        dg_ref[...] = dgate.astype(dg_ref.dtype)
        du_ref[...] = dup.astype(du_ref.dtype)

        @pl.when(i == 0)
        def _():
            for r in (dwg_ref, dwu_ref, dbg_ref, dbu_ref):
                r[...] = jnp.zeros_like(r)

        for d, xs3, dw_ref, db_ref in ((dgate, sh[0], dwg_ref, dbg_ref), (dup, sh[1], dwu_ref, dbu_ref)):
            dw_ref[...] += jnp.concatenate([jnp.sum(d * xs3[0], axis=0, keepdims=True), jnp.sum(d * xs3[1], axis=0, keepdims=True),
                                            jnp.sum(d * xs3[2], axis=0, keepdims=True)], axis=0)
            db_ref[...] += jnp.sum(d, axis=0, keepdims=True)

    r = tm // 16

    def xspecs(c0):
        return [pl.BlockSpec((tm, tc), lambda j, i: (i, c0 + j)),
                pl.BlockSpec((16, tc), lambda j, i: (jnp.maximum(i * r - 1, 0), c0 + j)),
                pl.BlockSpec((16, tc), lambda j, i: (jnp.minimum((i + 1) * r, nb * r - 1), c0 + j))]

    in_specs = (xspecs(0) + xspecs(nj)
                + [pl.BlockSpec((3, tc), lambda j, i: (0, j)), pl.BlockSpec((3, tc), lambda j, i: (0, nj + j)),
                   pl.BlockSpec((1, tc), lambda j, i: (0, j)), pl.BlockSpec((1, tc), lambda j, i: (0, nj + j)),
                   pl.BlockSpec((tm, tc), lambda j, i: (i, j))])
    blk = pl.BlockSpec((tm, tc), lambda j, i: (i, j))
    w_o = pl.BlockSpec((3, tc), lambda j, i: (0, j))
    b_o = pl.BlockSpec((1, tc), lambda j, i: (0, j))
    return pl.pallas_call(
        body, grid=(nj, nb), in_specs=in_specs, out_specs=[blk, blk, w_o, w_o, b_o, b_o],
        out_shape=[SDS((s, D_FF), BF16), SDS((s, D_FF), BF16), SDS((3, D_FF), F32), SDS((3, D_FF), F32), SDS((1, D_FF), F32), SDS((1, D_FF), F32)],
        name=name, compiler_params=_cp("parallel", "arbitrary"))(u, u, u, u, u, u, w3, w3, bias, bias, dact)


def _exchange_sems(n):
    return [pltpu.SemaphoreType.DMA((7 * n,)), pltpu.SemaphoreType.DMA((7 * n,)), pltpu.SemaphoreType.DMA((n,))]


def _exchange_copies(srcs, outs, send_sems, recv_sems, local_sems):
    x, y, c = lax.axis_index("x"), lax.axis_index("y"), lax.axis_index("c")
    me = 4 * x + 2 * y + c
    locals_ = [pltpu.make_async_copy(srcs[a].at[me], outs[a].at[me], local_sems.at[a]) for a in range(len(srcs))]
    sends, recvs = [], []
    for k in range(1, N_DEV):
        px, py, pc = x ^ ((k >> 2) & 1), y ^ ((k >> 1) & 1), c ^ (k & 1)
        peer = 4 * px + 2 * py + pc
        for a in range(len(srcs)):
            sems = dict(send_sem=send_sems.at[a * 7 + k - 1], recv_sem=recv_sems.at[a * 7 + k - 1], device_id_type=MESH)
            sends.append(pltpu.make_async_remote_copy(src_ref=srcs[a].at[peer], dst_ref=outs[a].at[me], device_id=(px, py, pc), **sems))
            recvs.append(pltpu.make_async_remote_copy(src_ref=srcs[a].at[peer], dst_ref=outs[a].at[peer], device_id=(x, y, c), **sems))
    return locals_, sends, recvs


def _exchange_start(srcs, outs, send_sems, recv_sems, local_sems):
    locals_, sends, _ = _exchange_copies(srcs, outs, send_sems, recv_sems, local_sems)
    for cp in locals_ + sends:
        cp.start()


def _exchange_wait(srcs, outs, send_sems, recv_sems, local_sems):
    locals_, sends, recvs = _exchange_copies(srcs, outs, send_sems, recv_sems, local_sems)
    for cp in recvs:
        cp.wait_recv()
    for cp in sends:
        cp.wait_send()
    for cp in locals_:
        cp.wait()


def _ssd_common(dt_ref, dtt_ref, al_r, al_c, bi_r, bi_c, off, rev):
    li = lax.broadcasted_iota(jnp.int32, (CHUNK, CHUNK), 0)
    si = lax.broadcasted_iota(jnp.int32, (CHUNK, CHUNK), 1)
    mask = (li <= si) if rev else (li >= si)
    mask_t = (li >= si) if rev else (li <= si)
    a_r = -jnp.exp(al_r[...])
    a_c = -jnp.exp(al_c[...])
    pre = dt_ref[:, off:off + N_HEADS] + bi_r[...]
    dt = _softplus(pre)
    cs = jnp.dot(mask.astype(F32), dt * a_r, precision=HIGH, preferred_element_type=F32)
    dt_t = _softplus(dtt_ref[off:off + N_HEADS, :] + bi_c[...])
    cs_t = jnp.dot(dt_t * a_c, mask_t.astype(F32), precision=HIGH, preferred_element_type=F32)
    tot = cs[0:1, :] if rev else cs[CHUNK - 1:CHUNK, :]
    return mask, mask_t, a_r, pre, dt, cs, cs_t, tot


def _ssd_fwd(xbc, dt_raw, dt_t, al_r, al_c, bi_r, bi_c, off, rev, name):
    s = xbc.shape[0]
    nc = s // CHUNK
    hp = D_INNER // N_HEADS
    hpg = N_HEADS // N_GROUPS
    gs = N_GROUPS * D_STATE
    cm = (lambda c: nc - 1 - c) if rev else (lambda c: c)

    def body(x_ref, b_ref, c_ref, dt_ref, dtt_ref, alr, alc, bir, bic, y_ref, st_ref, h_scr):
        @pl.when(pl.program_id(0) == 0)
        def _():
            h_scr[...] = jnp.zeros_like(h_scr)

        mask, _, _, _, dt, cs, cs_t, tot = _ssd_common(dt_ref, dtt_ref, alr, alc, bir, bic, off, rev)
        xs = x_ref[...]
        ys = []
        for g in range(N_GROUPS):
            bg = b_ref[:, g * D_STATE:(g + 1) * D_STATE]
            cg = c_ref[:, g * D_STATE:(g + 1) * D_STATE]
            gm = lax.dot_general(cg, bg, NT, preferred_element_type=F32)
            hcat = h_scr[g]
            st_ref[0, g] = hcat
            ch = lax.dot_general(cg, hcat.astype(BF16), NT, preferred_element_type=F32)
            xdd = []
            for r in range(hpg):
                h = g * hpg + r
                csc, csr, tot_h = cs[:, h:h + 1], cs_t[h:h + 1, :], tot[:, h:h + 1]
                lm = jnp.exp(jnp.where(mask, csc - csr, NEG))
                xdf = xs[:, h * hp:(h + 1) * hp].astype(F32) * dt[:, h:h + 1]
                y = jnp.dot((gm * lm).astype(BF16), xdf.astype(BF16), preferred_element_type=F32)
                ys.append(y + jnp.exp(csc) * ch[:, r * hp:(r + 1) * hp])
                xdd.append((xdf * jnp.exp(tot_h - csc)).astype(BF16))
            snew = lax.dot_general(jnp.concatenate(xdd, axis=1), bg, TN, preferred_element_type=F32)
            for r in range(hpg):
                rs = slice(r * hp, (r + 1) * hp)
                h_scr[g, rs, :] = jnp.exp(tot[:, g * hpg + r:g * hpg + r + 1]) * hcat[rs] + snew[rs]
        y_ref[...] = jnp.concatenate(ys, axis=1)

    small = lambda shape: pl.BlockSpec(shape, lambda c: (0, 0))
    in_specs = [pl.BlockSpec((CHUNK, D_INNER), lambda c: (cm(c), 0)),
                pl.BlockSpec((CHUNK, gs), lambda c: (cm(c), D_INNER // gs)),
                pl.BlockSpec((CHUNK, gs), lambda c: (cm(c), D_INNER // gs + 1)),
                pl.BlockSpec((CHUNK, 128), lambda c: (cm(c), 0)),
                pl.BlockSpec((2 * N_HEADS, CHUNK), lambda c: (0, cm(c))),
                small((1, N_HEADS)), small((N_HEADS, 1)), small((1, N_HEADS)), small((N_HEADS, 1))]
    return pl.pallas_call(
        body, grid=(nc,), in_specs=in_specs,
        out_specs=[pl.BlockSpec((CHUNK, D_INNER), lambda c: (cm(c), 0)), pl.BlockSpec((1, N_GROUPS, hpg * hp, D_STATE), lambda c: (cm(c), 0, 0, 0))],
        out_shape=[SDS((s, D_INNER), F32), SDS((nc, N_GROUPS, hpg * hp, D_STATE), F32)],
        scratch_shapes=[pltpu.VMEM((N_GROUPS, hpg * hp, D_STATE), F32)], name=name, compiler_params=_cp("arbitrary"))(
            xbc, xbc, xbc, dt_raw, dt_t, al_r, al_c, bi_r, bi_c)


def _ssd_bwd(xbc, dt_raw, dt_t, al_r, al_c, bi_r, bi_c, states, dy, off, rev, name, exchange=()):
    s = xbc.shape[0]
    nc = s // CHUNK
    hp = D_INNER // N_HEADS
    gs = N_GROUPS * D_STATE
    hpg = N_HEADS // N_GROUPS
    cm = (lambda c: c) if rev else (lambda c: nc - 1 - c)
    nx = len(exchange)

    def body(*refs):
        x_ref, b_ref, c_ref, dt_ref, dtt_ref, alr, alc, bir, bic, st_ref, dy_ref = refs[:11]
        xch_src = refs[11:11 + nx]
        dx_ref, ddt_ref, dal_ref, dbi_ref = refs[11 + nx:15 + nx]
        xch_dst = refs[15 + nx:15 + 2 * nx]
        dh_scr = refs[15 + 2 * nx]
        xch_sems = refs[16 + 2 * nx:]

        @pl.when(pl.program_id(0) == 0)
        def _():
            dh_scr[...] = jnp.zeros_like(dh_scr)
            dal_ref[...] = jnp.zeros_like(dal_ref)
            dbi_ref[...] = jnp.zeros_like(dbi_ref)
            if nx:
                _exchange_start(xch_src, xch_dst, *xch_sems)

        if nx:
            @pl.when(pl.program_id(0) == nc - 1)
            def _():
                _exchange_wait(xch_src, xch_dst, *xch_sems)

        mask, mask_t, a_r, pre, dt, cs, cs_t, tot = _ssd_common(dt_ref, dtt_ref, alr, alc, bir, bic, off, rev)
        xs = x_ref[...]
        dyv = dy_ref[...]
        rows = lax.broadcasted_iota(jnp.int32, (CHUNK, 1), 0)
        end_row = (rows == 0) if rev else (rows == CHUNK - 1)
        lane_h = lax.broadcasted_iota(jnp.int32, (1, N_HEADS), 1)
        sub_h = lax.broadcasted_iota(jnp.int32, (N_HEADS, 1), 0)
        dcs_all = jnp.zeros((CHUNK, N_HEADS), F32)
        colw_all = jnp.zeros((N_HEADS, CHUNK), F32)
        dxsum_all = jnp.zeros((CHUNK, N_HEADS), F32)
        dxs, dbs, dcs_out = [], [], []
        for g in range(N_GROUPS):
            bg = b_ref[:, g * D_STATE:(g + 1) * D_STATE]
            cg = c_ref[:, g * D_STATE:(g + 1) * D_STATE]
            gm = lax.dot_general(cg, bg, NT, preferred_element_type=F32)
            hcat = st_ref[0, g]
            dhcat = dh_scr[g]
            hb, dhb = hcat.astype(BF16), dhcat.astype(BF16)
            ch = lax.dot_general(cg, hb, NT, preferred_element_type=F32)
            z = lax.dot_general(bg, dhb, NT, preferred_element_type=F32)
            dg_sum = jnp.zeros((CHUNK, CHUNK), F32)
            dchs, xdds, t_hs = [], [], []
            for r in range(hpg):
                h = g * hpg + r
                rs = slice(r * hp, (r + 1) * hp)
                csc, csr, tot_h = cs[:, h:h + 1], cs_t[h:h + 1, :], tot[:, h:h + 1]
                lm = jnp.exp(jnp.where(mask, csc - csr, NEG))
                xh = xs[:, h * hp:(h + 1) * hp].astype(F32)
                dtc = dt[:, h:h + 1]
                xdf = xh * dtc
                xd = xdf.astype(BF16)
                dyh = dyv[:, h * hp:(h + 1) * hp]
                dyb = dyh.astype(BF16)
                m = gm * lm
                ecs = jnp.exp(csc)
                dec = jnp.exp(tot_h - csc)
                t_h = jnp.exp(tot_h)
                dxd = lax.dot_general(m.astype(BF16), dyb, TN, preferred_element_type=F32)
                dm = lax.dot_general(dyb, xd, NT, preferred_element_type=F32)
                dg_sum = dg_sum + dm * lm
                w = dm * m
                dcs = jnp.sum(dyh * (ecs * ch[:, rs]), axis=1, keepdims=True)
                dchs.append((ecs * dyh).astype(BF16))
                dxd = dxd + dec * z[:, rs]
                ddec = jnp.sum(xdf * z[:, rs], axis=1, keepdims=True) * dec
                xdds.append((xdf * dec).astype(BF16))
                dtot = (jnp.sum(jnp.sum(dhcat[rs] * hcat[rs], axis=1, keepdims=True), axis=0, keepdims=True) * t_h
                        + jnp.sum(ddec, axis=0, keepdims=True))
                t_hs.append(t_h)
                dcs = dcs + jnp.sum(w, axis=1, keepdims=True) - ddec
                dcs = dcs + jnp.where(end_row, dtot, 0.0)
                colw_all = colw_all + (sub_h == h).astype(F32) * jnp.sum(w, axis=0, keepdims=True)
                onehot = (lane_h == h).astype(F32)
                dcs_all = dcs_all + dcs * onehot
                dxsum_all = dxsum_all + jnp.sum(dxd * xh, axis=1, keepdims=True) * onehot
                dxs.append(dxd * dtc)
            dgs = dg_sum.astype(BF16)
            dchc = jnp.concatenate(dchs, axis=1)
            dcs_out.append(jnp.dot(dgs, bg, preferred_element_type=F32) + jnp.dot(dchc, hb, preferred_element_type=F32))
            dbs.append(lax.dot_general(dgs, cg, TN, preferred_element_type=F32)
                       + jnp.dot(jnp.concatenate(xdds, axis=1), dhb, preferred_element_type=F32))
            dh_in = lax.dot_general(dchc, cg, TN, preferred_element_type=F32)
            for r in range(hpg):
                rs = slice(r * hp, (r + 1) * hp)
                dh_scr[g, rs, :] = dh_in[rs] + t_hs[r] * dhcat[rs]
        dx_ref[...] = jnp.concatenate(dxs + dbs + dcs_out, axis=1)
        mt = mask_t.astype(F32)
        da = (jnp.dot(mt, dcs_all, precision=HIGH, preferred_element_type=F32)
              - lax.dot_general(mt, colw_all, NT, precision=HIGH, preferred_element_type=F32))
        dal_ref[...] += jnp.sum(da * dt, axis=0, keepdims=True) * a_r
        ddt_raw = (da * a_r + dxsum_all) * _sigmoid(pre)
        ddt_ref[...] = ddt_raw
        dbi_ref[...] += jnp.sum(ddt_raw, axis=0, keepdims=True)

    small = lambda shape: pl.BlockSpec(shape, lambda c: (0, 0))
    in_specs = [pl.BlockSpec((CHUNK, D_INNER), lambda c: (cm(c), 0)),
                pl.BlockSpec((CHUNK, gs), lambda c: (cm(c), D_INNER // gs)),
                pl.BlockSpec((CHUNK, gs), lambda c: (cm(c), D_INNER // gs + 1)),
                pl.BlockSpec((CHUNK, 128), lambda c: (cm(c), 0)),
                pl.BlockSpec((2 * N_HEADS, CHUNK), lambda c: (0, cm(c))),
                small((1, N_HEADS)), small((N_HEADS, 1)), small((1, N_HEADS)), small((N_HEADS, 1)),
                pl.BlockSpec((1, N_GROUPS, hpg * hp, D_STATE), lambda c: (cm(c), 0, 0, 0)),
                pl.BlockSpec((CHUNK, D_INNER), lambda c: (cm(c), 0))]
    any_spec = pl.BlockSpec(memory_space=pl.ANY)
    return pl.pallas_call(
        body, grid=(nc,), in_specs=in_specs + [any_spec] * nx,
        out_specs=[pl.BlockSpec((CHUNK, XBC), lambda c: (cm(c), 0)), pl.BlockSpec((CHUNK, N_HEADS), lambda c: (cm(c), 0)),
                   small((1, N_HEADS)), small((1, N_HEADS))] + [any_spec] * nx,
        out_shape=[SDS((s, XBC), F32), SDS((s, N_HEADS), F32), SDS((1, N_HEADS), F32), SDS((1, N_HEADS), F32)]
        + [SDS(a.shape, a.dtype) for a in exchange],
        scratch_shapes=[pltpu.VMEM((N_GROUPS, hpg * hp, D_STATE), F32)] + (_exchange_sems(nx) if nx else []),
        name=name, compiler_params=_cp("arbitrary"))(xbc, xbc, xbc, dt_raw, dt_t, al_r, al_c, bi_r, bi_c, states, dy, *exchange)


def _gate_fwd(yf, yb, xbc, proj, dskip_x, norm_w, mix, name):
    s = yf.shape[0]
    tm = _pick(s, (256, 128))
    gw = D_INNER // N_GROUPS
    zc = 3 * D_MODEL // D_INNER

    def body(yf_ref, yb_ref, x_ref, z_ref, d_ref, w_ref, _, o_ref):
        y = yf_ref[...] + yb_ref[...] + d_ref[...] * x_ref[...].astype(F32)
        z = z_ref[...].astype(F32)
        gt = y * (z * _sigmoid(z))
        outs = []
        for g in range(N_GROUPS):
            gg = gt[:, g * gw:(g + 1) * gw]
            outs.append(gg * lax.rsqrt(jnp.mean(gg * gg, axis=-1, keepdims=True) + EPS))
        o_ref[...] = (jnp.concatenate(outs, axis=1) * w_ref[...]).astype(o_ref.dtype)

    row = pl.BlockSpec((tm, D_INNER), lambda i: (i, 0))
    vec = pl.BlockSpec((1, D_INNER), lambda i: (0, 0))
    out_shape, ocol, more, more_specs, alias = _slab(s, D_INNER, BF16, (mix.shape[1], mix.shape[1] - D_INNER, mix), 6)
    return pl.pallas_call(
        body, grid=(s // tm,), in_specs=[row, row, row, pl.BlockSpec((tm, D_INNER), lambda i: (i, zc)), vec, vec] + more_specs,
        out_specs=pl.BlockSpec((tm, D_INNER), lambda i: (i, ocol // D_INNER)), out_shape=out_shape, input_output_aliases=alias,
        name=name, compiler_params=_cp("parallel"))(yf, yb, xbc, proj, dskip_x, norm_w, *more)


def _gate_bwd(yf, yb, xbc, proj, dskip_x, norm_w, dmix, dproj, name):
    s = yf.shape[0]
    tm = _pick(s, (256, 128))
    gw = D_INNER // N_GROUPS
    zc = 3 * D_MODEL // D_INNER

    def body(yf_ref, yb_ref, x_ref, z_ref, d_ref, w_ref, do_ref, _, dy_ref, dz_ref, dxs_ref, dw_ref, dd_ref):
        xf = x_ref[...].astype(F32)
        y = yf_ref[...] + yb_ref[...] + d_ref[...] * xf
        z = z_ref[...].astype(F32)
        sg = _sigmoid(z)
        sz = z * sg
        gt = y * sz
        do = do_ref[...].astype(F32)
        dgh = do * w_ref[...]
        ghs, dgts = [], []
        for g in range(N_GROUPS):
            gg = gt[:, g * gw:(g + 1) * gw]
            r = lax.rsqrt(jnp.mean(gg * gg, axis=-1, keepdims=True) + EPS)
            gh = gg * r
            dg = dgh[:, g * gw:(g + 1) * gw]
            ghs.append(gh)
            dgts.append(r * (dg - gh * jnp.mean(dg * gh, axis=-1, keepdims=True)))
        ghat = jnp.concatenate(ghs, axis=1)
        dgt = jnp.concatenate(dgts, axis=1)
        dy = dgt * sz
        dy_ref[...] = dy
        dz_ref[...] = (dgt * y * (sg * (1.0 + z * (1.0 - sg)))).astype(dz_ref.dtype)
        dxs_ref[...] = dy * d_ref[...]

        @pl.when(pl.program_id(0) == 0)
        def _():
            dw_ref[...] = jnp.zeros_like(dw_ref)
            dd_ref[...] = jnp.zeros_like(dd_ref)

        dw_ref[...] += jnp.sum(do * ghat, axis=0, keepdims=True)
        dd_ref[...] += jnp.sum(dy * xf, axis=0, keepdims=True)

    row = pl.BlockSpec((tm, D_INNER), lambda i: (i, 0))
    vec = pl.BlockSpec((1, D_INNER), lambda i: (0, 0))
    dz_shape, _, more, more_specs, alias = _slab(s, D_INNER, BF16, (dproj.shape[1], zc * D_INNER, dproj), 7, out_idx=1)
    return pl.pallas_call(
        body, grid=(s // tm,),
        in_specs=[row, row, row, pl.BlockSpec((tm, D_INNER), lambda i: (i, zc)), vec, vec, pl.BlockSpec((tm, D_INNER), lambda i: (i, 1))] + more_specs,
        out_specs=[row, pl.BlockSpec((tm, D_INNER), lambda i: (i, zc)), row, vec, vec],
        out_shape=[SDS((s, D_INNER), F32), dz_shape, SDS((s, D_INNER), F32), SDS((1, D_INNER), F32), SDS((1, D_INNER), F32)],
        input_output_aliases=alias, name=name, compiler_params=_cp("arbitrary"))(yf, yb, xbc, proj, dskip_x, norm_w, dmix, *more)


def _adamw(parts, w, m, v, name):
    r, c = w.shape
    tr = _pick(r, (256, 352, 128))

    def body(p_ref, w_ref, m_ref, v_ref, g_ref, d_ref, nm_ref, nv_ref):
        g = p_ref[0].astype(F32)
        for i in range(1, N_DEV):
            g = g + p_ref[i].astype(F32)
        mn = B1 * m_ref[...] + (1.0 - B1) * g
        vn = B2 * v_ref[...] + (1.0 - B2) * (g * g)
        m_hat = mn / (1.0 - B1 ** STEP)
        v_hat = vn / (1.0 - B2 ** STEP)
        g_ref[...] = g
        d_ref[...] = -LR * (m_hat / (jnp.sqrt(v_hat) + AEPS) + WD * w_ref[...])
        nm_ref[...] = mn
        nv_ref[...] = vn

    blk = pl.BlockSpec((tr, c), lambda i: (i, 0))
    return pl.pallas_call(
        body, grid=(r // tr,), in_specs=[pl.BlockSpec((N_DEV, tr, c), lambda i: (0, i, 0)), blk, blk, blk],
        out_specs=[blk, blk, blk, blk], out_shape=[SDS((r, c), F32)] * 4, name=name, compiler_params=_cp("parallel"))(parts, w, m, v)


def _sum_parts(parts, name):
    _, r, c = parts.shape

    def body(p_ref, o_ref):
        g = p_ref[0]
        for i in range(1, N_DEV):
            g = g + p_ref[i]
        o_ref[...] = g

    return pl.pallas_call(body, out_shape=SDS((r, c), F32), name=name)(parts)


def _adamw_small(g, w, m, v, name):
    def body(g_ref, w_ref, m_ref, v_ref, d_ref, nm_ref, nv_ref):
        gv = g_ref[...]
        mn = B1 * m_ref[...] + (1.0 - B1) * gv
        vn = B2 * v_ref[...] + (1.0 - B2) * (gv * gv)
        m_hat = mn / (1.0 - B1 ** STEP)
        v_hat = vn / (1.0 - B2 ** STEP)
        d_ref[...] = -LR * (m_hat / (jnp.sqrt(v_hat) + AEPS) + WD * w_ref[...])
        nm_ref[...] = mn
        nv_ref[...] = vn

    return pl.pallas_call(body, out_shape=[SDS(g.shape, F32)] * 3, name=name)(g, w, m, v)


def _my_index():
    return 4 * lax.axis_index("x") + 2 * lax.axis_index("y") + lax.axis_index("c")


def _all_gather(shards, name):
    n = len(shards)

    def body(*refs):
        srcs, outs = refs[:n], refs[n:2 * n]
        send_sems, recv_sems, local_sems = refs[2 * n:]
        x, y, c = lax.axis_index("x"), lax.axis_index("y"), lax.axis_index("c")
        sibling = (x, y, 1 - c)
        chips = [(1 - x, y), (x, 1 - y), (1 - x, 1 - y)]

        def idx(px, py, pc):
            return 4 * px + 2 * py + pc

        def copy(a, k, block, to, src=None):
            dst = outs[a].at[idx(*block)]
            return pltpu.make_async_remote_copy(
                src_ref=dst if src is None else src, dst_ref=dst,
                send_sem=send_sems.at[a * 7 + k], recv_sem=recv_sems.at[a * 7 + k], device_id=to, device_id_type=MESH)

        me = (x, y, c)
        locals_, first, passed = [], [], []
        for a in range(n):
            lc = pltpu.make_async_copy(srcs[a], outs[a].at[idx(*me)], local_sems.at[a])
            lc.start()
            locals_.append(lc)
            first.append(copy(a, 0, me, sibling, src=srcs[a]))
            first += [copy(a, 1 + j, me, (*chip, c), src=srcs[a]) for j, chip in enumerate(chips)]
        for cp in first:
            cp.start()
        for j, chip in enumerate(chips):
            for a in range(n):
                copy(a, 1 + j, (*chip, c), me).wait_recv()
                fw = copy(a, 4 + j, (*chip, c), sibling)
                fw.start()
                passed.append(fw)
        for a in range(n):
            copy(a, 0, sibling, me).wait_recv()
            for j, chip in enumerate(chips):
                copy(a, 4 + j, (*chip, 1 - c), me).wait_recv()
        for cp in first + passed:
            cp.wait_send()
        for lc in locals_:
            lc.wait()

    any_spec = pl.BlockSpec(memory_space=pl.ANY)
    return pl.pallas_call(
        body, in_specs=[any_spec] * n, out_specs=[any_spec] * n,
        out_shape=[SDS((N_DEV,) + s.shape, s.dtype) for s in shards],
        scratch_shapes=[pltpu.SemaphoreType.DMA((7 * n,)), pltpu.SemaphoreType.DMA((7 * n,)), pltpu.SemaphoreType.DMA((n,))],
        name=name)(*shards)


def _exchange(arrays, name):
    n = len(arrays)

    def body(*refs):
        srcs, outs = refs[:n], refs[n:2 * n]
        _exchange_start(srcs, outs, *refs[2 * n:])
        _exchange_wait(srcs, outs, *refs[2 * n:])

    any_spec = pl.BlockSpec(memory_space=pl.ANY)
    return pl.pallas_call(
        body, in_specs=[any_spec] * n, out_specs=[any_spec] * n,
        out_shape=[SDS(a.shape, a.dtype) for a in arrays], scratch_shapes=_exchange_sems(n), name=name)(*arrays)


def _to_pattern(t, d):
    if d == 1:
        return t
    s, w = t.shape
    return t.reshape(s // d, d, w).transpose(1, 0, 2).reshape(s, w)


def _from_pattern(t, d):
    if d == 1:
        return t
    s, w = t.shape
    return t.reshape(d, s // d, w).transpose(1, 0, 2).reshape(s, w)


def _pad_lanes(t, n):
    return jnp.pad(t, ((0, 0), (0, n - t.shape[1])))


def _to_shards(g, axis):
    r, c = g.shape
    if axis == 0:
        return g.reshape(N_DEV, r // N_DEV, c)
    return g.reshape(r, N_DEV, c // N_DEV).transpose(1, 0, 2)


def _local_step(x, target, p, early_exchange=True):
    s = x.shape[0]
    tabs_f, tabs_b = _rope_tables(s)
    expand = jnp.repeat(jnp.eye(N_HEADS, dtype=F32), HEAD_DIM, axis=1)
    w_main, w_dt = p["w_in"][:, :MAIN_W], _pad_lanes(p["w_in"][:, MAIN_W:], 128)
    al_r = {"f": p["a_log_f"], "b": p["a_log_b"]}
    bi_r = {"f": p["dt_bias_f"], "b": p["dt_bias_b"]}
    dskip_x = jnp.repeat(p["d_skip"], D_INNER // N_HEADS, axis=1)
    ssm_w3, ffn_w3 = p["ssm_conv_w"].T, p["ffn_conv_w"].T

    h1 = _rmsnorm_fwd(x, p["norm1_w"], "norm1_fwd")
    proj = _matmul(h1, w_main, name="in_proj")
    dt_raw = _matmul(h1, w_dt, name="in_proj_dt", out_dtype=F32)
    dt_t = dt_raw[:, :2 * N_HEADS].T
    qk = _rope_fwd(proj, tabs_f, "rope_fwd")
    v_col = 2
    pat = []
    for d in DILATIONS:
        qk_p = _to_pattern(qk, d)
        v_p = proj if d == 1 else _to_pattern(proj[:, 2 * D_MODEL:3 * D_MODEL], d)
        pat.append((qk_p, v_p, v_col if d == 1 else 0))
    os_, lses = [], []
    for d, (qk_p, v_p, vc) in zip(DILATIONS, pat):
        o_p, lse_p = _attn_fwd(qk_p, v_p, vc, s // d, f"attn_fwd_d{d}")
        os_.append(_from_pattern(o_p, d))
        lses.append(_from_pattern(lse_p, d))
    mix, lse_tot = _attn_combine(os_, lses, expand, "attn_combine")

    xbc = _conv(proj, 3 * D_MODEL + D_INNER, XBC, ssm_w3, p["ssm_conv_b"], True, "ssm_conv_fwd")
    col = lambda r: r.reshape(N_HEADS, 1)
    ssd_args = {k: (al_r[k], col(al_r[k]), bi_r[k], col(bi_r[k])) for k in ("f", "b")}
    yf, st_f = _ssd_fwd(xbc, dt_raw, dt_t, *ssd_args["f"], 0, False, "ssd_fwd_f")
    yb, st_b = _ssd_fwd(xbc, dt_raw, dt_t, *ssd_args["b"], N_HEADS, True, "ssd_fwd_b")
    mix = _gate_fwd(yf, yb, xbc, proj, dskip_x, p["ssm_norm_w"], mix, "ssm_gate_fwd")

    x2 =_matmul(mix, p["w_out"], name="out_proj", out_dtype=F32, residual=x)
    h2 = _rmsnorm_fwd(x2, p["norm2_w"], "norm2_fwd")
    u = _matmul(h2, p["w_up"], name="ffn_up")
    act = _ffn_gate_fwd(u, ffn_w3, p["ffn_conv_b"], "ffn_gate_fwd")
    x3 = _matmul(act, p["w_down"], name="ffn_down", out_dtype=F32, residual=x2)

    dx3, g_final, loss = _final_norm_loss(x3, p["final_norm_w"].reshape(1, D_MODEL), target, "final_norm_loss")
    dx3b = dx3.astype(BF16)
    g_w_down = _matmul(act.T, dx3b, name="dw_down")
    dact = _matmul(dx3b, p["w_down"], name="d_act", trans_b=True)
    dug, duu, dwg, dwu, dbg, dbu = _ffn_gate_bwd(u, ffn_w3, p["ffn_conv_b"], dact, "ffn_gate_bwd")
    flip3 = lambda w3: w3[::-1]
    zero_b = jnp.zeros((1, D_FF), F32)
    du = _conv(dug, 0, D_FF, flip3(ffn_w3[:, :D_FF]), zero_b, False, "ffn_conv_bwd_gate", slab=(2 * D_FF, 0, None))
    du = _conv(duu, 0, D_FF, flip3(ffn_w3[:, D_FF:]), zero_b, False, "ffn_conv_bwd_up", slab=(2 * D_FF, D_FF, du))
    g_w_up = _matmul(h2.T, du, name="dw_up")
    dh2 = _matmul(du, p["w_up"], name="d_h2", trans_b=True)
    dx2, g_norm2 = _rmsnorm_bwd(x2, p["norm2_w"], dh2, dx3, "norm2_bwd")
    dx2b = dx2.astype(BF16)
    g_w_out = _matmul(mix.T, dx2b, name="dw_out")
    dmix = _matmul(dx2b, p["w_out"], name="d_mix", trans_b=True)

    delta = _attn_delta(dmix, mix, expand.T, "attn_delta")
    dq_u, dk_u, dv_u = [], [], []
    for d, (qk_p, v_p, vc) in zip(DILATIONS, pat):
        do_p = dmix if d == 1 else _to_pattern(dmix[:, :D_MODEL], d)
        lse_p, dl_p = _to_pattern(lse_tot, d), _to_pattern(delta, d)
        dq = _attn_bwd_dq(qk_p, v_p, vc, do_p, lse_p, dl_p, s // d, f"attn_bwd_dq_d{d}")
        dk, dv = _attn_bwd_dkv(qk_p, v_p, vc, do_p, lse_p.T, dl_p.T, s // d, f"attn_bwd_dkv_d{d}")
        dq_u.append(_from_pattern(dq, d))
        dk_u.append(_from_pattern(dk, d))
        dv_u.append(_from_pattern(dv, d))
    dproj = _sum3_rope(*dq_u, tabs_b, "rope_bwd_q", slab=(MAIN_W, 0, None))
    dproj = _sum3_rope(*dk_u, tabs_b, "rope_bwd_k", slab=(MAIN_W, D_MODEL, dproj))
    dproj = _sum3_rope(*dv_u, None, "sum_dv", slab=(MAIN_W, 2 * D_MODEL, dproj))

    dy, dproj, dxs_skip, g_ssm_norm, g_dskip_lanes = _gate_bwd(yf, yb, xbc, proj, dskip_x, p["ssm_norm_w"], dmix, dproj, "ssm_gate_bwd")
    early = [_to_shards(g_w_out, 0), _to_shards(g_w_up, 1), _to_shards(g_w_down, 0)] if early_exchange else []
    dxbc_f, ddt_f, g_al_f, g_bi_f, *got = _ssd_bwd(xbc, dt_raw, dt_t, *ssd_args["f"], st_f, dy, 0, False, "ssd_bwd_f", exchange=early)
    if early_exchange:
        g_w_out, g_w_up, g_w_down = got
    dxbc_b, ddt_b, g_al_b, g_bi_b = _ssd_bwd(xbc, dt_raw, dt_t, *ssd_args["b"], st_b, dy, N_HEADS, True, "ssd_bwd_b")
    dpre, g_ssm_w3, g_ssm_cb = _conv_silu_bwd(proj, 3 * D_MODEL + D_INNER, XBC, ssm_w3, p["ssm_conv_b"],
                                              [dxbc_f, dxbc_b, dxs_skip], [XBC, XBC, D_INNER], "ssm_conv_bwd")
    dproj = _conv(dpre, 0, XBC, flip3(ssm_w3), jnp.zeros((1, XBC), F32), False, "ssm_conv_bwd_x",
                  slab=(MAIN_W, 3 * D_MODEL + D_INNER, dproj))

    ddt =_pad_lanes(jnp.concatenate([ddt_f, ddt_b], axis=1), 128).astype(BF16)
    h1t = h1.T
    g_w_main = _matmul(h1t, dproj, name="dw_in")
    g_w_dt = _matmul(h1t, ddt, name="dw_in_dt")
    dh1 = _matmul(dproj, w_main, name="d_h1", trans_b=True, out_dtype=F32)
    dh1 = _matmul(ddt, w_dt, name="d_h1_dt", trans_b=True, out_dtype=F32, residual=dh1)
    grad_x, g_norm1 = _rmsnorm_bwd(x, p["norm1_w"], dh1, dx2, "norm1_bwd")

    g_w_in = jnp.concatenate([g_w_main, g_w_dt[:, :2 * N_HEADS]], axis=1)
    g_dskip = jnp.sum(g_dskip_lanes.reshape(N_HEADS, D_INNER // N_HEADS), axis=1).reshape(1, N_HEADS)
    small = {
        "norm1_w": g_norm1, "ssm_conv_w": g_ssm_w3.T, "ssm_conv_b": g_ssm_cb, "a_log_f": g_al_f, "a_log_b": g_al_b,
        "dt_bias_f": g_bi_f, "dt_bias_b": g_bi_b, "d_skip": g_dskip, "ssm_norm_w": g_ssm_norm, "norm2_w": g_norm2,
        "ffn_conv_w": jnp.concatenate([dwg, dwu], axis=1).T, "ffn_conv_b": jnp.concatenate([dbg, dbu], axis=1), "final_norm_w": g_final,
    }
    big = {"w_in": g_w_in, "w_out": g_w_out, "w_up": g_w_up, "w_down": g_w_down}
    return loss[0, 0], grad_x, big, small


SMALL_ORDER = ("norm1_w", "ssm_conv_w", "ssm_conv_b", "a_log_f", "a_log_b", "dt_bias_f", "dt_bias_b", "d_skip",
               "ssm_norm_w", "norm2_w", "ffn_conv_w", "ffn_conv_b", "final_norm_w")
SHARDED_SMALL = ("ssm_conv_w", "ffn_conv_w")
BIG_ORDER = ("w_in", "w_out", "w_up", "w_down")


def _pack(vals):
    rows = []
    for v in vals:
        f = v.reshape(-1).astype(F32)
        n = -(-f.shape[0] // 128) * 128
        rows.append(jnp.pad(f, (0, n - f.shape[0])).reshape(-1, 128))
    out = jnp.concatenate(rows, axis=0)
    pad = -out.shape[0] % 8
    return jnp.pad(out, ((0, pad), (0, 0)))


def _unpack(packed, shapes):
    out, r = [], 0
    for shp in shapes:
        n = math.prod(shp)
        nr = -(-n // 128)
        out.append(packed[r:r + nr].reshape(-1)[:n].reshape(shp))
        r += nr
    return out


def kernel(x, norm1_w, w_in, ssm_conv_w, ssm_conv_b, a_log_f, a_log_b, dt_bias_f, dt_bias_b, d_skip, ssm_norm_w, w_out, norm2_w, w_up, ffn_conv_w, ffn_conv_b, w_down, final_norm_w, loss_target, m_norm1_w, m_w_in, m_ssm_conv_w, m_ssm_conv_b, m_a_log_f, m_a_log_b, m_dt_bias_f, m_dt_bias_b, m_d_skip, m_ssm_norm_w, m_w_out, m_norm2_w, m_w_up, m_ffn_conv_w, m_ffn_conv_b, m_w_down, m_final_norm_w, v_norm1_w, v_w_in, v_ssm_conv_w, v_ssm_conv_b, v_a_log_f, v_a_log_b, v_dt_bias_f, v_dt_bias_b, v_d_skip, v_ssm_norm_w, v_w_out, v_norm2_w, v_w_up, v_ffn_conv_w, v_ffn_conv_b, v_w_down, v_final_norm_w):
    w = dict(norm1_w=norm1_w, w_in=w_in, ssm_conv_w=ssm_conv_w, ssm_conv_b=ssm_conv_b, a_log_f=a_log_f, a_log_b=a_log_b,
             dt_bias_f=dt_bias_f, dt_bias_b=dt_bias_b, d_skip=d_skip, ssm_norm_w=ssm_norm_w, w_out=w_out, norm2_w=norm2_w,
             w_up=w_up, ffn_conv_w=ffn_conv_w, ffn_conv_b=ffn_conv_b, w_down=w_down, final_norm_w=final_norm_w)
    mo = dict(norm1_w=m_norm1_w, w_in=m_w_in, ssm_conv_w=m_ssm_conv_w, ssm_conv_b=m_ssm_conv_b, a_log_f=m_a_log_f, a_log_b=m_a_log_b,
              dt_bias_f=m_dt_bias_f, dt_bias_b=m_dt_bias_b, d_skip=m_d_skip, ssm_norm_w=m_ssm_norm_w, w_out=m_w_out, norm2_w=m_norm2_w,
              w_up=m_w_up, ffn_conv_w=m_ffn_conv_w, ffn_conv_b=m_ffn_conv_b, w_down=m_w_down, final_norm_w=m_final_norm_w)
    vo = dict(norm1_w=v_norm1_w, w_in=v_w_in, ssm_conv_w=v_ssm_conv_w, ssm_conv_b=v_ssm_conv_b, a_log_f=v_a_log_f, a_log_b=v_a_log_b,
              dt_bias_f=v_dt_bias_f, dt_bias_b=v_dt_bias_b, d_skip=v_d_skip, ssm_norm_w=v_ssm_norm_w, w_out=v_w_out, norm2_w=v_norm2_w,
              w_up=v_w_up, ffn_conv_w=v_ffn_conv_w, ffn_conv_b=v_ffn_conv_b, w_down=v_w_down, final_norm_w=v_final_norm_w)
    me = _my_index()

    shards = [w["w_in"][0].astype(BF16), w["w_out"][0].astype(BF16), w["w_up"][0].astype(BF16), w["w_down"][0].astype(BF16),
              _pack([w["ssm_conv_w"][0], w["ffn_conv_w"][0]])]
    g_in, g_out, g_up, g_down, g_conv = _all_gather(shards, "weights_all_gather")
    n_in, n_up = w_in.shape[2], w_up.shape[2]
    conv_rows = [_unpack(g_conv[i], [ssm_conv_w.shape[1:], ffn_conv_w.shape[1:]]) for i in range(N_DEV)]
    full = {
        "w_in": g_in.transpose(1, 0, 2).reshape(D_MODEL, N_DEV * n_in),
        "w_out": g_out.reshape(2 * D_MODEL, D_MODEL),
        "w_up": g_up.transpose(1, 0, 2).reshape(D_MODEL, N_DEV * n_up),
        "w_down": g_down.reshape(D_FF, D_MODEL),
        "ssm_conv_w": jnp.concatenate([c[0] for c in conv_rows], axis=0),
        "ffn_conv_w": jnp.concatenate([c[1] for c in conv_rows], axis=0),
    }
    for k in ("norm1_w", "ssm_conv_b", "a_log_f", "a_log_b", "dt_bias_f", "dt_bias_b", "d_skip", "ssm_norm_w", "norm2_w", "ffn_conv_b"):
        full[k] = w[k]
    full["final_norm_w"] = w["final_norm_w"]

    loss_part, grad_x, big, small = _local_step(x[0], loss_target[0], full)

    small_shapes = [(1,)] + [small[k].shape for k in SMALL_ORDER]
    packed = _pack([loss_part] + [small[k] for k in SMALL_ORDER])
    out_small = jnp.broadcast_to(packed[None], (N_DEV,) + packed.shape)
    r_in, r_small = _exchange([_to_shards(big["w_in"], 1), out_small], "grads_exchange")
    r_out, r_up, r_down = big["w_out"], big["w_up"], big["w_down"]

    outs_g, outs_d, outs_m, outs_v = {}, {}, {}, {}
    for k, parts in zip(BIG_ORDER, (r_in, r_out, r_up, r_down)):
        g, dlt, nm, nv = _adamw(parts, w[k][0], mo[k][0], vo[k][0], f"adamw_{k}")
        outs_g[k], outs_d[k], outs_m[k], outs_v[k] = g[None], dlt[None], nm[None], nv[None]
    tot = _unpack(_sum_parts(r_small, "small_grads_sum"), small_shapes)
    loss = tot[0][0]
    gs = dict(zip(SMALL_ORDER, tot[1:]))
    g_own = {}
    for k in SMALL_ORDER:
        if k in SHARDED_SMALL:
            rows = w[k].shape[1]
            g_own[k] = lax.dynamic_slice_in_dim(gs[k], me * rows, rows, axis=0)[None]
        else:
            g_own[k] = gs[k].reshape(w[k].shape)
    shapes = [w[k].shape for k in SMALL_ORDER]
    d_s, m_s, v_s = _adamw_small(_pack([g_own[k] for k in SMALL_ORDER]), _pack([w[k] for k in SMALL_ORDER]),
                                 _pack([mo[k] for k in SMALL_ORDER]), _pack([vo[k] for k in SMALL_ORDER]), "adamw_small")
    for k, a, b, c in zip(SMALL_ORDER, _unpack(d_s, shapes), _unpack(m_s, shapes), _unpack(v_s, shapes)):
        outs_g[k], outs_d[k], outs_m[k], outs_v[k] = g_own[k], a, b, c

    order = ("norm1_w", "w_in", "ssm_conv_w", "ssm_conv_b", "a_log_f", "a_log_b", "dt_bias_f", "dt_bias_b", "d_skip", "ssm_norm_w",
             "w_out", "norm2_w", "w_up", "ffn_conv_w", "ffn_conv_b", "w_down", "final_norm_w")
    return (loss, grad_x[None], *[outs_g[k] for k in order], *[outs_d[k] for k in order],
            *[outs_m[k] for k in order], *[outs_v[k] for k in order])
```

```python
import math

import numpy as np
import jax
import jax.numpy as jnp
from jax import lax
from jax.experimental import pallas as pl
from jax.experimental.pallas import tpu as pltpu

F32 = jnp.float32
BF16 = jnp.bfloat16
SDS = jax.ShapeDtypeStruct

N_DEV = 8
D_MODEL = 1024
N_HEADS = 16
HEAD_DIM = 64
ROPE_DIM = 16
ROPE_THETA = 500000.0
DILATIONS = (1, 4, 16)
BAND_HALF = 64
D_INNER = 1024
N_GROUPS = 4
D_STATE = 128
CHUNK = 128
XBC = D_INNER + 2 * N_GROUPS * D_STATE
D_FF = 2816
MAIN_W = 3 * D_MODEL + D_INNER + XBC
EPS = 1e-6
LR, B1, B2, AEPS, WD, STEP = 0.001, 0.9, 0.999, 1e-08, 0.01, 10
NEG = -1e30
VMEM_LIMIT = 56 * 1024 * 1024
MESH = pl.DeviceIdType.MESH
HIGH = lax.Precision.HIGHEST
NT = (((1,), (1,)), ((), ()))
TN = (((0,), (0,)), ((), ()))


def _cp(*sem):
    return pltpu.CompilerParams(dimension_semantics=sem, vmem_limit_bytes=VMEM_LIMIT)


def _pick(n, cands):
    for c in cands:
        if n % c == 0:
            return c
    raise ValueError(f"no tile for {n}")


def _sigmoid(x):
    return 1.0 / (1.0 + jnp.exp(-x))


def _softplus(x):
    return jnp.maximum(x, 0.0) + jnp.log1p(jnp.exp(-jnp.abs(x)))


def _slab(s, width, dtype, slab, n_in, out_idx=0):
    if slab is None:
        return SDS((s, width), dtype), 0, [], [], {}
    total, col0, into = slab
    if into is None:
        return SDS((s, total), dtype), col0, [], [], {}
    return SDS((s, total), dtype), col0, [into], [pl.BlockSpec(memory_space=pl.ANY)], {n_in: out_idx}


def _matmul(a, b, *, name, trans_b=False, out_dtype=BF16, residual=None, exchange=()):
    m, k = a.shape
    n = b.shape[0] if trans_b else b.shape[1]
    tk = k if k <= 2048 else _pick(k, (2048, 1408, 1024, 512))
    nk = k // tk
    if nk == 1:
        tm = _pick(m, (2048, 1408, 1024, 512, 256, 128))
        tn = _pick(n, (512, 256, 128))
    else:
        tm = _pick(m, (1024, 1408, 512, 256, 128))
        tn = _pick(n, (1024, 1408, 512, 256, 128))
    dn = NT if trans_b else (((1,), (0,)), ((), ()))
    nx = len(exchange)
    n_in = 2 + (residual is not None)
    grid = (m // tm, n // tn, nk)

    def body(*refs):
        a_ref, b_ref = refs[0], refs[1]
        xch_src = refs[n_in:n_in + nx]
        o_ref = refs[n_in + nx]
        xch_dst = refs[n_in + nx + 1:n_in + 2 * nx + 1]
        acc = refs[n_in + 2 * nx + 1]
        xch_sems = refs[n_in + 2 * nx + 2:]
        i, j, kk = pl.program_id(0), pl.program_id(1), pl.program_id(2)

        if nx:
            @pl.when((i == 0) & (j == 0) & (kk == 0))
            def _():
                _exchange_start(xch_src, xch_dst, *xch_sems)

        @pl.when(kk == 0)
        def _():
            acc[...] = jnp.zeros_like(acc)

        acc[...] += lax.dot_general(a_ref[...], b_ref[...], dn, preferred_element_type=F32)

        @pl.when(kk == nk - 1)
        def _():
            r = acc[...]
            if residual is not None:
                r = r + refs[2][...].astype(F32)
            o_ref[...] = r.astype(o_ref.dtype)

        if nx:
            @pl.when((i == grid[0] - 1) & (j == grid[1] - 1) & (kk == nk - 1))
            def _():
                _exchange_wait(xch_src, xch_dst, *xch_sems)

    in_specs = [pl.BlockSpec((tm, tk), lambda i, j, kk: (i, kk)),
                pl.BlockSpec((tn, tk), lambda i, j, kk: (j, kk)) if trans_b else pl.BlockSpec((tk, tn), lambda i, j, kk: (kk, j))]
    args = [a, b]
    if residual is not None:
        in_specs.append(pl.BlockSpec((tm, tn), lambda i, j, kk: (i, j)))
        args.append(residual)
    any_spec = pl.BlockSpec(memory_space=pl.ANY)
    out = pl.pallas_call(
        body, grid=grid, in_specs=in_specs + [any_spec] * nx,
        out_specs=[pl.BlockSpec((tm, tn), lambda i, j, kk: (i, j))] + [any_spec] * nx,
        out_shape=[SDS((m, n), out_dtype)] + [SDS(e.shape, e.dtype) for e in exchange],
        scratch_shapes=[pltpu.VMEM((tm, tn), F32)] + (_exchange_sems(nx) if nx else []),
        name=name, compiler_params=_cp(*(("arbitrary",) * 3 if nx else ("parallel", "parallel", "arbitrary"))))(*args, *exchange)
    return out if nx else out[0]


def _rmsnorm_fwd(x, w, name):
    s, d = x.shape
    tm = _pick(s, (512, 128))

    def body(x_ref, w_ref, o_ref):
        xf = x_ref[...]
        r = lax.rsqrt(jnp.mean(xf * xf, axis=-1, keepdims=True) + EPS)
        o_ref[...] = (xf * r * w_ref[...]).astype(o_ref.dtype)

    return pl.pallas_call(
        body, grid=(s // tm,), in_specs=[pl.BlockSpec((tm, d), lambda i: (i, 0)), pl.BlockSpec((1, d), lambda i: (0, 0))],
        out_specs=pl.BlockSpec((tm, d), lambda i: (i, 0)), out_shape=SDS((s, d), BF16),
        name=name, compiler_params=_cp("parallel"))(x, w)


def _rmsnorm_bwd(x, w, dh, dres, name):
    s, d = x.shape
    tm = _pick(s, (512, 128))

    def body(x_ref, w_ref, dh_ref, dres_ref, dx_ref, dxb_ref, dw_ref):
        xf = x_ref[...]
        r = lax.rsqrt(jnp.mean(xf * xf, axis=-1, keepdims=True) + EPS)
        xhat = xf * r
        dhf = dh_ref[...].astype(F32)
        g = dhf * w_ref[...]
        dx = dres_ref[...] + r * (g - xhat * jnp.mean(g * xhat, axis=-1, keepdims=True))
        dx_ref[...] = dx
        dxb_ref[...] = dx.astype(dxb_ref.dtype)

        @pl.when(pl.program_id(0) == 0)
        def _():
            dw_ref[...] = jnp.zeros_like(dw_ref)

        dw_ref[...] += jnp.sum(dhf * xhat, axis=0, keepdims=True)

    row = pl.BlockSpec((tm, d), lambda i: (i, 0))
    vec = pl.BlockSpec((1, d), lambda i: (0, 0))
    return pl.pallas_call(
        body, grid=(s // tm,), in_specs=[row, vec, row, row], out_specs=[row, row, vec],
        out_shape=[SDS((s, d), F32), SDS((s, d), BF16), SDS((1, d), F32)], name=name, compiler_params=_cp("arbitrary"))(x, w, dh, dres)


def _final_norm_loss(x, w, target, name):
    s, d = x.shape
    tm = _pick(s, (512, 128))

    def body(x_ref, w_ref, t_ref, dx_ref, dxb_ref, dw_ref, loss_ref):
        xf = x_ref[...]
        r = lax.rsqrt(jnp.mean(xf * xf, axis=-1, keepdims=True) + EPS)
        xhat = xf * r
        wv = w_ref[...]
        e = xhat * wv - t_ref[...]
        dy = e * (1.0 / d)
        g = dy * wv
        dx = r * (g - xhat * jnp.mean(g * xhat, axis=-1, keepdims=True))
        dx_ref[...] = dx
        dxb_ref[...] = dx.astype(dxb_ref.dtype)

        @pl.when(pl.program_id(0) == 0)
        def _():
            dw_ref[...] = jnp.zeros_like(dw_ref)
            loss_ref[...] = jnp.zeros_like(loss_ref)

        dw_ref[...] += jnp.sum(dy * xhat, axis=0, keepdims=True)
        loss_ref[...] += jnp.sum(jnp.sum(e * e, axis=1, keepdims=True), axis=0, keepdims=True) * (0.5 / d)

    row = pl.BlockSpec((tm, d), lambda i: (i, 0))
    vec = pl.BlockSpec((1, d), lambda i: (0, 0))
    return pl.pallas_call(
        body, grid=(s // tm,), in_specs=[row, vec, row], out_specs=[row, row, vec, pl.BlockSpec((1, 128), lambda i: (0, 0))],
        out_shape=[SDS((s, d), F32), SDS((s, d), BF16), SDS((1, d), F32), SDS((1, 128), F32)],
        name=name, compiler_params=_cp("arbitrary"))(x, w, target)


def _rope_tables(s):
    half = ROPE_DIM // 2
    f32 = np.float32
    inv_freq = np.power(f32(ROPE_THETA), -np.arange(half, dtype=f32) * f32(2.0) / f32(ROPE_DIM)).astype(f32)
    ang = (np.arange(s, dtype=f32)[:, None] * inv_freq[None, :]).astype(f32)
    cos, sin = np.cos(ang).astype(f32), np.sin(ang).astype(f32)
    z = np.zeros((s, HEAD_DIM - ROPE_DIM), f32)
    zh = np.zeros((s, half), f32)
    c = np.concatenate([cos, cos, z + 1.0], axis=1)
    sa = np.concatenate([zh, sin, z], axis=1)
    sb = np.concatenate([-sin, zh, z], axis=1)
    two = lambda t: np.concatenate([t, t], axis=1)
    c, sa, sb = two(c), two(sa), two(sb)
    fwd, bwd = (c, sa, sb), (c, np.roll(sb, half, axis=1), np.roll(sa, -half, axis=1))
    return tuple(jnp.asarray(t) for t in fwd), tuple(jnp.asarray(t) for t in bwd)


def _rope_fwd(proj, tabs, name):
    s = proj.shape[0]
    tm = _pick(s, (256, 128))
    half = ROPE_DIM // 2
    wb = D_MODEL

    def body(x_ref, c_ref, sa_ref, sb_ref, o_ref):
        c, sa, sb = c_ref[...], sa_ref[...], sb_ref[...]
        for j in range(wb // 128):
            x = x_ref[:, j * 128:(j + 1) * 128].astype(F32)
            o_ref[:, j * 128:(j + 1) * 128] = (x * c + pltpu.roll(x, half, 1) * sa + pltpu.roll(x, 128 - half, 1) * sb).astype(o_ref.dtype)

    blk = pl.BlockSpec((tm, wb), lambda i, j: (i, j))
    tab = pl.BlockSpec((tm, 128), lambda i, j: (i, 0))
    return pl.pallas_call(
        body, grid=(s // tm, 2 * D_MODEL // wb), in_specs=[blk, tab, tab, tab], out_specs=blk,
        out_shape=SDS((s, 2 * D_MODEL), BF16), name=name, compiler_params=_cp("parallel", "parallel"))(proj, *tabs)


def _sum3_rope(d1, d2, d3, tabs, name, slab=None):
    s, w = d1.shape
    tm = _pick(s, (256, 128))
    half = ROPE_DIM // 2

    def body(*refs):
        for j in range(w // 128):
            ls = slice(j * 128, (j + 1) * 128)
            x = refs[0][:, ls].astype(F32) + refs[1][:, ls].astype(F32) + refs[2][:, ls].astype(F32)
            if tabs is not None:
                x = x * refs[3][...] + pltpu.roll(x, half, 1) * refs[4][...] + pltpu.roll(x, 128 - half, 1) * refs[5][...]
            refs[-1][:, ls] = x.astype(refs[-1].dtype)

    blk = pl.BlockSpec((tm, w), lambda i: (i, 0))
    tab = pl.BlockSpec((tm, 128), lambda i: (i, 0))
    extra = [] if tabs is None else list(tabs)
    out_shape, col0, more, more_specs, alias = _slab(s, w, BF16, slab, 3 + len(extra))
    cb = col0 // w
    return pl.pallas_call(
        body, grid=(s // tm,), in_specs=[blk, blk, blk] + [tab] * len(extra) + more_specs,
        out_specs=pl.BlockSpec((tm, w), lambda i: (i, cb)), out_shape=out_shape, input_output_aliases=alias,
        name=name, compiler_params=_cp("parallel"))(d1, d2, d3, *extra, *more)


def _band_valid(t, nq, nk, qofs, kofs, seq_len):
    qpos = t * 128 + qofs + lax.broadcasted_iota(jnp.int32, (nq, nk), 0)
    kpos = t * 128 + kofs + lax.broadcasted_iota(jnp.int32, (nq, nk), 1)
    sh = int(math.log2(seq_len))
    same = lax.shift_right_arithmetic(qpos, sh) == lax.shift_right_arithmetic(kpos, sh)
    return same & (jnp.abs(kpos - qpos) <= BAND_HALF)


def _window(r0, r1, r2):
    return jnp.concatenate([r0[128 - BAND_HALF:128], r1[...], r2[0:BAND_HALF]], axis=0)


WIN = 128 + 2 * BAND_HALF


def _win_specs(width, col, nt):
    return [pl.BlockSpec((128, width), lambda t: (jnp.maximum(t - 1, 0), col)),
            pl.BlockSpec((128, width), lambda t: (t, col)),
            pl.BlockSpec((128, width), lambda t: (jnp.minimum(t + 1, nt - 1), col))]


def _attn_fwd(qk, v_src, v_col, seq_len, name):
    s = qk.shape[0]
    nt = s // 128
    dm = D_MODEL

    def body(q_ref, k0, k1, k2, v0, v1, v2, o_ref, lse_ref):
        t = pl.program_id(0)
        valid = _band_valid(t, 128, WIN, 0, -BAND_HALF, seq_len)
        q = q_ref[...]
        kc = _window(k0, k1, k2)
        vc = _window(v0, v1, v2)
        outs, lses = [], []
        for h in range(N_HEADS):
            sl = slice(h * HEAD_DIM, (h + 1) * HEAD_DIM)
            sc = lax.dot_general(q[:, sl], kc[:, sl], NT, preferred_element_type=F32) * (HEAD_DIM ** -0.5)
            sc = jnp.where(valid, sc, NEG)
            m = jnp.max(sc, axis=1, keepdims=True)
            e = jnp.exp(sc - m)
            den = jnp.sum(e, axis=1, keepdims=True)
            o = jnp.dot(e.astype(BF16), vc[:, sl], preferred_element_type=F32) / den
            outs.append(o)
            lses.append(m + jnp.log(den))
        o_ref[...] = jnp.concatenate(outs, axis=1).astype(o_ref.dtype)
        lse_ref[...] = jnp.concatenate(lses, axis=1)

    in_specs = [pl.BlockSpec((128, dm), lambda t: (t, 0))] + _win_specs(dm, 1, nt) + _win_specs(dm, v_col, nt)
    return pl.pallas_call(
        body, grid=(nt,), in_specs=in_specs,
        out_specs=[pl.BlockSpec((128, dm), lambda t: (t, 0)), pl.BlockSpec((128, N_HEADS), lambda t: (t, 0))],
        out_shape=[SDS((s, dm), BF16), SDS((s, N_HEADS), F32)], name=name, compiler_params=_cp("parallel"))(
            qk, qk, qk, qk, v_src, v_src, v_src)


def _attn_combine(os_, lses, expand, name):
    s, dm = os_[0].shape
    tm = _pick(s, (256, 128))

    def body(o1, o2, o3, l1, l2, l3, e_ref, out_ref, lt_ref):
        ls = [l1[...], l2[...], l3[...]]
        m = jnp.maximum(jnp.maximum(ls[0], ls[1]), ls[2])
        es = [jnp.exp(l - m) for l in ls]
        tot = es[0] + es[1] + es[2]
        lt_ref[...] = m + jnp.log(tot)
        acc = jnp.zeros((tm, dm), F32)
        for e, o in zip(es, (o1, o2, o3)):
            acc = acc + jnp.dot(e / tot, e_ref[...], precision=HIGH, preferred_element_type=F32) * o[...].astype(F32)
        out_ref[...] = acc.astype(out_ref.dtype)

    row = pl.BlockSpec((tm, dm), lambda i: (i, 0))
    st = pl.BlockSpec((tm, N_HEADS), lambda i: (i, 0))
    return pl.pallas_call(
        body, grid=(s // tm,), in_specs=[row, row, row, st, st, st, pl.BlockSpec((N_HEADS, dm), lambda i: (0, 0))],
        out_specs=[row, st], out_shape=[SDS((s, dm + D_INNER), BF16), SDS((s, N_HEADS), F32)],
        name=name, compiler_params=_cp("parallel"))(*os_, *lses, expand)


def _attn_delta(dmix, attn, expand_t, name):
    s, dm = attn.shape[0], D_MODEL
    tm = _pick(s, (256, 128))

    def body(d_ref, a_ref, e_ref, o_ref):
        prod = d_ref[...].astype(F32) * a_ref[...].astype(F32)
        o_ref[...] = jnp.dot(prod, e_ref[...], precision=HIGH, preferred_element_type=F32)

    row = pl.BlockSpec((tm, dm), lambda i: (i, 0))
    return pl.pallas_call(
        body, grid=(s // tm,), in_specs=[row, row, pl.BlockSpec((dm, N_HEADS), lambda i: (0, 0))],
        out_specs=pl.BlockSpec((tm, N_HEADS), lambda i: (i, 0)), out_shape=SDS((s, N_HEADS), F32),
        name=name, compiler_params=_cp("parallel"))(dmix, attn, expand_t)


def _attn_bwd_dq(qk, v_src, v_col, do_src, lse, delta, seq_len, name):
    s = qk.shape[0]
    nt = s // 128
    dm = D_MODEL

    def body(q_ref, k0, k1, k2, v0, v1, v2, do_ref, lse_ref, dl_ref, dq_ref):
        t = pl.program_id(0)
        valid = _band_valid(t, 128, WIN, 0, -BAND_HALF, seq_len)
        q = q_ref[...]
        do = do_ref[...]
        kc = _window(k0, k1, k2)
        vc = _window(v0, v1, v2)
        lse_v, dl_v = lse_ref[...], dl_ref[...]
        outs = []
        for h in range(N_HEADS):
            sl = slice(h * HEAD_DIM, (h + 1) * HEAD_DIM)
            sc = lax.dot_general(q[:, sl], kc[:, sl], NT, preferred_element_type=F32) * (HEAD_DIM ** -0.5)
            p = jnp.exp(jnp.where(valid, sc - lse_v[:, h:h + 1], NEG))
            dp = lax.dot_general(do[:, sl], vc[:, sl], NT, preferred_element_type=F32)
            ds = p * (dp - dl_v[:, h:h + 1])
            outs.append(jnp.dot(ds.astype(BF16), kc[:, sl], preferred_element_type=F32) * (HEAD_DIM ** -0.5))
        dq_ref[...] = jnp.concatenate(outs, axis=1).astype(dq_ref.dtype)

    row = pl.BlockSpec((128, dm), lambda t: (t, 0))
    st = pl.BlockSpec((128, N_HEADS), lambda t: (t, 0))
    in_specs = [row] + _win_specs(dm, 1, nt) + _win_specs(dm, v_col, nt) + [row, st, st]
    return pl.pallas_call(
        body, grid=(nt,), in_specs=in_specs, out_specs=row, out_shape=SDS((s, dm), BF16),
        name=name, compiler_params=_cp("parallel"))(qk, qk, qk, qk, v_src, v_src, v_src, do_src, lse, delta)


def _attn_bwd_dkv(qk, v_src, v_col, do_src, lse_t, delta_t, seq_len, name):
    s = qk.shape[0]
    nt = s // 128
    dm = D_MODEL

    def lane_window(r0, r1, r2):
        return jnp.concatenate([r0[:, 128 - BAND_HALF:128], r1[...], r2[:, 0:BAND_HALF]], axis=1)

    def body(k_ref, v_ref, q0, q1, q2, d0, d1, d2, l0, l1, l2, e0, e1, e2, dk_ref, dv_ref):
        t = pl.program_id(0)
        valid = _band_valid(t, 128, WIN, 0, -BAND_HALF, seq_len)
        k = k_ref[...]
        v = v_ref[...]
        qc = _window(q0, q1, q2)
        dc = _window(d0, d1, d2)
        lse_v = lane_window(l0, l1, l2)
        dl_v = lane_window(e0, e1, e2)
        dks, dvs = [], []
        for h in range(N_HEADS):
            sl = slice(h * HEAD_DIM, (h + 1) * HEAD_DIM)
            sc = lax.dot_general(k[:, sl], qc[:, sl], NT, preferred_element_type=F32) * (HEAD_DIM ** -0.5)
            p = jnp.exp(jnp.where(valid, sc - lse_v[h:h + 1, :], NEG))
            dvs.append(jnp.dot(p.astype(BF16), dc[:, sl], preferred_element_type=F32))
            dp = lax.dot_general(v[:, sl], dc[:, sl], NT, preferred_element_type=F32)
            ds = p * (dp - dl_v[h:h + 1, :])
            dks.append(jnp.dot(ds.astype(BF16), qc[:, sl], preferred_element_type=F32) * (HEAD_DIM ** -0.5))
        dk_ref[...] = jnp.concatenate(dks, axis=1).astype(dk_ref.dtype)
        dv_ref[...] = jnp.concatenate(dvs, axis=1).astype(dv_ref.dtype)

    row = pl.BlockSpec((128, dm), lambda t: (t, 0))
    stat = [pl.BlockSpec((N_HEADS, 128), lambda t: (0, jnp.maximum(t - 1, 0))), pl.BlockSpec((N_HEADS, 128), lambda t: (0, t)),
            pl.BlockSpec((N_HEADS, 128), lambda t: (0, jnp.minimum(t + 1, nt - 1)))]
    in_specs = ([pl.BlockSpec((128, dm), lambda t: (t, 1)), pl.BlockSpec((128, dm), lambda t: (t, v_col))]
                + _win_specs(dm, 0, nt) + _win_specs(dm, 0, nt) + stat + stat)
    return pl.pallas_call(
        body, grid=(nt,), in_specs=in_specs, out_specs=[row, row], out_shape=[SDS((s, dm), BF16), SDS((s, dm), BF16)],
        name=name, compiler_params=_cp("parallel"))(qk, v_src, qk, qk, qk, do_src, do_src, do_src, lse_t, lse_t, lse_t, delta_t, delta_t, delta_t)


CONV_COLS = (1024, 1408, 512, 256)


def _halo_specs(tm, tc, col0, nrow_blocks):
    r = tm // 16
    return [pl.BlockSpec((16, tc), lambda i, j: (jnp.maximum(i * r - 1, 0), col0 + j)),
            pl.BlockSpec((16, tc), lambda i, j: (jnp.minimum((i + 1) * r, nrow_blocks * r - 1), col0 + j))]


def _shifted(x_ref, hp_ref, hn_ref, i, last):
    x = x_ref[...].astype(F32)
    tm = x.shape[0]
    rows = lax.broadcasted_iota(jnp.int32, x.shape, 0)
    prev_row = jnp.where(i > 0, hp_ref[15:16, :].astype(F32), 0.0)
    next_row = jnp.where(i < last, hn_ref[0:1, :].astype(F32), 0.0)
    xp = jnp.where(rows == 0, prev_row, pltpu.roll(x, 1, 0))
    xn = jnp.where(rows == tm - 1, next_row, pltpu.roll(x, tm - 1, 0))
    return xp, x, xn


def _conv(x_src, col0, width, w3, bias, act, name, out_dtype=BF16, slab=None, transpose=False, wcol0=0):
    s = x_src.shape[0]
    tm = _pick(s, (256, 128))
    tc = _pick(width, CONV_COLS)
    nb = s // tm
    c0 = col0 // tc
    wc0 = wcol0 // tc

    def body(*refs):
        x_ref, hp_ref, hn_ref, w_ref, b_ref = refs[:5]
        o_ref = refs[-1]
        i = pl.program_id(0)
        xp, x, xn = _shifted(x_ref, hp_ref, hn_ref, i, nb - 1)
        w = w_ref[...]
        if transpose:
            y = w[2:3, :] * xp + w[1:2, :] * x + w[0:1, :] * xn
        else:
            y = w[0:1, :] * xp + w[1:2, :] * x + w[2:3, :] * xn + b_ref[...]
        if act:
            y = y * _sigmoid(y)
        o_ref[...] = y.astype(o_ref.dtype)

    in_specs = ([pl.BlockSpec((tm, tc), lambda i, j: (i, c0 + j))] + _halo_specs(tm, tc, c0, nb)
                + [pl.BlockSpec((3, tc), lambda i, j: (0, wc0 + j)), pl.BlockSpec((1, tc), lambda i, j: (0, wc0 + j))])
    out_shape, ocol, more, more_specs, alias = _slab(s, width, out_dtype, slab, 5)
    ob = ocol // tc
    return pl.pallas_call(
        body, grid=(nb, width // tc), in_specs=in_specs + more_specs, out_specs=pl.BlockSpec((tm, tc), lambda i, j: (i, ob + j)),
        out_shape=out_shape, input_output_aliases=alias, name=name, compiler_params=_cp("parallel", "parallel"))(
            x_src, x_src, x_src, w3, bias, *more)


def _conv_silu_bwd(x_src, col0, width, w3, bias, addends, add_widths, name):
    s = x_src.shape[0]
    tm = _pick(s, (256, 128))
    tc = _pick(width, CONV_COLS)
    nb = s // tm
    c0 = col0 // tc
    na = len(addends)

    def body(*refs):
        x_ref, hp_ref, hn_ref, w_ref, b_ref = refs[:5]
        a_refs = refs[5:5 + na]
        dp_ref, dw_ref, db_ref = refs[5 + na:]
        i, j = pl.program_id(1), pl.program_id(0)
        xp, x, xn = _shifted(x_ref, hp_ref, hn_ref, i, nb - 1)
        w = w_ref[...]
        pre = w[0:1, :] * xp + w[1:2, :] * x + w[2:3, :] * xn + b_ref[...]
        g = jnp.zeros_like(pre)
        for a_ref, aw in zip(a_refs, add_widths):
            av = a_ref[...].astype(F32)
            g = g + (av if aw == width else jnp.where(j < aw // tc, av, 0.0))
        sg = _sigmoid(pre)
        dpre = g * (sg * (1.0 + pre * (1.0 - sg)))
        dp_ref[...] = dpre.astype(dp_ref.dtype)

        @pl.when(i == 0)
        def _():
            dw_ref[...] = jnp.zeros_like(dw_ref)
            db_ref[...] = jnp.zeros_like(db_ref)

        dw_ref[...] += jnp.concatenate([jnp.sum(dpre * xp, axis=0, keepdims=True), jnp.sum(dpre * x, axis=0, keepdims=True),
                                        jnp.sum(dpre * xn, axis=0, keepdims=True)], axis=0)
        db_ref[...] += jnp.sum(dpre, axis=0, keepdims=True)

    r = tm // 16
    in_specs = [pl.BlockSpec((tm, tc), lambda j, i: (i, c0 + j)),
                pl.BlockSpec((16, tc), lambda j, i: (jnp.maximum(i * r - 1, 0), c0 + j)),
                pl.BlockSpec((16, tc), lambda j, i: (jnp.minimum((i + 1) * r, nb * r - 1), c0 + j)),
                pl.BlockSpec((3, tc), lambda j, i: (0, j)), pl.BlockSpec((1, tc), lambda j, i: (0, j))]
    for aw in add_widths:
        nblk = aw // tc
        in_specs.append(pl.BlockSpec((tm, tc), lambda j, i, nblk=nblk: (i, jnp.minimum(j, nblk - 1))))
    return pl.pallas_call(
        body, grid=(width // tc, nb), in_specs=in_specs,
        out_specs=[pl.BlockSpec((tm, tc), lambda j, i: (i, j)), pl.BlockSpec((3, tc), lambda j, i: (0, j)), pl.BlockSpec((1, tc), lambda j, i: (0, j))],
        out_shape=[SDS((s, width), BF16), SDS((3, width), F32), SDS((1, width), F32)],
        name=name, compiler_params=_cp("parallel", "arbitrary"))(x_src, x_src, x_src, w3, bias, *addends)


def _ffn_gate_fwd(u, w3, bias, name):
    s = u.shape[0]
    tm = _pick(s, (256, 128))
    tc = _pick(D_FF, CONV_COLS)
    nb = s // tm
    nj = D_FF // tc

    def body(g_ref, gp, gn, u_ref, up, un, wg_ref, wu_ref, bg_ref, bu_ref, o_ref):
        i = pl.program_id(0)
        outs = []
        for (x_ref, hp, hn, w_ref, b_ref) in ((g_ref, gp, gn, wg_ref, bg_ref), (u_ref, up, un, wu_ref, bu_ref)):
            xp, x, xn = _shifted(x_ref, hp, hn, i, nb - 1)
            w = w_ref[...]
            outs.append(w[0:1, :] * xp + w[1:2, :] * x + w[2:3, :] * xn + b_ref[...])
        gate, upv = outs
        o_ref[...] = (gate * _sigmoid(gate) * upv).astype(o_ref.dtype)

    def xspecs(c0):
        return [pl.BlockSpec((tm, tc), lambda i, j: (i, c0 + j))] + _halo_specs(tm, tc, c0, nb)

    in_specs = (xspecs(0) + xspecs(nj)
                + [pl.BlockSpec((3, tc), lambda i, j: (0, j)), pl.BlockSpec((3, tc), lambda i, j: (0, nj + j)),
                   pl.BlockSpec((1, tc), lambda i, j: (0, j)), pl.BlockSpec((1, tc), lambda i, j: (0, nj + j))])
    return pl.pallas_call(
        body, grid=(nb, nj), in_specs=in_specs, out_specs=pl.BlockSpec((tm, tc), lambda i, j: (i, j)),
        out_shape=SDS((s, D_FF), BF16), name=name, compiler_params=_cp("parallel", "parallel"))(u, u, u, u, u, u, w3, w3, bias, bias)


def _ffn_gate_bwd(u, w3, bias, dact, name):
    s = u.shape[0]
    tm = _pick(s, (256, 128))
    tc = _pick(D_FF, CONV_COLS)
    nb = s // tm
    nj = D_FF // tc

    def body(g_ref, gp, gn, u_ref, up, un, wg_ref, wu_ref, bg_ref, bu_ref, da_ref, dg_ref, du_ref, dwg_ref, dwu_ref, dbg_ref, dbu_ref):
        i = pl.program_id(1)
        sh, pre = [], []
        for (x_ref, hp, hn, w_ref, b_ref) in ((g_ref, gp, gn, wg_ref, bg_ref), (u_ref, up, un, wu_ref, bu_ref)):
            xs3 = _shifted(x_ref, hp, hn, i, nb - 1)
            w = w_ref[...]
            sh.append(xs3)
            pre.append(w[0:1, :] * xs3[0] + w[1:2, :] * xs3[1] + w[2:3, :] * xs3[2] + b_ref[...])
        gate, upv = pre
        da = da_ref[...].astype(F32)
        sg = _sigmoid(gate)
        dgate = da * upv * (sg * (1.0 + gate * (1.0 - sg)))
        dup = da * gate * sg
        dg_ref[...] = dgate.astype(dg_ref.dtype)
        du_ref[...] = dup.astype(du_ref.dtype)

        @pl.when(i == 0)
        def _():
            for r in (dwg_ref, dwu_ref, dbg_ref, dbu_ref):
                r[...] = jnp.zeros_like(r)

        for d, xs3, dw_ref, db_ref in ((dgate, sh[0], dwg_ref, dbg_ref), (dup, sh[1], dwu_ref, dbu_ref)):
            dw_ref[...] += jnp.concatenate([jnp.sum(d * xs3[0], axis=0, keepdims=True), jnp.sum(d * xs3[1], axis=0, keepdims=True),
                                            jnp.sum(d * xs3[2], axis=0, keepdims=True)], axis=0)
            db_ref[...] += jnp.sum(d, axis=0, keepdims=True)

    r = tm // 16

    def xspecs(c0):
        return [pl.BlockSpec((tm, tc), lambda j, i: (i, c0 + j)),
                pl.BlockSpec((16, tc), lambda j, i: (jnp.maximum(i * r - 1, 0), c0 + j)),
                pl.BlockSpec((16, tc), lambda j, i: (jnp.minimum((i + 1) * r, nb * r - 1), c0 + j))]

    in_specs = (xspecs(0) + xspecs(nj)
                + [pl.BlockSpec((3, tc), lambda j, i: (0, j)), pl.BlockSpec((3, tc), lambda j, i: (0, nj + j)),
                   pl.BlockSpec((1, tc), lambda j, i: (0, j)), pl.BlockSpec((1, tc), lambda j, i: (0, nj + j)),
                   pl.BlockSpec((tm, tc), lambda j, i: (i, j))])
    blk = pl.BlockSpec((tm, tc), lambda j, i: (i, j))
    w_o = pl.BlockSpec((3, tc), lambda j, i: (0, j))
    b_o = pl.BlockSpec((1, tc), lambda j, i: (0, j))
    return pl.pallas_call(
        body, grid=(nj, nb), in_specs=in_specs, out_specs=[blk, blk, w_o, w_o, b_o, b_o],
        out_shape=[SDS((s, D_FF), BF16), SDS((s, D_FF), BF16), SDS((3, D_FF), F32), SDS((3, D_FF), F32), SDS((1, D_FF), F32), SDS((1, D_FF), F32)],
        name=name, compiler_params=_cp("parallel", "arbitrary"))(u, u, u, u, u, u, w3, w3, bias, bias, dact)


def _exchange_sems(n):
    return [pltpu.SemaphoreType.DMA((7 * n,)), pltpu.SemaphoreType.DMA((7 * n,)), pltpu.SemaphoreType.DMA((n,))]


def _exchange_copies(srcs, outs, send_sems, recv_sems, local_sems):
    x, y, c = lax.axis_index("x"), lax.axis_index("y"), lax.axis_index("c")
    me = 4 * x + 2 * y + c
    locals_ = [pltpu.make_async_copy(srcs[a].at[me], outs[a].at[me], local_sems.at[a]) for a in range(len(srcs))]
    sends, recvs = [], []
    for k in range(1, N_DEV):
        px, py, pc = x ^ ((k >> 2) & 1), y ^ ((k >> 1) & 1), c ^ (k & 1)
        peer = 4 * px + 2 * py + pc
        for a in range(len(srcs)):
            sems = dict(send_sem=send_sems.at[a * 7 + k - 1], recv_sem=recv_sems.at[a * 7 + k - 1], device_id_type=MESH)
            sends.append(pltpu.make_async_remote_copy(src_ref=srcs[a].at[peer], dst_ref=outs[a].at[me], device_id=(px, py, pc), **sems))
            recvs.append(pltpu.make_async_remote_copy(src_ref=srcs[a].at[peer], dst_ref=outs[a].at[peer], device_id=(x, y, c), **sems))
    return locals_, sends, recvs


def _exchange_start(srcs, outs, send_sems, recv_sems, local_sems):
    locals_, sends, _ = _exchange_copies(srcs, outs, send_sems, recv_sems, local_sems)
    for cp in locals_ + sends:
        cp.start()


def _exchange_wait(srcs, outs, send_sems, recv_sems, local_sems):
    locals_, sends, recvs = _exchange_copies(srcs, outs, send_sems, recv_sems, local_sems)
    for cp in recvs:
        cp.wait_recv()
    for cp in sends:
        cp.wait_send()
    for cp in locals_:
        cp.wait()


def _gather_copies(srcs, outs, send_sems, recv_sems, local_sems):
    x, y, c = lax.axis_index("x"), lax.axis_index("y"), lax.axis_index("c")
    me, sibling = (x, y, c), (x, y, 1 - c)
    chips = [(1 - x, y), (x, 1 - y), (1 - x, 1 - y)]

    def copy(a, k, block, to, src=None):
        dst = outs[a].at[4 * block[0] + 2 * block[1] + block[2]]
        return pltpu.make_async_remote_copy(
            src_ref=dst if src is None else src, dst_ref=dst,
            send_sem=send_sems.at[a * 7 + k], recv_sem=recv_sems.at[a * 7 + k], device_id=to, device_id_type=MESH)

    n = len(srcs)
    locals_ = [pltpu.make_async_copy(srcs[a], outs[a].at[4 * x + 2 * y + c], local_sems.at[a]) for a in range(n)]
    own = [copy(a, 0, me, sibling, src=srcs[a]) for a in range(n)]
    own += [copy(a, 1 + j, me, (*chip, c), src=srcs[a]) for a in range(n) for j, chip in enumerate(chips)]
    landed_ici = [copy(a, 1 + j, (*chip, c), me) for j, chip in enumerate(chips) for a in range(n)]
    passed = [copy(a, 4 + j, (*chip, c), sibling) for j, chip in enumerate(chips) for a in range(n)]
    landed_d2d = [copy(a, 0, sibling, me) for a in range(n)]
    landed_d2d += [copy(a, 4 + j, (*chip, 1 - c), me) for a in range(n) for j, chip in enumerate(chips)]
    return locals_, own, landed_ici, passed, landed_d2d


def _gather_start(*refs):
    locals_, own, _, _, _ = _gather_copies(*refs)
    for cp in locals_ + own:
        cp.start()


def _gather_forward(*refs):
    _, _, landed_ici, passed, _ = _gather_copies(*refs)
    for arrived, onward in zip(landed_ici, passed):
        arrived.wait_recv()
        onward.start()


def _gather_finish(*refs):
    locals_, own, _, passed, landed_d2d = _gather_copies(*refs)
    for cp in landed_d2d:
        cp.wait_recv()
    for cp in own + passed:
        cp.wait_send()
    for cp in locals_:
        cp.wait()


def _ssd_common(dt_ref, dtt_ref, al_r, al_c, bi_r, bi_c, off, rev):
    li = lax.broadcasted_iota(jnp.int32, (CHUNK, CHUNK), 0)
    si = lax.broadcasted_iota(jnp.int32, (CHUNK, CHUNK), 1)
    mask = (li <= si) if rev else (li >= si)
    mask_t = (li >= si) if rev else (li <= si)
    a_r = -jnp.exp(al_r[...])
    a_c = -jnp.exp(al_c[...])
    pre = dt_ref[:, off:off + N_HEADS] + bi_r[...]
    dt = _softplus(pre)
    cs = jnp.dot(mask.astype(F32), dt * a_r, precision=HIGH, preferred_element_type=F32)
    dt_t = _softplus(dtt_ref[off:off + N_HEADS, :] + bi_c[...])
    cs_t = jnp.dot(dt_t * a_c, mask_t.astype(F32), precision=HIGH, preferred_element_type=F32)
    tot = cs[0:1, :] if rev else cs[CHUNK - 1:CHUNK, :]
    return mask, mask_t, a_r, pre, dt, cs, cs_t, tot


def _ssd_fwd(xbc, dt_raw, dt_t, args_f, args_b, name, gather=()):
    s = xbc.shape[0]
    nc = s // CHUNK
    hp = D_INNER // N_HEADS
    hpg = N_HEADS // N_GROUPS
    gs = N_GROUPS * D_STATE
    ng = len(gather)
    fwd_step = (nc * 11) // 16

    def chunk(x_ref, b_ref, c_ref, dt_ref, dtt_ref, alr, alc, bir, bic, y_ref, st_ref, h_scr, off, rev):
        mask, _, _, _, dt, cs, cs_t, tot = _ssd_common(dt_ref, dtt_ref, alr, alc, bir, bic, off, rev)
        xs = x_ref[...]
        ys = []
        for g in range(N_GROUPS):
            bg = b_ref[:, g * D_STATE:(g + 1) * D_STATE]
            cg = c_ref[:, g * D_STATE:(g + 1) * D_STATE]
            gm = lax.dot_general(cg, bg, NT, preferred_element_type=F32)
            hcat = h_scr[g]
            st_ref[0, g] = hcat
            ch = lax.dot_general(cg, hcat.astype(BF16), NT, preferred_element_type=F32)
            xdd = []
            for r in range(hpg):
                h = g * hpg + r
                csc, csr, tot_h = cs[:, h:h + 1], cs_t[h:h + 1, :], tot[:, h:h + 1]
                lm = jnp.exp(jnp.where(mask, csc - csr, NEG))
                xdf = xs[:, h * hp:(h + 1) * hp].astype(F32) * dt[:, h:h + 1]
                y = jnp.dot((gm * lm).astype(BF16), xdf.astype(BF16), preferred_element_type=F32)
                ys.append(y + jnp.exp(csc) * ch[:, r * hp:(r + 1) * hp])
                xdd.append((xdf * jnp.exp(tot_h - csc)).astype(BF16))
            snew = lax.dot_general(jnp.concatenate(xdd, axis=1), bg, TN, preferred_element_type=F32)
            for r in range(hpg):
                rs = slice(r * hp, (r + 1) * hp)
                h_scr[g, rs, :] = jnp.exp(tot[:, g * hpg + r:g * hpg + r + 1]) * hcat[rs] + snew[rs]
        y_ref[...] = jnp.concatenate(ys, axis=1)

    def body(*refs):
        in_f, in_b = refs[0:9], refs[9:18]
        g_src = refs[18:18 + ng]
        out_f, out_b = refs[18 + ng:20 + ng], refs[20 + ng:22 + ng]
        g_dst = refs[22 + ng:22 + 2 * ng]
        hs_f, hs_b = refs[22 + 2 * ng], refs[23 + 2 * ng]
        g_sems = refs[24 + 2 * ng:]
        step = pl.program_id(0)

        @pl.when(step == 0)
        def _():
            hs_f[...] = jnp.zeros_like(hs_f)
            hs_b[...] = jnp.zeros_like(hs_b)
            if ng:
                _gather_start(g_src, g_dst, *g_sems)

        chunk(*in_f, *out_f, hs_f, 0, False)
        chunk(*in_b, *out_b, hs_b, N_HEADS, True)

        if ng:
            @pl.when(step == fwd_step)
            def _():
                _gather_forward(g_src, g_dst, *g_sems)

            @pl.when(step == nc - 1)
            def _():
                _gather_finish(g_src, g_dst, *g_sems)

    small = lambda shape: pl.BlockSpec(shape, lambda c: (0, 0))

    def specs(cm):
        ins = [pl.BlockSpec((CHUNK, D_INNER), lambda c: (cm(c), 0)),
               pl.BlockSpec((CHUNK, gs), lambda c: (cm(c), D_INNER // gs)),
               pl.BlockSpec((CHUNK, gs), lambda c: (cm(c), D_INNER // gs + 1)),
               pl.BlockSpec((CHUNK, 128), lambda c: (cm(c), 0)),
               pl.BlockSpec((2 * N_HEADS, CHUNK), lambda c: (0, cm(c))),
               small((1, N_HEADS)), small((N_HEADS, 1)), small((1, N_HEADS)), small((N_HEADS, 1))]
        outs = [pl.BlockSpec((CHUNK, D_INNER), lambda c: (cm(c), 0)),
                pl.BlockSpec((1, N_GROUPS, hpg * hp, D_STATE), lambda c: (cm(c), 0, 0, 0))]
        return ins, outs

    ins_f, outs_f = specs(lambda c: c)
    ins_b, outs_b = specs(lambda c: nc - 1 - c)
    any_spec = pl.BlockSpec(memory_space=pl.ANY)
    one_dir = [SDS((s, D_INNER), F32), SDS((nc, N_GROUPS, hpg * hp, D_STATE), F32)]
    state = pltpu.VMEM((N_GROUPS, hpg * hp, D_STATE), F32)
    return pl.pallas_call(
        body, grid=(nc,), in_specs=ins_f + ins_b + [any_spec] * ng, out_specs=outs_f + outs_b + [any_spec] * ng,
        out_shape=one_dir + one_dir + [SDS((N_DEV,) + g.shape, g.dtype) for g in gather],
        scratch_shapes=[state, state] + (_exchange_sems(ng) if ng else []), name=name, compiler_params=_cp("arbitrary"))(
            xbc, xbc, xbc, dt_raw, dt_t, *args_f, xbc, xbc, xbc, dt_raw, dt_t, *args_b, *gather)


def _ssd_bwd(xbc, dt_raw, dt_t, al_r, al_c, bi_r, bi_c, states, dy, off, rev, name, exchange=()):
    s = xbc.shape[0]
    nc = s // CHUNK
    hp = D_INNER // N_HEADS
    gs = N_GROUPS * D_STATE
    hpg = N_HEADS // N_GROUPS
    cm = (lambda c: c) if rev else (lambda c: nc - 1 - c)
    nx = len(exchange)

    def body(*refs):
        x_ref, b_ref, c_ref, dt_ref, dtt_ref, alr, alc, bir, bic, st_ref, dy_ref = refs[:11]
        xch_src = refs[11:11 + nx]
        dx_ref, ddt_ref, dal_ref, dbi_ref = refs[11 + nx:15 + nx]
        xch_dst = refs[15 + nx:15 + 2 * nx]
        dh_scr = refs[15 + 2 * nx]
        xch_sems = refs[16 + 2 * nx:]

        @pl.when(pl.program_id(0) == 0)
        def _():
            dh_scr[...] = jnp.zeros_like(dh_scr)
            dal_ref[...] = jnp.zeros_like(dal_ref)
            dbi_ref[...] = jnp.zeros_like(dbi_ref)
            if nx:
                _exchange_start(xch_src, xch_dst, *xch_sems)

        if nx:
            @pl.when(pl.program_id(0) == nc - 1)
            def _():
                _exchange_wait(xch_src, xch_dst, *xch_sems)

        mask, mask_t, a_r, pre, dt, cs, cs_t, tot = _ssd_common(dt_ref, dtt_ref, alr, alc, bir, bic, off, rev)
        xs = x_ref[...]
        dyv = dy_ref[...]
        rows = lax.broadcasted_iota(jnp.int32, (CHUNK, 1), 0)
        end_row = (rows == 0) if rev else (rows == CHUNK - 1)
        lane_h = lax.broadcasted_iota(jnp.int32, (1, N_HEADS), 1)
        sub_h = lax.broadcasted_iota(jnp.int32, (N_HEADS, 1), 0)
        dcs_all = jnp.zeros((CHUNK, N_HEADS), F32)
        colw_all = jnp.zeros((N_HEADS, CHUNK), F32)
        dxsum_all = jnp.zeros((CHUNK, N_HEADS), F32)
        dxs, dbs, dcs_out = [], [], []
        for g in range(N_GROUPS):
            bg = b_ref[:, g * D_STATE:(g + 1) * D_STATE]
            cg = c_ref[:, g * D_STATE:(g + 1) * D_STATE]
            gm = lax.dot_general(cg, bg, NT, preferred_element_type=F32)
            hcat = st_ref[0, g]
            dhcat = dh_scr[g]
            hb, dhb = hcat.astype(BF16), dhcat.astype(BF16)
            ch = lax.dot_general(cg, hb, NT, preferred_element_type=F32)
            z = lax.dot_general(bg, dhb, NT, preferred_element_type=F32)
            dg_sum = jnp.zeros((CHUNK, CHUNK), F32)
            dchs, xdds, t_hs = [], [], []
            for r in range(hpg):
                h = g * hpg + r
                rs = slice(r * hp, (r + 1) * hp)
                csc, csr, tot_h = cs[:, h:h + 1], cs_t[h:h + 1, :], tot[:, h:h + 1]
                lm = jnp.exp(jnp.where(mask, csc - csr, NEG))
                xh = xs[:, h * hp:(h + 1) * hp].astype(F32)
                dtc = dt[:, h:h + 1]
                xdf = xh * dtc
                xd = xdf.astype(BF16)
                dyh = dyv[:, h * hp:(h + 1) * hp]
                dyb = dyh.astype(BF16)
                m = gm * lm
                ecs = jnp.exp(csc)
                dec = jnp.exp(tot_h - csc)
                t_h = jnp.exp(tot_h)
                dxd = lax.dot_general(m.astype(BF16), dyb, TN, preferred_element_type=F32)
                dm = lax.dot_general(dyb, xd, NT, preferred_element_type=F32)
                dg_sum = dg_sum + dm * lm
                w = dm * m
                dcs = jnp.sum(dyh * (ecs * ch[:, rs]), axis=1, keepdims=True)
                dchs.append((ecs * dyh).astype(BF16))
                dxd = dxd + dec * z[:, rs]
                ddec = jnp.sum(xdf * z[:, rs], axis=1, keepdims=True) * dec
                xdds.append((xdf * dec).astype(BF16))
                dtot = (jnp.sum(jnp.sum(dhcat[rs] * hcat[rs], axis=1, keepdims=True), axis=0, keepdims=True) * t_h
                        + jnp.sum(ddec, axis=0, keepdims=True))
                t_hs.append(t_h)
                dcs = dcs + jnp.sum(w, axis=1, keepdims=True) - ddec
                dcs = dcs + jnp.where(end_row, dtot, 0.0)
                colw_all = colw_all + (sub_h == h).astype(F32) * jnp.sum(w, axis=0, keepdims=True)
                onehot = (lane_h == h).astype(F32)
                dcs_all = dcs_all + dcs * onehot
                dxsum_all = dxsum_all + jnp.sum(dxd * xh, axis=1, keepdims=True) * onehot
                dxs.append(dxd * dtc)
            dgs = dg_sum.astype(BF16)
            dchc = jnp.concatenate(dchs, axis=1)
            dcs_out.append(jnp.dot(dgs, bg, preferred_element_type=F32) + jnp.dot(dchc, hb, preferred_element_type=F32))
            dbs.append(lax.dot_general(dgs, cg, TN, preferred_element_type=F32)
                       + jnp.dot(jnp.concatenate(xdds, axis=1), dhb, preferred_element_type=F32))
            dh_in = lax.dot_general(dchc, cg, TN, preferred_element_type=F32)
            for r in range(hpg):
                rs = slice(r * hp, (r + 1) * hp)
                dh_scr[g, rs, :] = dh_in[rs] + t_hs[r] * dhcat[rs]
        dx_ref[...] = jnp.concatenate(dxs + dbs + dcs_out, axis=1)
        mt = mask_t.astype(F32)
        da = (jnp.dot(mt, dcs_all, precision=HIGH, preferred_element_type=F32)
              - lax.dot_general(mt, colw_all, NT, precision=HIGH, preferred_element_type=F32))
        dal_ref[...] += jnp.sum(da * dt, axis=0, keepdims=True) * a_r
        ddt_raw = (da * a_r + dxsum_all) * _sigmoid(pre)
        ddt_ref[...] = ddt_raw
        dbi_ref[...] += jnp.sum(ddt_raw, axis=0, keepdims=True)

    small = lambda shape: pl.BlockSpec(shape, lambda c: (0, 0))
    in_specs = [pl.BlockSpec((CHUNK, D_INNER), lambda c: (cm(c), 0)),
                pl.BlockSpec((CHUNK, gs), lambda c: (cm(c), D_INNER // gs)),
                pl.BlockSpec((CHUNK, gs), lambda c: (cm(c), D_INNER // gs + 1)),
                pl.BlockSpec((CHUNK, 128), lambda c: (cm(c), 0)),
                pl.BlockSpec((2 * N_HEADS, CHUNK), lambda c: (0, cm(c))),
                small((1, N_HEADS)), small((N_HEADS, 1)), small((1, N_HEADS)), small((N_HEADS, 1)),
                pl.BlockSpec((1, N_GROUPS, hpg * hp, D_STATE), lambda c: (cm(c), 0, 0, 0)),
                pl.BlockSpec((CHUNK, D_INNER), lambda c: (cm(c), 0))]
    any_spec = pl.BlockSpec(memory_space=pl.ANY)
    return pl.pallas_call(
        body, grid=(nc,), in_specs=in_specs + [any_spec] * nx,
        out_specs=[pl.BlockSpec((CHUNK, XBC), lambda c: (cm(c), 0)), pl.BlockSpec((CHUNK, N_HEADS), lambda c: (cm(c), 0)),
                   small((1, N_HEADS)), small((1, N_HEADS))] + [any_spec] * nx,
        out_shape=[SDS((s, XBC), F32), SDS((s, N_HEADS), F32), SDS((1, N_HEADS), F32), SDS((1, N_HEADS), F32)]
        + [SDS(a.shape, a.dtype) for a in exchange],
        scratch_shapes=[pltpu.VMEM((N_GROUPS, hpg * hp, D_STATE), F32)] + (_exchange_sems(nx) if nx else []),
        name=name, compiler_params=_cp("arbitrary"))(xbc, xbc, xbc, dt_raw, dt_t, al_r, al_c, bi_r, bi_c, states, dy, *exchange)


def _gate_fwd(yf, yb, xbc, proj, dskip_x, norm_w, mix, name):
    s = yf.shape[0]
    tm = _pick(s, (256, 128))
    gw = D_INNER // N_GROUPS
    zc = 3 * D_MODEL // D_INNER

    def body(yf_ref, yb_ref, x_ref, z_ref, d_ref, w_ref, _, o_ref):
        y = yf_ref[...] + yb_ref[...] + d_ref[...] * x_ref[...].astype(F32)
        z = z_ref[...].astype(F32)
        gt = y * (z * _sigmoid(z))
        outs = []
        for g in range(N_GROUPS):
            gg = gt[:, g * gw:(g + 1) * gw]
            outs.append(gg * lax.rsqrt(jnp.mean(gg * gg, axis=-1, keepdims=True) + EPS))
        o_ref[...] = (jnp.concatenate(outs, axis=1) * w_ref[...]).astype(o_ref.dtype)

    row = pl.BlockSpec((tm, D_INNER), lambda i: (i, 0))
    vec = pl.BlockSpec((1, D_INNER), lambda i: (0, 0))
    out_shape, ocol, more, more_specs, alias = _slab(s, D_INNER, BF16, (mix.shape[1], mix.shape[1] - D_INNER, mix), 6)
    return pl.pallas_call(
        body, grid=(s // tm,), in_specs=[row, row, row, pl.BlockSpec((tm, D_INNER), lambda i: (i, zc)), vec, vec] + more_specs,
        out_specs=pl.BlockSpec((tm, D_INNER), lambda i: (i, ocol // D_INNER)), out_shape=out_shape, input_output_aliases=alias,
        name=name, compiler_params=_cp("parallel"))(yf, yb, xbc, proj, dskip_x, norm_w, *more)


def _gate_bwd(yf, yb, xbc, proj, dskip_x, norm_w, dmix, dproj, name):
    s = yf.shape[0]
    tm = _pick(s, (256, 128))
    gw = D_INNER // N_GROUPS
    zc = 3 * D_MODEL // D_INNER

    def body(yf_ref, yb_ref, x_ref, z_ref, d_ref, w_ref, do_ref, _, dy_ref, dz_ref, dxs_ref, dw_ref, dd_ref):
        xf = x_ref[...].astype(F32)
        y = yf_ref[...] + yb_ref[...] + d_ref[...] * xf
        z = z_ref[...].astype(F32)
        sg = _sigmoid(z)
        sz = z * sg
        gt = y * sz
        do = do_ref[...].astype(F32)
        dgh = do * w_ref[...]
        ghs, dgts = [], []
        for g in range(N_GROUPS):
            gg = gt[:, g * gw:(g + 1) * gw]
            r = lax.rsqrt(jnp.mean(gg * gg, axis=-1, keepdims=True) + EPS)
            gh = gg * r
            dg = dgh[:, g * gw:(g + 1) * gw]
            ghs.append(gh)
            dgts.append(r * (dg - gh * jnp.mean(dg * gh, axis=-1, keepdims=True)))
        ghat = jnp.concatenate(ghs, axis=1)
        dgt = jnp.concatenate(dgts, axis=1)
        dy = dgt * sz
        dy_ref[...] = dy
        dz_ref[...] = (dgt * y * (sg * (1.0 + z * (1.0 - sg)))).astype(dz_ref.dtype)
        dxs_ref[...] = dy * d_ref[...]

        @pl.when(pl.program_id(0) == 0)
        def _():
            dw_ref[...] = jnp.zeros_like(dw_ref)
            dd_ref[...] = jnp.zeros_like(dd_ref)

        dw_ref[...] += jnp.sum(do * ghat, axis=0, keepdims=True)
        dd_ref[...] += jnp.sum(dy * xf, axis=0, keepdims=True)

    row = pl.BlockSpec((tm, D_INNER), lambda i: (i, 0))
    vec = pl.BlockSpec((1, D_INNER), lambda i: (0, 0))
    dz_shape, _, more, more_specs, alias = _slab(s, D_INNER, BF16, (dproj.shape[1], zc * D_INNER, dproj), 7, out_idx=1)
    return pl.pallas_call(
        body, grid=(s // tm,),
        in_specs=[row, row, row, pl.BlockSpec((tm, D_INNER), lambda i: (i, zc)), vec, vec, pl.BlockSpec((tm, D_INNER), lambda i: (i, 1))] + more_specs,
        out_specs=[row, pl.BlockSpec((tm, D_INNER), lambda i: (i, zc)), row, vec, vec],
        out_shape=[SDS((s, D_INNER), F32), dz_shape, SDS((s, D_INNER), F32), SDS((1, D_INNER), F32), SDS((1, D_INNER), F32)],
        input_output_aliases=alias, name=name, compiler_params=_cp("arbitrary"))(yf, yb, xbc, proj, dskip_x, norm_w, dmix, *more)


def _adamw(parts, w, m, v, name):
    r, c = w.shape
    tr = _pick(r, (256, 352, 128))

    def body(p_ref, w_ref, m_ref, v_ref, g_ref, d_ref, nm_ref, nv_ref):
        g = p_ref[0].astype(F32)
        for i in range(1, N_DEV):
            g = g + p_ref[i].astype(F32)
        mn = B1 * m_ref[...] + (1.0 - B1) * g
        vn = B2 * v_ref[...] + (1.0 - B2) * (g * g)
        m_hat = mn / (1.0 - B1 ** STEP)
        v_hat = vn / (1.0 - B2 ** STEP)
        g_ref[...] = g
        d_ref[...] = -LR * (m_hat / (jnp.sqrt(v_hat) + AEPS) + WD * w_ref[...])
        nm_ref[...] = mn
        nv_ref[...] = vn

    blk = pl.BlockSpec((tr, c), lambda i: (i, 0))
    return pl.pallas_call(
        body, grid=(r // tr,), in_specs=[pl.BlockSpec((N_DEV, tr, c), lambda i: (0, i, 0)), blk, blk, blk],
        out_specs=[blk, blk, blk, blk], out_shape=[SDS((r, c), F32)] * 4, name=name, compiler_params=_cp("parallel"))(parts, w, m, v)


def _sum_parts(parts, name):
    _, r, c = parts.shape

    def body(p_ref, o_ref):
        g = p_ref[0]
        for i in range(1, N_DEV):
            g = g + p_ref[i]
        o_ref[...] = g

    return pl.pallas_call(body, out_shape=SDS((r, c), F32), name=name)(parts)


def _adamw_small(g, w, m, v, name):
    def body(g_ref, w_ref, m_ref, v_ref, d_ref, nm_ref, nv_ref):
        gv = g_ref[...]
        mn = B1 * m_ref[...] + (1.0 - B1) * gv
        vn = B2 * v_ref[...] + (1.0 - B2) * (gv * gv)
        m_hat = mn / (1.0 - B1 ** STEP)
        v_hat = vn / (1.0 - B2 ** STEP)
        d_ref[...] = -LR * (m_hat / (jnp.sqrt(v_hat) + AEPS) + WD * w_ref[...])
        nm_ref[...] = mn
        nv_ref[...] = vn

    return pl.pallas_call(body, out_shape=[SDS(g.shape, F32)] * 3, name=name)(g, w, m, v)


def _my_index():
    return 4 * lax.axis_index("x") + 2 * lax.axis_index("y") + lax.axis_index("c")


def _all_gather(shards, name):
    n = len(shards)

    def body(*refs):
        srcs, outs = refs[:n], refs[n:2 * n]
        _gather_start(srcs, outs, *refs[2 * n:])
        _gather_forward(srcs, outs, *refs[2 * n:])
        _gather_finish(srcs, outs, *refs[2 * n:])

    any_spec = pl.BlockSpec(memory_space=pl.ANY)
    return pl.pallas_call(
        body, in_specs=[any_spec] * n, out_specs=[any_spec] * n,
        out_shape=[SDS((N_DEV,) + s.shape, s.dtype) for s in shards], scratch_shapes=_exchange_sems(n), name=name)(*shards)


def _exchange(arrays, name):
    n = len(arrays)

    def body(*refs):
        srcs, outs = refs[:n], refs[n:2 * n]
        _exchange_start(srcs, outs, *refs[2 * n:])
        _exchange_wait(srcs, outs, *refs[2 * n:])

    any_spec = pl.BlockSpec(memory_space=pl.ANY)
    return pl.pallas_call(
        body, in_specs=[any_spec] * n, out_specs=[any_spec] * n,
        out_shape=[SDS(a.shape, a.dtype) for a in arrays], scratch_shapes=_exchange_sems(n), name=name)(*arrays)


def _to_pattern(t, d):
    if d == 1:
        return t
    s, w = t.shape
    return t.reshape(s // d, d, w).transpose(1, 0, 2).reshape(s, w)


def _from_pattern(t, d):
    if d == 1:
        return t
    s, w = t.shape
    return t.reshape(d, s // d, w).transpose(1, 0, 2).reshape(s, w)


def _pad_lanes(t, n):
    return jnp.pad(t, ((0, 0), (0, n - t.shape[1])))


def _to_shards(g, axis):
    r, c = g.shape
    if axis == 0:
        return g.reshape(N_DEV, r // N_DEV, c)
    return g.reshape(r, N_DEV, c // N_DEV).transpose(1, 0, 2)


def _local_step(x, target, p, late_shards=(), early_exchange=True):
    s = x.shape[0]
    tabs_f, tabs_b = _rope_tables(s)
    expand = jnp.asarray(np.repeat(np.eye(N_HEADS, dtype=np.float32), HEAD_DIM, axis=1))
    w_main, w_dt = p["w_in"][:, :MAIN_W], _pad_lanes(p["w_in"][:, MAIN_W:], 128)
    al_r = {"f": p["a_log_f"], "b": p["a_log_b"]}
    bi_r = {"f": p["dt_bias_f"], "b": p["dt_bias_b"]}
    dskip_x = jnp.repeat(p["d_skip"], D_INNER // N_HEADS, axis=1)
    ssm_w3, ffn_w3 = p["ssm_conv_w"].T, p["ffn_conv_w"].T

    h1 = _rmsnorm_fwd(x, p["norm1_w"], "norm1_fwd")
    proj = _matmul(h1, w_main, name="in_proj")
    dt_raw = _matmul(h1, w_dt, name="in_proj_dt", out_dtype=F32)
    dt_t = dt_raw[:, :2 * N_HEADS].T
    qk = _rope_fwd(proj, tabs_f, "rope_fwd")
    v_col = 2
    pat = []
    for d in DILATIONS:
        qk_p = _to_pattern(qk, d)
        v_p = proj if d == 1 else _to_pattern(proj[:, 2 * D_MODEL:3 * D_MODEL], d)
        pat.append((qk_p, v_p, v_col if d == 1 else 0))
    os_, lses = [], []
    for d, (qk_p, v_p, vc) in zip(DILATIONS, pat):
        o_p, lse_p = _attn_fwd(qk_p, v_p, vc, s // d, f"attn_fwd_d{d}")
        os_.append(_from_pattern(o_p, d))
        lses.append(_from_pattern(lse_p, d))
    mix, lse_tot = _attn_combine(os_, lses, expand, "attn_combine")

    xbc = _conv(proj, 3 * D_MODEL + D_INNER, XBC, ssm_w3, p["ssm_conv_b"], True, "ssm_conv_fwd")
    col = lambda r: r.reshape(N_HEADS, 1)
    ssd_args = {k: (al_r[k], col(al_r[k]), bi_r[k], col(bi_r[k])) for k in ("f", "b")}
    yf, st_f, yb, st_b, *got = _ssd_fwd(xbc, dt_raw, dt_t, ssd_args["f"], ssd_args["b"], "ssd_fwd", gather=late_shards)
    if late_shards:
        p = dict(p, w_out=got[0].reshape(2 * D_MODEL, D_MODEL), w_down=got[2].reshape(D_FF, D_MODEL),
                 w_up=got[1].transpose(1, 0, 2).reshape(D_MODEL, 2 * D_FF))
    mix = _gate_fwd(yf, yb, xbc, proj, dskip_x, p["ssm_norm_w"], mix, "ssm_gate_fwd")

    x2 =_matmul(mix, p["w_out"], name="out_proj", out_dtype=F32, residual=x)
    h2 = _rmsnorm_fwd(x2, p["norm2_w"], "norm2_fwd")
    u = _matmul(h2, p["w_up"], name="ffn_up")
    act = _ffn_gate_fwd(u, ffn_w3, p["ffn_conv_b"], "ffn_gate_fwd")
    x3 = _matmul(act, p["w_down"], name="ffn_down", out_dtype=F32, residual=x2)

    dx3, dx3b, g_final, loss = _final_norm_loss(x3, p["final_norm_w"].reshape(1, D_MODEL), target, "final_norm_loss")
    g_w_down = _matmul(act.T, dx3b, name="dw_down")
    dact = _matmul(dx3b, p["w_down"], name="d_act", trans_b=True)
    dug, duu, dwg, dwu, dbg, dbu = _ffn_gate_bwd(u, ffn_w3, p["ffn_conv_b"], dact, "ffn_gate_bwd")
    du = _conv(dug, 0, D_FF, ffn_w3, p["ffn_conv_b"], False, "ffn_conv_bwd_gate", slab=(2 * D_FF, 0, None), transpose=True)
    du = _conv(duu, 0, D_FF, ffn_w3, p["ffn_conv_b"], False, "ffn_conv_bwd_up", slab=(2 * D_FF, D_FF, du), transpose=True, wcol0=D_FF)
    g_w_up = _matmul(h2.T, du, name="dw_up")
    dh2 = _matmul(du, p["w_up"], name="d_h2", trans_b=True)
    dx2, dx2b, g_norm2 = _rmsnorm_bwd(x2, p["norm2_w"], dh2, dx3, "norm2_bwd")
    g_w_out = _matmul(mix.T, dx2b, name="dw_out")
    dmix = _matmul(dx2b, p["w_out"], name="d_mix", trans_b=True)

    delta = _attn_delta(dmix, mix, expand.T, "attn_delta")
    dq_u, dk_u, dv_u = [], [], []
    for d, (qk_p, v_p, vc) in zip(DILATIONS, pat):
        do_p = dmix if d == 1 else _to_pattern(dmix[:, :D_MODEL], d)
        lse_p, dl_p = _to_pattern(lse_tot, d), _to_pattern(delta, d)
        dq = _attn_bwd_dq(qk_p, v_p, vc, do_p, lse_p, dl_p, s // d, f"attn_bwd_dq_d{d}")
        dk, dv = _attn_bwd_dkv(qk_p, v_p, vc, do_p, lse_p.T, dl_p.T, s // d, f"attn_bwd_dkv_d{d}")
        dq_u.append(_from_pattern(dq, d))
        dk_u.append(_from_pattern(dk, d))
        dv_u.append(_from_pattern(dv, d))
    dproj = _sum3_rope(*dq_u, tabs_b, "rope_bwd_q", slab=(MAIN_W, 0, None))
    dproj = _sum3_rope(*dk_u, tabs_b, "rope_bwd_k", slab=(MAIN_W, D_MODEL, dproj))
    dproj = _sum3_rope(*dv_u, None, "sum_dv", slab=(MAIN_W, 2 * D_MODEL, dproj))

    dy, dproj, dxs_skip, g_ssm_norm, g_dskip_lanes = _gate_bwd(yf, yb, xbc, proj, dskip_x, p["ssm_norm_w"], dmix, dproj, "ssm_gate_bwd")
    early = [_to_shards(g_w_out, 0), _to_shards(g_w_up, 1), _to_shards(g_w_down, 0)] if early_exchange else []
    dxbc_f, ddt_f, g_al_f, g_bi_f, *got = _ssd_bwd(xbc, dt_raw, dt_t, *ssd_args["f"], st_f, dy, 0, False, "ssd_bwd_f", exchange=early)
    if early_exchange:
        g_w_out, g_w_up, g_w_down = got
    dxbc_b, ddt_b, g_al_b, g_bi_b = _ssd_bwd(xbc, dt_raw, dt_t, *ssd_args["b"], st_b, dy, N_HEADS, True, "ssd_bwd_b")
    dpre, g_ssm_w3, g_ssm_cb = _conv_silu_bwd(proj, 3 * D_MODEL + D_INNER, XBC, ssm_w3, p["ssm_conv_b"],
                                              [dxbc_f, dxbc_b, dxs_skip], [XBC, XBC, D_INNER], "ssm_conv_bwd")
    dproj = _conv(dpre, 0, XBC, ssm_w3, p["ssm_conv_b"], False, "ssm_conv_bwd_x", transpose=True,
                  slab=(MAIN_W, 3 * D_MODEL + D_INNER, dproj))

    ddt =_pad_lanes(jnp.concatenate([ddt_f, ddt_b], axis=1), 128).astype(BF16)
    h1t = h1.T
    g_w_main = _matmul(h1t, dproj, name="dw_in")
    g_w_dt = _matmul(h1t, ddt, name="dw_in_dt")
    g_w_in = jnp.concatenate([g_w_main, g_w_dt[:, :2 * N_HEADS]], axis=1)
    if early_exchange:
        dh1, g_w_in = _matmul(dproj, w_main, name="d_h1", trans_b=True, out_dtype=F32, exchange=[_to_shards(g_w_in, 1)])
    else:
        dh1 = _matmul(dproj, w_main, name="d_h1", trans_b=True, out_dtype=F32)
    dh1 = _matmul(ddt, w_dt, name="d_h1_dt", trans_b=True, out_dtype=F32, residual=dh1)
    grad_x, _, g_norm1 = _rmsnorm_bwd(x, p["norm1_w"], dh1, dx2, "norm1_bwd")

    g_dskip = jnp.sum(g_dskip_lanes.reshape(N_HEADS, D_INNER // N_HEADS), axis=1).reshape(1, N_HEADS)
    small = {
        "norm1_w": g_norm1, "ssm_conv_w": g_ssm_w3.T, "ssm_conv_b": g_ssm_cb, "a_log_f": g_al_f, "a_log_b": g_al_b,
        "dt_bias_f": g_bi_f, "dt_bias_b": g_bi_b, "d_skip": g_dskip, "ssm_norm_w": g_ssm_norm, "norm2_w": g_norm2,
        "ffn_conv_w": jnp.concatenate([dwg, dwu], axis=1).T, "ffn_conv_b": jnp.concatenate([dbg, dbu], axis=1), "final_norm_w": g_final,
    }
    big = {"w_in": g_w_in, "w_out": g_w_out, "w_up": g_w_up, "w_down": g_w_down}
    return loss[0, 0], grad_x, big, small


SMALL_ORDER = ("norm1_w", "ssm_conv_w", "ssm_conv_b", "a_log_f", "a_log_b", "dt_bias_f", "dt_bias_b", "d_skip",
               "ssm_norm_w", "norm2_w", "ffn_conv_w", "ffn_conv_b", "final_norm_w")
SHARDED_SMALL = ("ssm_conv_w", "ffn_conv_w")
BIG_ORDER = ("w_in", "w_out", "w_up", "w_down")


def _pack(vals):
    rows = []
    for v in vals:
        f = v.reshape(-1).astype(F32)
        n = -(-f.shape[0] // 128) * 128
        rows.append(jnp.pad(f, (0, n - f.shape[0])).reshape(-1, 128))
    out = jnp.concatenate(rows, axis=0)
    pad = -out.shape[0] % 8
    return jnp.pad(out, ((0, pad), (0, 0)))


def _unpack(packed, shapes):
    out, r = [], 0
    for shp in shapes:
        n = math.prod(shp)
        nr = -(-n // 128)
        out.append(packed[r:r + nr].reshape(-1)[:n].reshape(shp))
        r += nr
    return out


def kernel(x, norm1_w, w_in, ssm_conv_w, ssm_conv_b, a_log_f, a_log_b, dt_bias_f, dt_bias_b, d_skip, ssm_norm_w, w_out, norm2_w, w_up, ffn_conv_w, ffn_conv_b, w_down, final_norm_w, loss_target, m_norm1_w, m_w_in, m_ssm_conv_w, m_ssm_conv_b, m_a_log_f, m_a_log_b, m_dt_bias_f, m_dt_bias_b, m_d_skip, m_ssm_norm_w, m_w_out, m_norm2_w, m_w_up, m_ffn_conv_w, m_ffn_conv_b, m_w_down, m_final_norm_w, v_norm1_w, v_w_in, v_ssm_conv_w, v_ssm_conv_b, v_a_log_f, v_a_log_b, v_dt_bias_f, v_dt_bias_b, v_d_skip, v_ssm_norm_w, v_w_out, v_norm2_w, v_w_up, v_ffn_conv_w, v_ffn_conv_b, v_w_down, v_final_norm_w):
    w = dict(norm1_w=norm1_w, w_in=w_in, ssm_conv_w=ssm_conv_w, ssm_conv_b=ssm_conv_b, a_log_f=a_log_f, a_log_b=a_log_b,
             dt_bias_f=dt_bias_f, dt_bias_b=dt_bias_b, d_skip=d_skip, ssm_norm_w=ssm_norm_w, w_out=w_out, norm2_w=norm2_w,
             w_up=w_up, ffn_conv_w=ffn_conv_w, ffn_conv_b=ffn_conv_b, w_down=w_down, final_norm_w=final_norm_w)
    mo = dict(norm1_w=m_norm1_w, w_in=m_w_in, ssm_conv_w=m_ssm_conv_w, ssm_conv_b=m_ssm_conv_b, a_log_f=m_a_log_f, a_log_b=m_a_log_b,
              dt_bias_f=m_dt_bias_f, dt_bias_b=m_dt_bias_b, d_skip=m_d_skip, ssm_norm_w=m_ssm_norm_w, w_out=m_w_out, norm2_w=m_norm2_w,
              w_up=m_w_up, ffn_conv_w=m_ffn_conv_w, ffn_conv_b=m_ffn_conv_b, w_down=m_w_down, final_norm_w=m_final_norm_w)
    vo = dict(norm1_w=v_norm1_w, w_in=v_w_in, ssm_conv_w=v_ssm_conv_w, ssm_conv_b=v_ssm_conv_b, a_log_f=v_a_log_f, a_log_b=v_a_log_b,
              dt_bias_f=v_dt_bias_f, dt_bias_b=v_dt_bias_b, d_skip=v_d_skip, ssm_norm_w=v_ssm_norm_w, w_out=v_w_out, norm2_w=v_norm2_w,
              w_up=v_w_up, ffn_conv_w=v_ffn_conv_w, ffn_conv_b=v_ffn_conv_b, w_down=v_w_down, final_norm_w=v_final_norm_w)
    me = _my_index()

    g_in, g_conv = _all_gather([w["w_in"][0].astype(BF16), _pack([w["ssm_conv_w"][0], w["ffn_conv_w"][0]])], "w_in_all_gather")
    conv_rows = [_unpack(g_conv[i], [ssm_conv_w.shape[1:], ffn_conv_w.shape[1:]]) for i in range(N_DEV)]
    full = {
        "w_in": g_in.transpose(1, 0, 2).reshape(D_MODEL, N_DEV * w_in.shape[2]),
        "ssm_conv_w": jnp.concatenate([c[0] for c in conv_rows], axis=0),
        "ffn_conv_w": jnp.concatenate([c[1] for c in conv_rows], axis=0),
    }
    for k in ("norm1_w", "ssm_conv_b", "a_log_f", "a_log_b", "dt_bias_f", "dt_bias_b", "d_skip", "ssm_norm_w", "norm2_w", "ffn_conv_b",
              "final_norm_w"):
        full[k] = w[k]
    late = [w["w_out"][0].astype(BF16), w["w_up"][0].astype(BF16), w["w_down"][0].astype(BF16)]

    loss_part, grad_x, big, small = _local_step(x[0], loss_target[0], full, late)

    small_shapes = [(1,)] + [small[k].shape for k in SMALL_ORDER]
    packed = _pack([loss_part] + [small[k] for k in SMALL_ORDER])
    out_small = jnp.broadcast_to(packed[None], (N_DEV,) + packed.shape)
    (r_small,) = _exchange([out_small], "small_grads_exchange")
    r_in, r_out, r_up, r_down = big["w_in"], big["w_out"], big["w_up"], big["w_down"]

    outs_g, outs_d, outs_m, outs_v = {}, {}, {}, {}
    for k, parts in zip(BIG_ORDER, (r_in, r_out, r_up, r_down)):
        g, dlt, nm, nv = _adamw(parts, w[k][0], mo[k][0], vo[k][0], f"adamw_{k}")
        outs_g[k], outs_d[k], outs_m[k], outs_v[k] = g[None], dlt[None], nm[None], nv[None]
    tot = _unpack(_sum_parts(r_small, "small_grads_sum"), small_shapes)
    loss = tot[0][0]
    gs = dict(zip(SMALL_ORDER, tot[1:]))
    g_own = {}
    for k in SMALL_ORDER:
        if k in SHARDED_SMALL:
            rows = w[k].shape[1]
            g_own[k] = lax.dynamic_slice_in_dim(gs[k], me * rows, rows, axis=0)[None]
        else:
            g_own[k] = gs[k].reshape(w[k].shape)
    shapes = [w[k].shape for k in SMALL_ORDER]
    d_s, m_s, v_s = _adamw_small(_pack([g_own[k] for k in SMALL_ORDER]), _pack([w[k] for k in SMALL_ORDER]),
                                 _pack([mo[k] for k in SMALL_ORDER]), _pack([vo[k] for k in SMALL_ORDER]), "adamw_small")
    for k, a, b, c in zip(SMALL_ORDER, _unpack(d_s, shapes), _unpack(m_s, shapes), _unpack(v_s, shapes)):
        outs_g[k], outs_d[k], outs_m[k], outs_v[k] = g_own[k], a, b, c

    order = ("norm1_w", "w_in", "ssm_conv_w", "ssm_conv_b", "a_log_f", "a_log_b", "dt_bias_f", "dt_bias_b", "d_skip", "ssm_norm_w",
             "w_out", "norm2_w", "w_up", "ffn_conv_w", "ffn_conv_b", "w_down", "final_norm_w")
    return (loss, grad_x[None], *[outs_g[k] for k in order], *[outs_d[k] for k in order],
            *[outs_m[k] for k in order], *[outs_v[k] for k in order])
```

```python
import math

import numpy as np
import jax
import jax.numpy as jnp
from jax import lax
from jax.experimental import pallas as pl
from jax.experimental.pallas import tpu as pltpu

F32 = jnp.float32
BF16 = jnp.bfloat16
SDS = jax.ShapeDtypeStruct

N_DEV = 8
D_MODEL = 1024
N_HEADS = 16
HEAD_DIM = 64
ROPE_DIM = 16
ROPE_THETA = 500000.0
DILATIONS = (1, 4, 16)
BAND_HALF = 64
D_INNER = 1024
N_GROUPS = 4
D_STATE = 128
CHUNK = 128
XBC = D_INNER + 2 * N_GROUPS * D_STATE
D_FF = 2816
MAIN_W = 3 * D_MODEL + D_INNER + XBC
EPS = 1e-6
LR, B1, B2, AEPS, WD, STEP = 0.001, 0.9, 0.999, 1e-08, 0.01, 10
NEG = -1e30
VMEM_LIMIT = 56 * 1024 * 1024
MESH = pl.DeviceIdType.MESH
HIGH = lax.Precision.HIGHEST
NT = (((1,), (1,)), ((), ()))
TN = (((0,), (0,)), ((), ()))


def _cp(*sem):
    return pltpu.CompilerParams(dimension_semantics=sem, vmem_limit_bytes=VMEM_LIMIT)


def _pick(n, cands):
    for c in cands:
        if n % c == 0:
            return c
    raise ValueError(f"no tile for {n}")


def _sigmoid(x):
    return 1.0 / (1.0 + jnp.exp(-x))


def _softplus(x):
    return jnp.maximum(x, 0.0) + jnp.log1p(jnp.exp(-jnp.abs(x)))


def _slab(s, width, dtype, slab, n_in, out_idx=0):
    if slab is None:
        return SDS((s, width), dtype), 0, [], [], {}
    total, col0, into = slab
    if into is None:
        return SDS((s, total), dtype), col0, [], [], {}
    return SDS((s, total), dtype), col0, [into], [pl.BlockSpec(memory_space=pl.ANY)], {n_in: out_idx}


def _matmul(a, b, *, name, trans_b=False, out_dtype=BF16, residual=None, exchange=()):
    m, k = a.shape
    n = b.shape[0] if trans_b else b.shape[1]
    tk = k if k <= 2048 else _pick(k, (2048, 1408, 1024, 512))
    nk = k // tk
    if nk == 1:
        tm = _pick(m, (2048, 1408, 1024, 512, 256, 128))
        tn = _pick(n, (512, 256, 128))
    else:
        tm = _pick(m, (1024, 1408, 512, 256, 128))
        tn = _pick(n, (1024, 1408, 512, 256, 128))
    dn = NT if trans_b else (((1,), (0,)), ((), ()))
    nx = len(exchange)
    n_in = 2 + (residual is not None)
    grid = (m // tm, n // tn, nk)

    def body(*refs):
        a_ref, b_ref = refs[0], refs[1]
        xch_src = refs[n_in:n_in + nx]
        o_ref = refs[n_in + nx]
        xch_dst = refs[n_in + nx + 1:n_in + 2 * nx + 1]
        acc = refs[n_in + 2 * nx + 1]
        xch_sems = refs[n_in + 2 * nx + 2:]
        i, j, kk = pl.program_id(0), pl.program_id(1), pl.program_id(2)

        if nx:
            @pl.when((i == 0) & (j == 0) & (kk == 0))
            def _():
                _exchange_start(xch_src, xch_dst, *xch_sems)

        @pl.when(kk == 0)
        def _():
            acc[...] = jnp.zeros_like(acc)

        acc[...] += lax.dot_general(a_ref[...], b_ref[...], dn, preferred_element_type=F32)

        @pl.when(kk == nk - 1)
        def _():
            r = acc[...]
            if residual is not None:
                r = r + refs[2][...].astype(F32)
            o_ref[...] = r.astype(o_ref.dtype)

        if nx:
            @pl.when((i == grid[0] - 1) & (j == grid[1] - 1) & (kk == nk - 1))
            def _():
                _exchange_wait(xch_src, xch_dst, *xch_sems)

    in_specs = [pl.BlockSpec((tm, tk), lambda i, j, kk: (i, kk)),
                pl.BlockSpec((tn, tk), lambda i, j, kk: (j, kk)) if trans_b else pl.BlockSpec((tk, tn), lambda i, j, kk: (kk, j))]
    args = [a, b]
    if residual is not None:
        in_specs.append(pl.BlockSpec((tm, tn), lambda i, j, kk: (i, j)))
        args.append(residual)
    any_spec = pl.BlockSpec(memory_space=pl.ANY)
    out = pl.pallas_call(
        body, grid=grid, in_specs=in_specs + [any_spec] * nx,
        out_specs=[pl.BlockSpec((tm, tn), lambda i, j, kk: (i, j))] + [any_spec] * nx,
        out_shape=[SDS((m, n), out_dtype)] + [SDS(e.shape, e.dtype) for e in exchange],
        scratch_shapes=[pltpu.VMEM((tm, tn), F32)] + (_exchange_sems(nx) if nx else []),
        name=name, compiler_params=_cp(*(("arbitrary",) * 3 if nx else ("parallel", "parallel", "arbitrary"))))(*args, *exchange)
    return out if nx else out[0]


def _rmsnorm_fwd(x, w, name):
    s, d = x.shape
    tm = _pick(s, (512, 128))

    def body(x_ref, w_ref, o_ref, ot_ref):
        xf = x_ref[...]
        r = lax.rsqrt(jnp.mean(xf * xf, axis=-1, keepdims=True) + EPS)
        out = (xf * r * w_ref[...]).astype(o_ref.dtype)
        o_ref[...] = out
        ot_ref[...] = out.T

    return pl.pallas_call(
        body, grid=(s // tm,), in_specs=[pl.BlockSpec((tm, d), lambda i: (i, 0)), pl.BlockSpec((1, d), lambda i: (0, 0))],
        out_specs=[pl.BlockSpec((tm, d), lambda i: (i, 0)), pl.BlockSpec((d, tm), lambda i: (0, i))],
        out_shape=[SDS((s, d), BF16), SDS((d, s), BF16)], name=name, compiler_params=_cp("parallel"))(x, w)


def _rmsnorm_bwd(x, w, dh, dres, name):
    s, d = x.shape
    tm = _pick(s, (512, 128))

    def body(x_ref, w_ref, dh_ref, dres_ref, dx_ref, dxb_ref, dw_ref):
        xf = x_ref[...]
        r = lax.rsqrt(jnp.mean(xf * xf, axis=-1, keepdims=True) + EPS)
        xhat = xf * r
        dhf = dh_ref[...].astype(F32)
        g = dhf * w_ref[...]
        dx = dres_ref[...] + r * (g - xhat * jnp.mean(g * xhat, axis=-1, keepdims=True))
        dx_ref[...] = dx
        dxb_ref[...] = dx.astype(dxb_ref.dtype)

        @pl.when(pl.program_id(0) == 0)
        def _():
            dw_ref[...] = jnp.zeros_like(dw_ref)

        dw_ref[...] += jnp.sum(dhf * xhat, axis=0, keepdims=True)

    row = pl.BlockSpec((tm, d), lambda i: (i, 0))
    vec = pl.BlockSpec((1, d), lambda i: (0, 0))
    return pl.pallas_call(
        body, grid=(s // tm,), in_specs=[row, vec, row, row], out_specs=[row, row, vec],
        out_shape=[SDS((s, d), F32), SDS((s, d), BF16), SDS((1, d), F32)], name=name, compiler_params=_cp("arbitrary"))(x, w, dh, dres)


def _final_norm_loss(x, w, target, name):
    s, d = x.shape
    tm = _pick(s, (512, 128))

    def body(x_ref, w_ref, t_ref, dx_ref, dxb_ref, dw_ref, loss_ref):
        xf = x_ref[...]
        r = lax.rsqrt(jnp.mean(xf * xf, axis=-1, keepdims=True) + EPS)
        xhat = xf * r
        wv = w_ref[...]
        e = xhat * wv - t_ref[...]
        dy = e * (1.0 / d)
        g = dy * wv
        dx = r * (g - xhat * jnp.mean(g * xhat, axis=-1, keepdims=True))
        dx_ref[...] = dx
        dxb_ref[...] = dx.astype(dxb_ref.dtype)

        @pl.when(pl.program_id(0) == 0)
        def _():
            dw_ref[...] = jnp.zeros_like(dw_ref)
            loss_ref[...] = jnp.zeros_like(loss_ref)

        dw_ref[...] += jnp.sum(dy * xhat, axis=0, keepdims=True)
        loss_ref[...] += jnp.sum(jnp.sum(e * e, axis=1, keepdims=True), axis=0, keepdims=True) * (0.5 / d)

    row = pl.BlockSpec((tm, d), lambda i: (i, 0))
    vec = pl.BlockSpec((1, d), lambda i: (0, 0))
    return pl.pallas_call(
        body, grid=(s // tm,), in_specs=[row, vec, row], out_specs=[row, row, vec, pl.BlockSpec((1, 128), lambda i: (0, 0))],
        out_shape=[SDS((s, d), F32), SDS((s, d), BF16), SDS((1, d), F32), SDS((1, 128), F32)],
        name=name, compiler_params=_cp("arbitrary"))(x, w, target)


def _rope_tables(s):
    half = ROPE_DIM // 2
    f32 = np.float32
    inv_freq = np.power(f32(ROPE_THETA), -np.arange(half, dtype=f32) * f32(2.0) / f32(ROPE_DIM)).astype(f32)
    ang = (np.arange(s, dtype=f32)[:, None] * inv_freq[None, :]).astype(f32)
    cos, sin = np.cos(ang).astype(f32), np.sin(ang).astype(f32)
    z = np.zeros((s, HEAD_DIM - ROPE_DIM), f32)
    zh = np.zeros((s, half), f32)
    c = np.concatenate([cos, cos, z + 1.0], axis=1)
    sa = np.concatenate([zh, sin, z], axis=1)
    sb = np.concatenate([-sin, zh, z], axis=1)
    two = lambda t: np.concatenate([t, t], axis=1)
    c, sa, sb = two(c), two(sa), two(sb)
    fwd, bwd = (c, sa, sb), (c, np.roll(sb, half, axis=1), np.roll(sa, -half, axis=1))
    return tuple(jnp.asarray(t) for t in fwd), tuple(jnp.asarray(t) for t in bwd)


PERM_TILE = 256


def _perm_matrices(d):
    n = PERM_TILE // d
    o = np.arange(PERM_TILE)
    p = np.zeros((PERM_TILE, PERM_TILE), np.float32)
    p[o, (o % n) * d + o // n] = 1.0
    return jnp.asarray(p, dtype=BF16), jnp.asarray(p.T.copy(), dtype=BF16)


def _store_pattern(o_ref, tile, perm_ref, d, cols=slice(None)):
    n = PERM_TILE // d
    z = jnp.dot(perm_ref[...], tile, preferred_element_type=F32).astype(o_ref.dtype)
    for r in range(d):
        o_ref[r, :, cols] = z[r * n:(r + 1) * n]


def _load_pattern(x_ref, perm_ref, d):
    tile = jnp.concatenate([x_ref[r] for r in range(d)], axis=0)
    return jnp.dot(perm_ref[...], tile, preferred_element_type=F32)


def _pattern_spec(d, w, col=0):
    return pl.BlockSpec((d, PERM_TILE // d, w), lambda i, *_: (0, i, col))


def _rope_fwd(proj, tabs, perms, name):
    s = proj.shape[0]
    tm = PERM_TILE
    half = ROPE_DIM // 2
    wb = D_MODEL
    nd = len(DILATIONS) - 1

    def body(x_ref, c_ref, sa_ref, sb_ref, *rest):
        perm_refs, o_ref, op_refs = rest[:nd], rest[nd], rest[nd + 1:]
        is_v = pl.program_id(1) == 2
        c = jnp.where(is_v, 1.0, c_ref[...])
        sa = jnp.where(is_v, 0.0, sa_ref[...])
        sb = jnp.where(is_v, 0.0, sb_ref[...])
        for j in range(wb // 128):
            x = x_ref[:, j * 128:(j + 1) * 128].astype(F32)
            o_ref[:, j * 128:(j + 1) * 128] = (x * c + pltpu.roll(x, half, 1) * sa + pltpu.roll(x, 128 - half, 1) * sb).astype(o_ref.dtype)
        y = o_ref[...]
        for d, perm_ref, op_ref in zip(DILATIONS[1:], perm_refs, op_refs):
            _store_pattern(op_ref, y, perm_ref, d)

    blk = pl.BlockSpec((tm, wb), lambda i, j: (i, j))
    tab = pl.BlockSpec((tm, 128), lambda i, j: (i, 0))
    pm = pl.BlockSpec((tm, tm), lambda i, j: (0, 0))
    outs = pl.pallas_call(
        body, grid=(s // tm, 3), in_specs=[blk, tab, tab, tab] + [pm] * nd,
        out_specs=[blk] + [pl.BlockSpec((d, tm // d, wb), lambda i, j: (0, i, j)) for d in DILATIONS[1:]],
        out_shape=[SDS((s, 3 * wb), BF16)] + [SDS((d, s // d, 3 * wb), BF16) for d in DILATIONS[1:]],
        name=name, compiler_params=_cp("parallel", "parallel"))(proj, *tabs, *[perms[d][0] for d in DILATIONS[1:]])
    return [o.reshape(s, 3 * wb) for o in outs]


def _sum3_rope(ds_, perms, tabs, name, slab=None):
    s, w = ds_[0].shape
    tm = PERM_TILE
    half = ROPE_DIM // 2
    nd = len(DILATIONS) - 1

    def body(*refs):
        x_refs, perm_refs = refs[:nd + 1], refs[nd + 1:2 * nd + 1]
        tab_refs = refs[2 * nd + 1:2 * nd + 4]
        tot = x_refs[0][...].astype(F32)
        for d, x_ref, perm_ref in zip(DILATIONS[1:], x_refs[1:], perm_refs):
            tot = tot + _load_pattern(x_ref, perm_ref, d)
        for j in range(w // 128):
            x = tot[:, j * 128:(j + 1) * 128]
            if tabs is not None:
                x = x * tab_refs[0][...] + pltpu.roll(x, half, 1) * tab_refs[1][...] + pltpu.roll(x, 128 - half, 1) * tab_refs[2][...]
            refs[-1][:, j * 128:(j + 1) * 128] = x.astype(refs[-1].dtype)

    blk = pl.BlockSpec((tm, w), lambda i: (i, 0))
    tab = pl.BlockSpec((tm, 128), lambda i: (i, 0))
    pm = pl.BlockSpec((tm, tm), lambda i: (0, 0))
    extra = [] if tabs is None else list(tabs)
    out_shape, col0, more, more_specs, alias = _slab(s, w, BF16, slab, 2 * nd + 1 + len(extra))
    cb = col0 // w
    args = [ds_[0]] + [x.reshape(d, s // d, w) for d, x in zip(DILATIONS[1:], ds_[1:])] + [perms[d][1] for d in DILATIONS[1:]]
    return pl.pallas_call(
        body, grid=(s // tm,),
        in_specs=[blk] + [_pattern_spec(d, w) for d in DILATIONS[1:]] + [pm] * nd + [tab] * len(extra) + more_specs,
        out_specs=pl.BlockSpec((tm, w), lambda i: (i, cb)), out_shape=out_shape, input_output_aliases=alias,
        name=name, compiler_params=_cp("parallel"))(*args, *extra, *more)


def _band_valid(t, nq, nk, qofs, kofs, seq_len):
    qpos = t * 128 + qofs + lax.broadcasted_iota(jnp.int32, (nq, nk), 0)
    kpos = t * 128 + kofs + lax.broadcasted_iota(jnp.int32, (nq, nk), 1)
    sh = int(math.log2(seq_len))
    same = lax.shift_right_arithmetic(qpos, sh) == lax.shift_right_arithmetic(kpos, sh)
    return same & (jnp.abs(kpos - qpos) <= BAND_HALF)


def _window(r0, r1, r2):
    return jnp.concatenate([r0[128 - BAND_HALF:128], r1[...], r2[0:BAND_HALF]], axis=0)


WIN = 128 + 2 * BAND_HALF


def _win_specs(width, col, nt):
    return [pl.BlockSpec((128, width), lambda t: (jnp.maximum(t - 1, 0), col)),
            pl.BlockSpec((128, width), lambda t: (t, col)),
            pl.BlockSpec((128, width), lambda t: (jnp.minimum(t + 1, nt - 1), col))]


def _attn_fwd(qk, v_src, v_col, seq_len, name):
    s = qk.shape[0]
    nt = s // 128
    dm = D_MODEL

    def body(q_ref, k0, k1, k2, v0, v1, v2, o_ref, lse_ref):
        t = pl.program_id(0)
        valid = _band_valid(t, 128, WIN, 0, -BAND_HALF, seq_len)
        q = q_ref[...]
        kc = _window(k0, k1, k2)
        vc = _window(v0, v1, v2)
        outs, lses = [], []
        for h in range(N_HEADS):
            sl = slice(h * HEAD_DIM, (h + 1) * HEAD_DIM)
            sc = lax.dot_general(q[:, sl], kc[:, sl], NT, preferred_element_type=F32) * (HEAD_DIM ** -0.5)
            sc = jnp.where(valid, sc, NEG)
            m = jnp.max(sc, axis=1, keepdims=True)
            e = jnp.exp(sc - m)
            den = jnp.sum(e, axis=1, keepdims=True)
            o = jnp.dot(e.astype(BF16), vc[:, sl], preferred_element_type=F32) / den
            outs.append(o)
            lses.append(m + jnp.log(den))
        o_ref[...] = jnp.concatenate(outs, axis=1).astype(o_ref.dtype)
        lse_ref[...] = jnp.concatenate(lses, axis=1)

    in_specs = [pl.BlockSpec((128, dm), lambda t: (t, 0))] + _win_specs(dm, 1, nt) + _win_specs(dm, v_col, nt)
    return pl.pallas_call(
        body, grid=(nt,), in_specs=in_specs,
        out_specs=[pl.BlockSpec((128, dm), lambda t: (t, 0)), pl.BlockSpec((128, N_HEADS), lambda t: (t, 0))],
        out_shape=[SDS((s, dm), BF16), SDS((s, N_HEADS), F32)], name=name, compiler_params=_cp("parallel"))(
            qk, qk, qk, qk, v_src, v_src, v_src)


def _attn_combine(os_, lses, perms, expand, name):
    s, dm = os_[0].shape
    tm = PERM_TILE
    nd = len(DILATIONS) - 1

    def body(o1, o2, o3, l1, l2, l3, p2, p3, e_ref, out_ref, out_t_ref, lt_ref):
        ls = [l1[...], l2[...], l3[...]]
        m = jnp.maximum(jnp.maximum(ls[0], ls[1]), ls[2])
        es = [jnp.exp(l - m) for l in ls]
        tot = es[0] + es[1] + es[2]
        lt_ref[...] = m + jnp.log(tot)
        ovs = [o1[...].astype(F32), _load_pattern(o2, p2, DILATIONS[1]), _load_pattern(o3, p3, DILATIONS[2])]
        acc = jnp.zeros((tm, dm), F32)
        for e, o in zip(es, ovs):
            acc = acc + jnp.dot(e / tot, e_ref[...], precision=HIGH, preferred_element_type=F32) * o
        out = acc.astype(out_ref.dtype)
        out_ref[...] = out
        out_t_ref[...] = out.T

    row = pl.BlockSpec((tm, dm), lambda i: (i, 0))
    st = pl.BlockSpec((tm, N_HEADS), lambda i: (i, 0))
    pm = pl.BlockSpec((tm, tm), lambda i: (0, 0))
    args = [os_[0]] + [o.reshape(d, s // d, dm) for d, o in zip(DILATIONS[1:], os_[1:])]
    return pl.pallas_call(
        body, grid=(s // tm,),
        in_specs=[row] + [_pattern_spec(d, dm) for d in DILATIONS[1:]] + [st, st, st, pm, pm, pl.BlockSpec((N_HEADS, dm), lambda i: (0, 0))],
        out_specs=[row, pl.BlockSpec((dm, tm), lambda i: (0, i)), st],
        out_shape=[SDS((s, dm + D_INNER), BF16), SDS((dm + D_INNER, s), BF16), SDS((s, N_HEADS), F32)],
        name=name, compiler_params=_cp("parallel"))(*args, *lses, *[perms[d][1] for d in DILATIONS[1:]], expand)


def _attn_delta(dmix, attn, expand_t, perms, name):
    s, dm = attn.shape[0], D_MODEL
    tm = PERM_TILE
    nd = len(DILATIONS) - 1

    def body(d_ref, a_ref, e_ref, *rest):
        perm_refs, o_ref, op_refs = rest[:nd], rest[nd], rest[nd + 1:]
        dv = d_ref[...]
        prod = dv.astype(F32) * a_ref[...].astype(F32)
        o_ref[...] = jnp.dot(prod, e_ref[...], precision=HIGH, preferred_element_type=F32)
        for d, perm_ref, op_ref in zip(DILATIONS[1:], perm_refs, op_refs):
            _store_pattern(op_ref, dv, perm_ref, d)

    row = pl.BlockSpec((tm, dm), lambda i: (i, 0))
    pm = pl.BlockSpec((tm, tm), lambda i: (0, 0))
    outs = pl.pallas_call(
        body, grid=(s // tm,), in_specs=[row, row, pl.BlockSpec((dm, N_HEADS), lambda i: (0, 0))] + [pm] * nd,
        out_specs=[pl.BlockSpec((tm, N_HEADS), lambda i: (i, 0))] + [_pattern_spec(d, dm) for d in DILATIONS[1:]],
        out_shape=[SDS((s, N_HEADS), F32)] + [SDS((d, s // d, dm), BF16) for d in DILATIONS[1:]],
        name=name, compiler_params=_cp("parallel"))(dmix, attn, expand_t, *[perms[d][0] for d in DILATIONS[1:]])
    return outs[0], [o.reshape(s, dm) for o in outs[1:]]


def _attn_bwd_dq(qk, v_src, v_col, do_src, lse, delta, seq_len, name):
    s = qk.shape[0]
    nt = s // 128
    dm = D_MODEL

    def body(q_ref, k0, k1, k2, v0, v1, v2, do_ref, lse_ref, dl_ref, dq_ref):
        t = pl.program_id(0)
        valid = _band_valid(t, 128, WIN, 0, -BAND_HALF, seq_len)
        q = q_ref[...]
        do = do_ref[...]
        kc = _window(k0, k1, k2)
        vc = _window(v0, v1, v2)
        lse_v, dl_v = lse_ref[...], dl_ref[...]
        outs = []
        for h in range(N_HEADS):
            sl = slice(h * HEAD_DIM, (h + 1) * HEAD_DIM)
            sc = lax.dot_general(q[:, sl], kc[:, sl], NT, preferred_element_type=F32) * (HEAD_DIM ** -0.5)
            p = jnp.exp(jnp.where(valid, sc - lse_v[:, h:h + 1], NEG))
            dp = lax.dot_general(do[:, sl], vc[:, sl], NT, preferred_element_type=F32)
            ds = p * (dp - dl_v[:, h:h + 1])
            outs.append(jnp.dot(ds.astype(BF16), kc[:, sl], preferred_element_type=F32) * (HEAD_DIM ** -0.5))
        dq_ref[...] = jnp.concatenate(outs, axis=1).astype(dq_ref.dtype)

    row = pl.BlockSpec((128, dm), lambda t: (t, 0))
    st = pl.BlockSpec((128, N_HEADS), lambda t: (t, 0))
    in_specs = [row] + _win_specs(dm, 1, nt) + _win_specs(dm, v_col, nt) + [row, st, st]
    return pl.pallas_call(
        body, grid=(nt,), in_specs=in_specs, out_specs=row, out_shape=SDS((s, dm), BF16),
        name=name, compiler_params=_cp("parallel"))(qk, qk, qk, qk, v_src, v_src, v_src, do_src, lse, delta)


def _attn_bwd_dkv(qk, v_src, v_col, do_src, lse_t, delta_t, seq_len, name):
    s = qk.shape[0]
    nt = s // 128
    dm = D_MODEL

    def lane_window(r0, r1, r2):
        return jnp.concatenate([r0[:, 128 - BAND_HALF:128], r1[...], r2[:, 0:BAND_HALF]], axis=1)

    def body(k_ref, v_ref, q0, q1, q2, d0, d1, d2, l0, l1, l2, e0, e1, e2, dk_ref, dv_ref):
        t = pl.program_id(0)
        valid = _band_valid(t, 128, WIN, 0, -BAND_HALF, seq_len)
        k = k_ref[...]
        v = v_ref[...]
        qc = _window(q0, q1, q2)
        dc = _window(d0, d1, d2)
        lse_v = lane_window(l0, l1, l2)
        dl_v = lane_window(e0, e1, e2)
        dks, dvs = [], []
        for h in range(N_HEADS):
            sl = slice(h * HEAD_DIM, (h + 1) * HEAD_DIM)
            sc = lax.dot_general(k[:, sl], qc[:, sl], NT, preferred_element_type=F32) * (HEAD_DIM ** -0.5)
            p = jnp.exp(jnp.where(valid, sc - lse_v[h:h + 1, :], NEG))
            dvs.append(jnp.dot(p.astype(BF16), dc[:, sl], preferred_element_type=F32))
            dp = lax.dot_general(v[:, sl], dc[:, sl], NT, preferred_element_type=F32)
            ds = p * (dp - dl_v[h:h + 1, :])
            dks.append(jnp.dot(ds.astype(BF16), qc[:, sl], preferred_element_type=F32) * (HEAD_DIM ** -0.5))
        dk_ref[...] = jnp.concatenate(dks, axis=1).astype(dk_ref.dtype)
        dv_ref[...] = jnp.concatenate(dvs, axis=1).astype(dv_ref.dtype)

    row = pl.BlockSpec((128, dm), lambda t: (t, 0))
    stat = [pl.BlockSpec((N_HEADS, 128), lambda t: (0, jnp.maximum(t - 1, 0))), pl.BlockSpec((N_HEADS, 128), lambda t: (0, t)),
            pl.BlockSpec((N_HEADS, 128), lambda t: (0, jnp.minimum(t + 1, nt - 1)))]
    in_specs = ([pl.BlockSpec((128, dm), lambda t: (t, 1)), pl.BlockSpec((128, dm), lambda t: (t, v_col))]
                + _win_specs(dm, 0, nt) + _win_specs(dm, 0, nt) + stat + stat)
    return pl.pallas_call(
        body, grid=(nt,), in_specs=in_specs, out_specs=[row, row], out_shape=[SDS((s, dm), BF16), SDS((s, dm), BF16)],
        name=name, compiler_params=_cp("parallel"))(qk, v_src, qk, qk, qk, do_src, do_src, do_src, lse_t, lse_t, lse_t, delta_t, delta_t, delta_t)


CONV_COLS = (1024, 1408, 512, 256)


def _halo_specs(tm, tc, col0, nrow_blocks):
    r = tm // 16
    return [pl.BlockSpec((16, tc), lambda i, j: (jnp.maximum(i * r - 1, 0), col0 + j)),
            pl.BlockSpec((16, tc), lambda i, j: (jnp.minimum((i + 1) * r, nrow_blocks * r - 1), col0 + j))]


def _shifted(x_ref, hp_ref, hn_ref, i, last):
    x = x_ref[...].astype(F32)
    tm = x.shape[0]
    rows = lax.broadcasted_iota(jnp.int32, x.shape, 0)
    prev_row = jnp.where(i > 0, hp_ref[15:16, :].astype(F32), 0.0)
    next_row = jnp.where(i < last, hn_ref[0:1, :].astype(F32), 0.0)
    xp = jnp.where(rows == 0, prev_row, pltpu.roll(x, 1, 0))
    xn = jnp.where(rows == tm - 1, next_row, pltpu.roll(x, tm - 1, 0))
    return xp, x, xn


def _conv(x_src, col0, width, w3, bias, act, name, out_dtype=BF16, slab=None, transpose=False, wcol0=0):
    s = x_src.shape[0]
    tm = _pick(s, (256, 128))
    tc = _pick(width, CONV_COLS)
    nb = s // tm
    c0 = col0 // tc
    wc0 = wcol0 // tc

    def body(*refs):
        x_ref, hp_ref, hn_ref, w_ref, b_ref = refs[:5]
        o_ref = refs[-1]
        i = pl.program_id(0)
        xp, x, xn = _shifted(x_ref, hp_ref, hn_ref, i, nb - 1)
        w = w_ref[...]
        if transpose:
            y = w[2:3, :] * xp + w[1:2, :] * x + w[0:1, :] * xn
        else:
            y = w[0:1, :] * xp + w[1:2, :] * x + w[2:3, :] * xn + b_ref[...]
        if act:
            y = y * _sigmoid(y)
        o_ref[...] = y.astype(o_ref.dtype)

    in_specs = ([pl.BlockSpec((tm, tc), lambda i, j: (i, c0 + j))] + _halo_specs(tm, tc, c0, nb)
                + [pl.BlockSpec((3, tc), lambda i, j: (0, wc0 + j)), pl.BlockSpec((1, tc), lambda i, j: (0, wc0 + j))])
    out_shape, ocol, more, more_specs, alias = _slab(s, width, out_dtype, slab, 5)
    ob = ocol // tc
    return pl.pallas_call(
        body, grid=(nb, width // tc), in_specs=in_specs + more_specs, out_specs=pl.BlockSpec((tm, tc), lambda i, j: (i, ob + j)),
        out_shape=out_shape, input_output_aliases=alias, name=name, compiler_params=_cp("parallel", "parallel"))(
            x_src, x_src, x_src, w3, bias, *more)


def _conv_silu_bwd(x_src, col0, width, w3, bias, addends, add_widths, name):
    s = x_src.shape[0]
    tm = _pick(s, (256, 128))
    tc = _pick(width, CONV_COLS)
    nb = s // tm
    c0 = col0 // tc
    na = len(addends)

    def body(*refs):
        x_ref, hp_ref, hn_ref, w_ref, b_ref = refs[:5]
        a_refs = refs[5:5 + na]
        dp_ref, dw_ref, db_ref = refs[5 + na:]
        i, j = pl.program_id(1), pl.program_id(0)
        xp, x, xn = _shifted(x_ref, hp_ref, hn_ref, i, nb - 1)
        w = w_ref[...]
        pre = w[0:1, :] * xp + w[1:2, :] * x + w[2:3, :] * xn + b_ref[...]
        g = jnp.zeros_like(pre)
        for a_ref, aw in zip(a_refs, add_widths):
            av = a_ref[...].astype(F32)
            g = g + (av if aw == width else jnp.where(j < aw // tc, av, 0.0))
        sg = _sigmoid(pre)
        dpre = g * (sg * (1.0 + pre * (1.0 - sg)))
        dp_ref[...] = dpre.astype(dp_ref.dtype)

        @pl.when(i == 0)
        def _():
            dw_ref[...] = jnp.zeros_like(dw_ref)
            db_ref[...] = jnp.zeros_like(db_ref)

        dw_ref[...] += jnp.concatenate([jnp.sum(dpre * xp, axis=0, keepdims=True), jnp.sum(dpre * x, axis=0, keepdims=True),
                                        jnp.sum(dpre * xn, axis=0, keepdims=True)], axis=0)
        db_ref[...] += jnp.sum(dpre, axis=0, keepdims=True)

    r = tm // 16
    in_specs = [pl.BlockSpec((tm, tc), lambda j, i: (i, c0 + j)),
                pl.BlockSpec((16, tc), lambda j, i: (jnp.maximum(i * r - 1, 0), c0 + j)),
                pl.BlockSpec((16, tc), lambda j, i: (jnp.minimum((i + 1) * r, nb * r - 1), c0 + j)),
                pl.BlockSpec((3, tc), lambda j, i: (0, j)), pl.BlockSpec((1, tc), lambda j, i: (0, j))]
    for aw in add_widths:
        nblk = aw // tc
        in_specs.append(pl.BlockSpec((tm, tc), lambda j, i, nblk=nblk: (i, jnp.minimum(j, nblk - 1))))
    return pl.pallas_call(
        body, grid=(width // tc, nb), in_specs=in_specs,
        out_specs=[pl.BlockSpec((tm, tc), lambda j, i: (i, j)), pl.BlockSpec((3, tc), lambda j, i: (0, j)), pl.BlockSpec((1, tc), lambda j, i: (0, j))],
        out_shape=[SDS((s, width), BF16), SDS((3, width), F32), SDS((1, width), F32)],
        name=name, compiler_params=_cp("parallel", "arbitrary"))(x_src, x_src, x_src, w3, bias, *addends)


def _ffn_gate_fwd(u, w3, bias, name):
    s = u.shape[0]
    tm = _pick(s, (256, 128))
    tc = _pick(D_FF, CONV_COLS)
    nb = s // tm
    nj = D_FF // tc

    def body(g_ref, gp, gn, u_ref, up, un, wg_ref, wu_ref, bg_ref, bu_ref, o_ref, ot_ref):
        i = pl.program_id(0)
        outs = []
        for (x_ref, hp, hn, w_ref, b_ref) in ((g_ref, gp, gn, wg_ref, bg_ref), (u_ref, up, un, wu_ref, bu_ref)):
            xp, x, xn = _shifted(x_ref, hp, hn, i, nb - 1)
            w = w_ref[...]
            outs.append(w[0:1, :] * xp + w[1:2, :] * x + w[2:3, :] * xn + b_ref[...])
        gate, upv = outs
        out = (gate * _sigmoid(gate) * upv).astype(o_ref.dtype)
        o_ref[...] = out
        ot_ref[...] = out.T

    def xspecs(c0):
        return [pl.BlockSpec((tm, tc), lambda i, j: (i, c0 + j))] + _halo_specs(tm, tc, c0, nb)

    in_specs = (xspecs(0) + xspecs(nj)
                + [pl.BlockSpec((3, tc), lambda i, j: (0, j)), pl.BlockSpec((3, tc), lambda i, j: (0, nj + j)),
                   pl.BlockSpec((1, tc), lambda i, j: (0, j)), pl.BlockSpec((1, tc), lambda i, j: (0, nj + j))])
    return pl.pallas_call(
        body, grid=(nb, nj), in_specs=in_specs,
        out_specs=[pl.BlockSpec((tm, tc), lambda i, j: (i, j)), pl.BlockSpec((tc, tm), lambda i, j: (j, i))],
        out_shape=[SDS((s, D_FF), BF16), SDS((D_FF, s), BF16)], name=name, compiler_params=_cp("parallel", "parallel"))(
            u, u, u, u, u, u, w3, w3, bias, bias)


def _ffn_gate_bwd(u, w3, bias, dact, name):
    s = u.shape[0]
    tm = _pick(s, (256, 128))
    tc = _pick(D_FF, CONV_COLS)
    nb = s // tm
    nj = D_FF // tc

    def body(g_ref, gp, gn, u_ref, up, un, wg_ref, wu_ref, bg_ref, bu_ref, da_ref, dg_ref, du_ref, dwg_ref, dwu_ref, dbg_ref, dbu_ref):
        i = pl.program_id(1)
        sh, pre = [], []
        for (x_ref, hp, hn, w_ref, b_ref) in ((g_ref, gp, gn, wg_ref, bg_ref), (u_ref, up, un, wu_ref, bu_ref)):
            xs3 = _shifted(x_ref, hp, hn, i, nb - 1)
            w = w_ref[...]
            sh.append(xs3)
            pre.append(w[0:1, :] * xs3[0] + w[1:2, :] * xs3[1] + w[2:3, :] * xs3[2] + b_ref[...])
        gate, upv = pre
        da = da_ref[...].astype(F32)
        sg = _sigmoid(gate)
        dgate = da * upv * (sg * (1.0 + gate * (1.0 - sg)))
        dup = da * gate * sg
        dg_ref[...] = dgate.astype(dg_ref.dtype)
        du_ref[...] = dup.astype(du_ref.dtype)

        @pl.when(i == 0)
        def _():
            for r in (dwg_ref, dwu_ref, dbg_ref, dbu_ref):
                r[...] = jnp.zeros_like(r)

        for d, xs3, dw_ref, db_ref in ((dgate, sh[0], dwg_ref, dbg_ref), (dup, sh[1], dwu_ref, dbu_ref)):
            dw_ref[...] += jnp.concatenate([jnp.sum(d * xs3[0], axis=0, keepdims=True), jnp.sum(d * xs3[1], axis=0, keepdims=True),
                                            jnp.sum(d * xs3[2], axis=0, keepdims=True)], axis=0)
            db_ref[...] += jnp.sum(d, axis=0, keepdims=True)

    r = tm // 16

    def xspecs(c0):
        return [pl.BlockSpec((tm, tc), lambda j, i: (i, c0 + j)),
                pl.BlockSpec((16, tc), lambda j, i: (jnp.maximum(i * r - 1, 0), c0 + j)),
                pl.BlockSpec((16, tc), lambda j, i: (jnp.minimum((i + 1) * r, nb * r - 1), c0 + j))]

    in_specs = (xspecs(0) + xspecs(nj)
                + [pl.BlockSpec((3, tc), lambda j, i: (0, j)), pl.BlockSpec((3, tc), lambda j, i: (0, nj + j)),
                   pl.BlockSpec((1, tc), lambda j, i: (0, j)), pl.BlockSpec((1, tc), lambda j, i: (0, nj + j)),
                   pl.BlockSpec((tm, tc), lambda j, i: (i, j))])
    blk = pl.BlockSpec((tm, tc), lambda j, i: (i, j))
    w_o = pl.BlockSpec((3, tc), lambda j, i: (0, j))
    b_o = pl.BlockSpec((1, tc), lambda j, i: (0, j))
    return pl.pallas_call(
        body, grid=(nj, nb), in_specs=in_specs, out_specs=[blk, blk, w_o, w_o, b_o, b_o],
        out_shape=[SDS((s, D_FF), BF16), SDS((s, D_FF), BF16), SDS((3, D_FF), F32), SDS((3, D_FF), F32), SDS((1, D_FF), F32), SDS((1, D_FF), F32)],
        name=name, compiler_params=_cp("parallel", "arbitrary"))(u, u, u, u, u, u, w3, w3, bias, bias, dact)


def _exchange_sems(n):
    return [pltpu.SemaphoreType.DMA((7 * n,)), pltpu.SemaphoreType.DMA((7 * n,)), pltpu.SemaphoreType.DMA((n,))]


def _exchange_copies(srcs, outs, send_sems, recv_sems, local_sems):
    x, y, c = lax.axis_index("x"), lax.axis_index("y"), lax.axis_index("c")
    me = 4 * x + 2 * y + c
    locals_ = [pltpu.make_async_copy(srcs[a].at[me], outs[a].at[me], local_sems.at[a]) for a in range(len(srcs))]
    sends, recvs = [], []
    for k in range(1, N_DEV):
        px, py, pc = x ^ ((k >> 2) & 1), y ^ ((k >> 1) & 1), c ^ (k & 1)
        peer = 4 * px + 2 * py + pc
        for a in range(len(srcs)):
            sems = dict(send_sem=send_sems.at[a * 7 + k - 1], recv_sem=recv_sems.at[a * 7 + k - 1], device_id_type=MESH)
            sends.append(pltpu.make_async_remote_copy(src_ref=srcs[a].at[peer], dst_ref=outs[a].at[me], device_id=(px, py, pc), **sems))
            recvs.append(pltpu.make_async_remote_copy(src_ref=srcs[a].at[peer], dst_ref=outs[a].at[peer], device_id=(x, y, c), **sems))
    return locals_, sends, recvs


def _exchange_start(srcs, outs, send_sems, recv_sems, local_sems):
    locals_, sends, _ = _exchange_copies(srcs, outs, send_sems, recv_sems, local_sems)
    for cp in locals_ + sends:
        cp.start()


def _exchange_wait(srcs, outs, send_sems, recv_sems, local_sems):
    locals_, sends, recvs = _exchange_copies(srcs, outs, send_sems, recv_sems, local_sems)
    for cp in recvs:
        cp.wait_recv()
    for cp in sends:
        cp.wait_send()
    for cp in locals_:
        cp.wait()


def _gather_copies(srcs, outs, send_sems, recv_sems, local_sems):
    x, y, c = lax.axis_index("x"), lax.axis_index("y"), lax.axis_index("c")
    me, sibling = (x, y, c), (x, y, 1 - c)
    chips = [(1 - x, y), (x, 1 - y), (1 - x, 1 - y)]

    def copy(a, k, block, to, src=None):
        dst = outs[a].at[4 * block[0] + 2 * block[1] + block[2]]
        return pltpu.make_async_remote_copy(
            src_ref=dst if src is None else src, dst_ref=dst,
            send_sem=send_sems.at[a * 7 + k], recv_sem=recv_sems.at[a * 7 + k], device_id=to, device_id_type=MESH)

    n = len(srcs)
    locals_ = [pltpu.make_async_copy(srcs[a], outs[a].at[4 * x + 2 * y + c], local_sems.at[a]) for a in range(n)]
    own = [copy(a, 0, me, sibling, src=srcs[a]) for a in range(n)]
    own += [copy(a, 1 + j, me, (*chip, c), src=srcs[a]) for a in range(n) for j, chip in enumerate(chips)]
    landed_ici = [copy(a, 1 + j, (*chip, c), me) for j, chip in enumerate(chips) for a in range(n)]
    passed = [copy(a, 4 + j, (*chip, c), sibling) for j, chip in enumerate(chips) for a in range(n)]
    landed_d2d = [copy(a, 0, sibling, me) for a in range(n)]
    landed_d2d += [copy(a, 4 + j, (*chip, 1 - c), me) for a in range(n) for j, chip in enumerate(chips)]
    return locals_, own, landed_ici, passed, landed_d2d


def _gather_start(*refs):
    locals_, own, _, _, _ = _gather_copies(*refs)
    for cp in locals_ + own:
        cp.start()


def _gather_forward(*refs):
    _, _, landed_ici, passed, _ = _gather_copies(*refs)
    for arrived, onward in zip(landed_ici, passed):
        arrived.wait_recv()
        onward.start()


def _gather_finish(*refs):
    locals_, own, _, passed, landed_d2d = _gather_copies(*refs)
    for cp in landed_d2d:
        cp.wait_recv()
    for cp in own + passed:
        cp.wait_send()
    for cp in locals_:
        cp.wait()


def _ssd_common(dt_ref, dtt_ref, al_r, al_c, bi_r, bi_c, off, rev):
    li = lax.broadcasted_iota(jnp.int32, (CHUNK, CHUNK), 0)
    si = lax.broadcasted_iota(jnp.int32, (CHUNK, CHUNK), 1)
    mask = (li <= si) if rev else (li >= si)
    mask_t = (li >= si) if rev else (li <= si)
    a_r = -jnp.exp(al_r[...])
    a_c = -jnp.exp(al_c[...])
    pre = dt_ref[:, off:off + N_HEADS] + bi_r[...]
    dt = _softplus(pre)
    cs = jnp.dot(mask.astype(F32), dt * a_r, precision=HIGH, preferred_element_type=F32)
    dt_t = _softplus(dtt_ref[off:off + N_HEADS, :] + bi_c[...])
    cs_t = jnp.dot(dt_t * a_c, mask_t.astype(F32), precision=HIGH, preferred_element_type=F32)
    tot = cs[0:1, :] if rev else cs[CHUNK - 1:CHUNK, :]
    return mask, mask_t, a_r, pre, dt, cs, cs_t, tot


def _ssd_fwd(xbc, dt_raw, dt_t, args_f, args_b, name, gather=()):
    s = xbc.shape[0]
    nc = s // CHUNK
    hp = D_INNER // N_HEADS
    hpg = N_HEADS // N_GROUPS
    gs = N_GROUPS * D_STATE
    ng = len(gather)
    fwd_step = (nc * 11) // 16

    def chunk(x_ref, b_ref, c_ref, dt_ref, dtt_ref, alr, alc, bir, bic, y_ref, st_ref, h_scr, off, rev):
        mask, _, _, _, dt, cs, cs_t, tot = _ssd_common(dt_ref, dtt_ref, alr, alc, bir, bic, off, rev)
        xs = x_ref[...]
        ys = []
        for g in range(N_GROUPS):
            bg = b_ref[:, g * D_STATE:(g + 1) * D_STATE]
            cg = c_ref[:, g * D_STATE:(g + 1) * D_STATE]
            gm = lax.dot_general(cg, bg, NT, preferred_element_type=F32)
            hcat = h_scr[g]
            st_ref[0, g] = hcat
            ch = lax.dot_general(cg, hcat.astype(BF16), NT, preferred_element_type=F32)
            xdd = []
            for r in range(hpg):
                h = g * hpg + r
                csc, csr, tot_h = cs[:, h:h + 1], cs_t[h:h + 1, :], tot[:, h:h + 1]
                lm = jnp.exp(jnp.where(mask, csc - csr, NEG))
                xdf = xs[:, h * hp:(h + 1) * hp].astype(F32) * dt[:, h:h + 1]
                y = jnp.dot((gm * lm).astype(BF16), xdf.astype(BF16), preferred_element_type=F32)
                ys.append(y + jnp.exp(csc) * ch[:, r * hp:(r + 1) * hp])
                xdd.append((xdf * jnp.exp(tot_h - csc)).astype(BF16))
            snew = lax.dot_general(jnp.concatenate(xdd, axis=1), bg, TN, preferred_element_type=F32)
            for r in range(hpg):
                rs = slice(r * hp, (r + 1) * hp)
                h_scr[g, rs, :] = jnp.exp(tot[:, g * hpg + r:g * hpg + r + 1]) * hcat[rs] + snew[rs]
        y_ref[...] = jnp.concatenate(ys, axis=1)

    def body(*refs):
        in_f, in_b = refs[0:9], refs[9:18]
        g_src = refs[18:18 + ng]
        out_f, out_b = refs[18 + ng:20 + ng], refs[20 + ng:22 + ng]
        g_dst = refs[22 + ng:22 + 2 * ng]
        hs_f, hs_b = refs[22 + 2 * ng], refs[23 + 2 * ng]
        g_sems = refs[24 + 2 * ng:]
        step = pl.program_id(0)

        @pl.when(step == 0)
        def _():
            hs_f[...] = jnp.zeros_like(hs_f)
            hs_b[...] = jnp.zeros_like(hs_b)
            if ng:
                _gather_start(g_src, g_dst, *g_sems)

        chunk(*in_f, *out_f, hs_f, 0, False)
        chunk(*in_b, *out_b, hs_b, N_HEADS, True)

        if ng:
            @pl.when(step == fwd_step)
            def _():
                _gather_forward(g_src, g_dst, *g_sems)

            @pl.when(step == nc - 1)
            def _():
                _gather_finish(g_src, g_dst, *g_sems)

    small = lambda shape: pl.BlockSpec(shape, lambda c: (0, 0))

    def specs(cm):
        ins = [pl.BlockSpec((CHUNK, D_INNER), lambda c: (cm(c), 0)),
               pl.BlockSpec((CHUNK, gs), lambda c: (cm(c), D_INNER // gs)),
               pl.BlockSpec((CHUNK, gs), lambda c: (cm(c), D_INNER // gs + 1)),
               pl.BlockSpec((CHUNK, 128), lambda c: (cm(c), 0)),
               pl.BlockSpec((2 * N_HEADS, CHUNK), lambda c: (0, cm(c))),
               small((1, N_HEADS)), small((N_HEADS, 1)), small((1, N_HEADS)), small((N_HEADS, 1))]
        outs = [pl.BlockSpec((CHUNK, D_INNER), lambda c: (cm(c), 0)),
                pl.BlockSpec((1, N_GROUPS, hpg * hp, D_STATE), lambda c: (cm(c), 0, 0, 0))]
        return ins, outs

    ins_f, outs_f = specs(lambda c: c)
    ins_b, outs_b = specs(lambda c: nc - 1 - c)
    any_spec = pl.BlockSpec(memory_space=pl.ANY)
    one_dir = [SDS((s, D_INNER), F32), SDS((nc, N_GROUPS, hpg * hp, D_STATE), F32)]
    state = pltpu.VMEM((N_GROUPS, hpg * hp, D_STATE), F32)
    return pl.pallas_call(
        body, grid=(nc,), in_specs=ins_f + ins_b + [any_spec] * ng, out_specs=outs_f + outs_b + [any_spec] * ng,
        out_shape=one_dir + one_dir + [SDS((N_DEV,) + g.shape, g.dtype) for g in gather],
        scratch_shapes=[state, state] + (_exchange_sems(ng) if ng else []), name=name, compiler_params=_cp("arbitrary"))(
            xbc, xbc, xbc, dt_raw, dt_t, *args_f, xbc, xbc, xbc, dt_raw, dt_t, *args_b, *gather)


def _ssd_bwd(xbc, dt_raw, dt_t, al_r, al_c, bi_r, bi_c, states, dy, off, rev, name, exchange=()):
    s = xbc.shape[0]
    nc = s // CHUNK
    hp = D_INNER // N_HEADS
    gs = N_GROUPS * D_STATE
    hpg = N_HEADS // N_GROUPS
    cm = (lambda c: c) if rev else (lambda c: nc - 1 - c)
    nx = len(exchange)

    def body(*refs):
        x_ref, b_ref, c_ref, dt_ref, dtt_ref, alr, alc, bir, bic, st_ref, dy_ref = refs[:11]
        xch_src = refs[11:11 + nx]
        dx_ref, ddt_ref, dal_ref, dbi_ref = refs[11 + nx:15 + nx]
        xch_dst = refs[15 + nx:15 + 2 * nx]
        dh_scr = refs[15 + 2 * nx]
        xch_sems = refs[16 + 2 * nx:]

        @pl.when(pl.program_id(0) == 0)
        def _():
            dh_scr[...] = jnp.zeros_like(dh_scr)
            dal_ref[...] = jnp.zeros_like(dal_ref)
            dbi_ref[...] = jnp.zeros_like(dbi_ref)
            if nx:
                _exchange_start(xch_src, xch_dst, *xch_sems)

        if nx:
            @pl.when(pl.program_id(0) == nc - 1)
            def _():
                _exchange_wait(xch_src, xch_dst, *xch_sems)

        mask, mask_t, a_r, pre, dt, cs, cs_t, tot = _ssd_common(dt_ref, dtt_ref, alr, alc, bir, bic, off, rev)
        xs = x_ref[...]
        dyv = dy_ref[...]
        rows = lax.broadcasted_iota(jnp.int32, (CHUNK, 1), 0)
        end_row = (rows == 0) if rev else (rows == CHUNK - 1)
        lane_h = lax.broadcasted_iota(jnp.int32, (1, N_HEADS), 1)
        sub_h = lax.broadcasted_iota(jnp.int32, (N_HEADS, 1), 0)
        dcs_all = jnp.zeros((CHUNK, N_HEADS), F32)
        colw_all = jnp.zeros((N_HEADS, CHUNK), F32)
        dxsum_all = jnp.zeros((CHUNK, N_HEADS), F32)
        dxs, dbs, dcs_out = [], [], []
        for g in range(N_GROUPS):
            bg = b_ref[:, g * D_STATE:(g + 1) * D_STATE]
            cg = c_ref[:, g * D_STATE:(g + 1) * D_STATE]
            gm = lax.dot_general(cg, bg, NT, preferred_element_type=F32)
            hcat = st_ref[0, g]
            dhcat = dh_scr[g]
            hb, dhb = hcat.astype(BF16), dhcat.astype(BF16)
            ch = lax.dot_general(cg, hb, NT, preferred_element_type=F32)
            z = lax.dot_general(bg, dhb, NT, preferred_element_type=F32)
            dg_sum = jnp.zeros((CHUNK, CHUNK), F32)
            dchs, xdds, t_hs = [], [], []
            for r in range(hpg):
                h = g * hpg + r
                rs = slice(r * hp, (r + 1) * hp)
                csc, csr, tot_h = cs[:, h:h + 1], cs_t[h:h + 1, :], tot[:, h:h + 1]
                lm = jnp.exp(jnp.where(mask, csc - csr, NEG))
                xh = xs[:, h * hp:(h + 1) * hp].astype(F32)
                dtc = dt[:, h:h + 1]
                xdf = xh * dtc
                xd = xdf.astype(BF16)
                dyh = dyv[:, h * hp:(h + 1) * hp]
                dyb = dyh.astype(BF16)
                m = gm * lm
                ecs = jnp.exp(csc)
                dec = jnp.exp(tot_h - csc)
                t_h = jnp.exp(tot_h)
                dxd = lax.dot_general(m.astype(BF16), dyb, TN, preferred_element_type=F32)
                dm = lax.dot_general(dyb, xd, NT, preferred_element_type=F32)
                dg_sum = dg_sum + dm * lm
                w = dm * m
                dcs = jnp.sum(dyh * (ecs * ch[:, rs]), axis=1, keepdims=True)
                dchs.append((ecs * dyh).astype(BF16))
                dxd = dxd + dec * z[:, rs]
                ddec = jnp.sum(xdf * z[:, rs], axis=1, keepdims=True) * dec
                xdds.append((xdf * dec).astype(BF16))
                dtot = (jnp.sum(jnp.sum(dhcat[rs] * hcat[rs], axis=1, keepdims=True), axis=0, keepdims=True) * t_h
                        + jnp.sum(ddec, axis=0, keepdims=True))
                t_hs.append(t_h)
                dcs = dcs + jnp.sum(w, axis=1, keepdims=True) - ddec
                dcs = dcs + jnp.where(end_row, dtot, 0.0)
                colw_all = colw_all + (sub_h == h).astype(F32) * jnp.sum(w, axis=0, keepdims=True)
                onehot = (lane_h == h).astype(F32)
                dcs_all = dcs_all + dcs * onehot
                dxsum_all = dxsum_all + jnp.sum(dxd * xh, axis=1, keepdims=True) * onehot
                dxs.append(dxd * dtc)
            dgs = dg_sum.astype(BF16)
            dchc = jnp.concatenate(dchs, axis=1)
            dcs_out.append(jnp.dot(dgs, bg, preferred_element_type=F32) + jnp.dot(dchc, hb, preferred_element_type=F32))
            dbs.append(lax.dot_general(dgs, cg, TN, preferred_element_type=F32)
                       + jnp.dot(jnp.concatenate(xdds, axis=1), dhb, preferred_element_type=F32))
            dh_in = lax.dot_general(dchc, cg, TN, preferred_element_type=F32)
            for r in range(hpg):
                rs = slice(r * hp, (r + 1) * hp)
                dh_scr[g, rs, :] = dh_in[rs] + t_hs[r] * dhcat[rs]
        dx_ref[...] = jnp.concatenate(dxs + dbs + dcs_out, axis=1)
        mt = mask_t.astype(F32)
        da = (jnp.dot(mt, dcs_all, precision=HIGH, preferred_element_type=F32)
              - lax.dot_general(mt, colw_all, NT, precision=HIGH, preferred_element_type=F32))
        dal_ref[...] += jnp.sum(da * dt, axis=0, keepdims=True) * a_r
        ddt_raw = (da * a_r + dxsum_all) * _sigmoid(pre)
        ddt_ref[...] = ddt_raw
        dbi_ref[...] += jnp.sum(ddt_raw, axis=0, keepdims=True)

    small = lambda shape: pl.BlockSpec(shape, lambda c: (0, 0))
    in_specs = [pl.BlockSpec((CHUNK, D_INNER), lambda c: (cm(c), 0)),
                pl.BlockSpec((CHUNK, gs), lambda c: (cm(c), D_INNER // gs)),
                pl.BlockSpec((CHUNK, gs), lambda c: (cm(c), D_INNER // gs + 1)),
                pl.BlockSpec((CHUNK, 128), lambda c: (cm(c), 0)),
                pl.BlockSpec((2 * N_HEADS, CHUNK), lambda c: (0, cm(c))),
                small((1, N_HEADS)), small((N_HEADS, 1)), small((1, N_HEADS)), small((N_HEADS, 1)),
                pl.BlockSpec((1, N_GROUPS, hpg * hp, D_STATE), lambda c: (cm(c), 0, 0, 0)),
                pl.BlockSpec((CHUNK, D_INNER), lambda c: (cm(c), 0))]
    any_spec = pl.BlockSpec(memory_space=pl.ANY)
    return pl.pallas_call(
        body, grid=(nc,), in_specs=in_specs + [any_spec] * nx,
        out_specs=[pl.BlockSpec((CHUNK, XBC), lambda c: (cm(c), 0)), pl.BlockSpec((CHUNK, N_HEADS), lambda c: (cm(c), 0)),
                   small((1, N_HEADS)), small((1, N_HEADS))] + [any_spec] * nx,
        out_shape=[SDS((s, XBC), F32), SDS((s, N_HEADS), F32), SDS((1, N_HEADS), F32), SDS((1, N_HEADS), F32)]
        + [SDS(a.shape, a.dtype) for a in exchange],
        scratch_shapes=[pltpu.VMEM((N_GROUPS, hpg * hp, D_STATE), F32)] + (_exchange_sems(nx) if nx else []),
        name=name, compiler_params=_cp("arbitrary"))(xbc, xbc, xbc, dt_raw, dt_t, al_r, al_c, bi_r, bi_c, states, dy, *exchange)


def _gate_fwd(yf, yb, xbc, proj, dskip_x, norm_w, mix, mix_t, name):
    s = yf.shape[0]
    tm = _pick(s, (256, 128))
    gw = D_INNER // N_GROUPS
    zc = 3 * D_MODEL // D_INNER

    def body(yf_ref, yb_ref, x_ref, z_ref, d_ref, w_ref, _m, _mt, o_ref, ot_ref):
        y = yf_ref[...] + yb_ref[...] + d_ref[...] * x_ref[...].astype(F32)
        z = z_ref[...].astype(F32)
        gt = y * (z * _sigmoid(z))
        outs = []
        for g in range(N_GROUPS):
            gg = gt[:, g * gw:(g + 1) * gw]
            outs.append(gg * lax.rsqrt(jnp.mean(gg * gg, axis=-1, keepdims=True) + EPS))
        out = (jnp.concatenate(outs, axis=1) * w_ref[...]).astype(o_ref.dtype)
        o_ref[...] = out
        ot_ref[...] = out.T

    row = pl.BlockSpec((tm, D_INNER), lambda i: (i, 0))
    vec = pl.BlockSpec((1, D_INNER), lambda i: (0, 0))
    any_spec = pl.BlockSpec(memory_space=pl.ANY)
    cb = (mix.shape[1] - D_INNER) // D_INNER
    return pl.pallas_call(
        body, grid=(s // tm,), in_specs=[row, row, row, pl.BlockSpec((tm, D_INNER), lambda i: (i, zc)), vec, vec, any_spec, any_spec],
        out_specs=[pl.BlockSpec((tm, D_INNER), lambda i: (i, cb)), pl.BlockSpec((D_INNER, tm), lambda i: (cb, i))],
        out_shape=[SDS(mix.shape, BF16), SDS(mix_t.shape, BF16)], input_output_aliases={6: 0, 7: 1},
        name=name, compiler_params=_cp("parallel"))(yf, yb, xbc, proj, dskip_x, norm_w, mix, mix_t)


def _gate_bwd(yf, yb, xbc, proj, dskip_x, norm_w, dmix, dproj, name):
    s = yf.shape[0]
    tm = _pick(s, (256, 128))
    gw = D_INNER // N_GROUPS
    zc = 3 * D_MODEL // D_INNER

    def body(yf_ref, yb_ref, x_ref, z_ref, d_ref, w_ref, do_ref, _, dy_ref, dz_ref, dxs_ref, dw_ref, dd_ref):
        xf = x_ref[...].astype(F32)
        y = yf_ref[...] + yb_ref[...] + d_ref[...] * xf
        z = z_ref[...].astype(F32)
        sg = _sigmoid(z)
        sz = z * sg
        gt = y * sz
        do = do_ref[...].astype(F32)
        dgh = do * w_ref[...]
        ghs, dgts = [], []
        for g in range(N_GROUPS):
            gg = gt[:, g * gw:(g + 1) * gw]
            r = lax.rsqrt(jnp.mean(gg * gg, axis=-1, keepdims=True) + EPS)
            gh = gg * r
            dg = dgh[:, g * gw:(g + 1) * gw]
            ghs.append(gh)
            dgts.append(r * (dg - gh * jnp.mean(dg * gh, axis=-1, keepdims=True)))
        ghat = jnp.concatenate(ghs, axis=1)
        dgt = jnp.concatenate(dgts, axis=1)
        dy = dgt * sz
        dy_ref[...] = dy
        dz_ref[...] = (dgt * y * (sg * (1.0 + z * (1.0 - sg)))).astype(dz_ref.dtype)
        dxs_ref[...] = dy * d_ref[...]

        @pl.when(pl.program_id(0) == 0)
        def _():
            dw_ref[...] = jnp.zeros_like(dw_ref)
            dd_ref[...] = jnp.zeros_like(dd_ref)

        dw_ref[...] += jnp.sum(do * ghat, axis=0, keepdims=True)
        dd_ref[...] += jnp.sum(dy * xf, axis=0, keepdims=True)

    row = pl.BlockSpec((tm, D_INNER), lambda i: (i, 0))
    vec = pl.BlockSpec((1, D_INNER), lambda i: (0, 0))
    dz_shape, _, more, more_specs, alias = _slab(s, D_INNER, BF16, (dproj.shape[1], zc * D_INNER, dproj), 7, out_idx=1)
    return pl.pallas_call(
        body, grid=(s // tm,),
        in_specs=[row, row, row, pl.BlockSpec((tm, D_INNER), lambda i: (i, zc)), vec, vec, pl.BlockSpec((tm, D_INNER), lambda i: (i, 1))] + more_specs,
        out_specs=[row, pl.BlockSpec((tm, D_INNER), lambda i: (i, zc)), row, vec, vec],
        out_shape=[SDS((s, D_INNER), F32), dz_shape, SDS((s, D_INNER), F32), SDS((1, D_INNER), F32), SDS((1, D_INNER), F32)],
        input_output_aliases=alias, name=name, compiler_params=_cp("arbitrary"))(yf, yb, xbc, proj, dskip_x, norm_w, dmix, *more)


def _adamw(parts, w, m, v, name):
    r, c = w.shape
    tr = _pick(r, (256, 352, 128))

    def body(p_ref, w_ref, m_ref, v_ref, g_ref, d_ref, nm_ref, nv_ref):
        g = p_ref[0].astype(F32)
        for i in range(1, N_DEV):
            g = g + p_ref[i].astype(F32)
        mn = B1 * m_ref[...] + (1.0 - B1) * g
        vn = B2 * v_ref[...] + (1.0 - B2) * (g * g)
        m_hat = mn / (1.0 - B1 ** STEP)
        v_hat = vn / (1.0 - B2 ** STEP)
        g_ref[...] = g
        d_ref[...] = -LR * (m_hat / (jnp.sqrt(v_hat) + AEPS) + WD * w_ref[...])
        nm_ref[...] = mn
        nv_ref[...] = vn

    blk = pl.BlockSpec((tr, c), lambda i: (i, 0))
    return pl.pallas_call(
        body, grid=(r // tr,), in_specs=[pl.BlockSpec((N_DEV, tr, c), lambda i: (0, i, 0)), blk, blk, blk],
        out_specs=[blk, blk, blk, blk], out_shape=[SDS((r, c), F32)] * 4, name=name, compiler_params=_cp("parallel"))(parts, w, m, v)


def _sum_parts(parts, name):
    _, r, c = parts.shape

    def body(p_ref, o_ref):
        g = p_ref[0]
        for i in range(1, N_DEV):
            g = g + p_ref[i]
        o_ref[...] = g

    return pl.pallas_call(body, out_shape=SDS((r, c), F32), name=name)(parts)


def _adamw_small(g, w, m, v, name):
    def body(g_ref, w_ref, m_ref, v_ref, d_ref, nm_ref, nv_ref):
        gv = g_ref[...]
        mn = B1 * m_ref[...] + (1.0 - B1) * gv
        vn = B2 * v_ref[...] + (1.0 - B2) * (gv * gv)
        m_hat = mn / (1.0 - B1 ** STEP)
        v_hat = vn / (1.0 - B2 ** STEP)
        d_ref[...] = -LR * (m_hat / (jnp.sqrt(v_hat) + AEPS) + WD * w_ref[...])
        nm_ref[...] = mn
        nv_ref[...] = vn

    return pl.pallas_call(body, out_shape=[SDS(g.shape, F32)] * 3, name=name)(g, w, m, v)


def _my_index():
    return 4 * lax.axis_index("x") + 2 * lax.axis_index("y") + lax.axis_index("c")


def _all_gather(shards, name):
    n = len(shards)

    def body(*refs):
        srcs, outs = refs[:n], refs[n:2 * n]
        _gather_start(srcs, outs, *refs[2 * n:])
        _gather_forward(srcs, outs, *refs[2 * n:])
        _gather_finish(srcs, outs, *refs[2 * n:])

    any_spec = pl.BlockSpec(memory_space=pl.ANY)
    return pl.pallas_call(
        body, in_specs=[any_spec] * n, out_specs=[any_spec] * n,
        out_shape=[SDS((N_DEV,) + s.shape, s.dtype) for s in shards], scratch_shapes=_exchange_sems(n), name=name)(*shards)


def _exchange(arrays, name):
    n = len(arrays)

    def body(*refs):
        srcs, outs = refs[:n], refs[n:2 * n]
        _exchange_start(srcs, outs, *refs[2 * n:])
        _exchange_wait(srcs, outs, *refs[2 * n:])

    any_spec = pl.BlockSpec(memory_space=pl.ANY)
    return pl.pallas_call(
        body, in_specs=[any_spec] * n, out_specs=[any_spec] * n,
        out_shape=[SDS(a.shape, a.dtype) for a in arrays], scratch_shapes=_exchange_sems(n), name=name)(*arrays)


def _to_pattern(t, d):
    if d == 1:
        return t
    s, w = t.shape
    return t.reshape(s // d, d, w).transpose(1, 0, 2).reshape(s, w)


def _from_pattern(t, d):
    if d == 1:
        return t
    s, w = t.shape
    return t.reshape(d, s // d, w).transpose(1, 0, 2).reshape(s, w)


def _pad_lanes(t, n):
    return jnp.pad(t, ((0, 0), (0, n - t.shape[1])))


def _to_shards(g, axis):
    r, c = g.shape
    if axis == 0:
        return g.reshape(N_DEV, r // N_DEV, c)
    return g.reshape(r, N_DEV, c // N_DEV).transpose(1, 0, 2)


def _local_step(x, target, p, late_shards=(), early_exchange=True):
    s = x.shape[0]
    tabs_f, tabs_b = _rope_tables(s)
    expand = jnp.asarray(np.repeat(np.eye(N_HEADS, dtype=np.float32), HEAD_DIM, axis=1))
    w_main, w_dt = p["w_in"][:, :MAIN_W], _pad_lanes(p["w_in"][:, MAIN_W:], 128)
    al_r = {"f": p["a_log_f"], "b": p["a_log_b"]}
    bi_r = {"f": p["dt_bias_f"], "b": p["dt_bias_b"]}
    dskip_x = jnp.repeat(p["d_skip"], D_INNER // N_HEADS, axis=1)
    ssm_w3, ffn_w3 = p["ssm_conv_w"].T, p["ffn_conv_w"].T

    h1, h1t = _rmsnorm_fwd(x, p["norm1_w"], "norm1_fwd")
    proj = _matmul(h1, w_main, name="in_proj")
    dt_raw = _matmul(h1, w_dt, name="in_proj_dt", out_dtype=F32)
    dt_t = dt_raw[:, :2 * N_HEADS].T
    perms = {d: _perm_matrices(d) for d in DILATIONS[1:]}
    qkv = _rope_fwd(proj, tabs_f, perms, "rope_fwd")
    v_col = 2
    os_, lses = [], []
    for d, qkv_p in zip(DILATIONS, qkv):
        o_p, lse_p = _attn_fwd(qkv_p, qkv_p, v_col, s // d, f"attn_fwd_d{d}")
        os_.append(o_p)
        lses.append(_from_pattern(lse_p, d))
    mix, mix_t, lse_tot = _attn_combine(os_, lses, perms, expand, "attn_combine")

    xbc = _conv(proj, 3 * D_MODEL + D_INNER, XBC, ssm_w3, p["ssm_conv_b"], True, "ssm_conv_fwd")
    col = lambda r: r.reshape(N_HEADS, 1)
    ssd_args = {k: (al_r[k], col(al_r[k]), bi_r[k], col(bi_r[k])) for k in ("f", "b")}
    yf, st_f, yb, st_b, *got = _ssd_fwd(xbc, dt_raw, dt_t, ssd_args["f"], ssd_args["b"], "ssd_fwd", gather=late_shards)
    if late_shards:
        p = dict(p, w_out=got[0].reshape(2 * D_MODEL, D_MODEL), w_down=got[2].reshape(D_FF, D_MODEL),
                 w_up=got[1].transpose(1, 0, 2).reshape(D_MODEL, 2 * D_FF))
    mix, mix_t = _gate_fwd(yf, yb, xbc, proj, dskip_x, p["ssm_norm_w"], mix, mix_t, "ssm_gate_fwd")

    x2 =_matmul(mix, p["w_out"], name="out_proj", out_dtype=F32, residual=x)
    h2, h2t = _rmsnorm_fwd(x2, p["norm2_w"], "norm2_fwd")
    u = _matmul(h2, p["w_up"], name="ffn_up")
    act, act_t = _ffn_gate_fwd(u, ffn_w3, p["ffn_conv_b"], "ffn_gate_fwd")
    x3 = _matmul(act, p["w_down"], name="ffn_down", out_dtype=F32, residual=x2)

    dx3, dx3b, g_final, loss = _final_norm_loss(x3, p["final_norm_w"].reshape(1, D_MODEL), target, "final_norm_loss")
    g_w_down = _matmul(act_t, dx3b, name="dw_down")
    dact = _matmul(dx3b, p["w_down"], name="d_act", trans_b=True)
    dug, duu, dwg, dwu, dbg, dbu = _ffn_gate_bwd(u, ffn_w3, p["ffn_conv_b"], dact, "ffn_gate_bwd")
    du = _conv(dug, 0, D_FF, ffn_w3, p["ffn_conv_b"], False, "ffn_conv_bwd_gate", slab=(2 * D_FF, 0, None), transpose=True)
    du = _conv(duu, 0, D_FF, ffn_w3, p["ffn_conv_b"], False, "ffn_conv_bwd_up", slab=(2 * D_FF, D_FF, du), transpose=True, wcol0=D_FF)
    g_w_up = _matmul(h2t, du, name="dw_up")
    dh2 = _matmul(du, p["w_up"], name="d_h2", trans_b=True)
    dx2, dx2b, g_norm2 = _rmsnorm_bwd(x2, p["norm2_w"], dh2, dx3, "norm2_bwd")
    g_w_out = _matmul(mix_t, dx2b, name="dw_out")
    dmix = _matmul(dx2b, p["w_out"], name="d_mix", trans_b=True)

    delta, do_pat = _attn_delta(dmix, mix, expand.T, perms, "attn_delta")
    dqs, dks, dvs = [], [], []
    for d, qkv_p, do_p in zip(DILATIONS, qkv, [dmix] + do_pat):
        lse_p, dl_p = _to_pattern(lse_tot, d), _to_pattern(delta, d)
        dqs.append(_attn_bwd_dq(qkv_p, qkv_p, v_col, do_p, lse_p, dl_p, s // d, f"attn_bwd_dq_d{d}"))
        dk, dv = _attn_bwd_dkv(qkv_p, qkv_p, v_col, do_p, lse_p.T, dl_p.T, s // d, f"attn_bwd_dkv_d{d}")
        dks.append(dk)
        dvs.append(dv)
    dproj = _sum3_rope(dqs, perms, tabs_b, "rope_bwd_q", slab=(MAIN_W, 0, None))
    dproj = _sum3_rope(dks, perms, tabs_b, "rope_bwd_k", slab=(MAIN_W, D_MODEL, dproj))
    dproj = _sum3_rope(dvs, perms, None, "sum_dv", slab=(MAIN_W, 2 * D_MODEL, dproj))

    dy, dproj, dxs_skip, g_ssm_norm, g_dskip_lanes = _gate_bwd(yf, yb, xbc, proj, dskip_x, p["ssm_norm_w"], dmix, dproj, "ssm_gate_bwd")
    early = [_to_shards(g_w_out, 0), _to_shards(g_w_up, 1), _to_shards(g_w_down, 0)] if early_exchange else []
    dxbc_f, ddt_f, g_al_f, g_bi_f, *got = _ssd_bwd(xbc, dt_raw, dt_t, *ssd_args["f"], st_f, dy, 0, False, "ssd_bwd_f", exchange=early)
    if early_exchange:
        g_w_out, g_w_up, g_w_down = got
    dxbc_b, ddt_b, g_al_b, g_bi_b = _ssd_bwd(xbc, dt_raw, dt_t, *ssd_args["b"], st_b, dy, N_HEADS, True, "ssd_bwd_b")
    dpre, g_ssm_w3, g_ssm_cb = _conv_silu_bwd(proj, 3 * D_MODEL + D_INNER, XBC, ssm_w3, p["ssm_conv_b"],
                                              [dxbc_f, dxbc_b, dxs_skip], [XBC, XBC, D_INNER], "ssm_conv_bwd")
    dproj = _conv(dpre, 0, XBC, ssm_w3, p["ssm_conv_b"], False, "ssm_conv_bwd_x", transpose=True,
                  slab=(MAIN_W, 3 * D_MODEL + D_INNER, dproj))

    ddt =_pad_lanes(jnp.concatenate([ddt_f, ddt_b], axis=1), 128).astype(BF16)
    g_w_main = _matmul(h1t, dproj, name="dw_in")
    g_w_dt = _matmul(h1t, ddt, name="dw_in_dt")
    g_w_in = jnp.concatenate([g_w_main, g_w_dt[:, :2 * N_HEADS]], axis=1)
    if early_exchange:
        dh1, g_w_in = _matmul(dproj, w_main, name="d_h1", trans_b=True, out_dtype=F32, exchange=[_to_shards(g_w_in, 1)])
    else:
        dh1 = _matmul(dproj, w_main, name="d_h1", trans_b=True, out_dtype=F32)
    dh1 = _matmul(ddt, w_dt, name="d_h1_dt", trans_b=True, out_dtype=F32, residual=dh1)
    grad_x, _, g_norm1 = _rmsnorm_bwd(x, p["norm1_w"], dh1, dx2, "norm1_bwd")

    g_dskip = jnp.sum(g_dskip_lanes.reshape(N_HEADS, D_INNER // N_HEADS), axis=1).reshape(1, N_HEADS)
    small = {
        "norm1_w": g_norm1, "ssm_conv_w": g_ssm_w3.T, "ssm_conv_b": g_ssm_cb, "a_log_f": g_al_f, "a_log_b": g_al_b,
        "dt_bias_f": g_bi_f, "dt_bias_b": g_bi_b, "d_skip": g_dskip, "ssm_norm_w": g_ssm_norm, "norm2_w": g_norm2,
        "ffn_conv_w": jnp.concatenate([dwg, dwu], axis=1).T, "ffn_conv_b": jnp.concatenate([dbg, dbu], axis=1), "final_norm_w": g_final,
    }
    big = {"w_in": g_w_in, "w_out": g_w_out, "w_up": g_w_up, "w_down": g_w_down}
    return loss[0, 0], grad_x, big, small


SMALL_ORDER = ("norm1_w", "ssm_conv_w", "ssm_conv_b", "a_log_f", "a_log_b", "dt_bias_f", "dt_bias_b", "d_skip",
               "ssm_norm_w", "norm2_w", "ffn_conv_w", "ffn_conv_b", "final_norm_w")
SHARDED_SMALL = ("ssm_conv_w", "ffn_conv_w")
BIG_ORDER = ("w_in", "w_out", "w_up", "w_down")


def _pack(vals):
    rows = []
    for v in vals:
        f = v.reshape(-1).astype(F32)
        n = -(-f.shape[0] // 128) * 128
        rows.append(jnp.pad(f, (0, n - f.shape[0])).reshape(-1, 128))
    out = jnp.concatenate(rows, axis=0)
    pad = -out.shape[0] % 8
    return jnp.pad(out, ((0, pad), (0, 0)))


def _unpack(packed, shapes):
    out, r = [], 0
    for shp in shapes:
        n = math.prod(shp)
        nr = -(-n // 128)
        out.append(packed[r:r + nr].reshape(-1)[:n].reshape(shp))
        r += nr
    return out


def kernel(x, norm1_w, w_in, ssm_conv_w, ssm_conv_b, a_log_f, a_log_b, dt_bias_f, dt_bias_b, d_skip, ssm_norm_w, w_out, norm2_w, w_up, ffn_conv_w, ffn_conv_b, w_down, final_norm_w, loss_target, m_norm1_w, m_w_in, m_ssm_conv_w, m_ssm_conv_b, m_a_log_f, m_a_log_b, m_dt_bias_f, m_dt_bias_b, m_d_skip, m_ssm_norm_w, m_w_out, m_norm2_w, m_w_up, m_ffn_conv_w, m_ffn_conv_b, m_w_down, m_final_norm_w, v_norm1_w, v_w_in, v_ssm_conv_w, v_ssm_conv_b, v_a_log_f, v_a_log_b, v_dt_bias_f, v_dt_bias_b, v_d_skip, v_ssm_norm_w, v_w_out, v_norm2_w, v_w_up, v_ffn_conv_w, v_ffn_conv_b, v_w_down, v_final_norm_w):
    w = dict(norm1_w=norm1_w, w_in=w_in, ssm_conv_w=ssm_conv_w, ssm_conv_b=ssm_conv_b, a_log_f=a_log_f, a_log_b=a_log_b,
             dt_bias_f=dt_bias_f, dt_bias_b=dt_bias_b, d_skip=d_skip, ssm_norm_w=ssm_norm_w, w_out=w_out, norm2_w=norm2_w,
             w_up=w_up, ffn_conv_w=ffn_conv_w, ffn_conv_b=ffn_conv_b, w_down=w_down, final_norm_w=final_norm_w)
    mo = dict(norm1_w=m_norm1_w, w_in=m_w_in, ssm_conv_w=m_ssm_conv_w, ssm_conv_b=m_ssm_conv_b, a_log_f=m_a_log_f, a_log_b=m_a_log_b,
              dt_bias_f=m_dt_bias_f, dt_bias_b=m_dt_bias_b, d_skip=m_d_skip, ssm_norm_w=m_ssm_norm_w, w_out=m_w_out, norm2_w=m_norm2_w,
              w_up=m_w_up, ffn_conv_w=m_ffn_conv_w, ffn_conv_b=m_ffn_conv_b, w_down=m_w_down, final_norm_w=m_final_norm_w)
    vo = dict(norm1_w=v_norm1_w, w_in=v_w_in, ssm_conv_w=v_ssm_conv_w, ssm_conv_b=v_ssm_conv_b, a_log_f=v_a_log_f, a_log_b=v_a_log_b,
              dt_bias_f=v_dt_bias_f, dt_bias_b=v_dt_bias_b, d_skip=v_d_skip, ssm_norm_w=v_ssm_norm_w, w_out=v_w_out, norm2_w=v_norm2_w,
              w_up=v_w_up, ffn_conv_w=v_ffn_conv_w, ffn_conv_b=v_ffn_conv_b, w_down=v_w_down, final_norm_w=v_final_norm_w)
    me = _my_index()

    g_in, g_conv = _all_gather([w["w_in"][0].astype(BF16), _pack([w["ssm_conv_w"][0], w["ffn_conv_w"][0]])], "w_in_all_gather")
    conv_rows = [_unpack(g_conv[i], [ssm_conv_w.shape[1:], ffn_conv_w.shape[1:]]) for i in range(N_DEV)]
    full = {
        "w_in": g_in.transpose(1, 0, 2).reshape(D_MODEL, N_DEV * w_in.shape[2]),
        "ssm_conv_w": jnp.concatenate([c[0] for c in conv_rows], axis=0),
        "ffn_conv_w": jnp.concatenate([c[1] for c in conv_rows], axis=0),
    }
    for k in ("norm1_w", "ssm_conv_b", "a_log_f", "a_log_b", "dt_bias_f", "dt_bias_b", "d_skip", "ssm_norm_w", "norm2_w", "ffn_conv_b",
              "final_norm_w"):
        full[k] = w[k]
    late = [w["w_out"][0].astype(BF16), w["w_up"][0].astype(BF16), w["w_down"][0].astype(BF16)]

    loss_part, grad_x, big, small = _local_step(x[0], loss_target[0], full, late)

    small_shapes = [(1,)] + [small[k].shape for k in SMALL_ORDER]
    packed = _pack([loss_part] + [small[k] for k in SMALL_ORDER])
    out_small = jnp.broadcast_to(packed[None], (N_DEV,) + packed.shape)
    (r_small,) = _exchange([out_small], "small_grads_exchange")
    r_in, r_out, r_up, r_down = big["w_in"], big["w_out"], big["w_up"], big["w_down"]

    outs_g, outs_d, outs_m, outs_v = {}, {}, {}, {}
    for k, parts in zip(BIG_ORDER, (r_in, r_out, r_up, r_down)):
        g, dlt, nm, nv = _adamw(parts, w[k][0], mo[k][0], vo[k][0], f"adamw_{k}")
        outs_g[k], outs_d[k], outs_m[k], outs_v[k] = g[None], dlt[None], nm[None], nv[None]
    tot = _unpack(_sum_parts(r_small, "small_grads_sum"), small_shapes)
    loss = tot[0][0]
    gs = dict(zip(SMALL_ORDER, tot[1:]))
    g_own = {}
    for k in SMALL_ORDER:
        if k in SHARDED_SMALL:
            rows = w[k].shape[1]
            g_own[k] = lax.dynamic_slice_in_dim(gs[k], me * rows, rows, axis=0)[None]
        else:
            g_own[k] = gs[k].reshape(w[k].shape)
    shapes = [w[k].shape for k in SMALL_ORDER]
    d_s, m_s, v_s = _adamw_small(_pack([g_own[k] for k in SMALL_ORDER]), _pack([w[k] for k in SMALL_ORDER]),
                                 _pack([mo[k] for k in SMALL_ORDER]), _pack([vo[k] for k in SMALL_ORDER]), "adamw_small")
    for k, a, b, c in zip(SMALL_ORDER, _unpack(d_s, shapes), _unpack(m_s, shapes), _unpack(v_s, shapes)):
        outs_g[k], outs_d[k], outs_m[k], outs_v[k] = g_own[k], a, b, c

    order = ("norm1_w", "w_in", "ssm_conv_w", "ssm_conv_b", "a_log_f", "a_log_b", "dt_bias_f", "dt_bias_b", "d_skip", "ssm_norm_w",
             "w_out", "norm2_w", "w_up", "ffn_conv_w", "ffn_conv_b", "w_down", "final_norm_w")
    return (loss, grad_x[None], *[outs_g[k] for k in order], *[outs_d[k] for k in order],
            *[outs_m[k] for k in order], *[outs_v[k] for k in order])
```

```python
import math

import numpy as np
import jax
import jax.numpy as jnp
from jax import lax
from jax.experimental import pallas as pl
from jax.experimental.pallas import tpu as pltpu

F32 = jnp.float32
BF16 = jnp.bfloat16
SDS = jax.ShapeDtypeStruct

N_DEV = 8
D_MODEL = 1024
N_HEADS = 16
HEAD_DIM = 64
ROPE_DIM = 16
ROPE_THETA = 500000.0
DILATIONS = (1, 4, 16)
BAND_HALF = 64
D_INNER = 1024
N_GROUPS = 4
D_STATE = 128
CHUNK = 128
XBC = D_INNER + 2 * N_GROUPS * D_STATE
D_FF = 2816
MAIN_W = 3 * D_MODEL + D_INNER + XBC
EPS = 1e-6
LR, B1, B2, AEPS, WD, STEP = 0.001, 0.9, 0.999, 1e-08, 0.01, 10
NEG = -1e30
VMEM_LIMIT = 56 * 1024 * 1024
MESH = pl.DeviceIdType.MESH
HIGH = lax.Precision.HIGHEST
NT = (((1,), (1,)), ((), ()))
TN = (((0,), (0,)), ((), ()))


def _cp(*sem):
    return pltpu.CompilerParams(dimension_semantics=sem, vmem_limit_bytes=VMEM_LIMIT)


def _pick(n, cands):
    for c in cands:
        if n % c == 0:
            return c
    raise ValueError(f"no tile for {n}")


def _sigmoid(x):
    return 1.0 / (1.0 + jnp.exp(-x))


def _softplus(x):
    return jnp.maximum(x, 0.0) + jnp.log1p(jnp.exp(-jnp.abs(x)))


def _slab(s, width, dtype, slab, n_in, out_idx=0):
    if slab is None:
        return SDS((s, width), dtype), 0, [], [], {}
    total, col0, into = slab
    if into is None:
        return SDS((s, total), dtype), col0, [], [], {}
    return SDS((s, total), dtype), col0, [into], [pl.BlockSpec(memory_space=pl.ANY)], {n_in: out_idx}


def _matmul(a, b, *, name, trans_b=False, out_dtype=BF16, residual=None):
    m, k = a.shape
    n = b.shape[0] if trans_b else b.shape[1]
    tk = k if k <= 2048 else _pick(k, (2048, 1408, 1024, 512))
    nk = k // tk
    if nk == 1:
        tm = _pick(m, (2048, 1408, 1024, 512, 256, 128))
        tn = _pick(n, (512, 256, 128))
    else:
        tm = _pick(m, (1024, 1408, 512, 256, 128))
        tn = _pick(n, (1024, 1408, 512, 256, 128))
    dn = NT if trans_b else (((1,), (0,)), ((), ()))

    def body(*refs):
        a_ref, b_ref = refs[0], refs[1]
        o_ref, acc = refs[-2], refs[-1]
        kk = pl.program_id(2)

        @pl.when(kk == 0)
        def _():
            acc[...] = jnp.zeros_like(acc)

        acc[...] += lax.dot_general(a_ref[...], b_ref[...], dn, preferred_element_type=F32)

        @pl.when(kk == nk - 1)
        def _():
            r = acc[...]
            if residual is not None:
                r = r + refs[2][...].astype(F32)
            o_ref[...] = r.astype(o_ref.dtype)

    in_specs = [pl.BlockSpec((tm, tk), lambda i, j, kk: (i, kk)),
                pl.BlockSpec((tn, tk), lambda i, j, kk: (j, kk)) if trans_b else pl.BlockSpec((tk, tn), lambda i, j, kk: (kk, j))]
    args = [a, b]
    if residual is not None:
        in_specs.append(pl.BlockSpec((tm, tn), lambda i, j, kk: (i, j)))
        args.append(residual)
    return pl.pallas_call(
        body, grid=(m // tm, n // tn, nk), in_specs=in_specs,
        out_specs=pl.BlockSpec((tm, tn), lambda i, j, kk: (i, j)),
        out_shape=SDS((m, n), out_dtype), scratch_shapes=[pltpu.VMEM((tm, tn), F32)],
        name=name, compiler_params=_cp("parallel", "parallel", "arbitrary"))(*args)


def _rmsnorm_fwd(x, w, name):
    s, d = x.shape
    tm = _pick(s, (512, 128))

    def body(x_ref, w_ref, o_ref, ot_ref):
        xf = x_ref[...]
        r = lax.rsqrt(jnp.mean(xf * xf, axis=-1, keepdims=True) + EPS)
        out = (xf * r * w_ref[...]).astype(o_ref.dtype)
        o_ref[...] = out
        ot_ref[...] = out.T

    return pl.pallas_call(
        body, grid=(s // tm,), in_specs=[pl.BlockSpec((tm, d), lambda i: (i, 0)), pl.BlockSpec((1, d), lambda i: (0, 0))],
        out_specs=[pl.BlockSpec((tm, d), lambda i: (i, 0)), pl.BlockSpec((d, tm), lambda i: (0, i))],
        out_shape=[SDS((s, d), BF16), SDS((d, s), BF16)], name=name, compiler_params=_cp("parallel"))(x, w)


def _proj_norm_bwd(da, wt, da2, wt2, x, w, dres, name, exchange=()):
    s, k = da.shape
    d = x.shape[1]
    tm = _pick(s, (512, 128))
    tk = _pick(k, (2048, 1408, 1024, 512))
    nk = k // tk
    nx = len(exchange)
    grid = (s // tm, nk)

    def body(*refs):
        a_ref, b_ref, a2_ref, b2_ref, x_ref, w_ref, dres_ref = refs[:7]
        xch_src = refs[7:7 + nx]
        dx_ref, dxb_ref, dw_ref = refs[7 + nx:10 + nx]
        xch_dst = refs[10 + nx:10 + 2 * nx]
        acc = refs[10 + 2 * nx]
        xch_sems = refs[11 + 2 * nx:]
        i, kk = pl.program_id(0), pl.program_id(1)

        @pl.when((i == 0) & (kk == 0))
        def _():
            dw_ref[...] = jnp.zeros_like(dw_ref)
            if nx:
                _exchange_start(xch_src, xch_dst, *xch_sems)

        @pl.when(kk == 0)
        def _():
            acc[...] = lax.dot_general(a2_ref[...], b2_ref[...], NT, preferred_element_type=F32)

        acc[...] += lax.dot_general(a_ref[...], b_ref[...], NT, preferred_element_type=F32)

        @pl.when(kk == nk - 1)
        def _():
            dh = acc[...]
            xf = x_ref[...]
            r = lax.rsqrt(jnp.mean(xf * xf, axis=-1, keepdims=True) + EPS)
            xhat = xf * r
            g = dh * w_ref[...]
            dx = dres_ref[...] + r * (g - xhat * jnp.mean(g * xhat, axis=-1, keepdims=True))
            dx_ref[...] = dx
            dxb_ref[...] = dx.astype(dxb_ref.dtype)
            dw_ref[...] += jnp.sum(dh * xhat, axis=0, keepdims=True)

        if nx:
            @pl.when((i == grid[0] - 1) & (kk == nk - 1))
            def _():
                _exchange_wait(xch_src, xch_dst, *xch_sems)

    row = pl.BlockSpec((tm, d), lambda i, kk: (i, 0))
    vec = pl.BlockSpec((1, d), lambda i, kk: (0, 0))
    any_spec = pl.BlockSpec(memory_space=pl.ANY)
    in_specs = [pl.BlockSpec((tm, tk), lambda i, kk: (i, kk)), pl.BlockSpec((d, tk), lambda i, kk: (0, kk)),
                pl.BlockSpec((tm, da2.shape[1]), lambda i, kk: (i, 0)), pl.BlockSpec((d, wt2.shape[1]), lambda i, kk: (0, 0)),
                row, vec, row]
    return pl.pallas_call(
        body, grid=grid, in_specs=in_specs + [any_spec] * nx, out_specs=[row, row, vec] + [any_spec] * nx,
        out_shape=[SDS((s, d), F32), SDS((s, d), BF16), SDS((1, d), F32)] + [SDS(e.shape, e.dtype) for e in exchange],
        scratch_shapes=[pltpu.VMEM((tm, d), F32)] + (_exchange_sems(nx) if nx else []),
        name=name, compiler_params=_cp("arbitrary", "arbitrary"))(da, wt, da2, wt2, x, w, dres, *exchange)


def _final_norm_loss(x, w, target, name):
    s, d = x.shape
    tm = _pick(s, (512, 128))

    def body(x_ref, w_ref, t_ref, dx_ref, dxb_ref, dw_ref, loss_ref):
        xf = x_ref[...]
        r = lax.rsqrt(jnp.mean(xf * xf, axis=-1, keepdims=True) + EPS)
        xhat = xf * r
        wv = w_ref[...]
        e = xhat * wv - t_ref[...]
        dy = e * (1.0 / d)
        g = dy * wv
        dx = r * (g - xhat * jnp.mean(g * xhat, axis=-1, keepdims=True))
        dx_ref[...] = dx
        dxb_ref[...] = dx.astype(dxb_ref.dtype)

        @pl.when(pl.program_id(0) == 0)
        def _():
            dw_ref[...] = jnp.zeros_like(dw_ref)
            loss_ref[...] = jnp.zeros_like(loss_ref)

        dw_ref[...] += jnp.sum(dy * xhat, axis=0, keepdims=True)
        loss_ref[...] += jnp.sum(jnp.sum(e * e, axis=1, keepdims=True), axis=0, keepdims=True) * (0.5 / d)

    row = pl.BlockSpec((tm, d), lambda i: (i, 0))
    vec = pl.BlockSpec((1, d), lambda i: (0, 0))
    return pl.pallas_call(
        body, grid=(s // tm,), in_specs=[row, vec, row], out_specs=[row, row, vec, pl.BlockSpec((1, 128), lambda i: (0, 0))],
        out_shape=[SDS((s, d), F32), SDS((s, d), BF16), SDS((1, d), F32), SDS((1, 128), F32)],
        name=name, compiler_params=_cp("arbitrary"))(x, w, target)


def _rope_tables(s):
    half = ROPE_DIM // 2
    f32 = np.float32
    inv_freq = np.power(f32(ROPE_THETA), -np.arange(half, dtype=f32) * f32(2.0) / f32(ROPE_DIM)).astype(f32)
    ang = (np.arange(s, dtype=f32)[:, None] * inv_freq[None, :]).astype(f32)
    cos, sin = np.cos(ang).astype(f32), np.sin(ang).astype(f32)
    z = np.zeros((s, HEAD_DIM - ROPE_DIM), f32)
    zh = np.zeros((s, half), f32)
    c = np.concatenate([cos, cos, z + 1.0], axis=1)
    sa = np.concatenate([zh, sin, z], axis=1)
    sb = np.concatenate([-sin, zh, z], axis=1)
    two = lambda t: np.concatenate([t, t], axis=1)
    c, sa, sb = two(c), two(sa), two(sb)
    fwd, bwd = (c, sa, sb), (c, np.roll(sb, half, axis=1), np.roll(sa, -half, axis=1))
    return tuple(jnp.asarray(t) for t in fwd), tuple(jnp.asarray(t) for t in bwd)


PERM_TILE = 256


def _perm_matrices(d):
    n = PERM_TILE // d
    o = np.arange(PERM_TILE)
    p = np.zeros((PERM_TILE, PERM_TILE), np.float32)
    p[o, (o % n) * d + o // n] = 1.0
    return jnp.asarray(p, dtype=BF16), jnp.asarray(p.T.copy(), dtype=BF16)


def _store_pattern(o_ref, tile, perm_ref, d, cols=slice(None)):
    n = PERM_TILE // d
    z = jnp.dot(perm_ref[...], tile, preferred_element_type=F32).astype(o_ref.dtype)
    for r in range(d):
        o_ref[r, :, cols] = z[r * n:(r + 1) * n]


def _load_pattern(x_ref, perm_ref, d):
    tile = jnp.concatenate([x_ref[r] for r in range(d)], axis=0)
    return jnp.dot(perm_ref[...], tile, preferred_element_type=F32)


def _pattern_spec(d, w, col=0):
    return pl.BlockSpec((d, PERM_TILE // d, w), lambda i, *_: (0, i, col))


def _rope_fwd(proj, tabs, perms, name):
    s = proj.shape[0]
    tm = PERM_TILE
    half = ROPE_DIM // 2
    wb = D_MODEL
    nd = len(DILATIONS) - 1

    def body(x_ref, c_ref, sa_ref, sb_ref, *rest):
        perm_refs, o_ref, op_refs = rest[:nd], rest[nd], rest[nd + 1:]
        is_v = pl.program_id(1) == 2
        c = jnp.where(is_v, 1.0, c_ref[...])
        sa = jnp.where(is_v, 0.0, sa_ref[...])
        sb = jnp.where(is_v, 0.0, sb_ref[...])
        for j in range(wb // 128):
            x = x_ref[:, j * 128:(j + 1) * 128].astype(F32)
            o_ref[:, j * 128:(j + 1) * 128] = (x * c + pltpu.roll(x, half, 1) * sa + pltpu.roll(x, 128 - half, 1) * sb).astype(o_ref.dtype)
        y = o_ref[...]
        for d, perm_ref, op_ref in zip(DILATIONS[1:], perm_refs, op_refs):
            _store_pattern(op_ref, y, perm_ref, d)

    blk = pl.BlockSpec((tm, wb), lambda i, j: (i, j))
    tab = pl.BlockSpec((tm, 128), lambda i, j: (i, 0))
    pm = pl.BlockSpec((tm, tm), lambda i, j: (0, 0))
    outs = pl.pallas_call(
        body, grid=(s // tm, 3), in_specs=[blk, tab, tab, tab] + [pm] * nd,
        out_specs=[blk] + [pl.BlockSpec((d, tm // d, wb), lambda i, j: (0, i, j)) for d in DILATIONS[1:]],
        out_shape=[SDS((s, 3 * wb), BF16)] + [SDS((d, s // d, 3 * wb), BF16) for d in DILATIONS[1:]],
        name=name, compiler_params=_cp("parallel", "parallel"))(proj, *tabs, *[perms[d][0] for d in DILATIONS[1:]])
    return [o.reshape(s, 3 * wb) for o in outs]


def _sum3_rope(ds_, perms, tabs, name, slab=None):
    s, w = ds_[0].shape
    tm = PERM_TILE
    half = ROPE_DIM // 2
    nd = len(DILATIONS) - 1

    def body(*refs):
        x_refs, perm_refs = refs[:nd + 1], refs[nd + 1:2 * nd + 1]
        tab_refs = refs[2 * nd + 1:2 * nd + 4]
        tot = x_refs[0][...].astype(F32)
        for d, x_ref, perm_ref in zip(DILATIONS[1:], x_refs[1:], perm_refs):
            tot = tot + _load_pattern(x_ref, perm_ref, d)
        for j in range(w // 128):
            x = tot[:, j * 128:(j + 1) * 128]
            if tabs is not None:
                x = x * tab_refs[0][...] + pltpu.roll(x, half, 1) * tab_refs[1][...] + pltpu.roll(x, 128 - half, 1) * tab_refs[2][...]
            refs[-1][:, j * 128:(j + 1) * 128] = x.astype(refs[-1].dtype)

    blk = pl.BlockSpec((tm, w), lambda i: (i, 0))
    tab = pl.BlockSpec((tm, 128), lambda i: (i, 0))
    pm = pl.BlockSpec((tm, tm), lambda i: (0, 0))
    extra = [] if tabs is None else list(tabs)
    out_shape, col0, more, more_specs, alias = _slab(s, w, BF16, slab, 2 * nd + 1 + len(extra))
    cb = col0 // w
    args = [ds_[0]] + [x.reshape(d, s // d, w) for d, x in zip(DILATIONS[1:], ds_[1:])] + [perms[d][1] for d in DILATIONS[1:]]
    return pl.pallas_call(
        body, grid=(s // tm,),
        in_specs=[blk] + [_pattern_spec(d, w) for d in DILATIONS[1:]] + [pm] * nd + [tab] * len(extra) + more_specs,
        out_specs=pl.BlockSpec((tm, w), lambda i: (i, cb)), out_shape=out_shape, input_output_aliases=alias,
        name=name, compiler_params=_cp("parallel"))(*args, *extra, *more)


def _band_valid(t, nq, nk, qofs, kofs, seq_len):
    qpos = t * 128 + qofs + lax.broadcasted_iota(jnp.int32, (nq, nk), 0)
    kpos = t * 128 + kofs + lax.broadcasted_iota(jnp.int32, (nq, nk), 1)
    sh = int(math.log2(seq_len))
    same = lax.shift_right_arithmetic(qpos, sh) == lax.shift_right_arithmetic(kpos, sh)
    return same & (jnp.abs(kpos - qpos) <= BAND_HALF)


def _window(r0, r1, r2):
    return jnp.concatenate([r0[128 - BAND_HALF:128], r1[...], r2[0:BAND_HALF]], axis=0)


WIN = 128 + 2 * BAND_HALF


def _win_specs(width, col, nt):
    return [pl.BlockSpec((128, width), lambda t: (jnp.maximum(t - 1, 0), col)),
            pl.BlockSpec((128, width), lambda t: (t, col)),
            pl.BlockSpec((128, width), lambda t: (jnp.minimum(t + 1, nt - 1), col))]


def _attn_fwd(qk, v_src, v_col, seq_len, name):
    s = qk.shape[0]
    nt = s // 128
    dm = D_MODEL

    def body(q_ref, k0, k1, k2, v0, v1, v2, o_ref, lse_ref):
        t = pl.program_id(0)
        valid = _band_valid(t, 128, WIN, 0, -BAND_HALF, seq_len)
        q = q_ref[...]
        kc = _window(k0, k1, k2)
        vc = _window(v0, v1, v2)
        outs, lses = [], []
        for h in range(N_HEADS):
            sl = slice(h * HEAD_DIM, (h + 1) * HEAD_DIM)
            sc = lax.dot_general(q[:, sl], kc[:, sl], NT, preferred_element_type=F32) * (HEAD_DIM ** -0.5)
            sc = jnp.where(valid, sc, NEG)
            m = jnp.max(sc, axis=1, keepdims=True)
            e = jnp.exp(sc - m)
            den = jnp.sum(e, axis=1, keepdims=True)
            o = jnp.dot(e.astype(BF16), vc[:, sl], preferred_element_type=F32) / den
            outs.append(o)
            lses.append(m + jnp.log(den))
        o_ref[...] = jnp.concatenate(outs, axis=1).astype(o_ref.dtype)
        lse_ref[...] = jnp.concatenate(lses, axis=1)

    in_specs = [pl.BlockSpec((128, dm), lambda t: (t, 0))] + _win_specs(dm, 1, nt) + _win_specs(dm, v_col, nt)
    return pl.pallas_call(
        body, grid=(nt,), in_specs=in_specs,
        out_specs=[pl.BlockSpec((128, dm), lambda t: (t, 0)), pl.BlockSpec((128, N_HEADS), lambda t: (t, 0))],
        out_shape=[SDS((s, dm), BF16), SDS((s, N_HEADS), F32)], name=name, compiler_params=_cp("parallel"))(
            qk, qk, qk, qk, v_src, v_src, v_src)


def _attn_combine(os_, lses, perms, expand, name):
    s, dm = os_[0].shape
    tm = PERM_TILE
    nd = len(DILATIONS) - 1

    def body(o1, o2, o3, l1, l2, l3, p2, p3, e_ref, out_ref, out_t_ref, lt_ref):
        ls = [l1[...], l2[...], l3[...]]
        m = jnp.maximum(jnp.maximum(ls[0], ls[1]), ls[2])
        es = [jnp.exp(l - m) for l in ls]
        tot = es[0] + es[1] + es[2]
        lt_ref[...] = m + jnp.log(tot)
        ovs = [o1[...].astype(F32), _load_pattern(o2, p2, DILATIONS[1]), _load_pattern(o3, p3, DILATIONS[2])]
        acc = jnp.zeros((tm, dm), F32)
        for e, o in zip(es, ovs):
            acc = acc + jnp.dot(e / tot, e_ref[...], precision=HIGH, preferred_element_type=F32) * o
        out = acc.astype(out_ref.dtype)
        out_ref[...] = out
        out_t_ref[...] = out.T

    row = pl.BlockSpec((tm, dm), lambda i: (i, 0))
    st = pl.BlockSpec((tm, N_HEADS), lambda i: (i, 0))
    pm = pl.BlockSpec((tm, tm), lambda i: (0, 0))
    args = [os_[0]] + [o.reshape(d, s // d, dm) for d, o in zip(DILATIONS[1:], os_[1:])]
    return pl.pallas_call(
        body, grid=(s // tm,),
        in_specs=[row] + [_pattern_spec(d, dm) for d in DILATIONS[1:]] + [st, st, st, pm, pm, pl.BlockSpec((N_HEADS, dm), lambda i: (0, 0))],
        out_specs=[row, pl.BlockSpec((dm, tm), lambda i: (0, i)), st],
        out_shape=[SDS((s, dm + D_INNER), BF16), SDS((dm + D_INNER, s), BF16), SDS((s, N_HEADS), F32)],
        name=name, compiler_params=_cp("parallel"))(*args, *lses, *[perms[d][1] for d in DILATIONS[1:]], expand)


def _attn_delta(dmix, attn, expand_t, perms, name):
    s, dm = attn.shape[0], D_MODEL
    tm = PERM_TILE
    nd = len(DILATIONS) - 1

    def body(d_ref, a_ref, e_ref, *rest):
        perm_refs, o_ref, op_refs = rest[:nd], rest[nd], rest[nd + 1:]
        dv = d_ref[...]
        prod = dv.astype(F32) * a_ref[...].astype(F32)
        o_ref[...] = jnp.dot(prod, e_ref[...], precision=HIGH, preferred_element_type=F32)
        for d, perm_ref, op_ref in zip(DILATIONS[1:], perm_refs, op_refs):
            _store_pattern(op_ref, dv, perm_ref, d)

    row = pl.BlockSpec((tm, dm), lambda i: (i, 0))
    pm = pl.BlockSpec((tm, tm), lambda i: (0, 0))
    outs = pl.pallas_call(
        body, grid=(s // tm,), in_specs=[row, row, pl.BlockSpec((dm, N_HEADS), lambda i: (0, 0))] + [pm] * nd,
        out_specs=[pl.BlockSpec((tm, N_HEADS), lambda i: (i, 0))] + [_pattern_spec(d, dm) for d in DILATIONS[1:]],
        out_shape=[SDS((s, N_HEADS), F32)] + [SDS((d, s // d, dm), BF16) for d in DILATIONS[1:]],
        name=name, compiler_params=_cp("parallel"))(dmix, attn, expand_t, *[perms[d][0] for d in DILATIONS[1:]])
    return outs[0], [o.reshape(s, dm) for o in outs[1:]]


def _attn_bwd_dq(qk, v_src, v_col, do_src, lse, delta, seq_len, name):
    s = qk.shape[0]
    nt = s // 128
    dm = D_MODEL

    def body(q_ref, k0, k1, k2, v0, v1, v2, do_ref, lse_ref, dl_ref, dq_ref):
        t = pl.program_id(0)
        valid = _band_valid(t, 128, WIN, 0, -BAND_HALF, seq_len)
        q = q_ref[...]
        do = do_ref[...]
        kc = _window(k0, k1, k2)
        vc = _window(v0, v1, v2)
        lse_v, dl_v = lse_ref[...], dl_ref[...]
        outs = []
        for h in range(N_HEADS):
            sl = slice(h * HEAD_DIM, (h + 1) * HEAD_DIM)
            sc = lax.dot_general(q[:, sl], kc[:, sl], NT, preferred_element_type=F32) * (HEAD_DIM ** -0.5)
            p = jnp.exp(jnp.where(valid, sc - lse_v[:, h:h + 1], NEG))
            dp = lax.dot_general(do[:, sl], vc[:, sl], NT, preferred_element_type=F32)
            ds = p * (dp - dl_v[:, h:h + 1])
            outs.append(jnp.dot(ds.astype(BF16), kc[:, sl], preferred_element_type=F32) * (HEAD_DIM ** -0.5))
        dq_ref[...] = jnp.concatenate(outs, axis=1).astype(dq_ref.dtype)

    row = pl.BlockSpec((128, dm), lambda t: (t, 0))
    st = pl.BlockSpec((128, N_HEADS), lambda t: (t, 0))
    in_specs = [row] + _win_specs(dm, 1, nt) + _win_specs(dm, v_col, nt) + [row, st, st]
    return pl.pallas_call(
        body, grid=(nt,), in_specs=in_specs, out_specs=row, out_shape=SDS((s, dm), BF16),
        name=name, compiler_params=_cp("parallel"))(qk, qk, qk, qk, v_src, v_src, v_src, do_src, lse, delta)


def _attn_bwd_dkv(qk, v_src, v_col, do_src, lse_t, delta_t, seq_len, name):
    s = qk.shape[0]
    nt = s // 128
    dm = D_MODEL

    def lane_window(r0, r1, r2):
        return jnp.concatenate([r0[:, 128 - BAND_HALF:128], r1[...], r2[:, 0:BAND_HALF]], axis=1)

    def body(k_ref, v_ref, q0, q1, q2, d0, d1, d2, l0, l1, l2, e0, e1, e2, dk_ref, dv_ref):
        t = pl.program_id(0)
        valid = _band_valid(t, 128, WIN, 0, -BAND_HALF, seq_len)
        k = k_ref[...]
        v = v_ref[...]
        qc = _window(q0, q1, q2)
        dc = _window(d0, d1, d2)
        lse_v = lane_window(l0, l1, l2)
        dl_v = lane_window(e0, e1, e2)
        dks, dvs = [], []
        for h in range(N_HEADS):
            sl = slice(h * HEAD_DIM, (h + 1) * HEAD_DIM)
            sc = lax.dot_general(k[:, sl], qc[:, sl], NT, preferred_element_type=F32) * (HEAD_DIM ** -0.5)
            p = jnp.exp(jnp.where(valid, sc - lse_v[h:h + 1, :], NEG))
            dvs.append(jnp.dot(p.astype(BF16), dc[:, sl], preferred_element_type=F32))
            dp = lax.dot_general(v[:, sl], dc[:, sl], NT, preferred_element_type=F32)
            ds = p * (dp - dl_v[h:h + 1, :])
            dks.append(jnp.dot(ds.astype(BF16), qc[:, sl], preferred_element_type=F32) * (HEAD_DIM ** -0.5))
        dk_ref[...] = jnp.concatenate(dks, axis=1).astype(dk_ref.dtype)
        dv_ref[...] = jnp.concatenate(dvs, axis=1).astype(dv_ref.dtype)

    row = pl.BlockSpec((128, dm), lambda t: (t, 0))
    stat = [pl.BlockSpec((N_HEADS, 128), lambda t: (0, jnp.maximum(t - 1, 0))), pl.BlockSpec((N_HEADS, 128), lambda t: (0, t)),
            pl.BlockSpec((N_HEADS, 128), lambda t: (0, jnp.minimum(t + 1, nt - 1)))]
    in_specs = ([pl.BlockSpec((128, dm), lambda t: (t, 1)), pl.BlockSpec((128, dm), lambda t: (t, v_col))]
                + _win_specs(dm, 0, nt) + _win_specs(dm, 0, nt) + stat + stat)
    return pl.pallas_call(
        body, grid=(nt,), in_specs=in_specs, out_specs=[row, row], out_shape=[SDS((s, dm), BF16), SDS((s, dm), BF16)],
        name=name, compiler_params=_cp("parallel"))(qk, v_src, qk, qk, qk, do_src, do_src, do_src, lse_t, lse_t, lse_t, delta_t, delta_t, delta_t)


CONV_COLS = (1024, 1408, 512, 256)


def _halo_specs(tm, tc, col0, nrow_blocks):
    r = tm // 16
    return [pl.BlockSpec((16, tc), lambda i, j: (jnp.maximum(i * r - 1, 0), col0 + j)),
            pl.BlockSpec((16, tc), lambda i, j: (jnp.minimum((i + 1) * r, nrow_blocks * r - 1), col0 + j))]


def _shifted(x_ref, hp_ref, hn_ref, i, last):
    x = x_ref[...].astype(F32)
    tm = x.shape[0]
    rows = lax.broadcasted_iota(jnp.int32, x.shape, 0)
    prev_row = jnp.where(i > 0, hp_ref[15:16, :].astype(F32), 0.0)
    next_row = jnp.where(i < last, hn_ref[0:1, :].astype(F32), 0.0)
    xp = jnp.where(rows == 0, prev_row, pltpu.roll(x, 1, 0))
    xn = jnp.where(rows == tm - 1, next_row, pltpu.roll(x, tm - 1, 0))
    return xp, x, xn


def _conv(x_src, col0, width, w3, bias, act, name, out_dtype=BF16, slab=None, transpose=False, wcol0=0):
    s = x_src.shape[0]
    tm = _pick(s, (256, 128))
    tc = _pick(width, CONV_COLS)
    nb = s // tm
    c0 = col0 // tc
    wc0 = wcol0 // tc

    def body(*refs):
        x_ref, hp_ref, hn_ref, w_ref, b_ref = refs[:5]
        o_ref = refs[-1]
        i = pl.program_id(0)
        xp, x, xn = _shifted(x_ref, hp_ref, hn_ref, i, nb - 1)
        w = w_ref[...]
        if transpose:
            y = w[2:3, :] * xp + w[1:2, :] * x + w[0:1, :] * xn
        else:
            y = w[0:1, :] * xp + w[1:2, :] * x + w[2:3, :] * xn + b_ref[...]
        if act:
            y = y * _sigmoid(y)
        o_ref[...] = y.astype(o_ref.dtype)

    in_specs = ([pl.BlockSpec((tm, tc), lambda i, j: (i, c0 + j))] + _halo_specs(tm, tc, c0, nb)
                + [pl.BlockSpec((3, tc), lambda i, j: (0, wc0 + j)), pl.BlockSpec((1, tc), lambda i, j: (0, wc0 + j))])
    out_shape, ocol, more, more_specs, alias = _slab(s, width, out_dtype, slab, 5)
    ob = ocol // tc
    return pl.pallas_call(
        body, grid=(nb, width // tc), in_specs=in_specs + more_specs, out_specs=pl.BlockSpec((tm, tc), lambda i, j: (i, ob + j)),
        out_shape=out_shape, input_output_aliases=alias, name=name, compiler_params=_cp("parallel", "parallel"))(
            x_src, x_src, x_src, w3, bias, *more)


def _conv_silu_bwd(x_src, col0, width, w3, bias, addends, add_widths, name):
    s = x_src.shape[0]
    tm = _pick(s, (256, 128))
    tc = _pick(width, CONV_COLS)
    nb = s // tm
    c0 = col0 // tc
    na = len(addends)

    def body(*refs):
        x_ref, hp_ref, hn_ref, w_ref, b_ref = refs[:5]
        a_refs = refs[5:5 + na]
        dp_ref, dw_ref, db_ref = refs[5 + na:]
        i, j = pl.program_id(1), pl.program_id(0)
        xp, x, xn = _shifted(x_ref, hp_ref, hn_ref, i, nb - 1)
        w = w_ref[...]
        pre = w[0:1, :] * xp + w[1:2, :] * x + w[2:3, :] * xn + b_ref[...]
        g = jnp.zeros_like(pre)
        for a_ref, aw in zip(a_refs, add_widths):
            av = a_ref[...].astype(F32)
            g = g + (av if aw == width else jnp.where(j < aw // tc, av, 0.0))
        sg = _sigmoid(pre)
        dpre = g * (sg * (1.0 + pre * (1.0 - sg)))
        dp_ref[...] = dpre.astype(dp_ref.dtype)

        @pl.when(i == 0)
        def _():
            dw_ref[...] = jnp.zeros_like(dw_ref)
            db_ref[...] = jnp.zeros_like(db_ref)

        dw_ref[...] += jnp.concatenate([jnp.sum(dpre * xp, axis=0, keepdims=True), jnp.sum(dpre * x, axis=0, keepdims=True),
                                        jnp.sum(dpre * xn, axis=0, keepdims=True)], axis=0)
        db_ref[...] += jnp.sum(dpre, axis=0, keepdims=True)

    r = tm // 16
    in_specs = [pl.BlockSpec((tm, tc), lambda j, i: (i, c0 + j)),
                pl.BlockSpec((16, tc), lambda j, i: (jnp.maximum(i * r - 1, 0), c0 + j)),
                pl.BlockSpec((16, tc), lambda j, i: (jnp.minimum((i + 1) * r, nb * r - 1), c0 + j)),
                pl.BlockSpec((3, tc), lambda j, i: (0, j)), pl.BlockSpec((1, tc), lambda j, i: (0, j))]
    for aw in add_widths:
        nblk = aw // tc
        in_specs.append(pl.BlockSpec((tm, tc), lambda j, i, nblk=nblk: (i, jnp.minimum(j, nblk - 1))))
    return pl.pallas_call(
        body, grid=(width // tc, nb), in_specs=in_specs,
        out_specs=[pl.BlockSpec((tm, tc), lambda j, i: (i, j)), pl.BlockSpec((3, tc), lambda j, i: (0, j)), pl.BlockSpec((1, tc), lambda j, i: (0, j))],
        out_shape=[SDS((s, width), BF16), SDS((3, width), F32), SDS((1, width), F32)],
        name=name, compiler_params=_cp("parallel", "arbitrary"))(x_src, x_src, x_src, w3, bias, *addends)


def _ffn_gate_fwd(u, w3, bias, name):
    s = u.shape[0]
    tm = _pick(s, (256, 128))
    tc = _pick(D_FF, CONV_COLS)
    nb = s // tm
    nj = D_FF // tc

    def body(g_ref, gp, gn, u_ref, up, un, wg_ref, wu_ref, bg_ref, bu_ref, o_ref, ot_ref):
        i = pl.program_id(0)
        outs = []
        for (x_ref, hp, hn, w_ref, b_ref) in ((g_ref, gp, gn, wg_ref, bg_ref), (u_ref, up, un, wu_ref, bu_ref)):
            xp, x, xn = _shifted(x_ref, hp, hn, i, nb - 1)
            w = w_ref[...]
            outs.append(w[0:1, :] * xp + w[1:2, :] * x + w[2:3, :] * xn + b_ref[...])
        gate, upv = outs
        out = (gate * _sigmoid(gate) * upv).astype(o_ref.dtype)
        o_ref[...] = out
        ot_ref[...] = out.T

    def xspecs(c0):
        return [pl.BlockSpec((tm, tc), lambda i, j: (i, c0 + j))] + _halo_specs(tm, tc, c0, nb)

    in_specs = (xspecs(0) + xspecs(nj)
                + [pl.BlockSpec((3, tc), lambda i, j: (0, j)), pl.BlockSpec((3, tc), lambda i, j: (0, nj + j)),
                   pl.BlockSpec((1, tc), lambda i, j: (0, j)), pl.BlockSpec((1, tc), lambda i, j: (0, nj + j))])
    return pl.pallas_call(
        body, grid=(nb, nj), in_specs=in_specs,
        out_specs=[pl.BlockSpec((tm, tc), lambda i, j: (i, j)), pl.BlockSpec((tc, tm), lambda i, j: (j, i))],
        out_shape=[SDS((s, D_FF), BF16), SDS((D_FF, s), BF16)], name=name, compiler_params=_cp("parallel", "parallel"))(
            u, u, u, u, u, u, w3, w3, bias, bias)


def _ffn_gate_bwd(u, w3, bias, dact, name):
    s = u.shape[0]
    tm = _pick(s, (256, 128))
    tc = _pick(D_FF, CONV_COLS)
    nb = s // tm
    nj = D_FF // tc

    def body(g_ref, gp, gn, u_ref, up, un, wg_ref, wu_ref, bg_ref, bu_ref, da_ref, dg_ref, du_ref, dwg_ref, dwu_ref, dbg_ref, dbu_ref):
        i = pl.program_id(1)
        sh, pre = [], []
        for (x_ref, hp, hn, w_ref, b_ref) in ((g_ref, gp, gn, wg_ref, bg_ref), (u_ref, up, un, wu_ref, bu_ref)):
            xs3 = _shifted(x_ref, hp, hn, i, nb - 1)
            w = w_ref[...]
            sh.append(xs3)
            pre.append(w[0:1, :] * xs3[0] + w[1:2, :] * xs3[1] + w[2:3, :] * xs3[2] + b_ref[...])
        gate, upv = pre
        da = da_ref[...].astype(F32)
        sg = _sigmoid(gate)
        dgate = da * upv * (sg * (1.0 + gate * (1.0 - sg)))
        dup = da * gate * sg
        dg_ref[...] = dgate.astype(dg_ref.dtype)
        du_ref[...] = dup.astype(du_ref.dtype)

        @pl.when(i == 0)
        def _():
            for r in (dwg_ref, dwu_ref, dbg_ref, dbu_ref):
                r[...] = jnp.zeros_like(r)

        for d, xs3, dw_ref, db_ref in ((dgate, sh[0], dwg_ref, dbg_ref), (dup, sh[1], dwu_ref, dbu_ref)):
            dw_ref[...] += jnp.concatenate([jnp.sum(d * xs3[0], axis=0, keepdims=True), jnp.sum(d * xs3[1], axis=0, keepdims=True),
                                            jnp.sum(d * xs3[2], axis=0, keepdims=True)], axis=0)
            db_ref[...] += jnp.sum(d, axis=0, keepdims=True)

    r = tm // 16

    def xspecs(c0):
        return [pl.BlockSpec((tm, tc), lambda j, i: (i, c0 + j)),
                pl.BlockSpec((16, tc), lambda j, i: (jnp.maximum(i * r - 1, 0), c0 + j)),
                pl.BlockSpec((16, tc), lambda j, i: (jnp.minimum((i + 1) * r, nb * r - 1), c0 + j))]

    in_specs = (xspecs(0) + xspecs(nj)
                + [pl.BlockSpec((3, tc), lambda j, i: (0, j)), pl.BlockSpec((3, tc), lambda j, i: (0, nj + j)),
                   pl.BlockSpec((1, tc), lambda j, i: (0, j)), pl.BlockSpec((1, tc), lambda j, i: (0, nj + j)),
                   pl.BlockSpec((tm, tc), lambda j, i: (i, j))])
    blk = pl.BlockSpec((tm, tc), lambda j, i: (i, j))
    w_o = pl.BlockSpec((3, tc), lambda j, i: (0, j))
    b_o = pl.BlockSpec((1, tc), lambda j, i: (0, j))
    return pl.pallas_call(
        body, grid=(nj, nb), in_specs=in_specs, out_specs=[blk, blk, w_o, w_o, b_o, b_o],
        out_shape=[SDS((s, D_FF), BF16), SDS((s, D_FF), BF16), SDS((3, D_FF), F32), SDS((3, D_FF), F32), SDS((1, D_FF), F32), SDS((1, D_FF), F32)],
        name=name, compiler_params=_cp("parallel", "arbitrary"))(u, u, u, u, u, u, w3, w3, bias, bias, dact)


def _exchange_sems(n):
    return [pltpu.SemaphoreType.DMA((7 * n,)), pltpu.SemaphoreType.DMA((7 * n,)), pltpu.SemaphoreType.DMA((n,))]


def _exchange_copies(srcs, outs, send_sems, recv_sems, local_sems):
    x, y, c = lax.axis_index("x"), lax.axis_index("y"), lax.axis_index("c")
    me = 4 * x + 2 * y + c
    locals_ = [pltpu.make_async_copy(srcs[a].at[me], outs[a].at[me], local_sems.at[a]) for a in range(len(srcs))]
    sends, recvs = [], []
    for k in range(1, N_DEV):
        px, py, pc = x ^ ((k >> 2) & 1), y ^ ((k >> 1) & 1), c ^ (k & 1)
        peer = 4 * px + 2 * py + pc
        for a in range(len(srcs)):
            sems = dict(send_sem=send_sems.at[a * 7 + k - 1], recv_sem=recv_sems.at[a * 7 + k - 1], device_id_type=MESH)
            sends.append(pltpu.make_async_remote_copy(src_ref=srcs[a].at[peer], dst_ref=outs[a].at[me], device_id=(px, py, pc), **sems))
            recvs.append(pltpu.make_async_remote_copy(src_ref=srcs[a].at[peer], dst_ref=outs[a].at[peer], device_id=(x, y, c), **sems))
    return locals_, sends, recvs


def _exchange_start(srcs, outs, send_sems, recv_sems, local_sems):
    locals_, sends, _ = _exchange_copies(srcs, outs, send_sems, recv_sems, local_sems)
    for cp in locals_ + sends:
        cp.start()


def _exchange_wait(srcs, outs, send_sems, recv_sems, local_sems):
    locals_, sends, recvs = _exchange_copies(srcs, outs, send_sems, recv_sems, local_sems)
    for cp in recvs:
        cp.wait_recv()
    for cp in sends:
        cp.wait_send()
    for cp in locals_:
        cp.wait()


def _gather_copies(srcs, outs, send_sems, recv_sems, local_sems):
    x, y, c = lax.axis_index("x"), lax.axis_index("y"), lax.axis_index("c")
    me, sibling = (x, y, c), (x, y, 1 - c)
    chips = [(1 - x, y), (x, 1 - y), (1 - x, 1 - y)]

    def copy(a, k, block, to, src=None):
        dst = outs[a].at[4 * block[0] + 2 * block[1] + block[2]]
        return pltpu.make_async_remote_copy(
            src_ref=dst if src is None else src, dst_ref=dst,
            send_sem=send_sems.at[a * 7 + k], recv_sem=recv_sems.at[a * 7 + k], device_id=to, device_id_type=MESH)

    n = len(srcs)
    locals_ = [pltpu.make_async_copy(srcs[a], outs[a].at[4 * x + 2 * y + c], local_sems.at[a]) for a in range(n)]
    own = [copy(a, 0, me, sibling, src=srcs[a]) for a in range(n)]
    own += [copy(a, 1 + j, me, (*chip, c), src=srcs[a]) for a in range(n) for j, chip in enumerate(chips)]
    landed_ici = [copy(a, 1 + j, (*chip, c), me) for j, chip in enumerate(chips) for a in range(n)]
    passed = [copy(a, 4 + j, (*chip, c), sibling) for j, chip in enumerate(chips) for a in range(n)]
    landed_d2d = [copy(a, 0, sibling, me) for a in range(n)]
    landed_d2d += [copy(a, 4 + j, (*chip, 1 - c), me) for a in range(n) for j, chip in enumerate(chips)]
    return locals_, own, landed_ici, passed, landed_d2d


def _gather_start(*refs):
    locals_, own, _, _, _ = _gather_copies(*refs)
    for cp in locals_ + own:
        cp.start()


def _gather_forward(*refs):
    _, _, landed_ici, passed, _ = _gather_copies(*refs)
    for arrived, onward in zip(landed_ici, passed):
        arrived.wait_recv()
        onward.start()


def _gather_finish(*refs):
    locals_, own, _, passed, landed_d2d = _gather_copies(*refs)
    for cp in landed_d2d:
        cp.wait_recv()
    for cp in own + passed:
        cp.wait_send()
    for cp in locals_:
        cp.wait()


def _ssd_common(dt_ref, dtt_ref, al_r, al_c, bi_r, bi_c, off, rev):
    li = lax.broadcasted_iota(jnp.int32, (CHUNK, CHUNK), 0)
    si = lax.broadcasted_iota(jnp.int32, (CHUNK, CHUNK), 1)
    mask = (li <= si) if rev else (li >= si)
    mask_t = (li >= si) if rev else (li <= si)
    a_r = -jnp.exp(al_r[...])
    a_c = -jnp.exp(al_c[...])
    pre = dt_ref[:, off:off + N_HEADS] + bi_r[...]
    dt = _softplus(pre)
    cs = jnp.dot(mask.astype(F32), dt * a_r, precision=HIGH, preferred_element_type=F32)
    dt_t = _softplus(dtt_ref[off:off + N_HEADS, :] + bi_c[...])
    cs_t = jnp.dot(dt_t * a_c, mask_t.astype(F32), precision=HIGH, preferred_element_type=F32)
    tot = cs[0:1, :] if rev else cs[CHUNK - 1:CHUNK, :]
    return mask, mask_t, a_r, pre, dt, cs, cs_t, tot


def _ssd_fwd(xbc, dt_raw, dt_t, args_f, args_b, name, gather=()):
    s = xbc.shape[0]
    nc = s // CHUNK
    hp = D_INNER // N_HEADS
    hpg = N_HEADS // N_GROUPS
    gs = N_GROUPS * D_STATE
    ng = len(gather)
    fwd_step = (nc * 11) // 16

    def chunk(x_ref, b_ref, c_ref, dt_ref, dtt_ref, alr, alc, bir, bic, y_ref, st_ref, h_scr, off, rev):
        mask, _, _, _, dt, cs, cs_t, tot = _ssd_common(dt_ref, dtt_ref, alr, alc, bir, bic, off, rev)
        xs = x_ref[...]
        ys = []
        for g in range(N_GROUPS):
            bg = b_ref[:, g * D_STATE:(g + 1) * D_STATE]
            cg = c_ref[:, g * D_STATE:(g + 1) * D_STATE]
            gm = lax.dot_general(cg, bg, NT, preferred_element_type=F32)
            hcat = h_scr[g]
            st_ref[0, g] = hcat
            ch = lax.dot_general(cg, hcat.astype(BF16), NT, preferred_element_type=F32)
            xdd = []
            for r in range(hpg):
                h = g * hpg + r
                csc, csr, tot_h = cs[:, h:h + 1], cs_t[h:h + 1, :], tot[:, h:h + 1]
                lm = jnp.exp(jnp.where(mask, csc - csr, NEG))
                xdf = xs[:, h * hp:(h + 1) * hp].astype(F32) * dt[:, h:h + 1]
                y = jnp.dot((gm * lm).astype(BF16), xdf.astype(BF16), preferred_element_type=F32)
                ys.append(y + jnp.exp(csc) * ch[:, r * hp:(r + 1) * hp])
                xdd.append((xdf * jnp.exp(tot_h - csc)).astype(BF16))
            snew = lax.dot_general(jnp.concatenate(xdd, axis=1), bg, TN, preferred_element_type=F32)
            for r in range(hpg):
                rs = slice(r * hp, (r + 1) * hp)
                h_scr[g, rs, :] = jnp.exp(tot[:, g * hpg + r:g * hpg + r + 1]) * hcat[rs] + snew[rs]
        y_ref[...] = jnp.concatenate(ys, axis=1)

    def body(*refs):
        in_f, in_b = refs[0:9], refs[9:18]
        g_src = refs[18:18 + ng]
        out_f, out_b = refs[18 + ng:20 + ng], refs[20 + ng:22 + ng]
        g_dst = refs[22 + ng:22 + 2 * ng]
        hs_f, hs_b = refs[22 + 2 * ng], refs[23 + 2 * ng]
        g_sems = refs[24 + 2 * ng:]
        step = pl.program_id(0)

        @pl.when(step == 0)
        def _():
            hs_f[...] = jnp.zeros_like(hs_f)
            hs_b[...] = jnp.zeros_like(hs_b)
            if ng:
                _gather_start(g_src, g_dst, *g_sems)

        chunk(*in_f, *out_f, hs_f, 0, False)
        chunk(*in_b, *out_b, hs_b, N_HEADS, True)

        if ng:
            @pl.when(step == fwd_step)
            def _():
                _gather_forward(g_src, g_dst, *g_sems)

            @pl.when(step == nc - 1)
            def _():
                _gather_finish(g_src, g_dst, *g_sems)

    small = lambda shape: pl.BlockSpec(shape, lambda c: (0, 0))

    def specs(cm):
        ins = [pl.BlockSpec((CHUNK, D_INNER), lambda c: (cm(c), 0)),
               pl.BlockSpec((CHUNK, gs), lambda c: (cm(c), D_INNER // gs)),
               pl.BlockSpec((CHUNK, gs), lambda c: (cm(c), D_INNER // gs + 1)),
               pl.BlockSpec((CHUNK, 128), lambda c: (cm(c), 0)),
               pl.BlockSpec((2 * N_HEADS, CHUNK), lambda c: (0, cm(c))),
               small((1, N_HEADS)), small((N_HEADS, 1)), small((1, N_HEADS)), small((N_HEADS, 1))]
        outs = [pl.BlockSpec((CHUNK, D_INNER), lambda c: (cm(c), 0)),
                pl.BlockSpec((1, N_GROUPS, hpg * hp, D_STATE), lambda c: (cm(c), 0, 0, 0))]
        return ins, outs

    ins_f, outs_f = specs(lambda c: c)
    ins_b, outs_b = specs(lambda c: nc - 1 - c)
    any_spec = pl.BlockSpec(memory_space=pl.ANY)
    one_dir = [SDS((s, D_INNER), F32), SDS((nc, N_GROUPS, hpg * hp, D_STATE), F32)]
    state = pltpu.VMEM((N_GROUPS, hpg * hp, D_STATE), F32)
    return pl.pallas_call(
        body, grid=(nc,), in_specs=ins_f + ins_b + [any_spec] * ng, out_specs=outs_f + outs_b + [any_spec] * ng,
        out_shape=one_dir + one_dir + [SDS((N_DEV,) + g.shape, g.dtype) for g in gather],
        scratch_shapes=[state, state] + (_exchange_sems(ng) if ng else []), name=name, compiler_params=_cp("arbitrary"))(
            xbc, xbc, xbc, dt_raw, dt_t, *args_f, xbc, xbc, xbc, dt_raw, dt_t, *args_b, *gather)


def _ssd_bwd(xbc, dt_raw, dt_t, al_r, al_c, bi_r, bi_c, states, dy, off, rev, name, exchange=()):
    s = xbc.shape[0]
    nc = s // CHUNK
    hp = D_INNER // N_HEADS
    gs = N_GROUPS * D_STATE
    hpg = N_HEADS // N_GROUPS
    cm = (lambda c: c) if rev else (lambda c: nc - 1 - c)
    nx = len(exchange)

    def body(*refs):
        x_ref, b_ref, c_ref, dt_ref, dtt_ref, alr, alc, bir, bic, st_ref, dy_ref = refs[:11]
        xch_src = refs[11:11 + nx]
        dx_ref, ddt_ref, dal_ref, dbi_ref = refs[11 + nx:15 + nx]
        xch_dst = refs[15 + nx:15 + 2 * nx]
        dh_scr = refs[15 + 2 * nx]
        xch_sems = refs[16 + 2 * nx:]

        @pl.when(pl.program_id(0) == 0)
        def _():
            dh_scr[...] = jnp.zeros_like(dh_scr)
            dal_ref[...] = jnp.zeros_like(dal_ref)
            dbi_ref[...] = jnp.zeros_like(dbi_ref)
            if nx:
                _exchange_start(xch_src, xch_dst, *xch_sems)

        if nx:
            @pl.when(pl.program_id(0) == nc - 1)
            def _():
                _exchange_wait(xch_src, xch_dst, *xch_sems)

        mask, mask_t, a_r, pre, dt, cs, cs_t, tot = _ssd_common(dt_ref, dtt_ref, alr, alc, bir, bic, off, rev)
        xs = x_ref[...]
        dyv = dy_ref[...]
        rows = lax.broadcasted_iota(jnp.int32, (CHUNK, 1), 0)
        end_row = (rows == 0) if rev else (rows == CHUNK - 1)
        lane_h = lax.broadcasted_iota(jnp.int32, (1, N_HEADS), 1)
        sub_h = lax.broadcasted_iota(jnp.int32, (N_HEADS, 1), 0)
        dcs_all = jnp.zeros((CHUNK, N_HEADS), F32)
        colw_all = jnp.zeros((N_HEADS, CHUNK), F32)
        dxsum_all = jnp.zeros((CHUNK, N_HEADS), F32)
        dxs, dbs, dcs_out = [], [], []
        for g in range(N_GROUPS):
            bg = b_ref[:, g * D_STATE:(g + 1) * D_STATE]
            cg = c_ref[:, g * D_STATE:(g + 1) * D_STATE]
            gm = lax.dot_general(cg, bg, NT, preferred_element_type=F32)
            hcat = st_ref[0, g]
            dhcat = dh_scr[g]
            hb, dhb = hcat.astype(BF16), dhcat.astype(BF16)
            ch = lax.dot_general(cg, hb, NT, preferred_element_type=F32)
            z = lax.dot_general(bg, dhb, NT, preferred_element_type=F32)
            dg_sum = jnp.zeros((CHUNK, CHUNK), F32)
            dchs, xdds, t_hs = [], [], []
            for r in range(hpg):
                h = g * hpg + r
                rs = slice(r * hp, (r + 1) * hp)
                csc, csr, tot_h = cs[:, h:h + 1], cs_t[h:h + 1, :], tot[:, h:h + 1]
                lm = jnp.exp(jnp.where(mask, csc - csr, NEG))
                xh = xs[:, h * hp:(h + 1) * hp].astype(F32)
                dtc = dt[:, h:h + 1]
                xdf = xh * dtc
                xd = xdf.astype(BF16)
                dyh = dyv[:, h * hp:(h + 1) * hp]
                dyb = dyh.astype(BF16)
                m = gm * lm
                ecs = jnp.exp(csc)
                dec = jnp.exp(tot_h - csc)
                t_h = jnp.exp(tot_h)
                dxd = lax.dot_general(m.astype(BF16), dyb, TN, preferred_element_type=F32)
                dm = lax.dot_general(dyb, xd, NT, preferred_element_type=F32)
                dg_sum = dg_sum + dm * lm
                w = dm * m
                dcs = jnp.sum(dyh * (ecs * ch[:, rs]), axis=1, keepdims=True)
                dchs.append((ecs * dyh).astype(BF16))
                dxd = dxd + dec * z[:, rs]
                xdd = xdf * dec
                ddec = jnp.sum(xdd * z[:, rs], axis=1, keepdims=True)
                xdds.append(xdd.astype(BF16))
                dtot = (jnp.sum(jnp.sum(dhcat[rs] * hcat[rs], axis=1, keepdims=True), axis=0, keepdims=True) * t_h
                        + jnp.sum(ddec, axis=0, keepdims=True))
                t_hs.append(t_h)
                dcs = dcs + jnp.sum(w, axis=1, keepdims=True) - ddec
                dcs = dcs + jnp.where(end_row, dtot, 0.0)
                colw_all = colw_all + (sub_h == h).astype(F32) * jnp.sum(w, axis=0, keepdims=True)
                onehot = (lane_h == h).astype(F32)
                dcs_all = dcs_all + dcs * onehot
                dxsum_all = dxsum_all + jnp.sum(dxd * xh, axis=1, keepdims=True) * onehot
                dxs.append(dxd * dtc)
            dgs = dg_sum.astype(BF16)
            dchc = jnp.concatenate(dchs, axis=1)
            dcs_out.append(jnp.dot(dgs, bg, preferred_element_type=F32) + jnp.dot(dchc, hb, preferred_element_type=F32))
            dbs.append(lax.dot_general(dgs, cg, TN, preferred_element_type=F32)
                       + jnp.dot(jnp.concatenate(xdds, axis=1), dhb, preferred_element_type=F32))
            dh_in = lax.dot_general(dchc, cg, TN, preferred_element_type=F32)
            for r in range(hpg):
                rs = slice(r * hp, (r + 1) * hp)
                dh_scr[g, rs, :] = dh_in[rs] + t_hs[r] * dhcat[rs]
        dx_ref[...] = jnp.concatenate(dxs + dbs + dcs_out, axis=1)
        mt = mask_t.astype(F32)
        da = (jnp.dot(mt, dcs_all, precision=HIGH, preferred_element_type=F32)
              - lax.dot_general(mt, colw_all, NT, precision=HIGH, preferred_element_type=F32))
        dal_ref[...] += jnp.sum(da * dt, axis=0, keepdims=True) * a_r
        ddt_raw = (da * a_r + dxsum_all) * _sigmoid(pre)
        ddt_ref[...] = ddt_raw
        dbi_ref[...] += jnp.sum(ddt_raw, axis=0, keepdims=True)

    small = lambda shape: pl.BlockSpec(shape, lambda c: (0, 0))
    in_specs = [pl.BlockSpec((CHUNK, D_INNER), lambda c: (cm(c), 0)),
                pl.BlockSpec((CHUNK, gs), lambda c: (cm(c), D_INNER // gs)),
                pl.BlockSpec((CHUNK, gs), lambda c: (cm(c), D_INNER // gs + 1)),
                pl.BlockSpec((CHUNK, 128), lambda c: (cm(c), 0)),
                pl.BlockSpec((2 * N_HEADS, CHUNK), lambda c: (0, cm(c))),
                small((1, N_HEADS)), small((N_HEADS, 1)), small((1, N_HEADS)), small((N_HEADS, 1)),
                pl.BlockSpec((1, N_GROUPS, hpg * hp, D_STATE), lambda c: (cm(c), 0, 0, 0)),
                pl.BlockSpec((CHUNK, D_INNER), lambda c: (cm(c), 0))]
    any_spec = pl.BlockSpec(memory_space=pl.ANY)
    return pl.pallas_call(
        body, grid=(nc,), in_specs=in_specs + [any_spec] * nx,
        out_specs=[pl.BlockSpec((CHUNK, XBC), lambda c: (cm(c), 0)), pl.BlockSpec((CHUNK, N_HEADS), lambda c: (cm(c), 0)),
                   small((1, N_HEADS)), small((1, N_HEADS))] + [any_spec] * nx,
        out_shape=[SDS((s, XBC), F32), SDS((s, N_HEADS), F32), SDS((1, N_HEADS), F32), SDS((1, N_HEADS), F32)]
        + [SDS(a.shape, a.dtype) for a in exchange],
        scratch_shapes=[pltpu.VMEM((N_GROUPS, hpg * hp, D_STATE), F32)] + (_exchange_sems(nx) if nx else []),
        name=name, compiler_params=_cp("arbitrary"))(xbc, xbc, xbc, dt_raw, dt_t, al_r, al_c, bi_r, bi_c, states, dy, *exchange)


def _gate_fwd(yf, yb, xbc, proj, dskip_x, norm_w, mix, mix_t, name):
    s = yf.shape[0]
    tm = _pick(s, (256, 128))
    gw = D_INNER // N_GROUPS
    zc = 3 * D_MODEL // D_INNER

    def body(yf_ref, yb_ref, x_ref, z_ref, d_ref, w_ref, _m, _mt, o_ref, ot_ref):
        y = yf_ref[...] + yb_ref[...] + d_ref[...] * x_ref[...].astype(F32)
        z = z_ref[...].astype(F32)
        gt = y * (z * _sigmoid(z))
        outs = []
        for g in range(N_GROUPS):
            gg = gt[:, g * gw:(g + 1) * gw]
            outs.append(gg * lax.rsqrt(jnp.mean(gg * gg, axis=-1, keepdims=True) + EPS))
        out = (jnp.concatenate(outs, axis=1) * w_ref[...]).astype(o_ref.dtype)
        o_ref[...] = out
        ot_ref[...] = out.T

    row = pl.BlockSpec((tm, D_INNER), lambda i: (i, 0))
    vec = pl.BlockSpec((1, D_INNER), lambda i: (0, 0))
    any_spec = pl.BlockSpec(memory_space=pl.ANY)
    cb = (mix.shape[1] - D_INNER) // D_INNER
    return pl.pallas_call(
        body, grid=(s // tm,), in_specs=[row, row, row, pl.BlockSpec((tm, D_INNER), lambda i: (i, zc)), vec, vec, any_spec, any_spec],
        out_specs=[pl.BlockSpec((tm, D_INNER), lambda i: (i, cb)), pl.BlockSpec((D_INNER, tm), lambda i: (cb, i))],
        out_shape=[SDS(mix.shape, BF16), SDS(mix_t.shape, BF16)], input_output_aliases={6: 0, 7: 1},
        name=name, compiler_params=_cp("parallel"))(yf, yb, xbc, proj, dskip_x, norm_w, mix, mix_t)


def _gate_bwd(yf, yb, xbc, proj, dskip_x, norm_w, dmix, dproj, name):
    s = yf.shape[0]
    tm = _pick(s, (256, 128))
    gw = D_INNER // N_GROUPS
    zc = 3 * D_MODEL // D_INNER

    def body(yf_ref, yb_ref, x_ref, z_ref, d_ref, w_ref, do_ref, _, dy_ref, dz_ref, dxs_ref, dw_ref, dd_ref):
        xf = x_ref[...].astype(F32)
        y = yf_ref[...] + yb_ref[...] + d_ref[...] * xf
        z = z_ref[...].astype(F32)
        sg = _sigmoid(z)
        sz = z * sg
        gt = y * sz
        do = do_ref[...].astype(F32)
        dgh = do * w_ref[...]
        ghs, dgts = [], []
        for g in range(N_GROUPS):
            gg = gt[:, g * gw:(g + 1) * gw]
            r = lax.rsqrt(jnp.mean(gg * gg, axis=-1, keepdims=True) + EPS)
            gh = gg * r
            dg = dgh[:, g * gw:(g + 1) * gw]
            ghs.append(gh)
            dgts.append(r * (dg - gh * jnp.mean(dg * gh, axis=-1, keepdims=True)))
        ghat = jnp.concatenate(ghs, axis=1)
        dgt = jnp.concatenate(dgts, axis=1)
        dy = dgt * sz
        dy_ref[...] = dy
        dz_ref[...] = (dgt * y * (sg * (1.0 + z * (1.0 - sg)))).astype(dz_ref.dtype)
        dxs_ref[...] = dy * d_ref[...]

        @pl.when(pl.program_id(0) == 0)
        def _():
            dw_ref[...] = jnp.zeros_like(dw_ref)
            dd_ref[...] = jnp.zeros_like(dd_ref)

        dw_ref[...] += jnp.sum(do * ghat, axis=0, keepdims=True)
        dd_ref[...] += jnp.sum(dy * xf, axis=0, keepdims=True)

    row = pl.BlockSpec((tm, D_INNER), lambda i: (i, 0))
    vec = pl.BlockSpec((1, D_INNER), lambda i: (0, 0))
    dz_shape, _, more, more_specs, alias = _slab(s, D_INNER, BF16, (dproj.shape[1], zc * D_INNER, dproj), 7, out_idx=1)
    return pl.pallas_call(
        body, grid=(s // tm,),
        in_specs=[row, row, row, pl.BlockSpec((tm, D_INNER), lambda i: (i, zc)), vec, vec, pl.BlockSpec((tm, D_INNER), lambda i: (i, 1))] + more_specs,
        out_specs=[row, pl.BlockSpec((tm, D_INNER), lambda i: (i, zc)), row, vec, vec],
        out_shape=[SDS((s, D_INNER), F32), dz_shape, SDS((s, D_INNER), F32), SDS((1, D_INNER), F32), SDS((1, D_INNER), F32)],
        input_output_aliases=alias, name=name, compiler_params=_cp("arbitrary"))(yf, yb, xbc, proj, dskip_x, norm_w, dmix, *more)


def _adamw(parts, w, m, v, name):
    r, c = w.shape
    tr = _pick(r, (256, 352, 128))

    def body(p_ref, w_ref, m_ref, v_ref, g_ref, d_ref, nm_ref, nv_ref):
        g = p_ref[0].astype(F32)
        for i in range(1, N_DEV):
            g = g + p_ref[i].astype(F32)
        mn = B1 * m_ref[...] + (1.0 - B1) * g
        vn = B2 * v_ref[...] + (1.0 - B2) * (g * g)
        m_hat = mn / (1.0 - B1 ** STEP)
        v_hat = vn / (1.0 - B2 ** STEP)
        g_ref[...] = g
        d_ref[...] = -LR * (m_hat / (jnp.sqrt(v_hat) + AEPS) + WD * w_ref[...])
        nm_ref[...] = mn
        nv_ref[...] = vn

    blk = pl.BlockSpec((tr, c), lambda i: (i, 0))
    return pl.pallas_call(
        body, grid=(r // tr,), in_specs=[pl.BlockSpec((N_DEV, tr, c), lambda i: (0, i, 0)), blk, blk, blk],
        out_specs=[blk, blk, blk, blk], out_shape=[SDS((r, c), F32)] * 4, name=name, compiler_params=_cp("parallel"))(parts, w, m, v)


def _sum_parts(parts, name):
    _, r, c = parts.shape

    def body(p_ref, o_ref):
        g = p_ref[0]
        for i in range(1, N_DEV):
            g = g + p_ref[i]
        o_ref[...] = g

    return pl.pallas_call(body, out_shape=SDS((r, c), F32), name=name)(parts)


def _adamw_small(g, w, m, v, name):
    def body(g_ref, w_ref, m_ref, v_ref, d_ref, nm_ref, nv_ref):
        gv = g_ref[...]
        mn = B1 * m_ref[...] + (1.0 - B1) * gv
        vn = B2 * v_ref[...] + (1.0 - B2) * (gv * gv)
        m_hat = mn / (1.0 - B1 ** STEP)
        v_hat = vn / (1.0 - B2 ** STEP)
        d_ref[...] = -LR * (m_hat / (jnp.sqrt(v_hat) + AEPS) + WD * w_ref[...])
        nm_ref[...] = mn
        nv_ref[...] = vn

    return pl.pallas_call(body, out_shape=[SDS(g.shape, F32)] * 3, name=name)(g, w, m, v)


def _my_index():
    return 4 * lax.axis_index("x") + 2 * lax.axis_index("y") + lax.axis_index("c")


def _all_gather(shards, name):
    n = len(shards)

    def body(*refs):
        srcs, outs = refs[:n], refs[n:2 * n]
        _gather_start(srcs, outs, *refs[2 * n:])
        _gather_forward(srcs, outs, *refs[2 * n:])
        _gather_finish(srcs, outs, *refs[2 * n:])

    any_spec = pl.BlockSpec(memory_space=pl.ANY)
    return pl.pallas_call(
        body, in_specs=[any_spec] * n, out_specs=[any_spec] * n,
        out_shape=[SDS((N_DEV,) + s.shape, s.dtype) for s in shards], scratch_shapes=_exchange_sems(n), name=name)(*shards)


def _exchange(arrays, name):
    n = len(arrays)

    def body(*refs):
        srcs, outs = refs[:n], refs[n:2 * n]
        _exchange_start(srcs, outs, *refs[2 * n:])
        _exchange_wait(srcs, outs, *refs[2 * n:])

    any_spec = pl.BlockSpec(memory_space=pl.ANY)
    return pl.pallas_call(
        body, in_specs=[any_spec] * n, out_specs=[any_spec] * n,
        out_shape=[SDS(a.shape, a.dtype) for a in arrays], scratch_shapes=_exchange_sems(n), name=name)(*arrays)


def _to_pattern(t, d):
    if d == 1:
        return t
    s, w = t.shape
    return t.reshape(s // d, d, w).transpose(1, 0, 2).reshape(s, w)


def _from_pattern(t, d):
    if d == 1:
        return t
    s, w = t.shape
    return t.reshape(d, s // d, w).transpose(1, 0, 2).reshape(s, w)


def _pad_lanes(t, n):
    return jnp.pad(t, ((0, 0), (0, n - t.shape[1])))


def _to_shards(g, axis):
    r, c = g.shape
    if axis == 0:
        return g.reshape(N_DEV, r // N_DEV, c)
    return g.reshape(r, N_DEV, c // N_DEV).transpose(1, 0, 2)


def _local_step(x, target, p, late_shards=(), early_exchange=True):
    s = x.shape[0]
    tabs_f, tabs_b = _rope_tables(s)
    expand = jnp.asarray(np.repeat(np.eye(N_HEADS, dtype=np.float32), HEAD_DIM, axis=1))
    w_main, w_dt = p["w_in"][:, :MAIN_W], _pad_lanes(p["w_in"][:, MAIN_W:], 128)
    al_r = {"f": p["a_log_f"], "b": p["a_log_b"]}
    bi_r = {"f": p["dt_bias_f"], "b": p["dt_bias_b"]}
    dskip_x = jnp.repeat(p["d_skip"], D_INNER // N_HEADS, axis=1)
    ssm_w3, ffn_w3 = p["ssm_conv_w"].T, p["ffn_conv_w"].T

    h1, h1t = _rmsnorm_fwd(x, p["norm1_w"], "norm1_fwd")
    proj = _matmul(h1, w_main, name="in_proj")
    dt_raw = _matmul(h1, w_dt, name="in_proj_dt", out_dtype=F32)
    dt_t = dt_raw[:, :2 * N_HEADS].T
    perms = {d: _perm_matrices(d) for d in DILATIONS[1:]}
    qkv = _rope_fwd(proj, tabs_f, perms, "rope_fwd")
    v_col = 2
    os_, lses = [], []
    for d, qkv_p in zip(DILATIONS, qkv):
        o_p, lse_p = _attn_fwd(qkv_p, qkv_p, v_col, s // d, f"attn_fwd_d{d}")
        os_.append(o_p)
        lses.append(_from_pattern(lse_p, d))
    mix, mix_t, lse_tot = _attn_combine(os_, lses, perms, expand, "attn_combine")

    xbc = _conv(proj, 3 * D_MODEL + D_INNER, XBC, ssm_w3, p["ssm_conv_b"], True, "ssm_conv_fwd")
    col = lambda r: r.reshape(N_HEADS, 1)
    ssd_args = {k: (al_r[k], col(al_r[k]), bi_r[k], col(bi_r[k])) for k in ("f", "b")}
    yf, st_f, yb, st_b, *got = _ssd_fwd(xbc, dt_raw, dt_t, ssd_args["f"], ssd_args["b"], "ssd_fwd", gather=late_shards)
    if late_shards:
        p = dict(p, w_out=got[0].reshape(2 * D_MODEL, D_MODEL), w_down=got[2].reshape(D_FF, D_MODEL),
                 w_up=got[1].transpose(1, 0, 2).reshape(D_MODEL, 2 * D_FF))
    mix, mix_t = _gate_fwd(yf, yb, xbc, proj, dskip_x, p["ssm_norm_w"], mix, mix_t, "ssm_gate_fwd")

    x2 =_matmul(mix, p["w_out"], name="out_proj", out_dtype=F32, residual=x)
    h2, h2t = _rmsnorm_fwd(x2, p["norm2_w"], "norm2_fwd")
    u = _matmul(h2, p["w_up"], name="ffn_up")
    act, act_t = _ffn_gate_fwd(u, ffn_w3, p["ffn_conv_b"], "ffn_gate_fwd")
    x3 = _matmul(act, p["w_down"], name="ffn_down", out_dtype=F32, residual=x2)

    dx3, dx3b, g_final, loss = _final_norm_loss(x3, p["final_norm_w"].reshape(1, D_MODEL), target, "final_norm_loss")
    g_w_down = _matmul(act_t, dx3b, name="dw_down")
    dact = _matmul(dx3b, p["w_down"], name="d_act", trans_b=True)
    dug, duu, dwg, dwu, dbg, dbu = _ffn_gate_bwd(u, ffn_w3, p["ffn_conv_b"], dact, "ffn_gate_bwd")
    du = _conv(dug, 0, D_FF, ffn_w3, p["ffn_conv_b"], False, "ffn_conv_bwd_gate", slab=(2 * D_FF, 0, None), transpose=True)
    du = _conv(duu, 0, D_FF, ffn_w3, p["ffn_conv_b"], False, "ffn_conv_bwd_up", slab=(2 * D_FF, D_FF, du), transpose=True, wcol0=D_FF)
    g_w_up = _matmul(h2t, du, name="dw_up")
    none_a, none_w = jnp.zeros((s, 128), BF16), jnp.zeros((D_MODEL, 128), BF16)
    dx2, dx2b, g_norm2 = _proj_norm_bwd(du, p["w_up"], none_a, none_w, x2, p["norm2_w"], dx3, "ffn_up_norm2_bwd")
    g_w_out = _matmul(mix_t, dx2b, name="dw_out")
    dmix = _matmul(dx2b, p["w_out"], name="d_mix", trans_b=True)

    delta, do_pat = _attn_delta(dmix, mix, expand.T, perms, "attn_delta")
    dqs, dks, dvs = [], [], []
    for d, qkv_p, do_p in zip(DILATIONS, qkv, [dmix] + do_pat):
        lse_p, dl_p = _to_pattern(lse_tot, d), _to_pattern(delta, d)
        dqs.append(_attn_bwd_dq(qkv_p, qkv_p, v_col, do_p, lse_p, dl_p, s // d, f"attn_bwd_dq_d{d}"))
        dk, dv = _attn_bwd_dkv(qkv_p, qkv_p, v_col, do_p, lse_p.T, dl_p.T, s // d, f"attn_bwd_dkv_d{d}")
        dks.append(dk)
        dvs.append(dv)
    dproj = _sum3_rope(dqs, perms, tabs_b, "rope_bwd_q", slab=(MAIN_W, 0, None))
    dproj = _sum3_rope(dks, perms, tabs_b, "rope_bwd_k", slab=(MAIN_W, D_MODEL, dproj))
    dproj = _sum3_rope(dvs, perms, None, "sum_dv", slab=(MAIN_W, 2 * D_MODEL, dproj))

    dy, dproj, dxs_skip, g_ssm_norm, g_dskip_lanes = _gate_bwd(yf, yb, xbc, proj, dskip_x, p["ssm_norm_w"], dmix, dproj, "ssm_gate_bwd")
    early = [_to_shards(g_w_out, 0), _to_shards(g_w_up, 1), _to_shards(g_w_down, 0)] if early_exchange else []
    dxbc_f, ddt_f, g_al_f, g_bi_f, *got = _ssd_bwd(xbc, dt_raw, dt_t, *ssd_args["f"], st_f, dy, 0, False, "ssd_bwd_f", exchange=early)
    if early_exchange:
        g_w_out, g_w_up, g_w_down = got
    dxbc_b, ddt_b, g_al_b, g_bi_b = _ssd_bwd(xbc, dt_raw, dt_t, *ssd_args["b"], st_b, dy, N_HEADS, True, "ssd_bwd_b")
    dpre, g_ssm_w3, g_ssm_cb = _conv_silu_bwd(proj, 3 * D_MODEL + D_INNER, XBC, ssm_w3, p["ssm_conv_b"],
                                              [dxbc_f, dxbc_b, dxs_skip], [XBC, XBC, D_INNER], "ssm_conv_bwd")
    dproj = _conv(dpre, 0, XBC, ssm_w3, p["ssm_conv_b"], False, "ssm_conv_bwd_x", transpose=True,
                  slab=(MAIN_W, 3 * D_MODEL + D_INNER, dproj))

    ddt =_pad_lanes(jnp.concatenate([ddt_f, ddt_b], axis=1), 128).astype(BF16)
    g_w_main = _matmul(h1t, dproj, name="dw_in")
    g_w_dt = _matmul(h1t, ddt, name="dw_in_dt")
    g_w_in = jnp.concatenate([g_w_main, g_w_dt[:, :2 * N_HEADS]], axis=1)
    late = [_to_shards(g_w_in, 1)] if early_exchange else []
    grad_x, _, g_norm1, *got = _proj_norm_bwd(dproj, w_main, ddt, w_dt, x, p["norm1_w"], dx2, "in_proj_norm1_bwd", exchange=late)
    if early_exchange:
        g_w_in = got[0]

    g_dskip = jnp.sum(g_dskip_lanes.reshape(N_HEADS, D_INNER // N_HEADS), axis=1).reshape(1, N_HEADS)
    small = {
        "norm1_w": g_norm1, "ssm_conv_w": g_ssm_w3.T, "ssm_conv_b": g_ssm_cb, "a_log_f": g_al_f, "a_log_b": g_al_b,
        "dt_bias_f": g_bi_f, "dt_bias_b": g_bi_b, "d_skip": g_dskip, "ssm_norm_w": g_ssm_norm, "norm2_w": g_norm2,
        "ffn_conv_w": jnp.concatenate([dwg, dwu], axis=1).T, "ffn_conv_b": jnp.concatenate([dbg, dbu], axis=1), "final_norm_w": g_final,
    }
    big = {"w_in": g_w_in, "w_out": g_w_out, "w_up": g_w_up, "w_down": g_w_down}
    return loss[0, 0], grad_x, big, small


SMALL_ORDER = ("norm1_w", "ssm_conv_w", "ssm_conv_b", "a_log_f", "a_log_b", "dt_bias_f", "dt_bias_b", "d_skip",
               "ssm_norm_w", "norm2_w", "ffn_conv_w", "ffn_conv_b", "final_norm_w")
SHARDED_SMALL = ("ssm_conv_w", "ffn_conv_w")
BIG_ORDER = ("w_in", "w_out", "w_up", "w_down")


def _pack(vals):
    rows = []
    for v in vals:
        f = v.reshape(-1).astype(F32)
        n = -(-f.shape[0] // 128) * 128
        rows.append(jnp.pad(f, (0, n - f.shape[0])).reshape(-1, 128))
    out = jnp.concatenate(rows, axis=0)
    pad = -out.shape[0] % 8
    return jnp.pad(out, ((0, pad), (0, 0)))


def _unpack(packed, shapes):
    out, r = [], 0
    for shp in shapes:
        n = math.prod(shp)
        nr = -(-n // 128)
        out.append(packed[r:r + nr].reshape(-1)[:n].reshape(shp))
        r += nr
    return out


def kernel(x, norm1_w, w_in, ssm_conv_w, ssm_conv_b, a_log_f, a_log_b, dt_bias_f, dt_bias_b, d_skip, ssm_norm_w, w_out, norm2_w, w_up, ffn_conv_w, ffn_conv_b, w_down, final_norm_w, loss_target, m_norm1_w, m_w_in, m_ssm_conv_w, m_ssm_conv_b, m_a_log_f, m_a_log_b, m_dt_bias_f, m_dt_bias_b, m_d_skip, m_ssm_norm_w, m_w_out, m_norm2_w, m_w_up, m_ffn_conv_w, m_ffn_conv_b, m_w_down, m_final_norm_w, v_norm1_w, v_w_in, v_ssm_conv_w, v_ssm_conv_b, v_a_log_f, v_a_log_b, v_dt_bias_f, v_dt_bias_b, v_d_skip, v_ssm_norm_w, v_w_out, v_norm2_w, v_w_up, v_ffn_conv_w, v_ffn_conv_b, v_w_down, v_final_norm_w):
    w = dict(norm1_w=norm1_w, w_in=w_in, ssm_conv_w=ssm_conv_w, ssm_conv_b=ssm_conv_b, a_log_f=a_log_f, a_log_b=a_log_b,
             dt_bias_f=dt_bias_f, dt_bias_b=dt_bias_b, d_skip=d_skip, ssm_norm_w=ssm_norm_w, w_out=w_out, norm2_w=norm2_w,
             w_up=w_up, ffn_conv_w=ffn_conv_w, ffn_conv_b=ffn_conv_b, w_down=w_down, final_norm_w=final_norm_w)
    mo = dict(norm1_w=m_norm1_w, w_in=m_w_in, ssm_conv_w=m_ssm_conv_w, ssm_conv_b=m_ssm_conv_b, a_log_f=m_a_log_f, a_log_b=m_a_log_b,
              dt_bias_f=m_dt_bias_f, dt_bias_b=m_dt_bias_b, d_skip=m_d_skip, ssm_norm_w=m_ssm_norm_w, w_out=m_w_out, norm2_w=m_norm2_w,
              w_up=m_w_up, ffn_conv_w=m_ffn_conv_w, ffn_conv_b=m_ffn_conv_b, w_down=m_w_down, final_norm_w=m_final_norm_w)
    vo = dict(norm1_w=v_norm1_w, w_in=v_w_in, ssm_conv_w=v_ssm_conv_w, ssm_conv_b=v_ssm_conv_b, a_log_f=v_a_log_f, a_log_b=v_a_log_b,
              dt_bias_f=v_dt_bias_f, dt_bias_b=v_dt_bias_b, d_skip=v_d_skip, ssm_norm_w=v_ssm_norm_w, w_out=v_w_out, norm2_w=v_norm2_w,
              w_up=v_w_up, ffn_conv_w=v_ffn_conv_w, ffn_conv_b=v_ffn_conv_b, w_down=v_w_down, final_norm_w=v_final_norm_w)
    me = _my_index()

    g_in, g_conv = _all_gather([w["w_in"][0].astype(BF16), _pack([w["ssm_conv_w"][0], w["ffn_conv_w"][0]])], "w_in_all_gather")
    conv_rows = [_unpack(g_conv[i], [ssm_conv_w.shape[1:], ffn_conv_w.shape[1:]]) for i in range(N_DEV)]
    full = {
        "w_in": g_in.transpose(1, 0, 2).reshape(D_MODEL, N_DEV * w_in.shape[2]),
        "ssm_conv_w": jnp.concatenate([c[0] for c in conv_rows], axis=0),
        "ffn_conv_w": jnp.concatenate([c[1] for c in conv_rows], axis=0),
    }
    for k in ("norm1_w", "ssm_conv_b", "a_log_f", "a_log_b", "dt_bias_f", "dt_bias_b", "d_skip", "ssm_norm_w", "norm2_w", "ffn_conv_b",
              "final_norm_w"):
        full[k] = w[k]
    late = [w["w_out"][0].astype(BF16), w["w_up"][0].astype(BF16), w["w_down"][0].astype(BF16)]

    loss_part, grad_x, big, small = _local_step(x[0], loss_target[0], full, late)

    small_shapes = [(1,)] + [small[k].shape for k in SMALL_ORDER]
    packed = _pack([loss_part] + [small[k] for k in SMALL_ORDER])
    out_small = jnp.broadcast_to(packed[None], (N_DEV,) + packed.shape)
    (r_small,) = _exchange([out_small], "small_grads_exchange")
    r_in, r_out, r_up, r_down = big["w_in"], big["w_out"], big["w_up"], big["w_down"]

    outs_g, outs_d, outs_m, outs_v = {}, {}, {}, {}
    for k, parts in zip(BIG_ORDER, (r_in, r_out, r_up, r_down)):
        g, dlt, nm, nv = _adamw(parts, w[k][0], mo[k][0], vo[k][0], f"adamw_{k}")
        outs_g[k], outs_d[k], outs_m[k], outs_v[k] = g[None], dlt[None], nm[None], nv[None]
    tot = _unpack(_sum_parts(r_small, "small_grads_sum"), small_shapes)
    loss = tot[0][0]
    gs = dict(zip(SMALL_ORDER, tot[1:]))
    g_own = {}
    for k in SMALL_ORDER:
        if k in SHARDED_SMALL:
            rows = w[k].shape[1]
            g_own[k] = lax.dynamic_slice_in_dim(gs[k], me * rows, rows, axis=0)[None]
        else:
            g_own[k] = gs[k].reshape(w[k].shape)
    shapes = [w[k].shape for k in SMALL_ORDER]
    d_s, m_s, v_s = _adamw_small(_pack([g_own[k] for k in SMALL_ORDER]), _pack([w[k] for k in SMALL_ORDER]),
                                 _pack([mo[k] for k in SMALL_ORDER]), _pack([vo[k] for k in SMALL_ORDER]), "adamw_small")
    for k, a, b, c in zip(SMALL_ORDER, _unpack(d_s, shapes), _unpack(m_s, shapes), _unpack(v_s, shapes)):
        outs_g[k], outs_d[k], outs_m[k], outs_v[k] = g_own[k], a, b, c

    order = ("norm1_w", "w_in", "ssm_conv_w", "ssm_conv_b", "a_log_f", "a_log_b", "dt_bias_f", "dt_bias_b", "d_skip", "ssm_norm_w",
             "w_out", "norm2_w", "w_up", "ffn_conv_w", "ffn_conv_b", "w_down", "final_norm_w")
    return (loss, grad_x[None], *[outs_g[k] for k in order], *[outs_d[k] for k in order],
            *[outs_m[k] for k in order], *[outs_v[k] for k in order])
```

```python
import math

import numpy as np
import jax
import jax.numpy as jnp
from jax import lax
from jax.experimental import pallas as pl
from jax.experimental.pallas import tpu as pltpu

F32 = jnp.float32
BF16 = jnp.bfloat16
SDS = jax.ShapeDtypeStruct

N_DEV = 8
D_MODEL = 1024
N_HEADS = 16
HEAD_DIM = 64
ROPE_DIM = 16
ROPE_THETA = 500000.0
DILATIONS = (1, 4, 16)
BAND_HALF = 64
D_INNER = 1024
N_GROUPS = 4
D_STATE = 128
CHUNK = 128
XBC = D_INNER + 2 * N_GROUPS * D_STATE
D_FF = 2816
MAIN_W = 3 * D_MODEL + D_INNER + XBC
EPS = 1e-6
LR, B1, B2, AEPS, WD, STEP = 0.001, 0.9, 0.999, 1e-08, 0.01, 10
NEG = -1e30
VMEM_LIMIT = 56 * 1024 * 1024
MESH = pl.DeviceIdType.MESH
HIGH = lax.Precision.HIGHEST
NT = (((1,), (1,)), ((), ()))
TN = (((0,), (0,)), ((), ()))


def _cp(*sem):
    return pltpu.CompilerParams(dimension_semantics=sem, vmem_limit_bytes=VMEM_LIMIT)


def _pick(n, cands):
    for c in cands:
        if n % c == 0:
            return c
    raise ValueError(f"no tile for {n}")


def _sigmoid(x):
    return 1.0 / (1.0 + jnp.exp(-x))


def _softplus(x):
    return jnp.maximum(x, 0.0) + jnp.log1p(jnp.exp(-jnp.abs(x)))


def _slab(s, width, dtype, slab, n_in, out_idx=0):
    if slab is None:
        return SDS((s, width), dtype), 0, [], [], {}
    total, col0, into = slab
    if into is None:
        return SDS((s, total), dtype), col0, [], [], {}
    return SDS((s, total), dtype), col0, [into], [pl.BlockSpec(memory_space=pl.ANY)], {n_in: out_idx}


def _matmul(a, b, *, name, trans_b=False, out_dtype=BF16, residual=None):
    m, k = a.shape
    n = b.shape[0] if trans_b else b.shape[1]
    tk = k if k <= 2048 else _pick(k, (2048, 1408, 1024, 512))
    nk = k // tk
    if nk == 1:
        tm = _pick(m, (2048, 1408, 1024, 512, 256, 128))
        tn = _pick(n, (512, 256, 128))
    else:
        tm = _pick(m, (1024, 1408, 512, 256, 128))
        tn = _pick(n, (1024, 1408, 512, 256, 128))
    dn = NT if trans_b else (((1,), (0,)), ((), ()))

    def body(*refs):
        a_ref, b_ref = refs[0], refs[1]
        o_ref, acc = refs[-2], refs[-1]
        kk = pl.program_id(2)

        @pl.when(kk == 0)
        def _():
            acc[...] = jnp.zeros_like(acc)

        acc[...] += lax.dot_general(a_ref[...], b_ref[...], dn, preferred_element_type=F32)

        @pl.when(kk == nk - 1)
        def _():
            r = acc[...]
            if residual is not None:
                r = r + refs[2][...].astype(F32)
            o_ref[...] = r.astype(o_ref.dtype)

    in_specs = [pl.BlockSpec((tm, tk), lambda i, j, kk: (i, kk)),
                pl.BlockSpec((tn, tk), lambda i, j, kk: (j, kk)) if trans_b else pl.BlockSpec((tk, tn), lambda i, j, kk: (kk, j))]
    args = [a, b]
    if residual is not None:
        in_specs.append(pl.BlockSpec((tm, tn), lambda i, j, kk: (i, j)))
        args.append(residual)
    return pl.pallas_call(
        body, grid=(m // tm, n // tn, nk), in_specs=in_specs,
        out_specs=pl.BlockSpec((tm, tn), lambda i, j, kk: (i, j)),
        out_shape=SDS((m, n), out_dtype), scratch_shapes=[pltpu.VMEM((tm, tn), F32)],
        name=name, compiler_params=_cp("parallel", "parallel", "arbitrary"))(*args)


def _rmsnorm_fwd(x, w, name):
    s, d = x.shape
    tm = _pick(s, (512, 128))

    def body(x_ref, w_ref, o_ref, ot_ref):
        xf = x_ref[...]
        r = lax.rsqrt(jnp.mean(xf * xf, axis=-1, keepdims=True) + EPS)
        out = (xf * r * w_ref[...]).astype(o_ref.dtype)
        o_ref[...] = out
        ot_ref[...] = out.T

    return pl.pallas_call(
        body, grid=(s // tm,), in_specs=[pl.BlockSpec((tm, d), lambda i: (i, 0)), pl.BlockSpec((1, d), lambda i: (0, 0))],
        out_specs=[pl.BlockSpec((tm, d), lambda i: (i, 0)), pl.BlockSpec((d, tm), lambda i: (0, i))],
        out_shape=[SDS((s, d), BF16), SDS((d, s), BF16)], name=name, compiler_params=_cp("parallel"))(x, w)


def _proj_norm_bwd(da, wt, da2, wt2, x, w, dres, name, exchange=()):
    s, k = da.shape
    d = x.shape[1]
    tm = _pick(s, (512, 128))
    tk = _pick(k, (2048, 1408, 1024, 512))
    nk = k // tk
    nx = len(exchange)
    grid = (s // tm, nk)

    def body(*refs):
        a_ref, b_ref, a2_ref, b2_ref, x_ref, w_ref, dres_ref = refs[:7]
        xch_src = refs[7:7 + nx]
        dx_ref, dxb_ref, dw_ref = refs[7 + nx:10 + nx]
        xch_dst = refs[10 + nx:10 + 2 * nx]
        acc = refs[10 + 2 * nx]
        xch_sems = refs[11 + 2 * nx:]
        i, kk = pl.program_id(0), pl.program_id(1)

        @pl.when((i == 0) & (kk == 0))
        def _():
            dw_ref[...] = jnp.zeros_like(dw_ref)
            if nx:
                _exchange_start(xch_src, xch_dst, *xch_sems)

        @pl.when(kk == 0)
        def _():
            acc[...] = lax.dot_general(a2_ref[...], b2_ref[...], NT, preferred_element_type=F32)

        acc[...] += lax.dot_general(a_ref[...], b_ref[...], NT, preferred_element_type=F32)

        @pl.when(kk == nk - 1)
        def _():
            dh = acc[...]
            xf = x_ref[...]
            r = lax.rsqrt(jnp.mean(xf * xf, axis=-1, keepdims=True) + EPS)
            xhat = xf * r
            g = dh * w_ref[...]
            dx = dres_ref[...] + r * (g - xhat * jnp.mean(g * xhat, axis=-1, keepdims=True))
            dx_ref[...] = dx
            dxb_ref[...] = dx.astype(dxb_ref.dtype)
            dw_ref[...] += jnp.sum(dh * xhat, axis=0, keepdims=True)

        if nx:
            @pl.when((i == grid[0] - 1) & (kk == nk - 1))
            def _():
                _exchange_wait(xch_src, xch_dst, *xch_sems)

    row = pl.BlockSpec((tm, d), lambda i, kk: (i, 0))
    vec = pl.BlockSpec((1, d), lambda i, kk: (0, 0))
    any_spec = pl.BlockSpec(memory_space=pl.ANY)
    in_specs = [pl.BlockSpec((tm, tk), lambda i, kk: (i, kk)), pl.BlockSpec((d, tk), lambda i, kk: (0, kk)),
                pl.BlockSpec((tm, da2.shape[1]), lambda i, kk: (i, 0)), pl.BlockSpec((d, wt2.shape[1]), lambda i, kk: (0, 0)),
                row, vec, row]
    return pl.pallas_call(
        body, grid=grid, in_specs=in_specs + [any_spec] * nx, out_specs=[row, row, vec] + [any_spec] * nx,
        out_shape=[SDS((s, d), F32), SDS((s, d), BF16), SDS((1, d), F32)] + [SDS(e.shape, e.dtype) for e in exchange],
        scratch_shapes=[pltpu.VMEM((tm, d), F32)] + (_exchange_sems(nx) if nx else []),
        name=name, compiler_params=_cp("arbitrary", "arbitrary"))(da, wt, da2, wt2, x, w, dres, *exchange)


def _final_norm_loss(x, w, target, name):
    s, d = x.shape
    tm = _pick(s, (512, 128))

    def body(x_ref, w_ref, t_ref, dx_ref, dxb_ref, dw_ref, loss_ref):
        xf = x_ref[...]
        r = lax.rsqrt(jnp.mean(xf * xf, axis=-1, keepdims=True) + EPS)
        xhat = xf * r
        wv = w_ref[...]
        e = xhat * wv - t_ref[...]
        dy = e * (1.0 / d)
        g = dy * wv
        dx = r * (g - xhat * jnp.mean(g * xhat, axis=-1, keepdims=True))
        dx_ref[...] = dx
        dxb_ref[...] = dx.astype(dxb_ref.dtype)

        @pl.when(pl.program_id(0) == 0)
        def _():
            dw_ref[...] = jnp.zeros_like(dw_ref)
            loss_ref[...] = jnp.zeros_like(loss_ref)

        dw_ref[...] += jnp.sum(dy * xhat, axis=0, keepdims=True)
        loss_ref[...] += jnp.sum(jnp.sum(e * e, axis=1, keepdims=True), axis=0, keepdims=True) * (0.5 / d)

    row = pl.BlockSpec((tm, d), lambda i: (i, 0))
    vec = pl.BlockSpec((1, d), lambda i: (0, 0))
    return pl.pallas_call(
        body, grid=(s // tm,), in_specs=[row, vec, row], out_specs=[row, row, vec, pl.BlockSpec((1, 128), lambda i: (0, 0))],
        out_shape=[SDS((s, d), F32), SDS((s, d), BF16), SDS((1, d), F32), SDS((1, 128), F32)],
        name=name, compiler_params=_cp("arbitrary"))(x, w, target)


def _rope_tables(s):
    half = ROPE_DIM // 2
    f32 = np.float32
    inv_freq = np.power(f32(ROPE_THETA), -np.arange(half, dtype=f32) * f32(2.0) / f32(ROPE_DIM)).astype(f32)
    ang = (np.arange(s, dtype=f32)[:, None] * inv_freq[None, :]).astype(f32)
    cos, sin = np.cos(ang).astype(f32), np.sin(ang).astype(f32)
    z = np.zeros((s, HEAD_DIM - ROPE_DIM), f32)
    zh = np.zeros((s, half), f32)
    c = np.concatenate([cos, cos, z + 1.0], axis=1)
    sa = np.concatenate([zh, sin, z], axis=1)
    sb = np.concatenate([-sin, zh, z], axis=1)
    two = lambda t: np.concatenate([t, t], axis=1)
    c, sa, sb = two(c), two(sa), two(sb)
    fwd, bwd = (c, sa, sb), (c, np.roll(sb, half, axis=1), np.roll(sa, -half, axis=1))
    return tuple(jnp.asarray(t) for t in fwd), tuple(jnp.asarray(t) for t in bwd)


PERM_TILE = 256


def _perm_matrices(d):
    n = PERM_TILE // d
    o = np.arange(PERM_TILE)
    p = np.zeros((PERM_TILE, PERM_TILE), np.float32)
    p[o, (o % n) * d + o // n] = 1.0
    return jnp.asarray(p, dtype=BF16), jnp.asarray(p.T.copy(), dtype=BF16)


def _store_pattern(o_ref, tile, perm_ref, d, cols=slice(None)):
    n = PERM_TILE // d
    z = jnp.dot(perm_ref[...], tile, preferred_element_type=F32).astype(o_ref.dtype)
    for r in range(d):
        o_ref[r, :, cols] = z[r * n:(r + 1) * n]


def _load_pattern(x_ref, perm_ref, d):
    tile = jnp.concatenate([x_ref[r] for r in range(d)], axis=0)
    return jnp.dot(perm_ref[...], tile, preferred_element_type=F32)


def _pattern_spec(d, w, col=0):
    return pl.BlockSpec((d, PERM_TILE // d, w), lambda i, *_: (0, i, col))


def _rope_fwd(proj, tabs, perms, name):
    s = proj.shape[0]
    tm = PERM_TILE
    half = ROPE_DIM // 2
    wb = D_MODEL
    nd = len(DILATIONS) - 1

    def body(x_ref, c_ref, sa_ref, sb_ref, *rest):
        perm_refs, o_ref, op_refs = rest[:nd], rest[nd], rest[nd + 1:]
        is_v = pl.program_id(1) == 2
        c = jnp.where(is_v, 1.0, c_ref[...])
        sa = jnp.where(is_v, 0.0, sa_ref[...])
        sb = jnp.where(is_v, 0.0, sb_ref[...])
        for j in range(wb // 128):
            x = x_ref[:, j * 128:(j + 1) * 128].astype(F32)
            o_ref[:, j * 128:(j + 1) * 128] = (x * c + pltpu.roll(x, half, 1) * sa + pltpu.roll(x, 128 - half, 1) * sb).astype(o_ref.dtype)
        y = o_ref[...]
        for d, perm_ref, op_ref in zip(DILATIONS[1:], perm_refs, op_refs):
            _store_pattern(op_ref, y, perm_ref, d)

    blk = pl.BlockSpec((tm, wb), lambda i, j: (i, j))
    tab = pl.BlockSpec((tm, 128), lambda i, j: (i, 0))
    pm = pl.BlockSpec((tm, tm), lambda i, j: (0, 0))
    outs = pl.pallas_call(
        body, grid=(s // tm, 3), in_specs=[blk, tab, tab, tab] + [pm] * nd,
        out_specs=[blk] + [pl.BlockSpec((d, tm // d, wb), lambda i, j: (0, i, j)) for d in DILATIONS[1:]],
        out_shape=[SDS((s, 3 * wb), BF16)] + [SDS((d, s // d, 3 * wb), BF16) for d in DILATIONS[1:]],
        name=name, compiler_params=_cp("parallel", "parallel"))(proj, *tabs, *[perms[d][0] for d in DILATIONS[1:]])
    return [o.reshape(s, 3 * wb) for o in outs]


def _sum3_rope(ds_, perms, tabs, name, slab=None):
    s, w = ds_[0].shape
    tm = PERM_TILE
    half = ROPE_DIM // 2
    nd = len(DILATIONS) - 1

    def body(*refs):
        x_refs, perm_refs = refs[:nd + 1], refs[nd + 1:2 * nd + 1]
        tab_refs = refs[2 * nd + 1:2 * nd + 4]
        tot = x_refs[0][...].astype(F32)
        for d, x_ref, perm_ref in zip(DILATIONS[1:], x_refs[1:], perm_refs):
            tot = tot + _load_pattern(x_ref, perm_ref, d)
        for j in range(w // 128):
            x = tot[:, j * 128:(j + 1) * 128]
            if tabs is not None:
                x = x * tab_refs[0][...] + pltpu.roll(x, half, 1) * tab_refs[1][...] + pltpu.roll(x, 128 - half, 1) * tab_refs[2][...]
            refs[-1][:, j * 128:(j + 1) * 128] = x.astype(refs[-1].dtype)

    blk = pl.BlockSpec((tm, w), lambda i: (i, 0))
    tab = pl.BlockSpec((tm, 128), lambda i: (i, 0))
    pm = pl.BlockSpec((tm, tm), lambda i: (0, 0))
    extra = [] if tabs is None else list(tabs)
    out_shape, col0, more, more_specs, alias = _slab(s, w, BF16, slab, 2 * nd + 1 + len(extra))
    cb = col0 // w
    args = [ds_[0]] + [x.reshape(d, s // d, w) for d, x in zip(DILATIONS[1:], ds_[1:])] + [perms[d][1] for d in DILATIONS[1:]]
    return pl.pallas_call(
        body, grid=(s // tm,),
        in_specs=[blk] + [_pattern_spec(d, w) for d in DILATIONS[1:]] + [pm] * nd + [tab] * len(extra) + more_specs,
        out_specs=pl.BlockSpec((tm, w), lambda i: (i, cb)), out_shape=out_shape, input_output_aliases=alias,
        name=name, compiler_params=_cp("parallel"))(*args, *extra, *more)


def _band_valid(t, nq, nk, qofs, kofs, seq_len):
    qpos = t * 128 + qofs + lax.broadcasted_iota(jnp.int32, (nq, nk), 0)
    kpos = t * 128 + kofs + lax.broadcasted_iota(jnp.int32, (nq, nk), 1)
    sh = int(math.log2(seq_len))
    same = lax.shift_right_arithmetic(qpos, sh) == lax.shift_right_arithmetic(kpos, sh)
    return same & (jnp.abs(kpos - qpos) <= BAND_HALF)


def _window(r0, r1, r2):
    return jnp.concatenate([r0[128 - BAND_HALF:128], r1[...], r2[0:BAND_HALF]], axis=0)


WIN = 128 + 2 * BAND_HALF


def _first_head_lanes():
    return lax.broadcasted_iota(jnp.int32, (1, 2 * HEAD_DIM), 1) < HEAD_DIM


def _only(x, keep):
    return jnp.where(keep, x, jnp.zeros((), x.dtype))


def _win_specs(width, col, nt):
    return [pl.BlockSpec((128, width), lambda t: (jnp.maximum(t - 1, 0), col)),
            pl.BlockSpec((128, width), lambda t: (t, col)),
            pl.BlockSpec((128, width), lambda t: (jnp.minimum(t + 1, nt - 1), col))]


def _attn_fwd(qk, v_src, v_col, seq_len, name):
    s = qk.shape[0]
    nt = s // 128
    dm = D_MODEL

    def body(q_ref, k0, k1, k2, v0, v1, v2, o_ref, lse_ref):
        t = pl.program_id(0)
        valid = _band_valid(t, 128, WIN, 0, -BAND_HALF, seq_len)
        q = q_ref[...]
        kc = _window(k0, k1, k2)
        vc = _window(v0, v1, v2)
        first = _first_head_lanes()
        outs, lses = [], []
        for pr in range(N_HEADS // 2):
            ps = slice(pr * 128, (pr + 1) * 128)
            qp, kp, vp = q[:, ps], kc[:, ps], vc[:, ps]
            halves = []
            for keep in (first, ~first):
                sc = lax.dot_general(_only(qp, keep), kp, NT, preferred_element_type=F32) * (HEAD_DIM ** -0.5)
                sc = jnp.where(valid, sc, NEG)
                m = jnp.max(sc, axis=1, keepdims=True)
                e = jnp.exp(sc - m)
                den = jnp.sum(e, axis=1, keepdims=True)
                halves.append(jnp.dot(e.astype(BF16), vp, preferred_element_type=F32) / den)
                lses.append(m + jnp.log(den))
            outs.append(jnp.where(first, halves[0], halves[1]))
        o_ref[...] = jnp.concatenate(outs, axis=1).astype(o_ref.dtype)
        lse_ref[...] = jnp.concatenate(lses, axis=1)

    in_specs = [pl.BlockSpec((128, dm), lambda t: (t, 0))] + _win_specs(dm, 1, nt) + _win_specs(dm, v_col, nt)
    return pl.pallas_call(
        body, grid=(nt,), in_specs=in_specs,
        out_specs=[pl.BlockSpec((128, dm), lambda t: (t, 0)), pl.BlockSpec((128, N_HEADS), lambda t: (t, 0))],
        out_shape=[SDS((s, dm), BF16), SDS((s, N_HEADS), F32)], name=name, compiler_params=_cp("parallel"))(
            qk, qk, qk, qk, v_src, v_src, v_src)


def _attn_combine(os_, lses, perms, expand, name):
    s, dm = os_[0].shape
    tm = PERM_TILE
    nd = len(DILATIONS) - 1

    def body(o1, o2, o3, l1, l2, l3, p2, p3, e_ref, out_ref, out_t_ref, lt_ref):
        ls = [l1[...], l2[...], l3[...]]
        m = jnp.maximum(jnp.maximum(ls[0], ls[1]), ls[2])
        es = [jnp.exp(l - m) for l in ls]
        tot = es[0] + es[1] + es[2]
        lt_ref[...] = m + jnp.log(tot)
        ovs = [o1[...].astype(F32), _load_pattern(o2, p2, DILATIONS[1]), _load_pattern(o3, p3, DILATIONS[2])]
        acc = jnp.zeros((tm, dm), F32)
        for e, o in zip(es, ovs):
            acc = acc + jnp.dot(e / tot, e_ref[...], precision=HIGH, preferred_element_type=F32) * o
        out = acc.astype(out_ref.dtype)
        out_ref[...] = out
        out_t_ref[...] = out.T

    row = pl.BlockSpec((tm, dm), lambda i: (i, 0))
    st = pl.BlockSpec((tm, N_HEADS), lambda i: (i, 0))
    pm = pl.BlockSpec((tm, tm), lambda i: (0, 0))
    args = [os_[0]] + [o.reshape(d, s // d, dm) for d, o in zip(DILATIONS[1:], os_[1:])]
    return pl.pallas_call(
        body, grid=(s // tm,),
        in_specs=[row] + [_pattern_spec(d, dm) for d in DILATIONS[1:]] + [st, st, st, pm, pm, pl.BlockSpec((N_HEADS, dm), lambda i: (0, 0))],
        out_specs=[row, pl.BlockSpec((dm, tm), lambda i: (0, i)), st],
        out_shape=[SDS((s, dm + D_INNER), BF16), SDS((dm + D_INNER, s), BF16), SDS((s, N_HEADS), F32)],
        name=name, compiler_params=_cp("parallel"))(*args, *lses, *[perms[d][1] for d in DILATIONS[1:]], expand)


def _attn_delta(dmix, attn, expand_t, perms, name):
    s, dm = attn.shape[0], D_MODEL
    tm = PERM_TILE
    nd = len(DILATIONS) - 1

    def body(d_ref, a_ref, e_ref, *rest):
        perm_refs, o_ref, op_refs = rest[:nd], rest[nd], rest[nd + 1:]
        dv = d_ref[...]
        prod = dv.astype(F32) * a_ref[...].astype(F32)
        o_ref[...] = jnp.dot(prod, e_ref[...], precision=HIGH, preferred_element_type=F32)
        for d, perm_ref, op_ref in zip(DILATIONS[1:], perm_refs, op_refs):
            _store_pattern(op_ref, dv, perm_ref, d)

    row = pl.BlockSpec((tm, dm), lambda i: (i, 0))
    pm = pl.BlockSpec((tm, tm), lambda i: (0, 0))
    outs = pl.pallas_call(
        body, grid=(s // tm,), in_specs=[row, row, pl.BlockSpec((dm, N_HEADS), lambda i: (0, 0))] + [pm] * nd,
        out_specs=[pl.BlockSpec((tm, N_HEADS), lambda i: (i, 0))] + [_pattern_spec(d, dm) for d in DILATIONS[1:]],
        out_shape=[SDS((s, N_HEADS), F32)] + [SDS((d, s // d, dm), BF16) for d in DILATIONS[1:]],
        name=name, compiler_params=_cp("parallel"))(dmix, attn, expand_t, *[perms[d][0] for d in DILATIONS[1:]])
    return outs[0], [o.reshape(s, dm) for o in outs[1:]]


def _attn_bwd_dq(qk, v_src, v_col, do_src, lse, delta, seq_len, name):
    s = qk.shape[0]
    nt = s // 128
    dm = D_MODEL

    def body(q_ref, k0, k1, k2, v0, v1, v2, do_ref, lse_ref, dl_ref, dq_ref):
        t = pl.program_id(0)
        valid = _band_valid(t, 128, WIN, 0, -BAND_HALF, seq_len)
        q = q_ref[...]
        do = do_ref[...]
        kc = _window(k0, k1, k2)
        vc = _window(v0, v1, v2)
        lse_v, dl_v = lse_ref[...], dl_ref[...]
        first = _first_head_lanes()
        outs = []
        for pr in range(N_HEADS // 2):
            ps = slice(pr * 128, (pr + 1) * 128)
            qp, dop, kp, vp = q[:, ps], do[:, ps], kc[:, ps], vc[:, ps]
            halves = []
            for i, keep in enumerate((first, ~first)):
                h = 2 * pr + i
                sc = lax.dot_general(_only(qp, keep), kp, NT, preferred_element_type=F32) * (HEAD_DIM ** -0.5)
                p = jnp.exp(jnp.where(valid, sc - lse_v[:, h:h + 1], NEG))
                dp = lax.dot_general(_only(dop, keep), vp, NT, preferred_element_type=F32)
                ds = p * (dp - dl_v[:, h:h + 1])
                halves.append(jnp.dot(ds.astype(BF16), kp, preferred_element_type=F32))
            outs.append(jnp.where(first, halves[0], halves[1]) * (HEAD_DIM ** -0.5))
        dq_ref[...] = jnp.concatenate(outs, axis=1).astype(dq_ref.dtype)

    row = pl.BlockSpec((128, dm), lambda t: (t, 0))
    st = pl.BlockSpec((128, N_HEADS), lambda t: (t, 0))
    in_specs = [row] + _win_specs(dm, 1, nt) + _win_specs(dm, v_col, nt) + [row, st, st]
    return pl.pallas_call(
        body, grid=(nt,), in_specs=in_specs, out_specs=row, out_shape=SDS((s, dm), BF16),
        name=name, compiler_params=_cp("parallel"))(qk, qk, qk, qk, v_src, v_src, v_src, do_src, lse, delta)


def _attn_bwd_dkv(qk, v_src, v_col, do_src, lse_t, delta_t, seq_len, name):
    s = qk.shape[0]
    nt = s // 128
    dm = D_MODEL

    def lane_window(r0, r1, r2):
        return jnp.concatenate([r0[:, 128 - BAND_HALF:128], r1[...], r2[:, 0:BAND_HALF]], axis=1)

    def body(k_ref, v_ref, q0, q1, q2, d0, d1, d2, l0, l1, l2, e0, e1, e2, dk_ref, dv_ref):
        t = pl.program_id(0)
        valid = _band_valid(t, 128, WIN, 0, -BAND_HALF, seq_len)
        k = k_ref[...]
        v = v_ref[...]
        qc = _window(q0, q1, q2)
        dc = _window(d0, d1, d2)
        lse_v = lane_window(l0, l1, l2)
        dl_v = lane_window(e0, e1, e2)
        first = _first_head_lanes()
        dks, dvs = [], []
        for pr in range(N_HEADS // 2):
            ps = slice(pr * 128, (pr + 1) * 128)
            kp, vp, qp, dop = k[:, ps], v[:, ps], qc[:, ps], dc[:, ps]
            dk_h, dv_h = [], []
            for i, keep in enumerate((first, ~first)):
                h = 2 * pr + i
                sc = lax.dot_general(_only(kp, keep), qp, NT, preferred_element_type=F32) * (HEAD_DIM ** -0.5)
                p = jnp.exp(jnp.where(valid, sc - lse_v[h:h + 1, :], NEG))
                dv_h.append(jnp.dot(p.astype(BF16), dop, preferred_element_type=F32))
                dp = lax.dot_general(_only(vp, keep), dop, NT, preferred_element_type=F32)
                ds = p * (dp - dl_v[h:h + 1, :])
                dk_h.append(jnp.dot(ds.astype(BF16), qp, preferred_element_type=F32))
            dks.append(jnp.where(first, dk_h[0], dk_h[1]) * (HEAD_DIM ** -0.5))
            dvs.append(jnp.where(first, dv_h[0], dv_h[1]))
        dk_ref[...] = jnp.concatenate(dks, axis=1).astype(dk_ref.dtype)
        dv_ref[...] = jnp.concatenate(dvs, axis=1).astype(dv_ref.dtype)

    row = pl.BlockSpec((128, dm), lambda t: (t, 0))
    stat = [pl.BlockSpec((N_HEADS, 128), lambda t: (0, jnp.maximum(t - 1, 0))), pl.BlockSpec((N_HEADS, 128), lambda t: (0, t)),
            pl.BlockSpec((N_HEADS, 128), lambda t: (0, jnp.minimum(t + 1, nt - 1)))]
    in_specs = ([pl.BlockSpec((128, dm), lambda t: (t, 1)), pl.BlockSpec((128, dm), lambda t: (t, v_col))]
                + _win_specs(dm, 0, nt) + _win_specs(dm, 0, nt) + stat + stat)
    return pl.pallas_call(
        body, grid=(nt,), in_specs=in_specs, out_specs=[row, row], out_shape=[SDS((s, dm), BF16), SDS((s, dm), BF16)],
        name=name, compiler_params=_cp("parallel"))(qk, v_src, qk, qk, qk, do_src, do_src, do_src, lse_t, lse_t, lse_t, delta_t, delta_t, delta_t)


CONV_COLS = (1024, 1408, 512, 256)


def _halo_specs(tm, tc, col0, nrow_blocks):
    r = tm // 16
    return [pl.BlockSpec((16, tc), lambda i, j: (jnp.maximum(i * r - 1, 0), col0 + j)),
            pl.BlockSpec((16, tc), lambda i, j: (jnp.minimum((i + 1) * r, nrow_blocks * r - 1), col0 + j))]


def _shifted(x_ref, hp_ref, hn_ref, i, last):
    x = x_ref[...].astype(F32)
    tm = x.shape[0]
    rows = lax.broadcasted_iota(jnp.int32, x.shape, 0)
    prev_row = jnp.where(i > 0, hp_ref[15:16, :].astype(F32), 0.0)
    next_row = jnp.where(i < last, hn_ref[0:1, :].astype(F32), 0.0)
    xp = jnp.where(rows == 0, prev_row, pltpu.roll(x, 1, 0))
    xn = jnp.where(rows == tm - 1, next_row, pltpu.roll(x, tm - 1, 0))
    return xp, x, xn


def _conv(x_src, col0, width, w3, bias, act, name, out_dtype=BF16, slab=None, transpose=False, wcol0=0):
    s = x_src.shape[0]
    tm = _pick(s, (256, 128))
    tc = _pick(width, CONV_COLS)
    nb = s // tm
    c0 = col0 // tc
    wc0 = wcol0 // tc

    def body(*refs):
        x_ref, hp_ref, hn_ref, w_ref, b_ref = refs[:5]
        o_ref = refs[-1]
        i = pl.program_id(0)
        xp, x, xn = _shifted(x_ref, hp_ref, hn_ref, i, nb - 1)
        w = w_ref[...]
        if transpose:
            y = w[2:3, :] * xp + w[1:2, :] * x + w[0:1, :] * xn
        else:
            y = w[0:1, :] * xp + w[1:2, :] * x + w[2:3, :] * xn + b_ref[...]
        if act:
            y = y * _sigmoid(y)
        o_ref[...] = y.astype(o_ref.dtype)

    in_specs = ([pl.BlockSpec((tm, tc), lambda i, j: (i, c0 + j))] + _halo_specs(tm, tc, c0, nb)
                + [pl.BlockSpec((3, tc), lambda i, j: (0, wc0 + j)), pl.BlockSpec((1, tc), lambda i, j: (0, wc0 + j))])
    out_shape, ocol, more, more_specs, alias = _slab(s, width, out_dtype, slab, 5)
    ob = ocol // tc
    return pl.pallas_call(
        body, grid=(nb, width // tc), in_specs=in_specs + more_specs, out_specs=pl.BlockSpec((tm, tc), lambda i, j: (i, ob + j)),
        out_shape=out_shape, input_output_aliases=alias, name=name, compiler_params=_cp("parallel", "parallel"))(
            x_src, x_src, x_src, w3, bias, *more)


def _conv_silu_bwd(x_src, col0, width, w3, bias, addends, add_widths, name):
    s = x_src.shape[0]
    tm = _pick(s, (256, 128))
    tc = _pick(width, CONV_COLS)
    nb = s // tm
    c0 = col0 // tc
    na = len(addends)

    def body(*refs):
        x_ref, hp_ref, hn_ref, w_ref, b_ref = refs[:5]
        a_refs = refs[5:5 + na]
        dp_ref, dw_ref, db_ref = refs[5 + na:]
        i, j = pl.program_id(1), pl.program_id(0)
        xp, x, xn = _shifted(x_ref, hp_ref, hn_ref, i, nb - 1)
        w = w_ref[...]
        pre = w[0:1, :] * xp + w[1:2, :] * x + w[2:3, :] * xn + b_ref[...]
        g = jnp.zeros_like(pre)
        for a_ref, aw in zip(a_refs, add_widths):
            av = a_ref[...].astype(F32)
            g = g + (av if aw == width else jnp.where(j < aw // tc, av, 0.0))
        sg = _sigmoid(pre)
        dpre = g * (sg * (1.0 + pre * (1.0 - sg)))
        dp_ref[...] = dpre.astype(dp_ref.dtype)

        @pl.when(i == 0)
        def _():
            dw_ref[...] = jnp.zeros_like(dw_ref)
            db_ref[...] = jnp.zeros_like(db_ref)

        dw_ref[...] += jnp.concatenate([jnp.sum(dpre * xp, axis=0, keepdims=True), jnp.sum(dpre * x, axis=0, keepdims=True),
                                        jnp.sum(dpre * xn, axis=0, keepdims=True)], axis=0)
        db_ref[...] += jnp.sum(dpre, axis=0, keepdims=True)

    r = tm // 16
    in_specs = [pl.BlockSpec((tm, tc), lambda j, i: (i, c0 + j)),
                pl.BlockSpec((16, tc), lambda j, i: (jnp.maximum(i * r - 1, 0), c0 + j)),
                pl.BlockSpec((16, tc), lambda j, i: (jnp.minimum((i + 1) * r, nb * r - 1), c0 + j)),
                pl.BlockSpec((3, tc), lambda j, i: (0, j)), pl.BlockSpec((1, tc), lambda j, i: (0, j))]
    for aw in add_widths:
        nblk = aw // tc
        in_specs.append(pl.BlockSpec((tm, tc), lambda j, i, nblk=nblk: (i, jnp.minimum(j, nblk - 1))))
    return pl.pallas_call(
        body, grid=(width // tc, nb), in_specs=in_specs,
        out_specs=[pl.BlockSpec((tm, tc), lambda j, i: (i, j)), pl.BlockSpec((3, tc), lambda j, i: (0, j)), pl.BlockSpec((1, tc), lambda j, i: (0, j))],
        out_shape=[SDS((s, width), BF16), SDS((3, width), F32), SDS((1, width), F32)],
        name=name, compiler_params=_cp("parallel", "arbitrary"))(x_src, x_src, x_src, w3, bias, *addends)


def _ffn_gate_fwd(u, w3, bias, name):
    s = u.shape[0]
    tm = _pick(s, (256, 128))
    tc = _pick(D_FF, CONV_COLS)
    nb = s // tm
    nj = D_FF // tc

    def body(g_ref, gp, gn, u_ref, up, un, wg_ref, wu_ref, bg_ref, bu_ref, o_ref, ot_ref):
        i = pl.program_id(0)
        outs = []
        for (x_ref, hp, hn, w_ref, b_ref) in ((g_ref, gp, gn, wg_ref, bg_ref), (u_ref, up, un, wu_ref, bu_ref)):
            xp, x, xn = _shifted(x_ref, hp, hn, i, nb - 1)
            w = w_ref[...]
            outs.append(w[0:1, :] * xp + w[1:2, :] * x + w[2:3, :] * xn + b_ref[...])
        gate, upv = outs
        out = (gate * _sigmoid(gate) * upv).astype(o_ref.dtype)
        o_ref[...] = out
        ot_ref[...] = out.T

    def xspecs(c0):
        return [pl.BlockSpec((tm, tc), lambda i, j: (i, c0 + j))] + _halo_specs(tm, tc, c0, nb)

    in_specs = (xspecs(0) + xspecs(nj)
                + [pl.BlockSpec((3, tc), lambda i, j: (0, j)), pl.BlockSpec((3, tc), lambda i, j: (0, nj + j)),
                   pl.BlockSpec((1, tc), lambda i, j: (0, j)), pl.BlockSpec((1, tc), lambda i, j: (0, nj + j))])
    return pl.pallas_call(
        body, grid=(nb, nj), in_specs=in_specs,
        out_specs=[pl.BlockSpec((tm, tc), lambda i, j: (i, j)), pl.BlockSpec((tc, tm), lambda i, j: (j, i))],
        out_shape=[SDS((s, D_FF), BF16), SDS((D_FF, s), BF16)], name=name, compiler_params=_cp("parallel", "parallel"))(
            u, u, u, u, u, u, w3, w3, bias, bias)


def _ffn_gate_bwd(u, w3, bias, dact, name):
    s = u.shape[0]
    tm = _pick(s, (256, 128))
    tc = _pick(D_FF, CONV_COLS)
    nb = s // tm
    nj = D_FF // tc

    def body(g_ref, gp, gn, u_ref, up, un, wg_ref, wu_ref, bg_ref, bu_ref, da_ref, dg_ref, du_ref, dwg_ref, dwu_ref, dbg_ref, dbu_ref):
        i = pl.program_id(1)
        sh, pre = [], []
        for (x_ref, hp, hn, w_ref, b_ref) in ((g_ref, gp, gn, wg_ref, bg_ref), (u_ref, up, un, wu_ref, bu_ref)):
            xs3 = _shifted(x_ref, hp, hn, i, nb - 1)
            w = w_ref[...]
            sh.append(xs3)
            pre.append(w[0:1, :] * xs3[0] + w[1:2, :] * xs3[1] + w[2:3, :] * xs3[2] + b_ref[...])
        gate, upv = pre
        da = da_ref[...].astype(F32)
        sg = _sigmoid(gate)
        dgate = da * upv * (sg * (1.0 + gate * (1.0 - sg)))
        dup = da * gate * sg
        dg_ref[...] = dgate.astype(dg_ref.dtype)
        du_ref[...] = dup.astype(du_ref.dtype)

        @pl.when(i == 0)
        def _():
            for r in (dwg_ref, dwu_ref, dbg_ref, dbu_ref):
                r[...] = jnp.zeros_like(r)

        for d, xs3, dw_ref, db_ref in ((dgate, sh[0], dwg_ref, dbg_ref), (dup, sh[1], dwu_ref, dbu_ref)):
            dw_ref[...] += jnp.concatenate([jnp.sum(d * xs3[0], axis=0, keepdims=True), jnp.sum(d * xs3[1], axis=0, keepdims=True),
                                            jnp.sum(d * xs3[2], axis=0, keepdims=True)], axis=0)
            db_ref[...] += jnp.sum(d, axis=0, keepdims=True)

    r = tm // 16

    def xspecs(c0):
        return [pl.BlockSpec((tm, tc), lambda j, i: (i, c0 + j)),
                pl.BlockSpec((16, tc), lambda j, i: (jnp.maximum(i * r - 1, 0), c0 + j)),
                pl.BlockSpec((16, tc), lambda j, i: (jnp.minimum((i + 1) * r, nb * r - 1), c0 + j))]

    in_specs = (xspecs(0) + xspecs(nj)
                + [pl.BlockSpec((3, tc), lambda j, i: (0, j)), pl.BlockSpec((3, tc), lambda j, i: (0, nj + j)),
                   pl.BlockSpec((1, tc), lambda j, i: (0, j)), pl.BlockSpec((1, tc), lambda j, i: (0, nj + j)),
                   pl.BlockSpec((tm, tc), lambda j, i: (i, j))])
    blk = pl.BlockSpec((tm, tc), lambda j, i: (i, j))
    w_o = pl.BlockSpec((3, tc), lambda j, i: (0, j))
    b_o = pl.BlockSpec((1, tc), lambda j, i: (0, j))
    return pl.pallas_call(
        body, grid=(nj, nb), in_specs=in_specs, out_specs=[blk, blk, w_o, w_o, b_o, b_o],
        out_shape=[SDS((s, D_FF), BF16), SDS((s, D_FF), BF16), SDS((3, D_FF), F32), SDS((3, D_FF), F32), SDS((1, D_FF), F32), SDS((1, D_FF), F32)],
        name=name, compiler_params=_cp("parallel", "arbitrary"))(u, u, u, u, u, u, w3, w3, bias, bias, dact)


def _exchange_sems(n):
    return [pltpu.SemaphoreType.DMA((7 * n,)), pltpu.SemaphoreType.DMA((7 * n,)), pltpu.SemaphoreType.DMA((n,))]


def _exchange_copies(srcs, outs, send_sems, recv_sems, local_sems):
    x, y, c = lax.axis_index("x"), lax.axis_index("y"), lax.axis_index("c")
    me = 4 * x + 2 * y + c
    locals_ = [pltpu.make_async_copy(srcs[a].at[me], outs[a].at[me], local_sems.at[a]) for a in range(len(srcs))]
    sends, recvs = [], []
    for k in range(1, N_DEV):
        px, py, pc = x ^ ((k >> 2) & 1), y ^ ((k >> 1) & 1), c ^ (k & 1)
        peer = 4 * px + 2 * py + pc
        for a in range(len(srcs)):
            sems = dict(send_sem=send_sems.at[a * 7 + k - 1], recv_sem=recv_sems.at[a * 7 + k - 1], device_id_type=MESH)
            sends.append(pltpu.make_async_remote_copy(src_ref=srcs[a].at[peer], dst_ref=outs[a].at[me], device_id=(px, py, pc), **sems))
            recvs.append(pltpu.make_async_remote_copy(src_ref=srcs[a].at[peer], dst_ref=outs[a].at[peer], device_id=(x, y, c), **sems))
    return locals_, sends, recvs


def _exchange_start(srcs, outs, send_sems, recv_sems, local_sems):
    locals_, sends, _ = _exchange_copies(srcs, outs, send_sems, recv_sems, local_sems)
    for cp in locals_ + sends:
        cp.start()


def _exchange_wait(srcs, outs, send_sems, recv_sems, local_sems):
    locals_, sends, recvs = _exchange_copies(srcs, outs, send_sems, recv_sems, local_sems)
    for cp in recvs:
        cp.wait_recv()
    for cp in sends:
        cp.wait_send()
    for cp in locals_:
        cp.wait()


def _gather_copies(srcs, outs, send_sems, recv_sems, local_sems):
    x, y, c = lax.axis_index("x"), lax.axis_index("y"), lax.axis_index("c")
    me, sibling = (x, y, c), (x, y, 1 - c)
    chips = [(1 - x, y), (x, 1 - y), (1 - x, 1 - y)]

    def copy(a, k, block, to, src=None):
        dst = outs[a].at[4 * block[0] + 2 * block[1] + block[2]]
        return pltpu.make_async_remote_copy(
            src_ref=dst if src is None else src, dst_ref=dst,
            send_sem=send_sems.at[a * 7 + k], recv_sem=recv_sems.at[a * 7 + k], device_id=to, device_id_type=MESH)

    n = len(srcs)
    locals_ = [pltpu.make_async_copy(srcs[a], outs[a].at[4 * x + 2 * y + c], local_sems.at[a]) for a in range(n)]
    own = [copy(a, 0, me, sibling, src=srcs[a]) for a in range(n)]
    own += [copy(a, 1 + j, me, (*chip, c), src=srcs[a]) for a in range(n) for j, chip in enumerate(chips)]
    landed_ici = [copy(a, 1 + j, (*chip, c), me) for j, chip in enumerate(chips) for a in range(n)]
    passed = [copy(a, 4 + j, (*chip, c), sibling) for j, chip in enumerate(chips) for a in range(n)]
    landed_d2d = [copy(a, 0, sibling, me) for a in range(n)]
    landed_d2d += [copy(a, 4 + j, (*chip, 1 - c), me) for a in range(n) for j, chip in enumerate(chips)]
    return locals_, own, landed_ici, passed, landed_d2d


def _gather_start(*refs):
    locals_, own, _, _, _ = _gather_copies(*refs)
    for cp in locals_ + own:
        cp.start()


def _gather_forward(*refs):
    _, _, landed_ici, passed, _ = _gather_copies(*refs)
    for arrived, onward in zip(landed_ici, passed):
        arrived.wait_recv()
        onward.start()


def _gather_finish(*refs):
    locals_, own, _, passed, landed_d2d = _gather_copies(*refs)
    for cp in landed_d2d:
        cp.wait_recv()
    for cp in own + passed:
        cp.wait_send()
    for cp in locals_:
        cp.wait()


def _ssd_common(dt_ref, dtt_ref, al_r, al_c, bi_r, bi_c, off, rev):
    li = lax.broadcasted_iota(jnp.int32, (CHUNK, CHUNK), 0)
    si = lax.broadcasted_iota(jnp.int32, (CHUNK, CHUNK), 1)
    mask = (li <= si) if rev else (li >= si)
    mask_t = (li >= si) if rev else (li <= si)
    a_r = -jnp.exp(al_r[...])
    a_c = -jnp.exp(al_c[...])
    pre = dt_ref[:, off:off + N_HEADS] + bi_r[...]
    dt = _softplus(pre)
    cs = jnp.dot(mask.astype(F32), dt * a_r, precision=HIGH, preferred_element_type=F32)
    dt_t = _softplus(dtt_ref[off:off + N_HEADS, :] + bi_c[...])
    cs_t = jnp.dot(dt_t * a_c, mask_t.astype(F32), precision=HIGH, preferred_element_type=F32)
    tot = cs[0:1, :] if rev else cs[CHUNK - 1:CHUNK, :]
    return mask, mask_t, a_r, pre, dt, cs, cs_t, tot


def _ssd_fwd(xbc, dt_raw, dt_t, args_f, args_b, name, gather=()):
    s = xbc.shape[0]
    nc = s // CHUNK
    hp = D_INNER // N_HEADS
    hpg = N_HEADS // N_GROUPS
    gs = N_GROUPS * D_STATE
    ng = len(gather)
    fwd_step = (nc * 11) // 16

    def chunk(x_ref, b_ref, c_ref, dt_ref, dtt_ref, alr, alc, bir, bic, y_ref, st_ref, h_scr, off, rev):
        mask, _, _, _, dt, cs, cs_t, tot = _ssd_common(dt_ref, dtt_ref, alr, alc, bir, bic, off, rev)
        xs = x_ref[...]
        ys = []
        for g in range(N_GROUPS):
            bg = b_ref[:, g * D_STATE:(g + 1) * D_STATE]
            cg = c_ref[:, g * D_STATE:(g + 1) * D_STATE]
            gm = lax.dot_general(cg, bg, NT, preferred_element_type=F32)
            hcat = h_scr[g]
            st_ref[0, g] = hcat
            ch = lax.dot_general(cg, hcat.astype(BF16), NT, preferred_element_type=F32)
            xdd = []
            for r in range(hpg):
                h = g * hpg + r
                csc, csr, tot_h = cs[:, h:h + 1], cs_t[h:h + 1, :], tot[:, h:h + 1]
                lm = jnp.exp(jnp.where(mask, csc - csr, NEG))
                xdf = xs[:, h * hp:(h + 1) * hp].astype(F32) * dt[:, h:h + 1]
                y = jnp.dot((gm * lm).astype(BF16), xdf.astype(BF16), preferred_element_type=F32)
                ys.append(y + jnp.exp(csc) * ch[:, r * hp:(r + 1) * hp])
                xdd.append((xdf * jnp.exp(tot_h - csc)).astype(BF16))
            snew = lax.dot_general(jnp.concatenate(xdd, axis=1), bg, TN, preferred_element_type=F32)
            for r in range(hpg):
                rs = slice(r * hp, (r + 1) * hp)
                h_scr[g, rs, :] = jnp.exp(tot[:, g * hpg + r:g * hpg + r + 1]) * hcat[rs] + snew[rs]
        y_ref[...] = jnp.concatenate(ys, axis=1)

    def body(*refs):
        in_f, in_b = refs[0:9], refs[9:18]
        g_src = refs[18:18 + ng]
        out_f, out_b = refs[18 + ng:20 + ng], refs[20 + ng:22 + ng]
        g_dst = refs[22 + ng:22 + 2 * ng]
        hs_f, hs_b = refs[22 + 2 * ng], refs[23 + 2 * ng]
        g_sems = refs[24 + 2 * ng:]
        step = pl.program_id(0)

        @pl.when(step == 0)
        def _():
            hs_f[...] = jnp.zeros_like(hs_f)
            hs_b[...] = jnp.zeros_like(hs_b)
            if ng:
                _gather_start(g_src, g_dst, *g_sems)

        chunk(*in_f, *out_f, hs_f, 0, False)
        chunk(*in_b, *out_b, hs_b, N_HEADS, True)

        if ng:
            @pl.when(step == fwd_step)
            def _():
                _gather_forward(g_src, g_dst, *g_sems)

            @pl.when(step == nc - 1)
            def _():
                _gather_finish(g_src, g_dst, *g_sems)

    small = lambda shape: pl.BlockSpec(shape, lambda c: (0, 0))

    def specs(cm):
        ins = [pl.BlockSpec((CHUNK, D_INNER), lambda c: (cm(c), 0)),
               pl.BlockSpec((CHUNK, gs), lambda c: (cm(c), D_INNER // gs)),
               pl.BlockSpec((CHUNK, gs), lambda c: (cm(c), D_INNER // gs + 1)),
               pl.BlockSpec((CHUNK, 128), lambda c: (cm(c), 0)),
               pl.BlockSpec((2 * N_HEADS, CHUNK), lambda c: (0, cm(c))),
               small((1, N_HEADS)), small((N_HEADS, 1)), small((1, N_HEADS)), small((N_HEADS, 1))]
        outs = [pl.BlockSpec((CHUNK, D_INNER), lambda c: (cm(c), 0)),
                pl.BlockSpec((1, N_GROUPS, hpg * hp, D_STATE), lambda c: (cm(c), 0, 0, 0))]
        return ins, outs

    ins_f, outs_f = specs(lambda c: c)
    ins_b, outs_b = specs(lambda c: nc - 1 - c)
    any_spec = pl.BlockSpec(memory_space=pl.ANY)
    one_dir = [SDS((s, D_INNER), F32), SDS((nc, N_GROUPS, hpg * hp, D_STATE), F32)]
    state = pltpu.VMEM((N_GROUPS, hpg * hp, D_STATE), F32)
    return pl.pallas_call(
        body, grid=(nc,), in_specs=ins_f + ins_b + [any_spec] * ng, out_specs=outs_f + outs_b + [any_spec] * ng,
        out_shape=one_dir + one_dir + [SDS((N_DEV,) + g.shape, g.dtype) for g in gather],
        scratch_shapes=[state, state] + (_exchange_sems(ng) if ng else []), name=name, compiler_params=_cp("arbitrary"))(
            xbc, xbc, xbc, dt_raw, dt_t, *args_f, xbc, xbc, xbc, dt_raw, dt_t, *args_b, *gather)


def _ssd_bwd(xbc, dt_raw, dt_t, al_r, al_c, bi_r, bi_c, states, dy, off, rev, name, exchange=()):
    s = xbc.shape[0]
    nc = s // CHUNK
    hp = D_INNER // N_HEADS
    gs = N_GROUPS * D_STATE
    hpg = N_HEADS // N_GROUPS
    cm = (lambda c: c) if rev else (lambda c: nc - 1 - c)
    nx = len(exchange)

    def body(*refs):
        x_ref, b_ref, c_ref, dt_ref, dtt_ref, alr, alc, bir, bic, st_ref, dy_ref = refs[:11]
        xch_src = refs[11:11 + nx]
        dx_ref, ddt_ref, dal_ref, dbi_ref = refs[11 + nx:15 + nx]
        xch_dst = refs[15 + nx:15 + 2 * nx]
        dh_scr = refs[15 + 2 * nx]
        xch_sems = refs[16 + 2 * nx:]

        @pl.when(pl.program_id(0) == 0)
        def _():
            dh_scr[...] = jnp.zeros_like(dh_scr)
            dal_ref[...] = jnp.zeros_like(dal_ref)
            dbi_ref[...] = jnp.zeros_like(dbi_ref)
            if nx:
                _exchange_start(xch_src, xch_dst, *xch_sems)

        if nx:
            @pl.when(pl.program_id(0) == nc - 1)
            def _():
                _exchange_wait(xch_src, xch_dst, *xch_sems)

        mask, mask_t, a_r, pre, dt, cs, cs_t, tot = _ssd_common(dt_ref, dtt_ref, alr, alc, bir, bic, off, rev)
        xs = x_ref[...]
        dyv = dy_ref[...]
        rows = lax.broadcasted_iota(jnp.int32, (CHUNK, 1), 0)
        end_row = (rows == 0) if rev else (rows == CHUNK - 1)
        lane_h = lax.broadcasted_iota(jnp.int32, (1, N_HEADS), 1)
        sub_h = lax.broadcasted_iota(jnp.int32, (N_HEADS, 1), 0)
        dcs_all = jnp.zeros((CHUNK, N_HEADS), F32)
        colw_all = jnp.zeros((N_HEADS, CHUNK), F32)
        dxsum_all = jnp.zeros((CHUNK, N_HEADS), F32)
        dxs, dbs, dcs_out = [], [], []
        for g in range(N_GROUPS):
            bg = b_ref[:, g * D_STATE:(g + 1) * D_STATE]
            cg = c_ref[:, g * D_STATE:(g + 1) * D_STATE]
            gm = lax.dot_general(cg, bg, NT, preferred_element_type=F32)
            hcat = st_ref[0, g]
            dhcat = dh_scr[g]
            hb, dhb = hcat.astype(BF16), dhcat.astype(BF16)
            ch = lax.dot_general(cg, hb, NT, preferred_element_type=F32)
            z = lax.dot_general(bg, dhb, NT, preferred_element_type=F32)
            dg_sum = jnp.zeros((CHUNK, CHUNK), F32)
            dchs, xdds, t_hs = [], [], []
            for r in range(hpg):
                h = g * hpg + r
                rs = slice(r * hp, (r + 1) * hp)
                csc, csr, tot_h = cs[:, h:h + 1], cs_t[h:h + 1, :], tot[:, h:h + 1]
                lm = jnp.exp(jnp.where(mask, csc - csr, NEG))
                xh = xs[:, h * hp:(h + 1) * hp].astype(F32)
                dtc = dt[:, h:h + 1]
                xdf = xh * dtc
                xd = xdf.astype(BF16)
                dyh = dyv[:, h * hp:(h + 1) * hp]
                dyb = dyh.astype(BF16)
                m = gm * lm
                ecs = jnp.exp(csc)
                dec = jnp.exp(tot_h - csc)
                t_h = jnp.exp(tot_h)
                dxd = lax.dot_general(m.astype(BF16), dyb, TN, preferred_element_type=F32)
                dm = lax.dot_general(dyb, xd, NT, preferred_element_type=F32)
                dg_sum = dg_sum + dm * lm
                w = dm * m
                dcs = jnp.sum(dyh * (ecs * ch[:, rs]), axis=1, keepdims=True)
                dchs.append((ecs * dyh).astype(BF16))
                dxd = dxd + dec * z[:, rs]
                ddec = jnp.sum(xdf * z[:, rs], axis=1, keepdims=True) * dec
                xdds.append((xdf * dec).astype(BF16))
                dtot = (jnp.sum(jnp.sum(dhcat[rs] * hcat[rs], axis=1, keepdims=True), axis=0, keepdims=True) * t_h
                        + jnp.sum(ddec, axis=0, keepdims=True))
                t_hs.append(t_h)
                dcs = dcs + jnp.sum(w, axis=1, keepdims=True) - ddec
                dcs = dcs + jnp.where(end_row, dtot, 0.0)
                colw_all = colw_all + (sub_h == h).astype(F32) * jnp.sum(w, axis=0, keepdims=True)
                onehot = (lane_h == h).astype(F32)
                dcs_all = dcs_all + dcs * onehot
                dxsum_all = dxsum_all + jnp.sum(dxd * xh, axis=1, keepdims=True) * onehot
                dxs.append(dxd * dtc)
            dgs = dg_sum.astype(BF16)
            dchc = jnp.concatenate(dchs, axis=1)
            dcs_out.append(jnp.dot(dgs, bg, preferred_element_type=F32) + jnp.dot(dchc, hb, preferred_element_type=F32))
            dbs.append(lax.dot_general(dgs, cg, TN, preferred_element_type=F32)
                       + jnp.dot(jnp.concatenate(xdds, axis=1), dhb, preferred_element_type=F32))
            dh_in = lax.dot_general(dchc, cg, TN, preferred_element_type=F32)
            for r in range(hpg):
                rs = slice(r * hp, (r + 1) * hp)
                dh_scr[g, rs, :] = dh_in[rs] + t_hs[r] * dhcat[rs]
        dx_ref[...] = jnp.concatenate(dxs + dbs + dcs_out, axis=1)
        mt = mask_t.astype(F32)
        da = (jnp.dot(mt, dcs_all, precision=HIGH, preferred_element_type=F32)
              - lax.dot_general(mt, colw_all, NT, precision=HIGH, preferred_element_type=F32))
        dal_ref[...] += jnp.sum(da * dt, axis=0, keepdims=True) * a_r
        ddt_raw = (da * a_r + dxsum_all) * _sigmoid(pre)
        ddt_ref[...] = ddt_raw
        dbi_ref[...] += jnp.sum(ddt_raw, axis=0, keepdims=True)

    small = lambda shape: pl.BlockSpec(shape, lambda c: (0, 0))
    in_specs = [pl.BlockSpec((CHUNK, D_INNER), lambda c: (cm(c), 0)),
                pl.BlockSpec((CHUNK, gs), lambda c: (cm(c), D_INNER // gs)),
                pl.BlockSpec((CHUNK, gs), lambda c: (cm(c), D_INNER // gs + 1)),
                pl.BlockSpec((CHUNK, 128), lambda c: (cm(c), 0)),
                pl.BlockSpec((2 * N_HEADS, CHUNK), lambda c: (0, cm(c))),
                small((1, N_HEADS)), small((N_HEADS, 1)), small((1, N_HEADS)), small((N_HEADS, 1)),
                pl.BlockSpec((1, N_GROUPS, hpg * hp, D_STATE), lambda c: (cm(c), 0, 0, 0)),
                pl.BlockSpec((CHUNK, D_INNER), lambda c: (cm(c), 0))]
    any_spec = pl.BlockSpec(memory_space=pl.ANY)
    return pl.pallas_call(
        body, grid=(nc,), in_specs=in_specs + [any_spec] * nx,
        out_specs=[pl.BlockSpec((CHUNK, XBC), lambda c: (cm(c), 0)), pl.BlockSpec((CHUNK, N_HEADS), lambda c: (cm(c), 0)),
                   small((1, N_HEADS)), small((1, N_HEADS))] + [any_spec] * nx,
        out_shape=[SDS((s, XBC), F32), SDS((s, N_HEADS), F32), SDS((1, N_HEADS), F32), SDS((1, N_HEADS), F32)]
        + [SDS(a.shape, a.dtype) for a in exchange],
        scratch_shapes=[pltpu.VMEM((N_GROUPS, hpg * hp, D_STATE), F32)] + (_exchange_sems(nx) if nx else []),
        name=name, compiler_params=_cp("arbitrary"))(xbc, xbc, xbc, dt_raw, dt_t, al_r, al_c, bi_r, bi_c, states, dy, *exchange)


def _gate_fwd(yf, yb, xbc, proj, dskip_x, norm_w, mix, mix_t, name):
    s = yf.shape[0]
    tm = _pick(s, (256, 128))
    gw = D_INNER // N_GROUPS
    zc = 3 * D_MODEL // D_INNER

    def body(yf_ref, yb_ref, x_ref, z_ref, d_ref, w_ref, _m, _mt, o_ref, ot_ref):
        y = yf_ref[...] + yb_ref[...] + d_ref[...] * x_ref[...].astype(F32)
        z = z_ref[...].astype(F32)
        gt = y * (z * _sigmoid(z))
        outs = []
        for g in range(N_GROUPS):
            gg = gt[:, g * gw:(g + 1) * gw]
            outs.append(gg * lax.rsqrt(jnp.mean(gg * gg, axis=-1, keepdims=True) + EPS))
        out = (jnp.concatenate(outs, axis=1) * w_ref[...]).astype(o_ref.dtype)
        o_ref[...] = out
        ot_ref[...] = out.T

    row = pl.BlockSpec((tm, D_INNER), lambda i: (i, 0))
    vec = pl.BlockSpec((1, D_INNER), lambda i: (0, 0))
    any_spec = pl.BlockSpec(memory_space=pl.ANY)
    cb = (mix.shape[1] - D_INNER) // D_INNER
    return pl.pallas_call(
        body, grid=(s // tm,), in_specs=[row, row, row, pl.BlockSpec((tm, D_INNER), lambda i: (i, zc)), vec, vec, any_spec, any_spec],
        out_specs=[pl.BlockSpec((tm, D_INNER), lambda i: (i, cb)), pl.BlockSpec((D_INNER, tm), lambda i: (cb, i))],
        out_shape=[SDS(mix.shape, BF16), SDS(mix_t.shape, BF16)], input_output_aliases={6: 0, 7: 1},
        name=name, compiler_params=_cp("parallel"))(yf, yb, xbc, proj, dskip_x, norm_w, mix, mix_t)


def _gate_bwd(yf, yb, xbc, proj, dskip_x, norm_w, dmix, dproj, name):
    s = yf.shape[0]
    tm = _pick(s, (256, 128))
    gw = D_INNER // N_GROUPS
    zc = 3 * D_MODEL // D_INNER

    def body(yf_ref, yb_ref, x_ref, z_ref, d_ref, w_ref, do_ref, _, dy_ref, dz_ref, dxs_ref, dw_ref, dd_ref):
        xf = x_ref[...].astype(F32)
        y = yf_ref[...] + yb_ref[...] + d_ref[...] * xf
        z = z_ref[...].astype(F32)
        sg = _sigmoid(z)
        sz = z * sg
        gt = y * sz
        do = do_ref[...].astype(F32)
        dgh = do * w_ref[...]
        ghs, dgts = [], []
        for g in range(N_GROUPS):
            gg = gt[:, g * gw:(g + 1) * gw]
            r = lax.rsqrt(jnp.mean(gg * gg, axis=-1, keepdims=True) + EPS)
            gh = gg * r
            dg = dgh[:, g * gw:(g + 1) * gw]
            ghs.append(gh)
            dgts.append(r * (dg - gh * jnp.mean(dg * gh, axis=-1, keepdims=True)))
        ghat = jnp.concatenate(ghs, axis=1)
        dgt = jnp.concatenate(dgts, axis=1)
        dy = dgt * sz
        dy_ref[...] = dy
        dz_ref[...] = (dgt * y * (sg * (1.0 + z * (1.0 - sg)))).astype(dz_ref.dtype)
        dxs_ref[...] = dy * d_ref[...]

        @pl.when(pl.program_id(0) == 0)
        def _():
            dw_ref[...] = jnp.zeros_like(dw_ref)
            dd_ref[...] = jnp.zeros_like(dd_ref)

        dw_ref[...] += jnp.sum(do * ghat, axis=0, keepdims=True)
        dd_ref[...] += jnp.sum(dy * xf, axis=0, keepdims=True)

    row = pl.BlockSpec((tm, D_INNER), lambda i: (i, 0))
    vec = pl.BlockSpec((1, D_INNER), lambda i: (0, 0))
    dz_shape, _, more, more_specs, alias = _slab(s, D_INNER, BF16, (dproj.shape[1], zc * D_INNER, dproj), 7, out_idx=1)
    return pl.pallas_call(
        body, grid=(s // tm,),
        in_specs=[row, row, row, pl.BlockSpec((tm, D_INNER), lambda i: (i, zc)), vec, vec, pl.BlockSpec((tm, D_INNER), lambda i: (i, 1))] + more_specs,
        out_specs=[row, pl.BlockSpec((tm, D_INNER), lambda i: (i, zc)), row, vec, vec],
        out_shape=[SDS((s, D_INNER), F32), dz_shape, SDS((s, D_INNER), F32), SDS((1, D_INNER), F32), SDS((1, D_INNER), F32)],
        input_output_aliases=alias, name=name, compiler_params=_cp("arbitrary"))(yf, yb, xbc, proj, dskip_x, norm_w, dmix, *more)


def _adamw(parts, w, m, v, name):
    r, c = w.shape
    tr = _pick(r, (256, 352, 128))

    def body(p_ref, w_ref, m_ref, v_ref, g_ref, d_ref, nm_ref, nv_ref):
        g = p_ref[0].astype(F32)
        for i in range(1, N_DEV):
            g = g + p_ref[i].astype(F32)
        mn = B1 * m_ref[...] + (1.0 - B1) * g
        vn = B2 * v_ref[...] + (1.0 - B2) * (g * g)
        m_hat = mn / (1.0 - B1 ** STEP)
        v_hat = vn / (1.0 - B2 ** STEP)
        g_ref[...] = g
        d_ref[...] = -LR * (m_hat / (jnp.sqrt(v_hat) + AEPS) + WD * w_ref[...])
        nm_ref[...] = mn
        nv_ref[...] = vn

    blk = pl.BlockSpec((tr, c), lambda i: (i, 0))
    return pl.pallas_call(
        body, grid=(r // tr,), in_specs=[pl.BlockSpec((N_DEV, tr, c), lambda i: (0, i, 0)), blk, blk, blk],
        out_specs=[blk, blk, blk, blk], out_shape=[SDS((r, c), F32)] * 4, name=name, compiler_params=_cp("parallel"))(parts, w, m, v)


def _sum_parts(parts, name):
    _, r, c = parts.shape

    def body(p_ref, o_ref):
        g = p_ref[0]
        for i in range(1, N_DEV):
            g = g + p_ref[i]
        o_ref[...] = g

    return pl.pallas_call(body, out_shape=SDS((r, c), F32), name=name)(parts)


def _adamw_small(g, w, m, v, name):
    def body(g_ref, w_ref, m_ref, v_ref, d_ref, nm_ref, nv_ref):
        gv = g_ref[...]
        mn = B1 * m_ref[...] + (1.0 - B1) * gv
        vn = B2 * v_ref[...] + (1.0 - B2) * (gv * gv)
        m_hat = mn / (1.0 - B1 ** STEP)
        v_hat = vn / (1.0 - B2 ** STEP)
        d_ref[...] = -LR * (m_hat / (jnp.sqrt(v_hat) + AEPS) + WD * w_ref[...])
        nm_ref[...] = mn
        nv_ref[...] = vn

    return pl.pallas_call(body, out_shape=[SDS(g.shape, F32)] * 3, name=name)(g, w, m, v)


def _my_index():
    return 4 * lax.axis_index("x") + 2 * lax.axis_index("y") + lax.axis_index("c")


def _all_gather(shards, name):
    n = len(shards)

    def body(*refs):
        srcs, outs = refs[:n], refs[n:2 * n]
        _gather_start(srcs, outs, *refs[2 * n:])
        _gather_forward(srcs, outs, *refs[2 * n:])
        _gather_finish(srcs, outs, *refs[2 * n:])

    any_spec = pl.BlockSpec(memory_space=pl.ANY)
    return pl.pallas_call(
        body, in_specs=[any_spec] * n, out_specs=[any_spec] * n,
        out_shape=[SDS((N_DEV,) + s.shape, s.dtype) for s in shards], scratch_shapes=_exchange_sems(n), name=name)(*shards)


def _exchange(arrays, name):
    n = len(arrays)

    def body(*refs):
        srcs, outs = refs[:n], refs[n:2 * n]
        _exchange_start(srcs, outs, *refs[2 * n:])
        _exchange_wait(srcs, outs, *refs[2 * n:])

    any_spec = pl.BlockSpec(memory_space=pl.ANY)
    return pl.pallas_call(
        body, in_specs=[any_spec] * n, out_specs=[any_spec] * n,
        out_shape=[SDS(a.shape, a.dtype) for a in arrays], scratch_shapes=_exchange_sems(n), name=name)(*arrays)


def _to_pattern(t, d):
    if d == 1:
        return t
    s, w = t.shape
    return t.reshape(s // d, d, w).transpose(1, 0, 2).reshape(s, w)


def _from_pattern(t, d):
    if d == 1:
        return t
    s, w = t.shape
    return t.reshape(d, s // d, w).transpose(1, 0, 2).reshape(s, w)


def _pad_lanes(t, n):
    return jnp.pad(t, ((0, 0), (0, n - t.shape[1])))


def _to_shards(g, axis):
    r, c = g.shape
    if axis == 0:
        return g.reshape(N_DEV, r // N_DEV, c)
    return g.reshape(r, N_DEV, c // N_DEV).transpose(1, 0, 2)


def _local_step(x, target, p, late_shards=(), early_exchange=True):
    s = x.shape[0]
    tabs_f, tabs_b = _rope_tables(s)
    expand = jnp.asarray(np.repeat(np.eye(N_HEADS, dtype=np.float32), HEAD_DIM, axis=1))
    w_main, w_dt = p["w_in"][:, :MAIN_W], _pad_lanes(p["w_in"][:, MAIN_W:], 128)
    al_r = {"f": p["a_log_f"], "b": p["a_log_b"]}
    bi_r = {"f": p["dt_bias_f"], "b": p["dt_bias_b"]}
    dskip_x = jnp.repeat(p["d_skip"], D_INNER // N_HEADS, axis=1)
    ssm_w3, ffn_w3 = p["ssm_conv_w"].T, p["ffn_conv_w"].T

    h1, h1t = _rmsnorm_fwd(x, p["norm1_w"], "norm1_fwd")
    proj = _matmul(h1, w_main, name="in_proj")
    dt_raw = _matmul(h1, w_dt, name="in_proj_dt", out_dtype=F32)
    dt_t = dt_raw[:, :2 * N_HEADS].T
    perms = {d: _perm_matrices(d) for d in DILATIONS[1:]}
    qkv = _rope_fwd(proj, tabs_f, perms, "rope_fwd")
    v_col = 2
    os_, lses = [], []
    for d, qkv_p in zip(DILATIONS, qkv):
        o_p, lse_p = _attn_fwd(qkv_p, qkv_p, v_col, s // d, f"attn_fwd_d{d}")
        os_.append(o_p)
        lses.append(_from_pattern(lse_p, d))
    mix, mix_t, lse_tot = _attn_combine(os_, lses, perms, expand, "attn_combine")

    xbc = _conv(proj, 3 * D_MODEL + D_INNER, XBC, ssm_w3, p["ssm_conv_b"], True, "ssm_conv_fwd")
    col = lambda r: r.reshape(N_HEADS, 1)
    ssd_args = {k: (al_r[k], col(al_r[k]), bi_r[k], col(bi_r[k])) for k in ("f", "b")}
    yf, st_f, yb, st_b, *got = _ssd_fwd(xbc, dt_raw, dt_t, ssd_args["f"], ssd_args["b"], "ssd_fwd", gather=late_shards)
    if late_shards:
        p = dict(p, w_out=got[0].reshape(2 * D_MODEL, D_MODEL), w_down=got[2].reshape(D_FF, D_MODEL),
                 w_up=got[1].transpose(1, 0, 2).reshape(D_MODEL, 2 * D_FF))
    mix, mix_t = _gate_fwd(yf, yb, xbc, proj, dskip_x, p["ssm_norm_w"], mix, mix_t, "ssm_gate_fwd")

    x2 =_matmul(mix, p["w_out"], name="out_proj", out_dtype=F32, residual=x)
    h2, h2t = _rmsnorm_fwd(x2, p["norm2_w"], "norm2_fwd")
    u = _matmul(h2, p["w_up"], name="ffn_up")
    act, act_t = _ffn_gate_fwd(u, ffn_w3, p["ffn_conv_b"], "ffn_gate_fwd")
    x3 = _matmul(act, p["w_down"], name="ffn_down", out_dtype=F32, residual=x2)

    dx3, dx3b, g_final, loss = _final_norm_loss(x3, p["final_norm_w"].reshape(1, D_MODEL), target, "final_norm_loss")
    g_w_down = _matmul(act_t, dx3b, name="dw_down")
    dact = _matmul(dx3b, p["w_down"], name="d_act", trans_b=True)
    dug, duu, dwg, dwu, dbg, dbu = _ffn_gate_bwd(u, ffn_w3, p["ffn_conv_b"], dact, "ffn_gate_bwd")
    du = _conv(dug, 0, D_FF, ffn_w3, p["ffn_conv_b"], False, "ffn_conv_bwd_gate", slab=(2 * D_FF, 0, None), transpose=True)
    du = _conv(duu, 0, D_FF, ffn_w3, p["ffn_conv_b"], False, "ffn_conv_bwd_up", slab=(2 * D_FF, D_FF, du), transpose=True, wcol0=D_FF)
    g_w_up = _matmul(h2t, du, name="dw_up")
    none_a, none_w = jnp.zeros((s, 128), BF16), jnp.zeros((D_MODEL, 128), BF16)
    dx2, dx2b, g_norm2 = _proj_norm_bwd(du, p["w_up"], none_a, none_w, x2, p["norm2_w"], dx3, "ffn_up_norm2_bwd")
    g_w_out = _matmul(mix_t, dx2b, name="dw_out")
    dmix = _matmul(dx2b, p["w_out"], name="d_mix", trans_b=True)

    delta, do_pat = _attn_delta(dmix, mix, expand.T, perms, "attn_delta")
    dqs, dks, dvs = [], [], []
    for d, qkv_p, do_p in zip(DILATIONS, qkv, [dmix] + do_pat):
        lse_p, dl_p = _to_pattern(lse_tot, d), _to_pattern(delta, d)
        dqs.append(_attn_bwd_dq(qkv_p, qkv_p, v_col, do_p, lse_p, dl_p, s // d, f"attn_bwd_dq_d{d}"))
        dk, dv = _attn_bwd_dkv(qkv_p, qkv_p, v_col, do_p, lse_p.T, dl_p.T, s // d, f"attn_bwd_dkv_d{d}")
        dks.append(dk)
        dvs.append(dv)
    dproj = _sum3_rope(dqs, perms, tabs_b, "rope_bwd_q", slab=(MAIN_W, 0, None))
    dproj = _sum3_rope(dks, perms, tabs_b, "rope_bwd_k", slab=(MAIN_W, D_MODEL, dproj))
    dproj = _sum3_rope(dvs, perms, None, "sum_dv", slab=(MAIN_W, 2 * D_MODEL, dproj))

    dy, dproj, dxs_skip, g_ssm_norm, g_dskip_lanes = _gate_bwd(yf, yb, xbc, proj, dskip_x, p["ssm_norm_w"], dmix, dproj, "ssm_gate_bwd")
    early = [_to_shards(g_w_out, 0), _to_shards(g_w_up, 1), _to_shards(g_w_down, 0)] if early_exchange else []
    dxbc_f, ddt_f, g_al_f, g_bi_f, *got = _ssd_bwd(xbc, dt_raw, dt_t, *ssd_args["f"], st_f, dy, 0, False, "ssd_bwd_f", exchange=early)
    if early_exchange:
        g_w_out, g_w_up, g_w_down = got
    dxbc_b, ddt_b, g_al_b, g_bi_b = _ssd_bwd(xbc, dt_raw, dt_t, *ssd_args["b"], st_b, dy, N_HEADS, True, "ssd_bwd_b")
    dpre, g_ssm_w3, g_ssm_cb = _conv_silu_bwd(proj, 3 * D_MODEL + D_INNER, XBC, ssm_w3, p["ssm_conv_b"],
                                              [dxbc_f, dxbc_b, dxs_skip], [XBC, XBC, D_INNER], "ssm_conv_bwd")
    dproj = _conv(dpre, 0, XBC, ssm_w3, p["ssm_conv_b"], False, "ssm_conv_bwd_x", transpose=True,
                  slab=(MAIN_W, 3 * D_MODEL + D_INNER, dproj))

    ddt =_pad_lanes(jnp.concatenate([ddt_f, ddt_b], axis=1), 128).astype(BF16)
    g_w_main = _matmul(h1t, dproj, name="dw_in")
    g_w_dt = _matmul(h1t, ddt, name="dw_in_dt")
    g_w_in = jnp.concatenate([g_w_main, g_w_dt[:, :2 * N_HEADS]], axis=1)
    late = [_to_shards(g_w_in, 1)] if early_exchange else []
    grad_x, _, g_norm1, *got = _proj_norm_bwd(dproj, w_main, ddt, w_dt, x, p["norm1_w"], dx2, "in_proj_norm1_bwd", exchange=late)
    if early_exchange:
        g_w_in = got[0]

    g_dskip = jnp.sum(g_dskip_lanes.reshape(N_HEADS, D_INNER // N_HEADS), axis=1).reshape(1, N_HEADS)
    small = {
        "norm1_w": g_norm1, "ssm_conv_w": g_ssm_w3.T, "ssm_conv_b": g_ssm_cb, "a_log_f": g_al_f, "a_log_b": g_al_b,
        "dt_bias_f": g_bi_f, "dt_bias_b": g_bi_b, "d_skip": g_dskip, "ssm_norm_w": g_ssm_norm, "norm2_w": g_norm2,
        "ffn_conv_w": jnp.concatenate([dwg, dwu], axis=1).T, "ffn_conv_b": jnp.concatenate([dbg, dbu], axis=1), "final_norm_w": g_final,
    }
    big = {"w_in": g_w_in, "w_out": g_w_out, "w_up": g_w_up, "w_down": g_w_down}
    return loss[0, 0], grad_x, big, small


SMALL_ORDER = ("norm1_w", "ssm_conv_w", "ssm_conv_b", "a_log_f", "a_log_b", "dt_bias_f", "dt_bias_b", "d_skip",
               "ssm_norm_w", "norm2_w", "ffn_conv_w", "ffn_conv_b", "final_norm_w")
SHARDED_SMALL = ("ssm_conv_w", "ffn_conv_w")
BIG_ORDER = ("w_in", "w_out", "w_up", "w_down")


def _pack(vals):
    rows = []
    for v in vals:
        f = v.reshape(-1).astype(F32)
        n = -(-f.shape[0] // 128) * 128
        rows.append(jnp.pad(f, (0, n - f.shape[0])).reshape(-1, 128))
    out = jnp.concatenate(rows, axis=0)
    pad = -out.shape[0] % 8
    return jnp.pad(out, ((0, pad), (0, 0)))


def _unpack(packed, shapes):
    out, r = [], 0
    for shp in shapes:
        n = math.prod(shp)
        nr = -(-n // 128)
        out.append(packed[r:r + nr].reshape(-1)[:n].reshape(shp))
        r += nr
    return out


def kernel(x, norm1_w, w_in, ssm_conv_w, ssm_conv_b, a_log_f, a_log_b, dt_bias_f, dt_bias_b, d_skip, ssm_norm_w, w_out, norm2_w, w_up, ffn_conv_w, ffn_conv_b, w_down, final_norm_w, loss_target, m_norm1_w, m_w_in, m_ssm_conv_w, m_ssm_conv_b, m_a_log_f, m_a_log_b, m_dt_bias_f, m_dt_bias_b, m_d_skip, m_ssm_norm_w, m_w_out, m_norm2_w, m_w_up, m_ffn_conv_w, m_ffn_conv_b, m_w_down, m_final_norm_w, v_norm1_w, v_w_in, v_ssm_conv_w, v_ssm_conv_b, v_a_log_f, v_a_log_b, v_dt_bias_f, v_dt_bias_b, v_d_skip, v_ssm_norm_w, v_w_out, v_norm2_w, v_w_up, v_ffn_conv_w, v_ffn_conv_b, v_w_down, v_final_norm_w):
    w = dict(norm1_w=norm1_w, w_in=w_in, ssm_conv_w=ssm_conv_w, ssm_conv_b=ssm_conv_b, a_log_f=a_log_f, a_log_b=a_log_b,
             dt_bias_f=dt_bias_f, dt_bias_b=dt_bias_b, d_skip=d_skip, ssm_norm_w=ssm_norm_w, w_out=w_out, norm2_w=norm2_w,
             w_up=w_up, ffn_conv_w=ffn_conv_w, ffn_conv_b=ffn_conv_b, w_down=w_down, final_norm_w=final_norm_w)
    mo = dict(norm1_w=m_norm1_w, w_in=m_w_in, ssm_conv_w=m_ssm_conv_w, ssm_conv_b=m_ssm_conv_b, a_log_f=m_a_log_f, a_log_b=m_a_log_b,
              dt_bias_f=m_dt_bias_f, dt_bias_b=m_dt_bias_b, d_skip=m_d_skip, ssm_norm_w=m_ssm_norm_w, w_out=m_w_out, norm2_w=m_norm2_w,
              w_up=m_w_up, ffn_conv_w=m_ffn_conv_w, ffn_conv_b=m_ffn_conv_b, w_down=m_w_down, final_norm_w=m_final_norm_w)
    vo = dict(norm1_w=v_norm1_w, w_in=v_w_in, ssm_conv_w=v_ssm_conv_w, ssm_conv_b=v_ssm_conv_b, a_log_f=v_a_log_f, a_log_b=v_a_log_b,
              dt_bias_f=v_dt_bias_f, dt_bias_b=v_dt_bias_b, d_skip=v_d_skip, ssm_norm_w=v_ssm_norm_w, w_out=v_w_out, norm2_w=v_norm2_w,
              w_up=v_w_up, ffn_conv_w=v_ffn_conv_w, ffn_conv_b=v_ffn_conv_b, w_down=v_w_down, final_norm_w=v_final_norm_w)
    me = _my_index()

    g_in, g_conv = _all_gather([w["w_in"][0].astype(BF16), _pack([w["ssm_conv_w"][0], w["ffn_conv_w"][0]])], "w_in_all_gather")
    conv_rows = [_unpack(g_conv[i], [ssm_conv_w.shape[1:], ffn_conv_w.shape[1:]]) for i in range(N_DEV)]
    full = {
        "w_in": g_in.transpose(1, 0, 2).reshape(D_MODEL, N_DEV * w_in.shape[2]),
        "ssm_conv_w": jnp.concatenate([c[0] for c in conv_rows], axis=0),
        "ffn_conv_w": jnp.concatenate([c[1] for c in conv_rows], axis=0),
    }
    for k in ("norm1_w", "ssm_conv_b", "a_log_f", "a_log_b", "dt_bias_f", "dt_bias_b", "d_skip", "ssm_norm_w", "norm2_w", "ffn_conv_b",
              "final_norm_w"):
        full[k] = w[k]
    late = [w["w_out"][0].astype(BF16), w["w_up"][0].astype(BF16), w["w_down"][0].astype(BF16)]

    loss_part, grad_x, big, small = _local_step(x[0], loss_target[0], full, late)

    small_shapes = [(1,)] + [small[k].shape for k in SMALL_ORDER]
    packed = _pack([loss_part] + [small[k] for k in SMALL_ORDER])
    out_small = jnp.broadcast_to(packed[None], (N_DEV,) + packed.shape)
    (r_small,) = _exchange([out_small], "small_grads_exchange")
    r_in, r_out, r_up, r_down = big["w_in"], big["w_out"], big["w_up"], big["w_down"]

    outs_g, outs_d, outs_m, outs_v = {}, {}, {}, {}
    for k, parts in zip(BIG_ORDER, (r_in, r_out, r_up, r_down)):
        g, dlt, nm, nv = _adamw(parts, w[k][0], mo[k][0], vo[k][0], f"adamw_{k}")
        outs_g[k], outs_d[k], outs_m[k], outs_v[k] = g[None], dlt[None], nm[None], nv[None]
    tot = _unpack(_sum_parts(r_small, "small_grads_sum"), small_shapes)
    loss = tot[0][0]
    gs = dict(zip(SMALL_ORDER, tot[1:]))
    g_own = {}
    for k in SMALL_ORDER:
        if k in SHARDED_SMALL:
            rows = w[k].shape[1]
            g_own[k] = lax.dynamic_slice_in_dim(gs[k], me * rows, rows, axis=0)[None]
        else:
            g_own[k] = gs[k].reshape(w[k].shape)
    shapes = [w[k].shape for k in SMALL_ORDER]
    d_s, m_s, v_s = _adamw_small(_pack([g_own[k] for k in SMALL_ORDER]), _pack([w[k] for k in SMALL_ORDER]),
                                 _pack([mo[k] for k in SMALL_ORDER]), _pack([vo[k] for k in SMALL_ORDER]), "adamw_small")
    for k, a, b, c in zip(SMALL_ORDER, _unpack(d_s, shapes), _unpack(m_s, shapes), _unpack(v_s, shapes)):
        outs_g[k], outs_d[k], outs_m[k], outs_v[k] = g_own[k], a, b, c

    order = ("norm1_w", "w_in", "ssm_conv_w", "ssm_conv_b", "a_log_f", "a_log_b", "dt_bias_f", "dt_bias_b", "d_skip", "ssm_norm_w",
             "w_out", "norm2_w", "w_up", "ffn_conv_w", "ffn_conv_b", "w_down", "final_norm_w")
    return (loss, grad_x[None], *[outs_g[k] for k in order], *[outs_d[k] for k in order],
            *[outs_m[k] for k in order], *[outs_v[k] for k in order])
```

```python
import math

import numpy as np
import jax
import jax.numpy as jnp
from jax import lax
from jax.experimental import pallas as pl
from jax.experimental.pallas import tpu as pltpu

F32 = jnp.float32
BF16 = jnp.bfloat16
SDS = jax.ShapeDtypeStruct

N_DEV = 8
D_MODEL = 1024
N_HEADS = 16
HEAD_DIM = 64
ROPE_DIM = 16
ROPE_THETA = 500000.0
DILATIONS = (1, 4, 16)
BAND_HALF = 64
D_INNER = 1024
N_GROUPS = 4
D_STATE = 128
CHUNK = 128
XBC = D_INNER + 2 * N_GROUPS * D_STATE
D_FF = 2816
MAIN_W = 3 * D_MODEL + D_INNER + XBC
EPS = 1e-6
LR, B1, B2, AEPS, WD, STEP = 0.001, 0.9, 0.999, 1e-08, 0.01, 10
NEG = -1e30
VMEM_LIMIT = 56 * 1024 * 1024
MESH = pl.DeviceIdType.MESH
HIGH = lax.Precision.HIGHEST
NT = (((1,), (1,)), ((), ()))
TN = (((0,), (0,)), ((), ()))


def _cp(*sem):
    return pltpu.CompilerParams(dimension_semantics=sem, vmem_limit_bytes=VMEM_LIMIT)


def _pick(n, cands):
    for c in cands:
        if n % c == 0:
            return c
    raise ValueError(f"no tile for {n}")


def _sigmoid(x):
    return 1.0 / (1.0 + jnp.exp(-x))


def _softplus(x):
    return jnp.maximum(x, 0.0) + jnp.log1p(jnp.exp(-jnp.abs(x)))


def _slab(s, width, dtype, slab, n_in, out_idx=0):
    if slab is None:
        return SDS((s, width), dtype), 0, [], [], {}
    total, col0, into = slab
    if into is None:
        return SDS((s, total), dtype), col0, [], [], {}
    return SDS((s, total), dtype), col0, [into], [pl.BlockSpec(memory_space=pl.ANY)], {n_in: out_idx}


def _matmul(a, b, *, name, trans_b=False, out_dtype=BF16, residual=None):
    m, k = a.shape
    n = b.shape[0] if trans_b else b.shape[1]
    tk = k if k <= 2048 else _pick(k, (2048, 1408, 1024, 512))
    nk = k // tk
    if nk == 1:
        tm = _pick(m, (2048, 1408, 1024, 512, 256, 128))
        tn = _pick(n, (512, 256, 128))
    else:
        tm = _pick(m, (1024, 1408, 512, 256, 128))
        tn = _pick(n, (1024, 1408, 512, 256, 128))
    dn = NT if trans_b else (((1,), (0,)), ((), ()))

    def body(*refs):
        a_ref, b_ref = refs[0], refs[1]
        o_ref, acc = refs[-2], refs[-1]
        kk = pl.program_id(2)

        @pl.when(kk == 0)
        def _():
            acc[...] = jnp.zeros_like(acc)

        acc[...] += lax.dot_general(a_ref[...], b_ref[...], dn, preferred_element_type=F32)

        @pl.when(kk == nk - 1)
        def _():
            r = acc[...]
            if residual is not None:
                r = r + refs[2][...].astype(F32)
            o_ref[...] = r.astype(o_ref.dtype)

    in_specs = [pl.BlockSpec((tm, tk), lambda i, j, kk: (i, kk)),
                pl.BlockSpec((tn, tk), lambda i, j, kk: (j, kk)) if trans_b else pl.BlockSpec((tk, tn), lambda i, j, kk: (kk, j))]
    args = [a, b]
    if residual is not None:
        in_specs.append(pl.BlockSpec((tm, tn), lambda i, j, kk: (i, j)))
        args.append(residual)
    return pl.pallas_call(
        body, grid=(m // tm, n // tn, nk), in_specs=in_specs,
        out_specs=pl.BlockSpec((tm, tn), lambda i, j, kk: (i, j)),
        out_shape=SDS((m, n), out_dtype), scratch_shapes=[pltpu.VMEM((tm, tn), F32)],
        name=name, compiler_params=_cp("parallel", "parallel", "arbitrary"))(*args)


def _rmsnorm_fwd(x, w, name):
    s, d = x.shape
    tm = _pick(s, (512, 128))

    def body(x_ref, w_ref, o_ref, ot_ref):
        xf = x_ref[...]
        r = lax.rsqrt(jnp.mean(xf * xf, axis=-1, keepdims=True) + EPS)
        out = (xf * r * w_ref[...]).astype(o_ref.dtype)
        o_ref[...] = out
        ot_ref[...] = out.T

    return pl.pallas_call(
        body, grid=(s // tm,), in_specs=[pl.BlockSpec((tm, d), lambda i: (i, 0)), pl.BlockSpec((1, d), lambda i: (0, 0))],
        out_specs=[pl.BlockSpec((tm, d), lambda i: (i, 0)), pl.BlockSpec((d, tm), lambda i: (0, i))],
        out_shape=[SDS((s, d), BF16), SDS((d, s), BF16)], name=name, compiler_params=_cp("parallel"))(x, w)


def _proj_norm_bwd(da, wt, da2, wt2, x, w, dres, name, exchange=()):
    s, k = da.shape
    d = x.shape[1]
    tm = _pick(s, (512, 128))
    tk = _pick(k, (2048, 1408, 1024, 512))
    nk = k // tk
    nx = len(exchange)
    grid = (s // tm, nk)

    def body(*refs):
        a_ref, b_ref, a2_ref, b2_ref, x_ref, w_ref, dres_ref = refs[:7]
        xch_src = refs[7:7 + nx]
        dx_ref, dxb_ref, dw_ref = refs[7 + nx:10 + nx]
        xch_dst = refs[10 + nx:10 + 2 * nx]
        acc = refs[10 + 2 * nx]
        xch_sems = refs[11 + 2 * nx:]
        i, kk = pl.program_id(0), pl.program_id(1)

        @pl.when((i == 0) & (kk == 0))
        def _():
            dw_ref[...] = jnp.zeros_like(dw_ref)
            if nx:
                _exchange_start(xch_src, xch_dst, *xch_sems)

        @pl.when(kk == 0)
        def _():
            acc[...] = lax.dot_general(a2_ref[...], b2_ref[...], NT, preferred_element_type=F32)

        acc[...] += lax.dot_general(a_ref[...], b_ref[...], NT, preferred_element_type=F32)

        @pl.when(kk == nk - 1)
        def _():
            dh = acc[...]
            xf = x_ref[...]
            r = lax.rsqrt(jnp.mean(xf * xf, axis=-1, keepdims=True) + EPS)
            xhat = xf * r
            g = dh * w_ref[...]
            dx = dres_ref[...] + r * (g - xhat * jnp.mean(g * xhat, axis=-1, keepdims=True))
            dx_ref[...] = dx
            dxb_ref[...] = dx.astype(dxb_ref.dtype)
            dw_ref[...] += jnp.sum(dh * xhat, axis=0, keepdims=True)

        if nx:
            @pl.when((i == grid[0] - 1) & (kk == nk - 1))
            def _():
                _exchange_wait(xch_src, xch_dst, *xch_sems)

    row = pl.BlockSpec((tm, d), lambda i, kk: (i, 0))
    vec = pl.BlockSpec((1, d), lambda i, kk: (0, 0))
    any_spec = pl.BlockSpec(memory_space=pl.ANY)
    in_specs = [pl.BlockSpec((tm, tk), lambda i, kk: (i, kk)), pl.BlockSpec((d, tk), lambda i, kk: (0, kk)),
                pl.BlockSpec((tm, da2.shape[1]), lambda i, kk: (i, 0)), pl.BlockSpec((d, wt2.shape[1]), lambda i, kk: (0, 0)),
                row, vec, row]
    return pl.pallas_call(
        body, grid=grid, in_specs=in_specs + [any_spec] * nx, out_specs=[row, row, vec] + [any_spec] * nx,
        out_shape=[SDS((s, d), F32), SDS((s, d), BF16), SDS((1, d), F32)] + [SDS(e.shape, e.dtype) for e in exchange],
        scratch_shapes=[pltpu.VMEM((tm, d), F32)] + (_exchange_sems(nx) if nx else []),
        name=name, compiler_params=_cp("arbitrary", "arbitrary"))(da, wt, da2, wt2, x, w, dres, *exchange)


def _final_norm_loss(x, w, target, name):
    s, d = x.shape
    tm = _pick(s, (512, 128))

    def body(x_ref, w_ref, t_ref, dx_ref, dxb_ref, dw_ref, loss_ref):
        xf = x_ref[...]
        r = lax.rsqrt(jnp.mean(xf * xf, axis=-1, keepdims=True) + EPS)
        xhat = xf * r
        wv = w_ref[...]
        e = xhat * wv - t_ref[...]
        dy = e * (1.0 / d)
        g = dy * wv
        dx = r * (g - xhat * jnp.mean(g * xhat, axis=-1, keepdims=True))
        dx_ref[...] = dx
        dxb_ref[...] = dx.astype(dxb_ref.dtype)

        @pl.when(pl.program_id(0) == 0)
        def _():
            dw_ref[...] = jnp.zeros_like(dw_ref)
            loss_ref[...] = jnp.zeros_like(loss_ref)

        dw_ref[...] += jnp.sum(dy * xhat, axis=0, keepdims=True)
        loss_ref[...] += jnp.sum(jnp.sum(e * e, axis=1, keepdims=True), axis=0, keepdims=True) * (0.5 / d)

    row = pl.BlockSpec((tm, d), lambda i: (i, 0))
    vec = pl.BlockSpec((1, d), lambda i: (0, 0))
    return pl.pallas_call(
        body, grid=(s // tm,), in_specs=[row, vec, row], out_specs=[row, row, vec, pl.BlockSpec((1, 128), lambda i: (0, 0))],
        out_shape=[SDS((s, d), F32), SDS((s, d), BF16), SDS((1, d), F32), SDS((1, 128), F32)],
        name=name, compiler_params=_cp("arbitrary"))(x, w, target)


def _rope_tables(s):
    half = ROPE_DIM // 2
    f32 = np.float32
    inv_freq = np.power(f32(ROPE_THETA), -np.arange(half, dtype=f32) * f32(2.0) / f32(ROPE_DIM)).astype(f32)
    ang = (np.arange(s, dtype=f32)[:, None] * inv_freq[None, :]).astype(f32)
    cos, sin = np.cos(ang).astype(f32), np.sin(ang).astype(f32)
    z = np.zeros((s, HEAD_DIM - ROPE_DIM), f32)
    zh = np.zeros((s, half), f32)
    c = np.concatenate([cos, cos, z + 1.0], axis=1)
    sa = np.concatenate([zh, sin, z], axis=1)
    sb = np.concatenate([-sin, zh, z], axis=1)
    two = lambda t: np.concatenate([t, t], axis=1)
    c, sa, sb = two(c), two(sa), two(sb)
    fwd, bwd = (c, sa, sb), (c, np.roll(sb, half, axis=1), np.roll(sa, -half, axis=1))
    return tuple(jnp.asarray(t) for t in fwd), tuple(jnp.asarray(t) for t in bwd)


PERM_TILE = 256


def _perm_matrices(d):
    n = PERM_TILE // d
    o = np.arange(PERM_TILE)
    p = np.zeros((PERM_TILE, PERM_TILE), np.float32)
    p[o, (o % n) * d + o // n] = 1.0
    return jnp.asarray(p, dtype=BF16), jnp.asarray(p.T.copy(), dtype=BF16)


def _store_pattern(o_ref, tile, perm_ref, d, cols=slice(None)):
    n = PERM_TILE // d
    z = jnp.dot(perm_ref[...], tile, preferred_element_type=F32).astype(o_ref.dtype)
    for r in range(d):
        o_ref[r, :, cols] = z[r * n:(r + 1) * n]


def _load_pattern(x_ref, perm_ref, d):
    tile = jnp.concatenate([x_ref[r] for r in range(d)], axis=0)
    return jnp.dot(perm_ref[...], tile, preferred_element_type=F32)


def _pattern_spec(d, w, col=0):
    return pl.BlockSpec((d, PERM_TILE // d, w), lambda i, *_: (0, i, col))


def _rope_fwd(proj, tabs, perms, name):
    s = proj.shape[0]
    tm = PERM_TILE
    half = ROPE_DIM // 2
    wb = D_MODEL
    nd = len(DILATIONS) - 1

    def body(x_ref, c_ref, sa_ref, sb_ref, *rest):
        perm_refs, o_ref, op_refs = rest[:nd], rest[nd], rest[nd + 1:]
        is_v = pl.program_id(1) == 2
        c = jnp.where(is_v, 1.0, c_ref[...])
        sa = jnp.where(is_v, 0.0, sa_ref[...])
        sb = jnp.where(is_v, 0.0, sb_ref[...])
        for j in range(wb // 128):
            x = x_ref[:, j * 128:(j + 1) * 128].astype(F32)
            o_ref[:, j * 128:(j + 1) * 128] = (x * c + pltpu.roll(x, half, 1) * sa + pltpu.roll(x, 128 - half, 1) * sb).astype(o_ref.dtype)
        y = o_ref[...]
        for d, perm_ref, op_ref in zip(DILATIONS[1:], perm_refs, op_refs):
            _store_pattern(op_ref, y, perm_ref, d)

    blk = pl.BlockSpec((tm, wb), lambda i, j: (i, j))
    tab = pl.BlockSpec((tm, 128), lambda i, j: (i, 0))
    pm = pl.BlockSpec((tm, tm), lambda i, j: (0, 0))
    outs = pl.pallas_call(
        body, grid=(s // tm, 3), in_specs=[blk, tab, tab, tab] + [pm] * nd,
        out_specs=[blk] + [pl.BlockSpec((d, tm // d, wb), lambda i, j: (0, i, j)) for d in DILATIONS[1:]],
        out_shape=[SDS((s, 3 * wb), BF16)] + [SDS((d, s // d, 3 * wb), BF16) for d in DILATIONS[1:]],
        name=name, compiler_params=_cp("parallel", "parallel"))(proj, *tabs, *[perms[d][0] for d in DILATIONS[1:]])
    return [o.reshape(s, 3 * wb) for o in outs]


def _sum3_rope(ds_, perms, tabs, name, slab=None):
    s, w = ds_[0].shape
    tm = PERM_TILE
    half = ROPE_DIM // 2
    nd = len(DILATIONS) - 1

    def body(*refs):
        x_refs, perm_refs = refs[:nd + 1], refs[nd + 1:2 * nd + 1]
        tab_refs = refs[2 * nd + 1:2 * nd + 4]
        tot = x_refs[0][...].astype(F32)
        for d, x_ref, perm_ref in zip(DILATIONS[1:], x_refs[1:], perm_refs):
            tot = tot + _load_pattern(x_ref, perm_ref, d)
        for j in range(w // 128):
            x = tot[:, j * 128:(j + 1) * 128]
            if tabs is not None:
                x = x * tab_refs[0][...] + pltpu.roll(x, half, 1) * tab_refs[1][...] + pltpu.roll(x, 128 - half, 1) * tab_refs[2][...]
            refs[-1][:, j * 128:(j + 1) * 128] = x.astype(refs[-1].dtype)

    blk = pl.BlockSpec((tm, w), lambda i: (i, 0))
    tab = pl.BlockSpec((tm, 128), lambda i: (i, 0))
    pm = pl.BlockSpec((tm, tm), lambda i: (0, 0))
    extra = [] if tabs is None else list(tabs)
    out_shape, col0, more, more_specs, alias = _slab(s, w, BF16, slab, 2 * nd + 1 + len(extra))
    cb = col0 // w
    args = [ds_[0]] + [x.reshape(d, s // d, w) for d, x in zip(DILATIONS[1:], ds_[1:])] + [perms[d][1] for d in DILATIONS[1:]]
    return pl.pallas_call(
        body, grid=(s // tm,),
        in_specs=[blk] + [_pattern_spec(d, w) for d in DILATIONS[1:]] + [pm] * nd + [tab] * len(extra) + more_specs,
        out_specs=pl.BlockSpec((tm, w), lambda i: (i, cb)), out_shape=out_shape, input_output_aliases=alias,
        name=name, compiler_params=_cp("parallel"))(*args, *extra, *more)


def _band_valid(t, nq, nk, qofs, kofs, seq_len):
    qpos = t * 128 + qofs + lax.broadcasted_iota(jnp.int32, (nq, nk), 0)
    kpos = t * 128 + kofs + lax.broadcasted_iota(jnp.int32, (nq, nk), 1)
    sh = int(math.log2(seq_len))
    same = lax.shift_right_arithmetic(qpos, sh) == lax.shift_right_arithmetic(kpos, sh)
    return same & (jnp.abs(kpos - qpos) <= BAND_HALF)


def _window(r0, r1, r2):
    return jnp.concatenate([r0[128 - BAND_HALF:128], r1[...], r2[0:BAND_HALF]], axis=0)


WIN = 128 + 2 * BAND_HALF


def _first_head_lanes():
    return lax.broadcasted_iota(jnp.int32, (1, 2 * HEAD_DIM), 1) < HEAD_DIM


def _only(x, keep):
    return jnp.where(keep, x, jnp.zeros((), x.dtype))


def _win_specs(width, col, nt):
    return [pl.BlockSpec((128, width), lambda t: (jnp.maximum(t - 1, 0), col)),
            pl.BlockSpec((128, width), lambda t: (t, col)),
            pl.BlockSpec((128, width), lambda t: (jnp.minimum(t + 1, nt - 1), col))]


def _attn_fwd(qk, v_src, v_col, seq_len, name):
    s = qk.shape[0]
    nt = s // 128
    dm = D_MODEL

    def body(q_ref, k0, k1, k2, v0, v1, v2, o_ref, lse_ref):
        t = pl.program_id(0)
        valid = _band_valid(t, 128, WIN, 0, -BAND_HALF, seq_len)
        q = q_ref[...]
        kc = _window(k0, k1, k2)
        vc = _window(v0, v1, v2)
        first = _first_head_lanes()
        outs, lses = [], []
        for pr in range(N_HEADS // 2):
            ps = slice(pr * 128, (pr + 1) * 128)
            qp, kp, vp = q[:, ps], kc[:, ps], vc[:, ps]
            halves = []
            for keep in (first, ~first):
                sc = lax.dot_general(_only(qp, keep), kp, NT, preferred_element_type=F32) * (HEAD_DIM ** -0.5)
                sc = jnp.where(valid, sc, NEG)
                m = jnp.max(sc, axis=1, keepdims=True)
                e = jnp.exp(sc - m)
                den = jnp.sum(e, axis=1, keepdims=True)
                halves.append(jnp.dot(e.astype(BF16), vp, preferred_element_type=F32) / den)
                lses.append(m + jnp.log(den))
            outs.append(jnp.where(first, halves[0], halves[1]))
        o_ref[...] = jnp.concatenate(outs, axis=1).astype(o_ref.dtype)
        lse_ref[...] = jnp.concatenate(lses, axis=1)

    in_specs = [pl.BlockSpec((128, dm), lambda t: (t, 0))] + _win_specs(dm, 1, nt) + _win_specs(dm, v_col, nt)
    return pl.pallas_call(
        body, grid=(nt,), in_specs=in_specs,
        out_specs=[pl.BlockSpec((128, dm), lambda t: (t, 0)), pl.BlockSpec((128, N_HEADS), lambda t: (t, 0))],
        out_shape=[SDS((s, dm), BF16), SDS((s, N_HEADS), F32)], name=name, compiler_params=_cp("parallel"))(
            qk, qk, qk, qk, v_src, v_src, v_src)


def _attn_combine(os_, lses, perms, expand, name):
    s, dm = os_[0].shape
    tm = PERM_TILE
    nd = len(DILATIONS) - 1

    def body(o1, o2, o3, l1, l2, l3, p2, p3, e_ref, out_ref, out_t_ref, lt_ref):
        ls = [l1[...], l2[...], l3[...]]
        m = jnp.maximum(jnp.maximum(ls[0], ls[1]), ls[2])
        es = [jnp.exp(l - m) for l in ls]
        tot = es[0] + es[1] + es[2]
        lt_ref[...] = m + jnp.log(tot)
        ovs = [o1[...].astype(F32), _load_pattern(o2, p2, DILATIONS[1]), _load_pattern(o3, p3, DILATIONS[2])]
        acc = jnp.zeros((tm, dm), F32)
        for e, o in zip(es, ovs):
            acc = acc + jnp.dot(e / tot, e_ref[...], precision=HIGH, preferred_element_type=F32) * o
        out = acc.astype(out_ref.dtype)
        out_ref[...] = out
        out_t_ref[...] = out.T

    row = pl.BlockSpec((tm, dm), lambda i: (i, 0))
    st = pl.BlockSpec((tm, N_HEADS), lambda i: (i, 0))
    pm = pl.BlockSpec((tm, tm), lambda i: (0, 0))
    args = [os_[0]] + [o.reshape(d, s // d, dm) for d, o in zip(DILATIONS[1:], os_[1:])]
    return pl.pallas_call(
        body, grid=(s // tm,),
        in_specs=[row] + [_pattern_spec(d, dm) for d in DILATIONS[1:]] + [st, st, st, pm, pm, pl.BlockSpec((N_HEADS, dm), lambda i: (0, 0))],
        out_specs=[row, pl.BlockSpec((dm, tm), lambda i: (0, i)), st],
        out_shape=[SDS((s, dm + D_INNER), BF16), SDS((dm + D_INNER, s), BF16), SDS((s, N_HEADS), F32)],
        name=name, compiler_params=_cp("parallel"))(*args, *lses, *[perms[d][1] for d in DILATIONS[1:]], expand)


def _attn_delta(dmix, attn, expand_t, perms, name):
    s, dm = attn.shape[0], D_MODEL
    tm = PERM_TILE
    nd = len(DILATIONS) - 1

    def body(d_ref, a_ref, e_ref, *rest):
        perm_refs, o_ref, op_refs = rest[:nd], rest[nd], rest[nd + 1:]
        dv = d_ref[...]
        prod = dv.astype(F32) * a_ref[...].astype(F32)
        o_ref[...] = jnp.dot(prod, e_ref[...], precision=HIGH, preferred_element_type=F32)
        for d, perm_ref, op_ref in zip(DILATIONS[1:], perm_refs, op_refs):
            _store_pattern(op_ref, dv, perm_ref, d)

    row = pl.BlockSpec((tm, dm), lambda i: (i, 0))
    pm = pl.BlockSpec((tm, tm), lambda i: (0, 0))
    outs = pl.pallas_call(
        body, grid=(s // tm,), in_specs=[row, row, pl.BlockSpec((dm, N_HEADS), lambda i: (0, 0))] + [pm] * nd,
        out_specs=[pl.BlockSpec((tm, N_HEADS), lambda i: (i, 0))] + [_pattern_spec(d, dm) for d in DILATIONS[1:]],
        out_shape=[SDS((s, N_HEADS), F32)] + [SDS((d, s // d, dm), BF16) for d in DILATIONS[1:]],
        name=name, compiler_params=_cp("parallel"))(dmix, attn, expand_t, *[perms[d][0] for d in DILATIONS[1:]])
    return outs[0], [o.reshape(s, dm) for o in outs[1:]]


def _attn_bwd_dq(qk, v_src, v_col, do_src, lse, delta, seq_len, name):
    s = qk.shape[0]
    nt = s // 128
    dm = D_MODEL

    def body(q_ref, k0, k1, k2, v0, v1, v2, do_ref, lse_ref, dl_ref, dq_ref):
        t = pl.program_id(0)
        valid = _band_valid(t, 128, WIN, 0, -BAND_HALF, seq_len)
        q = q_ref[...]
        do = do_ref[...]
        kc = _window(k0, k1, k2)
        vc = _window(v0, v1, v2)
        lse_v, dl_v = lse_ref[...], dl_ref[...]
        first = _first_head_lanes()
        outs = []
        for pr in range(N_HEADS // 2):
            ps = slice(pr * 128, (pr + 1) * 128)
            qp, dop, kp, vp = q[:, ps], do[:, ps], kc[:, ps], vc[:, ps]
            halves = []
            for i, keep in enumerate((first, ~first)):
                h = 2 * pr + i
                sc = lax.dot_general(_only(qp, keep), kp, NT, preferred_element_type=F32) * (HEAD_DIM ** -0.5)
                p = jnp.exp(jnp.where(valid, sc - lse_v[:, h:h + 1], NEG))
                dp = lax.dot_general(_only(dop, keep), vp, NT, preferred_element_type=F32)
                ds = p * (dp - dl_v[:, h:h + 1])
                halves.append(jnp.dot(ds.astype(BF16), kp, preferred_element_type=F32))
            outs.append(jnp.where(first, halves[0], halves[1]) * (HEAD_DIM ** -0.5))
        dq_ref[...] = jnp.concatenate(outs, axis=1).astype(dq_ref.dtype)

    row = pl.BlockSpec((128, dm), lambda t: (t, 0))
    st = pl.BlockSpec((128, N_HEADS), lambda t: (t, 0))
    in_specs = [row] + _win_specs(dm, 1, nt) + _win_specs(dm, v_col, nt) + [row, st, st]
    return pl.pallas_call(
        body, grid=(nt,), in_specs=in_specs, out_specs=row, out_shape=SDS((s, dm), BF16),
        name=name, compiler_params=_cp("parallel"))(qk, qk, qk, qk, v_src, v_src, v_src, do_src, lse, delta)


def _attn_bwd_dkv(qk, v_src, v_col, do_src, lse_t, delta_t, seq_len, name):
    s = qk.shape[0]
    nt = s // 128
    dm = D_MODEL

    def lane_window(r0, r1, r2):
        return jnp.concatenate([r0[:, 128 - BAND_HALF:128], r1[...], r2[:, 0:BAND_HALF]], axis=1)

    def body(k_ref, v_ref, q0, q1, q2, d0, d1, d2, l0, l1, l2, e0, e1, e2, dk_ref, dv_ref):
        t = pl.program_id(0)
        valid = _band_valid(t, 128, WIN, 0, -BAND_HALF, seq_len)
        k = k_ref[...]
        v = v_ref[...]
        qc = _window(q0, q1, q2)
        dc = _window(d0, d1, d2)
        lse_v = lane_window(l0, l1, l2)
        dl_v = lane_window(e0, e1, e2)
        first = _first_head_lanes()
        dks, dvs = [], []
        for pr in range(N_HEADS // 2):
            ps = slice(pr * 128, (pr + 1) * 128)
            kp, vp, qp, dop = k[:, ps], v[:, ps], qc[:, ps], dc[:, ps]
            dk_h, dv_h = [], []
            for i, keep in enumerate((first, ~first)):
                h = 2 * pr + i
                sc = lax.dot_general(_only(kp, keep), qp, NT, preferred_element_type=F32) * (HEAD_DIM ** -0.5)
                p = jnp.exp(jnp.where(valid, sc - lse_v[h:h + 1, :], NEG))
                dv_h.append(jnp.dot(p.astype(BF16), dop, preferred_element_type=F32))
                dp = lax.dot_general(_only(vp, keep), dop, NT, preferred_element_type=F32)
                ds = p * (dp - dl_v[h:h + 1, :])
                dk_h.append(jnp.dot(ds.astype(BF16), qp, preferred_element_type=F32))
            dks.append(jnp.where(first, dk_h[0], dk_h[1]) * (HEAD_DIM ** -0.5))
            dvs.append(jnp.where(first, dv_h[0], dv_h[1]))
        dk_ref[...] = jnp.concatenate(dks, axis=1).astype(dk_ref.dtype)
        dv_ref[...] = jnp.concatenate(dvs, axis=1).astype(dv_ref.dtype)

    row = pl.BlockSpec((128, dm), lambda t: (t, 0))
    stat = [pl.BlockSpec((N_HEADS, 128), lambda t: (0, jnp.maximum(t - 1, 0))), pl.BlockSpec((N_HEADS, 128), lambda t: (0, t)),
            pl.BlockSpec((N_HEADS, 128), lambda t: (0, jnp.minimum(t + 1, nt - 1)))]
    in_specs = ([pl.BlockSpec((128, dm), lambda t: (t, 1)), pl.BlockSpec((128, dm), lambda t: (t, v_col))]
                + _win_specs(dm, 0, nt) + _win_specs(dm, 0, nt) + stat + stat)
    return pl.pallas_call(
        body, grid=(nt,), in_specs=in_specs, out_specs=[row, row], out_shape=[SDS((s, dm), BF16), SDS((s, dm), BF16)],
        name=name, compiler_params=_cp("parallel"))(qk, v_src, qk, qk, qk, do_src, do_src, do_src, lse_t, lse_t, lse_t, delta_t, delta_t, delta_t)


CONV_COLS = (1024, 1408, 512, 256)


def _halo_specs(tm, tc, col0, nrow_blocks):
    r = tm // 16
    return [pl.BlockSpec((16, tc), lambda i, j: (jnp.maximum(i * r - 1, 0), col0 + j)),
            pl.BlockSpec((16, tc), lambda i, j: (jnp.minimum((i + 1) * r, nrow_blocks * r - 1), col0 + j))]


def _shifted(x_ref, hp_ref, hn_ref, i, last):
    x = x_ref[...].astype(F32)
    tm = x.shape[0]
    rows = lax.broadcasted_iota(jnp.int32, x.shape, 0)
    prev_row = jnp.where(i > 0, hp_ref[15:16, :].astype(F32), 0.0)
    next_row = jnp.where(i < last, hn_ref[0:1, :].astype(F32), 0.0)
    xp = jnp.where(rows == 0, prev_row, pltpu.roll(x, 1, 0))
    xn = jnp.where(rows == tm - 1, next_row, pltpu.roll(x, tm - 1, 0))
    return xp, x, xn


def _conv(x_src, col0, width, w3, bias, act, name, out_dtype=BF16, slab=None, transpose=False, wcol0=0):
    s = x_src.shape[0]
    tm = _pick(s, (256, 128))
    tc = _pick(width, CONV_COLS)
    nb = s // tm
    c0 = col0 // tc
    wc0 = wcol0 // tc

    def body(*refs):
        x_ref, hp_ref, hn_ref, w_ref, b_ref = refs[:5]
        o_ref = refs[-1]
        i = pl.program_id(0)
        xp, x, xn = _shifted(x_ref, hp_ref, hn_ref, i, nb - 1)
        w = w_ref[...]
        if transpose:
            y = w[2:3, :] * xp + w[1:2, :] * x + w[0:1, :] * xn
        else:
            y = w[0:1, :] * xp + w[1:2, :] * x + w[2:3, :] * xn + b_ref[...]
        if act:
            y = y * _sigmoid(y)
        o_ref[...] = y.astype(o_ref.dtype)

    in_specs = ([pl.BlockSpec((tm, tc), lambda i, j: (i, c0 + j))] + _halo_specs(tm, tc, c0, nb)
                + [pl.BlockSpec((3, tc), lambda i, j: (0, wc0 + j)), pl.BlockSpec((1, tc), lambda i, j: (0, wc0 + j))])
    out_shape, ocol, more, more_specs, alias = _slab(s, width, out_dtype, slab, 5)
    ob = ocol // tc
    return pl.pallas_call(
        body, grid=(nb, width // tc), in_specs=in_specs + more_specs, out_specs=pl.BlockSpec((tm, tc), lambda i, j: (i, ob + j)),
        out_shape=out_shape, input_output_aliases=alias, name=name, compiler_params=_cp("parallel", "parallel"))(
            x_src, x_src, x_src, w3, bias, *more)


def _conv_silu_bwd(x_src, col0, width, w3, bias, addends, add_widths, name):
    s = x_src.shape[0]
    tm = _pick(s, (256, 128))
    tc = _pick(width, CONV_COLS)
    nb = s // tm
    c0 = col0 // tc
    na = len(addends)

    def body(*refs):
        x_ref, hp_ref, hn_ref, w_ref, b_ref = refs[:5]
        a_refs = refs[5:5 + na]
        dp_ref, dw_ref, db_ref = refs[5 + na:]
        i, j = pl.program_id(1), pl.program_id(0)
        xp, x, xn = _shifted(x_ref, hp_ref, hn_ref, i, nb - 1)
        w = w_ref[...]
        pre = w[0:1, :] * xp + w[1:2, :] * x + w[2:3, :] * xn + b_ref[...]
        g = jnp.zeros_like(pre)
        for a_ref, aw in zip(a_refs, add_widths):
            av = a_ref[...].astype(F32)
            g = g + (av if aw == width else jnp.where(j < aw // tc, av, 0.0))
        sg = _sigmoid(pre)
        dpre = g * (sg * (1.0 + pre * (1.0 - sg)))
        dp_ref[...] = dpre.astype(dp_ref.dtype)

        @pl.when(i == 0)
        def _():
            dw_ref[...] = jnp.zeros_like(dw_ref)
            db_ref[...] = jnp.zeros_like(db_ref)

        dw_ref[...] += jnp.concatenate([jnp.sum(dpre * xp, axis=0, keepdims=True), jnp.sum(dpre * x, axis=0, keepdims=True),
                                        jnp.sum(dpre * xn, axis=0, keepdims=True)], axis=0)
        db_ref[...] += jnp.sum(dpre, axis=0, keepdims=True)

    r = tm // 16
    in_specs = [pl.BlockSpec((tm, tc), lambda j, i: (i, c0 + j)),
                pl.BlockSpec((16, tc), lambda j, i: (jnp.maximum(i * r - 1, 0), c0 + j)),
                pl.BlockSpec((16, tc), lambda j, i: (jnp.minimum((i + 1) * r, nb * r - 1), c0 + j)),
                pl.BlockSpec((3, tc), lambda j, i: (0, j)), pl.BlockSpec((1, tc), lambda j, i: (0, j))]
    for aw in add_widths:
        nblk = aw // tc
        in_specs.append(pl.BlockSpec((tm, tc), lambda j, i, nblk=nblk: (i, jnp.minimum(j, nblk - 1))))
    return pl.pallas_call(
        body, grid=(width // tc, nb), in_specs=in_specs,
        out_specs=[pl.BlockSpec((tm, tc), lambda j, i: (i, j)), pl.BlockSpec((3, tc), lambda j, i: (0, j)), pl.BlockSpec((1, tc), lambda j, i: (0, j))],
        out_shape=[SDS((s, width), BF16), SDS((3, width), F32), SDS((1, width), F32)],
        name=name, compiler_params=_cp("parallel", "arbitrary"))(x_src, x_src, x_src, w3, bias, *addends)


def _ffn_gate_fwd(u, w3, bias, name):
    s = u.shape[0]
    tm = _pick(s, (256, 128))
    tc = _pick(D_FF, CONV_COLS)
    nb = s // tm
    nj = D_FF // tc

    def body(g_ref, gp, gn, u_ref, up, un, wg_ref, wu_ref, bg_ref, bu_ref, o_ref, ot_ref):
        i = pl.program_id(0)
        outs = []
        for (x_ref, hp, hn, w_ref, b_ref) in ((g_ref, gp, gn, wg_ref, bg_ref), (u_ref, up, un, wu_ref, bu_ref)):
            xp, x, xn = _shifted(x_ref, hp, hn, i, nb - 1)
            w = w_ref[...]
            outs.append(w[0:1, :] * xp + w[1:2, :] * x + w[2:3, :] * xn + b_ref[...])
        gate, upv = outs
        out = (gate * _sigmoid(gate) * upv).astype(o_ref.dtype)
        o_ref[...] = out
        ot_ref[...] = out.T

    def xspecs(c0):
        return [pl.BlockSpec((tm, tc), lambda i, j: (i, c0 + j))] + _halo_specs(tm, tc, c0, nb)

    in_specs = (xspecs(0) + xspecs(nj)
                + [pl.BlockSpec((3, tc), lambda i, j: (0, j)), pl.BlockSpec((3, tc), lambda i, j: (0, nj + j)),
                   pl.BlockSpec((1, tc), lambda i, j: (0, j)), pl.BlockSpec((1, tc), lambda i, j: (0, nj + j))])
    return pl.pallas_call(
        body, grid=(nb, nj), in_specs=in_specs,
        out_specs=[pl.BlockSpec((tm, tc), lambda i, j: (i, j)), pl.BlockSpec((tc, tm), lambda i, j: (j, i))],
        out_shape=[SDS((s, D_FF), BF16), SDS((D_FF, s), BF16)], name=name, compiler_params=_cp("parallel", "parallel"))(
            u, u, u, u, u, u, w3, w3, bias, bias)


def _ffn_gate_bwd(u, w3, bias, dact, name):
    s = u.shape[0]
    tm = _pick(s, (256, 128))
    tc = _pick(D_FF, CONV_COLS)
    nb = s // tm
    nj = D_FF // tc

    def body(g_ref, gp, gn, u_ref, up, un, wg_ref, wu_ref, bg_ref, bu_ref, da_ref, dg_ref, du_ref, dwg_ref, dwu_ref, dbg_ref, dbu_ref):
        i = pl.program_id(1)
        sh, pre = [], []
        for (x_ref, hp, hn, w_ref, b_ref) in ((g_ref, gp, gn, wg_ref, bg_ref), (u_ref, up, un, wu_ref, bu_ref)):
            xs3 = _shifted(x_ref, hp, hn, i, nb - 1)
            w = w_ref[...]
            sh.append(xs3)
            pre.append(w[0:1, :] * xs3[0] + w[1:2, :] * xs3[1] + w[2:3, :] * xs3[2] + b_ref[...])
        gate, upv = pre
        da = da_ref[...].astype(F32)
        sg = _sigmoid(gate)
        dgate = da * upv * (sg * (1.0 + gate * (1.0 - sg)))
        dup = da * gate * sg
        dg_ref[...] = dgate.astype(dg_ref.dtype)
        du_ref[...] = dup.astype(du_ref.dtype)

        @pl.when(i == 0)
        def _():
            for r in (dwg_ref, dwu_ref, dbg_ref, dbu_ref):
                r[...] = jnp.zeros_like(r)

        for d, xs3, dw_ref, db_ref in ((dgate, sh[0], dwg_ref, dbg_ref), (dup, sh[1], dwu_ref, dbu_ref)):
            dw_ref[...] += jnp.concatenate([jnp.sum(d * xs3[0], axis=0, keepdims=True), jnp.sum(d * xs3[1], axis=0, keepdims=True),
                                            jnp.sum(d * xs3[2], axis=0, keepdims=True)], axis=0)
            db_ref[...] += jnp.sum(d, axis=0, keepdims=True)

    r = tm // 16

    def xspecs(c0):
        return [pl.BlockSpec((tm, tc), lambda j, i: (i, c0 + j)),
                pl.BlockSpec((16, tc), lambda j, i: (jnp.maximum(i * r - 1, 0), c0 + j)),
                pl.BlockSpec((16, tc), lambda j, i: (jnp.minimum((i + 1) * r, nb * r - 1), c0 + j))]

    in_specs = (xspecs(0) + xspecs(nj)
                + [pl.BlockSpec((3, tc), lambda j, i: (0, j)), pl.BlockSpec((3, tc), lambda j, i: (0, nj + j)),
                   pl.BlockSpec((1, tc), lambda j, i: (0, j)), pl.BlockSpec((1, tc), lambda j, i: (0, nj + j)),
                   pl.BlockSpec((tm, tc), lambda j, i: (i, j))])
    blk = pl.BlockSpec((tm, tc), lambda j, i: (i, j))
    w_o = pl.BlockSpec((3, tc), lambda j, i: (0, j))
    b_o = pl.BlockSpec((1, tc), lambda j, i: (0, j))
    return pl.pallas_call(
        body, grid=(nj, nb), in_specs=in_specs, out_specs=[blk, blk, w_o, w_o, b_o, b_o],
        out_shape=[SDS((s, D_FF), BF16), SDS((s, D_FF), BF16), SDS((3, D_FF), F32), SDS((3, D_FF), F32), SDS((1, D_FF), F32), SDS((1, D_FF), F32)],
        name=name, compiler_params=_cp("parallel", "arbitrary"))(u, u, u, u, u, u, w3, w3, bias, bias, dact)


def _exchange_sems(n):
    return [pltpu.SemaphoreType.DMA((7 * n,)), pltpu.SemaphoreType.DMA((7 * n,)), pltpu.SemaphoreType.DMA((n,))]


def _exchange_copies(srcs, outs, send_sems, recv_sems, local_sems):
    x, y, c = lax.axis_index("x"), lax.axis_index("y"), lax.axis_index("c")
    me = 4 * x + 2 * y + c
    locals_ = [pltpu.make_async_copy(srcs[a].at[me], outs[a].at[me], local_sems.at[a]) for a in range(len(srcs))]
    sends, recvs = [], []
    for k in range(1, N_DEV):
        px, py, pc = x ^ ((k >> 2) & 1), y ^ ((k >> 1) & 1), c ^ (k & 1)
        peer = 4 * px + 2 * py + pc
        for a in range(len(srcs)):
            sems = dict(send_sem=send_sems.at[a * 7 + k - 1], recv_sem=recv_sems.at[a * 7 + k - 1], device_id_type=MESH)
            sends.append(pltpu.make_async_remote_copy(src_ref=srcs[a].at[peer], dst_ref=outs[a].at[me], device_id=(px, py, pc), **sems))
            recvs.append(pltpu.make_async_remote_copy(src_ref=srcs[a].at[peer], dst_ref=outs[a].at[peer], device_id=(x, y, c), **sems))
    return locals_, sends, recvs


def _exchange_start(srcs, outs, send_sems, recv_sems, local_sems):
    locals_, sends, _ = _exchange_copies(srcs, outs, send_sems, recv_sems, local_sems)
    for cp in locals_ + sends:
        cp.start()


def _exchange_wait(srcs, outs, send_sems, recv_sems, local_sems):
    locals_, sends, recvs = _exchange_copies(srcs, outs, send_sems, recv_sems, local_sems)
    for cp in recvs:
        cp.wait_recv()
    for cp in sends:
        cp.wait_send()
    for cp in locals_:
        cp.wait()


def _gather_copies(srcs, outs, send_sems, recv_sems, local_sems):
    x, y, c = lax.axis_index("x"), lax.axis_index("y"), lax.axis_index("c")
    me, sibling = (x, y, c), (x, y, 1 - c)
    chips = [(1 - x, y), (x, 1 - y), (1 - x, 1 - y)]

    def copy(a, k, block, to, src=None):
        dst = outs[a].at[4 * block[0] + 2 * block[1] + block[2]]
        return pltpu.make_async_remote_copy(
            src_ref=dst if src is None else src, dst_ref=dst,
            send_sem=send_sems.at[a * 7 + k], recv_sem=recv_sems.at[a * 7 + k], device_id=to, device_id_type=MESH)

    n = len(srcs)
    locals_ = [pltpu.make_async_copy(srcs[a], outs[a].at[4 * x + 2 * y + c], local_sems.at[a]) for a in range(n)]
    own = [copy(a, 0, me, sibling, src=srcs[a]) for a in range(n)]
    own += [copy(a, 1 + j, me, (*chip, c), src=srcs[a]) for a in range(n) for j, chip in enumerate(chips)]
    landed_ici = [copy(a, 1 + j, (*chip, c), me) for j, chip in enumerate(chips) for a in range(n)]
    passed = [copy(a, 4 + j, (*chip, c), sibling) for j, chip in enumerate(chips) for a in range(n)]
    landed_d2d = [copy(a, 0, sibling, me) for a in range(n)]
    landed_d2d += [copy(a, 4 + j, (*chip, 1 - c), me) for a in range(n) for j, chip in enumerate(chips)]
    return locals_, own, landed_ici, passed, landed_d2d


def _gather_start(*refs):
    locals_, own, _, _, _ = _gather_copies(*refs)
    for cp in locals_ + own:
        cp.start()


def _gather_forward(*refs):
    _, _, landed_ici, passed, _ = _gather_copies(*refs)
    for arrived, onward in zip(landed_ici, passed):
        arrived.wait_recv()
        onward.start()


def _gather_finish(*refs):
    locals_, own, _, passed, landed_d2d = _gather_copies(*refs)
    for cp in landed_d2d:
        cp.wait_recv()
    for cp in own + passed:
        cp.wait_send()
    for cp in locals_:
        cp.wait()


def _ssd_common(dt_ref, dtt_ref, al_r, al_c, bi_r, bi_c, off, rev):
    li = lax.broadcasted_iota(jnp.int32, (CHUNK, CHUNK), 0)
    si = lax.broadcasted_iota(jnp.int32, (CHUNK, CHUNK), 1)
    mask = (li <= si) if rev else (li >= si)
    mask_t = (li >= si) if rev else (li <= si)
    a_r = -jnp.exp(al_r[...])
    a_c = -jnp.exp(al_c[...])
    pre = dt_ref[:, off:off + N_HEADS] + bi_r[...]
    dt = _softplus(pre)
    cs = jnp.dot(mask.astype(F32), dt * a_r, precision=HIGH, preferred_element_type=F32)
    dt_t = _softplus(dtt_ref[off:off + N_HEADS, :] + bi_c[...])
    cs_t = jnp.dot(dt_t * a_c, mask_t.astype(F32), precision=HIGH, preferred_element_type=F32)
    tot = cs[0:1, :] if rev else cs[CHUNK - 1:CHUNK, :]
    return mask, mask_t, a_r, pre, dt, cs, cs_t, tot


def _ssd_fwd(xbc, dt_raw, dt_t, args_f, args_b, name, gather=()):
    s = xbc.shape[0]
    nc = s // CHUNK
    hp = D_INNER // N_HEADS
    hpg = N_HEADS // N_GROUPS
    gs = N_GROUPS * D_STATE
    ng = len(gather)
    fwd_step = (nc * 11) // 16

    def chunk(x_ref, b_ref, c_ref, dt_ref, dtt_ref, alr, alc, bir, bic, y_ref, st_ref, h_scr, off, rev):
        mask, _, _, _, dt, cs, cs_t, tot = _ssd_common(dt_ref, dtt_ref, alr, alc, bir, bic, off, rev)
        first = _first_head_lanes()
        xs = x_ref[...]
        ys = []
        for g in range(N_GROUPS):
            bg = b_ref[:, g * D_STATE:(g + 1) * D_STATE]
            cg = c_ref[:, g * D_STATE:(g + 1) * D_STATE]
            gm = lax.dot_general(cg, bg, NT, preferred_element_type=F32)
            hcat = h_scr[g]
            st_ref[0, g] = hcat
            ch = lax.dot_general(cg, hcat.astype(BF16), NT, preferred_element_type=F32)
            xdd = []
            for pr in range(hpg // 2):
                h0 = g * hpg + 2 * pr
                lanes = slice(h0 * hp, (h0 + 2) * hp)
                cols = [cs[:, h:h + 1] for h in (h0, h0 + 1)]
                pair = lambda a, b: jnp.where(first, a, b)
                xdf = xs[:, lanes].astype(F32) * pair(dt[:, h0:h0 + 1], dt[:, h0 + 1:h0 + 2])
                xdb = xdf.astype(BF16)
                yh = []
                for i, h in enumerate((h0, h0 + 1)):
                    lm = jnp.exp(jnp.where(mask, cols[i] - cs_t[h:h + 1, :], NEG))
                    yh.append(jnp.dot((gm * lm).astype(BF16), xdb, preferred_element_type=F32))
                ecs = pair(jnp.exp(cols[0]), jnp.exp(cols[1]))
                ys.append(pair(yh[0], yh[1]) + ecs * ch[:, 2 * pr * hp:(2 * pr + 2) * hp])
                dec = pair(jnp.exp(tot[:, h0:h0 + 1] - cols[0]), jnp.exp(tot[:, h0 + 1:h0 + 2] - cols[1]))
                xdd.append((xdf * dec).astype(BF16))
            snew = lax.dot_general(jnp.concatenate(xdd, axis=1), bg, TN, preferred_element_type=F32)
            for r in range(hpg):
                rs = slice(r * hp, (r + 1) * hp)
                h_scr[g, rs, :] = jnp.exp(tot[:, g * hpg + r:g * hpg + r + 1]) * hcat[rs] + snew[rs]
        y_ref[...] = jnp.concatenate(ys, axis=1)

    def body(*refs):
        in_f, in_b = refs[0:9], refs[9:18]
        g_src = refs[18:18 + ng]
        out_f, out_b = refs[18 + ng:20 + ng], refs[20 + ng:22 + ng]
        g_dst = refs[22 + ng:22 + 2 * ng]
        hs_f, hs_b = refs[22 + 2 * ng], refs[23 + 2 * ng]
        g_sems = refs[24 + 2 * ng:]
        step = pl.program_id(0)

        @pl.when(step == 0)
        def _():
            hs_f[...] = jnp.zeros_like(hs_f)
            hs_b[...] = jnp.zeros_like(hs_b)
            if ng:
                _gather_start(g_src, g_dst, *g_sems)

        chunk(*in_f, *out_f, hs_f, 0, False)
        chunk(*in_b, *out_b, hs_b, N_HEADS, True)

        if ng:
            @pl.when(step == fwd_step)
            def _():
                _gather_forward(g_src, g_dst, *g_sems)

            @pl.when(step == nc - 1)
            def _():
                _gather_finish(g_src, g_dst, *g_sems)

    small = lambda shape: pl.BlockSpec(shape, lambda c: (0, 0))

    def specs(cm):
        ins = [pl.BlockSpec((CHUNK, D_INNER), lambda c: (cm(c), 0)),
               pl.BlockSpec((CHUNK, gs), lambda c: (cm(c), D_INNER // gs)),
               pl.BlockSpec((CHUNK, gs), lambda c: (cm(c), D_INNER // gs + 1)),
               pl.BlockSpec((CHUNK, 128), lambda c: (cm(c), 0)),
               pl.BlockSpec((2 * N_HEADS, CHUNK), lambda c: (0, cm(c))),
               small((1, N_HEADS)), small((N_HEADS, 1)), small((1, N_HEADS)), small((N_HEADS, 1))]
        outs = [pl.BlockSpec((CHUNK, D_INNER), lambda c: (cm(c), 0)),
                pl.BlockSpec((1, N_GROUPS, hpg * hp, D_STATE), lambda c: (cm(c), 0, 0, 0))]
        return ins, outs

    ins_f, outs_f = specs(lambda c: c)
    ins_b, outs_b = specs(lambda c: nc - 1 - c)
    any_spec = pl.BlockSpec(memory_space=pl.ANY)
    one_dir = [SDS((s, D_INNER), F32), SDS((nc, N_GROUPS, hpg * hp, D_STATE), F32)]
    state = pltpu.VMEM((N_GROUPS, hpg * hp, D_STATE), F32)
    return pl.pallas_call(
        body, grid=(nc,), in_specs=ins_f + ins_b + [any_spec] * ng, out_specs=outs_f + outs_b + [any_spec] * ng,
        out_shape=one_dir + one_dir + [SDS((N_DEV,) + g.shape, g.dtype) for g in gather],
        scratch_shapes=[state, state] + (_exchange_sems(ng) if ng else []), name=name, compiler_params=_cp("arbitrary"))(
            xbc, xbc, xbc, dt_raw, dt_t, *args_f, xbc, xbc, xbc, dt_raw, dt_t, *args_b, *gather)


def _ssd_bwd(xbc, dt_raw, dt_t, al_r, al_c, bi_r, bi_c, states, dy, off, rev, name, exchange=()):
    s = xbc.shape[0]
    nc = s // CHUNK
    hp = D_INNER // N_HEADS
    gs = N_GROUPS * D_STATE
    hpg = N_HEADS // N_GROUPS
    cm = (lambda c: c) if rev else (lambda c: nc - 1 - c)
    nx = len(exchange)

    def body(*refs):
        x_ref, b_ref, c_ref, dt_ref, dtt_ref, alr, alc, bir, bic, st_ref, dy_ref = refs[:11]
        xch_src = refs[11:11 + nx]
        dx_ref, ddt_ref, dal_ref, dbi_ref = refs[11 + nx:15 + nx]
        xch_dst = refs[15 + nx:15 + 2 * nx]
        dh_scr = refs[15 + 2 * nx]
        xch_sems = refs[16 + 2 * nx:]

        @pl.when(pl.program_id(0) == 0)
        def _():
            dh_scr[...] = jnp.zeros_like(dh_scr)
            dal_ref[...] = jnp.zeros_like(dal_ref)
            dbi_ref[...] = jnp.zeros_like(dbi_ref)
            if nx:
                _exchange_start(xch_src, xch_dst, *xch_sems)

        if nx:
            @pl.when(pl.program_id(0) == nc - 1)
            def _():
                _exchange_wait(xch_src, xch_dst, *xch_sems)

        mask, mask_t, a_r, pre, dt, cs, cs_t, tot = _ssd_common(dt_ref, dtt_ref, alr, alc, bir, bic, off, rev)
        xs = x_ref[...]
        dyv = dy_ref[...]
        rows = lax.broadcasted_iota(jnp.int32, (CHUNK, 1), 0)
        end_row = (rows == 0) if rev else (rows == CHUNK - 1)
        lane_h = lax.broadcasted_iota(jnp.int32, (1, N_HEADS), 1)
        sub_h = lax.broadcasted_iota(jnp.int32, (N_HEADS, 1), 0)
        first = _first_head_lanes()
        dcs_all = jnp.zeros((CHUNK, N_HEADS), F32)
        colw_all = jnp.zeros((N_HEADS, CHUNK), F32)
        dxsum_all = jnp.zeros((CHUNK, N_HEADS), F32)
        dxs, dbs, dcs_out = [], [], []
        for g in range(N_GROUPS):
            bg = b_ref[:, g * D_STATE:(g + 1) * D_STATE]
            cg = c_ref[:, g * D_STATE:(g + 1) * D_STATE]
            gm = lax.dot_general(cg, bg, NT, preferred_element_type=F32)
            hcat = st_ref[0, g]
            dhcat = dh_scr[g]
            hb, dhb = hcat.astype(BF16), dhcat.astype(BF16)
            ch = lax.dot_general(cg, hb, NT, preferred_element_type=F32)
            z = lax.dot_general(bg, dhb, NT, preferred_element_type=F32)
            dg_sum = jnp.zeros((CHUNK, CHUNK), F32)
            dchs, xdds, t_hs = [], [], []
            for pr in range(hpg // 2):
                h0 = g * hpg + 2 * pr
                lanes = slice(h0 * hp, (h0 + 2) * hp)
                ps = slice(2 * pr * hp, (2 * pr + 2) * hp)
                pair = lambda a, b: jnp.where(first, a, b)
                cols = [cs[:, h:h + 1] for h in (h0, h0 + 1)]
                tots = [tot[:, h:h + 1] for h in (h0, h0 + 1)]
                xh = xs[:, lanes].astype(F32)
                dtc = pair(dt[:, h0:h0 + 1], dt[:, h0 + 1:h0 + 2])
                xdf = xh * dtc
                xd = xdf.astype(BF16)
                dyh = dyv[:, lanes]
                dyb = dyh.astype(BF16)
                ecs = pair(jnp.exp(cols[0]), jnp.exp(cols[1]))
                dec = pair(jnp.exp(tots[0] - cols[0]), jnp.exp(tots[1] - cols[1]))
                yoff_t = dyh * (ecs * ch[:, ps])
                dchs.append((ecs * dyh).astype(BF16))
                xdd = xdf * dec
                ddec_t = xdd * z[:, ps]
                xdds.append(xdd.astype(BF16))
                dxd_h = []
                for i, (h, keep) in enumerate(((h0, first), (h0 + 1, ~first))):
                    rs = slice((2 * pr + i) * hp, (2 * pr + i + 1) * hp)
                    lm = jnp.exp(jnp.where(mask, cols[i] - cs_t[h:h + 1, :], NEG))
                    m = gm * lm
                    t_h = jnp.exp(tots[i])
                    dxd_h.append(lax.dot_general(m.astype(BF16), dyb, TN, preferred_element_type=F32))
                    dm = lax.dot_general(_only(dyb, keep), xd, NT, preferred_element_type=F32)
                    dg_sum = dg_sum + dm * lm
                    w = dm * m
                    ddec = jnp.sum(_only(ddec_t, keep), axis=1, keepdims=True)
                    dtot = (jnp.sum(jnp.sum(dhcat[rs] * hcat[rs], axis=1, keepdims=True), axis=0, keepdims=True) * t_h
                            + jnp.sum(ddec, axis=0, keepdims=True))
                    t_hs.append(t_h)
                    dcs = jnp.sum(_only(yoff_t, keep), axis=1, keepdims=True) + jnp.sum(w, axis=1, keepdims=True) - ddec
                    dcs = dcs + jnp.where(end_row, dtot, 0.0)
                    colw_all = colw_all + (sub_h == h).astype(F32) * jnp.sum(w, axis=0, keepdims=True)
                    dcs_all = dcs_all + dcs * (lane_h == h).astype(F32)
                dxd = pair(dxd_h[0], dxd_h[1]) + dec * z[:, ps]
                dxx = dxd * xh
                for i, (h, keep) in enumerate(((h0, first), (h0 + 1, ~first))):
                    dxsum_all = dxsum_all + jnp.sum(_only(dxx, keep), axis=1, keepdims=True) * (lane_h == h).astype(F32)
                dxs.append(dxd * dtc)
            dgs = dg_sum.astype(BF16)
            dchc = jnp.concatenate(dchs, axis=1)
            dcs_out.append(jnp.dot(dgs, bg, preferred_element_type=F32) + jnp.dot(dchc, hb, preferred_element_type=F32))
            dbs.append(lax.dot_general(dgs, cg, TN, preferred_element_type=F32)
                       + jnp.dot(jnp.concatenate(xdds, axis=1), dhb, preferred_element_type=F32))
            dh_in = lax.dot_general(dchc, cg, TN, preferred_element_type=F32)
            for r in range(hpg):
                rs = slice(r * hp, (r + 1) * hp)
                dh_scr[g, rs, :] = dh_in[rs] + t_hs[r] * dhcat[rs]
        dx_ref[...] = jnp.concatenate(dxs + dbs + dcs_out, axis=1)
        mt = mask_t.astype(F32)
        da = (jnp.dot(mt, dcs_all, precision=HIGH, preferred_element_type=F32)
              - lax.dot_general(mt, colw_all, NT, precision=HIGH, preferred_element_type=F32))
        dal_ref[...] += jnp.sum(da * dt, axis=0, keepdims=True) * a_r
        ddt_raw = (da * a_r + dxsum_all) * _sigmoid(pre)
        ddt_ref[...] = ddt_raw
        dbi_ref[...] += jnp.sum(ddt_raw, axis=0, keepdims=True)

    small = lambda shape: pl.BlockSpec(shape, lambda c: (0, 0))
    in_specs = [pl.BlockSpec((CHUNK, D_INNER), lambda c: (cm(c), 0)),
                pl.BlockSpec((CHUNK, gs), lambda c: (cm(c), D_INNER // gs)),
                pl.BlockSpec((CHUNK, gs), lambda c: (cm(c), D_INNER // gs + 1)),
                pl.BlockSpec((CHUNK, 128), lambda c: (cm(c), 0)),
                pl.BlockSpec((2 * N_HEADS, CHUNK), lambda c: (0, cm(c))),
                small((1, N_HEADS)), small((N_HEADS, 1)), small((1, N_HEADS)), small((N_HEADS, 1)),
                pl.BlockSpec((1, N_GROUPS, hpg * hp, D_STATE), lambda c: (cm(c), 0, 0, 0)),
                pl.BlockSpec((CHUNK, D_INNER), lambda c: (cm(c), 0))]
    any_spec = pl.BlockSpec(memory_space=pl.ANY)
    return pl.pallas_call(
        body, grid=(nc,), in_specs=in_specs + [any_spec] * nx,
        out_specs=[pl.BlockSpec((CHUNK, XBC), lambda c: (cm(c), 0)), pl.BlockSpec((CHUNK, N_HEADS), lambda c: (cm(c), 0)),
                   small((1, N_HEADS)), small((1, N_HEADS))] + [any_spec] * nx,
        out_shape=[SDS((s, XBC), F32), SDS((s, N_HEADS), F32), SDS((1, N_HEADS), F32), SDS((1, N_HEADS), F32)]
        + [SDS(a.shape, a.dtype) for a in exchange],
        scratch_shapes=[pltpu.VMEM((N_GROUPS, hpg * hp, D_STATE), F32)] + (_exchange_sems(nx) if nx else []),
        name=name, compiler_params=_cp("arbitrary"))(xbc, xbc, xbc, dt_raw, dt_t, al_r, al_c, bi_r, bi_c, states, dy, *exchange)


def _gate_fwd(yf, yb, xbc, proj, dskip_x, norm_w, mix, mix_t, name):
    s = yf.shape[0]
    tm = _pick(s, (256, 128))
    gw = D_INNER // N_GROUPS
    zc = 3 * D_MODEL // D_INNER

    def body(yf_ref, yb_ref, x_ref, z_ref, d_ref, w_ref, _m, _mt, o_ref, ot_ref):
        y = yf_ref[...] + yb_ref[...] + d_ref[...] * x_ref[...].astype(F32)
        z = z_ref[...].astype(F32)
        gt = y * (z * _sigmoid(z))
        outs = []
        for g in range(N_GROUPS):
            gg = gt[:, g * gw:(g + 1) * gw]
            outs.append(gg * lax.rsqrt(jnp.mean(gg * gg, axis=-1, keepdims=True) + EPS))
        out = (jnp.concatenate(outs, axis=1) * w_ref[...]).astype(o_ref.dtype)
        o_ref[...] = out
        ot_ref[...] = out.T

    row = pl.BlockSpec((tm, D_INNER), lambda i: (i, 0))
    vec = pl.BlockSpec((1, D_INNER), lambda i: (0, 0))
    any_spec = pl.BlockSpec(memory_space=pl.ANY)
    cb = (mix.shape[1] - D_INNER) // D_INNER
    return pl.pallas_call(
        body, grid=(s // tm,), in_specs=[row, row, row, pl.BlockSpec((tm, D_INNER), lambda i: (i, zc)), vec, vec, any_spec, any_spec],
        out_specs=[pl.BlockSpec((tm, D_INNER), lambda i: (i, cb)), pl.BlockSpec((D_INNER, tm), lambda i: (cb, i))],
        out_shape=[SDS(mix.shape, BF16), SDS(mix_t.shape, BF16)], input_output_aliases={6: 0, 7: 1},
        name=name, compiler_params=_cp("parallel"))(yf, yb, xbc, proj, dskip_x, norm_w, mix, mix_t)


def _gate_bwd(yf, yb, xbc, proj, dskip_x, norm_w, dmix, dproj, name):
    s = yf.shape[0]
    tm = _pick(s, (256, 128))
    gw = D_INNER // N_GROUPS
    zc = 3 * D_MODEL // D_INNER

    def body(yf_ref, yb_ref, x_ref, z_ref, d_ref, w_ref, do_ref, _, dy_ref, dz_ref, dxs_ref, dw_ref, dd_ref):
        xf = x_ref[...].astype(F32)
        y = yf_ref[...] + yb_ref[...] + d_ref[...] * xf
        z = z_ref[...].astype(F32)
        sg = _sigmoid(z)
        sz = z * sg
        gt = y * sz
        do = do_ref[...].astype(F32)
        dgh = do * w_ref[...]
        ghs, dgts = [], []
        for g in range(N_GROUPS):
            gg = gt[:, g * gw:(g + 1) * gw]
            r = lax.rsqrt(jnp.mean(gg * gg, axis=-1, keepdims=True) + EPS)
            gh = gg * r
            dg = dgh[:, g * gw:(g + 1) * gw]
            ghs.append(gh)
            dgts.append(r * (dg - gh * jnp.mean(dg * gh, axis=-1, keepdims=True)))
        ghat = jnp.concatenate(ghs, axis=1)
        dgt = jnp.concatenate(dgts, axis=1)
        dy = dgt * sz
        dy_ref[...] = dy
        dz_ref[...] = (dgt * y * (sg * (1.0 + z * (1.0 - sg)))).astype(dz_ref.dtype)
        dxs_ref[...] = dy * d_ref[...]

        @pl.when(pl.program_id(0) == 0)
        def _():
            dw_ref[...] = jnp.zeros_like(dw_ref)
            dd_ref[...] = jnp.zeros_like(dd_ref)

        dw_ref[...] += jnp.sum(do * ghat, axis=0, keepdims=True)
        dd_ref[...] += jnp.sum(dy * xf, axis=0, keepdims=True)

    row = pl.BlockSpec((tm, D_INNER), lambda i: (i, 0))
    vec = pl.BlockSpec((1, D_INNER), lambda i: (0, 0))
    dz_shape, _, more, more_specs, alias = _slab(s, D_INNER, BF16, (dproj.shape[1], zc * D_INNER, dproj), 7, out_idx=1)
    return pl.pallas_call(
        body, grid=(s // tm,),
        in_specs=[row, row, row, pl.BlockSpec((tm, D_INNER), lambda i: (i, zc)), vec, vec, pl.BlockSpec((tm, D_INNER), lambda i: (i, 1))] + more_specs,
        out_specs=[row, pl.BlockSpec((tm, D_INNER), lambda i: (i, zc)), row, vec, vec],
        out_shape=[SDS((s, D_INNER), F32), dz_shape, SDS((s, D_INNER), F32), SDS((1, D_INNER), F32), SDS((1, D_INNER), F32)],
        input_output_aliases=alias, name=name, compiler_params=_cp("arbitrary"))(yf, yb, xbc, proj, dskip_x, norm_w, dmix, *more)


def _adamw(parts, w, m, v, name):
    r, c = w.shape
    tr = _pick(r, (256, 352, 128))

    def body(p_ref, w_ref, m_ref, v_ref, g_ref, d_ref, nm_ref, nv_ref):
        g = p_ref[0].astype(F32)
        for i in range(1, N_DEV):
            g = g + p_ref[i].astype(F32)
        mn = B1 * m_ref[...] + (1.0 - B1) * g
        vn = B2 * v_ref[...] + (1.0 - B2) * (g * g)
        m_hat = mn / (1.0 - B1 ** STEP)
        v_hat = vn / (1.0 - B2 ** STEP)
        g_ref[...] = g
        d_ref[...] = -LR * (m_hat / (jnp.sqrt(v_hat) + AEPS) + WD * w_ref[...])
        nm_ref[...] = mn
        nv_ref[...] = vn

    blk = pl.BlockSpec((tr, c), lambda i: (i, 0))
    return pl.pallas_call(
        body, grid=(r // tr,), in_specs=[pl.BlockSpec((N_DEV, tr, c), lambda i: (0, i, 0)), blk, blk, blk],
        out_specs=[blk, blk, blk, blk], out_shape=[SDS((r, c), F32)] * 4, name=name, compiler_params=_cp("parallel"))(parts, w, m, v)


def _sum_parts(parts, name):
    _, r, c = parts.shape

    def body(p_ref, o_ref):
        g = p_ref[0]
        for i in range(1, N_DEV):
            g = g + p_ref[i]
        o_ref[...] = g

    return pl.pallas_call(body, out_shape=SDS((r, c), F32), name=name)(parts)


def _adamw_small(g, w, m, v, name):
    def body(g_ref, w_ref, m_ref, v_ref, d_ref, nm_ref, nv_ref):
        gv = g_ref[...]
        mn = B1 * m_ref[...] + (1.0 - B1) * gv
        vn = B2 * v_ref[...] + (1.0 - B2) * (gv * gv)
        m_hat = mn / (1.0 - B1 ** STEP)
        v_hat = vn / (1.0 - B2 ** STEP)
        d_ref[...] = -LR * (m_hat / (jnp.sqrt(v_hat) + AEPS) + WD * w_ref[...])
        nm_ref[...] = mn
        nv_ref[...] = vn

    return pl.pallas_call(body, out_shape=[SDS(g.shape, F32)] * 3, name=name)(g, w, m, v)


def _my_index():
    return 4 * lax.axis_index("x") + 2 * lax.axis_index("y") + lax.axis_index("c")


def _all_gather(shards, name):
    n = len(shards)

    def body(*refs):
        srcs, outs = refs[:n], refs[n:2 * n]
        _gather_start(srcs, outs, *refs[2 * n:])
        _gather_forward(srcs, outs, *refs[2 * n:])
        _gather_finish(srcs, outs, *refs[2 * n:])

    any_spec = pl.BlockSpec(memory_space=pl.ANY)
    return pl.pallas_call(
        body, in_specs=[any_spec] * n, out_specs=[any_spec] * n,
        out_shape=[SDS((N_DEV,) + s.shape, s.dtype) for s in shards], scratch_shapes=_exchange_sems(n), name=name)(*shards)


def _exchange(arrays, name):
    n = len(arrays)

    def body(*refs):
        srcs, outs = refs[:n], refs[n:2 * n]
        _exchange_start(srcs, outs, *refs[2 * n:])
        _exchange_wait(srcs, outs, *refs[2 * n:])

    any_spec = pl.BlockSpec(memory_space=pl.ANY)
    return pl.pallas_call(
        body, in_specs=[any_spec] * n, out_specs=[any_spec] * n,
        out_shape=[SDS(a.shape, a.dtype) for a in arrays], scratch_shapes=_exchange_sems(n), name=name)(*arrays)


def _to_pattern(t, d):
    if d == 1:
        return t
    s, w = t.shape
    return t.reshape(s // d, d, w).transpose(1, 0, 2).reshape(s, w)


def _from_pattern(t, d):
    if d == 1:
        return t
    s, w = t.shape
    return t.reshape(d, s // d, w).transpose(1, 0, 2).reshape(s, w)


def _pad_lanes(t, n):
    return jnp.pad(t, ((0, 0), (0, n - t.shape[1])))


def _to_shards(g, axis):
    r, c = g.shape
    if axis == 0:
        return g.reshape(N_DEV, r // N_DEV, c)
    return g.reshape(r, N_DEV, c // N_DEV).transpose(1, 0, 2)


def _local_step(x, target, p, late_shards=(), early_exchange=True):
    s = x.shape[0]
    tabs_f, tabs_b = _rope_tables(s)
    expand = jnp.asarray(np.repeat(np.eye(N_HEADS, dtype=np.float32), HEAD_DIM, axis=1))
    w_main, w_dt = p["w_in"][:, :MAIN_W], _pad_lanes(p["w_in"][:, MAIN_W:], 128)
    al_r = {"f": p["a_log_f"], "b": p["a_log_b"]}
    bi_r = {"f": p["dt_bias_f"], "b": p["dt_bias_b"]}
    dskip_x = jnp.repeat(p["d_skip"], D_INNER // N_HEADS, axis=1)
    ssm_w3, ffn_w3 = p["ssm_conv_w"].T, p["ffn_conv_w"].T

    h1, h1t = _rmsnorm_fwd(x, p["norm1_w"], "norm1_fwd")
    proj = _matmul(h1, w_main, name="in_proj")
    dt_raw = _matmul(h1, w_dt, name="in_proj_dt", out_dtype=F32)
    dt_t = dt_raw[:, :2 * N_HEADS].T
    perms = {d: _perm_matrices(d) for d in DILATIONS[1:]}
    qkv = _rope_fwd(proj, tabs_f, perms, "rope_fwd")
    v_col = 2
    os_, lses = [], []
    for d, qkv_p in zip(DILATIONS, qkv):
        o_p, lse_p = _attn_fwd(qkv_p, qkv_p, v_col, s // d, f"attn_fwd_d{d}")
        os_.append(o_p)
        lses.append(_from_pattern(lse_p, d))
    mix, mix_t, lse_tot = _attn_combine(os_, lses, perms, expand, "attn_combine")

    xbc = _conv(proj, 3 * D_MODEL + D_INNER, XBC, ssm_w3, p["ssm_conv_b"], True, "ssm_conv_fwd")
    col = lambda r: r.reshape(N_HEADS, 1)
    ssd_args = {k: (al_r[k], col(al_r[k]), bi_r[k], col(bi_r[k])) for k in ("f", "b")}
    yf, st_f, yb, st_b, *got = _ssd_fwd(xbc, dt_raw, dt_t, ssd_args["f"], ssd_args["b"], "ssd_fwd", gather=late_shards)
    if late_shards:
        p = dict(p, w_out=got[0].reshape(2 * D_MODEL, D_MODEL), w_down=got[2].reshape(D_FF, D_MODEL),
                 w_up=got[1].transpose(1, 0, 2).reshape(D_MODEL, 2 * D_FF))
    mix, mix_t = _gate_fwd(yf, yb, xbc, proj, dskip_x, p["ssm_norm_w"], mix, mix_t, "ssm_gate_fwd")

    x2 =_matmul(mix, p["w_out"], name="out_proj", out_dtype=F32, residual=x)
    h2, h2t = _rmsnorm_fwd(x2, p["norm2_w"], "norm2_fwd")
    u = _matmul(h2, p["w_up"], name="ffn_up")
    act, act_t = _ffn_gate_fwd(u, ffn_w3, p["ffn_conv_b"], "ffn_gate_fwd")
    x3 = _matmul(act, p["w_down"], name="ffn_down", out_dtype=F32, residual=x2)

    dx3, dx3b, g_final, loss = _final_norm_loss(x3, p["final_norm_w"].reshape(1, D_MODEL), target, "final_norm_loss")
    g_w_down = _matmul(act_t, dx3b, name="dw_down")
    dact = _matmul(dx3b, p["w_down"], name="d_act", trans_b=True)
    dug, duu, dwg, dwu, dbg, dbu = _ffn_gate_bwd(u, ffn_w3, p["ffn_conv_b"], dact, "ffn_gate_bwd")
    du = _conv(dug, 0, D_FF, ffn_w3, p["ffn_conv_b"], False, "ffn_conv_bwd_gate", slab=(2 * D_FF, 0, None), transpose=True)
    du = _conv(duu, 0, D_FF, ffn_w3, p["ffn_conv_b"], False, "ffn_conv_bwd_up", slab=(2 * D_FF, D_FF, du), transpose=True, wcol0=D_FF)
    g_w_up = _matmul(h2t, du, name="dw_up")
    none_a, none_w = jnp.zeros((s, 128), BF16), jnp.zeros((D_MODEL, 128), BF16)
    dx2, dx2b, g_norm2 = _proj_norm_bwd(du, p["w_up"], none_a, none_w, x2, p["norm2_w"], dx3, "ffn_up_norm2_bwd")
    g_w_out = _matmul(mix_t, dx2b, name="dw_out")
    dmix = _matmul(dx2b, p["w_out"], name="d_mix", trans_b=True)

    delta, do_pat = _attn_delta(dmix, mix, expand.T, perms, "attn_delta")
    dqs, dks, dvs = [], [], []
    for d, qkv_p, do_p in zip(DILATIONS, qkv, [dmix] + do_pat):
        lse_p, dl_p = _to_pattern(lse_tot, d), _to_pattern(delta, d)
        dqs.append(_attn_bwd_dq(qkv_p, qkv_p, v_col, do_p, lse_p, dl_p, s // d, f"attn_bwd_dq_d{d}"))
        dk, dv = _attn_bwd_dkv(qkv_p, qkv_p, v_col, do_p, lse_p.T, dl_p.T, s // d, f"attn_bwd_dkv_d{d}")
        dks.append(dk)
        dvs.append(dv)
    dproj = _sum3_rope(dqs, perms, tabs_b, "rope_bwd_q", slab=(MAIN_W, 0, None))
    dproj = _sum3_rope(dks, perms, tabs_b, "rope_bwd_k", slab=(MAIN_W, D_MODEL, dproj))
    dproj = _sum3_rope(dvs, perms, None, "sum_dv", slab=(MAIN_W, 2 * D_MODEL, dproj))

    dy, dproj, dxs_skip, g_ssm_norm, g_dskip_lanes = _gate_bwd(yf, yb, xbc, proj, dskip_x, p["ssm_norm_w"], dmix, dproj, "ssm_gate_bwd")
    early = [_to_shards(g_w_out, 0), _to_shards(g_w_up, 1), _to_shards(g_w_down, 0)] if early_exchange else []
    dxbc_f, ddt_f, g_al_f, g_bi_f, *got = _ssd_bwd(xbc, dt_raw, dt_t, *ssd_args["f"], st_f, dy, 0, False, "ssd_bwd_f", exchange=early)
    if early_exchange:
        g_w_out, g_w_up, g_w_down = got
    dxbc_b, ddt_b, g_al_b, g_bi_b = _ssd_bwd(xbc, dt_raw, dt_t, *ssd_args["b"], st_b, dy, N_HEADS, True, "ssd_bwd_b")
    dpre, g_ssm_w3, g_ssm_cb = _conv_silu_bwd(proj, 3 * D_MODEL + D_INNER, XBC, ssm_w3, p["ssm_conv_b"],
                                              [dxbc_f, dxbc_b, dxs_skip], [XBC, XBC, D_INNER], "ssm_conv_bwd")
    dproj = _conv(dpre, 0, XBC, ssm_w3, p["ssm_conv_b"], False, "ssm_conv_bwd_x", transpose=True,
                  slab=(MAIN_W, 3 * D_MODEL + D_INNER, dproj))

    ddt =_pad_lanes(jnp.concatenate([ddt_f, ddt_b], axis=1), 128).astype(BF16)
    g_w_main = _matmul(h1t, dproj, name="dw_in")
    g_w_dt = _matmul(h1t, ddt, name="dw_in_dt")
    g_w_in = jnp.concatenate([g_w_main, g_w_dt[:, :2 * N_HEADS]], axis=1)
    late = [_to_shards(g_w_in, 1)] if early_exchange else []
    grad_x, _, g_norm1, *got = _proj_norm_bwd(dproj, w_main, ddt, w_dt, x, p["norm1_w"], dx2, "in_proj_norm1_bwd", exchange=late)
    if early_exchange:
        g_w_in = got[0]

    g_dskip = jnp.sum(g_dskip_lanes.reshape(N_HEADS, D_INNER // N_HEADS), axis=1).reshape(1, N_HEADS)
    small = {
        "norm1_w": g_norm1, "ssm_conv_w": g_ssm_w3.T, "ssm_conv_b": g_ssm_cb, "a_log_f": g_al_f, "a_log_b": g_al_b,
        "dt_bias_f": g_bi_f, "dt_bias_b": g_bi_b, "d_skip": g_dskip, "ssm_norm_w": g_ssm_norm, "norm2_w": g_norm2,
        "ffn_conv_w": jnp.concatenate([dwg, dwu], axis=1).T, "ffn_conv_b": jnp.concatenate([dbg, dbu], axis=1), "final_norm_w": g_final,
    }
    big = {"w_in": g_w_in, "w_out": g_w_out, "w_up": g_w_up, "w_down": g_w_down}
    return loss[0, 0], grad_x, big, small


SMALL_ORDER = ("norm1_w", "ssm_conv_w", "ssm_conv_b", "a_log_f", "a_log_b", "dt_bias_f", "dt_bias_b", "d_skip",
               "ssm_norm_w", "norm2_w", "ffn_conv_w", "ffn_conv_b", "final_norm_w")
SHARDED_SMALL = ("ssm_conv_w", "ffn_conv_w")
BIG_ORDER = ("w_in", "w_out", "w_up", "w_down")


def _pack(vals):
    rows = []
    for v in vals:
        f = v.reshape(-1).astype(F32)
        n = -(-f.shape[0] // 128) * 128
        rows.append(jnp.pad(f, (0, n - f.shape[0])).reshape(-1, 128))
    out = jnp.concatenate(rows, axis=0)
    pad = -out.shape[0] % 8
    return jnp.pad(out, ((0, pad), (0, 0)))


def _unpack(packed, shapes):
    out, r = [], 0
    for shp in shapes:
        n = math.prod(shp)
        nr = -(-n // 128)
        out.append(packed[r:r + nr].reshape(-1)[:n].reshape(shp))
        r += nr
    return out


def kernel(x, norm1_w, w_in, ssm_conv_w, ssm_conv_b, a_log_f, a_log_b, dt_bias_f, dt_bias_b, d_skip, ssm_norm_w, w_out, norm2_w, w_up, ffn_conv_w, ffn_conv_b, w_down, final_norm_w, loss_target, m_norm1_w, m_w_in, m_ssm_conv_w, m_ssm_conv_b, m_a_log_f, m_a_log_b, m_dt_bias_f, m_dt_bias_b, m_d_skip, m_ssm_norm_w, m_w_out, m_norm2_w, m_w_up, m_ffn_conv_w, m_ffn_conv_b, m_w_down, m_final_norm_w, v_norm1_w, v_w_in, v_ssm_conv_w, v_ssm_conv_b, v_a_log_f, v_a_log_b, v_dt_bias_f, v_dt_bias_b, v_d_skip, v_ssm_norm_w, v_w_out, v_norm2_w, v_w_up, v_ffn_conv_w, v_ffn_conv_b, v_w_down, v_final_norm_w):
    w = dict(norm1_w=norm1_w, w_in=w_in, ssm_conv_w=ssm_conv_w, ssm_conv_b=ssm_conv_b, a_log_f=a_log_f, a_log_b=a_log_b,
             dt_bias_f=dt_bias_f, dt_bias_b=dt_bias_b, d_skip=d_skip, ssm_norm_w=ssm_norm_w, w_out=w_out, norm2_w=norm2_w,
             w_up=w_up, ffn_conv_w=ffn_conv_w, ffn_conv_b=ffn_conv_b, w_down=w_down, final_norm_w=final_norm_w)
    mo = dict(norm1_w=m_norm1_w, w_in=m_w_in, ssm_conv_w=m_ssm_conv_w, ssm_conv_b=m_ssm_conv_b, a_log_f=m_a_log_f, a_log_b=m_a_log_b,
              dt_bias_f=m_dt_bias_f, dt_bias_b=m_dt_bias_b, d_skip=m_d_skip, ssm_norm_w=m_ssm_norm_w, w_out=m_w_out, norm2_w=m_norm2_w,
              w_up=m_w_up, ffn_conv_w=m_ffn_conv_w, ffn_conv_b=m_ffn_conv_b, w_down=m_w_down, final_norm_w=m_final_norm_w)
    vo = dict(norm1_w=v_norm1_w, w_in=v_w_in, ssm_conv_w=v_ssm_conv_w, ssm_conv_b=v_ssm_conv_b, a_log_f=v_a_log_f, a_log_b=v_a_log_b,
              dt_bias_f=v_dt_bias_f, dt_bias_b=v_dt_bias_b, d_skip=v_d_skip, ssm_norm_w=v_ssm_norm_w, w_out=v_w_out, norm2_w=v_norm2_w,
              w_up=v_w_up, ffn_conv_w=v_ffn_conv_w, ffn_conv_b=v_ffn_conv_b, w_down=v_w_down, final_norm_w=v_final_norm_w)
    me = _my_index()

    g_in, g_conv = _all_gather([w["w_in"][0].astype(BF16), _pack([w["ssm_conv_w"][0], w["ffn_conv_w"][0]])], "w_in_all_gather")
    conv_rows = [_unpack(g_conv[i], [ssm_conv_w.shape[1:], ffn_conv_w.shape[1:]]) for i in range(N_DEV)]
    full = {
        "w_in": g_in.transpose(1, 0, 2).reshape(D_MODEL, N_DEV * w_in.shape[2]),
        "ssm_conv_w": jnp.concatenate([c[0] for c in conv_rows], axis=0),
        "ffn_conv_w": jnp.concatenate([c[1] for c in conv_rows], axis=0),
    }
    for k in ("norm1_w", "ssm_conv_b", "a_log_f", "a_log_b", "dt_bias_f", "dt_bias_b", "d_skip", "ssm_norm_w", "norm2_w", "ffn_conv_b",
              "final_norm_w"):
        full[k] = w[k]
    late = [w["w_out"][0].astype(BF16), w["w_up"][0].astype(BF16), w["w_down"][0].astype(BF16)]

    loss_part, grad_x, big, small = _local_step(x[0], loss_target[0], full, late)

    small_shapes = [(1,)] + [small[k].shape for k in SMALL_ORDER]
    packed = _pack([loss_part] + [small[k] for k in SMALL_ORDER])
    out_small = jnp.broadcast_to(packed[None], (N_DEV,) + packed.shape)
    (r_small,) = _exchange([out_small], "small_grads_exchange")
    r_in, r_out, r_up, r_down = big["w_in"], big["w_out"], big["w_up"], big["w_down"]

    outs_g, outs_d, outs_m, outs_v = {}, {}, {}, {}
    for k, parts in zip(BIG_ORDER, (r_in, r_out, r_up, r_down)):
        g, dlt, nm, nv = _adamw(parts, w[k][0], mo[k][0], vo[k][0], f"adamw_{k}")
        outs_g[k], outs_d[k], outs_m[k], outs_v[k] = g[None], dlt[None], nm[None], nv[None]
    tot = _unpack(_sum_parts(r_small, "small_grads_sum"), small_shapes)
    loss = tot[0][0]
    gs = dict(zip(SMALL_ORDER, tot[1:]))
    g_own = {}
    for k in SMALL_ORDER:
        if k in SHARDED_SMALL:
            rows = w[k].shape[1]
            g_own[k] = lax.dynamic_slice_in_dim(gs[k], me * rows, rows, axis=0)[None]
        else:
            g_own[k] = gs[k].reshape(w[k].shape)
    shapes = [w[k].shape for k in SMALL_ORDER]
    d_s, m_s, v_s = _adamw_small(_pack([g_own[k] for k in SMALL_ORDER]), _pack([w[k] for k in SMALL_ORDER]),
                                 _pack([mo[k] for k in SMALL_ORDER]), _pack([vo[k] for k in SMALL_ORDER]), "adamw_small")
    for k, a, b, c in zip(SMALL_ORDER, _unpack(d_s, shapes), _unpack(m_s, shapes), _unpack(v_s, shapes)):
        outs_g[k], outs_d[k], outs_m[k], outs_v[k] = g_own[k], a, b, c

    order = ("norm1_w", "w_in", "ssm_conv_w", "ssm_conv_b", "a_log_f", "a_log_b", "dt_bias_f", "dt_bias_b", "d_skip", "ssm_norm_w",
             "w_out", "norm2_w", "w_up", "ffn_conv_w", "ffn_conv_b", "w_down", "final_norm_w")
    return (loss, grad_x[None], *[outs_g[k] for k in order], *[outs_d[k] for k in order],
            *[outs_m[k] for k in order], *[outs_v[k] for k in order])
```

```python
import math

import numpy as np
import jax
import jax.numpy as jnp
from jax import lax
from jax.experimental import pallas as pl
from jax.experimental.pallas import tpu as pltpu

F32 = jnp.float32
BF16 = jnp.bfloat16
SDS = jax.ShapeDtypeStruct

N_DEV = 8
D_MODEL = 1024
N_HEADS = 16
HEAD_DIM = 64
ROPE_DIM = 16
ROPE_THETA = 500000.0
DILATIONS = (1, 4, 16)
BAND_HALF = 64
D_INNER = 1024
N_GROUPS = 4
D_STATE = 128
CHUNK = 128
XBC = D_INNER + 2 * N_GROUPS * D_STATE
D_FF = 2816
MAIN_W = 3 * D_MODEL + D_INNER + XBC
EPS = 1e-6
LR, B1, B2, AEPS, WD, STEP = 0.001, 0.9, 0.999, 1e-08, 0.01, 10
NEG = -1e30
VMEM_LIMIT = 56 * 1024 * 1024
MESH = pl.DeviceIdType.MESH
HIGH = lax.Precision.HIGHEST
NT = (((1,), (1,)), ((), ()))
TN = (((0,), (0,)), ((), ()))


def _cp(*sem):
    return pltpu.CompilerParams(dimension_semantics=sem, vmem_limit_bytes=VMEM_LIMIT)


def _pick(n, cands):
    for c in cands:
        if n % c == 0:
            return c
    raise ValueError(f"no tile for {n}")


def _sigmoid(x):
    return 1.0 / (1.0 + jnp.exp(-x))


def _softplus(x):
    return jnp.maximum(x, 0.0) + jnp.log1p(jnp.exp(-jnp.abs(x)))


def _slab(s, width, dtype, slab, n_in, out_idx=0):
    if slab is None:
        return SDS((s, width), dtype), 0, [], [], {}
    total, col0, into = slab
    if into is None:
        return SDS((s, total), dtype), col0, [], [], {}
    return SDS((s, total), dtype), col0, [into], [pl.BlockSpec(memory_space=pl.ANY)], {n_in: out_idx}


def _matmul(a, b, *, name, trans_b=False, out_dtype=BF16, residual=None):
    m, k = a.shape
    n = b.shape[0] if trans_b else b.shape[1]
    tk = k if k <= 2048 else _pick(k, (2048, 1408, 1024, 512))
    nk = k // tk
    if nk == 1:
        tm = _pick(m, (2048, 1408, 1024, 512, 256, 128))
        tn = _pick(n, (512, 256, 128))
    else:
        tm = _pick(m, (1024, 1408, 512, 256, 128))
        tn = _pick(n, (1024, 1408, 512, 256, 128))
    dn = NT if trans_b else (((1,), (0,)), ((), ()))

    def body(*refs):
        a_ref, b_ref = refs[0], refs[1]
        o_ref, acc = refs[-2], refs[-1]
        kk = pl.program_id(2)

        @pl.when(kk == 0)
        def _():
            acc[...] = jnp.zeros_like(acc)

        acc[...] += lax.dot_general(a_ref[...], b_ref[...], dn, preferred_element_type=F32)

        @pl.when(kk == nk - 1)
        def _():
            r = acc[...]
            if residual is not None:
                r = r + refs[2][...].astype(F32)
            o_ref[...] = r.astype(o_ref.dtype)

    in_specs = [pl.BlockSpec((tm, tk), lambda i, j, kk: (i, kk)),
                pl.BlockSpec((tn, tk), lambda i, j, kk: (j, kk)) if trans_b else pl.BlockSpec((tk, tn), lambda i, j, kk: (kk, j))]
    args = [a, b]
    if residual is not None:
        in_specs.append(pl.BlockSpec((tm, tn), lambda i, j, kk: (i, j)))
        args.append(residual)
    return pl.pallas_call(
        body, grid=(m // tm, n // tn, nk), in_specs=in_specs,
        out_specs=pl.BlockSpec((tm, tn), lambda i, j, kk: (i, j)),
        out_shape=SDS((m, n), out_dtype), scratch_shapes=[pltpu.VMEM((tm, tn), F32)],
        name=name, compiler_params=_cp("parallel", "parallel", "arbitrary"))(*args)


def _rmsnorm_fwd(x, w, name):
    s, d = x.shape
    tm = _pick(s, (512, 128))

    def body(x_ref, w_ref, o_ref, ot_ref):
        xf = x_ref[...]
        r = lax.rsqrt(jnp.mean(xf * xf, axis=-1, keepdims=True) + EPS)
        out = (xf * r * w_ref[...]).astype(o_ref.dtype)
        o_ref[...] = out
        ot_ref[...] = out.T

    return pl.pallas_call(
        body, grid=(s // tm,), in_specs=[pl.BlockSpec((tm, d), lambda i: (i, 0)), pl.BlockSpec((1, d), lambda i: (0, 0))],
        out_specs=[pl.BlockSpec((tm, d), lambda i: (i, 0)), pl.BlockSpec((d, tm), lambda i: (0, i))],
        out_shape=[SDS((s, d), BF16), SDS((d, s), BF16)], name=name, compiler_params=_cp("parallel"))(x, w)


def _proj_norm_bwd(da, wt, da2, wt2, x, w, dres, name, exchange=()):
    s, k = da.shape
    d = x.shape[1]
    tm = _pick(s, (512, 128))
    tk = _pick(k, (2048, 1408, 1024, 512))
    nk = k // tk
    nx = len(exchange)
    grid = (s // tm, nk)

    def body(*refs):
        a_ref, b_ref, a2_ref, b2_ref, x_ref, w_ref, dres_ref = refs[:7]
        xch_src = refs[7:7 + nx]
        dx_ref, dxb_ref, dw_ref = refs[7 + nx:10 + nx]
        xch_dst = refs[10 + nx:10 + 2 * nx]
        acc = refs[10 + 2 * nx]
        xch_sems = refs[11 + 2 * nx:]
        i, kk = pl.program_id(0), pl.program_id(1)

        @pl.when((i == 0) & (kk == 0))
        def _():
            dw_ref[...] = jnp.zeros_like(dw_ref)
            if nx:
                _exchange_start(xch_src, xch_dst, *xch_sems)

        @pl.when(kk == 0)
        def _():
            acc[...] = lax.dot_general(a2_ref[...], b2_ref[...], NT, preferred_element_type=F32)

        acc[...] += lax.dot_general(a_ref[...], b_ref[...], NT, preferred_element_type=F32)

        @pl.when(kk == nk - 1)
        def _():
            dh = acc[...]
            xf = x_ref[...]
            r = lax.rsqrt(jnp.mean(xf * xf, axis=-1, keepdims=True) + EPS)
            xhat = xf * r
            g = dh * w_ref[...]
            dx = dres_ref[...] + r * (g - xhat * jnp.mean(g * xhat, axis=-1, keepdims=True))
            dx_ref[...] = dx
            dxb_ref[...] = dx.astype(dxb_ref.dtype)
            dw_ref[...] += jnp.sum(dh * xhat, axis=0, keepdims=True)

        if nx:
            @pl.when((i == grid[0] - 1) & (kk == nk - 1))
            def _():
                _exchange_wait(xch_src, xch_dst, *xch_sems)

    row = pl.BlockSpec((tm, d), lambda i, kk: (i, 0))
    vec = pl.BlockSpec((1, d), lambda i, kk: (0, 0))
    any_spec = pl.BlockSpec(memory_space=pl.ANY)
    in_specs = [pl.BlockSpec((tm, tk), lambda i, kk: (i, kk)), pl.BlockSpec((d, tk), lambda i, kk: (0, kk)),
                pl.BlockSpec((tm, da2.shape[1]), lambda i, kk: (i, 0)), pl.BlockSpec((d, wt2.shape[1]), lambda i, kk: (0, 0)),
                row, vec, row]
    return pl.pallas_call(
        body, grid=grid, in_specs=in_specs + [any_spec] * nx, out_specs=[row, row, vec] + [any_spec] * nx,
        out_shape=[SDS((s, d), F32), SDS((s, d), BF16), SDS((1, d), F32)] + [SDS(e.shape, e.dtype) for e in exchange],
        scratch_shapes=[pltpu.VMEM((tm, d), F32)] + (_exchange_sems(nx) if nx else []),
        name=name, compiler_params=_cp("arbitrary", "arbitrary"))(da, wt, da2, wt2, x, w, dres, *exchange)


def _final_norm_loss(x, w, target, name):
    s, d = x.shape
    tm = _pick(s, (512, 128))

    def body(x_ref, w_ref, t_ref, dx_ref, dxb_ref, dw_ref, loss_ref):
        xf = x_ref[...]
        r = lax.rsqrt(jnp.mean(xf * xf, axis=-1, keepdims=True) + EPS)
        xhat = xf * r
        wv = w_ref[...]
        e = xhat * wv - t_ref[...]
        dy = e * (1.0 / d)
        g = dy * wv
        dx = r * (g - xhat * jnp.mean(g * xhat, axis=-1, keepdims=True))
        dx_ref[...] = dx
        dxb_ref[...] = dx.astype(dxb_ref.dtype)

        @pl.when(pl.program_id(0) == 0)
        def _():
            dw_ref[...] = jnp.zeros_like(dw_ref)
            loss_ref[...] = jnp.zeros_like(loss_ref)

        dw_ref[...] += jnp.sum(dy * xhat, axis=0, keepdims=True)
        loss_ref[...] += jnp.sum(jnp.sum(e * e, axis=1, keepdims=True), axis=0, keepdims=True) * (0.5 / d)

    row = pl.BlockSpec((tm, d), lambda i: (i, 0))
    vec = pl.BlockSpec((1, d), lambda i: (0, 0))
    return pl.pallas_call(
        body, grid=(s // tm,), in_specs=[row, vec, row], out_specs=[row, row, vec, pl.BlockSpec((1, 128), lambda i: (0, 0))],
        out_shape=[SDS((s, d), F32), SDS((s, d), BF16), SDS((1, d), F32), SDS((1, 128), F32)],
        name=name, compiler_params=_cp("arbitrary"))(x, w, target)


def _rope_tables(s):
    half = ROPE_DIM // 2
    f32 = np.float32
    inv_freq = np.power(f32(ROPE_THETA), -np.arange(half, dtype=f32) * f32(2.0) / f32(ROPE_DIM)).astype(f32)
    ang = (np.arange(s, dtype=f32)[:, None] * inv_freq[None, :]).astype(f32)
    cos, sin = np.cos(ang).astype(f32), np.sin(ang).astype(f32)
    z = np.zeros((s, HEAD_DIM - ROPE_DIM), f32)
    zh = np.zeros((s, half), f32)
    c = np.concatenate([cos, cos, z + 1.0], axis=1)
    sa = np.concatenate([zh, sin, z], axis=1)
    sb = np.concatenate([-sin, zh, z], axis=1)
    two = lambda t: np.concatenate([t, t], axis=1)
    c, sa, sb = two(c), two(sa), two(sb)
    fwd, bwd = (c, sa, sb), (c, np.roll(sb, half, axis=1), np.roll(sa, -half, axis=1))
    return tuple(jnp.asarray(t) for t in fwd), tuple(jnp.asarray(t) for t in bwd)


PERM_TILE = 256


def _perm_matrices(d):
    n = PERM_TILE // d
    o = np.arange(PERM_TILE)
    p = np.zeros((PERM_TILE, PERM_TILE), np.float32)
    p[o, (o % n) * d + o // n] = 1.0
    return jnp.asarray(p, dtype=BF16), jnp.asarray(p.T.copy(), dtype=BF16)


def _store_pattern(o_ref, tile, perm_ref, d, cols=slice(None)):
    n = PERM_TILE // d
    z = jnp.dot(perm_ref[...], tile, preferred_element_type=F32).astype(o_ref.dtype)
    for r in range(d):
        o_ref[r, :, cols] = z[r * n:(r + 1) * n]


def _load_pattern(x_ref, perm_ref, d):
    tile = jnp.concatenate([x_ref[r] for r in range(d)], axis=0)
    return jnp.dot(perm_ref[...], tile, preferred_element_type=F32)


def _pattern_spec(d, w, col=0):
    return pl.BlockSpec((d, PERM_TILE // d, w), lambda i, *_: (0, i, col))


def _rope_fwd(proj, tabs, perms, name):
    s = proj.shape[0]
    tm = PERM_TILE
    half = ROPE_DIM // 2
    wb = D_MODEL
    nd = len(DILATIONS) - 1

    def body(x_ref, c_ref, sa_ref, sb_ref, *rest):
        perm_refs, o_ref, op_refs = rest[:nd], rest[nd], rest[nd + 1:]
        is_v = pl.program_id(1) == 2
        c = jnp.where(is_v, 1.0, c_ref[...])
        sa = jnp.where(is_v, 0.0, sa_ref[...])
        sb = jnp.where(is_v, 0.0, sb_ref[...])
        for j in range(wb // 128):
            x = x_ref[:, j * 128:(j + 1) * 128].astype(F32)
            o_ref[:, j * 128:(j + 1) * 128] = (x * c + pltpu.roll(x, half, 1) * sa + pltpu.roll(x, 128 - half, 1) * sb).astype(o_ref.dtype)
        y = o_ref[...]
        for d, perm_ref, op_ref in zip(DILATIONS[1:], perm_refs, op_refs):
            _store_pattern(op_ref, y, perm_ref, d)

    blk = pl.BlockSpec((tm, wb), lambda i, j: (i, j))
    tab = pl.BlockSpec((tm, 128), lambda i, j: (i, 0))
    pm = pl.BlockSpec((tm, tm), lambda i, j: (0, 0))
    outs = pl.pallas_call(
        body, grid=(s // tm, 3), in_specs=[blk, tab, tab, tab] + [pm] * nd,
        out_specs=[blk] + [pl.BlockSpec((d, tm // d, wb), lambda i, j: (0, i, j)) for d in DILATIONS[1:]],
        out_shape=[SDS((s, 3 * wb), BF16)] + [SDS((d, s // d, 3 * wb), BF16) for d in DILATIONS[1:]],
        name=name, compiler_params=_cp("parallel", "parallel"))(proj, *tabs, *[perms[d][0] for d in DILATIONS[1:]])
    return [o.reshape(s, 3 * wb) for o in outs]


def _sum3_rope(ds_, perms, tabs, name, slab=None):
    s, w = ds_[0].shape
    tm = PERM_TILE
    half = ROPE_DIM // 2
    nd = len(DILATIONS) - 1

    def body(*refs):
        x_refs, perm_refs = refs[:nd + 1], refs[nd + 1:2 * nd + 1]
        tab_refs = refs[2 * nd + 1:2 * nd + 4]
        tot = x_refs[0][...].astype(F32)
        for d, x_ref, perm_ref in zip(DILATIONS[1:], x_refs[1:], perm_refs):
            tot = tot + _load_pattern(x_ref, perm_ref, d)
        for j in range(w // 128):
            x = tot[:, j * 128:(j + 1) * 128]
            if tabs is not None:
                x = x * tab_refs[0][...] + pltpu.roll(x, half, 1) * tab_refs[1][...] + pltpu.roll(x, 128 - half, 1) * tab_refs[2][...]
            refs[-1][:, j * 128:(j + 1) * 128] = x.astype(refs[-1].dtype)

    blk = pl.BlockSpec((tm, w), lambda i: (i, 0))
    tab = pl.BlockSpec((tm, 128), lambda i: (i, 0))
    pm = pl.BlockSpec((tm, tm), lambda i: (0, 0))
    extra = [] if tabs is None else list(tabs)
    out_shape, col0, more, more_specs, alias = _slab(s, w, BF16, slab, 2 * nd + 1 + len(extra))
    cb = col0 // w
    args = [ds_[0]] + [x.reshape(d, s // d, w) for d, x in zip(DILATIONS[1:], ds_[1:])] + [perms[d][1] for d in DILATIONS[1:]]
    return pl.pallas_call(
        body, grid=(s // tm,),
        in_specs=[blk] + [_pattern_spec(d, w) for d in DILATIONS[1:]] + [pm] * nd + [tab] * len(extra) + more_specs,
        out_specs=pl.BlockSpec((tm, w), lambda i: (i, cb)), out_shape=out_shape, input_output_aliases=alias,
        name=name, compiler_params=_cp("parallel"))(*args, *extra, *more)


def _band_valid(t, nq, nk, qofs, kofs, seq_len):
    qpos = t * 128 + qofs + lax.broadcasted_iota(jnp.int32, (nq, nk), 0)
    kpos = t * 128 + kofs + lax.broadcasted_iota(jnp.int32, (nq, nk), 1)
    sh = int(math.log2(seq_len))
    same = lax.shift_right_arithmetic(qpos, sh) == lax.shift_right_arithmetic(kpos, sh)
    return same & (jnp.abs(kpos - qpos) <= BAND_HALF)


def _window(r0, r1, r2):
    return jnp.concatenate([r0[128 - BAND_HALF:128], r1[...], r2[0:BAND_HALF]], axis=0)


WIN = 128 + 2 * BAND_HALF


def _first_head_lanes():
    return lax.broadcasted_iota(jnp.int32, (1, 2 * HEAD_DIM), 1) < HEAD_DIM


def _only(x, keep):
    return jnp.where(keep, x, jnp.zeros((), x.dtype))


def _win_specs(width, col, nt):
    return [pl.BlockSpec((128, width), lambda t: (jnp.maximum(t - 1, 0), col)),
            pl.BlockSpec((128, width), lambda t: (t, col)),
            pl.BlockSpec((128, width), lambda t: (jnp.minimum(t + 1, nt - 1), col))]


def _attn_fwd(qk, v_src, v_col, seq_len, name):
    s = qk.shape[0]
    nt = s // 128
    dm = D_MODEL

    def body(q_ref, k0, k1, k2, v0, v1, v2, o_ref, lse_ref):
        t = pl.program_id(0)
        valid = _band_valid(t, 128, WIN, 0, -BAND_HALF, seq_len)
        q = q_ref[...]
        kc = _window(k0, k1, k2)
        vc = _window(v0, v1, v2)
        first = _first_head_lanes()
        outs, lses = [], []
        for pr in range(N_HEADS // 2):
            ps = slice(pr * 128, (pr + 1) * 128)
            qp, kp, vp = q[:, ps], kc[:, ps], vc[:, ps]
            halves = []
            for keep in (first, ~first):
                sc = lax.dot_general(_only(qp, keep), kp, NT, preferred_element_type=F32) * (HEAD_DIM ** -0.5)
                sc = jnp.where(valid, sc, NEG)
                m = jnp.max(sc, axis=1, keepdims=True)
                e = jnp.exp(sc - m)
                den = jnp.sum(e, axis=1, keepdims=True)
                halves.append(jnp.dot(e.astype(BF16), vp, preferred_element_type=F32) / den)
                lses.append(m + jnp.log(den))
            outs.append(jnp.where(first, halves[0], halves[1]))
        o_ref[...] = jnp.concatenate(outs, axis=1).astype(o_ref.dtype)
        lse_ref[...] = jnp.concatenate(lses, axis=1)

    in_specs = [pl.BlockSpec((128, dm), lambda t: (t, 0))] + _win_specs(dm, 1, nt) + _win_specs(dm, v_col, nt)
    return pl.pallas_call(
        body, grid=(nt,), in_specs=in_specs,
        out_specs=[pl.BlockSpec((128, dm), lambda t: (t, 0)), pl.BlockSpec((128, N_HEADS), lambda t: (t, 0))],
        out_shape=[SDS((s, dm), BF16), SDS((s, N_HEADS), F32)], name=name, compiler_params=_cp("parallel"))(
            qk, qk, qk, qk, v_src, v_src, v_src)


def _attn_combine(os_, lses, perms, expand, name):
    s, dm = os_[0].shape
    tm = PERM_TILE
    nd = len(DILATIONS) - 1

    def body(o1, o2, o3, l1, l2, l3, p2, p3, e_ref, out_ref, out_t_ref, lt_ref):
        ls = [l1[...], l2[...], l3[...]]
        m = jnp.maximum(jnp.maximum(ls[0], ls[1]), ls[2])
        es = [jnp.exp(l - m) for l in ls]
        tot = es[0] + es[1] + es[2]
        lt_ref[...] = m + jnp.log(tot)
        ovs = [o1[...].astype(F32), _load_pattern(o2, p2, DILATIONS[1]), _load_pattern(o3, p3, DILATIONS[2])]
        acc = jnp.zeros((tm, dm), F32)
        for e, o in zip(es, ovs):
            acc = acc + jnp.dot(e / tot, e_ref[...], precision=HIGH, preferred_element_type=F32) * o
        out = acc.astype(out_ref.dtype)
        out_ref[...] = out
        out_t_ref[...] = out.T

    row = pl.BlockSpec((tm, dm), lambda i: (i, 0))
    st = pl.BlockSpec((tm, N_HEADS), lambda i: (i, 0))
    pm = pl.BlockSpec((tm, tm), lambda i: (0, 0))
    args = [os_[0]] + [o.reshape(d, s // d, dm) for d, o in zip(DILATIONS[1:], os_[1:])]
    return pl.pallas_call(
        body, grid=(s // tm,),
        in_specs=[row] + [_pattern_spec(d, dm) for d in DILATIONS[1:]] + [st, st, st, pm, pm, pl.BlockSpec((N_HEADS, dm), lambda i: (0, 0))],
        out_specs=[row, pl.BlockSpec((dm, tm), lambda i: (0, i)), st],
        out_shape=[SDS((s, dm + D_INNER), BF16), SDS((dm + D_INNER, s), BF16), SDS((s, N_HEADS), F32)],
        name=name, compiler_params=_cp("parallel"))(*args, *lses, *[perms[d][1] for d in DILATIONS[1:]], expand)


def _attn_delta(dmix, attn, expand_t, perms, name):
    s, dm = attn.shape[0], D_MODEL
    tm = PERM_TILE
    nd = len(DILATIONS) - 1

    def body(d_ref, a_ref, e_ref, *rest):
        perm_refs, o_ref, op_refs = rest[:nd], rest[nd], rest[nd + 1:]
        dv = d_ref[...]
        prod = dv.astype(F32) * a_ref[...].astype(F32)
        o_ref[...] = jnp.dot(prod, e_ref[...], precision=HIGH, preferred_element_type=F32)
        for d, perm_ref, op_ref in zip(DILATIONS[1:], perm_refs, op_refs):
            _store_pattern(op_ref, dv, perm_ref, d)

    row = pl.BlockSpec((tm, dm), lambda i: (i, 0))
    pm = pl.BlockSpec((tm, tm), lambda i: (0, 0))
    outs = pl.pallas_call(
        body, grid=(s // tm,), in_specs=[row, row, pl.BlockSpec((dm, N_HEADS), lambda i: (0, 0))] + [pm] * nd,
        out_specs=[pl.BlockSpec((tm, N_HEADS), lambda i: (i, 0))] + [_pattern_spec(d, dm) for d in DILATIONS[1:]],
        out_shape=[SDS((s, N_HEADS), F32)] + [SDS((d, s // d, dm), BF16) for d in DILATIONS[1:]],
        name=name, compiler_params=_cp("parallel"))(dmix, attn, expand_t, *[perms[d][0] for d in DILATIONS[1:]])
    return outs[0], [o.reshape(s, dm) for o in outs[1:]]


def _attn_bwd_dq(qk, v_src, v_col, do_src, lse, delta, seq_len, name):
    s = qk.shape[0]
    nt = s // 128
    dm = D_MODEL

    def body(q_ref, k0, k1, k2, v0, v1, v2, do_ref, lse_ref, dl_ref, dq_ref):
        t = pl.program_id(0)
        valid = _band_valid(t, 128, WIN, 0, -BAND_HALF, seq_len)
        q = q_ref[...]
        do = do_ref[...]
        kc = _window(k0, k1, k2)
        vc = _window(v0, v1, v2)
        lse_v, dl_v = lse_ref[...], dl_ref[...]
        first = _first_head_lanes()
        outs = []
        for pr in range(N_HEADS // 2):
            ps = slice(pr * 128, (pr + 1) * 128)
            qp, dop, kp, vp = q[:, ps], do[:, ps], kc[:, ps], vc[:, ps]
            halves = []
            for i, keep in enumerate((first, ~first)):
                h = 2 * pr + i
                sc = lax.dot_general(_only(qp, keep), kp, NT, preferred_element_type=F32) * (HEAD_DIM ** -0.5)
                p = jnp.exp(jnp.where(valid, sc - lse_v[:, h:h + 1], NEG))
                dp = lax.dot_general(_only(dop, keep), vp, NT, preferred_element_type=F32)
                ds = p * (dp - dl_v[:, h:h + 1])
                halves.append(jnp.dot(ds.astype(BF16), kp, preferred_element_type=F32))
            outs.append(jnp.where(first, halves[0], halves[1]) * (HEAD_DIM ** -0.5))
        dq_ref[...] = jnp.concatenate(outs, axis=1).astype(dq_ref.dtype)

    row = pl.BlockSpec((128, dm), lambda t: (t, 0))
    st = pl.BlockSpec((128, N_HEADS), lambda t: (t, 0))
    in_specs = [row] + _win_specs(dm, 1, nt) + _win_specs(dm, v_col, nt) + [row, st, st]
    return pl.pallas_call(
        body, grid=(nt,), in_specs=in_specs, out_specs=row, out_shape=SDS((s, dm), BF16),
        name=name, compiler_params=_cp("parallel"))(qk, qk, qk, qk, v_src, v_src, v_src, do_src, lse, delta)


def _attn_bwd_dkv(qk, v_src, v_col, do_src, lse_t, delta_t, seq_len, name):
    s = qk.shape[0]
    nt = s // 128
    dm = D_MODEL

    def lane_window(r0, r1, r2):
        return jnp.concatenate([r0[:, 128 - BAND_HALF:128], r1[...], r2[:, 0:BAND_HALF]], axis=1)

    def body(k_ref, v_ref, q0, q1, q2, d0, d1, d2, l0, l1, l2, e0, e1, e2, dk_ref, dv_ref):
        t = pl.program_id(0)
        valid = _band_valid(t, 128, WIN, 0, -BAND_HALF, seq_len)
        k = k_ref[...]
        v = v_ref[...]
        qc = _window(q0, q1, q2)
        dc = _window(d0, d1, d2)
        lse_v = lane_window(l0, l1, l2)
        dl_v = lane_window(e0, e1, e2)
        first = _first_head_lanes()
        dks, dvs = [], []
        for pr in range(N_HEADS // 2):
            ps = slice(pr * 128, (pr + 1) * 128)
            kp, vp, qp, dop = k[:, ps], v[:, ps], qc[:, ps], dc[:, ps]
            dk_h, dv_h = [], []
            for i, keep in enumerate((first, ~first)):
                h = 2 * pr + i
                sc = lax.dot_general(_only(kp, keep), qp, NT, preferred_element_type=F32) * (HEAD_DIM ** -0.5)
                p = jnp.exp(jnp.where(valid, sc - lse_v[h:h + 1, :], NEG))
                dv_h.append(jnp.dot(p.astype(BF16), dop, preferred_element_type=F32))
                dp = lax.dot_general(_only(vp, keep), dop, NT, preferred_element_type=F32)
                ds = p * (dp - dl_v[h:h + 1, :])
                dk_h.append(jnp.dot(ds.astype(BF16), qp, preferred_element_type=F32))
            dks.append(jnp.where(first, dk_h[0], dk_h[1]) * (HEAD_DIM ** -0.5))
            dvs.append(jnp.where(first, dv_h[0], dv_h[1]))
        dk_ref[...] = jnp.concatenate(dks, axis=1).astype(dk_ref.dtype)
        dv_ref[...] = jnp.concatenate(dvs, axis=1).astype(dv_ref.dtype)

    row = pl.BlockSpec((128, dm), lambda t: (t, 0))
    stat = [pl.BlockSpec((N_HEADS, 128), lambda t: (0, jnp.maximum(t - 1, 0))), pl.BlockSpec((N_HEADS, 128), lambda t: (0, t)),
            pl.BlockSpec((N_HEADS, 128), lambda t: (0, jnp.minimum(t + 1, nt - 1)))]
    in_specs = ([pl.BlockSpec((128, dm), lambda t: (t, 1)), pl.BlockSpec((128, dm), lambda t: (t, v_col))]
                + _win_specs(dm, 0, nt) + _win_specs(dm, 0, nt) + stat + stat)
    return pl.pallas_call(
        body, grid=(nt,), in_specs=in_specs, out_specs=[row, row], out_shape=[SDS((s, dm), BF16), SDS((s, dm), BF16)],
        name=name, compiler_params=_cp("parallel"))(qk, v_src, qk, qk, qk, do_src, do_src, do_src, lse_t, lse_t, lse_t, delta_t, delta_t, delta_t)


CONV_COLS = (1024, 1408, 512, 256)


def _halo_specs(tm, tc, col0, nrow_blocks):
    r = tm // 16
    return [pl.BlockSpec((16, tc), lambda i, j: (jnp.maximum(i * r - 1, 0), col0 + j)),
            pl.BlockSpec((16, tc), lambda i, j: (jnp.minimum((i + 1) * r, nrow_blocks * r - 1), col0 + j))]


def _shifted(x_ref, hp_ref, hn_ref, i, last):
    x = x_ref[...].astype(F32)
    tm = x.shape[0]
    rows = lax.broadcasted_iota(jnp.int32, x.shape, 0)
    prev_row = jnp.where(i > 0, hp_ref[15:16, :].astype(F32), 0.0)
    next_row = jnp.where(i < last, hn_ref[0:1, :].astype(F32), 0.0)
    xp = jnp.where(rows == 0, prev_row, pltpu.roll(x, 1, 0))
    xn = jnp.where(rows == tm - 1, next_row, pltpu.roll(x, tm - 1, 0))
    return xp, x, xn


def _conv(x_src, col0, width, w3, bias, act, name, out_dtype=BF16, slab=None, transpose=False, wcol0=0):
    s = x_src.shape[0]
    tm = _pick(s, (256, 128))
    tc = _pick(width, CONV_COLS)
    nb = s // tm
    c0 = col0 // tc
    wc0 = wcol0 // tc

    def body(*refs):
        x_ref, hp_ref, hn_ref, w_ref, b_ref = refs[:5]
        o_ref = refs[-1]
        i = pl.program_id(0)
        xp, x, xn = _shifted(x_ref, hp_ref, hn_ref, i, nb - 1)
        w = w_ref[...]
        if transpose:
            y = w[2:3, :] * xp + w[1:2, :] * x + w[0:1, :] * xn
        else:
            y = w[0:1, :] * xp + w[1:2, :] * x + w[2:3, :] * xn + b_ref[...]
        if act:
            y = y * _sigmoid(y)
        o_ref[...] = y.astype(o_ref.dtype)

    in_specs = ([pl.BlockSpec((tm, tc), lambda i, j: (i, c0 + j))] + _halo_specs(tm, tc, c0, nb)
                + [pl.BlockSpec((3, tc), lambda i, j: (0, wc0 + j)), pl.BlockSpec((1, tc), lambda i, j: (0, wc0 + j))])
    out_shape, ocol, more, more_specs, alias = _slab(s, width, out_dtype, slab, 5)
    ob = ocol // tc
    return pl.pallas_call(
        body, grid=(nb, width // tc), in_specs=in_specs + more_specs, out_specs=pl.BlockSpec((tm, tc), lambda i, j: (i, ob + j)),
        out_shape=out_shape, input_output_aliases=alias, name=name, compiler_params=_cp("parallel", "parallel"))(
            x_src, x_src, x_src, w3, bias, *more)


def _conv_silu_bwd(x_src, col0, width, w3, bias, addends, add_widths, name):
    s = x_src.shape[0]
    tm = _pick(s, (256, 128))
    tc = _pick(width, CONV_COLS)
    nb = s // tm
    c0 = col0 // tc
    na = len(addends)

    def body(*refs):
        x_ref, hp_ref, hn_ref, w_ref, b_ref = refs[:5]
        a_refs = refs[5:5 + na]
        dp_ref, dw_ref, db_ref = refs[5 + na:]
        i, j = pl.program_id(1), pl.program_id(0)
        xp, x, xn = _shifted(x_ref, hp_ref, hn_ref, i, nb - 1)
        w = w_ref[...]
        pre = w[0:1, :] * xp + w[1:2, :] * x + w[2:3, :] * xn + b_ref[...]
        g = jnp.zeros_like(pre)
        for a_ref, aw in zip(a_refs, add_widths):
            av = a_ref[...].astype(F32)
            g = g + (av if aw == width else jnp.where(j < aw // tc, av, 0.0))
        sg = _sigmoid(pre)
        dpre = g * (sg * (1.0 + pre * (1.0 - sg)))
        dp_ref[...] = dpre.astype(dp_ref.dtype)

        @pl.when(i == 0)
        def _():
            dw_ref[...] = jnp.zeros_like(dw_ref)
            db_ref[...] = jnp.zeros_like(db_ref)

        dw_ref[...] += jnp.concatenate([jnp.sum(dpre * xp, axis=0, keepdims=True), jnp.sum(dpre * x, axis=0, keepdims=True),
                                        jnp.sum(dpre * xn, axis=0, keepdims=True)], axis=0)
        db_ref[...] += jnp.sum(dpre, axis=0, keepdims=True)

    r = tm // 16
    in_specs = [pl.BlockSpec((tm, tc), lambda j, i: (i, c0 + j)),
                pl.BlockSpec((16, tc), lambda j, i: (jnp.maximum(i * r - 1, 0), c0 + j)),
                pl.BlockSpec((16, tc), lambda j, i: (jnp.minimum((i + 1) * r, nb * r - 1), c0 + j)),
                pl.BlockSpec((3, tc), lambda j, i: (0, j)), pl.BlockSpec((1, tc), lambda j, i: (0, j))]
    for aw in add_widths:
        nblk = aw // tc
        in_specs.append(pl.BlockSpec((tm, tc), lambda j, i, nblk=nblk: (i, jnp.minimum(j, nblk - 1))))
    return pl.pallas_call(
        body, grid=(width // tc, nb), in_specs=in_specs,
        out_specs=[pl.BlockSpec((tm, tc), lambda j, i: (i, j)), pl.BlockSpec((3, tc), lambda j, i: (0, j)), pl.BlockSpec((1, tc), lambda j, i: (0, j))],
        out_shape=[SDS((s, width), BF16), SDS((3, width), F32), SDS((1, width), F32)],
        name=name, compiler_params=_cp("parallel", "arbitrary"))(x_src, x_src, x_src, w3, bias, *addends)


def _ffn_gate_fwd(u, w3, bias, name):
    s = u.shape[0]
    tm = _pick(s, (256, 128))
    tc = _pick(D_FF, CONV_COLS)
    nb = s // tm
    nj = D_FF // tc

    def body(g_ref, gp, gn, u_ref, up, un, wg_ref, wu_ref, bg_ref, bu_ref, o_ref, ot_ref):
        i = pl.program_id(0)
        outs = []
        for (x_ref, hp, hn, w_ref, b_ref) in ((g_ref, gp, gn, wg_ref, bg_ref), (u_ref, up, un, wu_ref, bu_ref)):
            xp, x, xn = _shifted(x_ref, hp, hn, i, nb - 1)
            w = w_ref[...]
            outs.append(w[0:1, :] * xp + w[1:2, :] * x + w[2:3, :] * xn + b_ref[...])
        gate, upv = outs
        out = (gate * _sigmoid(gate) * upv).astype(o_ref.dtype)
        o_ref[...] = out
        ot_ref[...] = out.T

    def xspecs(c0):
        return [pl.BlockSpec((tm, tc), lambda i, j: (i, c0 + j))] + _halo_specs(tm, tc, c0, nb)

    in_specs = (xspecs(0) + xspecs(nj)
                + [pl.BlockSpec((3, tc), lambda i, j: (0, j)), pl.BlockSpec((3, tc), lambda i, j: (0, nj + j)),
                   pl.BlockSpec((1, tc), lambda i, j: (0, j)), pl.BlockSpec((1, tc), lambda i, j: (0, nj + j))])
    return pl.pallas_call(
        body, grid=(nb, nj), in_specs=in_specs,
        out_specs=[pl.BlockSpec((tm, tc), lambda i, j: (i, j)), pl.BlockSpec((tc, tm), lambda i, j: (j, i))],
        out_shape=[SDS((s, D_FF), BF16), SDS((D_FF, s), BF16)], name=name, compiler_params=_cp("parallel", "parallel"))(
            u, u, u, u, u, u, w3, w3, bias, bias)


def _ffn_gate_bwd(u, w3, bias, dact, name):
    s = u.shape[0]
    tm = _pick(s, (256, 128))
    tc = _pick(D_FF, CONV_COLS)
    nb = s // tm
    nj = D_FF // tc

    def body(g_ref, gp, gn, u_ref, up, un, wg_ref, wu_ref, bg_ref, bu_ref, da_ref, dg_ref, du_ref, dwg_ref, dwu_ref, dbg_ref, dbu_ref):
        i = pl.program_id(1)
        sh, pre = [], []
        for (x_ref, hp, hn, w_ref, b_ref) in ((g_ref, gp, gn, wg_ref, bg_ref), (u_ref, up, un, wu_ref, bu_ref)):
            xs3 = _shifted(x_ref, hp, hn, i, nb - 1)
            w = w_ref[...]
            sh.append(xs3)
            pre.append(w[0:1, :] * xs3[0] + w[1:2, :] * xs3[1] + w[2:3, :] * xs3[2] + b_ref[...])
        gate, upv = pre
        da = da_ref[...].astype(F32)
        sg = _sigmoid(gate)
        dgate = da * upv * (sg * (1.0 + gate * (1.0 - sg)))
        dup = da * gate * sg
        dg_ref[...] = dgate.astype(dg_ref.dtype)
        du_ref[...] = dup.astype(du_ref.dtype)

        @pl.when(i == 0)
        def _():
            for r in (dwg_ref, dwu_ref, dbg_ref, dbu_ref):
                r[...] = jnp.zeros_like(r)

        for d, xs3, dw_ref, db_ref in ((dgate, sh[0], dwg_ref, dbg_ref), (dup, sh[1], dwu_ref, dbu_ref)):
            dw_ref[...] += jnp.concatenate([jnp.sum(d * xs3[0], axis=0, keepdims=True), jnp.sum(d * xs3[1], axis=0, keepdims=True),
                                            jnp.sum(d * xs3[2], axis=0, keepdims=True)], axis=0)
            db_ref[...] += jnp.sum(d, axis=0, keepdims=True)

    r = tm // 16

    def xspecs(c0):
        return [pl.BlockSpec((tm, tc), lambda j, i: (i, c0 + j)),
                pl.BlockSpec((16, tc), lambda j, i: (jnp.maximum(i * r - 1, 0), c0 + j)),
                pl.BlockSpec((16, tc), lambda j, i: (jnp.minimum((i + 1) * r, nb * r - 1), c0 + j))]

    in_specs = (xspecs(0) + xspecs(nj)
                + [pl.BlockSpec((3, tc), lambda j, i: (0, j)), pl.BlockSpec((3, tc), lambda j, i: (0, nj + j)),
                   pl.BlockSpec((1, tc), lambda j, i: (0, j)), pl.BlockSpec((1, tc), lambda j, i: (0, nj + j)),
                   pl.BlockSpec((tm, tc), lambda j, i: (i, j))])
    blk = pl.BlockSpec((tm, tc), lambda j, i: (i, j))
    w_o = pl.BlockSpec((3, tc), lambda j, i: (0, j))
    b_o = pl.BlockSpec((1, tc), lambda j, i: (0, j))
    return pl.pallas_call(
        body, grid=(nj, nb), in_specs=in_specs, out_specs=[blk, blk, w_o, w_o, b_o, b_o],
        out_shape=[SDS((s, D_FF), BF16), SDS((s, D_FF), BF16), SDS((3, D_FF), F32), SDS((3, D_FF), F32), SDS((1, D_FF), F32), SDS((1, D_FF), F32)],
        name=name, compiler_params=_cp("parallel", "arbitrary"))(u, u, u, u, u, u, w3, w3, bias, bias, dact)


def _exchange_sems(n):
    return [pltpu.SemaphoreType.DMA((7 * n,)), pltpu.SemaphoreType.DMA((7 * n,)), pltpu.SemaphoreType.DMA((n,))]


def _exchange_copies(srcs, outs, send_sems, recv_sems, local_sems):
    x, y, c = lax.axis_index("x"), lax.axis_index("y"), lax.axis_index("c")
    me = 4 * x + 2 * y + c
    locals_ = [pltpu.make_async_copy(srcs[a].at[me], outs[a].at[me], local_sems.at[a]) for a in range(len(srcs))]
    sends, recvs = [], []
    for k in range(1, N_DEV):
        px, py, pc = x ^ ((k >> 2) & 1), y ^ ((k >> 1) & 1), c ^ (k & 1)
        peer = 4 * px + 2 * py + pc
        for a in range(len(srcs)):
            sems = dict(send_sem=send_sems.at[a * 7 + k - 1], recv_sem=recv_sems.at[a * 7 + k - 1], device_id_type=MESH)
            sends.append(pltpu.make_async_remote_copy(src_ref=srcs[a].at[peer], dst_ref=outs[a].at[me], device_id=(px, py, pc), **sems))
            recvs.append(pltpu.make_async_remote_copy(src_ref=srcs[a].at[peer], dst_ref=outs[a].at[peer], device_id=(x, y, c), **sems))
    return locals_, sends, recvs


def _exchange_start(srcs, outs, send_sems, recv_sems, local_sems):
    locals_, sends, _ = _exchange_copies(srcs, outs, send_sems, recv_sems, local_sems)
    for cp in locals_ + sends:
        cp.start()


def _exchange_wait(srcs, outs, send_sems, recv_sems, local_sems):
    locals_, sends, recvs = _exchange_copies(srcs, outs, send_sems, recv_sems, local_sems)
    for cp in recvs:
        cp.wait_recv()
    for cp in sends:
        cp.wait_send()
    for cp in locals_:
        cp.wait()


def _gather_copies(srcs, outs, send_sems, recv_sems, local_sems):
    x, y, c = lax.axis_index("x"), lax.axis_index("y"), lax.axis_index("c")
    me, sibling = (x, y, c), (x, y, 1 - c)
    chips = [(1 - x, y), (x, 1 - y), (1 - x, 1 - y)]

    def copy(a, k, block, to, src=None):
        dst = outs[a].at[4 * block[0] + 2 * block[1] + block[2]]
        return pltpu.make_async_remote_copy(
            src_ref=dst if src is None else src, dst_ref=dst,
            send_sem=send_sems.at[a * 7 + k], recv_sem=recv_sems.at[a * 7 + k], device_id=to, device_id_type=MESH)

    n = len(srcs)
    locals_ = [pltpu.make_async_copy(srcs[a], outs[a].at[4 * x + 2 * y + c], local_sems.at[a]) for a in range(n)]
    own = [copy(a, 0, me, sibling, src=srcs[a]) for a in range(n)]
    own += [copy(a, 1 + j, me, (*chip, c), src=srcs[a]) for a in range(n) for j, chip in enumerate(chips)]
    landed_ici = [copy(a, 1 + j, (*chip, c), me) for j, chip in enumerate(chips) for a in range(n)]
    passed = [copy(a, 4 + j, (*chip, c), sibling) for j, chip in enumerate(chips) for a in range(n)]
    landed_d2d = [copy(a, 0, sibling, me) for a in range(n)]
    landed_d2d += [copy(a, 4 + j, (*chip, 1 - c), me) for a in range(n) for j, chip in enumerate(chips)]
    return locals_, own, landed_ici, passed, landed_d2d


def _gather_start(*refs):
    locals_, own, _, _, _ = _gather_copies(*refs)
    for cp in locals_ + own:
        cp.start()


def _gather_forward(*refs):
    _, _, landed_ici, passed, _ = _gather_copies(*refs)
    for arrived, onward in zip(landed_ici, passed):
        arrived.wait_recv()
        onward.start()


def _gather_finish(*refs):
    locals_, own, _, passed, landed_d2d = _gather_copies(*refs)
    for cp in landed_d2d:
        cp.wait_recv()
    for cp in own + passed:
        cp.wait_send()
    for cp in locals_:
        cp.wait()


def _ssd_common(dt_ref, dtt_ref, al_r, al_c, bi_r, bi_c, off, rev):
    li = lax.broadcasted_iota(jnp.int32, (CHUNK, CHUNK), 0)
    si = lax.broadcasted_iota(jnp.int32, (CHUNK, CHUNK), 1)
    mask = (li <= si) if rev else (li >= si)
    mask_t = (li >= si) if rev else (li <= si)
    a_r = -jnp.exp(al_r[...])
    a_c = -jnp.exp(al_c[...])
    pre = dt_ref[:, off:off + N_HEADS] + bi_r[...]
    dt = _softplus(pre)
    cs = jnp.dot(mask.astype(F32), dt * a_r, precision=HIGH, preferred_element_type=F32)
    dt_t = _softplus(dtt_ref[off:off + N_HEADS, :] + bi_c[...])
    cs_t = jnp.dot(dt_t * a_c, mask_t.astype(F32), precision=HIGH, preferred_element_type=F32)
    tot = cs[0:1, :] if rev else cs[CHUNK - 1:CHUNK, :]
    return mask, mask_t, a_r, pre, dt, cs, cs_t, tot


def _ssd_fwd(xbc, dt_raw, dt_t, args_f, args_b, name, gather=()):
    s = xbc.shape[0]
    nc = s // CHUNK
    hp = D_INNER // N_HEADS
    hpg = N_HEADS // N_GROUPS
    gs = N_GROUPS * D_STATE
    ng = len(gather)
    fwd_step = (nc * 27) // 32

    def chunk(x_ref, b_ref, c_ref, dt_ref, dtt_ref, alr, alc, bir, bic, y_ref, st_ref, h_scr, off, rev):
        mask, _, _, _, dt, cs, cs_t, tot = _ssd_common(dt_ref, dtt_ref, alr, alc, bir, bic, off, rev)
        first = _first_head_lanes()
        xs = x_ref[...]
        ys = []
        for g in range(N_GROUPS):
            bg = b_ref[:, g * D_STATE:(g + 1) * D_STATE]
            cg = c_ref[:, g * D_STATE:(g + 1) * D_STATE]
            gm = lax.dot_general(cg, bg, NT, preferred_element_type=F32)
            hcat = h_scr[g]
            st_ref[0, g] = hcat
            ch = lax.dot_general(cg, hcat.astype(BF16), NT, preferred_element_type=F32)
            xdd = []
            for pr in range(hpg // 2):
                h0 = g * hpg + 2 * pr
                lanes = slice(h0 * hp, (h0 + 2) * hp)
                cols = [cs[:, h:h + 1] for h in (h0, h0 + 1)]
                pair = lambda a, b: jnp.where(first, a, b)
                xdf = xs[:, lanes].astype(F32) * pair(dt[:, h0:h0 + 1], dt[:, h0 + 1:h0 + 2])
                xdb = xdf.astype(BF16)
                yh = []
                for i, h in enumerate((h0, h0 + 1)):
                    lm = jnp.exp(jnp.where(mask, cols[i] - cs_t[h:h + 1, :], NEG))
                    yh.append(jnp.dot((gm * lm).astype(BF16), xdb, preferred_element_type=F32))
                ecs = pair(jnp.exp(cols[0]), jnp.exp(cols[1]))
                ys.append(pair(yh[0], yh[1]) + ecs * ch[:, 2 * pr * hp:(2 * pr + 2) * hp])
                dec = pair(jnp.exp(tot[:, h0:h0 + 1] - cols[0]), jnp.exp(tot[:, h0 + 1:h0 + 2] - cols[1]))
                xdd.append((xdf * dec).astype(BF16))
            snew = lax.dot_general(jnp.concatenate(xdd, axis=1), bg, TN, preferred_element_type=F32)
            for r in range(hpg):
                rs = slice(r * hp, (r + 1) * hp)
                h_scr[g, rs, :] = jnp.exp(tot[:, g * hpg + r:g * hpg + r + 1]) * hcat[rs] + snew[rs]
        y_ref[...] = jnp.concatenate(ys, axis=1)

    def body(*refs):
        in_f, in_b = refs[0:9], refs[9:18]
        g_src = refs[18:18 + ng]
        out_f, out_b = refs[18 + ng:20 + ng], refs[20 + ng:22 + ng]
        g_dst = refs[22 + ng:22 + 2 * ng]
        hs_f, hs_b = refs[22 + 2 * ng], refs[23 + 2 * ng]
        g_sems = refs[24 + 2 * ng:]
        step = pl.program_id(0)

        @pl.when(step == 0)
        def _():
            hs_f[...] = jnp.zeros_like(hs_f)
            hs_b[...] = jnp.zeros_like(hs_b)
            if ng:
                _gather_start(g_src, g_dst, *g_sems)

        chunk(*in_f, *out_f, hs_f, 0, False)
        chunk(*in_b, *out_b, hs_b, N_HEADS, True)

        if ng:
            @pl.when(step == fwd_step)
            def _():
                _gather_forward(g_src, g_dst, *g_sems)

            @pl.when(step == nc - 1)
            def _():
                _gather_finish(g_src, g_dst, *g_sems)

    small = lambda shape: pl.BlockSpec(shape, lambda c: (0, 0))

    def specs(cm):
        ins = [pl.BlockSpec((CHUNK, D_INNER), lambda c: (cm(c), 0)),
               pl.BlockSpec((CHUNK, gs), lambda c: (cm(c), D_INNER // gs)),
               pl.BlockSpec((CHUNK, gs), lambda c: (cm(c), D_INNER // gs + 1)),
               pl.BlockSpec((CHUNK, 128), lambda c: (cm(c), 0)),
               pl.BlockSpec((2 * N_HEADS, CHUNK), lambda c: (0, cm(c))),
               small((1, N_HEADS)), small((N_HEADS, 1)), small((1, N_HEADS)), small((N_HEADS, 1))]
        outs = [pl.BlockSpec((CHUNK, D_INNER), lambda c: (cm(c), 0)),
                pl.BlockSpec((1, N_GROUPS, hpg * hp, D_STATE), lambda c: (cm(c), 0, 0, 0))]
        return ins, outs

    ins_f, outs_f = specs(lambda c: c)
    ins_b, outs_b = specs(lambda c: nc - 1 - c)
    any_spec = pl.BlockSpec(memory_space=pl.ANY)
    one_dir = [SDS((s, D_INNER), F32), SDS((nc, N_GROUPS, hpg * hp, D_STATE), F32)]
    state = pltpu.VMEM((N_GROUPS, hpg * hp, D_STATE), F32)
    return pl.pallas_call(
        body, grid=(nc,), in_specs=ins_f + ins_b + [any_spec] * ng, out_specs=outs_f + outs_b + [any_spec] * ng,
        out_shape=one_dir + one_dir + [SDS((N_DEV,) + g.shape, g.dtype) for g in gather],
        scratch_shapes=[state, state] + (_exchange_sems(ng) if ng else []), name=name, compiler_params=_cp("arbitrary"))(
            xbc, xbc, xbc, dt_raw, dt_t, *args_f, xbc, xbc, xbc, dt_raw, dt_t, *args_b, *gather)


def _ssd_bwd(xbc, dt_raw, dt_t, al_r, al_c, bi_r, bi_c, states, dy, off, rev, name, exchange=()):
    s = xbc.shape[0]
    nc = s // CHUNK
    hp = D_INNER // N_HEADS
    gs = N_GROUPS * D_STATE
    hpg = N_HEADS // N_GROUPS
    cm = (lambda c: c) if rev else (lambda c: nc - 1 - c)
    nx = len(exchange)

    def body(*refs):
        x_ref, b_ref, c_ref, dt_ref, dtt_ref, alr, alc, bir, bic, st_ref, dy_ref = refs[:11]
        xch_src = refs[11:11 + nx]
        dx_ref, ddt_ref, dal_ref, dbi_ref = refs[11 + nx:15 + nx]
        xch_dst = refs[15 + nx:15 + 2 * nx]
        dh_scr = refs[15 + 2 * nx]
        xch_sems = refs[16 + 2 * nx:]

        @pl.when(pl.program_id(0) == 0)
        def _():
            dh_scr[...] = jnp.zeros_like(dh_scr)
            dal_ref[...] = jnp.zeros_like(dal_ref)
            dbi_ref[...] = jnp.zeros_like(dbi_ref)
            if nx:
                _exchange_start(xch_src, xch_dst, *xch_sems)

        if nx:
            @pl.when(pl.program_id(0) == nc - 1)
            def _():
                _exchange_wait(xch_src, xch_dst, *xch_sems)

        mask, mask_t, a_r, pre, dt, cs, cs_t, tot = _ssd_common(dt_ref, dtt_ref, alr, alc, bir, bic, off, rev)
        xs = x_ref[...]
        dyv = dy_ref[...]
        rows = lax.broadcasted_iota(jnp.int32, (CHUNK, 1), 0)
        end_row = (rows == 0) if rev else (rows == CHUNK - 1)
        lane_h = lax.broadcasted_iota(jnp.int32, (1, N_HEADS), 1)
        sub_h = lax.broadcasted_iota(jnp.int32, (N_HEADS, 1), 0)
        first = _first_head_lanes()
        dcs_all = jnp.zeros((CHUNK, N_HEADS), F32)
        colw_all = jnp.zeros((N_HEADS, CHUNK), F32)
        dxsum_all = jnp.zeros((CHUNK, N_HEADS), F32)
        dxs, dbs, dcs_out = [], [], []
        for g in range(N_GROUPS):
            bg = b_ref[:, g * D_STATE:(g + 1) * D_STATE]
            cg = c_ref[:, g * D_STATE:(g + 1) * D_STATE]
            gm = lax.dot_general(cg, bg, NT, preferred_element_type=F32)
            hcat = st_ref[0, g]
            dhcat = dh_scr[g]
            hb, dhb = hcat.astype(BF16), dhcat.astype(BF16)
            ch = lax.dot_general(cg, hb, NT, preferred_element_type=F32)
            z = lax.dot_general(bg, dhb, NT, preferred_element_type=F32)
            dg_sum = jnp.zeros((CHUNK, CHUNK), F32)
            dchs, xdds, t_hs = [], [], []
            for pr in range(hpg // 2):
                h0 = g * hpg + 2 * pr
                lanes = slice(h0 * hp, (h0 + 2) * hp)
                ps = slice(2 * pr * hp, (2 * pr + 2) * hp)
                pair = lambda a, b: jnp.where(first, a, b)
                cols = [cs[:, h:h + 1] for h in (h0, h0 + 1)]
                tots = [tot[:, h:h + 1] for h in (h0, h0 + 1)]
                xh = xs[:, lanes].astype(F32)
                dtc = pair(dt[:, h0:h0 + 1], dt[:, h0 + 1:h0 + 2])
                xdf = xh * dtc
                xd = xdf.astype(BF16)
                dyh = dyv[:, lanes]
                dyb = dyh.astype(BF16)
                ecs = pair(jnp.exp(cols[0]), jnp.exp(cols[1]))
                dec = pair(jnp.exp(tots[0] - cols[0]), jnp.exp(tots[1] - cols[1]))
                yoff_t = dyh * (ecs * ch[:, ps])
                dchs.append((ecs * dyh).astype(BF16))
                xdd = xdf * dec
                ddec_t = xdd * z[:, ps]
                xdds.append(xdd.astype(BF16))
                dxd_h = []
                for i, (h, keep) in enumerate(((h0, first), (h0 + 1, ~first))):
                    rs = slice((2 * pr + i) * hp, (2 * pr + i + 1) * hp)
                    lm = jnp.exp(jnp.where(mask, cols[i] - cs_t[h:h + 1, :], NEG))
                    m = gm * lm
                    t_h = jnp.exp(tots[i])
                    dxd_h.append(lax.dot_general(m.astype(BF16), dyb, TN, preferred_element_type=F32))
                    dm = lax.dot_general(_only(dyb, keep), xd, NT, preferred_element_type=F32)
                    dg_sum = dg_sum + dm * lm
                    w = dm * m
                    ddec = jnp.sum(_only(ddec_t, keep), axis=1, keepdims=True)
                    dtot = (jnp.sum(jnp.sum(dhcat[rs] * hcat[rs], axis=1, keepdims=True), axis=0, keepdims=True) * t_h
                            + jnp.sum(ddec, axis=0, keepdims=True))
                    t_hs.append(t_h)
                    dcs = jnp.sum(_only(yoff_t, keep), axis=1, keepdims=True) + jnp.sum(w, axis=1, keepdims=True) - ddec
                    dcs = dcs + jnp.where(end_row, dtot, 0.0)
                    colw_all = colw_all + (sub_h == h).astype(F32) * jnp.sum(w, axis=0, keepdims=True)
                    dcs_all = dcs_all + dcs * (lane_h == h).astype(F32)
                dxd = pair(dxd_h[0], dxd_h[1]) + dec * z[:, ps]
                dxx = dxd * xh
                for i, (h, keep) in enumerate(((h0, first), (h0 + 1, ~first))):
                    dxsum_all = dxsum_all + jnp.sum(_only(dxx, keep), axis=1, keepdims=True) * (lane_h == h).astype(F32)
                dxs.append(dxd * dtc)
            dgs = dg_sum.astype(BF16)
            dchc = jnp.concatenate(dchs, axis=1)
            dcs_out.append(jnp.dot(dgs, bg, preferred_element_type=F32) + jnp.dot(dchc, hb, preferred_element_type=F32))
            dbs.append(lax.dot_general(dgs, cg, TN, preferred_element_type=F32)
                       + jnp.dot(jnp.concatenate(xdds, axis=1), dhb, preferred_element_type=F32))
            dh_in = lax.dot_general(dchc, cg, TN, preferred_element_type=F32)
            for r in range(hpg):
                rs = slice(r * hp, (r + 1) * hp)
                dh_scr[g, rs, :] = dh_in[rs] + t_hs[r] * dhcat[rs]
        dx_ref[...] = jnp.concatenate(dxs + dbs + dcs_out, axis=1)
        mt = mask_t.astype(F32)
        da = (jnp.dot(mt, dcs_all, precision=HIGH, preferred_element_type=F32)
              - lax.dot_general(mt, colw_all, NT, precision=HIGH, preferred_element_type=F32))
        dal_ref[...] += jnp.sum(da * dt, axis=0, keepdims=True) * a_r
        ddt_raw = (da * a_r + dxsum_all) * _sigmoid(pre)
        ddt_ref[...] = ddt_raw
        dbi_ref[...] += jnp.sum(ddt_raw, axis=0, keepdims=True)

    small = lambda shape: pl.BlockSpec(shape, lambda c: (0, 0))
    in_specs = [pl.BlockSpec((CHUNK, D_INNER), lambda c: (cm(c), 0)),
                pl.BlockSpec((CHUNK, gs), lambda c: (cm(c), D_INNER // gs)),
                pl.BlockSpec((CHUNK, gs), lambda c: (cm(c), D_INNER // gs + 1)),
                pl.BlockSpec((CHUNK, 128), lambda c: (cm(c), 0)),
                pl.BlockSpec((2 * N_HEADS, CHUNK), lambda c: (0, cm(c))),
                small((1, N_HEADS)), small((N_HEADS, 1)), small((1, N_HEADS)), small((N_HEADS, 1)),
                pl.BlockSpec((1, N_GROUPS, hpg * hp, D_STATE), lambda c: (cm(c), 0, 0, 0)),
                pl.BlockSpec((CHUNK, D_INNER), lambda c: (cm(c), 0))]
    any_spec = pl.BlockSpec(memory_space=pl.ANY)
    return pl.pallas_call(
        body, grid=(nc,), in_specs=in_specs + [any_spec] * nx,
        out_specs=[pl.BlockSpec((CHUNK, XBC), lambda c: (cm(c), 0)), pl.BlockSpec((CHUNK, N_HEADS), lambda c: (cm(c), 0)),
                   small((1, N_HEADS)), small((1, N_HEADS))] + [any_spec] * nx,
        out_shape=[SDS((s, XBC), F32), SDS((s, N_HEADS), F32), SDS((1, N_HEADS), F32), SDS((1, N_HEADS), F32)]
        + [SDS(a.shape, a.dtype) for a in exchange],
        scratch_shapes=[pltpu.VMEM((N_GROUPS, hpg * hp, D_STATE), F32)] + (_exchange_sems(nx) if nx else []),
        name=name, compiler_params=_cp("arbitrary"))(xbc, xbc, xbc, dt_raw, dt_t, al_r, al_c, bi_r, bi_c, states, dy, *exchange)


def _gate_fwd(yf, yb, xbc, proj, dskip_x, norm_w, mix, mix_t, name):
    s = yf.shape[0]
    tm = _pick(s, (256, 128))
    gw = D_INNER // N_GROUPS
    zc = 3 * D_MODEL // D_INNER

    def body(yf_ref, yb_ref, x_ref, z_ref, d_ref, w_ref, _m, _mt, o_ref, ot_ref):
        y = yf_ref[...] + yb_ref[...] + d_ref[...] * x_ref[...].astype(F32)
        z = z_ref[...].astype(F32)
        gt = y * (z * _sigmoid(z))
        outs = []
        for g in range(N_GROUPS):
            gg = gt[:, g * gw:(g + 1) * gw]
            outs.append(gg * lax.rsqrt(jnp.mean(gg * gg, axis=-1, keepdims=True) + EPS))
        out = (jnp.concatenate(outs, axis=1) * w_ref[...]).astype(o_ref.dtype)
        o_ref[...] = out
        ot_ref[...] = out.T

    row = pl.BlockSpec((tm, D_INNER), lambda i: (i, 0))
    vec = pl.BlockSpec((1, D_INNER), lambda i: (0, 0))
    any_spec = pl.BlockSpec(memory_space=pl.ANY)
    cb = (mix.shape[1] - D_INNER) // D_INNER
    return pl.pallas_call(
        body, grid=(s // tm,), in_specs=[row, row, row, pl.BlockSpec((tm, D_INNER), lambda i: (i, zc)), vec, vec, any_spec, any_spec],
        out_specs=[pl.BlockSpec((tm, D_INNER), lambda i: (i, cb)), pl.BlockSpec((D_INNER, tm), lambda i: (cb, i))],
        out_shape=[SDS(mix.shape, BF16), SDS(mix_t.shape, BF16)], input_output_aliases={6: 0, 7: 1},
        name=name, compiler_params=_cp("parallel"))(yf, yb, xbc, proj, dskip_x, norm_w, mix, mix_t)


def _gate_bwd(yf, yb, xbc, proj, dskip_x, norm_w, dmix, dproj, name):
    s = yf.shape[0]
    tm = _pick(s, (256, 128))
    gw = D_INNER // N_GROUPS
    zc = 3 * D_MODEL // D_INNER

    def body(yf_ref, yb_ref, x_ref, z_ref, d_ref, w_ref, do_ref, _, dy_ref, dz_ref, dxs_ref, dw_ref, dd_ref):
        xf = x_ref[...].astype(F32)
        y = yf_ref[...] + yb_ref[...] + d_ref[...] * xf
        z = z_ref[...].astype(F32)
        sg = _sigmoid(z)
        sz = z * sg
        gt = y * sz
        do = do_ref[...].astype(F32)
        dgh = do * w_ref[...]
        ghs, dgts = [], []
        for g in range(N_GROUPS):
            gg = gt[:, g * gw:(g + 1) * gw]
            r = lax.rsqrt(jnp.mean(gg * gg, axis=-1, keepdims=True) + EPS)
            gh = gg * r
            dg = dgh[:, g * gw:(g + 1) * gw]
            ghs.append(gh)
            dgts.append(r * (dg - gh * jnp.mean(dg * gh, axis=-1, keepdims=True)))
        ghat = jnp.concatenate(ghs, axis=1)
        dgt = jnp.concatenate(dgts, axis=1)
        dy = dgt * sz
        dy_ref[...] = dy
        dz_ref[...] = (dgt * y * (sg * (1.0 + z * (1.0 - sg)))).astype(dz_ref.dtype)
        dxs_ref[...] = dy * d_ref[...]

        @pl.when(pl.program_id(0) == 0)
        def _():
            dw_ref[...] = jnp.zeros_like(dw_ref)
            dd_ref[...] = jnp.zeros_like(dd_ref)

        dw_ref[...] += jnp.sum(do * ghat, axis=0, keepdims=True)
        dd_ref[...] += jnp.sum(dy * xf, axis=0, keepdims=True)

    row = pl.BlockSpec((tm, D_INNER), lambda i: (i, 0))
    vec = pl.BlockSpec((1, D_INNER), lambda i: (0, 0))
    dz_shape, _, more, more_specs, alias = _slab(s, D_INNER, BF16, (dproj.shape[1], zc * D_INNER, dproj), 7, out_idx=1)
    return pl.pallas_call(
        body, grid=(s // tm,),
        in_specs=[row, row, row, pl.BlockSpec((tm, D_INNER), lambda i: (i, zc)), vec, vec, pl.BlockSpec((tm, D_INNER), lambda i: (i, 1))] + more_specs,
        out_specs=[row, pl.BlockSpec((tm, D_INNER), lambda i: (i, zc)), row, vec, vec],
        out_shape=[SDS((s, D_INNER), F32), dz_shape, SDS((s, D_INNER), F32), SDS((1, D_INNER), F32), SDS((1, D_INNER), F32)],
        input_output_aliases=alias, name=name, compiler_params=_cp("arbitrary"))(yf, yb, xbc, proj, dskip_x, norm_w, dmix, *more)


def _adamw(parts, w, m, v, name):
    r, c = w.shape
    tr = _pick(r, (256, 352, 128))

    def body(p_ref, w_ref, m_ref, v_ref, g_ref, d_ref, nm_ref, nv_ref):
        g = p_ref[0].astype(F32)
        for i in range(1, N_DEV):
            g = g + p_ref[i].astype(F32)
        mn = B1 * m_ref[...] + (1.0 - B1) * g
        vn = B2 * v_ref[...] + (1.0 - B2) * (g * g)
        m_hat = mn / (1.0 - B1 ** STEP)
        v_hat = vn / (1.0 - B2 ** STEP)
        g_ref[...] = g
        d_ref[...] = -LR * (m_hat / (jnp.sqrt(v_hat) + AEPS) + WD * w_ref[...])
        nm_ref[...] = mn
        nv_ref[...] = vn

    blk = pl.BlockSpec((tr, c), lambda i: (i, 0))
    return pl.pallas_call(
        body, grid=(r // tr,), in_specs=[pl.BlockSpec((N_DEV, tr, c), lambda i: (0, i, 0)), blk, blk, blk],
        out_specs=[blk, blk, blk, blk], out_shape=[SDS((r, c), F32)] * 4, name=name, compiler_params=_cp("parallel"))(parts, w, m, v)


def _sum_parts(parts, name):
    _, r, c = parts.shape

    def body(p_ref, o_ref):
        g = p_ref[0]
        for i in range(1, N_DEV):
            g = g + p_ref[i]
        o_ref[...] = g

    return pl.pallas_call(body, out_shape=SDS((r, c), F32), name=name)(parts)


def _adamw_small(gs, ws, ms, vs, name):
    n = len(gs)

    def body(*refs):
        g_refs, w_refs, m_refs, v_refs = refs[:n], refs[n:2 * n], refs[2 * n:3 * n], refs[3 * n:4 * n]
        d_refs, nm_refs, nv_refs = refs[4 * n:5 * n], refs[5 * n:6 * n], refs[6 * n:7 * n]
        for i in range(n):
            gv = g_refs[i][...]
            mn = B1 * m_refs[i][...] + (1.0 - B1) * gv
            vn = B2 * v_refs[i][...] + (1.0 - B2) * (gv * gv)
            m_hat = mn / (1.0 - B1 ** STEP)
            v_hat = vn / (1.0 - B2 ** STEP)
            d_refs[i][...] = -LR * (m_hat / (jnp.sqrt(v_hat) + AEPS) + WD * w_refs[i][...])
            nm_refs[i][...] = mn
            nv_refs[i][...] = vn

    outs = pl.pallas_call(body, out_shape=[SDS(g.shape, F32) for g in gs] * 3, name=name)(*gs, *ws, *ms, *vs)
    return outs[:n], outs[n:2 * n], outs[2 * n:]


def _my_index():
    return 4 * lax.axis_index("x") + 2 * lax.axis_index("y") + lax.axis_index("c")


def _all_gather(shards, name):
    n = len(shards)

    def body(*refs):
        srcs, outs = refs[:n], refs[n:2 * n]
        _gather_start(srcs, outs, *refs[2 * n:])
        _gather_forward(srcs, outs, *refs[2 * n:])
        _gather_finish(srcs, outs, *refs[2 * n:])

    any_spec = pl.BlockSpec(memory_space=pl.ANY)
    return pl.pallas_call(
        body, in_specs=[any_spec] * n, out_specs=[any_spec] * n,
        out_shape=[SDS((N_DEV,) + s.shape, s.dtype) for s in shards], scratch_shapes=_exchange_sems(n), name=name)(*shards)


def _exchange(arrays, name):
    n = len(arrays)

    def body(*refs):
        srcs, outs = refs[:n], refs[n:2 * n]
        _exchange_start(srcs, outs, *refs[2 * n:])
        _exchange_wait(srcs, outs, *refs[2 * n:])

    any_spec = pl.BlockSpec(memory_space=pl.ANY)
    return pl.pallas_call(
        body, in_specs=[any_spec] * n, out_specs=[any_spec] * n,
        out_shape=[SDS(a.shape, a.dtype) for a in arrays], scratch_shapes=_exchange_sems(n), name=name)(*arrays)


def _to_pattern(t, d):
    if d == 1:
        return t
    s, w = t.shape
    return t.reshape(s // d, d, w).transpose(1, 0, 2).reshape(s, w)


def _from_pattern(t, d):
    if d == 1:
        return t
    s, w = t.shape
    return t.reshape(d, s // d, w).transpose(1, 0, 2).reshape(s, w)


def _pad_lanes(t, n):
    return jnp.pad(t, ((0, 0), (0, n - t.shape[1])))


def _to_shards(g, axis):
    r, c = g.shape
    if axis == 0:
        return g.reshape(N_DEV, r // N_DEV, c)
    return g.reshape(r, N_DEV, c // N_DEV).transpose(1, 0, 2)


def _local_step(x, target, p, late_shards=(), early_exchange=True):
    s = x.shape[0]
    tabs_f, tabs_b = _rope_tables(s)
    expand = jnp.asarray(np.repeat(np.eye(N_HEADS, dtype=np.float32), HEAD_DIM, axis=1))
    w_main, w_dt = p["w_in"][:, :MAIN_W], _pad_lanes(p["w_in"][:, MAIN_W:], 128)
    al_r = {"f": p["a_log_f"], "b": p["a_log_b"]}
    bi_r = {"f": p["dt_bias_f"], "b": p["dt_bias_b"]}
    dskip_x = jnp.repeat(p["d_skip"], D_INNER // N_HEADS, axis=1)
    ssm_w3, ffn_w3 = p["ssm_conv_w"].T, p["ffn_conv_w"].T

    h1, h1t = _rmsnorm_fwd(x, p["norm1_w"], "norm1_fwd")
    proj = _matmul(h1, w_main, name="in_proj")
    dt_raw = _matmul(h1, w_dt, name="in_proj_dt", out_dtype=F32)
    dt_t = dt_raw[:, :2 * N_HEADS].T
    perms = {d: _perm_matrices(d) for d in DILATIONS[1:]}
    qkv = _rope_fwd(proj, tabs_f, perms, "rope_fwd")
    v_col = 2
    os_, lses = [], []
    for d, qkv_p in zip(DILATIONS, qkv):
        o_p, lse_p = _attn_fwd(qkv_p, qkv_p, v_col, s // d, f"attn_fwd_d{d}")
        os_.append(o_p)
        lses.append(_from_pattern(lse_p, d))
    mix, mix_t, lse_tot = _attn_combine(os_, lses, perms, expand, "attn_combine")

    xbc = _conv(proj, 3 * D_MODEL + D_INNER, XBC, ssm_w3, p["ssm_conv_b"], True, "ssm_conv_fwd")
    col = lambda r: r.reshape(N_HEADS, 1)
    ssd_args = {k: (al_r[k], col(al_r[k]), bi_r[k], col(bi_r[k])) for k in ("f", "b")}
    yf, st_f, yb, st_b, *got = _ssd_fwd(xbc, dt_raw, dt_t, ssd_args["f"], ssd_args["b"], "ssd_fwd", gather=late_shards)
    if late_shards:
        p = dict(p, w_out=got[0].reshape(2 * D_MODEL, D_MODEL), w_down=got[2].reshape(D_FF, D_MODEL),
                 w_up=got[1].transpose(1, 0, 2).reshape(D_MODEL, 2 * D_FF))
    mix, mix_t = _gate_fwd(yf, yb, xbc, proj, dskip_x, p["ssm_norm_w"], mix, mix_t, "ssm_gate_fwd")

    x2 =_matmul(mix, p["w_out"], name="out_proj", out_dtype=F32, residual=x)
    h2, h2t = _rmsnorm_fwd(x2, p["norm2_w"], "norm2_fwd")
    u = _matmul(h2, p["w_up"], name="ffn_up")
    act, act_t = _ffn_gate_fwd(u, ffn_w3, p["ffn_conv_b"], "ffn_gate_fwd")
    x3 = _matmul(act, p["w_down"], name="ffn_down", out_dtype=F32, residual=x2)

    dx3, dx3b, g_final, loss = _final_norm_loss(x3, p["final_norm_w"].reshape(1, D_MODEL), target, "final_norm_loss")
    g_w_down = _matmul(act_t, dx3b, name="dw_down")
    dact = _matmul(dx3b, p["w_down"], name="d_act", trans_b=True)
    dug, duu, dwg, dwu, dbg, dbu = _ffn_gate_bwd(u, ffn_w3, p["ffn_conv_b"], dact, "ffn_gate_bwd")
    du = _conv(dug, 0, D_FF, ffn_w3, p["ffn_conv_b"], False, "ffn_conv_bwd_gate", slab=(2 * D_FF, 0, None), transpose=True)
    du = _conv(duu, 0, D_FF, ffn_w3, p["ffn_conv_b"], False, "ffn_conv_bwd_up", slab=(2 * D_FF, D_FF, du), transpose=True, wcol0=D_FF)
    g_w_up = _matmul(h2t, du, name="dw_up")
    none_a, none_w = jnp.zeros((s, 128), BF16), jnp.zeros((D_MODEL, 128), BF16)
    dx2, dx2b, g_norm2 = _proj_norm_bwd(du, p["w_up"], none_a, none_w, x2, p["norm2_w"], dx3, "ffn_up_norm2_bwd")
    g_w_out = _matmul(mix_t, dx2b, name="dw_out")
    dmix = _matmul(dx2b, p["w_out"], name="d_mix", trans_b=True)

    delta, do_pat = _attn_delta(dmix, mix, expand.T, perms, "attn_delta")
    dqs, dks, dvs = [], [], []
    for d, qkv_p, do_p in zip(DILATIONS, qkv, [dmix] + do_pat):
        lse_p, dl_p = _to_pattern(lse_tot, d), _to_pattern(delta, d)
        dqs.append(_attn_bwd_dq(qkv_p, qkv_p, v_col, do_p, lse_p, dl_p, s // d, f"attn_bwd_dq_d{d}"))
        dk, dv = _attn_bwd_dkv(qkv_p, qkv_p, v_col, do_p, lse_p.T, dl_p.T, s // d, f"attn_bwd_dkv_d{d}")
        dks.append(dk)
        dvs.append(dv)
    dproj = _sum3_rope(dqs, perms, tabs_b, "rope_bwd_q", slab=(MAIN_W, 0, None))
    dproj = _sum3_rope(dks, perms, tabs_b, "rope_bwd_k", slab=(MAIN_W, D_MODEL, dproj))
    dproj = _sum3_rope(dvs, perms, None, "sum_dv", slab=(MAIN_W, 2 * D_MODEL, dproj))

    dy, dproj, dxs_skip, g_ssm_norm, g_dskip_lanes = _gate_bwd(yf, yb, xbc, proj, dskip_x, p["ssm_norm_w"], dmix, dproj, "ssm_gate_bwd")
    early_f = [_to_shards(g_w_up, 1), _to_shards(g_w_down, 0)] if early_exchange else []
    early_b = [_to_shards(g_w_out, 0)] if early_exchange else []
    dxbc_f, ddt_f, g_al_f, g_bi_f, *got_f = _ssd_bwd(xbc, dt_raw, dt_t, *ssd_args["f"], st_f, dy, 0, False, "ssd_bwd_f", exchange=early_f)
    dxbc_b, ddt_b, g_al_b, g_bi_b, *got_b = _ssd_bwd(xbc, dt_raw, dt_t, *ssd_args["b"], st_b, dy, N_HEADS, True, "ssd_bwd_b",
                                                     exchange=early_b)
    if early_exchange:
        (g_w_up, g_w_down), (g_w_out,) = got_f, got_b
    dpre, g_ssm_w3, g_ssm_cb = _conv_silu_bwd(proj, 3 * D_MODEL + D_INNER, XBC, ssm_w3, p["ssm_conv_b"],
                                              [dxbc_f, dxbc_b, dxs_skip], [XBC, XBC, D_INNER], "ssm_conv_bwd")
    dproj = _conv(dpre, 0, XBC, ssm_w3, p["ssm_conv_b"], False, "ssm_conv_bwd_x", transpose=True,
                  slab=(MAIN_W, 3 * D_MODEL + D_INNER, dproj))

    ddt =_pad_lanes(jnp.concatenate([ddt_f, ddt_b], axis=1), 128).astype(BF16)
    g_w_main = _matmul(h1t, dproj, name="dw_in")
    g_w_dt = _matmul(h1t, ddt, name="dw_in_dt")
    g_w_in = jnp.concatenate([g_w_main, g_w_dt[:, :2 * N_HEADS]], axis=1)
    late = [_to_shards(g_w_in, 1)] if early_exchange else []
    grad_x, _, g_norm1, *got = _proj_norm_bwd(dproj, w_main, ddt, w_dt, x, p["norm1_w"], dx2, "in_proj_norm1_bwd", exchange=late)
    if early_exchange:
        g_w_in = got[0]

    g_dskip = jnp.sum(g_dskip_lanes.reshape(N_HEADS, D_INNER // N_HEADS), axis=1).reshape(1, N_HEADS)
    small = {
        "norm1_w": g_norm1, "ssm_conv_w": g_ssm_w3.T, "ssm_conv_b": g_ssm_cb, "a_log_f": g_al_f, "a_log_b": g_al_b,
        "dt_bias_f": g_bi_f, "dt_bias_b": g_bi_b, "d_skip": g_dskip, "ssm_norm_w": g_ssm_norm, "norm2_w": g_norm2,
        "ffn_conv_w": jnp.concatenate([dwg, dwu], axis=1).T, "ffn_conv_b": jnp.concatenate([dbg, dbu], axis=1), "final_norm_w": g_final,
    }
    big = {"w_in": g_w_in, "w_out": g_w_out, "w_up": g_w_up, "w_down": g_w_down}
    return loss[0, 0], grad_x, big, small


SMALL_ORDER = ("norm1_w", "ssm_conv_w", "ssm_conv_b", "a_log_f", "a_log_b", "dt_bias_f", "dt_bias_b", "d_skip",
               "ssm_norm_w", "norm2_w", "ffn_conv_w", "ffn_conv_b", "final_norm_w")
SHARDED_SMALL = ("ssm_conv_w", "ffn_conv_w")
BIG_ORDER = ("w_in", "w_out", "w_up", "w_down")


def _pack(vals):
    rows = []
    for v in vals:
        f = v.reshape(-1).astype(F32)
        n = -(-f.shape[0] // 128) * 128
        rows.append(jnp.pad(f, (0, n - f.shape[0])).reshape(-1, 128))
    out = jnp.concatenate(rows, axis=0)
    pad = -out.shape[0] % 8
    return jnp.pad(out, ((0, pad), (0, 0)))


def _unpack(packed, shapes):
    out, r = [], 0
    for shp in shapes:
        n = math.prod(shp)
        nr = -(-n // 128)
        out.append(packed[r:r + nr].reshape(-1)[:n].reshape(shp))
        r += nr
    return out


def kernel(x, norm1_w, w_in, ssm_conv_w, ssm_conv_b, a_log_f, a_log_b, dt_bias_f, dt_bias_b, d_skip, ssm_norm_w, w_out, norm2_w, w_up, ffn_conv_w, ffn_conv_b, w_down, final_norm_w, loss_target, m_norm1_w, m_w_in, m_ssm_conv_w, m_ssm_conv_b, m_a_log_f, m_a_log_b, m_dt_bias_f, m_dt_bias_b, m_d_skip, m_ssm_norm_w, m_w_out, m_norm2_w, m_w_up, m_ffn_conv_w, m_ffn_conv_b, m_w_down, m_final_norm_w, v_norm1_w, v_w_in, v_ssm_conv_w, v_ssm_conv_b, v_a_log_f, v_a_log_b, v_dt_bias_f, v_dt_bias_b, v_d_skip, v_ssm_norm_w, v_w_out, v_norm2_w, v_w_up, v_ffn_conv_w, v_ffn_conv_b, v_w_down, v_final_norm_w):
    w = dict(norm1_w=norm1_w, w_in=w_in, ssm_conv_w=ssm_conv_w, ssm_conv_b=ssm_conv_b, a_log_f=a_log_f, a_log_b=a_log_b,
             dt_bias_f=dt_bias_f, dt_bias_b=dt_bias_b, d_skip=d_skip, ssm_norm_w=ssm_norm_w, w_out=w_out, norm2_w=norm2_w,
             w_up=w_up, ffn_conv_w=ffn_conv_w, ffn_conv_b=ffn_conv_b, w_down=w_down, final_norm_w=final_norm_w)
    mo = dict(norm1_w=m_norm1_w, w_in=m_w_in, ssm_conv_w=m_ssm_conv_w, ssm_conv_b=m_ssm_conv_b, a_log_f=m_a_log_f, a_log_b=m_a_log_b,
              dt_bias_f=m_dt_bias_f, dt_bias_b=m_dt_bias_b, d_skip=m_d_skip, ssm_norm_w=m_ssm_norm_w, w_out=m_w_out, norm2_w=m_norm2_w,
              w_up=m_w_up, ffn_conv_w=m_ffn_conv_w, ffn_conv_b=m_ffn_conv_b, w_down=m_w_down, final_norm_w=m_final_norm_w)
    vo = dict(norm1_w=v_norm1_w, w_in=v_w_in, ssm_conv_w=v_ssm_conv_w, ssm_conv_b=v_ssm_conv_b, a_log_f=v_a_log_f, a_log_b=v_a_log_b,
              dt_bias_f=v_dt_bias_f, dt_bias_b=v_dt_bias_b, d_skip=v_d_skip, ssm_norm_w=v_ssm_norm_w, w_out=v_w_out, norm2_w=v_norm2_w,
              w_up=v_w_up, ffn_conv_w=v_ffn_conv_w, ffn_conv_b=v_ffn_conv_b, w_down=v_w_down, final_norm_w=v_final_norm_w)
    me = _my_index()

    g_in, g_conv = _all_gather([w["w_in"][0].astype(BF16), _pack([w["ssm_conv_w"][0], w["ffn_conv_w"][0]])], "w_in_all_gather")
    conv_rows = [_unpack(g_conv[i], [ssm_conv_w.shape[1:], ffn_conv_w.shape[1:]]) for i in range(N_DEV)]
    full = {
        "w_in": g_in.transpose(1, 0, 2).reshape(D_MODEL, N_DEV * w_in.shape[2]),
        "ssm_conv_w": jnp.concatenate([c[0] for c in conv_rows], axis=0),
        "ffn_conv_w": jnp.concatenate([c[1] for c in conv_rows], axis=0),
    }
    for k in ("norm1_w", "ssm_conv_b", "a_log_f", "a_log_b", "dt_bias_f", "dt_bias_b", "d_skip", "ssm_norm_w", "norm2_w", "ffn_conv_b",
              "final_norm_w"):
        full[k] = w[k]
    late = [w["w_out"][0].astype(BF16), w["w_up"][0].astype(BF16), w["w_down"][0].astype(BF16)]

    loss_part, grad_x, big, small = _local_step(x[0], loss_target[0], full, late)

    small_shapes = [(1,)] + [small[k].shape for k in SMALL_ORDER]
    packed = _pack([loss_part] + [small[k] for k in SMALL_ORDER])
    out_small = jnp.broadcast_to(packed[None], (N_DEV,) + packed.shape)
    (r_small,) = _exchange([out_small], "small_grads_exchange")
    r_in, r_out, r_up, r_down = big["w_in"], big["w_out"], big["w_up"], big["w_down"]

    outs_g, outs_d, outs_m, outs_v = {}, {}, {}, {}
    for k, parts in zip(BIG_ORDER, (r_in, r_out, r_up, r_down)):
        g, dlt, nm, nv = _adamw(parts, w[k][0], mo[k][0], vo[k][0], f"adamw_{k}")
        outs_g[k], outs_d[k], outs_m[k], outs_v[k] = g[None], dlt[None], nm[None], nv[None]
    tot = _unpack(_sum_parts(r_small, "small_grads_sum"), small_shapes)
    loss = tot[0][0]
    gs = dict(zip(SMALL_ORDER, tot[1:]))
    g_own = {}
    for k in SMALL_ORDER:
        if k in SHARDED_SMALL:
            rows = w[k].shape[1]
            g_own[k] = lax.dynamic_slice_in_dim(gs[k], me * rows, rows, axis=0)[None]
        else:
            g_own[k] = gs[k].reshape(w[k].shape)
    two_d = lambda a: a.reshape(-1, a.shape[-1])
    d_s, m_s, v_s = _adamw_small([two_d(g_own[k]) for k in SMALL_ORDER], [two_d(w[k]) for k in SMALL_ORDER],
                                 [two_d(mo[k]) for k in SMALL_ORDER], [two_d(vo[k]) for k in SMALL_ORDER], "adamw_small")
    for k, a, b, c in zip(SMALL_ORDER, d_s, m_s, v_s):
        shp = w[k].shape
        outs_g[k], outs_d[k], outs_m[k], outs_v[k] = g_own[k], a.reshape(shp), b.reshape(shp), c.reshape(shp)

    order = ("norm1_w", "w_in", "ssm_conv_w", "ssm_conv_b", "a_log_f", "a_log_b", "dt_bias_f", "dt_bias_b", "d_skip", "ssm_norm_w",
             "w_out", "norm2_w", "w_up", "ffn_conv_w", "ffn_conv_b", "w_down", "final_norm_w")
    return (loss, grad_x[None], *[outs_g[k] for k in order], *[outs_d[k] for k in order],
            *[outs_m[k] for k in order], *[outs_v[k] for k in order])
```

```python
import math

import numpy as np
import jax
import jax.numpy as jnp
from jax import lax
from jax.experimental import pallas as pl
from jax.experimental.pallas import tpu as pltpu

F32 = jnp.float32
BF16 = jnp.bfloat16
SDS = jax.ShapeDtypeStruct

N_DEV = 8
D_MODEL = 1024
N_HEADS = 16
HEAD_DIM = 64
ROPE_DIM = 16
ROPE_THETA = 500000.0
DILATIONS = (1, 4, 16)
BAND_HALF = 64
D_INNER = 1024
N_GROUPS = 4
D_STATE = 128
CHUNK = 128
XBC = D_INNER + 2 * N_GROUPS * D_STATE
D_FF = 2816
MAIN_W = 3 * D_MODEL + D_INNER + XBC
EPS = 1e-6
LR, B1, B2, AEPS, WD, STEP = 0.001, 0.9, 0.999, 1e-08, 0.01, 10
NEG = -1e30
VMEM_LIMIT = 56 * 1024 * 1024
MESH = pl.DeviceIdType.MESH
HIGH = lax.Precision.HIGHEST
NT = (((1,), (1,)), ((), ()))
TN = (((0,), (0,)), ((), ()))


def _cp(*sem):
    return pltpu.CompilerParams(dimension_semantics=sem, vmem_limit_bytes=VMEM_LIMIT)


def _pick(n, cands):
    for c in cands:
        if n % c == 0:
            return c
    raise ValueError(f"no tile for {n}")


def _sigmoid(x):
    return 1.0 / (1.0 + jnp.exp(-x))


def _softplus(x):
    return jnp.maximum(x, 0.0) + jnp.log1p(jnp.exp(-jnp.abs(x)))


def _slab(s, width, dtype, slab, n_in, out_idx=0):
    if slab is None:
        return SDS((s, width), dtype), 0, [], [], {}
    total, col0, into = slab
    if into is None:
        return SDS((s, total), dtype), col0, [], [], {}
    return SDS((s, total), dtype), col0, [into], [pl.BlockSpec(memory_space=pl.ANY)], {n_in: out_idx}


def _matmul(a, b, *, name, trans_b=False, out_dtype=BF16, residual=None):
    m, k = a.shape
    n = b.shape[0] if trans_b else b.shape[1]
    tk = k if k <= 2048 else _pick(k, (2048, 1408, 1024, 512))
    nk = k // tk
    if nk == 1:
        tm = _pick(m, (2048, 1408, 1024, 512, 256, 128))
        tn = _pick(n, (512, 256, 128))
    else:
        tm = _pick(m, (1024, 1408, 512, 256, 128))
        tn = _pick(n, (1024, 1408, 512, 256, 128))
    dn = NT if trans_b else (((1,), (0,)), ((), ()))

    def body(*refs):
        a_ref, b_ref = refs[0], refs[1]
        o_ref, acc = refs[-2], refs[-1]
        kk = pl.program_id(2)

        @pl.when(kk == 0)
        def _():
            acc[...] = jnp.zeros_like(acc)

        acc[...] += lax.dot_general(a_ref[...], b_ref[...], dn, preferred_element_type=F32)

        @pl.when(kk == nk - 1)
        def _():
            r = acc[...]
            if residual is not None:
                r = r + refs[2][...].astype(F32)
            o_ref[...] = r.astype(o_ref.dtype)

    in_specs = [pl.BlockSpec((tm, tk), lambda i, j, kk: (i, kk)),
                pl.BlockSpec((tn, tk), lambda i, j, kk: (j, kk)) if trans_b else pl.BlockSpec((tk, tn), lambda i, j, kk: (kk, j))]
    args = [a, b]
    if residual is not None:
        in_specs.append(pl.BlockSpec((tm, tn), lambda i, j, kk: (i, j)))
        args.append(residual)
    return pl.pallas_call(
        body, grid=(m // tm, n // tn, nk), in_specs=in_specs,
        out_specs=pl.BlockSpec((tm, tn), lambda i, j, kk: (i, j)),
        out_shape=SDS((m, n), out_dtype), scratch_shapes=[pltpu.VMEM((tm, tn), F32)],
        name=name, compiler_params=_cp("parallel", "parallel", "arbitrary"))(*args)


def _rmsnorm_fwd(x, w, name):
    s, d = x.shape
    tm = _pick(s, (512, 128))

    def body(x_ref, w_ref, o_ref, ot_ref):
        xf = x_ref[...]
        r = lax.rsqrt(jnp.mean(xf * xf, axis=-1, keepdims=True) + EPS)
        out = (xf * r * w_ref[...]).astype(o_ref.dtype)
        o_ref[...] = out
        ot_ref[...] = out.T

    return pl.pallas_call(
        body, grid=(s // tm,), in_specs=[pl.BlockSpec((tm, d), lambda i: (i, 0)), pl.BlockSpec((1, d), lambda i: (0, 0))],
        out_specs=[pl.BlockSpec((tm, d), lambda i: (i, 0)), pl.BlockSpec((d, tm), lambda i: (0, i))],
        out_shape=[SDS((s, d), BF16), SDS((d, s), BF16)], name=name, compiler_params=_cp("parallel"))(x, w)


def _proj_norm_bwd(da, wt, da2, wt2, x, w, dres, name, exchange=()):
    s, k = da.shape
    d = x.shape[1]
    tm = _pick(s, (512, 128))
    tk = _pick(k, (2048, 1408, 1024, 512))
    nk = k // tk
    nx = len(exchange)
    grid = (s // tm, nk)

    def body(*refs):
        a_ref, b_ref, a2_ref, b2_ref, x_ref, w_ref, dres_ref = refs[:7]
        xch_src = refs[7:7 + nx]
        dx_ref, dxb_ref, dw_ref = refs[7 + nx:10 + nx]
        xch_dst = refs[10 + nx:10 + 2 * nx]
        acc = refs[10 + 2 * nx]
        xch_sems = refs[11 + 2 * nx:]
        i, kk = pl.program_id(0), pl.program_id(1)

        @pl.when((i == 0) & (kk == 0))
        def _():
            dw_ref[...] = jnp.zeros_like(dw_ref)
            if nx:
                _exchange_start(xch_src, xch_dst, *xch_sems)

        @pl.when(kk == 0)
        def _():
            acc[...] = lax.dot_general(a2_ref[...], b2_ref[...], NT, preferred_element_type=F32)

        acc[...] += lax.dot_general(a_ref[...], b_ref[...], NT, preferred_element_type=F32)

        @pl.when(kk == nk - 1)
        def _():
            dh = acc[...]
            xf = x_ref[...]
            r = lax.rsqrt(jnp.mean(xf * xf, axis=-1, keepdims=True) + EPS)
            xhat = xf * r
            g = dh * w_ref[...]
            dx = dres_ref[...] + r * (g - xhat * jnp.mean(g * xhat, axis=-1, keepdims=True))
            dx_ref[...] = dx
            dxb_ref[...] = dx.astype(dxb_ref.dtype)
            dw_ref[...] += jnp.sum(dh * xhat, axis=0, keepdims=True)

        if nx:
            @pl.when((i == grid[0] - 1) & (kk == nk - 1))
            def _():
                _exchange_wait(xch_src, xch_dst, *xch_sems)

    row = pl.BlockSpec((tm, d), lambda i, kk: (i, 0))
    vec = pl.BlockSpec((1, d), lambda i, kk: (0, 0))
    any_spec = pl.BlockSpec(memory_space=pl.ANY)
    in_specs = [pl.BlockSpec((tm, tk), lambda i, kk: (i, kk)), pl.BlockSpec((d, tk), lambda i, kk: (0, kk)),
                pl.BlockSpec((tm, da2.shape[1]), lambda i, kk: (i, 0)), pl.BlockSpec((d, wt2.shape[1]), lambda i, kk: (0, 0)),
                row, vec, row]
    return pl.pallas_call(
        body, grid=grid, in_specs=in_specs + [any_spec] * nx, out_specs=[row, row, vec] + [any_spec] * nx,
        out_shape=[SDS((s, d), F32), SDS((s, d), BF16), SDS((1, d), F32)] + [SDS(e.shape, e.dtype) for e in exchange],
        scratch_shapes=[pltpu.VMEM((tm, d), F32)] + (_exchange_sems(nx) if nx else []),
        name=name, compiler_params=_cp("arbitrary", "arbitrary"))(da, wt, da2, wt2, x, w, dres, *exchange)


def _final_norm_loss(x, w, target, name):
    s, d = x.shape
    tm = _pick(s, (512, 128))

    def body(x_ref, w_ref, t_ref, dx_ref, dxb_ref, dw_ref, loss_ref):
        xf = x_ref[...]
        r = lax.rsqrt(jnp.mean(xf * xf, axis=-1, keepdims=True) + EPS)
        xhat = xf * r
        wv = w_ref[...]
        e = xhat * wv - t_ref[...]
        dy = e * (1.0 / d)
        g = dy * wv
        dx = r * (g - xhat * jnp.mean(g * xhat, axis=-1, keepdims=True))
        dx_ref[...] = dx
        dxb_ref[...] = dx.astype(dxb_ref.dtype)

        @pl.when(pl.program_id(0) == 0)
        def _():
            dw_ref[...] = jnp.zeros_like(dw_ref)
            loss_ref[...] = jnp.zeros_like(loss_ref)

        dw_ref[...] += jnp.sum(dy * xhat, axis=0, keepdims=True)
        loss_ref[...] += jnp.sum(jnp.sum(e * e, axis=1, keepdims=True), axis=0, keepdims=True) * (0.5 / d)

    row = pl.BlockSpec((tm, d), lambda i: (i, 0))
    vec = pl.BlockSpec((1, d), lambda i: (0, 0))
    return pl.pallas_call(
        body, grid=(s // tm,), in_specs=[row, vec, row], out_specs=[row, row, vec, pl.BlockSpec((1, 128), lambda i: (0, 0))],
        out_shape=[SDS((s, d), F32), SDS((s, d), BF16), SDS((1, d), F32), SDS((1, 128), F32)],
        name=name, compiler_params=_cp("arbitrary"))(x, w, target)


def _rope_tables(s):
    half = ROPE_DIM // 2
    f32 = np.float32
    inv_freq = np.power(f32(ROPE_THETA), -np.arange(half, dtype=f32) * f32(2.0) / f32(ROPE_DIM)).astype(f32)
    ang = (np.arange(s, dtype=f32)[:, None] * inv_freq[None, :]).astype(f32)
    cos, sin = np.cos(ang).astype(f32), np.sin(ang).astype(f32)
    z = np.zeros((s, HEAD_DIM - ROPE_DIM), f32)
    zh = np.zeros((s, half), f32)
    c = np.concatenate([cos, cos, z + 1.0], axis=1)
    sa = np.concatenate([zh, sin, z], axis=1)
    sb = np.concatenate([-sin, zh, z], axis=1)
    two = lambda t: np.concatenate([t, t], axis=1)
    c, sa, sb = two(c), two(sa), two(sb)
    fwd, bwd = (c, sa, sb), (c, np.roll(sb, half, axis=1), np.roll(sa, -half, axis=1))
    return tuple(jnp.asarray(t) for t in fwd), tuple(jnp.asarray(t) for t in bwd)


PERM_TILE = 256


def _perm_matrices(d):
    n = PERM_TILE // d
    o = np.arange(PERM_TILE)
    p = np.zeros((PERM_TILE, PERM_TILE), np.float32)
    p[o, (o % n) * d + o // n] = 1.0
    return jnp.asarray(p, dtype=BF16), jnp.asarray(p.T.copy(), dtype=BF16)


def _store_pattern(o_ref, tile, perm_ref, d, cols=slice(None)):
    n = PERM_TILE // d
    z = jnp.dot(perm_ref[...], tile, preferred_element_type=F32).astype(o_ref.dtype)
    for r in range(d):
        o_ref[r, :, cols] = z[r * n:(r + 1) * n]


def _load_pattern(x_ref, perm_ref, d):
    tile = jnp.concatenate([x_ref[r] for r in range(d)], axis=0)
    return jnp.dot(perm_ref[...], tile, preferred_element_type=F32)


def _pattern_spec(d, w, col=0):
    return pl.BlockSpec((d, PERM_TILE // d, w), lambda i, *_: (0, i, col))


def _rope_fwd(proj, tabs, perms, name):
    s = proj.shape[0]
    tm = PERM_TILE
    half = ROPE_DIM // 2
    wb = D_MODEL
    nd = len(DILATIONS) - 1

    def body(x_ref, c_ref, sa_ref, sb_ref, *rest):
        perm_refs, o_ref, op_refs = rest[:nd], rest[nd], rest[nd + 1:]
        is_v = pl.program_id(1) == 2
        c = jnp.where(is_v, 1.0, c_ref[...])
        sa = jnp.where(is_v, 0.0, sa_ref[...])
        sb = jnp.where(is_v, 0.0, sb_ref[...])
        for j in range(wb // 128):
            x = x_ref[:, j * 128:(j + 1) * 128].astype(F32)
            o_ref[:, j * 128:(j + 1) * 128] = (x * c + pltpu.roll(x, half, 1) * sa + pltpu.roll(x, 128 - half, 1) * sb).astype(o_ref.dtype)
        y = o_ref[...]
        for d, perm_ref, op_ref in zip(DILATIONS[1:], perm_refs, op_refs):
            _store_pattern(op_ref, y, perm_ref, d)

    blk = pl.BlockSpec((tm, wb), lambda i, j: (i, j))
    tab = pl.BlockSpec((tm, 128), lambda i, j: (i, 0))
    pm = pl.BlockSpec((tm, tm), lambda i, j: (0, 0))
    outs = pl.pallas_call(
        body, grid=(s // tm, 3), in_specs=[blk, tab, tab, tab] + [pm] * nd,
        out_specs=[blk] + [pl.BlockSpec((d, tm // d, wb), lambda i, j: (0, i, j)) for d in DILATIONS[1:]],
        out_shape=[SDS((s, 3 * wb), BF16)] + [SDS((d, s // d, 3 * wb), BF16) for d in DILATIONS[1:]],
        name=name, compiler_params=_cp("parallel", "parallel"))(proj, *tabs, *[perms[d][0] for d in DILATIONS[1:]])
    return [o.reshape(s, 3 * wb) for o in outs]


def _sum3_rope(ds_, perms, tabs, name, slab=None):
    s, w = ds_[0].shape
    tm = PERM_TILE
    half = ROPE_DIM // 2
    nd = len(DILATIONS) - 1

    def body(*refs):
        x_refs, perm_refs = refs[:nd + 1], refs[nd + 1:2 * nd + 1]
        tab_refs = refs[2 * nd + 1:2 * nd + 4]
        tot = x_refs[0][...].astype(F32)
        for d, x_ref, perm_ref in zip(DILATIONS[1:], x_refs[1:], perm_refs):
            tot = tot + _load_pattern(x_ref, perm_ref, d)
        for j in range(w // 128):
            x = tot[:, j * 128:(j + 1) * 128]
            if tabs is not None:
                x = x * tab_refs[0][...] + pltpu.roll(x, half, 1) * tab_refs[1][...] + pltpu.roll(x, 128 - half, 1) * tab_refs[2][...]
            refs[-1][:, j * 128:(j + 1) * 128] = x.astype(refs[-1].dtype)

    blk = pl.BlockSpec((tm, w), lambda i: (i, 0))
    tab = pl.BlockSpec((tm, 128), lambda i: (i, 0))
    pm = pl.BlockSpec((tm, tm), lambda i: (0, 0))
    extra = [] if tabs is None else list(tabs)
    out_shape, col0, more, more_specs, alias = _slab(s, w, BF16, slab, 2 * nd + 1 + len(extra))
    cb = col0 // w
    args = [ds_[0]] + [x.reshape(d, s // d, w) for d, x in zip(DILATIONS[1:], ds_[1:])] + [perms[d][1] for d in DILATIONS[1:]]
    return pl.pallas_call(
        body, grid=(s // tm,),
        in_specs=[blk] + [_pattern_spec(d, w) for d in DILATIONS[1:]] + [pm] * nd + [tab] * len(extra) + more_specs,
        out_specs=pl.BlockSpec((tm, w), lambda i: (i, cb)), out_shape=out_shape, input_output_aliases=alias,
        name=name, compiler_params=_cp("parallel"))(*args, *extra, *more)


def _band_valid(t, nq, nk, qofs, kofs, seq_len):
    qpos = t * 128 + qofs + lax.broadcasted_iota(jnp.int32, (nq, nk), 0)
    kpos = t * 128 + kofs + lax.broadcasted_iota(jnp.int32, (nq, nk), 1)
    sh = int(math.log2(seq_len))
    same = lax.shift_right_arithmetic(qpos, sh) == lax.shift_right_arithmetic(kpos, sh)
    return same & (jnp.abs(kpos - qpos) <= BAND_HALF)


def _window(r0, r1, r2):
    return jnp.concatenate([r0[128 - BAND_HALF:128], r1[...], r2[0:BAND_HALF]], axis=0)


WIN = 128 + 2 * BAND_HALF


def _first_head_lanes():
    return lax.broadcasted_iota(jnp.int32, (1, 2 * HEAD_DIM), 1) < HEAD_DIM


def _only(x, keep):
    return jnp.where(keep, x, jnp.zeros((), x.dtype))


def _win_specs(width, col, nt):
    return [pl.BlockSpec((128, width), lambda t: (jnp.maximum(t - 1, 0), col)),
            pl.BlockSpec((128, width), lambda t: (t, col)),
            pl.BlockSpec((128, width), lambda t: (jnp.minimum(t + 1, nt - 1), col))]


def _attn_fwd(qk, v_src, v_col, seq_len, name):
    s = qk.shape[0]
    nt = s // 128
    dm = D_MODEL

    def body(q_ref, k0, k1, k2, v0, v1, v2, o_ref, lse_ref):
        t = pl.program_id(0)
        valid = _band_valid(t, 128, WIN, 0, -BAND_HALF, seq_len)
        q = q_ref[...]
        kc = _window(k0, k1, k2)
        vc = _window(v0, v1, v2)
        first = _first_head_lanes()
        outs, lses = [], []
        for pr in range(N_HEADS // 2):
            ps = slice(pr * 128, (pr + 1) * 128)
            qp, kp, vp = q[:, ps], kc[:, ps], vc[:, ps]
            halves = []
            for keep in (first, ~first):
                sc = lax.dot_general(_only(qp, keep), kp, NT, preferred_element_type=F32) * (HEAD_DIM ** -0.5)
                sc = jnp.where(valid, sc, NEG)
                m = jnp.max(sc, axis=1, keepdims=True)
                e = jnp.exp(sc - m)
                den = jnp.sum(e, axis=1, keepdims=True)
                halves.append(jnp.dot(e.astype(BF16), vp, preferred_element_type=F32) / den)
                lses.append(m + jnp.log(den))
            outs.append(jnp.where(first, halves[0], halves[1]))
        o_ref[...] = jnp.concatenate(outs, axis=1).astype(o_ref.dtype)
        lse_ref[...] = jnp.concatenate(lses, axis=1)

    in_specs = [pl.BlockSpec((128, dm), lambda t: (t, 0))] + _win_specs(dm, 1, nt) + _win_specs(dm, v_col, nt)
    return pl.pallas_call(
        body, grid=(nt,), in_specs=in_specs,
        out_specs=[pl.BlockSpec((128, dm), lambda t: (t, 0)), pl.BlockSpec((128, N_HEADS), lambda t: (t, 0))],
        out_shape=[SDS((s, dm), BF16), SDS((s, N_HEADS), F32)], name=name, compiler_params=_cp("parallel"))(
            qk, qk, qk, qk, v_src, v_src, v_src)


def _attn_combine(os_, lses, perms, expand, name):
    s, dm = os_[0].shape
    tm = PERM_TILE
    nd = len(DILATIONS) - 1

    def body(o1, o2, o3, l1, l2, l3, p2, p3, e_ref, out_ref, out_t_ref, lt_ref):
        ls = [l1[...], l2[...], l3[...]]
        m = jnp.maximum(jnp.maximum(ls[0], ls[1]), ls[2])
        es = [jnp.exp(l - m) for l in ls]
        tot = es[0] + es[1] + es[2]
        lt_ref[...] = m + jnp.log(tot)
        ovs = [o1[...].astype(F32), _load_pattern(o2, p2, DILATIONS[1]), _load_pattern(o3, p3, DILATIONS[2])]
        acc = jnp.zeros((tm, dm), F32)
        for e, o in zip(es, ovs):
            acc = acc + jnp.dot(e / tot, e_ref[...], precision=HIGH, preferred_element_type=F32) * o
        out = acc.astype(out_ref.dtype)
        out_ref[...] = out
        out_t_ref[...] = out.T

    row = pl.BlockSpec((tm, dm), lambda i: (i, 0))
    st = pl.BlockSpec((tm, N_HEADS), lambda i: (i, 0))
    pm = pl.BlockSpec((tm, tm), lambda i: (0, 0))
    args = [os_[0]] + [o.reshape(d, s // d, dm) for d, o in zip(DILATIONS[1:], os_[1:])]
    return pl.pallas_call(
        body, grid=(s // tm,),
        in_specs=[row] + [_pattern_spec(d, dm) for d in DILATIONS[1:]] + [st, st, st, pm, pm, pl.BlockSpec((N_HEADS, dm), lambda i: (0, 0))],
        out_specs=[row, pl.BlockSpec((dm, tm), lambda i: (0, i)), st],
        out_shape=[SDS((s, dm + D_INNER), BF16), SDS((dm + D_INNER, s), BF16), SDS((s, N_HEADS), F32)],
        name=name, compiler_params=_cp("parallel"))(*args, *lses, *[perms[d][1] for d in DILATIONS[1:]], expand)


def _attn_delta(dmix, attn, expand_t, perms, name):
    s, dm = attn.shape[0], D_MODEL
    tm = PERM_TILE
    nd = len(DILATIONS) - 1

    def body(d_ref, a_ref, e_ref, *rest):
        perm_refs, o_ref, op_refs = rest[:nd], rest[nd], rest[nd + 1:]
        dv = d_ref[...]
        prod = dv.astype(F32) * a_ref[...].astype(F32)
        o_ref[...] = jnp.dot(prod, e_ref[...], precision=HIGH, preferred_element_type=F32)
        for d, perm_ref, op_ref in zip(DILATIONS[1:], perm_refs, op_refs):
            _store_pattern(op_ref, dv, perm_ref, d)

    row = pl.BlockSpec((tm, dm), lambda i: (i, 0))
    pm = pl.BlockSpec((tm, tm), lambda i: (0, 0))
    outs = pl.pallas_call(
        body, grid=(s // tm,), in_specs=[row, row, pl.BlockSpec((dm, N_HEADS), lambda i: (0, 0))] + [pm] * nd,
        out_specs=[pl.BlockSpec((tm, N_HEADS), lambda i: (i, 0))] + [_pattern_spec(d, dm) for d in DILATIONS[1:]],
        out_shape=[SDS((s, N_HEADS), F32)] + [SDS((d, s // d, dm), BF16) for d in DILATIONS[1:]],
        name=name, compiler_params=_cp("parallel"))(dmix, attn, expand_t, *[perms[d][0] for d in DILATIONS[1:]])
    return outs[0], [o.reshape(s, dm) for o in outs[1:]]


def _attn_bwd_dq(qk, v_src, v_col, do_src, lse, delta, seq_len, name):
    s = qk.shape[0]
    nt = s // 128
    dm = D_MODEL

    def body(q_ref, k0, k1, k2, v0, v1, v2, do_ref, lse_ref, dl_ref, dq_ref):
        t = pl.program_id(0)
        valid = _band_valid(t, 128, WIN, 0, -BAND_HALF, seq_len)
        q = q_ref[...]
        do = do_ref[...]
        kc = _window(k0, k1, k2)
        vc = _window(v0, v1, v2)
        lse_v, dl_v = lse_ref[...], dl_ref[...]
        first = _first_head_lanes()
        outs = []
        for pr in range(N_HEADS // 2):
            ps = slice(pr * 128, (pr + 1) * 128)
            qp, dop, kp, vp = q[:, ps], do[:, ps], kc[:, ps], vc[:, ps]
            halves = []
            for i, keep in enumerate((first, ~first)):
                h = 2 * pr + i
                sc = lax.dot_general(_only(qp, keep), kp, NT, preferred_element_type=F32) * (HEAD_DIM ** -0.5)
                p = jnp.exp(jnp.where(valid, sc - lse_v[:, h:h + 1], NEG))
                dp = lax.dot_general(_only(dop, keep), vp, NT, preferred_element_type=F32)
                ds = p * (dp - dl_v[:, h:h + 1])
                halves.append(jnp.dot(ds.astype(BF16), kp, preferred_element_type=F32))
            outs.append(jnp.where(first, halves[0], halves[1]) * (HEAD_DIM ** -0.5))
        dq_ref[...] = jnp.concatenate(outs, axis=1).astype(dq_ref.dtype)

    row = pl.BlockSpec((128, dm), lambda t: (t, 0))
    st = pl.BlockSpec((128, N_HEADS), lambda t: (t, 0))
    in_specs = [row] + _win_specs(dm, 1, nt) + _win_specs(dm, v_col, nt) + [row, st, st]
    return pl.pallas_call(
        body, grid=(nt,), in_specs=in_specs, out_specs=row, out_shape=SDS((s, dm), BF16),
        name=name, compiler_params=_cp("parallel"))(qk, qk, qk, qk, v_src, v_src, v_src, do_src, lse, delta)


def _attn_bwd_dkv(qk, v_src, v_col, do_src, lse_t, delta_t, seq_len, name):
    s = qk.shape[0]
    nt = s // 128
    dm = D_MODEL

    def lane_window(r0, r1, r2):
        return jnp.concatenate([r0[:, 128 - BAND_HALF:128], r1[...], r2[:, 0:BAND_HALF]], axis=1)

    def body(k_ref, v_ref, q0, q1, q2, d0, d1, d2, l0, l1, l2, e0, e1, e2, dk_ref, dv_ref):
        t = pl.program_id(0)
        valid = _band_valid(t, 128, WIN, 0, -BAND_HALF, seq_len)
        k = k_ref[...]
        v = v_ref[...]
        qc = _window(q0, q1, q2)
        dc = _window(d0, d1, d2)
        lse_v = lane_window(l0, l1, l2)
        dl_v = lane_window(e0, e1, e2)
        first = _first_head_lanes()
        dks, dvs = [], []
        for pr in range(N_HEADS // 2):
            ps = slice(pr * 128, (pr + 1) * 128)
            kp, vp, qp, dop = k[:, ps], v[:, ps], qc[:, ps], dc[:, ps]
            dk_h, dv_h = [], []
            for i, keep in enumerate((first, ~first)):
                h = 2 * pr + i
                sc = lax.dot_general(_only(kp, keep), qp, NT, preferred_element_type=F32) * (HEAD_DIM ** -0.5)
                p = jnp.exp(jnp.where(valid, sc - lse_v[h:h + 1, :], NEG))
                dv_h.append(jnp.dot(p.astype(BF16), dop, preferred_element_type=F32))
                dp = lax.dot_general(_only(vp, keep), dop, NT, preferred_element_type=F32)
                ds = p * (dp - dl_v[h:h + 1, :])
                dk_h.append(jnp.dot(ds.astype(BF16), qp, preferred_element_type=F32))
            dks.append(jnp.where(first, dk_h[0], dk_h[1]) * (HEAD_DIM ** -0.5))
            dvs.append(jnp.where(first, dv_h[0], dv_h[1]))
        dk_ref[...] = jnp.concatenate(dks, axis=1).astype(dk_ref.dtype)
        dv_ref[...] = jnp.concatenate(dvs, axis=1).astype(dv_ref.dtype)

    row = pl.BlockSpec((128, dm), lambda t: (t, 0))
    stat = [pl.BlockSpec((N_HEADS, 128), lambda t: (0, jnp.maximum(t - 1, 0))), pl.BlockSpec((N_HEADS, 128), lambda t: (0, t)),
            pl.BlockSpec((N_HEADS, 128), lambda t: (0, jnp.minimum(t + 1, nt - 1)))]
    in_specs = ([pl.BlockSpec((128, dm), lambda t: (t, 1)), pl.BlockSpec((128, dm), lambda t: (t, v_col))]
                + _win_specs(dm, 0, nt) + _win_specs(dm, 0, nt) + stat + stat)
    return pl.pallas_call(
        body, grid=(nt,), in_specs=in_specs, out_specs=[row, row], out_shape=[SDS((s, dm), BF16), SDS((s, dm), BF16)],
        name=name, compiler_params=_cp("parallel"))(qk, v_src, qk, qk, qk, do_src, do_src, do_src, lse_t, lse_t, lse_t, delta_t, delta_t, delta_t)


CONV_COLS = (1024, 1408, 512, 256)


def _halo_specs(tm, tc, col0, nrow_blocks):
    r = tm // 16
    return [pl.BlockSpec((16, tc), lambda i, j: (jnp.maximum(i * r - 1, 0), col0 + j)),
            pl.BlockSpec((16, tc), lambda i, j: (jnp.minimum((i + 1) * r, nrow_blocks * r - 1), col0 + j))]


def _shifted(x_ref, hp_ref, hn_ref, i, last):
    x = x_ref[...].astype(F32)
    tm = x.shape[0]
    rows = lax.broadcasted_iota(jnp.int32, x.shape, 0)
    prev_row = jnp.where(i > 0, hp_ref[15:16, :].astype(F32), 0.0)
    next_row = jnp.where(i < last, hn_ref[0:1, :].astype(F32), 0.0)
    xp = jnp.where(rows == 0, prev_row, pltpu.roll(x, 1, 0))
    xn = jnp.where(rows == tm - 1, next_row, pltpu.roll(x, tm - 1, 0))
    return xp, x, xn


def _conv(x_src, col0, width, w3, bias, act, name, out_dtype=BF16, slab=None, transpose=False, wcol0=0):
    s = x_src.shape[0]
    tm = _pick(s, (256, 128))
    tc = _pick(width, CONV_COLS)
    nb = s // tm
    c0 = col0 // tc
    wc0 = wcol0 // tc

    def body(*refs):
        x_ref, hp_ref, hn_ref, w_ref, b_ref = refs[:5]
        o_ref = refs[-1]
        i = pl.program_id(0)
        xp, x, xn = _shifted(x_ref, hp_ref, hn_ref, i, nb - 1)
        w = w_ref[...]
        if transpose:
            y = w[2:3, :] * xp + w[1:2, :] * x + w[0:1, :] * xn
        else:
            y = w[0:1, :] * xp + w[1:2, :] * x + w[2:3, :] * xn + b_ref[...]
        if act:
            y = y * _sigmoid(y)
        o_ref[...] = y.astype(o_ref.dtype)

    in_specs = ([pl.BlockSpec((tm, tc), lambda i, j: (i, c0 + j))] + _halo_specs(tm, tc, c0, nb)
                + [pl.BlockSpec((3, tc), lambda i, j: (0, wc0 + j)), pl.BlockSpec((1, tc), lambda i, j: (0, wc0 + j))])
    out_shape, ocol, more, more_specs, alias = _slab(s, width, out_dtype, slab, 5)
    ob = ocol // tc
    return pl.pallas_call(
        body, grid=(nb, width // tc), in_specs=in_specs + more_specs, out_specs=pl.BlockSpec((tm, tc), lambda i, j: (i, ob + j)),
        out_shape=out_shape, input_output_aliases=alias, name=name, compiler_params=_cp("parallel", "parallel"))(
            x_src, x_src, x_src, w3, bias, *more)


def _conv_silu_bwd(x_src, col0, width, w3, bias, addends, add_widths, name):
    s = x_src.shape[0]
    tm = _pick(s, (256, 128))
    tc = _pick(width, CONV_COLS)
    nb = s // tm
    c0 = col0 // tc
    na = len(addends)

    def body(*refs):
        x_ref, hp_ref, hn_ref, w_ref, b_ref = refs[:5]
        a_refs = refs[5:5 + na]
        dp_ref, dw_ref, db_ref = refs[5 + na:]
        i, j = pl.program_id(1), pl.program_id(0)
        xp, x, xn = _shifted(x_ref, hp_ref, hn_ref, i, nb - 1)
        w = w_ref[...]
        pre = w[0:1, :] * xp + w[1:2, :] * x + w[2:3, :] * xn + b_ref[...]
        g = jnp.zeros_like(pre)
        for a_ref, aw in zip(a_refs, add_widths):
            av = a_ref[...].astype(F32)
            g = g + (av if aw == width else jnp.where(j < aw // tc, av, 0.0))
        sg = _sigmoid(pre)
        dpre = g * (sg * (1.0 + pre * (1.0 - sg)))
        dp_ref[...] = dpre.astype(dp_ref.dtype)

        @pl.when(i == 0)
        def _():
            dw_ref[...] = jnp.zeros_like(dw_ref)
            db_ref[...] = jnp.zeros_like(db_ref)

        dw_ref[...] += jnp.concatenate([jnp.sum(dpre * xp, axis=0, keepdims=True), jnp.sum(dpre * x, axis=0, keepdims=True),
                                        jnp.sum(dpre * xn, axis=0, keepdims=True)], axis=0)
        db_ref[...] += jnp.sum(dpre, axis=0, keepdims=True)

    r = tm // 16
    in_specs = [pl.BlockSpec((tm, tc), lambda j, i: (i, c0 + j)),
                pl.BlockSpec((16, tc), lambda j, i: (jnp.maximum(i * r - 1, 0), c0 + j)),
                pl.BlockSpec((16, tc), lambda j, i: (jnp.minimum((i + 1) * r, nb * r - 1), c0 + j)),
                pl.BlockSpec((3, tc), lambda j, i: (0, j)), pl.BlockSpec((1, tc), lambda j, i: (0, j))]
    for aw in add_widths:
        nblk = aw // tc
        in_specs.append(pl.BlockSpec((tm, tc), lambda j, i, nblk=nblk: (i, jnp.minimum(j, nblk - 1))))
    return pl.pallas_call(
        body, grid=(width // tc, nb), in_specs=in_specs,
        out_specs=[pl.BlockSpec((tm, tc), lambda j, i: (i, j)), pl.BlockSpec((3, tc), lambda j, i: (0, j)), pl.BlockSpec((1, tc), lambda j, i: (0, j))],
        out_shape=[SDS((s, width), BF16), SDS((3, width), F32), SDS((1, width), F32)],
        name=name, compiler_params=_cp("parallel", "arbitrary"))(x_src, x_src, x_src, w3, bias, *addends)


def _ffn_gate_fwd(u, w3, bias, name):
    s = u.shape[0]
    tm = _pick(s, (256, 128))
    tc = _pick(D_FF, CONV_COLS)
    nb = s // tm
    nj = D_FF // tc

    def body(g_ref, gp, gn, u_ref, up, un, wg_ref, wu_ref, bg_ref, bu_ref, o_ref, ot_ref):
        i = pl.program_id(0)
        outs = []
        for (x_ref, hp, hn, w_ref, b_ref) in ((g_ref, gp, gn, wg_ref, bg_ref), (u_ref, up, un, wu_ref, bu_ref)):
            xp, x, xn = _shifted(x_ref, hp, hn, i, nb - 1)
            w = w_ref[...]
            outs.append(w[0:1, :] * xp + w[1:2, :] * x + w[2:3, :] * xn + b_ref[...])
        gate, upv = outs
        out = (gate * _sigmoid(gate) * upv).astype(o_ref.dtype)
        o_ref[...] = out
        ot_ref[...] = out.T

    def xspecs(c0):
        return [pl.BlockSpec((tm, tc), lambda i, j: (i, c0 + j))] + _halo_specs(tm, tc, c0, nb)

    in_specs = (xspecs(0) + xspecs(nj)
                + [pl.BlockSpec((3, tc), lambda i, j: (0, j)), pl.BlockSpec((3, tc), lambda i, j: (0, nj + j)),
                   pl.BlockSpec((1, tc), lambda i, j: (0, j)), pl.BlockSpec((1, tc), lambda i, j: (0, nj + j))])
    return pl.pallas_call(
        body, grid=(nb, nj), in_specs=in_specs,
        out_specs=[pl.BlockSpec((tm, tc), lambda i, j: (i, j)), pl.BlockSpec((tc, tm), lambda i, j: (j, i))],
        out_shape=[SDS((s, D_FF), BF16), SDS((D_FF, s), BF16)], name=name, compiler_params=_cp("parallel", "parallel"))(
            u, u, u, u, u, u, w3, w3, bias, bias)


def _ffn_gate_bwd(u, w3, bias, dact, name):
    s = u.shape[0]
    tm = _pick(s, (256, 128))
    tc = _pick(D_FF, CONV_COLS)
    nb = s // tm
    nj = D_FF // tc

    def body(g_ref, gp, gn, u_ref, up, un, wg_ref, wu_ref, bg_ref, bu_ref, da_ref, dg_ref, du_ref, dwg_ref, dwu_ref, dbg_ref, dbu_ref):
        i = pl.program_id(1)
        sh, pre = [], []
        for (x_ref, hp, hn, w_ref, b_ref) in ((g_ref, gp, gn, wg_ref, bg_ref), (u_ref, up, un, wu_ref, bu_ref)):
            xs3 = _shifted(x_ref, hp, hn, i, nb - 1)
            w = w_ref[...]
            sh.append(xs3)
            pre.append(w[0:1, :] * xs3[0] + w[1:2, :] * xs3[1] + w[2:3, :] * xs3[2] + b_ref[...])
        gate, upv = pre
        da = da_ref[...].astype(F32)
        sg = _sigmoid(gate)
        dgate = da * upv * (sg * (1.0 + gate * (1.0 - sg)))
        dup = da * gate * sg
        dg_ref[...] = dgate.astype(dg_ref.dtype)
        du_ref[...] = dup.astype(du_ref.dtype)

        @pl.when(i == 0)
        def _():
            for r in (dwg_ref, dwu_ref, dbg_ref, dbu_ref):
                r[...] = jnp.zeros_like(r)

        for d, xs3, dw_ref, db_ref in ((dgate, sh[0], dwg_ref, dbg_ref), (dup, sh[1], dwu_ref, dbu_ref)):
            dw_ref[...] += jnp.concatenate([jnp.sum(d * xs3[0], axis=0, keepdims=True), jnp.sum(d * xs3[1], axis=0, keepdims=True),
                                            jnp.sum(d * xs3[2], axis=0, keepdims=True)], axis=0)
            db_ref[...] += jnp.sum(d, axis=0, keepdims=True)

    r = tm // 16

    def xspecs(c0):
        return [pl.BlockSpec((tm, tc), lambda j, i: (i, c0 + j)),
                pl.BlockSpec((16, tc), lambda j, i: (jnp.maximum(i * r - 1, 0), c0 + j)),
                pl.BlockSpec((16, tc), lambda j, i: (jnp.minimum((i + 1) * r, nb * r - 1), c0 + j))]

    in_specs = (xspecs(0) + xspecs(nj)
                + [pl.BlockSpec((3, tc), lambda j, i: (0, j)), pl.BlockSpec((3, tc), lambda j, i: (0, nj + j)),
                   pl.BlockSpec((1, tc), lambda j, i: (0, j)), pl.BlockSpec((1, tc), lambda j, i: (0, nj + j)),
                   pl.BlockSpec((tm, tc), lambda j, i: (i, j))])
    blk = pl.BlockSpec((tm, tc), lambda j, i: (i, j))
    w_o = pl.BlockSpec((3, tc), lambda j, i: (0, j))
    b_o = pl.BlockSpec((1, tc), lambda j, i: (0, j))
    return pl.pallas_call(
        body, grid=(nj, nb), in_specs=in_specs, out_specs=[blk, blk, w_o, w_o, b_o, b_o],
        out_shape=[SDS((s, D_FF), BF16), SDS((s, D_FF), BF16), SDS((3, D_FF), F32), SDS((3, D_FF), F32), SDS((1, D_FF), F32), SDS((1, D_FF), F32)],
        name=name, compiler_params=_cp("parallel", "arbitrary"))(u, u, u, u, u, u, w3, w3, bias, bias, dact)


def _exchange_sems(n):
    return [pltpu.SemaphoreType.DMA((7 * n,)), pltpu.SemaphoreType.DMA((7 * n,)), pltpu.SemaphoreType.DMA((n,))]


def _exchange_copies(srcs, outs, send_sems, recv_sems, local_sems):
    x, y, c = lax.axis_index("x"), lax.axis_index("y"), lax.axis_index("c")
    me = 4 * x + 2 * y + c
    locals_ = [pltpu.make_async_copy(srcs[a].at[me], outs[a].at[me], local_sems.at[a]) for a in range(len(srcs))]
    sends, recvs = [], []
    for k in range(1, N_DEV):
        px, py, pc = x ^ ((k >> 2) & 1), y ^ ((k >> 1) & 1), c ^ (k & 1)
        peer = 4 * px + 2 * py + pc
        for a in range(len(srcs)):
            sems = dict(send_sem=send_sems.at[a * 7 + k - 1], recv_sem=recv_sems.at[a * 7 + k - 1], device_id_type=MESH)
            sends.append(pltpu.make_async_remote_copy(src_ref=srcs[a].at[peer], dst_ref=outs[a].at[me], device_id=(px, py, pc), **sems))
            recvs.append(pltpu.make_async_remote_copy(src_ref=srcs[a].at[peer], dst_ref=outs[a].at[peer], device_id=(x, y, c), **sems))
    return locals_, sends, recvs


def _exchange_start(srcs, outs, send_sems, recv_sems, local_sems):
    locals_, sends, _ = _exchange_copies(srcs, outs, send_sems, recv_sems, local_sems)
    for cp in locals_ + sends:
        cp.start()


def _exchange_wait(srcs, outs, send_sems, recv_sems, local_sems):
    locals_, sends, recvs = _exchange_copies(srcs, outs, send_sems, recv_sems, local_sems)
    for cp in recvs:
        cp.wait_recv()
    for cp in sends:
        cp.wait_send()
    for cp in locals_:
        cp.wait()


def _gather_copies(srcs, outs, send_sems, recv_sems, local_sems):
    x, y, c = lax.axis_index("x"), lax.axis_index("y"), lax.axis_index("c")
    me, sibling = (x, y, c), (x, y, 1 - c)
    chips = [(1 - x, y), (x, 1 - y), (1 - x, 1 - y)]

    def copy(a, k, block, to, src=None):
        dst = outs[a].at[4 * block[0] + 2 * block[1] + block[2]]
        return pltpu.make_async_remote_copy(
            src_ref=dst if src is None else src, dst_ref=dst,
            send_sem=send_sems.at[a * 7 + k], recv_sem=recv_sems.at[a * 7 + k], device_id=to, device_id_type=MESH)

    n = len(srcs)
    locals_ = [pltpu.make_async_copy(srcs[a], outs[a].at[4 * x + 2 * y + c], local_sems.at[a]) for a in range(n)]
    own = [copy(a, 0, me, sibling, src=srcs[a]) for a in range(n)]
    own += [copy(a, 1 + j, me, (*chip, c), src=srcs[a]) for a in range(n) for j, chip in enumerate(chips)]
    landed_ici = [copy(a, 1 + j, (*chip, c), me) for j, chip in enumerate(chips) for a in range(n)]
    passed = [copy(a, 4 + j, (*chip, c), sibling) for j, chip in enumerate(chips) for a in range(n)]
    landed_d2d = [copy(a, 0, sibling, me) for a in range(n)]
    landed_d2d += [copy(a, 4 + j, (*chip, 1 - c), me) for a in range(n) for j, chip in enumerate(chips)]
    return locals_, own, landed_ici, passed, landed_d2d


def _gather_start(*refs):
    locals_, own, _, _, _ = _gather_copies(*refs)
    for cp in locals_ + own:
        cp.start()


def _gather_forward(*refs):
    _, _, landed_ici, passed, _ = _gather_copies(*refs)
    for arrived, onward in zip(landed_ici, passed):
        arrived.wait_recv()
        onward.start()


def _gather_finish(*refs):
    locals_, own, _, passed, landed_d2d = _gather_copies(*refs)
    for cp in landed_d2d:
        cp.wait_recv()
    for cp in own + passed:
        cp.wait_send()
    for cp in locals_:
        cp.wait()


def _ssd_common(dt_ref, dtt_ref, al_r, al_c, bi_r, bi_c, off, rev):
    li = lax.broadcasted_iota(jnp.int32, (CHUNK, CHUNK), 0)
    si = lax.broadcasted_iota(jnp.int32, (CHUNK, CHUNK), 1)
    mask = (li <= si) if rev else (li >= si)
    mask_t = (li >= si) if rev else (li <= si)
    a_r = -jnp.exp(al_r[...])
    a_c = -jnp.exp(al_c[...])
    pre = dt_ref[:, off:off + N_HEADS] + bi_r[...]
    dt = _softplus(pre)
    cs = jnp.dot(mask.astype(F32), dt * a_r, precision=HIGH, preferred_element_type=F32)
    dt_t = _softplus(dtt_ref[off:off + N_HEADS, :] + bi_c[...])
    cs_t = jnp.dot(dt_t * a_c, mask_t.astype(F32), precision=HIGH, preferred_element_type=F32)
    tot = cs[0:1, :] if rev else cs[CHUNK - 1:CHUNK, :]
    return mask, mask_t, a_r, pre, dt, cs, cs_t, tot


def _ssd_fwd(xbc, dt_raw, dt_t, args_f, args_b, name, gather=()):
    s = xbc.shape[0]
    nc = s // CHUNK
    hp = D_INNER // N_HEADS
    hpg = N_HEADS // N_GROUPS
    gs = N_GROUPS * D_STATE
    ng = len(gather)
    fwd_step = (nc * 27) // 32

    def chunk(x_ref, b_ref, c_ref, dt_ref, dtt_ref, alr, alc, bir, bic, y_ref, st_ref, h_scr, off, rev):
        mask, _, _, _, dt, cs, cs_t, tot = _ssd_common(dt_ref, dtt_ref, alr, alc, bir, bic, off, rev)
        first = _first_head_lanes()
        xs = x_ref[...]
        ys = []
        for g in range(N_GROUPS):
            bg = b_ref[:, g * D_STATE:(g + 1) * D_STATE]
            cg = c_ref[:, g * D_STATE:(g + 1) * D_STATE]
            gm = lax.dot_general(cg, bg, NT, preferred_element_type=F32)
            hcat = h_scr[g]
            st_ref[0, g] = hcat
            ch = lax.dot_general(cg, hcat.astype(BF16), NT, preferred_element_type=F32)
            xdd = []
            for pr in range(hpg // 2):
                h0 = g * hpg + 2 * pr
                lanes = slice(h0 * hp, (h0 + 2) * hp)
                cols = [cs[:, h:h + 1] for h in (h0, h0 + 1)]
                pair = lambda a, b: jnp.where(first, a, b)
                xdf = xs[:, lanes].astype(F32) * pair(dt[:, h0:h0 + 1], dt[:, h0 + 1:h0 + 2])
                xdb = xdf.astype(BF16)
                yh = []
                for i, h in enumerate((h0, h0 + 1)):
                    lm = jnp.exp(jnp.where(mask, cols[i] - cs_t[h:h + 1, :], NEG))
                    yh.append(jnp.dot((gm * lm).astype(BF16), xdb, preferred_element_type=F32))
                ecs = pair(jnp.exp(cols[0]), jnp.exp(cols[1]))
                ys.append(pair(yh[0], yh[1]) + ecs * ch[:, 2 * pr * hp:(2 * pr + 2) * hp])
                dec = pair(jnp.exp(tot[:, h0:h0 + 1] - cols[0]), jnp.exp(tot[:, h0 + 1:h0 + 2] - cols[1]))
                xdd.append((xdf * dec).astype(BF16))
            snew = lax.dot_general(jnp.concatenate(xdd, axis=1), bg, TN, preferred_element_type=F32)
            for r in range(hpg):
                rs = slice(r * hp, (r + 1) * hp)
                h_scr[g, rs, :] = jnp.exp(tot[:, g * hpg + r:g * hpg + r + 1]) * hcat[rs] + snew[rs]
        y_ref[...] = jnp.concatenate(ys, axis=1)

    def body(*refs):
        in_f, in_b = refs[0:9], refs[9:18]
        g_src = refs[18:18 + ng]
        out_f, out_b = refs[18 + ng:20 + ng], refs[20 + ng:22 + ng]
        g_dst = refs[22 + ng:22 + 2 * ng]
        hs_f, hs_b = refs[22 + 2 * ng], refs[23 + 2 * ng]
        g_sems = refs[24 + 2 * ng:]
        step = pl.program_id(0)

        @pl.when(step == 0)
        def _():
            hs_f[...] = jnp.zeros_like(hs_f)
            hs_b[...] = jnp.zeros_like(hs_b)
            if ng:
                _gather_start(g_src, g_dst, *g_sems)

        chunk(*in_f, *out_f, hs_f, 0, False)
        chunk(*in_b, *out_b, hs_b, N_HEADS, True)

        if ng:
            @pl.when(step == fwd_step)
            def _():
                _gather_forward(g_src, g_dst, *g_sems)

            @pl.when(step == nc - 1)
            def _():
                _gather_finish(g_src, g_dst, *g_sems)

    small = lambda shape: pl.BlockSpec(shape, lambda c: (0, 0))

    def specs(cm):
        ins = [pl.BlockSpec((CHUNK, D_INNER), lambda c: (cm(c), 0)),
               pl.BlockSpec((CHUNK, gs), lambda c: (cm(c), D_INNER // gs)),
               pl.BlockSpec((CHUNK, gs), lambda c: (cm(c), D_INNER // gs + 1)),
               pl.BlockSpec((CHUNK, 128), lambda c: (cm(c), 0)),
               pl.BlockSpec((2 * N_HEADS, CHUNK), lambda c: (0, cm(c))),
               small((1, N_HEADS)), small((N_HEADS, 1)), small((1, N_HEADS)), small((N_HEADS, 1))]
        outs = [pl.BlockSpec((CHUNK, D_INNER), lambda c: (cm(c), 0)),
                pl.BlockSpec((1, N_GROUPS, hpg * hp, D_STATE), lambda c: (cm(c), 0, 0, 0))]
        return ins, outs

    ins_f, outs_f = specs(lambda c: c)
    ins_b, outs_b = specs(lambda c: nc - 1 - c)
    any_spec = pl.BlockSpec(memory_space=pl.ANY)
    one_dir = [SDS((s, D_INNER), F32), SDS((nc, N_GROUPS, hpg * hp, D_STATE), F32)]
    state = pltpu.VMEM((N_GROUPS, hpg * hp, D_STATE), F32)
    return pl.pallas_call(
        body, grid=(nc,), in_specs=ins_f + ins_b + [any_spec] * ng, out_specs=outs_f + outs_b + [any_spec] * ng,
        out_shape=one_dir + one_dir + [SDS((N_DEV,) + g.shape, g.dtype) for g in gather],
        scratch_shapes=[state, state] + (_exchange_sems(ng) if ng else []), name=name, compiler_params=_cp("arbitrary"))(
            xbc, xbc, xbc, dt_raw, dt_t, *args_f, xbc, xbc, xbc, dt_raw, dt_t, *args_b, *gather)


def _ssd_bwd(xbc, dt_raw, dt_t, al_r, al_c, bi_r, bi_c, states, dy, off, rev, name, exchange=()):
    s = xbc.shape[0]
    nc = s // CHUNK
    hp = D_INNER // N_HEADS
    gs = N_GROUPS * D_STATE
    hpg = N_HEADS // N_GROUPS
    cm = (lambda c: c) if rev else (lambda c: nc - 1 - c)
    nx = len(exchange)

    def body(*refs):
        x_ref, b_ref, c_ref, dt_ref, dtt_ref, alr, alc, bir, bic, st_ref, dy_ref = refs[:11]
        xch_src = refs[11:11 + nx]
        dx_ref, ddt_ref, dal_ref, dbi_ref = refs[11 + nx:15 + nx]
        xch_dst = refs[15 + nx:15 + 2 * nx]
        dh_scr = refs[15 + 2 * nx]
        xch_sems = refs[16 + 2 * nx:]

        @pl.when(pl.program_id(0) == 0)
        def _():
            dh_scr[...] = jnp.zeros_like(dh_scr)
            dal_ref[...] = jnp.zeros_like(dal_ref)
            dbi_ref[...] = jnp.zeros_like(dbi_ref)
            if nx:
                _exchange_start(xch_src, xch_dst, *xch_sems)

        if nx:
            @pl.when(pl.program_id(0) == nc - 1)
            def _():
                _exchange_wait(xch_src, xch_dst, *xch_sems)

        mask, mask_t, a_r, pre, dt, cs, cs_t, tot = _ssd_common(dt_ref, dtt_ref, alr, alc, bir, bic, off, rev)
        xs = x_ref[...]
        dyv = dy_ref[...]
        rows = lax.broadcasted_iota(jnp.int32, (CHUNK, 1), 0)
        end_row = (rows == 0) if rev else (rows == CHUNK - 1)
        lane_h = lax.broadcasted_iota(jnp.int32, (1, N_HEADS), 1)
        sub_h = lax.broadcasted_iota(jnp.int32, (N_HEADS, 1), 0)
        first = _first_head_lanes()
        dcs_all = jnp.zeros((CHUNK, N_HEADS), F32)
        colw_all = jnp.zeros((N_HEADS, CHUNK), F32)
        dxsum_all = jnp.zeros((CHUNK, N_HEADS), F32)
        dxs, dbs, dcs_out = [], [], []
        for g in range(N_GROUPS):
            bg = b_ref[:, g * D_STATE:(g + 1) * D_STATE]
            cg = c_ref[:, g * D_STATE:(g + 1) * D_STATE]
            gm = lax.dot_general(cg, bg, NT, preferred_element_type=F32)
            gm_t = lax.dot_general(bg, cg, NT, preferred_element_type=F32)
            hcat = st_ref[0, g]
            dhcat = dh_scr[g]
            hb, dhb = hcat.astype(BF16), dhcat.astype(BF16)
            ch = lax.dot_general(cg, hb, NT, preferred_element_type=F32)
            z = lax.dot_general(bg, dhb, NT, preferred_element_type=F32)
            dg_sum = jnp.zeros((CHUNK, CHUNK), F32)
            dchs, xdds, t_hs = [], [], []
            for pr in range(hpg // 2):
                h0 = g * hpg + 2 * pr
                lanes = slice(h0 * hp, (h0 + 2) * hp)
                ps = slice(2 * pr * hp, (2 * pr + 2) * hp)
                pair = lambda a, b: jnp.where(first, a, b)
                cols = [cs[:, h:h + 1] for h in (h0, h0 + 1)]
                tots = [tot[:, h:h + 1] for h in (h0, h0 + 1)]
                xh = xs[:, lanes].astype(F32)
                dtc = pair(dt[:, h0:h0 + 1], dt[:, h0 + 1:h0 + 2])
                xdf = xh * dtc
                xd = xdf.astype(BF16)
                dyh = dyv[:, lanes]
                dyb = dyh.astype(BF16)
                ecs = pair(jnp.exp(cols[0]), jnp.exp(cols[1]))
                dec = pair(jnp.exp(tots[0] - cols[0]), jnp.exp(tots[1] - cols[1]))
                yoff_t = dyh * (ecs * ch[:, ps])
                dchs.append((ecs * dyh).astype(BF16))
                xdd = xdf * dec
                ddec_t = xdd * z[:, ps]
                row_terms = yoff_t - ddec_t
                xdds.append(xdd.astype(BF16))
                dxd_h = []
                for i, (h, keep) in enumerate(((h0, first), (h0 + 1, ~first))):
                    rs = slice((2 * pr + i) * hp, (2 * pr + i + 1) * hp)
                    lm = jnp.exp(jnp.where(mask, cols[i] - cs_t[h:h + 1, :], NEG))
                    m = gm * lm
                    t_h = jnp.exp(tots[i])
                    lm_t = jnp.exp(jnp.where(mask_t, cs_t[h:h + 1, :] - cols[i], NEG))
                    dxd_h.append(jnp.dot((gm_t * lm_t).astype(BF16), dyb, preferred_element_type=F32))
                    dm = lax.dot_general(_only(dyb, keep), xd, NT, preferred_element_type=F32)
                    dg_sum = dg_sum + dm * lm
                    w = dm * m
                    ddec_tot = jnp.sum(jnp.sum(_only(ddec_t, keep), axis=0, keepdims=True), axis=1, keepdims=True)
                    dtot = jnp.sum(jnp.sum(dhcat[rs] * hcat[rs], axis=0, keepdims=True), axis=1, keepdims=True) * t_h + ddec_tot
                    t_hs.append(t_h)
                    dcs = jnp.sum(_only(row_terms, keep) + w, axis=1, keepdims=True)
                    dcs = dcs + jnp.where(end_row, dtot, 0.0)
                    colw_all = colw_all + (sub_h == h).astype(F32) * jnp.sum(w, axis=0, keepdims=True)
                    dcs_all = dcs_all + dcs * (lane_h == h).astype(F32)
                dxd = pair(dxd_h[0], dxd_h[1]) + dec * z[:, ps]
                dxx = dxd * xh
                for i, (h, keep) in enumerate(((h0, first), (h0 + 1, ~first))):
                    dxsum_all = dxsum_all + jnp.sum(_only(dxx, keep), axis=1, keepdims=True) * (lane_h == h).astype(F32)
                dxs.append(dxd * dtc)
            dgs = dg_sum.astype(BF16)
            dchc = jnp.concatenate(dchs, axis=1)
            dcs_out.append(jnp.dot(dgs, bg, preferred_element_type=F32) + jnp.dot(dchc, hb, preferred_element_type=F32))
            dbs.append(lax.dot_general(dgs, cg, TN, preferred_element_type=F32)
                       + jnp.dot(jnp.concatenate(xdds, axis=1), dhb, preferred_element_type=F32))
            dh_in = lax.dot_general(dchc, cg, TN, preferred_element_type=F32)
            for r in range(hpg):
                rs = slice(r * hp, (r + 1) * hp)
                dh_scr[g, rs, :] = dh_in[rs] + t_hs[r] * dhcat[rs]
        dx_ref[...] = jnp.concatenate(dxs + dbs + dcs_out, axis=1)
        mt = mask_t.astype(F32)
        da = (jnp.dot(mt, dcs_all, precision=HIGH, preferred_element_type=F32)
              - lax.dot_general(mt, colw_all, NT, precision=HIGH, preferred_element_type=F32))
        dal_ref[...] += jnp.sum(da * dt, axis=0, keepdims=True) * a_r
        ddt_raw = (da * a_r + dxsum_all) * _sigmoid(pre)
        ddt_ref[...] = ddt_raw
        dbi_ref[...] += jnp.sum(ddt_raw, axis=0, keepdims=True)

    small = lambda shape: pl.BlockSpec(shape, lambda c: (0, 0))
    in_specs = [pl.BlockSpec((CHUNK, D_INNER), lambda c: (cm(c), 0)),
                pl.BlockSpec((CHUNK, gs), lambda c: (cm(c), D_INNER // gs)),
                pl.BlockSpec((CHUNK, gs), lambda c: (cm(c), D_INNER // gs + 1)),
                pl.BlockSpec((CHUNK, 128), lambda c: (cm(c), 0)),
                pl.BlockSpec((2 * N_HEADS, CHUNK), lambda c: (0, cm(c))),
                small((1, N_HEADS)), small((N_HEADS, 1)), small((1, N_HEADS)), small((N_HEADS, 1)),
                pl.BlockSpec((1, N_GROUPS, hpg * hp, D_STATE), lambda c: (cm(c), 0, 0, 0)),
                pl.BlockSpec((CHUNK, D_INNER), lambda c: (cm(c), 0))]
    any_spec = pl.BlockSpec(memory_space=pl.ANY)
    return pl.pallas_call(
        body, grid=(nc,), in_specs=in_specs + [any_spec] * nx,
        out_specs=[pl.BlockSpec((CHUNK, XBC), lambda c: (cm(c), 0)), pl.BlockSpec((CHUNK, N_HEADS), lambda c: (cm(c), 0)),
                   small((1, N_HEADS)), small((1, N_HEADS))] + [any_spec] * nx,
        out_shape=[SDS((s, XBC), F32), SDS((s, N_HEADS), F32), SDS((1, N_HEADS), F32), SDS((1, N_HEADS), F32)]
        + [SDS(a.shape, a.dtype) for a in exchange],
        scratch_shapes=[pltpu.VMEM((N_GROUPS, hpg * hp, D_STATE), F32)] + (_exchange_sems(nx) if nx else []),
        name=name, compiler_params=_cp("arbitrary"))(xbc, xbc, xbc, dt_raw, dt_t, al_r, al_c, bi_r, bi_c, states, dy, *exchange)


def _gate_fwd(yf, yb, xbc, proj, dskip_x, norm_w, mix, mix_t, name):
    s = yf.shape[0]
    tm = _pick(s, (256, 128))
    gw = D_INNER // N_GROUPS
    zc = 3 * D_MODEL // D_INNER

    def body(yf_ref, yb_ref, x_ref, z_ref, d_ref, w_ref, _m, _mt, o_ref, ot_ref):
        y = yf_ref[...] + yb_ref[...] + d_ref[...] * x_ref[...].astype(F32)
        z = z_ref[...].astype(F32)
        gt = y * (z * _sigmoid(z))
        outs = []
        for g in range(N_GROUPS):
            gg = gt[:, g * gw:(g + 1) * gw]
            outs.append(gg * lax.rsqrt(jnp.mean(gg * gg, axis=-1, keepdims=True) + EPS))
        out = (jnp.concatenate(outs, axis=1) * w_ref[...]).astype(o_ref.dtype)
        o_ref[...] = out
        ot_ref[...] = out.T

    row = pl.BlockSpec((tm, D_INNER), lambda i: (i, 0))
    vec = pl.BlockSpec((1, D_INNER), lambda i: (0, 0))
    any_spec = pl.BlockSpec(memory_space=pl.ANY)
    cb = (mix.shape[1] - D_INNER) // D_INNER
    return pl.pallas_call(
        body, grid=(s // tm,), in_specs=[row, row, row, pl.BlockSpec((tm, D_INNER), lambda i: (i, zc)), vec, vec, any_spec, any_spec],
        out_specs=[pl.BlockSpec((tm, D_INNER), lambda i: (i, cb)), pl.BlockSpec((D_INNER, tm), lambda i: (cb, i))],
        out_shape=[SDS(mix.shape, BF16), SDS(mix_t.shape, BF16)], input_output_aliases={6: 0, 7: 1},
        name=name, compiler_params=_cp("parallel"))(yf, yb, xbc, proj, dskip_x, norm_w, mix, mix_t)


def _gate_bwd(yf, yb, xbc, proj, dskip_x, norm_w, dmix, dproj, name):
    s = yf.shape[0]
    tm = _pick(s, (256, 128))
    gw = D_INNER // N_GROUPS
    zc = 3 * D_MODEL // D_INNER

    def body(yf_ref, yb_ref, x_ref, z_ref, d_ref, w_ref, do_ref, _, dy_ref, dz_ref, dxs_ref, dw_ref, dd_ref):
        xf = x_ref[...].astype(F32)
        y = yf_ref[...] + yb_ref[...] + d_ref[...] * xf
        z = z_ref[...].astype(F32)
        sg = _sigmoid(z)
        sz = z * sg
        gt = y * sz
        do = do_ref[...].astype(F32)
        dgh = do * w_ref[...]
        ghs, dgts = [], []
        for g in range(N_GROUPS):
            gg = gt[:, g * gw:(g + 1) * gw]
            r = lax.rsqrt(jnp.mean(gg * gg, axis=-1, keepdims=True) + EPS)
            gh = gg * r
            dg = dgh[:, g * gw:(g + 1) * gw]
            ghs.append(gh)
            dgts.append(r * (dg - gh * jnp.mean(dg * gh, axis=-1, keepdims=True)))
        ghat = jnp.concatenate(ghs, axis=1)
        dgt = jnp.concatenate(dgts, axis=1)
        dy = dgt * sz
        dy_ref[...] = dy
        dz_ref[...] = (dgt * y * (sg * (1.0 + z * (1.0 - sg)))).astype(dz_ref.dtype)
        dxs_ref[...] = dy * d_ref[...]

        @pl.when(pl.program_id(0) == 0)
        def _():
            dw_ref[...] = jnp.zeros_like(dw_ref)
            dd_ref[...] = jnp.zeros_like(dd_ref)

        dw_ref[...] += jnp.sum(do * ghat, axis=0, keepdims=True)
        dd_ref[...] += jnp.sum(dy * xf, axis=0, keepdims=True)

    row = pl.BlockSpec((tm, D_INNER), lambda i: (i, 0))
    vec = pl.BlockSpec((1, D_INNER), lambda i: (0, 0))
    dz_shape, _, more, more_specs, alias = _slab(s, D_INNER, BF16, (dproj.shape[1], zc * D_INNER, dproj), 7, out_idx=1)
    return pl.pallas_call(
        body, grid=(s // tm,),
        in_specs=[row, row, row, pl.BlockSpec((tm, D_INNER), lambda i: (i, zc)), vec, vec, pl.BlockSpec((tm, D_INNER), lambda i: (i, 1))] + more_specs,
        out_specs=[row, pl.BlockSpec((tm, D_INNER), lambda i: (i, zc)), row, vec, vec],
        out_shape=[SDS((s, D_INNER), F32), dz_shape, SDS((s, D_INNER), F32), SDS((1, D_INNER), F32), SDS((1, D_INNER), F32)],
        input_output_aliases=alias, name=name, compiler_params=_cp("arbitrary"))(yf, yb, xbc, proj, dskip_x, norm_w, dmix, *more)


def _adamw(parts, w, m, v, name):
    r, c = w.shape
    tr = _pick(r, (256, 352, 128))

    def body(p_ref, w_ref, m_ref, v_ref, g_ref, d_ref, nm_ref, nv_ref):
        g = p_ref[0].astype(F32)
        for i in range(1, N_DEV):
            g = g + p_ref[i].astype(F32)
        mn = B1 * m_ref[...] + (1.0 - B1) * g
        vn = B2 * v_ref[...] + (1.0 - B2) * (g * g)
        m_hat = mn / (1.0 - B1 ** STEP)
        v_hat = vn / (1.0 - B2 ** STEP)
        g_ref[...] = g
        d_ref[...] = -LR * (m_hat / (jnp.sqrt(v_hat) + AEPS) + WD * w_ref[...])
        nm_ref[...] = mn
        nv_ref[...] = vn

    blk = pl.BlockSpec((tr, c), lambda i: (i, 0))
    return pl.pallas_call(
        body, grid=(r // tr,), in_specs=[pl.BlockSpec((N_DEV, tr, c), lambda i: (0, i, 0)), blk, blk, blk],
        out_specs=[blk, blk, blk, blk], out_shape=[SDS((r, c), F32)] * 4, name=name, compiler_params=_cp("parallel"))(parts, w, m, v)


def _sum_parts(parts, name):
    _, r, c = parts.shape

    def body(p_ref, o_ref):
        g = p_ref[0]
        for i in range(1, N_DEV):
            g = g + p_ref[i]
        o_ref[...] = g

    return pl.pallas_call(body, out_shape=SDS((r, c), F32), name=name)(parts)


def _adamw_small(gs, ws, ms, vs, name):
    n = len(gs)

    def body(*refs):
        g_refs, w_refs, m_refs, v_refs = refs[:n], refs[n:2 * n], refs[2 * n:3 * n], refs[3 * n:4 * n]
        d_refs, nm_refs, nv_refs = refs[4 * n:5 * n], refs[5 * n:6 * n], refs[6 * n:7 * n]
        for i in range(n):
            gv = g_refs[i][...]
            mn = B1 * m_refs[i][...] + (1.0 - B1) * gv
            vn = B2 * v_refs[i][...] + (1.0 - B2) * (gv * gv)
            m_hat = mn / (1.0 - B1 ** STEP)
            v_hat = vn / (1.0 - B2 ** STEP)
            d_refs[i][...] = -LR * (m_hat / (jnp.sqrt(v_hat) + AEPS) + WD * w_refs[i][...])
            nm_refs[i][...] = mn
            nv_refs[i][...] = vn

    outs = pl.pallas_call(body, out_shape=[SDS(g.shape, F32) for g in gs] * 3, name=name)(*gs, *ws, *ms, *vs)
    return outs[:n], outs[n:2 * n], outs[2 * n:]


def _my_index():
    return 4 * lax.axis_index("x") + 2 * lax.axis_index("y") + lax.axis_index("c")


def _all_gather(shards, name):
    n = len(shards)

    def body(*refs):
        srcs, outs = refs[:n], refs[n:2 * n]
        _gather_start(srcs, outs, *refs[2 * n:])
        _gather_forward(srcs, outs, *refs[2 * n:])
        _gather_finish(srcs, outs, *refs[2 * n:])

    any_spec = pl.BlockSpec(memory_space=pl.ANY)
    return pl.pallas_call(
        body, in_specs=[any_spec] * n, out_specs=[any_spec] * n,
        out_shape=[SDS((N_DEV,) + s.shape, s.dtype) for s in shards], scratch_shapes=_exchange_sems(n), name=name)(*shards)


def _exchange(arrays, name):
    n = len(arrays)

    def body(*refs):
        srcs, outs = refs[:n], refs[n:2 * n]
        _exchange_start(srcs, outs, *refs[2 * n:])
        _exchange_wait(srcs, outs, *refs[2 * n:])

    any_spec = pl.BlockSpec(memory_space=pl.ANY)
    return pl.pallas_call(
        body, in_specs=[any_spec] * n, out_specs=[any_spec] * n,
        out_shape=[SDS(a.shape, a.dtype) for a in arrays], scratch_shapes=_exchange_sems(n), name=name)(*arrays)


def _to_pattern(t, d):
    if d == 1:
        return t
    s, w = t.shape
    return t.reshape(s // d, d, w).transpose(1, 0, 2).reshape(s, w)


def _from_pattern(t, d):
    if d == 1:
        return t
    s, w = t.shape
    return t.reshape(d, s // d, w).transpose(1, 0, 2).reshape(s, w)


def _pad_lanes(t, n):
    return jnp.pad(t, ((0, 0), (0, n - t.shape[1])))


def _to_shards(g, axis):
    r, c = g.shape
    if axis == 0:
        return g.reshape(N_DEV, r // N_DEV, c)
    return g.reshape(r, N_DEV, c // N_DEV).transpose(1, 0, 2)


def _local_step(x, target, p, late_shards=(), early_exchange=True):
    s = x.shape[0]
    tabs_f, tabs_b = _rope_tables(s)
    expand = jnp.asarray(np.repeat(np.eye(N_HEADS, dtype=np.float32), HEAD_DIM, axis=1))
    w_main, w_dt = p["w_in"][:, :MAIN_W], _pad_lanes(p["w_in"][:, MAIN_W:], 128)
    al_r = {"f": p["a_log_f"], "b": p["a_log_b"]}
    bi_r = {"f": p["dt_bias_f"], "b": p["dt_bias_b"]}
    dskip_x = jnp.repeat(p["d_skip"], D_INNER // N_HEADS, axis=1)
    ssm_w3, ffn_w3 = p["ssm_conv_w"].T, p["ffn_conv_w"].T

    h1, h1t = _rmsnorm_fwd(x, p["norm1_w"], "norm1_fwd")
    proj = _matmul(h1, w_main, name="in_proj")
    dt_raw = _matmul(h1, w_dt, name="in_proj_dt", out_dtype=F32)
    dt_t = dt_raw[:, :2 * N_HEADS].T
    perms = {d: _perm_matrices(d) for d in DILATIONS[1:]}
    qkv = _rope_fwd(proj, tabs_f, perms, "rope_fwd")
    v_col = 2
    os_, lses = [], []
    for d, qkv_p in zip(DILATIONS, qkv):
        o_p, lse_p = _attn_fwd(qkv_p, qkv_p, v_col, s // d, f"attn_fwd_d{d}")
        os_.append(o_p)
        lses.append(_from_pattern(lse_p, d))
    mix, mix_t, lse_tot = _attn_combine(os_, lses, perms, expand, "attn_combine")

    xbc = _conv(proj, 3 * D_MODEL + D_INNER, XBC, ssm_w3, p["ssm_conv_b"], True, "ssm_conv_fwd")
    col = lambda r: r.reshape(N_HEADS, 1)
    ssd_args = {k: (al_r[k], col(al_r[k]), bi_r[k], col(bi_r[k])) for k in ("f", "b")}
    yf, st_f, yb, st_b, *got = _ssd_fwd(xbc, dt_raw, dt_t, ssd_args["f"], ssd_args["b"], "ssd_fwd", gather=late_shards)
    if late_shards:
        p = dict(p, w_out=got[0].reshape(2 * D_MODEL, D_MODEL), w_down=got[2].reshape(D_FF, D_MODEL),
                 w_up=got[1].transpose(1, 0, 2).reshape(D_MODEL, 2 * D_FF))
    mix, mix_t = _gate_fwd(yf, yb, xbc, proj, dskip_x, p["ssm_norm_w"], mix, mix_t, "ssm_gate_fwd")

    x2 =_matmul(mix, p["w_out"], name="out_proj", out_dtype=F32, residual=x)
    h2, h2t = _rmsnorm_fwd(x2, p["norm2_w"], "norm2_fwd")
    u = _matmul(h2, p["w_up"], name="ffn_up")
    act, act_t = _ffn_gate_fwd(u, ffn_w3, p["ffn_conv_b"], "ffn_gate_fwd")
    x3 = _matmul(act, p["w_down"], name="ffn_down", out_dtype=F32, residual=x2)

    dx3, dx3b, g_final, loss = _final_norm_loss(x3, p["final_norm_w"].reshape(1, D_MODEL), target, "final_norm_loss")
    g_w_down = _matmul(act_t, dx3b, name="dw_down")
    dact = _matmul(dx3b, p["w_down"], name="d_act", trans_b=True)
    dug, duu, dwg, dwu, dbg, dbu = _ffn_gate_bwd(u, ffn_w3, p["ffn_conv_b"], dact, "ffn_gate_bwd")
    du = _conv(dug, 0, D_FF, ffn_w3, p["ffn_conv_b"], False, "ffn_conv_bwd_gate", slab=(2 * D_FF, 0, None), transpose=True)
    du = _conv(duu, 0, D_FF, ffn_w3, p["ffn_conv_b"], False, "ffn_conv_bwd_up", slab=(2 * D_FF, D_FF, du), transpose=True, wcol0=D_FF)
    g_w_up = _matmul(h2t, du, name="dw_up")
    none_a, none_w = jnp.zeros((s, 128), BF16), jnp.zeros((D_MODEL, 128), BF16)
    dx2, dx2b, g_norm2 = _proj_norm_bwd(du, p["w_up"], none_a, none_w, x2, p["norm2_w"], dx3, "ffn_up_norm2_bwd")
    g_w_out = _matmul(mix_t, dx2b, name="dw_out")
    dmix = _matmul(dx2b, p["w_out"], name="d_mix", trans_b=True)

    delta, do_pat = _attn_delta(dmix, mix, expand.T, perms, "attn_delta")
    dqs, dks, dvs = [], [], []
    for d, qkv_p, do_p in zip(DILATIONS, qkv, [dmix] + do_pat):
        lse_p, dl_p = _to_pattern(lse_tot, d), _to_pattern(delta, d)
        dqs.append(_attn_bwd_dq(qkv_p, qkv_p, v_col, do_p, lse_p, dl_p, s // d, f"attn_bwd_dq_d{d}"))
        dk, dv = _attn_bwd_dkv(qkv_p, qkv_p, v_col, do_p, lse_p.T, dl_p.T, s // d, f"attn_bwd_dkv_d{d}")
        dks.append(dk)
        dvs.append(dv)
    dproj = _sum3_rope(dqs, perms, tabs_b, "rope_bwd_q", slab=(MAIN_W, 0, None))
    dproj = _sum3_rope(dks, perms, tabs_b, "rope_bwd_k", slab=(MAIN_W, D_MODEL, dproj))
    dproj = _sum3_rope(dvs, perms, None, "sum_dv", slab=(MAIN_W, 2 * D_MODEL, dproj))

    dy, dproj, dxs_skip, g_ssm_norm, g_dskip_lanes = _gate_bwd(yf, yb, xbc, proj, dskip_x, p["ssm_norm_w"], dmix, dproj, "ssm_gate_bwd")
    early_f = [_to_shards(g_w_up, 1), _to_shards(g_w_down, 0)] if early_exchange else []
    early_b = [_to_shards(g_w_out, 0)] if early_exchange else []
    dxbc_f, ddt_f, g_al_f, g_bi_f, *got_f = _ssd_bwd(xbc, dt_raw, dt_t, *ssd_args["f"], st_f, dy, 0, False, "ssd_bwd_f", exchange=early_f)
    dxbc_b, ddt_b, g_al_b, g_bi_b, *got_b = _ssd_bwd(xbc, dt_raw, dt_t, *ssd_args["b"], st_b, dy, N_HEADS, True, "ssd_bwd_b",
                                                     exchange=early_b)
    if early_exchange:
        (g_w_up, g_w_down), (g_w_out,) = got_f, got_b
    dpre, g_ssm_w3, g_ssm_cb = _conv_silu_bwd(proj, 3 * D_MODEL + D_INNER, XBC, ssm_w3, p["ssm_conv_b"],
                                              [dxbc_f, dxbc_b, dxs_skip], [XBC, XBC, D_INNER], "ssm_conv_bwd")
    dproj = _conv(dpre, 0, XBC, ssm_w3, p["ssm_conv_b"], False, "ssm_conv_bwd_x", transpose=True,
                  slab=(MAIN_W, 3 * D_MODEL + D_INNER, dproj))

    ddt =_pad_lanes(jnp.concatenate([ddt_f, ddt_b], axis=1), 128).astype(BF16)
    g_w_main = _matmul(h1t, dproj, name="dw_in")
    g_w_dt = _matmul(h1t, ddt, name="dw_in_dt")
    g_w_in = jnp.concatenate([g_w_main, g_w_dt[:, :2 * N_HEADS]], axis=1)
    late = [_to_shards(g_w_in, 1)] if early_exchange else []
    grad_x, _, g_norm1, *got = _proj_norm_bwd(dproj, w_main, ddt, w_dt, x, p["norm1_w"], dx2, "in_proj_norm1_bwd", exchange=late)
    if early_exchange:
        g_w_in = got[0]

    g_dskip = jnp.sum(g_dskip_lanes.reshape(N_HEADS, D_INNER // N_HEADS), axis=1).reshape(1, N_HEADS)
    small = {
        "norm1_w": g_norm1, "ssm_conv_w": g_ssm_w3.T, "ssm_conv_b": g_ssm_cb, "a_log_f": g_al_f, "a_log_b": g_al_b,
        "dt_bias_f": g_bi_f, "dt_bias_b": g_bi_b, "d_skip": g_dskip, "ssm_norm_w": g_ssm_norm, "norm2_w": g_norm2,
        "ffn_conv_w": jnp.concatenate([dwg, dwu], axis=1).T, "ffn_conv_b": jnp.concatenate([dbg, dbu], axis=1), "final_norm_w": g_final,
    }
    big = {"w_in": g_w_in, "w_out": g_w_out, "w_up": g_w_up, "w_down": g_w_down}
    return loss[0, 0], grad_x, big, small


SMALL_ORDER = ("norm1_w", "ssm_conv_w", "ssm_conv_b", "a_log_f", "a_log_b", "dt_bias_f", "dt_bias_b", "d_skip",
               "ssm_norm_w", "norm2_w", "ffn_conv_w", "ffn_conv_b", "final_norm_w")
SHARDED_SMALL = ("ssm_conv_w", "ffn_conv_w")
BIG_ORDER = ("w_in", "w_out", "w_up", "w_down")


def _pack(vals):
    rows = []
    for v in vals:
        f = v.reshape(-1).astype(F32)
        n = -(-f.shape[0] // 128) * 128
        rows.append(jnp.pad(f, (0, n - f.shape[0])).reshape(-1, 128))
    out = jnp.concatenate(rows, axis=0)
    pad = -out.shape[0] % 8
    return jnp.pad(out, ((0, pad), (0, 0)))


def _unpack(packed, shapes):
    out, r = [], 0
    for shp in shapes:
        n = math.prod(shp)
        nr = -(-n // 128)
        out.append(packed[r:r + nr].reshape(-1)[:n].reshape(shp))
        r += nr
    return out


def kernel(x, norm1_w, w_in, ssm_conv_w, ssm_conv_b, a_log_f, a_log_b, dt_bias_f, dt_bias_b, d_skip, ssm_norm_w, w_out, norm2_w, w_up, ffn_conv_w, ffn_conv_b, w_down, final_norm_w, loss_target, m_norm1_w, m_w_in, m_ssm_conv_w, m_ssm_conv_b, m_a_log_f, m_a_log_b, m_dt_bias_f, m_dt_bias_b, m_d_skip, m_ssm_norm_w, m_w_out, m_norm2_w, m_w_up, m_ffn_conv_w, m_ffn_conv_b, m_w_down, m_final_norm_w, v_norm1_w, v_w_in, v_ssm_conv_w, v_ssm_conv_b, v_a_log_f, v_a_log_b, v_dt_bias_f, v_dt_bias_b, v_d_skip, v_ssm_norm_w, v_w_out, v_norm2_w, v_w_up, v_ffn_conv_w, v_ffn_conv_b, v_w_down, v_final_norm_w):
    w = dict(norm1_w=norm1_w, w_in=w_in, ssm_conv_w=ssm_conv_w, ssm_conv_b=ssm_conv_b, a_log_f=a_log_f, a_log_b=a_log_b,
             dt_bias_f=dt_bias_f, dt_bias_b=dt_bias_b, d_skip=d_skip, ssm_norm_w=ssm_norm_w, w_out=w_out, norm2_w=norm2_w,
             w_up=w_up, ffn_conv_w=ffn_conv_w, ffn_conv_b=ffn_conv_b, w_down=w_down, final_norm_w=final_norm_w)
    mo = dict(norm1_w=m_norm1_w, w_in=m_w_in, ssm_conv_w=m_ssm_conv_w, ssm_conv_b=m_ssm_conv_b, a_log_f=m_a_log_f, a_log_b=m_a_log_b,
              dt_bias_f=m_dt_bias_f, dt_bias_b=m_dt_bias_b, d_skip=m_d_skip, ssm_norm_w=m_ssm_norm_w, w_out=m_w_out, norm2_w=m_norm2_w,
              w_up=m_w_up, ffn_conv_w=m_ffn_conv_w, ffn_conv_b=m_ffn_conv_b, w_down=m_w_down, final_norm_w=m_final_norm_w)
    vo = dict(norm1_w=v_norm1_w, w_in=v_w_in, ssm_conv_w=v_ssm_conv_w, ssm_conv_b=v_ssm_conv_b, a_log_f=v_a_log_f, a_log_b=v_a_log_b,
              dt_bias_f=v_dt_bias_f, dt_bias_b=v_dt_bias_b, d_skip=v_d_skip, ssm_norm_w=v_ssm_norm_w, w_out=v_w_out, norm2_w=v_norm2_w,
              w_up=v_w_up, ffn_conv_w=v_ffn_conv_w, ffn_conv_b=v_ffn_conv_b, w_down=v_w_down, final_norm_w=v_final_norm_w)
    me = _my_index()

    g_in, g_conv = _all_gather([w["w_in"][0].astype(BF16), _pack([w["ssm_conv_w"][0], w["ffn_conv_w"][0]])], "w_in_all_gather")
    conv_rows = [_unpack(g_conv[i], [ssm_conv_w.shape[1:], ffn_conv_w.shape[1:]]) for i in range(N_DEV)]
    full = {
        "w_in": g_in.transpose(1, 0, 2).reshape(D_MODEL, N_DEV * w_in.shape[2]),
        "ssm_conv_w": jnp.concatenate([c[0] for c in conv_rows], axis=0),
        "ffn_conv_w": jnp.concatenate([c[1] for c in conv_rows], axis=0),
    }
    for k in ("norm1_w", "ssm_conv_b", "a_log_f", "a_log_b", "dt_bias_f", "dt_bias_b", "d_skip", "ssm_norm_w", "norm2_w", "ffn_conv_b",
              "final_norm_w"):
        full[k] = w[k]
    late = [w["w_out"][0].astype(BF16), w["w_up"][0].astype(BF16), w["w_down"][0].astype(BF16)]

    loss_part, grad_x, big, small = _local_step(x[0], loss_target[0], full, late)

    small_shapes = [(1,)] + [small[k].shape for k in SMALL_ORDER]
    packed = _pack([loss_part] + [small[k] for k in SMALL_ORDER])
    out_small = jnp.broadcast_to(packed[None], (N_DEV,) + packed.shape)
    (r_small,) = _exchange([out_small], "small_grads_exchange")
    r_in, r_out, r_up, r_down = big["w_in"], big["w_out"], big["w_up"], big["w_down"]

    outs_g, outs_d, outs_m, outs_v = {}, {}, {}, {}
    for k, parts in zip(BIG_ORDER, (r_in, r_out, r_up, r_down)):
        g, dlt, nm, nv = _adamw(parts, w[k][0], mo[k][0], vo[k][0], f"adamw_{k}")
        outs_g[k], outs_d[k], outs_m[k], outs_v[k] = g[None], dlt[None], nm[None], nv[None]
    tot = _unpack(_sum_parts(r_small, "small_grads_sum"), small_shapes)
    loss = tot[0][0]
    gs = dict(zip(SMALL_ORDER, tot[1:]))
    g_own = {}
    for k in SMALL_ORDER:
        if k in SHARDED_SMALL:
            rows = w[k].shape[1]
            g_own[k] = lax.dynamic_slice_in_dim(gs[k], me * rows, rows, axis=0)[None]
        else:
            g_own[k] = gs[k].reshape(w[k].shape)
    two_d = lambda a: a.reshape(-1, a.shape[-1])
    d_s, m_s, v_s = _adamw_small([two_d(g_own[k]) for k in SMALL_ORDER], [two_d(w[k]) for k in SMALL_ORDER],
                                 [two_d(mo[k]) for k in SMALL_ORDER], [two_d(vo[k]) for k in SMALL_ORDER], "adamw_small")
    for k, a, b, c in zip(SMALL_ORDER, d_s, m_s, v_s):
        shp = w[k].shape
        outs_g[k], outs_d[k], outs_m[k], outs_v[k] = g_own[k], a.reshape(shp), b.reshape(shp), c.reshape(shp)

    order = ("norm1_w", "w_in", "ssm_conv_w", "ssm_conv_b", "a_log_f", "a_log_b", "dt_bias_f", "dt_bias_b", "d_skip", "ssm_norm_w",
             "w_out", "norm2_w", "w_up", "ffn_conv_w", "ffn_conv_b", "w_down", "final_norm_w")
    return (loss, grad_x[None], *[outs_g[k] for k in order], *[outs_d[k] for k in order],
            *[outs_m[k] for k in order], *[outs_v[k] for k in order])
```

```python
import math

import numpy as np
import jax
import jax.numpy as jnp
from jax import lax
from jax.experimental import pallas as pl
from jax.experimental.pallas import tpu as pltpu

F32 = jnp.float32
BF16 = jnp.bfloat16
SDS = jax.ShapeDtypeStruct

N_DEV = 8
D_MODEL = 1024
N_HEADS = 16
HEAD_DIM = 64
ROPE_DIM = 16
ROPE_THETA = 500000.0
DILATIONS = (1, 4, 16)
BAND_HALF = 64
D_INNER = 1024
N_GROUPS = 4
D_STATE = 128
CHUNK = 128
XBC = D_INNER + 2 * N_GROUPS * D_STATE
D_FF = 2816
MAIN_W = 3 * D_MODEL + D_INNER + XBC
EPS = 1e-6
LR, B1, B2, AEPS, WD, STEP = 0.001, 0.9, 0.999, 1e-08, 0.01, 10
NEG = -1e30
VMEM_LIMIT = 56 * 1024 * 1024
MESH = pl.DeviceIdType.MESH
HIGH = lax.Precision.HIGHEST
NT = (((1,), (1,)), ((), ()))
TN = (((0,), (0,)), ((), ()))


def _cp(*sem):
    return pltpu.CompilerParams(dimension_semantics=sem, vmem_limit_bytes=VMEM_LIMIT)


def _pick(n, cands):
    for c in cands:
        if n % c == 0:
            return c
    raise ValueError(f"no tile for {n}")


def _sigmoid(x):
    return 1.0 / (1.0 + jnp.exp(-x))


def _dot_01(x, m01):
    mb = m01.astype(BF16)
    out, r = None, x
    for _ in range(3):
        p = r.astype(BF16)
        r = r - p.astype(F32)
        t = jnp.dot(p, mb, preferred_element_type=F32)
        out = t if out is None else out + t
    return out


def _softplus(x):
    return jnp.maximum(x, 0.0) + jnp.log1p(jnp.exp(-jnp.abs(x)))


def _slab(s, width, dtype, slab, n_in, out_idx=0):
    if slab is None:
        return SDS((s, width), dtype), 0, [], [], {}
    total, col0, into = slab
    if into is None:
        return SDS((s, total), dtype), col0, [], [], {}
    return SDS((s, total), dtype), col0, [into], [pl.BlockSpec(memory_space=pl.ANY)], {n_in: out_idx}


def _matmul(a, b, *, name, trans_b=False, out_dtype=BF16, residual=None):
    m, k = a.shape
    n = b.shape[0] if trans_b else b.shape[1]
    tk = k if k <= 2048 else _pick(k, (2048, 1408, 1024, 512))
    nk = k // tk
    if nk == 1:
        tm = _pick(m, (2048, 1408, 1024, 512, 256, 128))
        tn = _pick(n, (512, 256, 128))
    else:
        tm = _pick(m, (1024, 1408, 512, 256, 128))
        tn = _pick(n, (1024, 1408, 512, 256, 128))
    dn = NT if trans_b else (((1,), (0,)), ((), ()))

    def body(*refs):
        a_ref, b_ref = refs[0], refs[1]
        o_ref, acc = refs[-2], refs[-1]
        kk = pl.program_id(2)

        @pl.when(kk == 0)
        def _():
            acc[...] = jnp.zeros_like(acc)

        acc[...] += lax.dot_general(a_ref[...], b_ref[...], dn, preferred_element_type=F32)

        @pl.when(kk == nk - 1)
        def _():
            r = acc[...]
            if residual is not None:
                r = r + refs[2][...].astype(F32)
            o_ref[...] = r.astype(o_ref.dtype)

    in_specs = [pl.BlockSpec((tm, tk), lambda i, j, kk: (i, kk)),
                pl.BlockSpec((tn, tk), lambda i, j, kk: (j, kk)) if trans_b else pl.BlockSpec((tk, tn), lambda i, j, kk: (kk, j))]
    args = [a, b]
    if residual is not None:
        in_specs.append(pl.BlockSpec((tm, tn), lambda i, j, kk: (i, j)))
        args.append(residual)
    return pl.pallas_call(
        body, grid=(m // tm, n // tn, nk), in_specs=in_specs,
        out_specs=pl.BlockSpec((tm, tn), lambda i, j, kk: (i, j)),
        out_shape=SDS((m, n), out_dtype), scratch_shapes=[pltpu.VMEM((tm, tn), F32)],
        name=name, compiler_params=_cp("parallel", "parallel", "arbitrary"))(*args)


def _rmsnorm_fwd(x, w, name):
    s, d = x.shape
    tm = _pick(s, (512, 128))

    def body(x_ref, w_ref, o_ref, ot_ref):
        xf = x_ref[...]
        r = lax.rsqrt(jnp.mean(xf * xf, axis=-1, keepdims=True) + EPS)
        out = (xf * r * w_ref[...]).astype(o_ref.dtype)
        o_ref[...] = out
        ot_ref[...] = out.T

    return pl.pallas_call(
        body, grid=(s // tm,), in_specs=[pl.BlockSpec((tm, d), lambda i: (i, 0)), pl.BlockSpec((1, d), lambda i: (0, 0))],
        out_specs=[pl.BlockSpec((tm, d), lambda i: (i, 0)), pl.BlockSpec((d, tm), lambda i: (0, i))],
        out_shape=[SDS((s, d), BF16), SDS((d, s), BF16)], name=name, compiler_params=_cp("parallel"))(x, w)


def _proj_norm_bwd(da, wt, da2, wt2, x, w, dres, name, exchange=()):
    s, k = da.shape
    d = x.shape[1]
    tm = _pick(s, (512, 128))
    tk = _pick(k, (2048, 1408, 1024, 512))
    nk = k // tk
    nx = len(exchange)
    grid = (s // tm, nk)

    def body(*refs):
        a_ref, b_ref, a2_ref, b2_ref, x_ref, w_ref, dres_ref = refs[:7]
        xch_src = refs[7:7 + nx]
        dx_ref, dxb_ref, dw_ref = refs[7 + nx:10 + nx]
        xch_dst = refs[10 + nx:10 + 2 * nx]
        acc = refs[10 + 2 * nx]
        xch_sems = refs[11 + 2 * nx:]
        i, kk = pl.program_id(0), pl.program_id(1)

        @pl.when((i == 0) & (kk == 0))
        def _():
            dw_ref[...] = jnp.zeros_like(dw_ref)
            if nx:
                _exchange_start(xch_src, xch_dst, *xch_sems)

        @pl.when(kk == 0)
        def _():
            acc[...] = lax.dot_general(a2_ref[...], b2_ref[...], NT, preferred_element_type=F32)

        acc[...] += lax.dot_general(a_ref[...], b_ref[...], NT, preferred_element_type=F32)

        @pl.when(kk == nk - 1)
        def _():
            dh = acc[...]
            xf = x_ref[...]
            r = lax.rsqrt(jnp.mean(xf * xf, axis=-1, keepdims=True) + EPS)
            xhat = xf * r
            g = dh * w_ref[...]
            dx = dres_ref[...] + r * (g - xhat * jnp.mean(g * xhat, axis=-1, keepdims=True))
            dx_ref[...] = dx
            dxb_ref[...] = dx.astype(dxb_ref.dtype)
            dw_ref[...] += jnp.sum(dh * xhat, axis=0, keepdims=True)

        if nx:
            @pl.when((i == grid[0] - 1) & (kk == nk - 1))
            def _():
                _exchange_wait(xch_src, xch_dst, *xch_sems)

    row = pl.BlockSpec((tm, d), lambda i, kk: (i, 0))
    vec = pl.BlockSpec((1, d), lambda i, kk: (0, 0))
    any_spec = pl.BlockSpec(memory_space=pl.ANY)
    in_specs = [pl.BlockSpec((tm, tk), lambda i, kk: (i, kk)), pl.BlockSpec((d, tk), lambda i, kk: (0, kk)),
                pl.BlockSpec((tm, da2.shape[1]), lambda i, kk: (i, 0)), pl.BlockSpec((d, wt2.shape[1]), lambda i, kk: (0, 0)),
                row, vec, row]
    return pl.pallas_call(
        body, grid=grid, in_specs=in_specs + [any_spec] * nx, out_specs=[row, row, vec] + [any_spec] * nx,
        out_shape=[SDS((s, d), F32), SDS((s, d), BF16), SDS((1, d), F32)] + [SDS(e.shape, e.dtype) for e in exchange],
        scratch_shapes=[pltpu.VMEM((tm, d), F32)] + (_exchange_sems(nx) if nx else []),
        name=name, compiler_params=_cp("arbitrary", "arbitrary"))(da, wt, da2, wt2, x, w, dres, *exchange)


def _final_norm_loss(x, w, target, name):
    s, d = x.shape
    tm = _pick(s, (512, 128))

    def body(x_ref, w_ref, t_ref, dx_ref, dxb_ref, dw_ref, loss_ref):
        xf = x_ref[...]
        r = lax.rsqrt(jnp.mean(xf * xf, axis=-1, keepdims=True) + EPS)
        xhat = xf * r
        wv = w_ref[...]
        e = xhat * wv - t_ref[...]
        dy = e * (1.0 / d)
        g = dy * wv
        dx = r * (g - xhat * jnp.mean(g * xhat, axis=-1, keepdims=True))
        dx_ref[...] = dx
        dxb_ref[...] = dx.astype(dxb_ref.dtype)

        @pl.when(pl.program_id(0) == 0)
        def _():
            dw_ref[...] = jnp.zeros_like(dw_ref)
            loss_ref[...] = jnp.zeros_like(loss_ref)

        dw_ref[...] += jnp.sum(dy * xhat, axis=0, keepdims=True)
        loss_ref[...] += jnp.sum(jnp.sum(e * e, axis=1, keepdims=True), axis=0, keepdims=True) * (0.5 / d)

    row = pl.BlockSpec((tm, d), lambda i: (i, 0))
    vec = pl.BlockSpec((1, d), lambda i: (0, 0))
    return pl.pallas_call(
        body, grid=(s // tm,), in_specs=[row, vec, row], out_specs=[row, row, vec, pl.BlockSpec((1, 128), lambda i: (0, 0))],
        out_shape=[SDS((s, d), F32), SDS((s, d), BF16), SDS((1, d), F32), SDS((1, 128), F32)],
        name=name, compiler_params=_cp("arbitrary"))(x, w, target)


def _rope_tables(s):
    half = ROPE_DIM // 2
    f32 = np.float32
    inv_freq = np.power(f32(ROPE_THETA), -np.arange(half, dtype=f32) * f32(2.0) / f32(ROPE_DIM)).astype(f32)
    ang = (np.arange(s, dtype=f32)[:, None] * inv_freq[None, :]).astype(f32)
    cos, sin = np.cos(ang).astype(f32), np.sin(ang).astype(f32)
    z = np.zeros((s, HEAD_DIM - ROPE_DIM), f32)
    zh = np.zeros((s, half), f32)
    c = np.concatenate([cos, cos, z + 1.0], axis=1)
    sa = np.concatenate([zh, sin, z], axis=1)
    sb = np.concatenate([-sin, zh, z], axis=1)
    two = lambda t: np.concatenate([t, t], axis=1)
    c, sa, sb = two(c), two(sa), two(sb)
    fwd, bwd = (c, sa, sb), (c, np.roll(sb, half, axis=1), np.roll(sa, -half, axis=1))
    return tuple(jnp.asarray(t) for t in fwd), tuple(jnp.asarray(t) for t in bwd)


PERM_TILE = 256


def _perm_matrices(d):
    n = PERM_TILE // d
    o = np.arange(PERM_TILE)
    p = np.zeros((PERM_TILE, PERM_TILE), np.float32)
    p[o, (o % n) * d + o // n] = 1.0
    return jnp.asarray(p, dtype=BF16), jnp.asarray(p.T.copy(), dtype=BF16)


def _store_pattern(o_ref, tile, perm_ref, d, cols=slice(None)):
    n = PERM_TILE // d
    z = jnp.dot(perm_ref[...], tile, preferred_element_type=F32).astype(o_ref.dtype)
    for r in range(d):
        o_ref[r, :, cols] = z[r * n:(r + 1) * n]


def _load_pattern(x_ref, perm_ref, d):
    tile = jnp.concatenate([x_ref[r] for r in range(d)], axis=0)
    return jnp.dot(perm_ref[...], tile, preferred_element_type=F32)


def _pattern_spec(d, w, col=0):
    return pl.BlockSpec((d, PERM_TILE // d, w), lambda i, *_: (0, i, col))


def _rope_fwd(proj, tabs, perms, name):
    s = proj.shape[0]
    tm = PERM_TILE
    half = ROPE_DIM // 2
    wb = D_MODEL
    nd = len(DILATIONS) - 1

    def body(x_ref, c_ref, sa_ref, sb_ref, *rest):
        perm_refs, o_ref, op_refs = rest[:nd], rest[nd], rest[nd + 1:]
        is_v = pl.program_id(1) == 2
        qs = jnp.where(pl.program_id(1) == 0, HEAD_DIM ** -0.5, 1.0)
        c = jnp.where(is_v, 1.0, c_ref[...]) * qs
        sa = jnp.where(is_v, 0.0, sa_ref[...]) * qs
        sb = jnp.where(is_v, 0.0, sb_ref[...]) * qs
        for j in range(wb // 128):
            x = x_ref[:, j * 128:(j + 1) * 128].astype(F32)
            o_ref[:, j * 128:(j + 1) * 128] = (x * c + pltpu.roll(x, half, 1) * sa + pltpu.roll(x, 128 - half, 1) * sb).astype(o_ref.dtype)
        y = o_ref[...]
        for d, perm_ref, op_ref in zip(DILATIONS[1:], perm_refs, op_refs):
            _store_pattern(op_ref, y, perm_ref, d)

    blk = pl.BlockSpec((tm, wb), lambda i, j: (i, j))
    tab = pl.BlockSpec((tm, 128), lambda i, j: (i, 0))
    pm = pl.BlockSpec((tm, tm), lambda i, j: (0, 0))
    outs = pl.pallas_call(
        body, grid=(s // tm, 3), in_specs=[blk, tab, tab, tab] + [pm] * nd,
        out_specs=[blk] + [pl.BlockSpec((d, tm // d, wb), lambda i, j: (0, i, j)) for d in DILATIONS[1:]],
        out_shape=[SDS((s, 3 * wb), BF16)] + [SDS((d, s // d, 3 * wb), BF16) for d in DILATIONS[1:]],
        name=name, compiler_params=_cp("parallel", "parallel"))(proj, *tabs, *[perms[d][0] for d in DILATIONS[1:]])
    return [o.reshape(s, 3 * wb) for o in outs]


def _sum3_rope(ds_, perms, tabs, name, slab=None):
    s, w = ds_[0].shape
    tm = PERM_TILE
    half = ROPE_DIM // 2
    nd = len(DILATIONS) - 1

    def body(*refs):
        x_refs, perm_refs = refs[:nd + 1], refs[nd + 1:2 * nd + 1]
        tab_refs = refs[2 * nd + 1:2 * nd + 4]
        tot = x_refs[0][...].astype(F32)
        for d, x_ref, perm_ref in zip(DILATIONS[1:], x_refs[1:], perm_refs):
            tot = tot + _load_pattern(x_ref, perm_ref, d)
        for j in range(w // 128):
            x = tot[:, j * 128:(j + 1) * 128]
            if tabs is not None:
                x = x * tab_refs[0][...] + pltpu.roll(x, half, 1) * tab_refs[1][...] + pltpu.roll(x, 128 - half, 1) * tab_refs[2][...]
            refs[-1][:, j * 128:(j + 1) * 128] = x.astype(refs[-1].dtype)

    blk = pl.BlockSpec((tm, w), lambda i: (i, 0))
    tab = pl.BlockSpec((tm, 128), lambda i: (i, 0))
    pm = pl.BlockSpec((tm, tm), lambda i: (0, 0))
    extra = [] if tabs is None else list(tabs)
    out_shape, col0, more, more_specs, alias = _slab(s, w, BF16, slab, 2 * nd + 1 + len(extra))
    cb = col0 // w
    args = [ds_[0]] + [x.reshape(d, s // d, w) for d, x in zip(DILATIONS[1:], ds_[1:])] + [perms[d][1] for d in DILATIONS[1:]]
    return pl.pallas_call(
        body, grid=(s // tm,),
        in_specs=[blk] + [_pattern_spec(d, w) for d in DILATIONS[1:]] + [pm] * nd + [tab] * len(extra) + more_specs,
        out_specs=pl.BlockSpec((tm, w), lambda i: (i, cb)), out_shape=out_shape, input_output_aliases=alias,
        name=name, compiler_params=_cp("parallel"))(*args, *extra, *more)


def _band_valid(t, nq, nk, qofs, kofs, seq_len):
    qpos = t * 128 + qofs + lax.broadcasted_iota(jnp.int32, (nq, nk), 0)
    kpos = t * 128 + kofs + lax.broadcasted_iota(jnp.int32, (nq, nk), 1)
    sh = int(math.log2(seq_len))
    same = lax.shift_right_arithmetic(qpos, sh) == lax.shift_right_arithmetic(kpos, sh)
    return same & (jnp.abs(kpos - qpos) <= BAND_HALF)


def _window(r0, r1, r2):
    return jnp.concatenate([r0[128 - BAND_HALF:128], r1[...], r2[0:BAND_HALF]], axis=0)


WIN = 128 + 2 * BAND_HALF


def _first_head_lanes():
    return lax.broadcasted_iota(jnp.int32, (1, 2 * HEAD_DIM), 1) < HEAD_DIM


def _only(x, keep):
    return jnp.where(keep, x, jnp.zeros((), x.dtype))


def _win_specs(width, col, nt):
    return [pl.BlockSpec((128, width), lambda t: (jnp.maximum(t - 1, 0), col)),
            pl.BlockSpec((128, width), lambda t: (t, col)),
            pl.BlockSpec((128, width), lambda t: (jnp.minimum(t + 1, nt - 1), col))]


def _attn_fwd(qk, v_src, v_col, seq_len, name):
    s = qk.shape[0]
    nt = s // 128
    dm = D_MODEL

    def body(q_ref, k0, k1, k2, v0, v1, v2, o_ref, lse_ref):
        t = pl.program_id(0)
        valid = _band_valid(t, 128, WIN, 0, -BAND_HALF, seq_len)
        q = q_ref[...]
        kc = _window(k0, k1, k2)
        vc = _window(v0, v1, v2)
        first = _first_head_lanes()
        outs, lses = [], []
        for pr in range(N_HEADS // 2):
            ps = slice(pr * 128, (pr + 1) * 128)
            qp, kp, vp = q[:, ps], kc[:, ps], vc[:, ps]
            halves = []
            for keep in (first, ~first):
                sc = lax.dot_general(_only(qp, keep), kp, NT, preferred_element_type=F32)
                sc = jnp.where(valid, sc, NEG)
                m = jnp.max(sc, axis=1, keepdims=True)
                e = jnp.exp(sc - m)
                den = jnp.sum(e, axis=1, keepdims=True)
                halves.append(jnp.dot(e.astype(BF16), vp, preferred_element_type=F32) / den)
                lses.append(m + jnp.log(den))
            outs.append(jnp.where(first, halves[0], halves[1]))
        o_ref[...] = jnp.concatenate(outs, axis=1).astype(o_ref.dtype)
        lse_ref[...] = jnp.concatenate(lses, axis=1)

    in_specs = [pl.BlockSpec((128, dm), lambda t: (t, 0))] + _win_specs(dm, 1, nt) + _win_specs(dm, v_col, nt)
    return pl.pallas_call(
        body, grid=(nt,), in_specs=in_specs,
        out_specs=[pl.BlockSpec((128, dm), lambda t: (t, 0)), pl.BlockSpec((128, N_HEADS), lambda t: (t, 0))],
        out_shape=[SDS((s, dm), BF16), SDS((s, N_HEADS), F32)], name=name, compiler_params=_cp("parallel"))(
            qk, qk, qk, qk, v_src, v_src, v_src)


def _attn_combine(os_, lses, perms, expand, name):
    s, dm = os_[0].shape
    tm = PERM_TILE
    nd = len(DILATIONS) - 1

    def body(o1, o2, o3, l1, l2, l3, p2, p3, e_ref, out_ref, out_t_ref, lt_ref):
        ls = [l1[...], l2[...], l3[...]]
        m = jnp.maximum(jnp.maximum(ls[0], ls[1]), ls[2])
        es = [jnp.exp(l - m) for l in ls]
        tot = es[0] + es[1] + es[2]
        lt_ref[...] = m + jnp.log(tot)
        ovs = [o1[...].astype(F32), _load_pattern(o2, p2, DILATIONS[1]), _load_pattern(o3, p3, DILATIONS[2])]
        acc = jnp.zeros((tm, dm), F32)
        for e, o in zip(es, ovs):
            acc = acc + _dot_01(e / tot, e_ref[...]) * o
        out = acc.astype(out_ref.dtype)
        out_ref[...] = out
        out_t_ref[...] = out.T

    row = pl.BlockSpec((tm, dm), lambda i: (i, 0))
    st = pl.BlockSpec((tm, N_HEADS), lambda i: (i, 0))
    pm = pl.BlockSpec((tm, tm), lambda i: (0, 0))
    args = [os_[0]] + [o.reshape(d, s // d, dm) for d, o in zip(DILATIONS[1:], os_[1:])]
    return pl.pallas_call(
        body, grid=(s // tm,),
        in_specs=[row] + [_pattern_spec(d, dm) for d in DILATIONS[1:]] + [st, st, st, pm, pm, pl.BlockSpec((N_HEADS, dm), lambda i: (0, 0))],
        out_specs=[row, pl.BlockSpec((dm, tm), lambda i: (0, i)), st],
        out_shape=[SDS((s, dm + D_INNER), BF16), SDS((dm + D_INNER, s), BF16), SDS((s, N_HEADS), F32)],
        name=name, compiler_params=_cp("parallel"))(*args, *lses, *[perms[d][1] for d in DILATIONS[1:]], expand)


def _attn_delta(dmix, attn, expand_t, perms, name):
    s, dm = attn.shape[0], D_MODEL
    tm = PERM_TILE
    nd = len(DILATIONS) - 1

    def body(d_ref, a_ref, e_ref, *rest):
        perm_refs, o_ref, op_refs = rest[:nd], rest[nd], rest[nd + 1:]
        dv = d_ref[...]
        prod = dv.astype(F32) * a_ref[...].astype(F32)
        o_ref[...] = _dot_01(prod, e_ref[...])
        for d, perm_ref, op_ref in zip(DILATIONS[1:], perm_refs, op_refs):
            _store_pattern(op_ref, dv, perm_ref, d)

    row = pl.BlockSpec((tm, dm), lambda i: (i, 0))
    pm = pl.BlockSpec((tm, tm), lambda i: (0, 0))
    outs = pl.pallas_call(
        body, grid=(s // tm,), in_specs=[row, row, pl.BlockSpec((dm, N_HEADS), lambda i: (0, 0))] + [pm] * nd,
        out_specs=[pl.BlockSpec((tm, N_HEADS), lambda i: (i, 0))] + [_pattern_spec(d, dm) for d in DILATIONS[1:]],
        out_shape=[SDS((s, N_HEADS), F32)] + [SDS((d, s // d, dm), BF16) for d in DILATIONS[1:]],
        name=name, compiler_params=_cp("parallel"))(dmix, attn, expand_t, *[perms[d][0] for d in DILATIONS[1:]])
    return outs[0], [o.reshape(s, dm) for o in outs[1:]]


def _attn_bwd_dq(qk, v_src, v_col, do_src, lse, delta, seq_len, name):
    s = qk.shape[0]
    nt = s // 128
    dm = D_MODEL

    def body(q_ref, k0, k1, k2, v0, v1, v2, do_ref, lse_ref, dl_ref, dq_ref):
        t = pl.program_id(0)
        valid = _band_valid(t, 128, WIN, 0, -BAND_HALF, seq_len)
        q = q_ref[...]
        do = do_ref[...]
        kc = _window(k0, k1, k2)
        vc = _window(v0, v1, v2)
        lse_v, dl_v = lse_ref[...], dl_ref[...]
        first = _first_head_lanes()
        outs = []
        for pr in range(N_HEADS // 2):
            ps = slice(pr * 128, (pr + 1) * 128)
            qp, dop, kp, vp = q[:, ps], do[:, ps], kc[:, ps], vc[:, ps]
            halves = []
            for i, keep in enumerate((first, ~first)):
                h = 2 * pr + i
                sc = lax.dot_general(_only(qp, keep), kp, NT, preferred_element_type=F32)
                p = jnp.exp(jnp.where(valid, sc - lse_v[:, h:h + 1], NEG))
                dp = lax.dot_general(_only(dop, keep), vp, NT, preferred_element_type=F32)
                ds = p * (dp - dl_v[:, h:h + 1])
                halves.append(jnp.dot(ds.astype(BF16), kp, preferred_element_type=F32))
            outs.append(jnp.where(first, halves[0], halves[1]) * (HEAD_DIM ** -0.5))
        dq_ref[...] = jnp.concatenate(outs, axis=1).astype(dq_ref.dtype)

    row = pl.BlockSpec((128, dm), lambda t: (t, 0))
    st = pl.BlockSpec((128, N_HEADS), lambda t: (t, 0))
    in_specs = [row] + _win_specs(dm, 1, nt) + _win_specs(dm, v_col, nt) + [row, st, st]
    return pl.pallas_call(
        body, grid=(nt,), in_specs=in_specs, out_specs=row, out_shape=SDS((s, dm), BF16),
        name=name, compiler_params=_cp("parallel"))(qk, qk, qk, qk, v_src, v_src, v_src, do_src, lse, delta)


def _attn_bwd_dkv(qk, v_src, v_col, do_src, lse_t, delta_t, seq_len, name):
    s = qk.shape[0]
    nt = s // 128
    dm = D_MODEL

    def lane_window(r0, r1, r2):
        return jnp.concatenate([r0[:, 128 - BAND_HALF:128], r1[...], r2[:, 0:BAND_HALF]], axis=1)

    def body(k_ref, v_ref, q0, q1, q2, d0, d1, d2, l0, l1, l2, e0, e1, e2, dk_ref, dv_ref):
        t = pl.program_id(0)
        valid = _band_valid(t, 128, WIN, 0, -BAND_HALF, seq_len)
        k = k_ref[...]
        v = v_ref[...]
        qc = _window(q0, q1, q2)
        dc = _window(d0, d1, d2)
        lse_v = lane_window(l0, l1, l2)
        dl_v = lane_window(e0, e1, e2)
        first = _first_head_lanes()
        dks, dvs = [], []
        for pr in range(N_HEADS // 2):
            ps = slice(pr * 128, (pr + 1) * 128)
            kp, vp, qp, dop = k[:, ps], v[:, ps], qc[:, ps], dc[:, ps]
            dk_h, dv_h = [], []
            for i, keep in enumerate((first, ~first)):
                h = 2 * pr + i
                sc = lax.dot_general(_only(kp, keep), qp, NT, preferred_element_type=F32)
                p = jnp.exp(jnp.where(valid, sc - lse_v[h:h + 1, :], NEG))
                dv_h.append(jnp.dot(p.astype(BF16), dop, preferred_element_type=F32))
                dp = lax.dot_general(_only(vp, keep), dop, NT, preferred_element_type=F32)
                ds = p * (dp - dl_v[h:h + 1, :])
                dk_h.append(jnp.dot(ds.astype(BF16), qp, preferred_element_type=F32))
            dks.append(jnp.where(first, dk_h[0], dk_h[1]))
            dvs.append(jnp.where(first, dv_h[0], dv_h[1]))
        dk_ref[...] = jnp.concatenate(dks, axis=1).astype(dk_ref.dtype)
        dv_ref[...] = jnp.concatenate(dvs, axis=1).astype(dv_ref.dtype)

    row = pl.BlockSpec((128, dm), lambda t: (t, 0))
    stat = [pl.BlockSpec((N_HEADS, 128), lambda t: (0, jnp.maximum(t - 1, 0))), pl.BlockSpec((N_HEADS, 128), lambda t: (0, t)),
            pl.BlockSpec((N_HEADS, 128), lambda t: (0, jnp.minimum(t + 1, nt - 1)))]
    in_specs = ([pl.BlockSpec((128, dm), lambda t: (t, 1)), pl.BlockSpec((128, dm), lambda t: (t, v_col))]
                + _win_specs(dm, 0, nt) + _win_specs(dm, 0, nt) + stat + stat)
    return pl.pallas_call(
        body, grid=(nt,), in_specs=in_specs, out_specs=[row, row], out_shape=[SDS((s, dm), BF16), SDS((s, dm), BF16)],
        name=name, compiler_params=_cp("parallel"))(qk, v_src, qk, qk, qk, do_src, do_src, do_src, lse_t, lse_t, lse_t, delta_t, delta_t, delta_t)


CONV_COLS = (1024, 1408, 512, 256)


def _halo_specs(tm, tc, col0, nrow_blocks):
    r = tm // 16
    return [pl.BlockSpec((16, tc), lambda i, j: (jnp.maximum(i * r - 1, 0), col0 + j)),
            pl.BlockSpec((16, tc), lambda i, j: (jnp.minimum((i + 1) * r, nrow_blocks * r - 1), col0 + j))]


def _shifted(x_ref, hp_ref, hn_ref, i, last):
    x = x_ref[...].astype(F32)
    tm = x.shape[0]
    rows = lax.broadcasted_iota(jnp.int32, x.shape, 0)
    prev_row = jnp.where(i > 0, hp_ref[15:16, :].astype(F32), 0.0)
    next_row = jnp.where(i < last, hn_ref[0:1, :].astype(F32), 0.0)
    xp = jnp.where(rows == 0, prev_row, pltpu.roll(x, 1, 0))
    xn = jnp.where(rows == tm - 1, next_row, pltpu.roll(x, tm - 1, 0))
    return xp, x, xn


def _conv(x_src, col0, width, w3, bias, act, name, out_dtype=BF16, slab=None, transpose=False, wcol0=0):
    s = x_src.shape[0]
    tm = _pick(s, (256, 128))
    tc = _pick(width, CONV_COLS)
    nb = s // tm
    c0 = col0 // tc
    wc0 = wcol0 // tc

    def body(*refs):
        x_ref, hp_ref, hn_ref, w_ref, b_ref = refs[:5]
        o_ref = refs[-1]
        i = pl.program_id(0)
        xp, x, xn = _shifted(x_ref, hp_ref, hn_ref, i, nb - 1)
        w = w_ref[...]
        if transpose:
            y = w[2:3, :] * xp + w[1:2, :] * x + w[0:1, :] * xn
        else:
            y = w[0:1, :] * xp + w[1:2, :] * x + w[2:3, :] * xn + b_ref[...]
        if act:
            y = y * _sigmoid(y)
        o_ref[...] = y.astype(o_ref.dtype)

    in_specs = ([pl.BlockSpec((tm, tc), lambda i, j: (i, c0 + j))] + _halo_specs(tm, tc, c0, nb)
                + [pl.BlockSpec((3, tc), lambda i, j: (0, wc0 + j)), pl.BlockSpec((1, tc), lambda i, j: (0, wc0 + j))])
    out_shape, ocol, more, more_specs, alias = _slab(s, width, out_dtype, slab, 5)
    ob = ocol // tc
    return pl.pallas_call(
        body, grid=(nb, width // tc), in_specs=in_specs + more_specs, out_specs=pl.BlockSpec((tm, tc), lambda i, j: (i, ob + j)),
        out_shape=out_shape, input_output_aliases=alias, name=name, compiler_params=_cp("parallel", "parallel"))(
            x_src, x_src, x_src, w3, bias, *more)


def _conv_silu_bwd(x_src, col0, width, w3, bias, addends, add_widths, name):
    s = x_src.shape[0]
    tm = _pick(s, (256, 128))
    tc = _pick(width, CONV_COLS)
    nb = s // tm
    c0 = col0 // tc
    na = len(addends)

    def body(*refs):
        x_ref, hp_ref, hn_ref, w_ref, b_ref = refs[:5]
        a_refs = refs[5:5 + na]
        dp_ref, dw_ref, db_ref = refs[5 + na:]
        i, j = pl.program_id(1), pl.program_id(0)
        xp, x, xn = _shifted(x_ref, hp_ref, hn_ref, i, nb - 1)
        w = w_ref[...]
        pre = w[0:1, :] * xp + w[1:2, :] * x + w[2:3, :] * xn + b_ref[...]
        g = jnp.zeros_like(pre)
        for a_ref, aw in zip(a_refs, add_widths):
            av = a_ref[...].astype(F32)
            g = g + (av if aw == width else jnp.where(j < aw // tc, av, 0.0))
        sg = _sigmoid(pre)
        dpre = g * (sg * (1.0 + pre * (1.0 - sg)))
        dp_ref[...] = dpre.astype(dp_ref.dtype)

        @pl.when(i == 0)
        def _():
            dw_ref[...] = jnp.zeros_like(dw_ref)
            db_ref[...] = jnp.zeros_like(db_ref)

        dw_ref[...] += jnp.concatenate([jnp.sum(dpre * xp, axis=0, keepdims=True), jnp.sum(dpre * x, axis=0, keepdims=True),
                                        jnp.sum(dpre * xn, axis=0, keepdims=True)], axis=0)
        db_ref[...] += jnp.sum(dpre, axis=0, keepdims=True)

    r = tm // 16
    in_specs = [pl.BlockSpec((tm, tc), lambda j, i: (i, c0 + j)),
                pl.BlockSpec((16, tc), lambda j, i: (jnp.maximum(i * r - 1, 0), c0 + j)),
                pl.BlockSpec((16, tc), lambda j, i: (jnp.minimum((i + 1) * r, nb * r - 1), c0 + j)),
                pl.BlockSpec((3, tc), lambda j, i: (0, j)), pl.BlockSpec((1, tc), lambda j, i: (0, j))]
    for aw in add_widths:
        nblk = aw // tc
        in_specs.append(pl.BlockSpec((tm, tc), lambda j, i, nblk=nblk: (i, jnp.minimum(j, nblk - 1))))
    return pl.pallas_call(
        body, grid=(width // tc, nb), in_specs=in_specs,
        out_specs=[pl.BlockSpec((tm, tc), lambda j, i: (i, j)), pl.BlockSpec((3, tc), lambda j, i: (0, j)), pl.BlockSpec((1, tc), lambda j, i: (0, j))],
        out_shape=[SDS((s, width), BF16), SDS((3, width), F32), SDS((1, width), F32)],
        name=name, compiler_params=_cp("parallel", "arbitrary"))(x_src, x_src, x_src, w3, bias, *addends)


def _ffn_gate_fwd(u, w3, bias, name):
    s = u.shape[0]
    tm = _pick(s, (256, 128))
    tc = _pick(D_FF, CONV_COLS)
    nb = s // tm
    nj = D_FF // tc

    def body(g_ref, gp, gn, u_ref, up, un, wg_ref, wu_ref, bg_ref, bu_ref, o_ref, ot_ref):
        i = pl.program_id(0)
        outs = []
        for (x_ref, hp, hn, w_ref, b_ref) in ((g_ref, gp, gn, wg_ref, bg_ref), (u_ref, up, un, wu_ref, bu_ref)):
            xp, x, xn = _shifted(x_ref, hp, hn, i, nb - 1)
            w = w_ref[...]
            outs.append(w[0:1, :] * xp + w[1:2, :] * x + w[2:3, :] * xn + b_ref[...])
        gate, upv = outs
        out = (gate * _sigmoid(gate) * upv).astype(o_ref.dtype)
        o_ref[...] = out
        ot_ref[...] = out.T

    def xspecs(c0):
        return [pl.BlockSpec((tm, tc), lambda i, j: (i, c0 + j))] + _halo_specs(tm, tc, c0, nb)

    in_specs = (xspecs(0) + xspecs(nj)
                + [pl.BlockSpec((3, tc), lambda i, j: (0, j)), pl.BlockSpec((3, tc), lambda i, j: (0, nj + j)),
                   pl.BlockSpec((1, tc), lambda i, j: (0, j)), pl.BlockSpec((1, tc), lambda i, j: (0, nj + j))])
    return pl.pallas_call(
        body, grid=(nb, nj), in_specs=in_specs,
        out_specs=[pl.BlockSpec((tm, tc), lambda i, j: (i, j)), pl.BlockSpec((tc, tm), lambda i, j: (j, i))],
        out_shape=[SDS((s, D_FF), BF16), SDS((D_FF, s), BF16)], name=name, compiler_params=_cp("parallel", "parallel"))(
            u, u, u, u, u, u, w3, w3, bias, bias)


def _ffn_gate_bwd(u, w3, bias, dact, name):
    s = u.shape[0]
    tm = _pick(s, (256, 128))
    tc = _pick(D_FF, CONV_COLS)
    nb = s // tm
    nj = D_FF // tc

    def body(g_ref, gp, gn, u_ref, up, un, wg_ref, wu_ref, bg_ref, bu_ref, da_ref, dg_ref, du_ref, dwg_ref, dwu_ref, dbg_ref, dbu_ref):
        i = pl.program_id(1)
        sh, pre = [], []
        for (x_ref, hp, hn, w_ref, b_ref) in ((g_ref, gp, gn, wg_ref, bg_ref), (u_ref, up, un, wu_ref, bu_ref)):
            xs3 = _shifted(x_ref, hp, hn, i, nb - 1)
            w = w_ref[...]
            sh.append(xs3)
            pre.append(w[0:1, :] * xs3[0] + w[1:2, :] * xs3[1] + w[2:3, :] * xs3[2] + b_ref[...])
        gate, upv = pre
        da = da_ref[...].astype(F32)
        sg = _sigmoid(gate)
        dgate = da * upv * (sg * (1.0 + gate * (1.0 - sg)))
        dup = da * gate * sg
        dg_ref[...] = dgate.astype(dg_ref.dtype)
        du_ref[...] = dup.astype(du_ref.dtype)

        @pl.when(i == 0)
        def _():
            for r in (dwg_ref, dwu_ref, dbg_ref, dbu_ref):
                r[...] = jnp.zeros_like(r)

        for d, xs3, dw_ref, db_ref in ((dgate, sh[0], dwg_ref, dbg_ref), (dup, sh[1], dwu_ref, dbu_ref)):
            dw_ref[...] += jnp.concatenate([jnp.sum(d * xs3[0], axis=0, keepdims=True), jnp.sum(d * xs3[1], axis=0, keepdims=True),
                                            jnp.sum(d * xs3[2], axis=0, keepdims=True)], axis=0)
            db_ref[...] += jnp.sum(d, axis=0, keepdims=True)

    r = tm // 16

    def xspecs(c0):
        return [pl.BlockSpec((tm, tc), lambda j, i: (i, c0 + j)),
                pl.BlockSpec((16, tc), lambda j, i: (jnp.maximum(i * r - 1, 0), c0 + j)),
                pl.BlockSpec((16, tc), lambda j, i: (jnp.minimum((i + 1) * r, nb * r - 1), c0 + j))]

    in_specs = (xspecs(0) + xspecs(nj)
                + [pl.BlockSpec((3, tc), lambda j, i: (0, j)), pl.BlockSpec((3, tc), lambda j, i: (0, nj + j)),
                   pl.BlockSpec((1, tc), lambda j, i: (0, j)), pl.BlockSpec((1, tc), lambda j, i: (0, nj + j)),
                   pl.BlockSpec((tm, tc), lambda j, i: (i, j))])
    blk = pl.BlockSpec((tm, tc), lambda j, i: (i, j))
    w_o = pl.BlockSpec((3, tc), lambda j, i: (0, j))
    b_o = pl.BlockSpec((1, tc), lambda j, i: (0, j))
    return pl.pallas_call(
        body, grid=(nj, nb), in_specs=in_specs, out_specs=[blk, blk, w_o, w_o, b_o, b_o],
        out_shape=[SDS((s, D_FF), BF16), SDS((s, D_FF), BF16), SDS((3, D_FF), F32), SDS((3, D_FF), F32), SDS((1, D_FF), F32), SDS((1, D_FF), F32)],
        name=name, compiler_params=_cp("parallel", "arbitrary"))(u, u, u, u, u, u, w3, w3, bias, bias, dact)


def _exchange_sems(n):
    return [pltpu.SemaphoreType.DMA((7 * n,)), pltpu.SemaphoreType.DMA((7 * n,)), pltpu.SemaphoreType.DMA((n,))]


def _exchange_copies(srcs, outs, send_sems, recv_sems, local_sems):
    x, y, c = lax.axis_index("x"), lax.axis_index("y"), lax.axis_index("c")
    me = 4 * x + 2 * y + c
    locals_ = [pltpu.make_async_copy(srcs[a].at[me], outs[a].at[me], local_sems.at[a]) for a in range(len(srcs))]
    sends, recvs = [], []
    for k in range(1, N_DEV):
        px, py, pc = x ^ ((k >> 2) & 1), y ^ ((k >> 1) & 1), c ^ (k & 1)
        peer = 4 * px + 2 * py + pc
        for a in range(len(srcs)):
            sems = dict(send_sem=send_sems.at[a * 7 + k - 1], recv_sem=recv_sems.at[a * 7 + k - 1], device_id_type=MESH)
            sends.append(pltpu.make_async_remote_copy(src_ref=srcs[a].at[peer], dst_ref=outs[a].at[me], device_id=(px, py, pc), **sems))
            recvs.append(pltpu.make_async_remote_copy(src_ref=srcs[a].at[peer], dst_ref=outs[a].at[peer], device_id=(x, y, c), **sems))
    return locals_, sends, recvs


def _exchange_start(srcs, outs, send_sems, recv_sems, local_sems):
    locals_, sends, _ = _exchange_copies(srcs, outs, send_sems, recv_sems, local_sems)
    for cp in locals_ + sends:
        cp.start()


def _exchange_wait(srcs, outs, send_sems, recv_sems, local_sems):
    locals_, sends, recvs = _exchange_copies(srcs, outs, send_sems, recv_sems, local_sems)
    for cp in recvs:
        cp.wait_recv()
    for cp in sends:
        cp.wait_send()
    for cp in locals_:
        cp.wait()


def _gather_copies(srcs, outs, send_sems, recv_sems, local_sems):
    x, y, c = lax.axis_index("x"), lax.axis_index("y"), lax.axis_index("c")
    me, sibling = (x, y, c), (x, y, 1 - c)
    chips = [(1 - x, y), (x, 1 - y), (1 - x, 1 - y)]

    def copy(a, k, block, to, src=None):
        dst = outs[a].at[4 * block[0] + 2 * block[1] + block[2]]
        return pltpu.make_async_remote_copy(
            src_ref=dst if src is None else src, dst_ref=dst,
            send_sem=send_sems.at[a * 7 + k], recv_sem=recv_sems.at[a * 7 + k], device_id=to, device_id_type=MESH)

    n = len(srcs)
    locals_ = [pltpu.make_async_copy(srcs[a], outs[a].at[4 * x + 2 * y + c], local_sems.at[a]) for a in range(n)]
    own = [copy(a, 0, me, sibling, src=srcs[a]) for a in range(n)]
    own += [copy(a, 1 + j, me, (*chip, c), src=srcs[a]) for a in range(n) for j, chip in enumerate(chips)]
    landed_ici = [copy(a, 1 + j, (*chip, c), me) for j, chip in enumerate(chips) for a in range(n)]
    passed = [copy(a, 4 + j, (*chip, c), sibling) for j, chip in enumerate(chips) for a in range(n)]
    landed_d2d = [copy(a, 0, sibling, me) for a in range(n)]
    landed_d2d += [copy(a, 4 + j, (*chip, 1 - c), me) for a in range(n) for j, chip in enumerate(chips)]
    return locals_, own, landed_ici, passed, landed_d2d


def _gather_start(*refs):
    locals_, own, _, _, _ = _gather_copies(*refs)
    for cp in locals_ + own:
        cp.start()


def _gather_forward(*refs):
    _, _, landed_ici, passed, _ = _gather_copies(*refs)
    for arrived, onward in zip(landed_ici, passed):
        arrived.wait_recv()
        onward.start()


def _gather_finish(*refs):
    locals_, own, _, passed, landed_d2d = _gather_copies(*refs)
    for cp in landed_d2d:
        cp.wait_recv()
    for cp in own + passed:
        cp.wait_send()
    for cp in locals_:
        cp.wait()


def _ssd_common(dt_ref, dtt_ref, al_r, al_c, bi_r, bi_c, off, rev):
    li = lax.broadcasted_iota(jnp.int32, (CHUNK, CHUNK), 0)
    si = lax.broadcasted_iota(jnp.int32, (CHUNK, CHUNK), 1)
    mask = (li <= si) if rev else (li >= si)
    mask_t = (li >= si) if rev else (li <= si)
    a_r = -jnp.exp(al_r[...])
    a_c = -jnp.exp(al_c[...])
    pre = dt_ref[:, off:off + N_HEADS] + bi_r[...]
    dt = _softplus(pre)
    cs = jnp.dot(mask.astype(F32), dt * a_r, precision=HIGH, preferred_element_type=F32)
    dt_t = _softplus(dtt_ref[off:off + N_HEADS, :] + bi_c[...])
    cs_t = jnp.dot(dt_t * a_c, mask_t.astype(F32), precision=HIGH, preferred_element_type=F32)
    tot = cs[0:1, :] if rev else cs[CHUNK - 1:CHUNK, :]
    return mask, mask_t, a_r, pre, dt, cs, cs_t, tot


def _ssd_fwd(xbc, dt_raw, dt_t, args_f, args_b, name, gather=()):
    s = xbc.shape[0]
    nc = s // CHUNK
    hp = D_INNER // N_HEADS
    hpg = N_HEADS // N_GROUPS
    gs = N_GROUPS * D_STATE
    ng = len(gather)
    fwd_step = (nc * 27) // 32

    def chunk(x_ref, b_ref, c_ref, dt_ref, dtt_ref, alr, alc, bir, bic, y_ref, st_ref, h_scr, off, rev):
        mask, _, _, _, dt, cs, cs_t, tot = _ssd_common(dt_ref, dtt_ref, alr, alc, bir, bic, off, rev)
        first = _first_head_lanes()
        xs = x_ref[...]
        ys = []
        for g in range(N_GROUPS):
            bg = b_ref[:, g * D_STATE:(g + 1) * D_STATE]
            cg = c_ref[:, g * D_STATE:(g + 1) * D_STATE]
            gm = lax.dot_general(cg, bg, NT, preferred_element_type=F32)
            hcat = h_scr[g]
            st_ref[0, g] = hcat
            ch = lax.dot_general(cg, hcat.astype(BF16), NT, preferred_element_type=F32)
            xdd = []
            for pr in range(hpg // 2):
                h0 = g * hpg + 2 * pr
                lanes = slice(h0 * hp, (h0 + 2) * hp)
                cols = [cs[:, h:h + 1] for h in (h0, h0 + 1)]
                pair = lambda a, b: jnp.where(first, a, b)
                xdf = xs[:, lanes].astype(F32) * pair(dt[:, h0:h0 + 1], dt[:, h0 + 1:h0 + 2])
                xdb = xdf.astype(BF16)
                yh = []
                for i, h in enumerate((h0, h0 + 1)):
                    lm = jnp.exp(jnp.where(mask, cols[i] - cs_t[h:h + 1, :], NEG))
                    yh.append(jnp.dot((gm * lm).astype(BF16), xdb, preferred_element_type=F32))
                ecs = pair(jnp.exp(cols[0]), jnp.exp(cols[1]))
                ys.append(pair(yh[0], yh[1]) + ecs * ch[:, 2 * pr * hp:(2 * pr + 2) * hp])
                dec = pair(jnp.exp(tot[:, h0:h0 + 1] - cols[0]), jnp.exp(tot[:, h0 + 1:h0 + 2] - cols[1]))
                xdd.append((xdf * dec).astype(BF16))
            snew = lax.dot_general(jnp.concatenate(xdd, axis=1), bg, TN, preferred_element_type=F32)
            for r in range(hpg):
                rs = slice(r * hp, (r + 1) * hp)
                h_scr[g, rs, :] = jnp.exp(tot[:, g * hpg + r:g * hpg + r + 1]) * hcat[rs] + snew[rs]
        y_ref[...] = jnp.concatenate(ys, axis=1)

    def body(*refs):
        in_f, in_b = refs[0:9], refs[9:18]
        g_src = refs[18:18 + ng]
        out_f, out_b = refs[18 + ng:20 + ng], refs[20 + ng:22 + ng]
        g_dst = refs[22 + ng:22 + 2 * ng]
        hs_f, hs_b = refs[22 + 2 * ng], refs[23 + 2 * ng]
        g_sems = refs[24 + 2 * ng:]
        step = pl.program_id(0)

        @pl.when(step == 0)
        def _():
            hs_f[...] = jnp.zeros_like(hs_f)
            hs_b[...] = jnp.zeros_like(hs_b)
            if ng:
                _gather_start(g_src, g_dst, *g_sems)

        chunk(*in_f, *out_f, hs_f, 0, False)
        chunk(*in_b, *out_b, hs_b, N_HEADS, True)

        if ng:
            @pl.when(step == fwd_step)
            def _():
                _gather_forward(g_src, g_dst, *g_sems)

            @pl.when(step == nc - 1)
            def _():
                _gather_finish(g_src, g_dst, *g_sems)

    small = lambda shape: pl.BlockSpec(shape, lambda c: (0, 0))

    def specs(cm):
        ins = [pl.BlockSpec((CHUNK, D_INNER), lambda c: (cm(c), 0)),
               pl.BlockSpec((CHUNK, gs), lambda c: (cm(c), D_INNER // gs)),
               pl.BlockSpec((CHUNK, gs), lambda c: (cm(c), D_INNER // gs + 1)),
               pl.BlockSpec((CHUNK, 128), lambda c: (cm(c), 0)),
               pl.BlockSpec((2 * N_HEADS, CHUNK), lambda c: (0, cm(c))),
               small((1, N_HEADS)), small((N_HEADS, 1)), small((1, N_HEADS)), small((N_HEADS, 1))]
        outs = [pl.BlockSpec((CHUNK, D_INNER), lambda c: (cm(c), 0)),
                pl.BlockSpec((1, N_GROUPS, hpg * hp, D_STATE), lambda c: (cm(c), 0, 0, 0))]
        return ins, outs

    ins_f, outs_f = specs(lambda c: c)
    ins_b, outs_b = specs(lambda c: nc - 1 - c)
    any_spec = pl.BlockSpec(memory_space=pl.ANY)
    one_dir = [SDS((s, D_INNER), F32), SDS((nc, N_GROUPS, hpg * hp, D_STATE), F32)]
    state = pltpu.VMEM((N_GROUPS, hpg * hp, D_STATE), F32)
    return pl.pallas_call(
        body, grid=(nc,), in_specs=ins_f + ins_b + [any_spec] * ng, out_specs=outs_f + outs_b + [any_spec] * ng,
        out_shape=one_dir + one_dir + [SDS((N_DEV,) + g.shape, g.dtype) for g in gather],
        scratch_shapes=[state, state] + (_exchange_sems(ng) if ng else []), name=name, compiler_params=_cp("arbitrary"))(
            xbc, xbc, xbc, dt_raw, dt_t, *args_f, xbc, xbc, xbc, dt_raw, dt_t, *args_b, *gather)


def _ssd_bwd(xbc, dt_raw, dt_t, al_r, al_c, bi_r, bi_c, states, dy, off, rev, name, exchange=()):
    s = xbc.shape[0]
    nc = s // CHUNK
    hp = D_INNER // N_HEADS
    gs = N_GROUPS * D_STATE
    hpg = N_HEADS // N_GROUPS
    cm = (lambda c: c) if rev else (lambda c: nc - 1 - c)
    nx = len(exchange)

    def body(*refs):
        x_ref, b_ref, c_ref, dt_ref, dtt_ref, alr, alc, bir, bic, st_ref, dy_ref = refs[:11]
        xch_src = refs[11:11 + nx]
        dx_ref, ddt_ref, dal_ref, dbi_ref = refs[11 + nx:15 + nx]
        xch_dst = refs[15 + nx:15 + 2 * nx]
        dh_scr = refs[15 + 2 * nx]
        xch_sems = refs[16 + 2 * nx:]

        @pl.when(pl.program_id(0) == 0)
        def _():
            dh_scr[...] = jnp.zeros_like(dh_scr)
            dal_ref[...] = jnp.zeros_like(dal_ref)
            dbi_ref[...] = jnp.zeros_like(dbi_ref)
            if nx:
                _exchange_start(xch_src, xch_dst, *xch_sems)

        if nx:
            @pl.when(pl.program_id(0) == nc - 1)
            def _():
                _exchange_wait(xch_src, xch_dst, *xch_sems)

        mask, mask_t, a_r, pre, dt, cs, cs_t, tot = _ssd_common(dt_ref, dtt_ref, alr, alc, bir, bic, off, rev)
        xs = x_ref[...]
        dyv = dy_ref[...]
        rows = lax.broadcasted_iota(jnp.int32, (CHUNK, 1), 0)
        end_row = (rows == 0) if rev else (rows == CHUNK - 1)
        lane_h = lax.broadcasted_iota(jnp.int32, (1, N_HEADS), 1)
        sub_h = lax.broadcasted_iota(jnp.int32, (N_HEADS, 1), 0)
        first = _first_head_lanes()
        dcs_all = jnp.zeros((CHUNK, N_HEADS), F32)
        colw_all = jnp.zeros((N_HEADS, CHUNK), F32)
        dxsum_all = jnp.zeros((CHUNK, N_HEADS), F32)
        dxs, dbs, dcs_out = [], [], []
        for g in range(N_GROUPS):
            bg = b_ref[:, g * D_STATE:(g + 1) * D_STATE]
            cg = c_ref[:, g * D_STATE:(g + 1) * D_STATE]
            gm = lax.dot_general(cg, bg, NT, preferred_element_type=F32)
            gm_t = lax.dot_general(bg, cg, NT, preferred_element_type=F32)
            hcat = st_ref[0, g]
            dhcat = dh_scr[g]
            hb, dhb = hcat.astype(BF16), dhcat.astype(BF16)
            ch = lax.dot_general(cg, hb, NT, preferred_element_type=F32)
            z = lax.dot_general(bg, dhb, NT, preferred_element_type=F32)
            dg_sum = jnp.zeros((CHUNK, CHUNK), F32)
            dchs, xdds, t_hs = [], [], []
            for pr in range(hpg // 2):
                h0 = g * hpg + 2 * pr
                lanes = slice(h0 * hp, (h0 + 2) * hp)
                ps = slice(2 * pr * hp, (2 * pr + 2) * hp)
                pair = lambda a, b: jnp.where(first, a, b)
                cols = [cs[:, h:h + 1] for h in (h0, h0 + 1)]
                tots = [tot[:, h:h + 1] for h in (h0, h0 + 1)]
                xh = xs[:, lanes].astype(F32)
                dtc = pair(dt[:, h0:h0 + 1], dt[:, h0 + 1:h0 + 2])
                xdf = xh * dtc
                xd = xdf.astype(BF16)
                dyh = dyv[:, lanes]
                dyb = dyh.astype(BF16)
                ecs = pair(jnp.exp(cols[0]), jnp.exp(cols[1]))
                dec = pair(jnp.exp(tots[0] - cols[0]), jnp.exp(tots[1] - cols[1]))
                yoff_t = dyh * (ecs * ch[:, ps])
                dchs.append((ecs * dyh).astype(BF16))
                xdd = xdf * dec
                ddec_t = xdd * z[:, ps]
                row_terms = yoff_t - ddec_t
                xdds.append(xdd.astype(BF16))
                dxd_h = []
                for i, (h, keep) in enumerate(((h0, first), (h0 + 1, ~first))):
                    rs = slice((2 * pr + i) * hp, (2 * pr + i + 1) * hp)
                    lm = jnp.exp(jnp.where(mask, cols[i] - cs_t[h:h + 1, :], NEG))
                    m = gm * lm
                    t_h = jnp.exp(tots[i])
                    lm_t = jnp.exp(jnp.where(mask_t, cs_t[h:h + 1, :] - cols[i], NEG))
                    dxd_h.append(jnp.dot((gm_t * lm_t).astype(BF16), dyb, preferred_element_type=F32))
                    dm = lax.dot_general(_only(dyb, keep), xd, NT, preferred_element_type=F32)
                    dg_sum = dg_sum + dm * lm
                    w = dm * m
                    ddec_tot = jnp.sum(jnp.sum(_only(ddec_t, keep), axis=0, keepdims=True), axis=1, keepdims=True)
                    dtot = jnp.sum(jnp.sum(dhcat[rs] * hcat[rs], axis=0, keepdims=True), axis=1, keepdims=True) * t_h + ddec_tot
                    t_hs.append(t_h)
                    dcs = jnp.sum(_only(row_terms, keep) + w, axis=1, keepdims=True)
                    dcs = dcs + jnp.where(end_row, dtot, 0.0)
                    colw_all = colw_all + (sub_h == h).astype(F32) * jnp.sum(w, axis=0, keepdims=True)
                    dcs_all = dcs_all + dcs * (lane_h == h).astype(F32)
                dxd = pair(dxd_h[0], dxd_h[1]) + dec * z[:, ps]
                dxx = dxd * xh
                for i, (h, keep) in enumerate(((h0, first), (h0 + 1, ~first))):
                    dxsum_all = dxsum_all + jnp.sum(_only(dxx, keep), axis=1, keepdims=True) * (lane_h == h).astype(F32)
                dxs.append(dxd * dtc)
            dgs = dg_sum.astype(BF16)
            dchc = jnp.concatenate(dchs, axis=1)
            dcs_out.append(jnp.dot(dgs, bg, preferred_element_type=F32) + jnp.dot(dchc, hb, preferred_element_type=F32))
            dbs.append(lax.dot_general(dgs, cg, TN, preferred_element_type=F32)
                       + jnp.dot(jnp.concatenate(xdds, axis=1), dhb, preferred_element_type=F32))
            dh_in = lax.dot_general(dchc, cg, TN, preferred_element_type=F32)
            for r in range(hpg):
                rs = slice(r * hp, (r + 1) * hp)
                dh_scr[g, rs, :] = dh_in[rs] + t_hs[r] * dhcat[rs]
        dx_ref[...] = jnp.concatenate(dxs + dbs + dcs_out, axis=1).astype(dx_ref.dtype)
        mt = mask_t.astype(F32)
        da = (jnp.dot(mt, dcs_all, precision=HIGH, preferred_element_type=F32)
              - lax.dot_general(mt, colw_all, NT, precision=HIGH, preferred_element_type=F32))
        dal_ref[...] += jnp.sum(da * dt, axis=0, keepdims=True) * a_r
        ddt_raw = (da * a_r + dxsum_all) * _sigmoid(pre)
        ddt_ref[...] = ddt_raw
        dbi_ref[...] += jnp.sum(ddt_raw, axis=0, keepdims=True)

    small = lambda shape: pl.BlockSpec(shape, lambda c: (0, 0))
    in_specs = [pl.BlockSpec((CHUNK, D_INNER), lambda c: (cm(c), 0)),
                pl.BlockSpec((CHUNK, gs), lambda c: (cm(c), D_INNER // gs)),
                pl.BlockSpec((CHUNK, gs), lambda c: (cm(c), D_INNER // gs + 1)),
                pl.BlockSpec((CHUNK, 128), lambda c: (cm(c), 0)),
                pl.BlockSpec((2 * N_HEADS, CHUNK), lambda c: (0, cm(c))),
                small((1, N_HEADS)), small((N_HEADS, 1)), small((1, N_HEADS)), small((N_HEADS, 1)),
                pl.BlockSpec((1, N_GROUPS, hpg * hp, D_STATE), lambda c: (cm(c), 0, 0, 0)),
                pl.BlockSpec((CHUNK, D_INNER), lambda c: (cm(c), 0))]
    any_spec = pl.BlockSpec(memory_space=pl.ANY)
    return pl.pallas_call(
        body, grid=(nc,), in_specs=in_specs + [any_spec] * nx,
        out_specs=[pl.BlockSpec((CHUNK, XBC), lambda c: (cm(c), 0)), pl.BlockSpec((CHUNK, N_HEADS), lambda c: (cm(c), 0)),
                   small((1, N_HEADS)), small((1, N_HEADS))] + [any_spec] * nx,
        out_shape=[SDS((s, XBC), BF16), SDS((s, N_HEADS), F32), SDS((1, N_HEADS), F32), SDS((1, N_HEADS), F32)]
        + [SDS(a.shape, a.dtype) for a in exchange],
        scratch_shapes=[pltpu.VMEM((N_GROUPS, hpg * hp, D_STATE), F32)] + (_exchange_sems(nx) if nx else []),
        name=name, compiler_params=_cp("arbitrary"))(xbc, xbc, xbc, dt_raw, dt_t, al_r, al_c, bi_r, bi_c, states, dy, *exchange)


def _gate_fwd(yf, yb, xbc, proj, dskip_x, norm_w, mix, mix_t, name):
    s = yf.shape[0]
    tm = _pick(s, (256, 128))
    gw = D_INNER // N_GROUPS
    zc = 3 * D_MODEL // D_INNER

    def body(yf_ref, yb_ref, x_ref, z_ref, d_ref, w_ref, _m, _mt, o_ref, ot_ref):
        y = yf_ref[...] + yb_ref[...] + d_ref[...] * x_ref[...].astype(F32)
        z = z_ref[...].astype(F32)
        gt = y * (z * _sigmoid(z))
        outs = []
        for g in range(N_GROUPS):
            gg = gt[:, g * gw:(g + 1) * gw]
            outs.append(gg * lax.rsqrt(jnp.mean(gg * gg, axis=-1, keepdims=True) + EPS))
        out = (jnp.concatenate(outs, axis=1) * w_ref[...]).astype(o_ref.dtype)
        o_ref[...] = out
        ot_ref[...] = out.T

    row = pl.BlockSpec((tm, D_INNER), lambda i: (i, 0))
    vec = pl.BlockSpec((1, D_INNER), lambda i: (0, 0))
    any_spec = pl.BlockSpec(memory_space=pl.ANY)
    cb = (mix.shape[1] - D_INNER) // D_INNER
    return pl.pallas_call(
        body, grid=(s // tm,), in_specs=[row, row, row, pl.BlockSpec((tm, D_INNER), lambda i: (i, zc)), vec, vec, any_spec, any_spec],
        out_specs=[pl.BlockSpec((tm, D_INNER), lambda i: (i, cb)), pl.BlockSpec((D_INNER, tm), lambda i: (cb, i))],
        out_shape=[SDS(mix.shape, BF16), SDS(mix_t.shape, BF16)], input_output_aliases={6: 0, 7: 1},
        name=name, compiler_params=_cp("parallel"))(yf, yb, xbc, proj, dskip_x, norm_w, mix, mix_t)


def _gate_bwd(yf, yb, xbc, proj, dskip_x, norm_w, dmix, dproj, name):
    s = yf.shape[0]
    tm = _pick(s, (256, 128))
    gw = D_INNER // N_GROUPS
    zc = 3 * D_MODEL // D_INNER

    def body(yf_ref, yb_ref, x_ref, z_ref, d_ref, w_ref, do_ref, _, dy_ref, dz_ref, dxs_ref, dw_ref, dd_ref):
        xf = x_ref[...].astype(F32)
        y = yf_ref[...] + yb_ref[...] + d_ref[...] * xf
        z = z_ref[...].astype(F32)
        sg = _sigmoid(z)
        sz = z * sg
        gt = y * sz
        do = do_ref[...].astype(F32)
        dgh = do * w_ref[...]
        ghs, dgts = [], []
        for g in range(N_GROUPS):
            gg = gt[:, g * gw:(g + 1) * gw]
            r = lax.rsqrt(jnp.mean(gg * gg, axis=-1, keepdims=True) + EPS)
            gh = gg * r
            dg = dgh[:, g * gw:(g + 1) * gw]
            ghs.append(gh)
            dgts.append(r * (dg - gh * jnp.mean(dg * gh, axis=-1, keepdims=True)))
        ghat = jnp.concatenate(ghs, axis=1)
        dgt = jnp.concatenate(dgts, axis=1)
        dy = dgt * sz
        dy_ref[...] = dy
        dz_ref[...] = (dgt * y * (sg * (1.0 + z * (1.0 - sg)))).astype(dz_ref.dtype)
        dxs_ref[...] = (dy * d_ref[...]).astype(dxs_ref.dtype)

        @pl.when(pl.program_id(0) == 0)
        def _():
            dw_ref[...] = jnp.zeros_like(dw_ref)
            dd_ref[...] = jnp.zeros_like(dd_ref)

        dw_ref[...] += jnp.sum(do * ghat, axis=0, keepdims=True)
        dd_ref[...] += jnp.sum(dy * xf, axis=0, keepdims=True)

    row = pl.BlockSpec((tm, D_INNER), lambda i: (i, 0))
    vec = pl.BlockSpec((1, D_INNER), lambda i: (0, 0))
    dz_shape, _, more, more_specs, alias = _slab(s, D_INNER, BF16, (dproj.shape[1], zc * D_INNER, dproj), 7, out_idx=1)
    return pl.pallas_call(
        body, grid=(s // tm,),
        in_specs=[row, row, row, pl.BlockSpec((tm, D_INNER), lambda i: (i, zc)), vec, vec, pl.BlockSpec((tm, D_INNER), lambda i: (i, 1))] + more_specs,
        out_specs=[row, pl.BlockSpec((tm, D_INNER), lambda i: (i, zc)), row, vec, vec],
        out_shape=[SDS((s, D_INNER), F32), dz_shape, SDS((s, D_INNER), BF16), SDS((1, D_INNER), F32), SDS((1, D_INNER), F32)],
        input_output_aliases=alias, name=name, compiler_params=_cp("arbitrary"))(yf, yb, xbc, proj, dskip_x, norm_w, dmix, *more)


def _adamw(parts, w, m, v, name):
    r, c = w.shape
    tr = _pick(r, (256, 352, 128))

    def body(p_ref, w_ref, m_ref, v_ref, g_ref, d_ref, nm_ref, nv_ref):
        g = p_ref[0].astype(F32)
        for i in range(1, N_DEV):
            g = g + p_ref[i].astype(F32)
        mn = B1 * m_ref[...] + (1.0 - B1) * g
        vn = B2 * v_ref[...] + (1.0 - B2) * (g * g)
        m_hat = mn / (1.0 - B1 ** STEP)
        v_hat = vn / (1.0 - B2 ** STEP)
        g_ref[...] = g
        d_ref[...] = -LR * (m_hat / (jnp.sqrt(v_hat) + AEPS) + WD * w_ref[...])
        nm_ref[...] = mn
        nv_ref[...] = vn

    blk = pl.BlockSpec((tr, c), lambda i: (i, 0))
    return pl.pallas_call(
        body, grid=(r // tr,), in_specs=[pl.BlockSpec((N_DEV, tr, c), lambda i: (0, i, 0)), blk, blk, blk],
        out_specs=[blk, blk, blk, blk], out_shape=[SDS((r, c), F32)] * 4, name=name, compiler_params=_cp("parallel"))(parts, w, m, v)


def _sum_parts(parts, name):
    _, r, c = parts.shape

    def body(p_ref, o_ref):
        g = p_ref[0]
        for i in range(1, N_DEV):
            g = g + p_ref[i]
        o_ref[...] = g

    return pl.pallas_call(body, out_shape=SDS((r, c), F32), name=name)(parts)


def _adamw_small(gs, ws, ms, vs, name):
    n = len(gs)

    def body(*refs):
        g_refs, w_refs, m_refs, v_refs = refs[:n], refs[n:2 * n], refs[2 * n:3 * n], refs[3 * n:4 * n]
        d_refs, nm_refs, nv_refs = refs[4 * n:5 * n], refs[5 * n:6 * n], refs[6 * n:7 * n]
        for i in range(n):
            gv = g_refs[i][...]
            mn = B1 * m_refs[i][...] + (1.0 - B1) * gv
            vn = B2 * v_refs[i][...] + (1.0 - B2) * (gv * gv)
            m_hat = mn / (1.0 - B1 ** STEP)
            v_hat = vn / (1.0 - B2 ** STEP)
            d_refs[i][...] = -LR * (m_hat / (jnp.sqrt(v_hat) + AEPS) + WD * w_refs[i][...])
            nm_refs[i][...] = mn
            nv_refs[i][...] = vn

    outs = pl.pallas_call(body, out_shape=[SDS(g.shape, F32) for g in gs] * 3, name=name)(*gs, *ws, *ms, *vs)
    return outs[:n], outs[n:2 * n], outs[2 * n:]


def _my_index():
    return 4 * lax.axis_index("x") + 2 * lax.axis_index("y") + lax.axis_index("c")


def _all_gather(shards, name):
    n = len(shards)

    def body(*refs):
        srcs, outs = refs[:n], refs[n:2 * n]
        _gather_start(srcs, outs, *refs[2 * n:])
        _gather_forward(srcs, outs, *refs[2 * n:])
        _gather_finish(srcs, outs, *refs[2 * n:])

    any_spec = pl.BlockSpec(memory_space=pl.ANY)
    return pl.pallas_call(
        body, in_specs=[any_spec] * n, out_specs=[any_spec] * n,
        out_shape=[SDS((N_DEV,) + s.shape, s.dtype) for s in shards], scratch_shapes=_exchange_sems(n), name=name)(*shards)


def _exchange(arrays, name):
    n = len(arrays)

    def body(*refs):
        srcs, outs = refs[:n], refs[n:2 * n]
        _exchange_start(srcs, outs, *refs[2 * n:])
        _exchange_wait(srcs, outs, *refs[2 * n:])

    any_spec = pl.BlockSpec(memory_space=pl.ANY)
    return pl.pallas_call(
        body, in_specs=[any_spec] * n, out_specs=[any_spec] * n,
        out_shape=[SDS(a.shape, a.dtype) for a in arrays], scratch_shapes=_exchange_sems(n), name=name)(*arrays)


def _to_pattern(t, d):
    if d == 1:
        return t
    s, w = t.shape
    return t.reshape(s // d, d, w).transpose(1, 0, 2).reshape(s, w)


def _from_pattern(t, d):
    if d == 1:
        return t
    s, w = t.shape
    return t.reshape(d, s // d, w).transpose(1, 0, 2).reshape(s, w)


def _pad_lanes(t, n):
    return jnp.pad(t, ((0, 0), (0, n - t.shape[1])))


def _to_shards(g, axis):
    r, c = g.shape
    if axis == 0:
        return g.reshape(N_DEV, r // N_DEV, c)
    return g.reshape(r, N_DEV, c // N_DEV).transpose(1, 0, 2)


def _local_step(x, target, p, late_shards=(), early_exchange=True):
    s = x.shape[0]
    tabs_f, tabs_b = _rope_tables(s)
    expand = jnp.asarray(np.repeat(np.eye(N_HEADS, dtype=np.float32), HEAD_DIM, axis=1))
    w_main, w_dt = p["w_in"][:, :MAIN_W], _pad_lanes(p["w_in"][:, MAIN_W:], 128)
    al_r = {"f": p["a_log_f"], "b": p["a_log_b"]}
    bi_r = {"f": p["dt_bias_f"], "b": p["dt_bias_b"]}
    dskip_x = jnp.repeat(p["d_skip"], D_INNER // N_HEADS, axis=1)
    ssm_w3, ffn_w3 = p["ssm_conv_w"].T, p["ffn_conv_w"].T

    h1, h1t = _rmsnorm_fwd(x, p["norm1_w"], "norm1_fwd")
    proj = _matmul(h1, w_main, name="in_proj")
    dt_raw = _matmul(h1, w_dt, name="in_proj_dt", out_dtype=F32)
    dt_t = dt_raw[:, :2 * N_HEADS].T
    perms = {d: _perm_matrices(d) for d in DILATIONS[1:]}
    qkv = _rope_fwd(proj, tabs_f, perms, "rope_fwd")
    v_col = 2
    os_, lses = [], []
    for d, qkv_p in zip(DILATIONS, qkv):
        o_p, lse_p = _attn_fwd(qkv_p, qkv_p, v_col, s // d, f"attn_fwd_d{d}")
        os_.append(o_p)
        lses.append(_from_pattern(lse_p, d))
    mix, mix_t, lse_tot = _attn_combine(os_, lses, perms, expand, "attn_combine")

    xbc = _conv(proj, 3 * D_MODEL + D_INNER, XBC, ssm_w3, p["ssm_conv_b"], True, "ssm_conv_fwd")
    col = lambda r: r.reshape(N_HEADS, 1)
    ssd_args = {k: (al_r[k], col(al_r[k]), bi_r[k], col(bi_r[k])) for k in ("f", "b")}
    yf, st_f, yb, st_b, *got = _ssd_fwd(xbc, dt_raw, dt_t, ssd_args["f"], ssd_args["b"], "ssd_fwd", gather=late_shards)
    if late_shards:
        p = dict(p, w_out=got[0].reshape(2 * D_MODEL, D_MODEL), w_down=got[2].reshape(D_FF, D_MODEL),
                 w_up=got[1].transpose(1, 0, 2).reshape(D_MODEL, 2 * D_FF))
    mix, mix_t = _gate_fwd(yf, yb, xbc, proj, dskip_x, p["ssm_norm_w"], mix, mix_t, "ssm_gate_fwd")

    x2 =_matmul(mix, p["w_out"], name="out_proj", out_dtype=F32, residual=x)
    h2, h2t = _rmsnorm_fwd(x2, p["norm2_w"], "norm2_fwd")
    u = _matmul(h2, p["w_up"], name="ffn_up")
    act, act_t = _ffn_gate_fwd(u, ffn_w3, p["ffn_conv_b"], "ffn_gate_fwd")
    x3 = _matmul(act, p["w_down"], name="ffn_down", out_dtype=F32, residual=x2)

    dx3, dx3b, g_final, loss = _final_norm_loss(x3, p["final_norm_w"].reshape(1, D_MODEL), target, "final_norm_loss")
    g_w_down = _matmul(act_t, dx3b, name="dw_down")
    dact = _matmul(dx3b, p["w_down"], name="d_act", trans_b=True)
    dug, duu, dwg, dwu, dbg, dbu = _ffn_gate_bwd(u, ffn_w3, p["ffn_conv_b"], dact, "ffn_gate_bwd")
    du = _conv(dug, 0, D_FF, ffn_w3, p["ffn_conv_b"], False, "ffn_conv_bwd_gate", slab=(2 * D_FF, 0, None), transpose=True)
    du = _conv(duu, 0, D_FF, ffn_w3, p["ffn_conv_b"], False, "ffn_conv_bwd_up", slab=(2 * D_FF, D_FF, du), transpose=True, wcol0=D_FF)
    g_w_up = _matmul(h2t, du, name="dw_up")
    none_a, none_w = jnp.zeros((s, 128), BF16), jnp.zeros((D_MODEL, 128), BF16)
    dx2, dx2b, g_norm2 = _proj_norm_bwd(du, p["w_up"], none_a, none_w, x2, p["norm2_w"], dx3, "ffn_up_norm2_bwd")
    g_w_out = _matmul(mix_t, dx2b, name="dw_out")
    dmix = _matmul(dx2b, p["w_out"], name="d_mix", trans_b=True)

    delta, do_pat = _attn_delta(dmix, mix, expand.T, perms, "attn_delta")
    dqs, dks, dvs = [], [], []
    for d, qkv_p, do_p in zip(DILATIONS, qkv, [dmix] + do_pat):
        lse_p, dl_p = _to_pattern(lse_tot, d), _to_pattern(delta, d)
        dqs.append(_attn_bwd_dq(qkv_p, qkv_p, v_col, do_p, lse_p, dl_p, s // d, f"attn_bwd_dq_d{d}"))
        dk, dv = _attn_bwd_dkv(qkv_p, qkv_p, v_col, do_p, lse_p.T, dl_p.T, s // d, f"attn_bwd_dkv_d{d}")
        dks.append(dk)
        dvs.append(dv)
    dproj = _sum3_rope(dqs, perms, tabs_b, "rope_bwd_q", slab=(MAIN_W, 0, None))
    dproj = _sum3_rope(dks, perms, tabs_b, "rope_bwd_k", slab=(MAIN_W, D_MODEL, dproj))
    dproj = _sum3_rope(dvs, perms, None, "sum_dv", slab=(MAIN_W, 2 * D_MODEL, dproj))

    dy, dproj, dxs_skip, g_ssm_norm, g_dskip_lanes = _gate_bwd(yf, yb, xbc, proj, dskip_x, p["ssm_norm_w"], dmix, dproj, "ssm_gate_bwd")
    early_f = [_to_shards(g_w_up, 1), _to_shards(g_w_down, 0)] if early_exchange else []
    early_b = [_to_shards(g_w_out, 0)] if early_exchange else []
    dxbc_f, ddt_f, g_al_f, g_bi_f, *got_f = _ssd_bwd(xbc, dt_raw, dt_t, *ssd_args["f"], st_f, dy, 0, False, "ssd_bwd_f", exchange=early_f)
    dxbc_b, ddt_b, g_al_b, g_bi_b, *got_b = _ssd_bwd(xbc, dt_raw, dt_t, *ssd_args["b"], st_b, dy, N_HEADS, True, "ssd_bwd_b",
                                                     exchange=early_b)
    if early_exchange:
        (g_w_up, g_w_down), (g_w_out,) = got_f, got_b
    dpre, g_ssm_w3, g_ssm_cb = _conv_silu_bwd(proj, 3 * D_MODEL + D_INNER, XBC, ssm_w3, p["ssm_conv_b"],
                                              [dxbc_f, dxbc_b, dxs_skip], [XBC, XBC, D_INNER], "ssm_conv_bwd")
    dproj = _conv(dpre, 0, XBC, ssm_w3, p["ssm_conv_b"], False, "ssm_conv_bwd_x", transpose=True,
                  slab=(MAIN_W, 3 * D_MODEL + D_INNER, dproj))

    ddt =_pad_lanes(jnp.concatenate([ddt_f, ddt_b], axis=1), 128).astype(BF16)
    g_w_main = _matmul(h1t, dproj, name="dw_in")
    g_w_dt = _matmul(h1t, ddt, name="dw_in_dt")
    g_w_in = jnp.concatenate([g_w_main, g_w_dt[:, :2 * N_HEADS]], axis=1)
    late = [_to_shards(g_w_in, 1)] if early_exchange else []
    grad_x, _, g_norm1, *got = _proj_norm_bwd(dproj, w_main, ddt, w_dt, x, p["norm1_w"], dx2, "in_proj_norm1_bwd", exchange=late)
    if early_exchange:
        g_w_in = got[0]

    g_dskip = jnp.sum(g_dskip_lanes.reshape(N_HEADS, D_INNER // N_HEADS), axis=1).reshape(1, N_HEADS)
    small = {
        "norm1_w": g_norm1, "ssm_conv_w": g_ssm_w3.T, "ssm_conv_b": g_ssm_cb, "a_log_f": g_al_f, "a_log_b": g_al_b,
        "dt_bias_f": g_bi_f, "dt_bias_b": g_bi_b, "d_skip": g_dskip, "ssm_norm_w": g_ssm_norm, "norm2_w": g_norm2,
        "ffn_conv_w": jnp.concatenate([dwg, dwu], axis=1).T, "ffn_conv_b": jnp.concatenate([dbg, dbu], axis=1), "final_norm_w": g_final,
    }
    big = {"w_in": g_w_in, "w_out": g_w_out, "w_up": g_w_up, "w_down": g_w_down}
    return loss[0, 0], grad_x, big, small


SMALL_ORDER = ("norm1_w", "ssm_conv_w", "ssm_conv_b", "a_log_f", "a_log_b", "dt_bias_f", "dt_bias_b", "d_skip",
               "ssm_norm_w", "norm2_w", "ffn_conv_w", "ffn_conv_b", "final_norm_w")
SHARDED_SMALL = ("ssm_conv_w", "ffn_conv_w")
BIG_ORDER = ("w_in", "w_out", "w_up", "w_down")


def _pack(vals):
    rows = []
    for v in vals:
        f = v.reshape(-1).astype(F32)
        n = -(-f.shape[0] // 128) * 128
        rows.append(jnp.pad(f, (0, n - f.shape[0])).reshape(-1, 128))
    out = jnp.concatenate(rows, axis=0)
    pad = -out.shape[0] % 8
    return jnp.pad(out, ((0, pad), (0, 0)))


def _unpack(packed, shapes):
    out, r = [], 0
    for shp in shapes:
        n = math.prod(shp)
        nr = -(-n // 128)
        out.append(packed[r:r + nr].reshape(-1)[:n].reshape(shp))
        r += nr
    return out


def kernel(x, norm1_w, w_in, ssm_conv_w, ssm_conv_b, a_log_f, a_log_b, dt_bias_f, dt_bias_b, d_skip, ssm_norm_w, w_out, norm2_w, w_up, ffn_conv_w, ffn_conv_b, w_down, final_norm_w, loss_target, m_norm1_w, m_w_in, m_ssm_conv_w, m_ssm_conv_b, m_a_log_f, m_a_log_b, m_dt_bias_f, m_dt_bias_b, m_d_skip, m_ssm_norm_w, m_w_out, m_norm2_w, m_w_up, m_ffn_conv_w, m_ffn_conv_b, m_w_down, m_final_norm_w, v_norm1_w, v_w_in, v_ssm_conv_w, v_ssm_conv_b, v_a_log_f, v_a_log_b, v_dt_bias_f, v_dt_bias_b, v_d_skip, v_ssm_norm_w, v_w_out, v_norm2_w, v_w_up, v_ffn_conv_w, v_ffn_conv_b, v_w_down, v_final_norm_w):
    w = dict(norm1_w=norm1_w, w_in=w_in, ssm_conv_w=ssm_conv_w, ssm_conv_b=ssm_conv_b, a_log_f=a_log_f, a_log_b=a_log_b,
             dt_bias_f=dt_bias_f, dt_bias_b=dt_bias_b, d_skip=d_skip, ssm_norm_w=ssm_norm_w, w_out=w_out, norm2_w=norm2_w,
             w_up=w_up, ffn_conv_w=ffn_conv_w, ffn_conv_b=ffn_conv_b, w_down=w_down, final_norm_w=final_norm_w)
    mo = dict(norm1_w=m_norm1_w, w_in=m_w_in, ssm_conv_w=m_ssm_conv_w, ssm_conv_b=m_ssm_conv_b, a_log_f=m_a_log_f, a_log_b=m_a_log_b,
              dt_bias_f=m_dt_bias_f, dt_bias_b=m_dt_bias_b, d_skip=m_d_skip, ssm_norm_w=m_ssm_norm_w, w_out=m_w_out, norm2_w=m_norm2_w,
              w_up=m_w_up, ffn_conv_w=m_ffn_conv_w, ffn_conv_b=m_ffn_conv_b, w_down=m_w_down, final_norm_w=m_final_norm_w)
    vo = dict(norm1_w=v_norm1_w, w_in=v_w_in, ssm_conv_w=v_ssm_conv_w, ssm_conv_b=v_ssm_conv_b, a_log_f=v_a_log_f, a_log_b=v_a_log_b,
              dt_bias_f=v_dt_bias_f, dt_bias_b=v_dt_bias_b, d_skip=v_d_skip, ssm_norm_w=v_ssm_norm_w, w_out=v_w_out, norm2_w=v_norm2_w,
              w_up=v_w_up, ffn_conv_w=v_ffn_conv_w, ffn_conv_b=v_ffn_conv_b, w_down=v_w_down, final_norm_w=v_final_norm_w)
    me = _my_index()

    g_in, g_conv = _all_gather([w["w_in"][0].astype(BF16), _pack([w["ssm_conv_w"][0], w["ffn_conv_w"][0]])], "w_in_all_gather")
    conv_rows = [_unpack(g_conv[i], [ssm_conv_w.shape[1:], ffn_conv_w.shape[1:]]) for i in range(N_DEV)]
    full = {
        "w_in": g_in.transpose(1, 0, 2).reshape(D_MODEL, N_DEV * w_in.shape[2]),
        "ssm_conv_w": jnp.concatenate([c[0] for c in conv_rows], axis=0),
        "ffn_conv_w": jnp.concatenate([c[1] for c in conv_rows], axis=0),
    }
    for k in ("norm1_w", "ssm_conv_b", "a_log_f", "a_log_b", "dt_bias_f", "dt_bias_b", "d_skip", "ssm_norm_w", "norm2_w", "ffn_conv_b",
              "final_norm_w"):
        full[k] = w[k]
    late = [w["w_out"][0].astype(BF16), w["w_up"][0].astype(BF16), w["w_down"][0].astype(BF16)]

    loss_part, grad_x, big, small = _local_step(x[0], loss_target[0], full, late)

    small_shapes = [(1,)] + [small[k].shape for k in SMALL_ORDER]
    packed = _pack([loss_part] + [small[k] for k in SMALL_ORDER])
    out_small = jnp.broadcast_to(packed[None], (N_DEV,) + packed.shape)
    (r_small,) = _exchange([out_small], "small_grads_exchange")
    r_in, r_out, r_up, r_down = big["w_in"], big["w_out"], big["w_up"], big["w_down"]

    outs_g, outs_d, outs_m, outs_v = {}, {}, {}, {}
    for k, parts in zip(BIG_ORDER, (r_in, r_out, r_up, r_down)):
        g, dlt, nm, nv = _adamw(parts, w[k][0], mo[k][0], vo[k][0], f"adamw_{k}")
        outs_g[k], outs_d[k], outs_m[k], outs_v[k] = g[None], dlt[None], nm[None], nv[None]
    tot = _unpack(_sum_parts(r_small, "small_grads_sum"), small_shapes)
    loss = tot[0][0]
    gs = dict(zip(SMALL_ORDER, tot[1:]))
    g_own = {}
    for k in SMALL_ORDER:
        if k in SHARDED_SMALL:
            rows = w[k].shape[1]
            g_own[k] = lax.dynamic_slice_in_dim(gs[k], me * rows, rows, axis=0)[None]
        else:
            g_own[k] = gs[k].reshape(w[k].shape)
    two_d = lambda a: a.reshape(-1, a.shape[-1])
    d_s, m_s, v_s = _adamw_small([two_d(g_own[k]) for k in SMALL_ORDER], [two_d(w[k]) for k in SMALL_ORDER],
                                 [two_d(mo[k]) for k in SMALL_ORDER], [two_d(vo[k]) for k in SMALL_ORDER], "adamw_small")
    for k, a, b, c in zip(SMALL_ORDER, d_s, m_s, v_s):
        shp = w[k].shape
        outs_g[k], outs_d[k], outs_m[k], outs_v[k] = g_own[k], a.reshape(shp), b.reshape(shp), c.reshape(shp)

    order = ("norm1_w", "w_in", "ssm_conv_w", "ssm_conv_b", "a_log_f", "a_log_b", "dt_bias_f", "dt_bias_b", "d_skip", "ssm_norm_w",
             "w_out", "norm2_w", "w_up", "ffn_conv_w", "ffn_conv_b", "w_down", "final_norm_w")
    return (loss, grad_x[None], *[outs_g[k] for k in order], *[outs_d[k] for k in order],
            *[outs_m[k] for k in order], *[outs_v[k] for k in order])
```

```python
import math

import numpy as np
import jax
import jax.numpy as jnp
from jax import lax
from jax.experimental import pallas as pl
from jax.experimental.pallas import tpu as pltpu

F32 = jnp.float32
BF16 = jnp.bfloat16
SDS = jax.ShapeDtypeStruct

N_DEV = 8
D_MODEL = 1024
N_HEADS = 16
HEAD_DIM = 64
ROPE_DIM = 16
ROPE_THETA = 500000.0
DILATIONS = (1, 4, 16)
BAND_HALF = 64
D_INNER = 1024
N_GROUPS = 4
D_STATE = 128
CHUNK = 128
XBC = D_INNER + 2 * N_GROUPS * D_STATE
D_FF = 2816
MAIN_W = 3 * D_MODEL + D_INNER + XBC
EPS = 1e-6
LR, B1, B2, AEPS, WD, STEP = 0.001, 0.9, 0.999, 1e-08, 0.01, 10
NEG = -1e30
VMEM_LIMIT = 56 * 1024 * 1024
MESH = pl.DeviceIdType.MESH
HIGH = lax.Precision.HIGHEST
NT = (((1,), (1,)), ((), ()))
TN = (((0,), (0,)), ((), ()))


def _cp(*sem):
    return pltpu.CompilerParams(dimension_semantics=sem, vmem_limit_bytes=VMEM_LIMIT)


def _pick(n, cands):
    for c in cands:
        if n % c == 0:
            return c
    raise ValueError(f"no tile for {n}")


def _sigmoid(x):
    return 1.0 / (1.0 + jnp.exp(-x))


def _dot_01(x, m01):
    mb = m01.astype(BF16)
    out, r = None, x
    for _ in range(3):
        p = r.astype(BF16)
        r = r - p.astype(F32)
        t = jnp.dot(p, mb, preferred_element_type=F32)
        out = t if out is None else out + t
    return out


def _softplus(x):
    return jnp.maximum(x, 0.0) + jnp.log1p(jnp.exp(-jnp.abs(x)))


def _slab(s, width, dtype, slab, n_in, out_idx=0):
    if slab is None:
        return SDS((s, width), dtype), 0, [], [], {}
    total, col0, into = slab
    if into is None:
        return SDS((s, total), dtype), col0, [], [], {}
    return SDS((s, total), dtype), col0, [into], [pl.BlockSpec(memory_space=pl.ANY)], {n_in: out_idx}


def _matmul(a, b, *, name, trans_b=False, out_dtype=BF16, residual=None):
    m, k = a.shape
    n = b.shape[0] if trans_b else b.shape[1]
    tk = k if k <= 2048 else _pick(k, (2048, 1408, 1024, 512))
    nk = k // tk
    if nk == 1:
        tm = _pick(m, (2048, 1408, 1024, 512, 256, 128))
        tn = _pick(n, (512, 256, 128))
    else:
        tm = _pick(m, (1024, 1408, 512, 256, 128))
        tn = _pick(n, (1024, 1408, 512, 256, 128))
    dn = NT if trans_b else (((1,), (0,)), ((), ()))

    def body(*refs):
        a_ref, b_ref = refs[0], refs[1]
        o_ref, acc = refs[-2], refs[-1]
        kk = pl.program_id(2)

        @pl.when(kk == 0)
        def _():
            acc[...] = jnp.zeros_like(acc)

        acc[...] += lax.dot_general(a_ref[...], b_ref[...], dn, preferred_element_type=F32)

        @pl.when(kk == nk - 1)
        def _():
            r = acc[...]
            if residual is not None:
                r = r + refs[2][...].astype(F32)
            o_ref[...] = r.astype(o_ref.dtype)

    in_specs = [pl.BlockSpec((tm, tk), lambda i, j, kk: (i, kk)),
                pl.BlockSpec((tn, tk), lambda i, j, kk: (j, kk)) if trans_b else pl.BlockSpec((tk, tn), lambda i, j, kk: (kk, j))]
    args = [a, b]
    if residual is not None:
        in_specs.append(pl.BlockSpec((tm, tn), lambda i, j, kk: (i, j)))
        args.append(residual)
    return pl.pallas_call(
        body, grid=(m // tm, n // tn, nk), in_specs=in_specs,
        out_specs=pl.BlockSpec((tm, tn), lambda i, j, kk: (i, j)),
        out_shape=SDS((m, n), out_dtype), scratch_shapes=[pltpu.VMEM((tm, tn), F32)],
        name=name, compiler_params=_cp("parallel", "parallel", "arbitrary"))(*args)


def _rmsnorm_fwd(x, w, name):
    s, d = x.shape
    tm = _pick(s, (512, 128))

    def body(x_ref, w_ref, o_ref, ot_ref):
        xf = x_ref[...]
        r = lax.rsqrt(jnp.mean(xf * xf, axis=-1, keepdims=True) + EPS)
        out = (xf * r * w_ref[...]).astype(o_ref.dtype)
        o_ref[...] = out
        ot_ref[...] = out.T

    return pl.pallas_call(
        body, grid=(s // tm,), in_specs=[pl.BlockSpec((tm, d), lambda i: (i, 0)), pl.BlockSpec((1, d), lambda i: (0, 0))],
        out_specs=[pl.BlockSpec((tm, d), lambda i: (i, 0)), pl.BlockSpec((d, tm), lambda i: (0, i))],
        out_shape=[SDS((s, d), BF16), SDS((d, s), BF16)], name=name, compiler_params=_cp("parallel"))(x, w)


def _proj_norm_bwd(da, wt, da2, wt2, x, w, dres, name, exchange=()):
    s, k = da.shape
    d = x.shape[1]
    tm = _pick(s, (512, 128))
    tk = _pick(k, (1408, 1024, 512))
    nk, ni = k // tk, s // tm
    nx = len(exchange)
    last = nk - 1

    def body(*refs):
        a_ref, b_ref, a2_ref, b2_ref, x_ref, w_ref, dres_ref = refs[:7]
        xch_src = refs[7:7 + nx]
        dx_ref, dxb_ref, dw_ref = refs[7 + nx:10 + nx]
        xch_dst = refs[10 + nx:10 + 2 * nx]
        acc = refs[10 + 2 * nx]
        xch_sems = refs[11 + 2 * nx:]
        kk, i = pl.program_id(0), pl.program_id(1)

        @pl.when((i == 0) & (kk == 0))
        def _():
            dw_ref[...] = jnp.zeros_like(dw_ref)
            if nx:
                _exchange_start(xch_src, xch_dst, *xch_sems)

        @pl.when(kk == 0)
        def _():
            acc[i] = lax.dot_general(a2_ref[...], b2_ref[...], NT, preferred_element_type=F32)

        acc[i] += lax.dot_general(a_ref[...], b_ref[...], NT, preferred_element_type=F32)

        @pl.when(kk == last)
        def _():
            dh = acc[i]
            xf = x_ref[...]
            r = lax.rsqrt(jnp.mean(xf * xf, axis=-1, keepdims=True) + EPS)
            xhat = xf * r
            g = dh * w_ref[...]
            dx = dres_ref[...] + r * (g - xhat * jnp.mean(g * xhat, axis=-1, keepdims=True))
            dx_ref[...] = dx
            dxb_ref[...] = dx.astype(dxb_ref.dtype)
            dw_ref[...] += jnp.sum(dh * xhat, axis=0, keepdims=True)

        if nx:
            @pl.when((i == ni - 1) & (kk == last))
            def _():
                _exchange_wait(xch_src, xch_dst, *xch_sems)

    row_last = pl.BlockSpec((tm, d), lambda kk, i: (jnp.where(kk == last, i, 0), 0))
    vec = pl.BlockSpec((1, d), lambda kk, i: (0, 0))
    any_spec = pl.BlockSpec(memory_space=pl.ANY)
    in_specs = [pl.BlockSpec((tm, tk), lambda kk, i: (i, kk)), pl.BlockSpec((d, tk), lambda kk, i: (0, kk)),
                pl.BlockSpec((tm, da2.shape[1]), lambda kk, i: (jnp.where(kk == 0, i, 0), 0)),
                pl.BlockSpec((d, wt2.shape[1]), lambda kk, i: (0, 0)),
                row_last, vec, row_last]
    return pl.pallas_call(
        body, grid=(nk, ni), in_specs=in_specs + [any_spec] * nx, out_specs=[row_last, row_last, vec] + [any_spec] * nx,
        out_shape=[SDS((s, d), F32), SDS((s, d), BF16), SDS((1, d), F32)] + [SDS(e.shape, e.dtype) for e in exchange],
        scratch_shapes=[pltpu.VMEM((ni, tm, d), F32)] + (_exchange_sems(nx) if nx else []),
        name=name, compiler_params=_cp("arbitrary", "arbitrary"))(da, wt, da2, wt2, x, w, dres, *exchange)


def _final_norm_loss(x, w, target, name):
    s, d = x.shape
    tm = _pick(s, (512, 128))

    def body(x_ref, w_ref, t_ref, dx_ref, dxb_ref, dw_ref, loss_ref):
        xf = x_ref[...]
        r = lax.rsqrt(jnp.mean(xf * xf, axis=-1, keepdims=True) + EPS)
        xhat = xf * r
        wv = w_ref[...]
        e = xhat * wv - t_ref[...]
        dy = e * (1.0 / d)
        g = dy * wv
        dx = r * (g - xhat * jnp.mean(g * xhat, axis=-1, keepdims=True))
        dx_ref[...] = dx
        dxb_ref[...] = dx.astype(dxb_ref.dtype)

        @pl.when(pl.program_id(0) == 0)
        def _():
            dw_ref[...] = jnp.zeros_like(dw_ref)
            loss_ref[...] = jnp.zeros_like(loss_ref)

        dw_ref[...] += jnp.sum(dy * xhat, axis=0, keepdims=True)
        loss_ref[...] += jnp.sum(jnp.sum(e * e, axis=1, keepdims=True), axis=0, keepdims=True) * (0.5 / d)

    row = pl.BlockSpec((tm, d), lambda i: (i, 0))
    vec = pl.BlockSpec((1, d), lambda i: (0, 0))
    return pl.pallas_call(
        body, grid=(s // tm,), in_specs=[row, vec, row], out_specs=[row, row, vec, pl.BlockSpec((1, 128), lambda i: (0, 0))],
        out_shape=[SDS((s, d), F32), SDS((s, d), BF16), SDS((1, d), F32), SDS((1, 128), F32)],
        name=name, compiler_params=_cp("arbitrary"))(x, w, target)


def _rope_tables(s):
    half = ROPE_DIM // 2
    f32 = np.float32
    inv_freq = np.power(f32(ROPE_THETA), -np.arange(half, dtype=f32) * f32(2.0) / f32(ROPE_DIM)).astype(f32)
    ang = (np.arange(s, dtype=f32)[:, None] * inv_freq[None, :]).astype(f32)
    cos, sin = np.cos(ang).astype(f32), np.sin(ang).astype(f32)
    z = np.zeros((s, HEAD_DIM - ROPE_DIM), f32)
    zh = np.zeros((s, half), f32)
    c = np.concatenate([cos, cos, z + 1.0], axis=1)
    sa = np.concatenate([zh, sin, z], axis=1)
    sb = np.concatenate([-sin, zh, z], axis=1)
    two = lambda t: np.concatenate([t, t], axis=1)
    c, sa, sb = two(c), two(sa), two(sb)
    fwd, bwd = (c, sa, sb), (c, np.roll(sb, half, axis=1), np.roll(sa, -half, axis=1))
    return tuple(jnp.asarray(t) for t in fwd), tuple(jnp.asarray(t) for t in bwd)


PERM_TILE = 256


def _perm_matrices(d):
    n = PERM_TILE // d
    o = np.arange(PERM_TILE)
    p = np.zeros((PERM_TILE, PERM_TILE), np.float32)
    p[o, (o % n) * d + o // n] = 1.0
    return jnp.asarray(p, dtype=BF16), jnp.asarray(p.T.copy(), dtype=BF16)


def _store_pattern(o_ref, tile, perm_ref, d, cols=slice(None)):
    n = PERM_TILE // d
    z = jnp.dot(perm_ref[...], tile, preferred_element_type=F32).astype(o_ref.dtype)
    for r in range(d):
        o_ref[r, :, cols] = z[r * n:(r + 1) * n]


def _load_pattern(x_ref, perm_ref, d):
    tile = jnp.concatenate([x_ref[r] for r in range(d)], axis=0)
    return jnp.dot(perm_ref[...], tile, preferred_element_type=F32)


def _pattern_spec(d, w, col=0):
    return pl.BlockSpec((d, PERM_TILE // d, w), lambda i, *_: (0, i, col))


def _rope_fwd(proj, tabs, perms, name):
    s = proj.shape[0]
    tm = PERM_TILE
    half = ROPE_DIM // 2
    wb = D_MODEL
    nd = len(DILATIONS) - 1

    def body(x_ref, c_ref, sa_ref, sb_ref, *rest):
        perm_refs, o_ref, op_refs = rest[:nd], rest[nd], rest[nd + 1:]
        is_v = pl.program_id(1) == 2
        qs = jnp.where(pl.program_id(1) == 0, HEAD_DIM ** -0.5, 1.0)
        c = jnp.where(is_v, 1.0, c_ref[...]) * qs
        sa = jnp.where(is_v, 0.0, sa_ref[...]) * qs
        sb = jnp.where(is_v, 0.0, sb_ref[...]) * qs
        for j in range(wb // 128):
            x = x_ref[:, j * 128:(j + 1) * 128].astype(F32)
            o_ref[:, j * 128:(j + 1) * 128] = (x * c + pltpu.roll(x, half, 1) * sa + pltpu.roll(x, 128 - half, 1) * sb).astype(o_ref.dtype)
        y = o_ref[...]
        for d, perm_ref, op_ref in zip(DILATIONS[1:], perm_refs, op_refs):
            _store_pattern(op_ref, y, perm_ref, d)

    blk = pl.BlockSpec((tm, wb), lambda i, j: (i, j))
    tab = pl.BlockSpec((tm, 128), lambda i, j: (i, 0))
    pm = pl.BlockSpec((tm, tm), lambda i, j: (0, 0))
    outs = pl.pallas_call(
        body, grid=(s // tm, 3), in_specs=[blk, tab, tab, tab] + [pm] * nd,
        out_specs=[blk] + [pl.BlockSpec((d, tm // d, wb), lambda i, j: (0, i, j)) for d in DILATIONS[1:]],
        out_shape=[SDS((s, 3 * wb), BF16)] + [SDS((d, s // d, 3 * wb), BF16) for d in DILATIONS[1:]],
        name=name, compiler_params=_cp("parallel", "parallel"))(proj, *tabs, *[perms[d][0] for d in DILATIONS[1:]])
    return [o.reshape(s, 3 * wb) for o in outs]


def _sum3_rope(ds_, perms, tabs, name, slab=None):
    s, w = ds_[0].shape
    tm = PERM_TILE
    half = ROPE_DIM // 2
    nd = len(DILATIONS) - 1

    def body(*refs):
        x_refs, perm_refs = refs[:nd + 1], refs[nd + 1:2 * nd + 1]
        tab_refs = refs[2 * nd + 1:2 * nd + 4]
        tot = x_refs[0][...].astype(F32)
        for d, x_ref, perm_ref in zip(DILATIONS[1:], x_refs[1:], perm_refs):
            tot = tot + _load_pattern(x_ref, perm_ref, d)
        for j in range(w // 128):
            x = tot[:, j * 128:(j + 1) * 128]
            if tabs is not None:
                x = x * tab_refs[0][...] + pltpu.roll(x, half, 1) * tab_refs[1][...] + pltpu.roll(x, 128 - half, 1) * tab_refs[2][...]
            refs[-1][:, j * 128:(j + 1) * 128] = x.astype(refs[-1].dtype)

    blk = pl.BlockSpec((tm, w), lambda i: (i, 0))
    tab = pl.BlockSpec((tm, 128), lambda i: (i, 0))
    pm = pl.BlockSpec((tm, tm), lambda i: (0, 0))
    extra = [] if tabs is None else list(tabs)
    out_shape, col0, more, more_specs, alias = _slab(s, w, BF16, slab, 2 * nd + 1 + len(extra))
    cb = col0 // w
    args = [ds_[0]] + [x.reshape(d, s // d, w) for d, x in zip(DILATIONS[1:], ds_[1:])] + [perms[d][1] for d in DILATIONS[1:]]
    return pl.pallas_call(
        body, grid=(s // tm,),
        in_specs=[blk] + [_pattern_spec(d, w) for d in DILATIONS[1:]] + [pm] * nd + [tab] * len(extra) + more_specs,
        out_specs=pl.BlockSpec((tm, w), lambda i: (i, cb)), out_shape=out_shape, input_output_aliases=alias,
        name=name, compiler_params=_cp("parallel"))(*args, *extra, *more)


def _band_valid(t, nq, nk, qofs, kofs, seq_len):
    qpos = t * 128 + qofs + lax.broadcasted_iota(jnp.int32, (nq, nk), 0)
    kpos = t * 128 + kofs + lax.broadcasted_iota(jnp.int32, (nq, nk), 1)
    sh = int(math.log2(seq_len))
    same = lax.shift_right_arithmetic(qpos, sh) == lax.shift_right_arithmetic(kpos, sh)
    return same & (jnp.abs(kpos - qpos) <= BAND_HALF)


def _window(r0, r1, r2):
    return jnp.concatenate([r0[128 - BAND_HALF:128], r1[...], r2[0:BAND_HALF]], axis=0)


WIN = 128 + 2 * BAND_HALF


def _first_head_lanes():
    return lax.broadcasted_iota(jnp.int32, (1, 2 * HEAD_DIM), 1) < HEAD_DIM


def _only(x, keep):
    return jnp.where(keep, x, jnp.zeros((), x.dtype))


def _win_specs(width, col, nt):
    return [pl.BlockSpec((128, width), lambda t: (jnp.maximum(t - 1, 0), col)),
            pl.BlockSpec((128, width), lambda t: (t, col)),
            pl.BlockSpec((128, width), lambda t: (jnp.minimum(t + 1, nt - 1), col))]


def _attn_fwd(qk, v_src, v_col, seq_len, name):
    s = qk.shape[0]
    nt = s // 128
    dm = D_MODEL

    def body(q_ref, k0, k1, k2, v0, v1, v2, o_ref, lse_ref):
        t = pl.program_id(0)
        valid = _band_valid(t, 128, WIN, 0, -BAND_HALF, seq_len)
        q = q_ref[...]
        kc = _window(k0, k1, k2)
        vc = _window(v0, v1, v2)
        first = _first_head_lanes()
        outs, lses = [], []
        for pr in range(N_HEADS // 2):
            ps = slice(pr * 128, (pr + 1) * 128)
            qp, kp, vp = q[:, ps], kc[:, ps], vc[:, ps]
            halves = []
            for keep in (first, ~first):
                sc = lax.dot_general(_only(qp, keep), kp, NT, preferred_element_type=F32)
                sc = jnp.where(valid, sc, NEG)
                m = jnp.max(sc, axis=1, keepdims=True)
                e = jnp.exp(sc - m)
                den = jnp.sum(e, axis=1, keepdims=True)
                halves.append(jnp.dot(e.astype(BF16), vp, preferred_element_type=F32) / den)
                lses.append(m + jnp.log(den))
            outs.append(jnp.where(first, halves[0], halves[1]))
        o_ref[...] = jnp.concatenate(outs, axis=1).astype(o_ref.dtype)
        lse_ref[...] = jnp.concatenate(lses, axis=1)

    in_specs = [pl.BlockSpec((128, dm), lambda t: (t, 0))] + _win_specs(dm, 1, nt) + _win_specs(dm, v_col, nt)
    return pl.pallas_call(
        body, grid=(nt,), in_specs=in_specs,
        out_specs=[pl.BlockSpec((128, dm), lambda t: (t, 0)), pl.BlockSpec((128, N_HEADS), lambda t: (t, 0))],
        out_shape=[SDS((s, dm), BF16), SDS((s, N_HEADS), F32)], name=name, compiler_params=_cp("parallel"))(
            qk, qk, qk, qk, v_src, v_src, v_src)


def _attn_combine(os_, lses, perms, expand, name):
    s, dm = os_[0].shape
    tm = PERM_TILE
    nd = len(DILATIONS) - 1

    def body(o1, o2, o3, l1, l2, l3, p2, p3, e_ref, out_ref, out_t_ref, lt_ref):
        ls = [l1[...], l2[...], l3[...]]
        m = jnp.maximum(jnp.maximum(ls[0], ls[1]), ls[2])
        es = [jnp.exp(l - m) for l in ls]
        tot = es[0] + es[1] + es[2]
        lt_ref[...] = m + jnp.log(tot)
        ovs = [o1[...].astype(F32), _load_pattern(o2, p2, DILATIONS[1]), _load_pattern(o3, p3, DILATIONS[2])]
        acc = jnp.zeros((tm, dm), F32)
        for e, o in zip(es, ovs):
            acc = acc + _dot_01(e / tot, e_ref[...]) * o
        out = acc.astype(out_ref.dtype)
        out_ref[...] = out
        out_t_ref[...] = out.T

    row = pl.BlockSpec((tm, dm), lambda i: (i, 0))
    st = pl.BlockSpec((tm, N_HEADS), lambda i: (i, 0))
    pm = pl.BlockSpec((tm, tm), lambda i: (0, 0))
    args = [os_[0]] + [o.reshape(d, s // d, dm) for d, o in zip(DILATIONS[1:], os_[1:])]
    return pl.pallas_call(
        body, grid=(s // tm,),
        in_specs=[row] + [_pattern_spec(d, dm) for d in DILATIONS[1:]] + [st, st, st, pm, pm, pl.BlockSpec((N_HEADS, dm), lambda i: (0, 0))],
        out_specs=[row, pl.BlockSpec((dm, tm), lambda i: (0, i)), st],
        out_shape=[SDS((s, dm + D_INNER), BF16), SDS((dm + D_INNER, s), BF16), SDS((s, N_HEADS), F32)],
        name=name, compiler_params=_cp("parallel"))(*args, *lses, *[perms[d][1] for d in DILATIONS[1:]], expand)


def _attn_delta(dmix, attn, expand_t, perms, name):
    s, dm = attn.shape[0], D_MODEL
    tm = PERM_TILE
    nd = len(DILATIONS) - 1

    def body(d_ref, a_ref, e_ref, *rest):
        perm_refs, o_ref, op_refs = rest[:nd], rest[nd], rest[nd + 1:]
        dv = d_ref[...]
        prod = dv.astype(F32) * a_ref[...].astype(F32)
        o_ref[...] = _dot_01(prod, e_ref[...])
        for d, perm_ref, op_ref in zip(DILATIONS[1:], perm_refs, op_refs):
            _store_pattern(op_ref, dv, perm_ref, d)

    row = pl.BlockSpec((tm, dm), lambda i: (i, 0))
    pm = pl.BlockSpec((tm, tm), lambda i: (0, 0))
    outs = pl.pallas_call(
        body, grid=(s // tm,), in_specs=[row, row, pl.BlockSpec((dm, N_HEADS), lambda i: (0, 0))] + [pm] * nd,
        out_specs=[pl.BlockSpec((tm, N_HEADS), lambda i: (i, 0))] + [_pattern_spec(d, dm) for d in DILATIONS[1:]],
        out_shape=[SDS((s, N_HEADS), F32)] + [SDS((d, s // d, dm), BF16) for d in DILATIONS[1:]],
        name=name, compiler_params=_cp("parallel"))(dmix, attn, expand_t, *[perms[d][0] for d in DILATIONS[1:]])
    return outs[0], [o.reshape(s, dm) for o in outs[1:]]


def _attn_bwd_dq(qk, v_src, v_col, do_src, lse, delta, seq_len, name):
    s = qk.shape[0]
    nt = s // 128
    dm = D_MODEL

    def body(q_ref, k0, k1, k2, v0, v1, v2, do_ref, lse_ref, dl_ref, dq_ref):
        t = pl.program_id(0)
        valid = _band_valid(t, 128, WIN, 0, -BAND_HALF, seq_len)
        q = q_ref[...]
        do = do_ref[...]
        kc = _window(k0, k1, k2)
        vc = _window(v0, v1, v2)
        lse_v, dl_v = lse_ref[...], dl_ref[...]
        first = _first_head_lanes()
        outs = []
        for pr in range(N_HEADS // 2):
            ps = slice(pr * 128, (pr + 1) * 128)
            qp, dop, kp, vp = q[:, ps], do[:, ps], kc[:, ps], vc[:, ps]
            halves = []
            for i, keep in enumerate((first, ~first)):
                h = 2 * pr + i
                sc = lax.dot_general(_only(qp, keep), kp, NT, preferred_element_type=F32)
                p = jnp.exp(jnp.where(valid, sc - lse_v[:, h:h + 1], NEG))
                dp = lax.dot_general(_only(dop, keep), vp, NT, preferred_element_type=F32)
                ds = p * (dp - dl_v[:, h:h + 1])
                halves.append(jnp.dot(ds.astype(BF16), kp, preferred_element_type=F32))
            outs.append(jnp.where(first, halves[0], halves[1]) * (HEAD_DIM ** -0.5))
        dq_ref[...] = jnp.concatenate(outs, axis=1).astype(dq_ref.dtype)

    row = pl.BlockSpec((128, dm), lambda t: (t, 0))
    st = pl.BlockSpec((128, N_HEADS), lambda t: (t, 0))
    in_specs = [row] + _win_specs(dm, 1, nt) + _win_specs(dm, v_col, nt) + [row, st, st]
    return pl.pallas_call(
        body, grid=(nt,), in_specs=in_specs, out_specs=row, out_shape=SDS((s, dm), BF16),
        name=name, compiler_params=_cp("parallel"))(qk, qk, qk, qk, v_src, v_src, v_src, do_src, lse, delta)


def _attn_bwd_dkv(qk, v_src, v_col, do_src, lse_t, delta_t, seq_len, name):
    s = qk.shape[0]
    nt = s // 128
    dm = D_MODEL

    def lane_window(r0, r1, r2):
        return jnp.concatenate([r0[:, 128 - BAND_HALF:128], r1[...], r2[:, 0:BAND_HALF]], axis=1)

    def body(k_ref, v_ref, q0, q1, q2, d0, d1, d2, l0, l1, l2, e0, e1, e2, dk_ref, dv_ref):
        t = pl.program_id(0)
        valid = _band_valid(t, 128, WIN, 0, -BAND_HALF, seq_len)
        k = k_ref[...]
        v = v_ref[...]
        qc = _window(q0, q1, q2)
        dc = _window(d0, d1, d2)
        lse_v = lane_window(l0, l1, l2)
        dl_v = lane_window(e0, e1, e2)
        first = _first_head_lanes()
        dks, dvs = [], []
        for pr in range(N_HEADS // 2):
            ps = slice(pr * 128, (pr + 1) * 128)
            kp, vp, qp, dop = k[:, ps], v[:, ps], qc[:, ps], dc[:, ps]
            dk_h, dv_h = [], []
            for i, keep in enumerate((first, ~first)):
                h = 2 * pr + i
                sc = lax.dot_general(_only(kp, keep), qp, NT, preferred_element_type=F32)
                p = jnp.exp(jnp.where(valid, sc - lse_v[h:h + 1, :], NEG))
                dv_h.append(jnp.dot(p.astype(BF16), dop, preferred_element_type=F32))
                dp = lax.dot_general(_only(vp, keep), dop, NT, preferred_element_type=F32)
                ds = p * (dp - dl_v[h:h + 1, :])
                dk_h.append(jnp.dot(ds.astype(BF16), qp, preferred_element_type=F32))
            dks.append(jnp.where(first, dk_h[0], dk_h[1]))
            dvs.append(jnp.where(first, dv_h[0], dv_h[1]))
        dk_ref[...] = jnp.concatenate(dks, axis=1).astype(dk_ref.dtype)
        dv_ref[...] = jnp.concatenate(dvs, axis=1).astype(dv_ref.dtype)

    row = pl.BlockSpec((128, dm), lambda t: (t, 0))
    stat = [pl.BlockSpec((N_HEADS, 128), lambda t: (0, jnp.maximum(t - 1, 0))), pl.BlockSpec((N_HEADS, 128), lambda t: (0, t)),
            pl.BlockSpec((N_HEADS, 128), lambda t: (0, jnp.minimum(t + 1, nt - 1)))]
    in_specs = ([pl.BlockSpec((128, dm), lambda t: (t, 1)), pl.BlockSpec((128, dm), lambda t: (t, v_col))]
                + _win_specs(dm, 0, nt) + _win_specs(dm, 0, nt) + stat + stat)
    return pl.pallas_call(
        body, grid=(nt,), in_specs=in_specs, out_specs=[row, row], out_shape=[SDS((s, dm), BF16), SDS((s, dm), BF16)],
        name=name, compiler_params=_cp("parallel"))(qk, v_src, qk, qk, qk, do_src, do_src, do_src, lse_t, lse_t, lse_t, delta_t, delta_t, delta_t)


CONV_COLS = (1024, 1408, 512, 256)


def _halo_specs(tm, tc, col0, nrow_blocks):
    r = tm // 16
    return [pl.BlockSpec((16, tc), lambda i, j: (jnp.maximum(i * r - 1, 0), col0 + j)),
            pl.BlockSpec((16, tc), lambda i, j: (jnp.minimum((i + 1) * r, nrow_blocks * r - 1), col0 + j))]


def _shifted(x_ref, hp_ref, hn_ref, i, last):
    x = x_ref[...].astype(F32)
    tm = x.shape[0]
    rows = lax.broadcasted_iota(jnp.int32, x.shape, 0)
    prev_row = jnp.where(i > 0, hp_ref[15:16, :].astype(F32), 0.0)
    next_row = jnp.where(i < last, hn_ref[0:1, :].astype(F32), 0.0)
    xp = jnp.where(rows == 0, prev_row, pltpu.roll(x, 1, 0))
    xn = jnp.where(rows == tm - 1, next_row, pltpu.roll(x, tm - 1, 0))
    return xp, x, xn


def _conv(x_src, col0, width, w3, bias, act, name, out_dtype=BF16, slab=None, transpose=False, wcol0=0):
    s = x_src.shape[0]
    tm = _pick(s, (256, 128))
    tc = _pick(width, CONV_COLS)
    nb = s // tm
    c0 = col0 // tc
    wc0 = wcol0 // tc

    def body(*refs):
        x_ref, hp_ref, hn_ref, w_ref, b_ref = refs[:5]
        o_ref = refs[-1]
        i = pl.program_id(0)
        xp, x, xn = _shifted(x_ref, hp_ref, hn_ref, i, nb - 1)
        w = w_ref[...]
        if transpose:
            y = w[2:3, :] * xp + w[1:2, :] * x + w[0:1, :] * xn
        else:
            y = w[0:1, :] * xp + w[1:2, :] * x + w[2:3, :] * xn + b_ref[...]
        if act:
            y = y * _sigmoid(y)
        o_ref[...] = y.astype(o_ref.dtype)

    in_specs = ([pl.BlockSpec((tm, tc), lambda i, j: (i, c0 + j))] + _halo_specs(tm, tc, c0, nb)
                + [pl.BlockSpec((3, tc), lambda i, j: (0, wc0 + j)), pl.BlockSpec((1, tc), lambda i, j: (0, wc0 + j))])
    out_shape, ocol, more, more_specs, alias = _slab(s, width, out_dtype, slab, 5)
    ob = ocol // tc
    return pl.pallas_call(
        body, grid=(nb, width // tc), in_specs=in_specs + more_specs, out_specs=pl.BlockSpec((tm, tc), lambda i, j: (i, ob + j)),
        out_shape=out_shape, input_output_aliases=alias, name=name, compiler_params=_cp("parallel", "parallel"))(
            x_src, x_src, x_src, w3, bias, *more)


def _conv_silu_bwd(x_src, col0, width, w3, bias, addends, add_widths, name):
    s = x_src.shape[0]
    tm = _pick(s, (256, 128))
    tc = _pick(width, CONV_COLS)
    nb = s // tm
    c0 = col0 // tc
    na = len(addends)

    def body(*refs):
        x_ref, hp_ref, hn_ref, w_ref, b_ref = refs[:5]
        a_refs = refs[5:5 + na]
        dp_ref, dw_ref, db_ref = refs[5 + na:]
        i, j = pl.program_id(1), pl.program_id(0)
        xp, x, xn = _shifted(x_ref, hp_ref, hn_ref, i, nb - 1)
        w = w_ref[...]
        pre = w[0:1, :] * xp + w[1:2, :] * x + w[2:3, :] * xn + b_ref[...]
        g = jnp.zeros_like(pre)
        for a_ref, aw in zip(a_refs, add_widths):
            av = a_ref[...].astype(F32)
            g = g + (av if aw == width else jnp.where(j < aw // tc, av, 0.0))
        sg = _sigmoid(pre)
        dpre = g * (sg * (1.0 + pre * (1.0 - sg)))
        dp_ref[...] = dpre.astype(dp_ref.dtype)

        @pl.when(i == 0)
        def _():
            dw_ref[...] = jnp.zeros_like(dw_ref)
            db_ref[...] = jnp.zeros_like(db_ref)

        dw_ref[...] += jnp.concatenate([jnp.sum(dpre * xp, axis=0, keepdims=True), jnp.sum(dpre * x, axis=0, keepdims=True),
                                        jnp.sum(dpre * xn, axis=0, keepdims=True)], axis=0)
        db_ref[...] += jnp.sum(dpre, axis=0, keepdims=True)

    r = tm // 16
    in_specs = [pl.BlockSpec((tm, tc), lambda j, i: (i, c0 + j)),
                pl.BlockSpec((16, tc), lambda j, i: (jnp.maximum(i * r - 1, 0), c0 + j)),
                pl.BlockSpec((16, tc), lambda j, i: (jnp.minimum((i + 1) * r, nb * r - 1), c0 + j)),
                pl.BlockSpec((3, tc), lambda j, i: (0, j)), pl.BlockSpec((1, tc), lambda j, i: (0, j))]
    for aw in add_widths:
        nblk = aw // tc
        in_specs.append(pl.BlockSpec((tm, tc), lambda j, i, nblk=nblk: (i, jnp.minimum(j, nblk - 1))))
    return pl.pallas_call(
        body, grid=(width // tc, nb), in_specs=in_specs,
        out_specs=[pl.BlockSpec((tm, tc), lambda j, i: (i, j)), pl.BlockSpec((3, tc), lambda j, i: (0, j)), pl.BlockSpec((1, tc), lambda j, i: (0, j))],
        out_shape=[SDS((s, width), BF16), SDS((3, width), F32), SDS((1, width), F32)],
        name=name, compiler_params=_cp("parallel", "arbitrary"))(x_src, x_src, x_src, w3, bias, *addends)


def _ffn_gate_fwd(u, w3, bias, name):
    s = u.shape[0]
    tm = _pick(s, (256, 128))
    tc = _pick(D_FF, CONV_COLS)
    nb = s // tm
    nj = D_FF // tc

    def body(g_ref, gp, gn, u_ref, up, un, wg_ref, wu_ref, bg_ref, bu_ref, o_ref, ot_ref):
        i = pl.program_id(0)
        outs = []
        for (x_ref, hp, hn, w_ref, b_ref) in ((g_ref, gp, gn, wg_ref, bg_ref), (u_ref, up, un, wu_ref, bu_ref)):
            xp, x, xn = _shifted(x_ref, hp, hn, i, nb - 1)
            w = w_ref[...]
            outs.append(w[0:1, :] * xp + w[1:2, :] * x + w[2:3, :] * xn + b_ref[...])
        gate, upv = outs
        out = (gate * _sigmoid(gate) * upv).astype(o_ref.dtype)
        o_ref[...] = out
        ot_ref[...] = out.T

    def xspecs(c0):
        return [pl.BlockSpec((tm, tc), lambda i, j: (i, c0 + j))] + _halo_specs(tm, tc, c0, nb)

    in_specs = (xspecs(0) + xspecs(nj)
                + [pl.BlockSpec((3, tc), lambda i, j: (0, j)), pl.BlockSpec((3, tc), lambda i, j: (0, nj + j)),
                   pl.BlockSpec((1, tc), lambda i, j: (0, j)), pl.BlockSpec((1, tc), lambda i, j: (0, nj + j))])
    return pl.pallas_call(
        body, grid=(nb, nj), in_specs=in_specs,
        out_specs=[pl.BlockSpec((tm, tc), lambda i, j: (i, j)), pl.BlockSpec((tc, tm), lambda i, j: (j, i))],
        out_shape=[SDS((s, D_FF), BF16), SDS((D_FF, s), BF16)], name=name, compiler_params=_cp("parallel", "parallel"))(
            u, u, u, u, u, u, w3, w3, bias, bias)


def _ffn_gate_bwd(u, w3, bias, dact, name):
    s = u.shape[0]
    tm = _pick(s, (256, 128))
    tc = _pick(D_FF, CONV_COLS)
    nb = s // tm
    nj = D_FF // tc

    def body(g_ref, gp, gn, u_ref, up, un, wg_ref, wu_ref, bg_ref, bu_ref, da_ref, dg_ref, du_ref, dwg_ref, dwu_ref, dbg_ref, dbu_ref):
        i = pl.program_id(1)
        sh, pre = [], []
        for (x_ref, hp, hn, w_ref, b_ref) in ((g_ref, gp, gn, wg_ref, bg_ref), (u_ref, up, un, wu_ref, bu_ref)):
            xs3 = _shifted(x_ref, hp, hn, i, nb - 1)
            w = w_ref[...]
            sh.append(xs3)
            pre.append(w[0:1, :] * xs3[0] + w[1:2, :] * xs3[1] + w[2:3, :] * xs3[2] + b_ref[...])
        gate, upv = pre
        da = da_ref[...].astype(F32)
        sg = _sigmoid(gate)
        dgate = da * upv * (sg * (1.0 + gate * (1.0 - sg)))
        dup = da * gate * sg
        dg_ref[...] = dgate.astype(dg_ref.dtype)
        du_ref[...] = dup.astype(du_ref.dtype)

        @pl.when(i == 0)
        def _():
            for r in (dwg_ref, dwu_ref, dbg_ref, dbu_ref):
                r[...] = jnp.zeros_like(r)

        for d, xs3, dw_ref, db_ref in ((dgate, sh[0], dwg_ref, dbg_ref), (dup, sh[1], dwu_ref, dbu_ref)):
            dw_ref[...] += jnp.concatenate([jnp.sum(d * xs3[0], axis=0, keepdims=True), jnp.sum(d * xs3[1], axis=0, keepdims=True),
                                            jnp.sum(d * xs3[2], axis=0, keepdims=True)], axis=0)
            db_ref[...] += jnp.sum(d, axis=0, keepdims=True)

    r = tm // 16

    def xspecs(c0):
        return [pl.BlockSpec((tm, tc), lambda j, i: (i, c0 + j)),
                pl.BlockSpec((16, tc), lambda j, i: (jnp.maximum(i * r - 1, 0), c0 + j)),
                pl.BlockSpec((16, tc), lambda j, i: (jnp.minimum((i + 1) * r, nb * r - 1), c0 + j))]

    in_specs = (xspecs(0) + xspecs(nj)
                + [pl.BlockSpec((3, tc), lambda j, i: (0, j)), pl.BlockSpec((3, tc), lambda j, i: (0, nj + j)),
                   pl.BlockSpec((1, tc), lambda j, i: (0, j)), pl.BlockSpec((1, tc), lambda j, i: (0, nj + j)),
                   pl.BlockSpec((tm, tc), lambda j, i: (i, j))])
    blk = pl.BlockSpec((tm, tc), lambda j, i: (i, j))
    w_o = pl.BlockSpec((3, tc), lambda j, i: (0, j))
    b_o = pl.BlockSpec((1, tc), lambda j, i: (0, j))
    return pl.pallas_call(
        body, grid=(nj, nb), in_specs=in_specs, out_specs=[blk, blk, w_o, w_o, b_o, b_o],
        out_shape=[SDS((s, D_FF), BF16), SDS((s, D_FF), BF16), SDS((3, D_FF), F32), SDS((3, D_FF), F32), SDS((1, D_FF), F32), SDS((1, D_FF), F32)],
        name=name, compiler_params=_cp("parallel", "arbitrary"))(u, u, u, u, u, u, w3, w3, bias, bias, dact)


def _exchange_sems(n):
    return [pltpu.SemaphoreType.DMA((7 * n,)), pltpu.SemaphoreType.DMA((7 * n,)), pltpu.SemaphoreType.DMA((n,))]


def _exchange_copies(srcs, outs, send_sems, recv_sems, local_sems):
    x, y, c = lax.axis_index("x"), lax.axis_index("y"), lax.axis_index("c")
    me = 4 * x + 2 * y + c
    locals_ = [pltpu.make_async_copy(srcs[a].at[me], outs[a].at[me], local_sems.at[a]) for a in range(len(srcs))]
    sends, recvs = [], []
    for k in range(1, N_DEV):
        px, py, pc = x ^ ((k >> 2) & 1), y ^ ((k >> 1) & 1), c ^ (k & 1)
        peer = 4 * px + 2 * py + pc
        for a in range(len(srcs)):
            sems = dict(send_sem=send_sems.at[a * 7 + k - 1], recv_sem=recv_sems.at[a * 7 + k - 1], device_id_type=MESH)
            sends.append(pltpu.make_async_remote_copy(src_ref=srcs[a].at[peer], dst_ref=outs[a].at[me], device_id=(px, py, pc), **sems))
            recvs.append(pltpu.make_async_remote_copy(src_ref=srcs[a].at[peer], dst_ref=outs[a].at[peer], device_id=(x, y, c), **sems))
    return locals_, sends, recvs


def _exchange_start(srcs, outs, send_sems, recv_sems, local_sems):
    locals_, sends, _ = _exchange_copies(srcs, outs, send_sems, recv_sems, local_sems)
    for cp in locals_ + sends:
        cp.start()


def _exchange_wait(srcs, outs, send_sems, recv_sems, local_sems):
    locals_, sends, recvs = _exchange_copies(srcs, outs, send_sems, recv_sems, local_sems)
    for cp in recvs:
        cp.wait_recv()
    for cp in sends:
        cp.wait_send()
    for cp in locals_:
        cp.wait()


def _gather_copies(srcs, outs, send_sems, recv_sems, local_sems):
    x, y, c = lax.axis_index("x"), lax.axis_index("y"), lax.axis_index("c")
    me, sibling = (x, y, c), (x, y, 1 - c)
    chips = [(1 - x, y), (x, 1 - y), (1 - x, 1 - y)]

    def copy(a, k, block, to, src=None):
        dst = outs[a].at[4 * block[0] + 2 * block[1] + block[2]]
        return pltpu.make_async_remote_copy(
            src_ref=dst if src is None else src, dst_ref=dst,
            send_sem=send_sems.at[a * 7 + k], recv_sem=recv_sems.at[a * 7 + k], device_id=to, device_id_type=MESH)

    n = len(srcs)
    locals_ = [pltpu.make_async_copy(srcs[a], outs[a].at[4 * x + 2 * y + c], local_sems.at[a]) for a in range(n)]
    own = [copy(a, 0, me, sibling, src=srcs[a]) for a in range(n)]
    own += [copy(a, 1 + j, me, (*chip, c), src=srcs[a]) for a in range(n) for j, chip in enumerate(chips)]
    landed_ici = [copy(a, 1 + j, (*chip, c), me) for j, chip in enumerate(chips) for a in range(n)]
    passed = [copy(a, 4 + j, (*chip, c), sibling) for j, chip in enumerate(chips) for a in range(n)]
    landed_d2d = [copy(a, 0, sibling, me) for a in range(n)]
    landed_d2d += [copy(a, 4 + j, (*chip, 1 - c), me) for a in range(n) for j, chip in enumerate(chips)]
    return locals_, own, landed_ici, passed, landed_d2d


def _gather_start(*refs):
    locals_, own, _, _, _ = _gather_copies(*refs)
    for cp in locals_ + own:
        cp.start()


def _gather_forward(*refs):
    _, _, landed_ici, passed, _ = _gather_copies(*refs)
    for arrived, onward in zip(landed_ici, passed):
        arrived.wait_recv()
        onward.start()


def _gather_finish(*refs):
    locals_, own, _, passed, landed_d2d = _gather_copies(*refs)
    for cp in landed_d2d:
        cp.wait_recv()
    for cp in own + passed:
        cp.wait_send()
    for cp in locals_:
        cp.wait()


def _ssd_common(dt_ref, dtt_ref, al_r, al_c, bi_r, bi_c, off, rev):
    li = lax.broadcasted_iota(jnp.int32, (CHUNK, CHUNK), 0)
    si = lax.broadcasted_iota(jnp.int32, (CHUNK, CHUNK), 1)
    mask = (li <= si) if rev else (li >= si)
    mask_t = (li >= si) if rev else (li <= si)
    a_r = -jnp.exp(al_r[...])
    a_c = -jnp.exp(al_c[...])
    pre = dt_ref[:, off:off + N_HEADS] + bi_r[...]
    dt = _softplus(pre)
    cs = jnp.dot(mask.astype(F32), dt * a_r, precision=HIGH, preferred_element_type=F32)
    dt_t = _softplus(dtt_ref[off:off + N_HEADS, :] + bi_c[...])
    cs_t = jnp.dot(dt_t * a_c, mask_t.astype(F32), precision=HIGH, preferred_element_type=F32)
    tot = cs[0:1, :] if rev else cs[CHUNK - 1:CHUNK, :]
    return mask, mask_t, a_r, pre, dt, cs, cs_t, tot


def _ssd_fwd(xbc, dt_raw, dt_t, args_f, args_b, name, gather=()):
    s = xbc.shape[0]
    nc = s // CHUNK
    hp = D_INNER // N_HEADS
    hpg = N_HEADS // N_GROUPS
    gs = N_GROUPS * D_STATE
    ng = len(gather)
    fwd_step = (nc * 27) // 32

    def chunk(x_ref, b_ref, c_ref, dt_ref, dtt_ref, alr, alc, bir, bic, y_ref, st_ref, h_scr, off, rev):
        mask, _, _, _, dt, cs, cs_t, tot = _ssd_common(dt_ref, dtt_ref, alr, alc, bir, bic, off, rev)
        first = _first_head_lanes()
        xs = x_ref[...]
        ys = []
        for g in range(N_GROUPS):
            bg = b_ref[:, g * D_STATE:(g + 1) * D_STATE]
            cg = c_ref[:, g * D_STATE:(g + 1) * D_STATE]
            gm = lax.dot_general(cg, bg, NT, preferred_element_type=F32)
            hcat = h_scr[g]
            st_ref[0, g] = hcat
            ch = lax.dot_general(cg, hcat.astype(BF16), NT, preferred_element_type=F32)
            xdd = []
            for pr in range(hpg // 2):
                h0 = g * hpg + 2 * pr
                lanes = slice(h0 * hp, (h0 + 2) * hp)
                cols = [cs[:, h:h + 1] for h in (h0, h0 + 1)]
                pair = lambda a, b: jnp.where(first, a, b)
                xdf = xs[:, lanes].astype(F32) * pair(dt[:, h0:h0 + 1], dt[:, h0 + 1:h0 + 2])
                xdb = xdf.astype(BF16)
                yh = []
                for i, h in enumerate((h0, h0 + 1)):
                    lm = jnp.exp(jnp.where(mask, cols[i] - cs_t[h:h + 1, :], NEG))
                    yh.append(jnp.dot((gm * lm).astype(BF16), xdb, preferred_element_type=F32))
                ecs = pair(jnp.exp(cols[0]), jnp.exp(cols[1]))
                ys.append(pair(yh[0], yh[1]) + ecs * ch[:, 2 * pr * hp:(2 * pr + 2) * hp])
                dec = pair(jnp.exp(tot[:, h0:h0 + 1] - cols[0]), jnp.exp(tot[:, h0 + 1:h0 + 2] - cols[1]))
                xdd.append((xdf * dec).astype(BF16))
            snew = lax.dot_general(jnp.concatenate(xdd, axis=1), bg, TN, preferred_element_type=F32)
            for r in range(hpg):
                rs = slice(r * hp, (r + 1) * hp)
                h_scr[g, rs, :] = jnp.exp(tot[:, g * hpg + r:g * hpg + r + 1]) * hcat[rs] + snew[rs]
        y_ref[...] = jnp.concatenate(ys, axis=1).astype(y_ref.dtype)

    def body(*refs):
        in_f, in_b = refs[0:9], refs[9:18]
        g_src = refs[18:18 + ng]
        out_f, out_b = refs[18 + ng:20 + ng], refs[20 + ng:22 + ng]
        g_dst = refs[22 + ng:22 + 2 * ng]
        hs_f, hs_b = refs[22 + 2 * ng], refs[23 + 2 * ng]
        g_sems = refs[24 + 2 * ng:]
        step = pl.program_id(0)

        @pl.when(step == 0)
        def _():
            hs_f[...] = jnp.zeros_like(hs_f)
            hs_b[...] = jnp.zeros_like(hs_b)
            if ng:
                _gather_start(g_src, g_dst, *g_sems)

        chunk(*in_f, *out_f, hs_f, 0, False)
        chunk(*in_b, *out_b, hs_b, N_HEADS, True)

        if ng:
            @pl.when(step == fwd_step)
            def _():
                _gather_forward(g_src, g_dst, *g_sems)

            @pl.when(step == nc - 1)
            def _():
                _gather_finish(g_src, g_dst, *g_sems)

    small = lambda shape: pl.BlockSpec(shape, lambda c: (0, 0))

    def specs(cm):
        ins = [pl.BlockSpec((CHUNK, D_INNER), lambda c: (cm(c), 0)),
               pl.BlockSpec((CHUNK, gs), lambda c: (cm(c), D_INNER // gs)),
               pl.BlockSpec((CHUNK, gs), lambda c: (cm(c), D_INNER // gs + 1)),
               pl.BlockSpec((CHUNK, 128), lambda c: (cm(c), 0)),
               pl.BlockSpec((2 * N_HEADS, CHUNK), lambda c: (0, cm(c))),
               small((1, N_HEADS)), small((N_HEADS, 1)), small((1, N_HEADS)), small((N_HEADS, 1))]
        outs = [pl.BlockSpec((CHUNK, D_INNER), lambda c: (cm(c), 0)),
                pl.BlockSpec((1, N_GROUPS, hpg * hp, D_STATE), lambda c: (cm(c), 0, 0, 0))]
        return ins, outs

    ins_f, outs_f = specs(lambda c: c)
    ins_b, outs_b = specs(lambda c: nc - 1 - c)
    any_spec = pl.BlockSpec(memory_space=pl.ANY)
    one_dir = [SDS((s, D_INNER), BF16), SDS((nc, N_GROUPS, hpg * hp, D_STATE), F32)]
    state = pltpu.VMEM((N_GROUPS, hpg * hp, D_STATE), F32)
    return pl.pallas_call(
        body, grid=(nc,), in_specs=ins_f + ins_b + [any_spec] * ng, out_specs=outs_f + outs_b + [any_spec] * ng,
        out_shape=one_dir + one_dir + [SDS((N_DEV,) + g.shape, g.dtype) for g in gather],
        scratch_shapes=[state, state] + (_exchange_sems(ng) if ng else []), name=name, compiler_params=_cp("arbitrary"))(
            xbc, xbc, xbc, dt_raw, dt_t, *args_f, xbc, xbc, xbc, dt_raw, dt_t, *args_b, *gather)


def _ssd_bwd(xbc, dt_raw, dt_t, al_r, al_c, bi_r, bi_c, states, dy, off, rev, name, exchange=()):
    s = xbc.shape[0]
    nc = s // CHUNK
    hp = D_INNER // N_HEADS
    gs = N_GROUPS * D_STATE
    hpg = N_HEADS // N_GROUPS
    cm = (lambda c: c) if rev else (lambda c: nc - 1 - c)
    nx = len(exchange)

    def body(*refs):
        x_ref, b_ref, c_ref, dt_ref, dtt_ref, alr, alc, bir, bic, st_ref, dy_ref = refs[:11]
        xch_src = refs[11:11 + nx]
        dx_ref, ddt_ref, dal_ref, dbi_ref = refs[11 + nx:15 + nx]
        xch_dst = refs[15 + nx:15 + 2 * nx]
        dh_scr = refs[15 + 2 * nx]
        xch_sems = refs[16 + 2 * nx:]

        @pl.when(pl.program_id(0) == 0)
        def _():
            dh_scr[...] = jnp.zeros_like(dh_scr)
            dal_ref[...] = jnp.zeros_like(dal_ref)
            dbi_ref[...] = jnp.zeros_like(dbi_ref)
            if nx:
                _exchange_start(xch_src, xch_dst, *xch_sems)

        if nx:
            @pl.when(pl.program_id(0) == nc - 1)
            def _():
                _exchange_wait(xch_src, xch_dst, *xch_sems)

        mask, mask_t, a_r, pre, dt, cs, cs_t, tot = _ssd_common(dt_ref, dtt_ref, alr, alc, bir, bic, off, rev)
        xs = x_ref[...]
        dyv = dy_ref[...]
        rows = lax.broadcasted_iota(jnp.int32, (CHUNK, 1), 0)
        end_row = (rows == 0) if rev else (rows == CHUNK - 1)
        lane_h = lax.broadcasted_iota(jnp.int32, (1, N_HEADS), 1)
        sub_h = lax.broadcasted_iota(jnp.int32, (N_HEADS, 1), 0)
        first = _first_head_lanes()
        dcs_all = jnp.zeros((CHUNK, N_HEADS), F32)
        colw_all = jnp.zeros((N_HEADS, CHUNK), F32)
        dxsum_all = jnp.zeros((CHUNK, N_HEADS), F32)
        dxs, dbs, dcs_out = [], [], []
        for g in range(N_GROUPS):
            bg = b_ref[:, g * D_STATE:(g + 1) * D_STATE]
            cg = c_ref[:, g * D_STATE:(g + 1) * D_STATE]
            gm = lax.dot_general(cg, bg, NT, preferred_element_type=F32)
            gm_t = lax.dot_general(bg, cg, NT, preferred_element_type=F32)
            hcat = st_ref[0, g]
            dhcat = dh_scr[g]
            hb, dhb = hcat.astype(BF16), dhcat.astype(BF16)
            ch = lax.dot_general(cg, hb, NT, preferred_element_type=F32)
            z = lax.dot_general(bg, dhb, NT, preferred_element_type=F32)
            dg_sum = jnp.zeros((CHUNK, CHUNK), F32)
            dchs, xdds, t_hs = [], [], []
            for pr in range(hpg // 2):
                h0 = g * hpg + 2 * pr
                lanes = slice(h0 * hp, (h0 + 2) * hp)
                ps = slice(2 * pr * hp, (2 * pr + 2) * hp)
                pair = lambda a, b: jnp.where(first, a, b)
                cols = [cs[:, h:h + 1] for h in (h0, h0 + 1)]
                tots = [tot[:, h:h + 1] for h in (h0, h0 + 1)]
                xh = xs[:, lanes].astype(F32)
                dtc = pair(dt[:, h0:h0 + 1], dt[:, h0 + 1:h0 + 2])
                xdf = xh * dtc
                xd = xdf.astype(BF16)
                dyh = dyv[:, lanes]
                dyb = dyh.astype(BF16)
                ecs = pair(jnp.exp(cols[0]), jnp.exp(cols[1]))
                dec = pair(jnp.exp(tots[0] - cols[0]), jnp.exp(tots[1] - cols[1]))
                yoff_t = dyh * (ecs * ch[:, ps])
                dchs.append((ecs * dyh).astype(BF16))
                xdd = xdf * dec
                ddec_t = xdd * z[:, ps]
                row_terms = yoff_t - ddec_t
                xdds.append(xdd.astype(BF16))
                dxd_h = []
                for i, (h, keep) in enumerate(((h0, first), (h0 + 1, ~first))):
                    rs = slice((2 * pr + i) * hp, (2 * pr + i + 1) * hp)
                    lm = jnp.exp(jnp.where(mask, cols[i] - cs_t[h:h + 1, :], NEG))
                    m = gm * lm
                    t_h = jnp.exp(tots[i])
                    lm_t = jnp.exp(jnp.where(mask_t, cs_t[h:h + 1, :] - cols[i], NEG))
                    dxd_h.append(jnp.dot((gm_t * lm_t).astype(BF16), dyb, preferred_element_type=F32))
                    dm = lax.dot_general(_only(dyb, keep), xd, NT, preferred_element_type=F32)
                    dg_sum = dg_sum + dm * lm
                    w = dm * m
                    ddec_tot = jnp.sum(jnp.sum(_only(ddec_t, keep), axis=0, keepdims=True), axis=1, keepdims=True)
                    dtot = jnp.sum(jnp.sum(dhcat[rs] * hcat[rs], axis=0, keepdims=True), axis=1, keepdims=True) * t_h + ddec_tot
                    t_hs.append(t_h)
                    dcs = jnp.sum(_only(row_terms, keep) + w, axis=1, keepdims=True)
                    dcs = dcs + jnp.where(end_row, dtot, 0.0)
                    colw_all = colw_all + (sub_h == h).astype(F32) * jnp.sum(w, axis=0, keepdims=True)
                    dcs_all = dcs_all + dcs * (lane_h == h).astype(F32)
                dxd = pair(dxd_h[0], dxd_h[1]) + dec * z[:, ps]
                dxx = dxd * xh
                for i, (h, keep) in enumerate(((h0, first), (h0 + 1, ~first))):
                    dxsum_all = dxsum_all + jnp.sum(_only(dxx, keep), axis=1, keepdims=True) * (lane_h == h).astype(F32)
                dxs.append(dxd * dtc)
            dgs = dg_sum.astype(BF16)
            dchc = jnp.concatenate(dchs, axis=1)
            dcs_out.append(jnp.dot(dgs, bg, preferred_element_type=F32) + jnp.dot(dchc, hb, preferred_element_type=F32))
            dbs.append(lax.dot_general(dgs, cg, TN, preferred_element_type=F32)
                       + jnp.dot(jnp.concatenate(xdds, axis=1), dhb, preferred_element_type=F32))
            dh_in = lax.dot_general(dchc, cg, TN, preferred_element_type=F32)
            for r in range(hpg):
                rs = slice(r * hp, (r + 1) * hp)
                dh_scr[g, rs, :] = dh_in[rs] + t_hs[r] * dhcat[rs]
        dx_ref[...] = jnp.concatenate(dxs + dbs + dcs_out, axis=1).astype(dx_ref.dtype)
        mt = mask_t.astype(F32)
        da = (jnp.dot(mt, dcs_all, precision=HIGH, preferred_element_type=F32)
              - lax.dot_general(mt, colw_all, NT, precision=HIGH, preferred_element_type=F32))
        dal_ref[...] += jnp.sum(da * dt, axis=0, keepdims=True) * a_r
        ddt_raw = (da * a_r + dxsum_all) * _sigmoid(pre)
        ddt_ref[...] = ddt_raw
        dbi_ref[...] += jnp.sum(ddt_raw, axis=0, keepdims=True)

    small = lambda shape: pl.BlockSpec(shape, lambda c: (0, 0))
    in_specs = [pl.BlockSpec((CHUNK, D_INNER), lambda c: (cm(c), 0)),
                pl.BlockSpec((CHUNK, gs), lambda c: (cm(c), D_INNER // gs)),
                pl.BlockSpec((CHUNK, gs), lambda c: (cm(c), D_INNER // gs + 1)),
                pl.BlockSpec((CHUNK, 128), lambda c: (cm(c), 0)),
                pl.BlockSpec((2 * N_HEADS, CHUNK), lambda c: (0, cm(c))),
                small((1, N_HEADS)), small((N_HEADS, 1)), small((1, N_HEADS)), small((N_HEADS, 1)),
                pl.BlockSpec((1, N_GROUPS, hpg * hp, D_STATE), lambda c: (cm(c), 0, 0, 0)),
                pl.BlockSpec((CHUNK, D_INNER), lambda c: (cm(c), 0))]
    any_spec = pl.BlockSpec(memory_space=pl.ANY)
    return pl.pallas_call(
        body, grid=(nc,), in_specs=in_specs + [any_spec] * nx,
        out_specs=[pl.BlockSpec((CHUNK, XBC), lambda c: (cm(c), 0)), pl.BlockSpec((CHUNK, N_HEADS), lambda c: (cm(c), 0)),
                   small((1, N_HEADS)), small((1, N_HEADS))] + [any_spec] * nx,
        out_shape=[SDS((s, XBC), BF16), SDS((s, N_HEADS), F32), SDS((1, N_HEADS), F32), SDS((1, N_HEADS), F32)]
        + [SDS(a.shape, a.dtype) for a in exchange],
        scratch_shapes=[pltpu.VMEM((N_GROUPS, hpg * hp, D_STATE), F32)] + (_exchange_sems(nx) if nx else []),
        name=name, compiler_params=_cp("arbitrary"))(xbc, xbc, xbc, dt_raw, dt_t, al_r, al_c, bi_r, bi_c, states, dy, *exchange)


def _gate_fwd(yf, yb, xbc, proj, dskip_x, norm_w, mix, mix_t, name):
    s = yf.shape[0]
    tm = _pick(s, (256, 128))
    gw = D_INNER // N_GROUPS
    zc = 3 * D_MODEL // D_INNER

    def body(yf_ref, yb_ref, x_ref, z_ref, d_ref, w_ref, _m, _mt, o_ref, ot_ref):
        y = yf_ref[...].astype(F32) + yb_ref[...].astype(F32) + d_ref[...] * x_ref[...].astype(F32)
        z = z_ref[...].astype(F32)
        gt = y * (z * _sigmoid(z))
        outs = []
        for g in range(N_GROUPS):
            gg = gt[:, g * gw:(g + 1) * gw]
            outs.append(gg * lax.rsqrt(jnp.mean(gg * gg, axis=-1, keepdims=True) + EPS))
        out = (jnp.concatenate(outs, axis=1) * w_ref[...]).astype(o_ref.dtype)
        o_ref[...] = out
        ot_ref[...] = out.T

    row = pl.BlockSpec((tm, D_INNER), lambda i: (i, 0))
    vec = pl.BlockSpec((1, D_INNER), lambda i: (0, 0))
    any_spec = pl.BlockSpec(memory_space=pl.ANY)
    cb = (mix.shape[1] - D_INNER) // D_INNER
    return pl.pallas_call(
        body, grid=(s // tm,), in_specs=[row, row, row, pl.BlockSpec((tm, D_INNER), lambda i: (i, zc)), vec, vec, any_spec, any_spec],
        out_specs=[pl.BlockSpec((tm, D_INNER), lambda i: (i, cb)), pl.BlockSpec((D_INNER, tm), lambda i: (cb, i))],
        out_shape=[SDS(mix.shape, BF16), SDS(mix_t.shape, BF16)], input_output_aliases={6: 0, 7: 1},
        name=name, compiler_params=_cp("parallel"))(yf, yb, xbc, proj, dskip_x, norm_w, mix, mix_t)


def _gate_bwd(yf, yb, xbc, proj, dskip_x, norm_w, dmix, dproj, name):
    s = yf.shape[0]
    tm = _pick(s, (256, 128))
    gw = D_INNER // N_GROUPS
    zc = 3 * D_MODEL // D_INNER

    def body(yf_ref, yb_ref, x_ref, z_ref, d_ref, w_ref, do_ref, _, dy_ref, dz_ref, dxs_ref, dw_ref, dd_ref):
        xf = x_ref[...].astype(F32)
        y = yf_ref[...].astype(F32) + yb_ref[...].astype(F32) + d_ref[...] * xf
        z = z_ref[...].astype(F32)
        sg = _sigmoid(z)
        sz = z * sg
        gt = y * sz
        do = do_ref[...].astype(F32)
        dgh = do * w_ref[...]
        ghs, dgts = [], []
        for g in range(N_GROUPS):
            gg = gt[:, g * gw:(g + 1) * gw]
            r = lax.rsqrt(jnp.mean(gg * gg, axis=-1, keepdims=True) + EPS)
            gh = gg * r
            dg = dgh[:, g * gw:(g + 1) * gw]
            ghs.append(gh)
            dgts.append(r * (dg - gh * jnp.mean(dg * gh, axis=-1, keepdims=True)))
        ghat = jnp.concatenate(ghs, axis=1)
        dgt = jnp.concatenate(dgts, axis=1)
        dy = dgt * sz
        dy_ref[...] = dy
        dz_ref[...] = (dgt * y * (sg * (1.0 + z * (1.0 - sg)))).astype(dz_ref.dtype)
        dxs_ref[...] = (dy * d_ref[...]).astype(dxs_ref.dtype)

        @pl.when(pl.program_id(0) == 0)
        def _():
            dw_ref[...] = jnp.zeros_like(dw_ref)
            dd_ref[...] = jnp.zeros_like(dd_ref)

        dw_ref[...] += jnp.sum(do * ghat, axis=0, keepdims=True)
        dd_ref[...] += jnp.sum(dy * xf, axis=0, keepdims=True)

    row = pl.BlockSpec((tm, D_INNER), lambda i: (i, 0))
    vec = pl.BlockSpec((1, D_INNER), lambda i: (0, 0))
    dz_shape, _, more, more_specs, alias = _slab(s, D_INNER, BF16, (dproj.shape[1], zc * D_INNER, dproj), 7, out_idx=1)
    return pl.pallas_call(
        body, grid=(s // tm,),
        in_specs=[row, row, row, pl.BlockSpec((tm, D_INNER), lambda i: (i, zc)), vec, vec, pl.BlockSpec((tm, D_INNER), lambda i: (i, 1))] + more_specs,
        out_specs=[row, pl.BlockSpec((tm, D_INNER), lambda i: (i, zc)), row, vec, vec],
        out_shape=[SDS((s, D_INNER), F32), dz_shape, SDS((s, D_INNER), BF16), SDS((1, D_INNER), F32), SDS((1, D_INNER), F32)],
        input_output_aliases=alias, name=name, compiler_params=_cp("arbitrary"))(yf, yb, xbc, proj, dskip_x, norm_w, dmix, *more)


def _adamw(parts, w, m, v, name):
    r, c = w.shape
    tr = _pick(r, (256, 352, 128))

    def body(p_ref, w_ref, m_ref, v_ref, g_ref, d_ref, nm_ref, nv_ref):
        g = p_ref[0].astype(F32)
        for i in range(1, N_DEV):
            g = g + p_ref[i].astype(F32)
        mn = B1 * m_ref[...] + (1.0 - B1) * g
        vn = B2 * v_ref[...] + (1.0 - B2) * (g * g)
        m_hat = mn / (1.0 - B1 ** STEP)
        v_hat = vn / (1.0 - B2 ** STEP)
        g_ref[...] = g
        d_ref[...] = -LR * (m_hat / (jnp.sqrt(v_hat) + AEPS) + WD * w_ref[...])
        nm_ref[...] = mn
        nv_ref[...] = vn

    blk = pl.BlockSpec((tr, c), lambda i: (i, 0))
    return pl.pallas_call(
        body, grid=(r // tr,), in_specs=[pl.BlockSpec((N_DEV, tr, c), lambda i: (0, i, 0)), blk, blk, blk],
        out_specs=[blk, blk, blk, blk], out_shape=[SDS((r, c), F32)] * 4, name=name, compiler_params=_cp("parallel"))(parts, w, m, v)


def _sum_parts(parts, name):
    _, r, c = parts.shape

    def body(p_ref, o_ref):
        g = p_ref[0]
        for i in range(1, N_DEV):
            g = g + p_ref[i]
        o_ref[...] = g

    return pl.pallas_call(body, out_shape=SDS((r, c), F32), name=name)(parts)


def _adamw_small(gs, ws, ms, vs, name):
    n = len(gs)

    def body(*refs):
        g_refs, w_refs, m_refs, v_refs = refs[:n], refs[n:2 * n], refs[2 * n:3 * n], refs[3 * n:4 * n]
        d_refs, nm_refs, nv_refs = refs[4 * n:5 * n], refs[5 * n:6 * n], refs[6 * n:7 * n]
        for i in range(n):
            gv = g_refs[i][...]
            mn = B1 * m_refs[i][...] + (1.0 - B1) * gv
            vn = B2 * v_refs[i][...] + (1.0 - B2) * (gv * gv)
            m_hat = mn / (1.0 - B1 ** STEP)
            v_hat = vn / (1.0 - B2 ** STEP)
            d_refs[i][...] = -LR * (m_hat / (jnp.sqrt(v_hat) + AEPS) + WD * w_refs[i][...])
            nm_refs[i][...] = mn
            nv_refs[i][...] = vn

    outs = pl.pallas_call(body, out_shape=[SDS(g.shape, F32) for g in gs] * 3, name=name)(*gs, *ws, *ms, *vs)
    return outs[:n], outs[n:2 * n], outs[2 * n:]


def _my_index():
    return 4 * lax.axis_index("x") + 2 * lax.axis_index("y") + lax.axis_index("c")


def _all_gather(shards, name):
    n = len(shards)

    def body(*refs):
        srcs, outs = refs[:n], refs[n:2 * n]
        _gather_start(srcs, outs, *refs[2 * n:])
        _gather_forward(srcs, outs, *refs[2 * n:])
        _gather_finish(srcs, outs, *refs[2 * n:])

    any_spec = pl.BlockSpec(memory_space=pl.ANY)
    return pl.pallas_call(
        body, in_specs=[any_spec] * n, out_specs=[any_spec] * n,
        out_shape=[SDS((N_DEV,) + s.shape, s.dtype) for s in shards], scratch_shapes=_exchange_sems(n), name=name)(*shards)


def _exchange(arrays, name):
    n = len(arrays)

    def body(*refs):
        srcs, outs = refs[:n], refs[n:2 * n]
        _exchange_start(srcs, outs, *refs[2 * n:])
        _exchange_wait(srcs, outs, *refs[2 * n:])

    any_spec = pl.BlockSpec(memory_space=pl.ANY)
    return pl.pallas_call(
        body, in_specs=[any_spec] * n, out_specs=[any_spec] * n,
        out_shape=[SDS(a.shape, a.dtype) for a in arrays], scratch_shapes=_exchange_sems(n), name=name)(*arrays)


def _to_pattern(t, d):
    if d == 1:
        return t
    s, w = t.shape
    return t.reshape(s // d, d, w).transpose(1, 0, 2).reshape(s, w)


def _from_pattern(t, d):
    if d == 1:
        return t
    s, w = t.shape
    return t.reshape(d, s // d, w).transpose(1, 0, 2).reshape(s, w)


def _pad_lanes(t, n):
    return jnp.pad(t, ((0, 0), (0, n - t.shape[1])))


def _to_shards(g, axis):
    r, c = g.shape
    if axis == 0:
        return g.reshape(N_DEV, r // N_DEV, c)
    return g.reshape(r, N_DEV, c // N_DEV).transpose(1, 0, 2)


def _local_step(x, target, p, late_shards=(), early_exchange=True):
    s = x.shape[0]
    tabs_f, tabs_b = _rope_tables(s)
    expand = jnp.asarray(np.repeat(np.eye(N_HEADS, dtype=np.float32), HEAD_DIM, axis=1))
    w_main, w_dt = p["w_in"][:, :MAIN_W], _pad_lanes(p["w_in"][:, MAIN_W:], 128)
    al_r = {"f": p["a_log_f"], "b": p["a_log_b"]}
    bi_r = {"f": p["dt_bias_f"], "b": p["dt_bias_b"]}
    dskip_x = jnp.repeat(p["d_skip"], D_INNER // N_HEADS, axis=1)
    ssm_w3, ffn_w3 = p["ssm_conv_w"].T, p["ffn_conv_w"].T

    h1, h1t = _rmsnorm_fwd(x, p["norm1_w"], "norm1_fwd")
    proj = _matmul(h1, w_main, name="in_proj")
    dt_raw = _matmul(h1, w_dt, name="in_proj_dt", out_dtype=F32)
    dt_t = dt_raw[:, :2 * N_HEADS].T
    perms = {d: _perm_matrices(d) for d in DILATIONS[1:]}
    qkv = _rope_fwd(proj, tabs_f, perms, "rope_fwd")
    v_col = 2
    os_, lses = [], []
    for d, qkv_p in zip(DILATIONS, qkv):
        o_p, lse_p = _attn_fwd(qkv_p, qkv_p, v_col, s // d, f"attn_fwd_d{d}")
        os_.append(o_p)
        lses.append(_from_pattern(lse_p, d))
    mix, mix_t, lse_tot = _attn_combine(os_, lses, perms, expand, "attn_combine")

    xbc = _conv(proj, 3 * D_MODEL + D_INNER, XBC, ssm_w3, p["ssm_conv_b"], True, "ssm_conv_fwd")
    col = lambda r: r.reshape(N_HEADS, 1)
    ssd_args = {k: (al_r[k], col(al_r[k]), bi_r[k], col(bi_r[k])) for k in ("f", "b")}
    yf, st_f, yb, st_b, *got = _ssd_fwd(xbc, dt_raw, dt_t, ssd_args["f"], ssd_args["b"], "ssd_fwd", gather=late_shards)
    if late_shards:
        p = dict(p, w_out=got[0].reshape(2 * D_MODEL, D_MODEL), w_down=got[2].reshape(D_FF, D_MODEL),
                 w_up=got[1].transpose(1, 0, 2).reshape(D_MODEL, 2 * D_FF))
    mix, mix_t = _gate_fwd(yf, yb, xbc, proj, dskip_x, p["ssm_norm_w"], mix, mix_t, "ssm_gate_fwd")

    x2 =_matmul(mix, p["w_out"], name="out_proj", out_dtype=F32, residual=x)
    h2, h2t = _rmsnorm_fwd(x2, p["norm2_w"], "norm2_fwd")
    u = _matmul(h2, p["w_up"], name="ffn_up")
    act, act_t = _ffn_gate_fwd(u, ffn_w3, p["ffn_conv_b"], "ffn_gate_fwd")
    x3 = _matmul(act, p["w_down"], name="ffn_down", out_dtype=F32, residual=x2)

    dx3, dx3b, g_final, loss = _final_norm_loss(x3, p["final_norm_w"].reshape(1, D_MODEL), target, "final_norm_loss")
    g_w_down = _matmul(act_t, dx3b, name="dw_down")
    dact = _matmul(dx3b, p["w_down"], name="d_act", trans_b=True)
    dug, duu, dwg, dwu, dbg, dbu = _ffn_gate_bwd(u, ffn_w3, p["ffn_conv_b"], dact, "ffn_gate_bwd")
    du = _conv(dug, 0, D_FF, ffn_w3, p["ffn_conv_b"], False, "ffn_conv_bwd_gate", slab=(2 * D_FF, 0, None), transpose=True)
    du = _conv(duu, 0, D_FF, ffn_w3, p["ffn_conv_b"], False, "ffn_conv_bwd_up", slab=(2 * D_FF, D_FF, du), transpose=True, wcol0=D_FF)
    g_w_up = _matmul(h2t, du, name="dw_up")
    none_a, none_w = jnp.zeros((s, 128), BF16), jnp.zeros((D_MODEL, 128), BF16)
    dx2, dx2b, g_norm2 = _proj_norm_bwd(du, p["w_up"], none_a, none_w, x2, p["norm2_w"], dx3, "ffn_up_norm2_bwd")
    g_w_out = _matmul(mix_t, dx2b, name="dw_out")
    dmix = _matmul(dx2b, p["w_out"], name="d_mix", trans_b=True)

    delta, do_pat = _attn_delta(dmix, mix, expand.T, perms, "attn_delta")
    dqs, dks, dvs = [], [], []
    for d, qkv_p, do_p in zip(DILATIONS, qkv, [dmix] + do_pat):
        lse_p, dl_p = _to_pattern(lse_tot, d), _to_pattern(delta, d)
        dqs.append(_attn_bwd_dq(qkv_p, qkv_p, v_col, do_p, lse_p, dl_p, s // d, f"attn_bwd_dq_d{d}"))
        dk, dv = _attn_bwd_dkv(qkv_p, qkv_p, v_col, do_p, lse_p.T, dl_p.T, s // d, f"attn_bwd_dkv_d{d}")
        dks.append(dk)
        dvs.append(dv)
    dproj = _sum3_rope(dqs, perms, tabs_b, "rope_bwd_q", slab=(MAIN_W, 0, None))
    dproj = _sum3_rope(dks, perms, tabs_b, "rope_bwd_k", slab=(MAIN_W, D_MODEL, dproj))
    dproj = _sum3_rope(dvs, perms, None, "sum_dv", slab=(MAIN_W, 2 * D_MODEL, dproj))

    dy, dproj, dxs_skip, g_ssm_norm, g_dskip_lanes = _gate_bwd(yf, yb, xbc, proj, dskip_x, p["ssm_norm_w"], dmix, dproj, "ssm_gate_bwd")
    early_f = [_to_shards(g_w_up, 1), _to_shards(g_w_down, 0)] if early_exchange else []
    early_b = [_to_shards(g_w_out, 0)] if early_exchange else []
    dxbc_f, ddt_f, g_al_f, g_bi_f, *got_f = _ssd_bwd(xbc, dt_raw, dt_t, *ssd_args["f"], st_f, dy, 0, False, "ssd_bwd_f", exchange=early_f)
    dxbc_b, ddt_b, g_al_b, g_bi_b, *got_b = _ssd_bwd(xbc, dt_raw, dt_t, *ssd_args["b"], st_b, dy, N_HEADS, True, "ssd_bwd_b",
                                                     exchange=early_b)
    if early_exchange:
        (g_w_up, g_w_down), (g_w_out,) = got_f, got_b
    dpre, g_ssm_w3, g_ssm_cb = _conv_silu_bwd(proj, 3 * D_MODEL + D_INNER, XBC, ssm_w3, p["ssm_conv_b"],
                                              [dxbc_f, dxbc_b, dxs_skip], [XBC, XBC, D_INNER], "ssm_conv_bwd")
    dproj = _conv(dpre, 0, XBC, ssm_w3, p["ssm_conv_b"], False, "ssm_conv_bwd_x", transpose=True,
                  slab=(MAIN_W, 3 * D_MODEL + D_INNER, dproj))

    ddt =_pad_lanes(jnp.concatenate([ddt_f, ddt_b], axis=1), 128).astype(BF16)
    g_w_main = _matmul(h1t, dproj, name="dw_in")
    g_w_dt = _matmul(h1t, ddt, name="dw_in_dt")
    g_w_in = jnp.concatenate([g_w_main, g_w_dt[:, :2 * N_HEADS]], axis=1)
    late = [_to_shards(g_w_in, 1)] if early_exchange else []
    grad_x, _, g_norm1, *got = _proj_norm_bwd(dproj, w_main, ddt, w_dt, x, p["norm1_w"], dx2, "in_proj_norm1_bwd", exchange=late)
    if early_exchange:
        g_w_in = got[0]

    g_dskip = jnp.sum(g_dskip_lanes.reshape(N_HEADS, D_INNER // N_HEADS), axis=1).reshape(1, N_HEADS)
    small = {
        "norm1_w": g_norm1, "ssm_conv_w": g_ssm_w3.T, "ssm_conv_b": g_ssm_cb, "a_log_f": g_al_f, "a_log_b": g_al_b,
        "dt_bias_f": g_bi_f, "dt_bias_b": g_bi_b, "d_skip": g_dskip, "ssm_norm_w": g_ssm_norm, "norm2_w": g_norm2,
        "ffn_conv_w": jnp.concatenate([dwg, dwu], axis=1).T, "ffn_conv_b": jnp.concatenate([dbg, dbu], axis=1), "final_norm_w": g_final,
    }
    big = {"w_in": g_w_in, "w_out": g_w_out, "w_up": g_w_up, "w_down": g_w_down}
    return loss[0, 0], grad_x, big, small


SMALL_ORDER = ("norm1_w", "ssm_conv_w", "ssm_conv_b", "a_log_f", "a_log_b", "dt_bias_f", "dt_bias_b", "d_skip",
               "ssm_norm_w", "norm2_w", "ffn_conv_w", "ffn_conv_b", "final_norm_w")
SHARDED_SMALL = ("ssm_conv_w", "ffn_conv_w")
BIG_ORDER = ("w_in", "w_out", "w_up", "w_down")


def _pack(vals):
    rows = []
    for v in vals:
        f = v.reshape(-1).astype(F32)
        n = -(-f.shape[0] // 128) * 128
        rows.append(jnp.pad(f, (0, n - f.shape[0])).reshape(-1, 128))
    out = jnp.concatenate(rows, axis=0)
    pad = -out.shape[0] % 8
    return jnp.pad(out, ((0, pad), (0, 0)))


def _unpack(packed, shapes):
    out, r = [], 0
    for shp in shapes:
        n = math.prod(shp)
        nr = -(-n // 128)
        out.append(packed[r:r + nr].reshape(-1)[:n].reshape(shp))
        r += nr
    return out


def kernel(x, norm1_w, w_in, ssm_conv_w, ssm_conv_b, a_log_f, a_log_b, dt_bias_f, dt_bias_b, d_skip, ssm_norm_w, w_out, norm2_w, w_up, ffn_conv_w, ffn_conv_b, w_down, final_norm_w, loss_target, m_norm1_w, m_w_in, m_ssm_conv_w, m_ssm_conv_b, m_a_log_f, m_a_log_b, m_dt_bias_f, m_dt_bias_b, m_d_skip, m_ssm_norm_w, m_w_out, m_norm2_w, m_w_up, m_ffn_conv_w, m_ffn_conv_b, m_w_down, m_final_norm_w, v_norm1_w, v_w_in, v_ssm_conv_w, v_ssm_conv_b, v_a_log_f, v_a_log_b, v_dt_bias_f, v_dt_bias_b, v_d_skip, v_ssm_norm_w, v_w_out, v_norm2_w, v_w_up, v_ffn_conv_w, v_ffn_conv_b, v_w_down, v_final_norm_w):
    w = dict(norm1_w=norm1_w, w_in=w_in, ssm_conv_w=ssm_conv_w, ssm_conv_b=ssm_conv_b, a_log_f=a_log_f, a_log_b=a_log_b,
             dt_bias_f=dt_bias_f, dt_bias_b=dt_bias_b, d_skip=d_skip, ssm_norm_w=ssm_norm_w, w_out=w_out, norm2_w=norm2_w,
             w_up=w_up, ffn_conv_w=ffn_conv_w, ffn_conv_b=ffn_conv_b, w_down=w_down, final_norm_w=final_norm_w)
    mo = dict(norm1_w=m_norm1_w, w_in=m_w_in, ssm_conv_w=m_ssm_conv_w, ssm_conv_b=m_ssm_conv_b, a_log_f=m_a_log_f, a_log_b=m_a_log_b,
              dt_bias_f=m_dt_bias_f, dt_bias_b=m_dt_bias_b, d_skip=m_d_skip, ssm_norm_w=m_ssm_norm_w, w_out=m_w_out, norm2_w=m_norm2_w,
              w_up=m_w_up, ffn_conv_w=m_ffn_conv_w, ffn_conv_b=m_ffn_conv_b, w_down=m_w_down, final_norm_w=m_final_norm_w)
    vo = dict(norm1_w=v_norm1_w, w_in=v_w_in, ssm_conv_w=v_ssm_conv_w, ssm_conv_b=v_ssm_conv_b, a_log_f=v_a_log_f, a_log_b=v_a_log_b,
              dt_bias_f=v_dt_bias_f, dt_bias_b=v_dt_bias_b, d_skip=v_d_skip, ssm_norm_w=v_ssm_norm_w, w_out=v_w_out, norm2_w=v_norm2_w,
              w_up=v_w_up, ffn_conv_w=v_ffn_conv_w, ffn_conv_b=v_ffn_conv_b, w_down=v_w_down, final_norm_w=v_final_norm_w)
    me = _my_index()

    g_in, g_conv = _all_gather([w["w_in"][0].astype(BF16), _pack([w["ssm_conv_w"][0], w["ffn_conv_w"][0]])], "w_in_all_gather")
    conv_rows = [_unpack(g_conv[i], [ssm_conv_w.shape[1:], ffn_conv_w.shape[1:]]) for i in range(N_DEV)]
    full = {
        "w_in": g_in.transpose(1, 0, 2).reshape(D_MODEL, N_DEV * w_in.shape[2]),
        "ssm_conv_w": jnp.concatenate([c[0] for c in conv_rows], axis=0),
        "ffn_conv_w": jnp.concatenate([c[1] for c in conv_rows], axis=0),
    }
    for k in ("norm1_w", "ssm_conv_b", "a_log_f", "a_log_b", "dt_bias_f", "dt_bias_b", "d_skip", "ssm_norm_w", "norm2_w", "ffn_conv_b",
              "final_norm_w"):
        full[k] = w[k]
    late = [w["w_out"][0].astype(BF16), w["w_up"][0].astype(BF16), w["w_down"][0].astype(BF16)]

    loss_part, grad_x, big, small = _local_step(x[0], loss_target[0], full, late)

    small_shapes = [(1,)] + [small[k].shape for k in SMALL_ORDER]
    packed = _pack([loss_part] + [small[k] for k in SMALL_ORDER])
    out_small = jnp.broadcast_to(packed[None], (N_DEV,) + packed.shape)
    (r_small,) = _exchange([out_small], "small_grads_exchange")
    r_in, r_out, r_up, r_down = big["w_in"], big["w_out"], big["w_up"], big["w_down"]

    outs_g, outs_d, outs_m, outs_v = {}, {}, {}, {}
    for k, parts in zip(BIG_ORDER, (r_in, r_out, r_up, r_down)):
        g, dlt, nm, nv = _adamw(parts, w[k][0], mo[k][0], vo[k][0], f"adamw_{k}")
        outs_g[k], outs_d[k], outs_m[k], outs_v[k] = g[None], dlt[None], nm[None], nv[None]
    tot = _unpack(_sum_parts(r_small, "small_grads_sum"), small_shapes)
    loss = tot[0][0]
    gs = dict(zip(SMALL_ORDER, tot[1:]))
    g_own = {}
    for k in SMALL_ORDER:
        if k in SHARDED_SMALL:
            rows = w[k].shape[1]
            g_own[k] = lax.dynamic_slice_in_dim(gs[k], me * rows, rows, axis=0)[None]
        else:
            g_own[k] = gs[k].reshape(w[k].shape)
    two_d = lambda a: a.reshape(-1, a.shape[-1])
    d_s, m_s, v_s = _adamw_small([two_d(g_own[k]) for k in SMALL_ORDER], [two_d(w[k]) for k in SMALL_ORDER],
                                 [two_d(mo[k]) for k in SMALL_ORDER], [two_d(vo[k]) for k in SMALL_ORDER], "adamw_small")
    for k, a, b, c in zip(SMALL_ORDER, d_s, m_s, v_s):
        shp = w[k].shape
        outs_g[k], outs_d[k], outs_m[k], outs_v[k] = g_own[k], a.reshape(shp), b.reshape(shp), c.reshape(shp)

    order = ("norm1_w", "w_in", "ssm_conv_w", "ssm_conv_b", "a_log_f", "a_log_b", "dt_bias_f", "dt_bias_b", "d_skip", "ssm_norm_w",
             "w_out", "norm2_w", "w_up", "ffn_conv_w", "ffn_conv_b", "w_down", "final_norm_w")
    return (loss, grad_x[None], *[outs_g[k] for k in order], *[outs_d[k] for k in order],
            *[outs_m[k] for k in order], *[outs_v[k] for k in order])
```

```python
import math

import numpy as np
import jax
import jax.numpy as jnp
from jax import lax
from jax.experimental import pallas as pl
from jax.experimental.pallas import tpu as pltpu

F32 = jnp.float32
BF16 = jnp.bfloat16
SDS = jax.ShapeDtypeStruct

N_DEV = 8
D_MODEL = 1024
N_HEADS = 16
HEAD_DIM = 64
ROPE_DIM = 16
ROPE_THETA = 500000.0
DILATIONS = (1, 4, 16)
BAND_HALF = 64
D_INNER = 1024
N_GROUPS = 4
D_STATE = 128
CHUNK = 128
XBC = D_INNER + 2 * N_GROUPS * D_STATE
D_FF = 2816
MAIN_W = 3 * D_MODEL + D_INNER + XBC
EPS = 1e-6
LR, B1, B2, AEPS, WD, STEP = 0.001, 0.9, 0.999, 1e-08, 0.01, 10
NEG = -1e30
VMEM_LIMIT = 56 * 1024 * 1024
MESH = pl.DeviceIdType.MESH
HIGH = lax.Precision.HIGHEST
NT = (((1,), (1,)), ((), ()))
TN = (((0,), (0,)), ((), ()))


def _cp(*sem):
    return pltpu.CompilerParams(dimension_semantics=sem, vmem_limit_bytes=VMEM_LIMIT)


def _pick(n, cands):
    for c in cands:
        if n % c == 0:
            return c
    raise ValueError(f"no tile for {n}")


def _sigmoid(x):
    return 1.0 / (1.0 + jnp.exp(-x))


def _dot_01(x, m01):
    mb = m01.astype(BF16)
    out, r = None, x
    for _ in range(3):
        p = r.astype(BF16)
        r = r - p.astype(F32)
        t = jnp.dot(p, mb, preferred_element_type=F32)
        out = t if out is None else out + t
    return out


def _softplus(x):
    return jnp.maximum(x, 0.0) + jnp.log1p(jnp.exp(-jnp.abs(x)))


def _slab(s, width, dtype, slab, n_in, out_idx=0):
    if slab is None:
        return SDS((s, width), dtype), 0, [], [], {}
    total, col0, into = slab
    if into is None:
        return SDS((s, total), dtype), col0, [], [], {}
    return SDS((s, total), dtype), col0, [into], [pl.BlockSpec(memory_space=pl.ANY)], {n_in: out_idx}


def _matmul(a, b, *, name, trans_b=False, out_dtype=BF16, residual=None):
    m, k = a.shape
    n = b.shape[0] if trans_b else b.shape[1]
    tk = k if k <= 2048 else _pick(k, (2048, 1408, 1024, 512))
    nk = k // tk
    if nk == 1:
        tm = _pick(m, (2048, 1408, 1024, 512, 256, 128))
        tn = _pick(n, (512, 256, 128))
    else:
        tm = _pick(m, (1024, 1408, 512, 256, 128))
        tn = _pick(n, (1024, 1408, 512, 256, 128))
    dn = NT if trans_b else (((1,), (0,)), ((), ()))

    def body(*refs):
        a_ref, b_ref = refs[0], refs[1]
        o_ref, acc = refs[-2], refs[-1]
        kk = pl.program_id(2)

        @pl.when(kk == 0)
        def _():
            acc[...] = jnp.zeros_like(acc)

        acc[...] += lax.dot_general(a_ref[...], b_ref[...], dn, preferred_element_type=F32)

        @pl.when(kk == nk - 1)
        def _():
            r = acc[...]
            if residual is not None:
                r = r + refs[2][...].astype(F32)
            o_ref[...] = r.astype(o_ref.dtype)

    in_specs = [pl.BlockSpec((tm, tk), lambda i, j, kk: (i, kk)),
                pl.BlockSpec((tn, tk), lambda i, j, kk: (j, kk)) if trans_b else pl.BlockSpec((tk, tn), lambda i, j, kk: (kk, j))]
    args = [a, b]
    if residual is not None:
        in_specs.append(pl.BlockSpec((tm, tn), lambda i, j, kk: (i, j)))
        args.append(residual)
    return pl.pallas_call(
        body, grid=(m // tm, n // tn, nk), in_specs=in_specs,
        out_specs=pl.BlockSpec((tm, tn), lambda i, j, kk: (i, j)),
        out_shape=SDS((m, n), out_dtype), scratch_shapes=[pltpu.VMEM((tm, tn), F32)],
        name=name, compiler_params=_cp("parallel", "parallel", "arbitrary"))(*args)


def _rmsnorm_fwd(x, w, name, gather=()):
    s, d = x.shape
    tm = _pick(s, (512, 128))
    n = s // tm
    ng = len(gather)

    def body(*refs):
        x_ref, w_ref = refs[:2]
        g_src = refs[2:2 + ng]
        o_ref, ot_ref = refs[2 + ng:4 + ng]
        g_dst = refs[4 + ng:4 + 2 * ng]
        g_sems = refs[4 + 2 * ng:]
        step = pl.program_id(0)
        if ng:
            @pl.when(step == 0)
            def _():
                _gather_start(g_src, g_dst, *g_sems)

        xf = x_ref[...]
        r = lax.rsqrt(jnp.mean(xf * xf, axis=-1, keepdims=True) + EPS)
        out = (xf * r * w_ref[...]).astype(o_ref.dtype)
        o_ref[...] = out
        ot_ref[...] = out.T

        if ng:
            @pl.when(step == n - 1)
            def _():
                _gather_forward(g_src, g_dst, *g_sems)
                _gather_finish(g_src, g_dst, *g_sems)

    any_spec = pl.BlockSpec(memory_space=pl.ANY)
    return pl.pallas_call(
        body, grid=(n,), in_specs=[pl.BlockSpec((tm, d), lambda i: (i, 0)), pl.BlockSpec((1, d), lambda i: (0, 0))] + [any_spec] * ng,
        out_specs=[pl.BlockSpec((tm, d), lambda i: (i, 0)), pl.BlockSpec((d, tm), lambda i: (0, i))] + [any_spec] * ng,
        out_shape=[SDS((s, d), BF16), SDS((d, s), BF16)] + [SDS((N_DEV,) + g.shape, g.dtype) for g in gather],
        scratch_shapes=_exchange_sems(ng) if ng else [],
        name=name, compiler_params=_cp("arbitrary" if ng else "parallel"))(x, w, *gather)


def _proj_norm_bwd(da, wt, da2, wt2, x, w, dres, name, exchange=()):
    s, k = da.shape
    d = x.shape[1]
    tm = _pick(s, (512, 128))
    tk = _pick(k, (1408, 1024, 512))
    nk, ni = k // tk, s // tm
    nx = len(exchange)
    last = nk - 1

    def body(*refs):
        a_ref, b_ref, a2_ref, b2_ref, x_ref, w_ref, dres_ref = refs[:7]
        xch_src = refs[7:7 + nx]
        dx_ref, dxb_ref, dw_ref = refs[7 + nx:10 + nx]
        xch_dst = refs[10 + nx:10 + 2 * nx]
        acc = refs[10 + 2 * nx]
        xch_sems = refs[11 + 2 * nx:]
        kk, i = pl.program_id(0), pl.program_id(1)

        @pl.when((i == 0) & (kk == 0))
        def _():
            dw_ref[...] = jnp.zeros_like(dw_ref)
            if nx:
                _exchange_start(xch_src, xch_dst, *xch_sems)

        @pl.when(kk == 0)
        def _():
            acc[i] = lax.dot_general(a2_ref[...], b2_ref[...], NT, preferred_element_type=F32)

        acc[i] += lax.dot_general(a_ref[...], b_ref[...], NT, preferred_element_type=F32)

        @pl.when(kk == last)
        def _():
            dh = acc[i]
            xf = x_ref[...]
            r = lax.rsqrt(jnp.mean(xf * xf, axis=-1, keepdims=True) + EPS)
            xhat = xf * r
            g = dh * w_ref[...]
            dx = dres_ref[...] + r * (g - xhat * jnp.mean(g * xhat, axis=-1, keepdims=True))
            dx_ref[...] = dx
            dxb_ref[...] = dx.astype(dxb_ref.dtype)
            dw_ref[...] += jnp.sum(dh * xhat, axis=0, keepdims=True)

        if nx:
            @pl.when((i == ni - 1) & (kk == last))
            def _():
                _exchange_wait(xch_src, xch_dst, *xch_sems)

    row_last = pl.BlockSpec((tm, d), lambda kk, i: (jnp.where(kk == last, i, 0), 0))
    vec = pl.BlockSpec((1, d), lambda kk, i: (0, 0))
    any_spec = pl.BlockSpec(memory_space=pl.ANY)
    in_specs = [pl.BlockSpec((tm, tk), lambda kk, i: (i, kk)), pl.BlockSpec((d, tk), lambda kk, i: (0, kk)),
                pl.BlockSpec((tm, da2.shape[1]), lambda kk, i: (jnp.where(kk == 0, i, 0), 0)),
                pl.BlockSpec((d, wt2.shape[1]), lambda kk, i: (0, 0)),
                row_last, vec, row_last]
    return pl.pallas_call(
        body, grid=(nk, ni), in_specs=in_specs + [any_spec] * nx, out_specs=[row_last, row_last, vec] + [any_spec] * nx,
        out_shape=[SDS((s, d), F32), SDS((s, d), BF16), SDS((1, d), F32)] + [SDS(e.shape, e.dtype) for e in exchange],
        scratch_shapes=[pltpu.VMEM((ni, tm, d), F32)] + (_exchange_sems(nx) if nx else []),
        name=name, compiler_params=_cp("arbitrary", "arbitrary"))(da, wt, da2, wt2, x, w, dres, *exchange)


def _final_norm_loss(x, w, target, name):
    s, d = x.shape
    tm = _pick(s, (512, 128))

    def body(x_ref, w_ref, t_ref, dx_ref, dxb_ref, dw_ref, loss_ref):
        xf = x_ref[...]
        r = lax.rsqrt(jnp.mean(xf * xf, axis=-1, keepdims=True) + EPS)
        xhat = xf * r
        wv = w_ref[...]
        e = xhat * wv - t_ref[...]
        dy = e * (1.0 / d)
        g = dy * wv
        dx = r * (g - xhat * jnp.mean(g * xhat, axis=-1, keepdims=True))
        dx_ref[...] = dx
        dxb_ref[...] = dx.astype(dxb_ref.dtype)

        @pl.when(pl.program_id(0) == 0)
        def _():
            dw_ref[...] = jnp.zeros_like(dw_ref)
            loss_ref[...] = jnp.zeros_like(loss_ref)

        dw_ref[...] += jnp.sum(dy * xhat, axis=0, keepdims=True)
        loss_ref[...] += jnp.sum(jnp.sum(e * e, axis=1, keepdims=True), axis=0, keepdims=True) * (0.5 / d)

    row = pl.BlockSpec((tm, d), lambda i: (i, 0))
    vec = pl.BlockSpec((1, d), lambda i: (0, 0))
    return pl.pallas_call(
        body, grid=(s // tm,), in_specs=[row, vec, row], out_specs=[row, row, vec, pl.BlockSpec((1, 128), lambda i: (0, 0))],
        out_shape=[SDS((s, d), F32), SDS((s, d), BF16), SDS((1, d), F32), SDS((1, 128), F32)],
        name=name, compiler_params=_cp("arbitrary"))(x, w, target)


def _rope_tables(s):
    half = ROPE_DIM // 2
    f32 = np.float32
    inv_freq = np.power(f32(ROPE_THETA), -np.arange(half, dtype=f32) * f32(2.0) / f32(ROPE_DIM)).astype(f32)
    ang = (np.arange(s, dtype=f32)[:, None] * inv_freq[None, :]).astype(f32)
    cos, sin = np.cos(ang).astype(f32), np.sin(ang).astype(f32)
    z = np.zeros((s, HEAD_DIM - ROPE_DIM), f32)
    zh = np.zeros((s, half), f32)
    c = np.concatenate([cos, cos, z + 1.0], axis=1)
    sa = np.concatenate([zh, sin, z], axis=1)
    sb = np.concatenate([-sin, zh, z], axis=1)
    two = lambda t: np.concatenate([t, t], axis=1)
    c, sa, sb = two(c), two(sa), two(sb)
    fwd, bwd = (c, sa, sb), (c, np.roll(sb, half, axis=1), np.roll(sa, -half, axis=1))
    return tuple(jnp.asarray(t) for t in fwd), tuple(jnp.asarray(t) for t in bwd)


PERM_TILE = 256


def _perm_matrices(d):
    n = PERM_TILE // d
    o = np.arange(PERM_TILE)
    p = np.zeros((PERM_TILE, PERM_TILE), np.float32)
    p[o, (o % n) * d + o // n] = 1.0
    return jnp.asarray(p, dtype=BF16), jnp.asarray(p.T.copy(), dtype=BF16)


def _store_pattern(o_ref, tile, perm_ref, d, cols=slice(None)):
    n = PERM_TILE // d
    z = jnp.dot(perm_ref[...], tile, preferred_element_type=F32).astype(o_ref.dtype)
    for r in range(d):
        o_ref[r, :, cols] = z[r * n:(r + 1) * n]


def _load_pattern(x_ref, perm_ref, d):
    tile = jnp.concatenate([x_ref[r] for r in range(d)], axis=0)
    return jnp.dot(perm_ref[...], tile, preferred_element_type=F32)


def _pattern_spec(d, w, col=0):
    return pl.BlockSpec((d, PERM_TILE // d, w), lambda i, *_: (0, i, col))


def _rope_fwd(proj, tabs, perms, name):
    s = proj.shape[0]
    tm = PERM_TILE
    half = ROPE_DIM // 2
    wb = D_MODEL
    nd = len(DILATIONS) - 1

    def body(x_ref, c_ref, sa_ref, sb_ref, *rest):
        perm_refs, o_ref, op_refs = rest[:nd], rest[nd], rest[nd + 1:]
        is_v = pl.program_id(1) == 2
        qs = jnp.where(pl.program_id(1) == 0, HEAD_DIM ** -0.5, 1.0)
        c = jnp.where(is_v, 1.0, c_ref[...]) * qs
        sa = jnp.where(is_v, 0.0, sa_ref[...]) * qs
        sb = jnp.where(is_v, 0.0, sb_ref[...]) * qs
        for j in range(wb // 128):
            x = x_ref[:, j * 128:(j + 1) * 128].astype(F32)
            o_ref[:, j * 128:(j + 1) * 128] = (x * c + pltpu.roll(x, half, 1) * sa + pltpu.roll(x, 128 - half, 1) * sb).astype(o_ref.dtype)
        y = o_ref[...]
        for d, perm_ref, op_ref in zip(DILATIONS[1:], perm_refs, op_refs):
            _store_pattern(op_ref, y, perm_ref, d)

    blk = pl.BlockSpec((tm, wb), lambda i, j: (i, j))
    tab = pl.BlockSpec((tm, 128), lambda i, j: (i, 0))
    pm = pl.BlockSpec((tm, tm), lambda i, j: (0, 0))
    outs = pl.pallas_call(
        body, grid=(s // tm, 3), in_specs=[blk, tab, tab, tab] + [pm] * nd,
        out_specs=[blk] + [pl.BlockSpec((d, tm // d, wb), lambda i, j: (0, i, j)) for d in DILATIONS[1:]],
        out_shape=[SDS((s, 3 * wb), BF16)] + [SDS((d, s // d, 3 * wb), BF16) for d in DILATIONS[1:]],
        name=name, compiler_params=_cp("parallel", "parallel"))(proj, *tabs, *[perms[d][0] for d in DILATIONS[1:]])
    return [o.reshape(s, 3 * wb) for o in outs]


def _sum3_rope(ds_, perms, tabs, name, slab=None):
    s, w = ds_[0].shape
    tm = PERM_TILE
    half = ROPE_DIM // 2
    nd = len(DILATIONS) - 1

    def body(*refs):
        x_refs, perm_refs = refs[:nd + 1], refs[nd + 1:2 * nd + 1]
        tab_refs = refs[2 * nd + 1:2 * nd + 4]
        tot = x_refs[0][...].astype(F32)
        for d, x_ref, perm_ref in zip(DILATIONS[1:], x_refs[1:], perm_refs):
            tot = tot + _load_pattern(x_ref, perm_ref, d)
        for j in range(w // 128):
            x = tot[:, j * 128:(j + 1) * 128]
            if tabs is not None:
                x = x * tab_refs[0][...] + pltpu.roll(x, half, 1) * tab_refs[1][...] + pltpu.roll(x, 128 - half, 1) * tab_refs[2][...]
            refs[-1][:, j * 128:(j + 1) * 128] = x.astype(refs[-1].dtype)

    blk = pl.BlockSpec((tm, w), lambda i: (i, 0))
    tab = pl.BlockSpec((tm, 128), lambda i: (i, 0))
    pm = pl.BlockSpec((tm, tm), lambda i: (0, 0))
    extra = [] if tabs is None else list(tabs)
    out_shape, col0, more, more_specs, alias = _slab(s, w, BF16, slab, 2 * nd + 1 + len(extra))
    cb = col0 // w
    args = [ds_[0]] + [x.reshape(d, s // d, w) for d, x in zip(DILATIONS[1:], ds_[1:])] + [perms[d][1] for d in DILATIONS[1:]]
    return pl.pallas_call(
        body, grid=(s // tm,),
        in_specs=[blk] + [_pattern_spec(d, w) for d in DILATIONS[1:]] + [pm] * nd + [tab] * len(extra) + more_specs,
        out_specs=pl.BlockSpec((tm, w), lambda i: (i, cb)), out_shape=out_shape, input_output_aliases=alias,
        name=name, compiler_params=_cp("parallel"))(*args, *extra, *more)


def _band_valid(t, nq, nk, qofs, kofs, seq_len):
    qpos = t * 128 + qofs + lax.broadcasted_iota(jnp.int32, (nq, nk), 0)
    kpos = t * 128 + kofs + lax.broadcasted_iota(jnp.int32, (nq, nk), 1)
    sh = int(math.log2(seq_len))
    same = lax.shift_right_arithmetic(qpos, sh) == lax.shift_right_arithmetic(kpos, sh)
    return same & (jnp.abs(kpos - qpos) <= BAND_HALF)


def _window(r0, r1, r2):
    return jnp.concatenate([r0[128 - BAND_HALF:128], r1[...], r2[0:BAND_HALF]], axis=0)


WIN = 128 + 2 * BAND_HALF


def _first_head_lanes():
    return lax.broadcasted_iota(jnp.int32, (1, 2 * HEAD_DIM), 1) < HEAD_DIM


def _only(x, keep):
    return jnp.where(keep, x, jnp.zeros((), x.dtype))


def _win_specs(width, col, nt):
    return [pl.BlockSpec((128, width), lambda t: (jnp.maximum(t - 1, 0), col)),
            pl.BlockSpec((128, width), lambda t: (t, col)),
            pl.BlockSpec((128, width), lambda t: (jnp.minimum(t + 1, nt - 1), col))]


def _attn_fwd(qk, v_src, v_col, seq_len, name):
    s = qk.shape[0]
    nt = s // 128
    dm = D_MODEL

    def body(q_ref, k0, k1, k2, v0, v1, v2, o_ref, lse_ref):
        t = pl.program_id(0)
        valid = _band_valid(t, 128, WIN, 0, -BAND_HALF, seq_len)
        q = q_ref[...]
        kc = _window(k0, k1, k2)
        vc = _window(v0, v1, v2)
        first = _first_head_lanes()
        outs, lses = [], []
        for pr in range(N_HEADS // 2):
            ps = slice(pr * 128, (pr + 1) * 128)
            qp, kp, vp = q[:, ps], kc[:, ps], vc[:, ps]
            halves = []
            for keep in (first, ~first):
                sc = lax.dot_general(_only(qp, keep), kp, NT, preferred_element_type=F32)
                sc = jnp.where(valid, sc, NEG)
                m = jnp.max(sc, axis=1, keepdims=True)
                e = jnp.exp(sc - m)
                den = jnp.sum(e, axis=1, keepdims=True)
                halves.append(jnp.dot(e.astype(BF16), vp, preferred_element_type=F32) / den)
                lses.append(m + jnp.log(den))
            outs.append(jnp.where(first, halves[0], halves[1]))
        o_ref[...] = jnp.concatenate(outs, axis=1).astype(o_ref.dtype)
        lse_ref[...] = jnp.concatenate(lses, axis=1)

    in_specs = [pl.BlockSpec((128, dm), lambda t: (t, 0))] + _win_specs(dm, 1, nt) + _win_specs(dm, v_col, nt)
    return pl.pallas_call(
        body, grid=(nt,), in_specs=in_specs,
        out_specs=[pl.BlockSpec((128, dm), lambda t: (t, 0)), pl.BlockSpec((128, N_HEADS), lambda t: (t, 0))],
        out_shape=[SDS((s, dm), BF16), SDS((s, N_HEADS), F32)], name=name, compiler_params=_cp("parallel"))(
            qk, qk, qk, qk, v_src, v_src, v_src)


def _attn_combine(os_, lses, perms, expand, name):
    s, dm = os_[0].shape
    tm = PERM_TILE
    nd = len(DILATIONS) - 1

    def body(o1, o2, o3, l1, l2, l3, p2, p3, e_ref, out_ref, out_t_ref, lt_ref):
        ls = [l1[...], l2[...], l3[...]]
        m = jnp.maximum(jnp.maximum(ls[0], ls[1]), ls[2])
        es = [jnp.exp(l - m) for l in ls]
        tot = es[0] + es[1] + es[2]
        lt_ref[...] = m + jnp.log(tot)
        ovs = [o1[...].astype(F32), _load_pattern(o2, p2, DILATIONS[1]), _load_pattern(o3, p3, DILATIONS[2])]
        acc = jnp.zeros((tm, dm), F32)
        for e, o in zip(es, ovs):
            acc = acc + _dot_01(e / tot, e_ref[...]) * o
        out = acc.astype(out_ref.dtype)
        out_ref[...] = out
        out_t_ref[...] = out.T

    row = pl.BlockSpec((tm, dm), lambda i: (i, 0))
    st = pl.BlockSpec((tm, N_HEADS), lambda i: (i, 0))
    pm = pl.BlockSpec((tm, tm), lambda i: (0, 0))
    args = [os_[0]] + [o.reshape(d, s // d, dm) for d, o in zip(DILATIONS[1:], os_[1:])]
    return pl.pallas_call(
        body, grid=(s // tm,),
        in_specs=[row] + [_pattern_spec(d, dm) for d in DILATIONS[1:]] + [st, st, st, pm, pm, pl.BlockSpec((N_HEADS, dm), lambda i: (0, 0))],
        out_specs=[row, pl.BlockSpec((dm, tm), lambda i: (0, i)), st],
        out_shape=[SDS((s, dm + D_INNER), BF16), SDS((dm + D_INNER, s), BF16), SDS((s, N_HEADS), F32)],
        name=name, compiler_params=_cp("parallel"))(*args, *lses, *[perms[d][1] for d in DILATIONS[1:]], expand)


def _attn_delta(dmix, attn, expand_t, perms, name):
    s, dm = attn.shape[0], D_MODEL
    tm = PERM_TILE
    nd = len(DILATIONS) - 1

    def body(d_ref, a_ref, e_ref, *rest):
        perm_refs, o_ref, op_refs = rest[:nd], rest[nd], rest[nd + 1:]
        dv = d_ref[...]
        prod = dv.astype(F32) * a_ref[...].astype(F32)
        o_ref[...] = _dot_01(prod, e_ref[...])
        for d, perm_ref, op_ref in zip(DILATIONS[1:], perm_refs, op_refs):
            _store_pattern(op_ref, dv, perm_ref, d)

    row = pl.BlockSpec((tm, dm), lambda i: (i, 0))
    pm = pl.BlockSpec((tm, tm), lambda i: (0, 0))
    outs = pl.pallas_call(
        body, grid=(s // tm,), in_specs=[row, row, pl.BlockSpec((dm, N_HEADS), lambda i: (0, 0))] + [pm] * nd,
        out_specs=[pl.BlockSpec((tm, N_HEADS), lambda i: (i, 0))] + [_pattern_spec(d, dm) for d in DILATIONS[1:]],
        out_shape=[SDS((s, N_HEADS), F32)] + [SDS((d, s // d, dm), BF16) for d in DILATIONS[1:]],
        name=name, compiler_params=_cp("parallel"))(dmix, attn, expand_t, *[perms[d][0] for d in DILATIONS[1:]])
    return outs[0], [o.reshape(s, dm) for o in outs[1:]]


def _attn_bwd_dq(qk, v_src, v_col, do_src, lse, delta, seq_len, name):
    s = qk.shape[0]
    nt = s // 128
    dm = D_MODEL

    def body(q_ref, k0, k1, k2, v0, v1, v2, do_ref, lse_ref, dl_ref, dq_ref):
        t = pl.program_id(0)
        valid = _band_valid(t, 128, WIN, 0, -BAND_HALF, seq_len)
        q = q_ref[...]
        do = do_ref[...]
        kc = _window(k0, k1, k2)
        vc = _window(v0, v1, v2)
        lse_v, dl_v = lse_ref[...], dl_ref[...]
        first = _first_head_lanes()
        outs = []
        for pr in range(N_HEADS // 2):
            ps = slice(pr * 128, (pr + 1) * 128)
            qp, dop, kp, vp = q[:, ps], do[:, ps], kc[:, ps], vc[:, ps]
            halves = []
            for i, keep in enumerate((first, ~first)):
                h = 2 * pr + i
                sc = lax.dot_general(_only(qp, keep), kp, NT, preferred_element_type=F32)
                p = jnp.exp(jnp.where(valid, sc - lse_v[:, h:h + 1], NEG))
                dp = lax.dot_general(_only(dop, keep), vp, NT, preferred_element_type=F32)
                ds = p * (dp - dl_v[:, h:h + 1])
                halves.append(jnp.dot(ds.astype(BF16), kp, preferred_element_type=F32))
            outs.append(jnp.where(first, halves[0], halves[1]) * (HEAD_DIM ** -0.5))
        dq_ref[...] = jnp.concatenate(outs, axis=1).astype(dq_ref.dtype)

    row = pl.BlockSpec((128, dm), lambda t: (t, 0))
    st = pl.BlockSpec((128, N_HEADS), lambda t: (t, 0))
    in_specs = [row] + _win_specs(dm, 1, nt) + _win_specs(dm, v_col, nt) + [row, st, st]
    return pl.pallas_call(
        body, grid=(nt,), in_specs=in_specs, out_specs=row, out_shape=SDS((s, dm), BF16),
        name=name, compiler_params=_cp("parallel"))(qk, qk, qk, qk, v_src, v_src, v_src, do_src, lse, delta)


def _attn_bwd_dkv(qk, v_src, v_col, do_src, lse_t, delta_t, seq_len, name):
    s = qk.shape[0]
    nt = s // 128
    dm = D_MODEL

    def lane_window(r0, r1, r2):
        return jnp.concatenate([r0[:, 128 - BAND_HALF:128], r1[...], r2[:, 0:BAND_HALF]], axis=1)

    def body(k_ref, v_ref, q0, q1, q2, d0, d1, d2, l0, l1, l2, e0, e1, e2, dk_ref, dv_ref):
        t = pl.program_id(0)
        valid = _band_valid(t, 128, WIN, 0, -BAND_HALF, seq_len)
        k = k_ref[...]
        v = v_ref[...]
        qc = _window(q0, q1, q2)
        dc = _window(d0, d1, d2)
        lse_v = lane_window(l0, l1, l2)
        dl_v = lane_window(e0, e1, e2)
        first = _first_head_lanes()
        dks, dvs = [], []
        for pr in range(N_HEADS // 2):
            ps = slice(pr * 128, (pr + 1) * 128)
            kp, vp, qp, dop = k[:, ps], v[:, ps], qc[:, ps], dc[:, ps]
            dk_h, dv_h = [], []
            for i, keep in enumerate((first, ~first)):
                h = 2 * pr + i
                sc = lax.dot_general(_only(kp, keep), qp, NT, preferred_element_type=F32)
                p = jnp.exp(jnp.where(valid, sc - lse_v[h:h + 1, :], NEG))
                dv_h.append(jnp.dot(p.astype(BF16), dop, preferred_element_type=F32))
                dp = lax.dot_general(_only(vp, keep), dop, NT, preferred_element_type=F32)
                ds = p * (dp - dl_v[h:h + 1, :])
                dk_h.append(jnp.dot(ds.astype(BF16), qp, preferred_element_type=F32))
            dks.append(jnp.where(first, dk_h[0], dk_h[1]))
            dvs.append(jnp.where(first, dv_h[0], dv_h[1]))
        dk_ref[...] = jnp.concatenate(dks, axis=1).astype(dk_ref.dtype)
        dv_ref[...] = jnp.concatenate(dvs, axis=1).astype(dv_ref.dtype)

    row = pl.BlockSpec((128, dm), lambda t: (t, 0))
    stat = [pl.BlockSpec((N_HEADS, 128), lambda t: (0, jnp.maximum(t - 1, 0))), pl.BlockSpec((N_HEADS, 128), lambda t: (0, t)),
            pl.BlockSpec((N_HEADS, 128), lambda t: (0, jnp.minimum(t + 1, nt - 1)))]
    in_specs = ([pl.BlockSpec((128, dm), lambda t: (t, 1)), pl.BlockSpec((128, dm), lambda t: (t, v_col))]
                + _win_specs(dm, 0, nt) + _win_specs(dm, 0, nt) + stat + stat)
    return pl.pallas_call(
        body, grid=(nt,), in_specs=in_specs, out_specs=[row, row], out_shape=[SDS((s, dm), BF16), SDS((s, dm), BF16)],
        name=name, compiler_params=_cp("parallel"))(qk, v_src, qk, qk, qk, do_src, do_src, do_src, lse_t, lse_t, lse_t, delta_t, delta_t, delta_t)


CONV_COLS = (1024, 1408, 512, 256)


def _halo_specs(tm, tc, col0, nrow_blocks):
    r = tm // 16
    return [pl.BlockSpec((16, tc), lambda i, j: (jnp.maximum(i * r - 1, 0), col0 + j)),
            pl.BlockSpec((16, tc), lambda i, j: (jnp.minimum((i + 1) * r, nrow_blocks * r - 1), col0 + j))]


def _shifted(x_ref, hp_ref, hn_ref, i, last):
    x = x_ref[...].astype(F32)
    tm = x.shape[0]
    rows = lax.broadcasted_iota(jnp.int32, x.shape, 0)
    prev_row = jnp.where(i > 0, hp_ref[15:16, :].astype(F32), 0.0)
    next_row = jnp.where(i < last, hn_ref[0:1, :].astype(F32), 0.0)
    xp = jnp.where(rows == 0, prev_row, pltpu.roll(x, 1, 0))
    xn = jnp.where(rows == tm - 1, next_row, pltpu.roll(x, tm - 1, 0))
    return xp, x, xn


def _conv(x_src, col0, width, w3, bias, act, name, out_dtype=BF16, slab=None, transpose=False, wcol0=0):
    s = x_src.shape[0]
    tm = _pick(s, (256, 128))
    tc = _pick(width, CONV_COLS)
    nb = s // tm
    c0 = col0 // tc
    wc0 = wcol0 // tc

    def body(*refs):
        x_ref, hp_ref, hn_ref, w_ref, b_ref = refs[:5]
        o_ref = refs[-1]
        i = pl.program_id(0)
        xp, x, xn = _shifted(x_ref, hp_ref, hn_ref, i, nb - 1)
        w = w_ref[...]
        if transpose:
            y = w[2:3, :] * xp + w[1:2, :] * x + w[0:1, :] * xn
        else:
            y = w[0:1, :] * xp + w[1:2, :] * x + w[2:3, :] * xn + b_ref[...]
        if act:
            y = y * _sigmoid(y)
        o_ref[...] = y.astype(o_ref.dtype)

    in_specs = ([pl.BlockSpec((tm, tc), lambda i, j: (i, c0 + j))] + _halo_specs(tm, tc, c0, nb)
                + [pl.BlockSpec((3, tc), lambda i, j: (0, wc0 + j)), pl.BlockSpec((1, tc), lambda i, j: (0, wc0 + j))])
    out_shape, ocol, more, more_specs, alias = _slab(s, width, out_dtype, slab, 5)
    ob = ocol // tc
    return pl.pallas_call(
        body, grid=(nb, width // tc), in_specs=in_specs + more_specs, out_specs=pl.BlockSpec((tm, tc), lambda i, j: (i, ob + j)),
        out_shape=out_shape, input_output_aliases=alias, name=name, compiler_params=_cp("parallel", "parallel"))(
            x_src, x_src, x_src, w3, bias, *more)


def _conv_silu_bwd(x_src, col0, width, w3, bias, addends, add_widths, name):
    s = x_src.shape[0]
    tm = _pick(s, (256, 128))
    tc = _pick(width, CONV_COLS)
    nb = s // tm
    c0 = col0 // tc
    na = len(addends)

    def body(*refs):
        x_ref, hp_ref, hn_ref, w_ref, b_ref = refs[:5]
        a_refs = refs[5:5 + na]
        dp_ref, dw_ref, db_ref = refs[5 + na:]
        i, j = pl.program_id(1), pl.program_id(0)
        xp, x, xn = _shifted(x_ref, hp_ref, hn_ref, i, nb - 1)
        w = w_ref[...]
        pre = w[0:1, :] * xp + w[1:2, :] * x + w[2:3, :] * xn + b_ref[...]
        g = jnp.zeros_like(pre)
        for a_ref, aw in zip(a_refs, add_widths):
            av = a_ref[...].astype(F32)
            g = g + (av if aw == width else jnp.where(j < aw // tc, av, 0.0))
        sg = _sigmoid(pre)
        dpre = g * (sg * (1.0 + pre * (1.0 - sg)))
        dp_ref[...] = dpre.astype(dp_ref.dtype)

        @pl.when(i == 0)
        def _():
            dw_ref[...] = jnp.zeros_like(dw_ref)
            db_ref[...] = jnp.zeros_like(db_ref)

        dw_ref[...] += jnp.concatenate([jnp.sum(dpre * xp, axis=0, keepdims=True), jnp.sum(dpre * x, axis=0, keepdims=True),
                                        jnp.sum(dpre * xn, axis=0, keepdims=True)], axis=0)
        db_ref[...] += jnp.sum(dpre, axis=0, keepdims=True)

    r = tm // 16
    in_specs = [pl.BlockSpec((tm, tc), lambda j, i: (i, c0 + j)),
                pl.BlockSpec((16, tc), lambda j, i: (jnp.maximum(i * r - 1, 0), c0 + j)),
                pl.BlockSpec((16, tc), lambda j, i: (jnp.minimum((i + 1) * r, nb * r - 1), c0 + j)),
                pl.BlockSpec((3, tc), lambda j, i: (0, j)), pl.BlockSpec((1, tc), lambda j, i: (0, j))]
    for aw in add_widths:
        nblk = aw // tc
        in_specs.append(pl.BlockSpec((tm, tc), lambda j, i, nblk=nblk: (i, jnp.minimum(j, nblk - 1))))
    return pl.pallas_call(
        body, grid=(width // tc, nb), in_specs=in_specs,
        out_specs=[pl.BlockSpec((tm, tc), lambda j, i: (i, j)), pl.BlockSpec((3, tc), lambda j, i: (0, j)), pl.BlockSpec((1, tc), lambda j, i: (0, j))],
        out_shape=[SDS((s, width), BF16), SDS((3, width), F32), SDS((1, width), F32)],
        name=name, compiler_params=_cp("parallel", "arbitrary"))(x_src, x_src, x_src, w3, bias, *addends)


def _ffn_gate_fwd(u, w3, bias, name):
    s = u.shape[0]
    tm = _pick(s, (256, 128))
    tc = _pick(D_FF, CONV_COLS)
    nb = s // tm
    nj = D_FF // tc

    def body(g_ref, gp, gn, u_ref, up, un, wg_ref, wu_ref, bg_ref, bu_ref, o_ref, ot_ref):
        i = pl.program_id(0)
        outs = []
        for (x_ref, hp, hn, w_ref, b_ref) in ((g_ref, gp, gn, wg_ref, bg_ref), (u_ref, up, un, wu_ref, bu_ref)):
            xp, x, xn = _shifted(x_ref, hp, hn, i, nb - 1)
            w = w_ref[...]
            outs.append(w[0:1, :] * xp + w[1:2, :] * x + w[2:3, :] * xn + b_ref[...])
        gate, upv = outs
        out = (gate * _sigmoid(gate) * upv).astype(o_ref.dtype)
        o_ref[...] = out
        ot_ref[...] = out.T

    def xspecs(c0):
        return [pl.BlockSpec((tm, tc), lambda i, j: (i, c0 + j))] + _halo_specs(tm, tc, c0, nb)

    in_specs = (xspecs(0) + xspecs(nj)
                + [pl.BlockSpec((3, tc), lambda i, j: (0, j)), pl.BlockSpec((3, tc), lambda i, j: (0, nj + j)),
                   pl.BlockSpec((1, tc), lambda i, j: (0, j)), pl.BlockSpec((1, tc), lambda i, j: (0, nj + j))])
    return pl.pallas_call(
        body, grid=(nb, nj), in_specs=in_specs,
        out_specs=[pl.BlockSpec((tm, tc), lambda i, j: (i, j)), pl.BlockSpec((tc, tm), lambda i, j: (j, i))],
        out_shape=[SDS((s, D_FF), BF16), SDS((D_FF, s), BF16)], name=name, compiler_params=_cp("parallel", "parallel"))(
            u, u, u, u, u, u, w3, w3, bias, bias)


def _ffn_gate_bwd(u, w3, bias, dact, name):
    s = u.shape[0]
    tm = _pick(s, (256, 128))
    tc = _pick(D_FF, CONV_COLS)
    nb = s // tm
    nj = D_FF // tc

    def body(g_ref, gp, gn, u_ref, up, un, wg_ref, wu_ref, bg_ref, bu_ref, da_ref, dg_ref, du_ref, dwg_ref, dwu_ref, dbg_ref, dbu_ref):
        i = pl.program_id(1)
        sh, pre = [], []
        for (x_ref, hp, hn, w_ref, b_ref) in ((g_ref, gp, gn, wg_ref, bg_ref), (u_ref, up, un, wu_ref, bu_ref)):
            xs3 = _shifted(x_ref, hp, hn, i, nb - 1)
            w = w_ref[...]
            sh.append(xs3)
            pre.append(w[0:1, :] * xs3[0] + w[1:2, :] * xs3[1] + w[2:3, :] * xs3[2] + b_ref[...])
        gate, upv = pre
        da = da_ref[...].astype(F32)
        sg = _sigmoid(gate)
        dgate = da * upv * (sg * (1.0 + gate * (1.0 - sg)))
        dup = da * gate * sg
        dg_ref[...] = dgate.astype(dg_ref.dtype)
        du_ref[...] = dup.astype(du_ref.dtype)

        @pl.when(i == 0)
        def _():
            for r in (dwg_ref, dwu_ref, dbg_ref, dbu_ref):
                r[...] = jnp.zeros_like(r)

        for d, xs3, dw_ref, db_ref in ((dgate, sh[0], dwg_ref, dbg_ref), (dup, sh[1], dwu_ref, dbu_ref)):
            dw_ref[...] += jnp.concatenate([jnp.sum(d * xs3[0], axis=0, keepdims=True), jnp.sum(d * xs3[1], axis=0, keepdims=True),
                                            jnp.sum(d * xs3[2], axis=0, keepdims=True)], axis=0)
            db_ref[...] += jnp.sum(d, axis=0, keepdims=True)

    r = tm // 16

    def xspecs(c0):
        return [pl.BlockSpec((tm, tc), lambda j, i: (i, c0 + j)),
                pl.BlockSpec((16, tc), lambda j, i: (jnp.maximum(i * r - 1, 0), c0 + j)),
                pl.BlockSpec((16, tc), lambda j, i: (jnp.minimum((i + 1) * r, nb * r - 1), c0 + j))]

    in_specs = (xspecs(0) + xspecs(nj)
                + [pl.BlockSpec((3, tc), lambda j, i: (0, j)), pl.BlockSpec((3, tc), lambda j, i: (0, nj + j)),
                   pl.BlockSpec((1, tc), lambda j, i: (0, j)), pl.BlockSpec((1, tc), lambda j, i: (0, nj + j)),
                   pl.BlockSpec((tm, tc), lambda j, i: (i, j))])
    blk = pl.BlockSpec((tm, tc), lambda j, i: (i, j))
    w_o = pl.BlockSpec((3, tc), lambda j, i: (0, j))
    b_o = pl.BlockSpec((1, tc), lambda j, i: (0, j))
    return pl.pallas_call(
        body, grid=(nj, nb), in_specs=in_specs, out_specs=[blk, blk, w_o, w_o, b_o, b_o],
        out_shape=[SDS((s, D_FF), BF16), SDS((s, D_FF), BF16), SDS((3, D_FF), F32), SDS((3, D_FF), F32), SDS((1, D_FF), F32), SDS((1, D_FF), F32)],
        name=name, compiler_params=_cp("parallel", "arbitrary"))(u, u, u, u, u, u, w3, w3, bias, bias, dact)


def _exchange_sems(n):
    return [pltpu.SemaphoreType.DMA((7 * n,)), pltpu.SemaphoreType.DMA((7 * n,)), pltpu.SemaphoreType.DMA((n,))]


def _exchange_copies(srcs, outs, send_sems, recv_sems, local_sems):
    x, y, c = lax.axis_index("x"), lax.axis_index("y"), lax.axis_index("c")
    me = 4 * x + 2 * y + c
    locals_ = [pltpu.make_async_copy(srcs[a].at[me], outs[a].at[me], local_sems.at[a]) for a in range(len(srcs))]
    sends, recvs = [], []
    for k in range(1, N_DEV):
        px, py, pc = x ^ ((k >> 2) & 1), y ^ ((k >> 1) & 1), c ^ (k & 1)
        peer = 4 * px + 2 * py + pc
        for a in range(len(srcs)):
            sems = dict(send_sem=send_sems.at[a * 7 + k - 1], recv_sem=recv_sems.at[a * 7 + k - 1], device_id_type=MESH)
            sends.append(pltpu.make_async_remote_copy(src_ref=srcs[a].at[peer], dst_ref=outs[a].at[me], device_id=(px, py, pc), **sems))
            recvs.append(pltpu.make_async_remote_copy(src_ref=srcs[a].at[peer], dst_ref=outs[a].at[peer], device_id=(x, y, c), **sems))
    return locals_, sends, recvs


def _exchange_start(srcs, outs, send_sems, recv_sems, local_sems):
    locals_, sends, _ = _exchange_copies(srcs, outs, send_sems, recv_sems, local_sems)
    for cp in locals_ + sends:
        cp.start()


def _exchange_wait(srcs, outs, send_sems, recv_sems, local_sems):
    locals_, sends, recvs = _exchange_copies(srcs, outs, send_sems, recv_sems, local_sems)
    for cp in recvs:
        cp.wait_recv()
    for cp in sends:
        cp.wait_send()
    for cp in locals_:
        cp.wait()


def _gather_copies(srcs, outs, send_sems, recv_sems, local_sems):
    x, y, c = lax.axis_index("x"), lax.axis_index("y"), lax.axis_index("c")
    me, sibling = (x, y, c), (x, y, 1 - c)
    chips = [(1 - x, y), (x, 1 - y), (1 - x, 1 - y)]

    def copy(a, k, block, to, src=None):
        dst = outs[a].at[4 * block[0] + 2 * block[1] + block[2]]
        return pltpu.make_async_remote_copy(
            src_ref=dst if src is None else src, dst_ref=dst,
            send_sem=send_sems.at[a * 7 + k], recv_sem=recv_sems.at[a * 7 + k], device_id=to, device_id_type=MESH)

    n = len(srcs)
    locals_ = [pltpu.make_async_copy(srcs[a], outs[a].at[4 * x + 2 * y + c], local_sems.at[a]) for a in range(n)]
    own = [copy(a, 0, me, sibling, src=srcs[a]) for a in range(n)]
    own += [copy(a, 1 + j, me, (*chip, c), src=srcs[a]) for a in range(n) for j, chip in enumerate(chips)]
    landed_ici = [copy(a, 1 + j, (*chip, c), me) for j, chip in enumerate(chips) for a in range(n)]
    passed = [copy(a, 4 + j, (*chip, c), sibling) for j, chip in enumerate(chips) for a in range(n)]
    landed_d2d = [copy(a, 0, sibling, me) for a in range(n)]
    landed_d2d += [copy(a, 4 + j, (*chip, 1 - c), me) for a in range(n) for j, chip in enumerate(chips)]
    return locals_, own, landed_ici, passed, landed_d2d


def _gather_start(*refs):
    locals_, own, _, _, _ = _gather_copies(*refs)
    for cp in locals_ + own:
        cp.start()


def _gather_forward(*refs):
    _, _, landed_ici, passed, _ = _gather_copies(*refs)
    for arrived, onward in zip(landed_ici, passed):
        arrived.wait_recv()
        onward.start()


def _gather_finish(*refs):
    locals_, own, _, passed, landed_d2d = _gather_copies(*refs)
    for cp in landed_d2d:
        cp.wait_recv()
    for cp in own + passed:
        cp.wait_send()
    for cp in locals_:
        cp.wait()


def _ssd_common(dt_ref, dtt_ref, al_r, al_c, bi_r, bi_c, off, rev):
    li = lax.broadcasted_iota(jnp.int32, (CHUNK, CHUNK), 0)
    si = lax.broadcasted_iota(jnp.int32, (CHUNK, CHUNK), 1)
    mask = (li <= si) if rev else (li >= si)
    mask_t = (li >= si) if rev else (li <= si)
    a_r = -jnp.exp(al_r[...])
    a_c = -jnp.exp(al_c[...])
    pre = dt_ref[:, off:off + N_HEADS] + bi_r[...]
    dt = _softplus(pre)
    cs = jnp.dot(mask.astype(F32), dt * a_r, precision=HIGH, preferred_element_type=F32)
    dt_t = _softplus(dtt_ref[off:off + N_HEADS, :] + bi_c[...])
    cs_t = jnp.dot(dt_t * a_c, mask_t.astype(F32), precision=HIGH, preferred_element_type=F32)
    tot = cs[0:1, :] if rev else cs[CHUNK - 1:CHUNK, :]
    return mask, mask_t, a_r, pre, dt, cs, cs_t, tot


def _ssd_fwd(xbc, dt_raw, dt_t, args_f, args_b, name, gather=()):
    s = xbc.shape[0]
    nc = s // CHUNK
    hp = D_INNER // N_HEADS
    hpg = N_HEADS // N_GROUPS
    gs = N_GROUPS * D_STATE
    ng = len(gather)
    fwd_step = (nc * 27) // 32

    def chunk(x_ref, b_ref, c_ref, dt_ref, dtt_ref, alr, alc, bir, bic, y_ref, st_ref, h_scr, off, rev):
        mask, _, _, _, dt, cs, cs_t, tot = _ssd_common(dt_ref, dtt_ref, alr, alc, bir, bic, off, rev)
        first = _first_head_lanes()
        xs = x_ref[...]
        ys = []
        for g in range(N_GROUPS):
            bg = b_ref[:, g * D_STATE:(g + 1) * D_STATE]
            cg = c_ref[:, g * D_STATE:(g + 1) * D_STATE]
            gm = lax.dot_general(cg, bg, NT, preferred_element_type=F32)
            hcat = h_scr[g]
            st_ref[0, g] = hcat
            ch = lax.dot_general(cg, hcat.astype(BF16), NT, preferred_element_type=F32)
            xdd = []
            for pr in range(hpg // 2):
                h0 = g * hpg + 2 * pr
                lanes = slice(h0 * hp, (h0 + 2) * hp)
                cols = [cs[:, h:h + 1] for h in (h0, h0 + 1)]
                pair = lambda a, b: jnp.where(first, a, b)
                xdf = xs[:, lanes].astype(F32) * pair(dt[:, h0:h0 + 1], dt[:, h0 + 1:h0 + 2])
                xdb = xdf.astype(BF16)
                yh = []
                for i, h in enumerate((h0, h0 + 1)):
                    lm = jnp.exp(jnp.where(mask, cols[i] - cs_t[h:h + 1, :], NEG))
                    yh.append(jnp.dot((gm * lm).astype(BF16), xdb, preferred_element_type=F32))
                ecs = pair(jnp.exp(cols[0]), jnp.exp(cols[1]))
                ys.append(pair(yh[0], yh[1]) + ecs * ch[:, 2 * pr * hp:(2 * pr + 2) * hp])
                dec = pair(jnp.exp(tot[:, h0:h0 + 1] - cols[0]), jnp.exp(tot[:, h0 + 1:h0 + 2] - cols[1]))
                xdd.append((xdf * dec).astype(BF16))
            snew = lax.dot_general(jnp.concatenate(xdd, axis=1), bg, TN, preferred_element_type=F32)
            for r in range(hpg):
                rs = slice(r * hp, (r + 1) * hp)
                h_scr[g, rs, :] = jnp.exp(tot[:, g * hpg + r:g * hpg + r + 1]) * hcat[rs] + snew[rs]
        y_ref[...] = jnp.concatenate(ys, axis=1).astype(y_ref.dtype)

    def body(*refs):
        in_f, in_b = refs[0:9], refs[9:18]
        g_src = refs[18:18 + ng]
        out_f, out_b = refs[18 + ng:20 + ng], refs[20 + ng:22 + ng]
        g_dst = refs[22 + ng:22 + 2 * ng]
        hs_f, hs_b = refs[22 + 2 * ng], refs[23 + 2 * ng]
        g_sems = refs[24 + 2 * ng:]
        step = pl.program_id(0)

        @pl.when(step == 0)
        def _():
            hs_f[...] = jnp.zeros_like(hs_f)
            hs_b[...] = jnp.zeros_like(hs_b)
            if ng:
                _gather_start(g_src, g_dst, *g_sems)

        chunk(*in_f, *out_f, hs_f, 0, False)
        chunk(*in_b, *out_b, hs_b, N_HEADS, True)

        if ng:
            @pl.when(step == fwd_step)
            def _():
                _gather_forward(g_src, g_dst, *g_sems)

            @pl.when(step == nc - 1)
            def _():
                _gather_finish(g_src, g_dst, *g_sems)

    small = lambda shape: pl.BlockSpec(shape, lambda c: (0, 0))

    def specs(cm):
        ins = [pl.BlockSpec((CHUNK, D_INNER), lambda c: (cm(c), 0)),
               pl.BlockSpec((CHUNK, gs), lambda c: (cm(c), D_INNER // gs)),
               pl.BlockSpec((CHUNK, gs), lambda c: (cm(c), D_INNER // gs + 1)),
               pl.BlockSpec((CHUNK, 128), lambda c: (cm(c), 0)),
               pl.BlockSpec((2 * N_HEADS, CHUNK), lambda c: (0, cm(c))),
               small((1, N_HEADS)), small((N_HEADS, 1)), small((1, N_HEADS)), small((N_HEADS, 1))]
        outs = [pl.BlockSpec((CHUNK, D_INNER), lambda c: (cm(c), 0)),
                pl.BlockSpec((1, N_GROUPS, hpg * hp, D_STATE), lambda c: (cm(c), 0, 0, 0))]
        return ins, outs

    ins_f, outs_f = specs(lambda c: c)
    ins_b, outs_b = specs(lambda c: nc - 1 - c)
    any_spec = pl.BlockSpec(memory_space=pl.ANY)
    one_dir = [SDS((s, D_INNER), BF16), SDS((nc, N_GROUPS, hpg * hp, D_STATE), F32)]
    state = pltpu.VMEM((N_GROUPS, hpg * hp, D_STATE), F32)
    return pl.pallas_call(
        body, grid=(nc,), in_specs=ins_f + ins_b + [any_spec] * ng, out_specs=outs_f + outs_b + [any_spec] * ng,
        out_shape=one_dir + one_dir + [SDS((N_DEV,) + g.shape, g.dtype) for g in gather],
        scratch_shapes=[state, state] + (_exchange_sems(ng) if ng else []), name=name, compiler_params=_cp("arbitrary"))(
            xbc, xbc, xbc, dt_raw, dt_t, *args_f, xbc, xbc, xbc, dt_raw, dt_t, *args_b, *gather)


def _ssd_bwd(xbc, dt_raw, dt_t, al_r, al_c, bi_r, bi_c, states, dy, off, rev, name, exchange=()):
    s = xbc.shape[0]
    nc = s // CHUNK
    hp = D_INNER // N_HEADS
    gs = N_GROUPS * D_STATE
    hpg = N_HEADS // N_GROUPS
    cm = (lambda c: c) if rev else (lambda c: nc - 1 - c)
    nx = len(exchange)

    def body(*refs):
        x_ref, b_ref, c_ref, dt_ref, dtt_ref, alr, alc, bir, bic, st_ref, dy_ref = refs[:11]
        xch_src = refs[11:11 + nx]
        dx_ref, ddt_ref, dal_ref, dbi_ref = refs[11 + nx:15 + nx]
        xch_dst = refs[15 + nx:15 + 2 * nx]
        dh_scr = refs[15 + 2 * nx]
        xch_sems = refs[16 + 2 * nx:]

        @pl.when(pl.program_id(0) == 0)
        def _():
            dh_scr[...] = jnp.zeros_like(dh_scr)
            dal_ref[...] = jnp.zeros_like(dal_ref)
            dbi_ref[...] = jnp.zeros_like(dbi_ref)
            if nx:
                _exchange_start(xch_src, xch_dst, *xch_sems)

        if nx:
            @pl.when(pl.program_id(0) == nc - 1)
            def _():
                _exchange_wait(xch_src, xch_dst, *xch_sems)

        mask, mask_t, a_r, pre, dt, cs, cs_t, tot = _ssd_common(dt_ref, dtt_ref, alr, alc, bir, bic, off, rev)
        xs = x_ref[...]
        dyv = dy_ref[...]
        rows = lax.broadcasted_iota(jnp.int32, (CHUNK, 1), 0)
        end_row = (rows == 0) if rev else (rows == CHUNK - 1)
        lane_h = lax.broadcasted_iota(jnp.int32, (1, N_HEADS), 1)
        sub_h = lax.broadcasted_iota(jnp.int32, (N_HEADS, 1), 0)
        first = _first_head_lanes()
        dcs_all = jnp.zeros((CHUNK, N_HEADS), F32)
        colw_all = jnp.zeros((N_HEADS, CHUNK), F32)
        dxsum_all = jnp.zeros((CHUNK, N_HEADS), F32)
        dxs, dbs, dcs_out = [], [], []
        for g in range(N_GROUPS):
            bg = b_ref[:, g * D_STATE:(g + 1) * D_STATE]
            cg = c_ref[:, g * D_STATE:(g + 1) * D_STATE]
            gm = lax.dot_general(cg, bg, NT, preferred_element_type=F32)
            gm_t = lax.dot_general(bg, cg, NT, preferred_element_type=F32)
            hcat = st_ref[0, g]
            dhcat = dh_scr[g]
            hb, dhb = hcat.astype(BF16), dhcat.astype(BF16)
            ch = lax.dot_general(cg, hb, NT, preferred_element_type=F32)
            z = lax.dot_general(bg, dhb, NT, preferred_element_type=F32)
            dg_sum = jnp.zeros((CHUNK, CHUNK), F32)
            dchs, xdds, t_hs = [], [], []
            for pr in range(hpg // 2):
                h0 = g * hpg + 2 * pr
                lanes = slice(h0 * hp, (h0 + 2) * hp)
                ps = slice(2 * pr * hp, (2 * pr + 2) * hp)
                pair = lambda a, b: jnp.where(first, a, b)
                cols = [cs[:, h:h + 1] for h in (h0, h0 + 1)]
                tots = [tot[:, h:h + 1] for h in (h0, h0 + 1)]
                xh = xs[:, lanes].astype(F32)
                dtc = pair(dt[:, h0:h0 + 1], dt[:, h0 + 1:h0 + 2])
                xdf = xh * dtc
                xd = xdf.astype(BF16)
                dyh = dyv[:, lanes]
                dyb = dyh.astype(BF16)
                ecs = pair(jnp.exp(cols[0]), jnp.exp(cols[1]))
                dec = pair(jnp.exp(tots[0] - cols[0]), jnp.exp(tots[1] - cols[1]))
                yoff_t = dyh * (ecs * ch[:, ps])
                dchs.append((ecs * dyh).astype(BF16))
                xdd = xdf * dec
                ddec_t = xdd * z[:, ps]
                row_terms = yoff_t - ddec_t
                xdds.append(xdd.astype(BF16))
                dxd_h = []
                for i, (h, keep) in enumerate(((h0, first), (h0 + 1, ~first))):
                    rs = slice((2 * pr + i) * hp, (2 * pr + i + 1) * hp)
                    lm = jnp.exp(jnp.where(mask, cols[i] - cs_t[h:h + 1, :], NEG))
                    m = gm * lm
                    t_h = jnp.exp(tots[i])
                    lm_t = jnp.exp(jnp.where(mask_t, cs_t[h:h + 1, :] - cols[i], NEG))
                    dxd_h.append(jnp.dot((gm_t * lm_t).astype(BF16), dyb, preferred_element_type=F32))
                    dm = lax.dot_general(_only(dyb, keep), xd, NT, preferred_element_type=F32)
                    dg_sum = dg_sum + dm * lm
                    w = dm * m
                    ddec_tot = jnp.sum(jnp.sum(_only(ddec_t, keep), axis=0, keepdims=True), axis=1, keepdims=True)
                    dtot = jnp.sum(jnp.sum(dhcat[rs] * hcat[rs], axis=0, keepdims=True), axis=1, keepdims=True) * t_h + ddec_tot
                    t_hs.append(t_h)
                    dcs = jnp.sum(_only(row_terms, keep) + w, axis=1, keepdims=True)
                    dcs = dcs + jnp.where(end_row, dtot, 0.0)
                    colw_all = colw_all + (sub_h == h).astype(F32) * jnp.sum(w, axis=0, keepdims=True)
                    dcs_all = dcs_all + dcs * (lane_h == h).astype(F32)
                dxd = pair(dxd_h[0], dxd_h[1]) + dec * z[:, ps]
                dxx = dxd * xh
                for i, (h, keep) in enumerate(((h0, first), (h0 + 1, ~first))):
                    dxsum_all = dxsum_all + jnp.sum(_only(dxx, keep), axis=1, keepdims=True) * (lane_h == h).astype(F32)
                dxs.append(dxd * dtc)
            dgs = dg_sum.astype(BF16)
            dchc = jnp.concatenate(dchs, axis=1)
            dcs_out.append(jnp.dot(dgs, bg, preferred_element_type=F32) + jnp.dot(dchc, hb, preferred_element_type=F32))
            dbs.append(lax.dot_general(dgs, cg, TN, preferred_element_type=F32)
                       + jnp.dot(jnp.concatenate(xdds, axis=1), dhb, preferred_element_type=F32))
            dh_in = lax.dot_general(dchc, cg, TN, preferred_element_type=F32)
            for r in range(hpg):
                rs = slice(r * hp, (r + 1) * hp)
                dh_scr[g, rs, :] = dh_in[rs] + t_hs[r] * dhcat[rs]
        dx_ref[...] = jnp.concatenate(dxs + dbs + dcs_out, axis=1).astype(dx_ref.dtype)
        mt = mask_t.astype(F32)
        da = (jnp.dot(mt, dcs_all, precision=HIGH, preferred_element_type=F32)
              - lax.dot_general(mt, colw_all, NT, precision=HIGH, preferred_element_type=F32))
        dal_ref[...] += jnp.sum(da * dt, axis=0, keepdims=True) * a_r
        ddt_raw = (da * a_r + dxsum_all) * _sigmoid(pre)
        ddt_ref[...] = ddt_raw
        dbi_ref[...] += jnp.sum(ddt_raw, axis=0, keepdims=True)

    small = lambda shape: pl.BlockSpec(shape, lambda c: (0, 0))
    in_specs = [pl.BlockSpec((CHUNK, D_INNER), lambda c: (cm(c), 0)),
                pl.BlockSpec((CHUNK, gs), lambda c: (cm(c), D_INNER // gs)),
                pl.BlockSpec((CHUNK, gs), lambda c: (cm(c), D_INNER // gs + 1)),
                pl.BlockSpec((CHUNK, 128), lambda c: (cm(c), 0)),
                pl.BlockSpec((2 * N_HEADS, CHUNK), lambda c: (0, cm(c))),
                small((1, N_HEADS)), small((N_HEADS, 1)), small((1, N_HEADS)), small((N_HEADS, 1)),
                pl.BlockSpec((1, N_GROUPS, hpg * hp, D_STATE), lambda c: (cm(c), 0, 0, 0)),
                pl.BlockSpec((CHUNK, D_INNER), lambda c: (cm(c), 0))]
    any_spec = pl.BlockSpec(memory_space=pl.ANY)
    return pl.pallas_call(
        body, grid=(nc,), in_specs=in_specs + [any_spec] * nx,
        out_specs=[pl.BlockSpec((CHUNK, XBC), lambda c: (cm(c), 0)), pl.BlockSpec((CHUNK, N_HEADS), lambda c: (cm(c), 0)),
                   small((1, N_HEADS)), small((1, N_HEADS))] + [any_spec] * nx,
        out_shape=[SDS((s, XBC), BF16), SDS((s, N_HEADS), F32), SDS((1, N_HEADS), F32), SDS((1, N_HEADS), F32)]
        + [SDS(a.shape, a.dtype) for a in exchange],
        scratch_shapes=[pltpu.VMEM((N_GROUPS, hpg * hp, D_STATE), F32)] + (_exchange_sems(nx) if nx else []),
        name=name, compiler_params=_cp("arbitrary"))(xbc, xbc, xbc, dt_raw, dt_t, al_r, al_c, bi_r, bi_c, states, dy, *exchange)


def _gate_fwd(yf, yb, xbc, proj, dskip_x, norm_w, mix, mix_t, name):
    s = yf.shape[0]
    tm = _pick(s, (256, 128))
    gw = D_INNER // N_GROUPS
    zc = 3 * D_MODEL // D_INNER

    def body(yf_ref, yb_ref, x_ref, z_ref, d_ref, w_ref, _m, _mt, o_ref, ot_ref):
        y = yf_ref[...].astype(F32) + yb_ref[...].astype(F32) + d_ref[...] * x_ref[...].astype(F32)
        z = z_ref[...].astype(F32)
        gt = y * (z * _sigmoid(z))
        outs = []
        for g in range(N_GROUPS):
            gg = gt[:, g * gw:(g + 1) * gw]
            outs.append(gg * lax.rsqrt(jnp.mean(gg * gg, axis=-1, keepdims=True) + EPS))
        out = (jnp.concatenate(outs, axis=1) * w_ref[...]).astype(o_ref.dtype)
        o_ref[...] = out
        ot_ref[...] = out.T

    row = pl.BlockSpec((tm, D_INNER), lambda i: (i, 0))
    vec = pl.BlockSpec((1, D_INNER), lambda i: (0, 0))
    any_spec = pl.BlockSpec(memory_space=pl.ANY)
    cb = (mix.shape[1] - D_INNER) // D_INNER
    return pl.pallas_call(
        body, grid=(s // tm,), in_specs=[row, row, row, pl.BlockSpec((tm, D_INNER), lambda i: (i, zc)), vec, vec, any_spec, any_spec],
        out_specs=[pl.BlockSpec((tm, D_INNER), lambda i: (i, cb)), pl.BlockSpec((D_INNER, tm), lambda i: (cb, i))],
        out_shape=[SDS(mix.shape, BF16), SDS(mix_t.shape, BF16)], input_output_aliases={6: 0, 7: 1},
        name=name, compiler_params=_cp("parallel"))(yf, yb, xbc, proj, dskip_x, norm_w, mix, mix_t)


def _gate_bwd(yf, yb, xbc, proj, dskip_x, norm_w, dmix, dproj, name):
    s = yf.shape[0]
    tm = _pick(s, (256, 128))
    gw = D_INNER // N_GROUPS
    zc = 3 * D_MODEL // D_INNER

    def body(yf_ref, yb_ref, x_ref, z_ref, d_ref, w_ref, do_ref, _, dy_ref, dz_ref, dxs_ref, dw_ref, dd_ref):
        xf = x_ref[...].astype(F32)
        y = yf_ref[...].astype(F32) + yb_ref[...].astype(F32) + d_ref[...] * xf
        z = z_ref[...].astype(F32)
        sg = _sigmoid(z)
        sz = z * sg
        gt = y * sz
        do = do_ref[...].astype(F32)
        dgh = do * w_ref[...]
        ghs, dgts = [], []
        for g in range(N_GROUPS):
            gg = gt[:, g * gw:(g + 1) * gw]
            r = lax.rsqrt(jnp.mean(gg * gg, axis=-1, keepdims=True) + EPS)
            gh = gg * r
            dg = dgh[:, g * gw:(g + 1) * gw]
            ghs.append(gh)
            dgts.append(r * (dg - gh * jnp.mean(dg * gh, axis=-1, keepdims=True)))
        ghat = jnp.concatenate(ghs, axis=1)
        dgt = jnp.concatenate(dgts, axis=1)
        dy = dgt * sz
        dy_ref[...] = dy
        dz_ref[...] = (dgt * y * (sg * (1.0 + z * (1.0 - sg)))).astype(dz_ref.dtype)
        dxs_ref[...] = (dy * d_ref[...]).astype(dxs_ref.dtype)

        @pl.when(pl.program_id(0) == 0)
        def _():
            dw_ref[...] = jnp.zeros_like(dw_ref)
            dd_ref[...] = jnp.zeros_like(dd_ref)

        dw_ref[...] += jnp.sum(do * ghat, axis=0, keepdims=True)
        dd_ref[...] += jnp.sum(dy * xf, axis=0, keepdims=True)

    row = pl.BlockSpec((tm, D_INNER), lambda i: (i, 0))
    vec = pl.BlockSpec((1, D_INNER), lambda i: (0, 0))
    dz_shape, _, more, more_specs, alias = _slab(s, D_INNER, BF16, (dproj.shape[1], zc * D_INNER, dproj), 7, out_idx=1)
    return pl.pallas_call(
        body, grid=(s // tm,),
        in_specs=[row, row, row, pl.BlockSpec((tm, D_INNER), lambda i: (i, zc)), vec, vec, pl.BlockSpec((tm, D_INNER), lambda i: (i, 1))] + more_specs,
        out_specs=[row, pl.BlockSpec((tm, D_INNER), lambda i: (i, zc)), row, vec, vec],
        out_shape=[SDS((s, D_INNER), F32), dz_shape, SDS((s, D_INNER), BF16), SDS((1, D_INNER), F32), SDS((1, D_INNER), F32)],
        input_output_aliases=alias, name=name, compiler_params=_cp("arbitrary"))(yf, yb, xbc, proj, dskip_x, norm_w, dmix, *more)


def _adamw(parts, w, m, v, name):
    r, c = w.shape
    tr = _pick(r, (256, 352, 128))

    def body(p_ref, w_ref, m_ref, v_ref, g_ref, d_ref, nm_ref, nv_ref):
        g = p_ref[0].astype(F32)
        for i in range(1, N_DEV):
            g = g + p_ref[i].astype(F32)
        mn = B1 * m_ref[...] + (1.0 - B1) * g
        vn = B2 * v_ref[...] + (1.0 - B2) * (g * g)
        m_hat = mn / (1.0 - B1 ** STEP)
        v_hat = vn / (1.0 - B2 ** STEP)
        g_ref[...] = g
        d_ref[...] = -LR * (m_hat / (jnp.sqrt(v_hat) + AEPS) + WD * w_ref[...])
        nm_ref[...] = mn
        nv_ref[...] = vn

    blk = pl.BlockSpec((tr, c), lambda i: (i, 0))
    return pl.pallas_call(
        body, grid=(r // tr,), in_specs=[pl.BlockSpec((N_DEV, tr, c), lambda i: (0, i, 0)), blk, blk, blk],
        out_specs=[blk, blk, blk, blk], out_shape=[SDS((r, c), F32)] * 4, name=name, compiler_params=_cp("parallel"))(parts, w, m, v)


def _sum_parts(parts, name):
    _, r, c = parts.shape

    def body(p_ref, o_ref):
        g = p_ref[0]
        for i in range(1, N_DEV):
            g = g + p_ref[i]
        o_ref[...] = g

    return pl.pallas_call(body, out_shape=SDS((r, c), F32), name=name)(parts)


def _adamw_small(gs, ws, ms, vs, name):
    n = len(gs)

    def body(*refs):
        g_refs, w_refs, m_refs, v_refs = refs[:n], refs[n:2 * n], refs[2 * n:3 * n], refs[3 * n:4 * n]
        d_refs, nm_refs, nv_refs = refs[4 * n:5 * n], refs[5 * n:6 * n], refs[6 * n:7 * n]
        for i in range(n):
            gv = g_refs[i][...]
            mn = B1 * m_refs[i][...] + (1.0 - B1) * gv
            vn = B2 * v_refs[i][...] + (1.0 - B2) * (gv * gv)
            m_hat = mn / (1.0 - B1 ** STEP)
            v_hat = vn / (1.0 - B2 ** STEP)
            d_refs[i][...] = -LR * (m_hat / (jnp.sqrt(v_hat) + AEPS) + WD * w_refs[i][...])
            nm_refs[i][...] = mn
            nv_refs[i][...] = vn

    outs = pl.pallas_call(body, out_shape=[SDS(g.shape, F32) for g in gs] * 3, name=name)(*gs, *ws, *ms, *vs)
    return outs[:n], outs[n:2 * n], outs[2 * n:]


def _my_index():
    return 4 * lax.axis_index("x") + 2 * lax.axis_index("y") + lax.axis_index("c")


def _all_gather(shards, name):
    n = len(shards)

    def body(*refs):
        srcs, outs = refs[:n], refs[n:2 * n]
        _gather_start(srcs, outs, *refs[2 * n:])
        _gather_forward(srcs, outs, *refs[2 * n:])
        _gather_finish(srcs, outs, *refs[2 * n:])

    any_spec = pl.BlockSpec(memory_space=pl.ANY)
    return pl.pallas_call(
        body, in_specs=[any_spec] * n, out_specs=[any_spec] * n,
        out_shape=[SDS((N_DEV,) + s.shape, s.dtype) for s in shards], scratch_shapes=_exchange_sems(n), name=name)(*shards)


def _exchange(arrays, name):
    n = len(arrays)

    def body(*refs):
        srcs, outs = refs[:n], refs[n:2 * n]
        _exchange_start(srcs, outs, *refs[2 * n:])
        _exchange_wait(srcs, outs, *refs[2 * n:])

    any_spec = pl.BlockSpec(memory_space=pl.ANY)
    return pl.pallas_call(
        body, in_specs=[any_spec] * n, out_specs=[any_spec] * n,
        out_shape=[SDS(a.shape, a.dtype) for a in arrays], scratch_shapes=_exchange_sems(n), name=name)(*arrays)


def _to_pattern(t, d):
    if d == 1:
        return t
    s, w = t.shape
    return t.reshape(s // d, d, w).transpose(1, 0, 2).reshape(s, w)


def _from_pattern(t, d):
    if d == 1:
        return t
    s, w = t.shape
    return t.reshape(d, s // d, w).transpose(1, 0, 2).reshape(s, w)


def _pad_lanes(t, n):
    return jnp.pad(t, ((0, 0), (0, n - t.shape[1])))


def _to_shards(g, axis):
    r, c = g.shape
    if axis == 0:
        return g.reshape(N_DEV, r // N_DEV, c)
    return g.reshape(r, N_DEV, c // N_DEV).transpose(1, 0, 2)


def _local_step(x, target, p, first_shards=None, late_shards=(), early_exchange=True):
    s = x.shape[0]
    tabs_f, tabs_b = _rope_tables(s)
    expand = jnp.asarray(np.repeat(np.eye(N_HEADS, dtype=np.float32), HEAD_DIM, axis=1))
    al_r = {"f": p["a_log_f"], "b": p["a_log_b"]}
    bi_r = {"f": p["dt_bias_f"], "b": p["dt_bias_b"]}
    dskip_x = jnp.repeat(p["d_skip"], D_INNER // N_HEADS, axis=1)

    h1, h1t, *got = _rmsnorm_fwd(x, p["norm1_w"], "norm1_fwd", gather=first_shards[0] if first_shards else ())
    if first_shards:
        p = dict(p, **first_shards[1](got))
    w_main, w_dt = p["w_in"][:, :MAIN_W], _pad_lanes(p["w_in"][:, MAIN_W:], 128)
    ssm_w3, ffn_w3 = p["ssm_conv_w"].T, p["ffn_conv_w"].T
    proj = _matmul(h1, w_main, name="in_proj")
    dt_raw = _matmul(h1, w_dt, name="in_proj_dt", out_dtype=F32)
    dt_t = dt_raw[:, :2 * N_HEADS].T
    perms = {d: _perm_matrices(d) for d in DILATIONS[1:]}
    qkv = _rope_fwd(proj, tabs_f, perms, "rope_fwd")
    v_col = 2
    os_, lses = [], []
    for d, qkv_p in zip(DILATIONS, qkv):
        o_p, lse_p = _attn_fwd(qkv_p, qkv_p, v_col, s // d, f"attn_fwd_d{d}")
        os_.append(o_p)
        lses.append(_from_pattern(lse_p, d))
    mix, mix_t, lse_tot = _attn_combine(os_, lses, perms, expand, "attn_combine")

    xbc = _conv(proj, 3 * D_MODEL + D_INNER, XBC, ssm_w3, p["ssm_conv_b"], True, "ssm_conv_fwd")
    col = lambda r: r.reshape(N_HEADS, 1)
    ssd_args = {k: (al_r[k], col(al_r[k]), bi_r[k], col(bi_r[k])) for k in ("f", "b")}
    yf, st_f, yb, st_b, *got = _ssd_fwd(xbc, dt_raw, dt_t, ssd_args["f"], ssd_args["b"], "ssd_fwd", gather=late_shards)
    if late_shards:
        p = dict(p, w_out=got[0].reshape(2 * D_MODEL, D_MODEL), w_down=got[2].reshape(D_FF, D_MODEL),
                 w_up=got[1].transpose(1, 0, 2).reshape(D_MODEL, 2 * D_FF))
    mix, mix_t = _gate_fwd(yf, yb, xbc, proj, dskip_x, p["ssm_norm_w"], mix, mix_t, "ssm_gate_fwd")

    x2 =_matmul(mix, p["w_out"], name="out_proj", out_dtype=F32, residual=x)
    h2, h2t = _rmsnorm_fwd(x2, p["norm2_w"], "norm2_fwd")
    u = _matmul(h2, p["w_up"], name="ffn_up")
    act, act_t = _ffn_gate_fwd(u, ffn_w3, p["ffn_conv_b"], "ffn_gate_fwd")
    x3 = _matmul(act, p["w_down"], name="ffn_down", out_dtype=F32, residual=x2)

    dx3, dx3b, g_final, loss = _final_norm_loss(x3, p["final_norm_w"].reshape(1, D_MODEL), target, "final_norm_loss")
    g_w_down = _matmul(act_t, dx3b, name="dw_down")
    dact = _matmul(dx3b, p["w_down"], name="d_act", trans_b=True)
    dug, duu, dwg, dwu, dbg, dbu = _ffn_gate_bwd(u, ffn_w3, p["ffn_conv_b"], dact, "ffn_gate_bwd")
    du = _conv(dug, 0, D_FF, ffn_w3, p["ffn_conv_b"], False, "ffn_conv_bwd_gate", slab=(2 * D_FF, 0, None), transpose=True)
    du = _conv(duu, 0, D_FF, ffn_w3, p["ffn_conv_b"], False, "ffn_conv_bwd_up", slab=(2 * D_FF, D_FF, du), transpose=True, wcol0=D_FF)
    g_w_up = _matmul(h2t, du, name="dw_up")
    none_a, none_w = jnp.zeros((s, 128), BF16), jnp.zeros((D_MODEL, 128), BF16)
    dx2, dx2b, g_norm2 = _proj_norm_bwd(du, p["w_up"], none_a, none_w, x2, p["norm2_w"], dx3, "ffn_up_norm2_bwd")
    g_w_out = _matmul(mix_t, dx2b, name="dw_out")
    dmix = _matmul(dx2b, p["w_out"], name="d_mix", trans_b=True)

    delta, do_pat = _attn_delta(dmix, mix, expand.T, perms, "attn_delta")
    dqs, dks, dvs = [], [], []
    for d, qkv_p, do_p in zip(DILATIONS, qkv, [dmix] + do_pat):
        lse_p, dl_p = _to_pattern(lse_tot, d), _to_pattern(delta, d)
        dqs.append(_attn_bwd_dq(qkv_p, qkv_p, v_col, do_p, lse_p, dl_p, s // d, f"attn_bwd_dq_d{d}"))
        dk, dv = _attn_bwd_dkv(qkv_p, qkv_p, v_col, do_p, lse_p.T, dl_p.T, s // d, f"attn_bwd_dkv_d{d}")
        dks.append(dk)
        dvs.append(dv)
    dproj = _sum3_rope(dqs, perms, tabs_b, "rope_bwd_q", slab=(MAIN_W, 0, None))
    dproj = _sum3_rope(dks, perms, tabs_b, "rope_bwd_k", slab=(MAIN_W, D_MODEL, dproj))
    dproj = _sum3_rope(dvs, perms, None, "sum_dv", slab=(MAIN_W, 2 * D_MODEL, dproj))

    dy, dproj, dxs_skip, g_ssm_norm, g_dskip_lanes = _gate_bwd(yf, yb, xbc, proj, dskip_x, p["ssm_norm_w"], dmix, dproj, "ssm_gate_bwd")
    early_f = [_to_shards(g_w_up, 1), _to_shards(g_w_down, 0)] if early_exchange else []
    early_b = [_to_shards(g_w_out, 0)] if early_exchange else []
    dxbc_f, ddt_f, g_al_f, g_bi_f, *got_f = _ssd_bwd(xbc, dt_raw, dt_t, *ssd_args["f"], st_f, dy, 0, False, "ssd_bwd_f", exchange=early_f)
    dxbc_b, ddt_b, g_al_b, g_bi_b, *got_b = _ssd_bwd(xbc, dt_raw, dt_t, *ssd_args["b"], st_b, dy, N_HEADS, True, "ssd_bwd_b",
                                                     exchange=early_b)
    if early_exchange:
        (g_w_up, g_w_down), (g_w_out,) = got_f, got_b
    dpre, g_ssm_w3, g_ssm_cb = _conv_silu_bwd(proj, 3 * D_MODEL + D_INNER, XBC, ssm_w3, p["ssm_conv_b"],
                                              [dxbc_f, dxbc_b, dxs_skip], [XBC, XBC, D_INNER], "ssm_conv_bwd")
    dproj = _conv(dpre, 0, XBC, ssm_w3, p["ssm_conv_b"], False, "ssm_conv_bwd_x", transpose=True,
                  slab=(MAIN_W, 3 * D_MODEL + D_INNER, dproj))

    ddt =_pad_lanes(jnp.concatenate([ddt_f, ddt_b], axis=1), 128).astype(BF16)
    g_w_main = _matmul(h1t, dproj, name="dw_in")
    g_w_dt = _matmul(h1t, ddt, name="dw_in_dt")
    g_w_in = jnp.concatenate([g_w_main, g_w_dt[:, :2 * N_HEADS]], axis=1)
    late = [_to_shards(g_w_in, 1)] if early_exchange else []
    grad_x, _, g_norm1, *got = _proj_norm_bwd(dproj, w_main, ddt, w_dt, x, p["norm1_w"], dx2, "in_proj_norm1_bwd", exchange=late)
    if early_exchange:
        g_w_in = got[0]

    g_dskip = jnp.sum(g_dskip_lanes.reshape(N_HEADS, D_INNER // N_HEADS), axis=1).reshape(1, N_HEADS)
    small = {
        "norm1_w": g_norm1, "ssm_conv_w": g_ssm_w3.T, "ssm_conv_b": g_ssm_cb, "a_log_f": g_al_f, "a_log_b": g_al_b,
        "dt_bias_f": g_bi_f, "dt_bias_b": g_bi_b, "d_skip": g_dskip, "ssm_norm_w": g_ssm_norm, "norm2_w": g_norm2,
        "ffn_conv_w": jnp.concatenate([dwg, dwu], axis=1).T, "ffn_conv_b": jnp.concatenate([dbg, dbu], axis=1), "final_norm_w": g_final,
    }
    big = {"w_in": g_w_in, "w_out": g_w_out, "w_up": g_w_up, "w_down": g_w_down}
    return loss[0, 0], grad_x, big, small


SMALL_ORDER = ("norm1_w", "ssm_conv_w", "ssm_conv_b", "a_log_f", "a_log_b", "dt_bias_f", "dt_bias_b", "d_skip",
               "ssm_norm_w", "norm2_w", "ffn_conv_w", "ffn_conv_b", "final_norm_w")
SHARDED_SMALL = ("ssm_conv_w", "ffn_conv_w")
BIG_ORDER = ("w_in", "w_out", "w_up", "w_down")


def _pack(vals):
    rows = []
    for v in vals:
        f = v.reshape(-1).astype(F32)
        n = -(-f.shape[0] // 128) * 128
        rows.append(jnp.pad(f, (0, n - f.shape[0])).reshape(-1, 128))
    out = jnp.concatenate(rows, axis=0)
    pad = -out.shape[0] % 8
    return jnp.pad(out, ((0, pad), (0, 0)))


def _unpack(packed, shapes):
    out, r = [], 0
    for shp in shapes:
        n = math.prod(shp)
        nr = -(-n // 128)
        out.append(packed[r:r + nr].reshape(-1)[:n].reshape(shp))
        r += nr
    return out


def kernel(x, norm1_w, w_in, ssm_conv_w, ssm_conv_b, a_log_f, a_log_b, dt_bias_f, dt_bias_b, d_skip, ssm_norm_w, w_out, norm2_w, w_up, ffn_conv_w, ffn_conv_b, w_down, final_norm_w, loss_target, m_norm1_w, m_w_in, m_ssm_conv_w, m_ssm_conv_b, m_a_log_f, m_a_log_b, m_dt_bias_f, m_dt_bias_b, m_d_skip, m_ssm_norm_w, m_w_out, m_norm2_w, m_w_up, m_ffn_conv_w, m_ffn_conv_b, m_w_down, m_final_norm_w, v_norm1_w, v_w_in, v_ssm_conv_w, v_ssm_conv_b, v_a_log_f, v_a_log_b, v_dt_bias_f, v_dt_bias_b, v_d_skip, v_ssm_norm_w, v_w_out, v_norm2_w, v_w_up, v_ffn_conv_w, v_ffn_conv_b, v_w_down, v_final_norm_w):
    w = dict(norm1_w=norm1_w, w_in=w_in, ssm_conv_w=ssm_conv_w, ssm_conv_b=ssm_conv_b, a_log_f=a_log_f, a_log_b=a_log_b,
             dt_bias_f=dt_bias_f, dt_bias_b=dt_bias_b, d_skip=d_skip, ssm_norm_w=ssm_norm_w, w_out=w_out, norm2_w=norm2_w,
             w_up=w_up, ffn_conv_w=ffn_conv_w, ffn_conv_b=ffn_conv_b, w_down=w_down, final_norm_w=final_norm_w)
    mo = dict(norm1_w=m_norm1_w, w_in=m_w_in, ssm_conv_w=m_ssm_conv_w, ssm_conv_b=m_ssm_conv_b, a_log_f=m_a_log_f, a_log_b=m_a_log_b,
              dt_bias_f=m_dt_bias_f, dt_bias_b=m_dt_bias_b, d_skip=m_d_skip, ssm_norm_w=m_ssm_norm_w, w_out=m_w_out, norm2_w=m_norm2_w,
              w_up=m_w_up, ffn_conv_w=m_ffn_conv_w, ffn_conv_b=m_ffn_conv_b, w_down=m_w_down, final_norm_w=m_final_norm_w)
    vo = dict(norm1_w=v_norm1_w, w_in=v_w_in, ssm_conv_w=v_ssm_conv_w, ssm_conv_b=v_ssm_conv_b, a_log_f=v_a_log_f, a_log_b=v_a_log_b,
              dt_bias_f=v_dt_bias_f, dt_bias_b=v_dt_bias_b, d_skip=v_d_skip, ssm_norm_w=v_ssm_norm_w, w_out=v_w_out, norm2_w=v_norm2_w,
              w_up=v_w_up, ffn_conv_w=v_ffn_conv_w, ffn_conv_b=v_ffn_conv_b, w_down=v_w_down, final_norm_w=v_final_norm_w)
    me = _my_index()

    def first_full(got):
        g_in, g_conv = got
        conv_rows = [_unpack(g_conv[i], [ssm_conv_w.shape[1:], ffn_conv_w.shape[1:]]) for i in range(N_DEV)]
        return {"w_in": g_in.transpose(1, 0, 2).reshape(D_MODEL, N_DEV * w_in.shape[2]),
                "ssm_conv_w": jnp.concatenate([c[0] for c in conv_rows], axis=0),
                "ffn_conv_w": jnp.concatenate([c[1] for c in conv_rows], axis=0)}

    first = ([w["w_in"][0].astype(BF16), _pack([w["ssm_conv_w"][0], w["ffn_conv_w"][0]])], first_full)
    full = {k: w[k] for k in ("norm1_w", "ssm_conv_b", "a_log_f", "a_log_b", "dt_bias_f", "dt_bias_b", "d_skip", "ssm_norm_w", "norm2_w",
                              "ffn_conv_b", "final_norm_w")}
    late = [w["w_out"][0].astype(BF16), w["w_up"][0].astype(BF16), w["w_down"][0].astype(BF16)]

    loss_part, grad_x, big, small = _local_step(x[0], loss_target[0], full, first, late)

    small_shapes = [(1,)] + [small[k].shape for k in SMALL_ORDER]
    packed = _pack([loss_part] + [small[k] for k in SMALL_ORDER])
    out_small = jnp.broadcast_to(packed[None], (N_DEV,) + packed.shape)
    (r_small,) = _exchange([out_small], "small_grads_exchange")
    r_in, r_out, r_up, r_down = big["w_in"], big["w_out"], big["w_up"], big["w_down"]

    outs_g, outs_d, outs_m, outs_v = {}, {}, {}, {}
    for k, parts in zip(BIG_ORDER, (r_in, r_out, r_up, r_down)):
        g, dlt, nm, nv = _adamw(parts, w[k][0], mo[k][0], vo[k][0], f"adamw_{k}")
        outs_g[k], outs_d[k], outs_m[k], outs_v[k] = g[None], dlt[None], nm[None], nv[None]
    tot = _unpack(_sum_parts(r_small, "small_grads_sum"), small_shapes)
    loss = tot[0][0]
    gs = dict(zip(SMALL_ORDER, tot[1:]))
    g_own = {}
    for k in SMALL_ORDER:
        if k in SHARDED_SMALL:
            rows = w[k].shape[1]
            g_own[k] = lax.dynamic_slice_in_dim(gs[k], me * rows, rows, axis=0)[None]
        else:
            g_own[k] = gs[k].reshape(w[k].shape)
    two_d = lambda a: a.reshape(-1, a.shape[-1])
    d_s, m_s, v_s = _adamw_small([two_d(g_own[k]) for k in SMALL_ORDER], [two_d(w[k]) for k in SMALL_ORDER],
                                 [two_d(mo[k]) for k in SMALL_ORDER], [two_d(vo[k]) for k in SMALL_ORDER], "adamw_small")
    for k, a, b, c in zip(SMALL_ORDER, d_s, m_s, v_s):
        shp = w[k].shape
        outs_g[k], outs_d[k], outs_m[k], outs_v[k] = g_own[k], a.reshape(shp), b.reshape(shp), c.reshape(shp)

    order = ("norm1_w", "w_in", "ssm_conv_w", "ssm_conv_b", "a_log_f", "a_log_b", "dt_bias_f", "dt_bias_b", "d_skip", "ssm_norm_w",
             "w_out", "norm2_w", "w_up", "ffn_conv_w", "ffn_conv_b", "w_down", "final_norm_w")
    return (loss, grad_x[None], *[outs_g[k] for k in order], *[outs_d[k] for k in order],
            *[outs_m[k] for k in order], *[outs_v[k] for k in order])
```

```python
import math

import numpy as np
import jax
import jax.numpy as jnp
from jax import lax
from jax.experimental import pallas as pl
from jax.experimental.pallas import tpu as pltpu

F32 = jnp.float32
BF16 = jnp.bfloat16
SDS = jax.ShapeDtypeStruct

N_DEV = 8
D_MODEL = 1024
N_HEADS = 16
HEAD_DIM = 64
ROPE_DIM = 16
ROPE_THETA = 500000.0
DILATIONS = (1, 4, 16)
BAND_HALF = 64
D_INNER = 1024
N_GROUPS = 4
D_STATE = 128
CHUNK = 128
XBC = D_INNER + 2 * N_GROUPS * D_STATE
D_FF = 2816
MAIN_W = 3 * D_MODEL + D_INNER + XBC
EPS = 1e-6
LR, B1, B2, AEPS, WD, STEP = 0.001, 0.9, 0.999, 1e-08, 0.01, 10
NEG = -1e30
VMEM_LIMIT = 56 * 1024 * 1024
MESH = pl.DeviceIdType.MESH
HIGH = lax.Precision.HIGHEST
NT = (((1,), (1,)), ((), ()))
TN = (((0,), (0,)), ((), ()))


def _cp(*sem):
    return pltpu.CompilerParams(dimension_semantics=sem, vmem_limit_bytes=VMEM_LIMIT)


def _pick(n, cands):
    for c in cands:
        if n % c == 0:
            return c
    raise ValueError(f"no tile for {n}")


def _sigmoid(x):
    return 1.0 / (1.0 + jnp.exp(-x))


def _dot_01(x, m01):
    mb = m01.astype(BF16)
    out, r = None, x
    for _ in range(3):
        p = r.astype(BF16)
        r = r - p.astype(F32)
        t = jnp.dot(p, mb, preferred_element_type=F32)
        out = t if out is None else out + t
    return out


def _softplus(x):
    return jnp.maximum(x, 0.0) + jnp.log1p(jnp.exp(-jnp.abs(x)))


def _slab(s, width, dtype, slab, n_in, out_idx=0):
    if slab is None:
        return SDS((s, width), dtype), 0, [], [], {}
    total, col0, into = slab
    if into is None:
        return SDS((s, total), dtype), col0, [], [], {}
    return SDS((s, total), dtype), col0, [into], [pl.BlockSpec(memory_space=pl.ANY)], {n_in: out_idx}


def _matmul(a, b, *, name, trans_b=False, out_dtype=BF16, residual=None):
    m, k = a.shape
    n = b.shape[0] if trans_b else b.shape[1]
    tk = k if k <= 2048 else _pick(k, (2048, 1408, 1024, 512))
    nk = k // tk
    if nk == 1:
        tm = _pick(m, (2048, 1408, 1024, 512, 256, 128))
        tn = _pick(n, (512, 256, 128))
    else:
        tm = _pick(m, (1024, 1408, 512, 256, 128))
        tn = _pick(n, (1024, 1408, 512, 256, 128))
    dn = NT if trans_b else (((1,), (0,)), ((), ()))

    def body(*refs):
        a_ref, b_ref = refs[0], refs[1]
        o_ref, acc = refs[-2], refs[-1]
        kk = pl.program_id(2)

        @pl.when(kk == 0)
        def _():
            acc[...] = jnp.zeros_like(acc)

        acc[...] += lax.dot_general(a_ref[...], b_ref[...], dn, preferred_element_type=F32)

        @pl.when(kk == nk - 1)
        def _():
            r = acc[...]
            if residual is not None:
                r = r + refs[2][...].astype(F32)
            o_ref[...] = r.astype(o_ref.dtype)

    in_specs = [pl.BlockSpec((tm, tk), lambda i, j, kk: (i, kk)),
                pl.BlockSpec((tn, tk), lambda i, j, kk: (j, kk)) if trans_b else pl.BlockSpec((tk, tn), lambda i, j, kk: (kk, j))]
    args = [a, b]
    if residual is not None:
        in_specs.append(pl.BlockSpec((tm, tn), lambda i, j, kk: (i, j)))
        args.append(residual)
    return pl.pallas_call(
        body, grid=(m // tm, n // tn, nk), in_specs=in_specs,
        out_specs=pl.BlockSpec((tm, tn), lambda i, j, kk: (i, j)),
        out_shape=SDS((m, n), out_dtype), scratch_shapes=[pltpu.VMEM((tm, tn), F32)],
        name=name, compiler_params=_cp("parallel", "parallel", "arbitrary"))(*args)


def _rmsnorm_fwd(x, w, name, gather=()):
    s, d = x.shape
    tm = _pick(s, (512, 128))
    n = s // tm
    ng = len(gather)

    def body(*refs):
        x_ref, w_ref = refs[:2]
        g_src = refs[2:2 + ng]
        o_ref, ot_ref = refs[2 + ng:4 + ng]
        g_dst = refs[4 + ng:4 + 2 * ng]
        g_sems = refs[4 + 2 * ng:]
        step = pl.program_id(0)
        if ng:
            @pl.when(step == 0)
            def _():
                _gather_start(g_src, g_dst, *g_sems)

        xf = x_ref[...]
        r = lax.rsqrt(jnp.mean(xf * xf, axis=-1, keepdims=True) + EPS)
        out = (xf * r * w_ref[...]).astype(o_ref.dtype)
        o_ref[...] = out
        ot_ref[...] = out.T

        if ng:
            @pl.when(step == n - 1)
            def _():
                _gather_forward(g_src, g_dst, *g_sems)
                _gather_finish(g_src, g_dst, *g_sems)

    any_spec = pl.BlockSpec(memory_space=pl.ANY)
    return pl.pallas_call(
        body, grid=(n,), in_specs=[pl.BlockSpec((tm, d), lambda i: (i, 0)), pl.BlockSpec((1, d), lambda i: (0, 0))] + [any_spec] * ng,
        out_specs=[pl.BlockSpec((tm, d), lambda i: (i, 0)), pl.BlockSpec((d, tm), lambda i: (0, i))] + [any_spec] * ng,
        out_shape=[SDS((s, d), BF16), SDS((d, s), BF16)] + [SDS((N_DEV,) + g.shape, g.dtype) for g in gather],
        scratch_shapes=_exchange_sems(ng) if ng else [],
        name=name, compiler_params=_cp("arbitrary" if ng else "parallel"))(x, w, *gather)


def _proj_norm_bwd(da, wt, da2, wt2, x, w, dres, name, exchange=()):
    s, k = da.shape
    d = x.shape[1]
    tm = _pick(s, (512, 128))
    tk = _pick(k, (1408, 1024, 512))
    nk, ni = k // tk, s // tm
    nx = len(exchange)
    last = nk - 1

    def body(*refs):
        a_ref, b_ref, a2_ref, b2_ref, x_ref, w_ref, dres_ref = refs[:7]
        xch_src = refs[7:7 + nx]
        dx_ref, dxb_ref, dw_ref = refs[7 + nx:10 + nx]
        xch_dst = refs[10 + nx:10 + 2 * nx]
        acc = refs[10 + 2 * nx]
        xch_sems = refs[11 + 2 * nx:]
        kk, i = pl.program_id(0), pl.program_id(1)

        @pl.when((i == 0) & (kk == 0))
        def _():
            dw_ref[...] = jnp.zeros_like(dw_ref)
            if nx:
                _exchange_start(xch_src, xch_dst, *xch_sems)

        @pl.when(kk == 0)
        def _():
            acc[i] = lax.dot_general(a2_ref[...], b2_ref[...], NT, preferred_element_type=F32)

        acc[i] += lax.dot_general(a_ref[...], b_ref[...], NT, preferred_element_type=F32)

        @pl.when(kk == last)
        def _():
            dh = acc[i]
            xf = x_ref[...]
            r = lax.rsqrt(jnp.mean(xf * xf, axis=-1, keepdims=True) + EPS)
            xhat = xf * r
            g = dh * w_ref[...]
            dx = dres_ref[...] + r * (g - xhat * jnp.mean(g * xhat, axis=-1, keepdims=True))
            dx_ref[...] = dx
            dxb_ref[...] = dx.astype(dxb_ref.dtype)
            dw_ref[...] += jnp.sum(dh * xhat, axis=0, keepdims=True)

        if nx:
            @pl.when((i == ni - 1) & (kk == last))
            def _():
                _exchange_wait(xch_src, xch_dst, *xch_sems)

    row_last = pl.BlockSpec((tm, d), lambda kk, i: (jnp.where(kk == last, i, 0), 0))
    vec = pl.BlockSpec((1, d), lambda kk, i: (0, 0))
    any_spec = pl.BlockSpec(memory_space=pl.ANY)
    in_specs = [pl.BlockSpec((tm, tk), lambda kk, i: (i, kk)), pl.BlockSpec((d, tk), lambda kk, i: (0, kk)),
                pl.BlockSpec((tm, da2.shape[1]), lambda kk, i: (jnp.where(kk == 0, i, 0), 0)),
                pl.BlockSpec((d, wt2.shape[1]), lambda kk, i: (0, 0)),
                row_last, vec, row_last]
    return pl.pallas_call(
        body, grid=(nk, ni), in_specs=in_specs + [any_spec] * nx, out_specs=[row_last, row_last, vec] + [any_spec] * nx,
        out_shape=[SDS((s, d), F32), SDS((s, d), BF16), SDS((1, d), F32)] + [SDS(e.shape, e.dtype) for e in exchange],
        scratch_shapes=[pltpu.VMEM((ni, tm, d), F32)] + (_exchange_sems(nx) if nx else []),
        name=name, compiler_params=_cp("arbitrary", "arbitrary"))(da, wt, da2, wt2, x, w, dres, *exchange)


def _final_norm_loss(x, w, target, name):
    s, d = x.shape
    tm = _pick(s, (512, 128))

    def body(x_ref, w_ref, t_ref, dx_ref, dxb_ref, dw_ref, loss_ref):
        xf = x_ref[...]
        r = lax.rsqrt(jnp.mean(xf * xf, axis=-1, keepdims=True) + EPS)
        xhat = xf * r
        wv = w_ref[...]
        e = xhat * wv - t_ref[...]
        dy = e * (1.0 / d)
        g = dy * wv
        dx = r * (g - xhat * jnp.mean(g * xhat, axis=-1, keepdims=True))
        dx_ref[...] = dx
        dxb_ref[...] = dx.astype(dxb_ref.dtype)

        @pl.when(pl.program_id(0) == 0)
        def _():
            dw_ref[...] = jnp.zeros_like(dw_ref)
            loss_ref[...] = jnp.zeros_like(loss_ref)

        dw_ref[...] += jnp.sum(dy * xhat, axis=0, keepdims=True)
        loss_ref[...] += jnp.sum(jnp.sum(e * e, axis=1, keepdims=True), axis=0, keepdims=True) * (0.5 / d)

    row = pl.BlockSpec((tm, d), lambda i: (i, 0))
    vec = pl.BlockSpec((1, d), lambda i: (0, 0))
    return pl.pallas_call(
        body, grid=(s // tm,), in_specs=[row, vec, row], out_specs=[row, row, vec, pl.BlockSpec((1, 128), lambda i: (0, 0))],
        out_shape=[SDS((s, d), F32), SDS((s, d), BF16), SDS((1, d), F32), SDS((1, 128), F32)],
        name=name, compiler_params=_cp("arbitrary"))(x, w, target)


def _rope_tables(s):
    half = ROPE_DIM // 2
    f32 = np.float32
    inv_freq = np.power(f32(ROPE_THETA), -np.arange(half, dtype=f32) * f32(2.0) / f32(ROPE_DIM)).astype(f32)
    ang = (np.arange(s, dtype=f32)[:, None] * inv_freq[None, :]).astype(f32)
    cos, sin = np.cos(ang).astype(f32), np.sin(ang).astype(f32)
    z = np.zeros((s, HEAD_DIM - ROPE_DIM), f32)
    zh = np.zeros((s, half), f32)
    c = np.concatenate([cos, cos, z + 1.0], axis=1)
    sa = np.concatenate([zh, sin, z], axis=1)
    sb = np.concatenate([-sin, zh, z], axis=1)
    two = lambda t: np.concatenate([t, t], axis=1)
    c, sa, sb = two(c), two(sa), two(sb)
    fwd, bwd = (c, sa, sb), (c, np.roll(sb, half, axis=1), np.roll(sa, -half, axis=1))
    return tuple(jnp.asarray(t) for t in fwd), tuple(jnp.asarray(t) for t in bwd)


PERM_TILE = 256


def _perm_matrices(d):
    n = PERM_TILE // d
    o = np.arange(PERM_TILE)
    p = np.zeros((PERM_TILE, PERM_TILE), np.float32)
    p[o, (o % n) * d + o // n] = 1.0
    return jnp.asarray(p, dtype=BF16), jnp.asarray(p.T.copy(), dtype=BF16)


def _store_pattern(o_ref, tile, perm_ref, d, cols=slice(None)):
    n = PERM_TILE // d
    z = jnp.dot(perm_ref[...], tile, preferred_element_type=F32).astype(o_ref.dtype)
    for r in range(d):
        o_ref[r, :, cols] = z[r * n:(r + 1) * n]


def _load_pattern(x_ref, perm_ref, d):
    tile = jnp.concatenate([x_ref[r] for r in range(d)], axis=0)
    return jnp.dot(perm_ref[...], tile, preferred_element_type=F32)


def _pattern_spec(d, w, col=0):
    return pl.BlockSpec((d, PERM_TILE // d, w), lambda i, *_: (0, i, col))


def _rope_fwd(proj, tabs, perms, name):
    s = proj.shape[0]
    tm = PERM_TILE
    half = ROPE_DIM // 2
    wb = D_MODEL
    nd = len(DILATIONS) - 1

    def body(x_ref, c_ref, sa_ref, sb_ref, *rest):
        perm_refs, o_ref, op_refs = rest[:nd], rest[nd], rest[nd + 1:]
        is_v = pl.program_id(1) == 2
        qs = jnp.where(pl.program_id(1) == 0, HEAD_DIM ** -0.5, 1.0)
        c = jnp.where(is_v, 1.0, c_ref[...]) * qs
        sa = jnp.where(is_v, 0.0, sa_ref[...]) * qs
        sb = jnp.where(is_v, 0.0, sb_ref[...]) * qs
        for j in range(wb // 128):
            x = x_ref[:, j * 128:(j + 1) * 128].astype(F32)
            o_ref[:, j * 128:(j + 1) * 128] = (x * c + pltpu.roll(x, half, 1) * sa + pltpu.roll(x, 128 - half, 1) * sb).astype(o_ref.dtype)
        y = o_ref[...]
        for d, perm_ref, op_ref in zip(DILATIONS[1:], perm_refs, op_refs):
            _store_pattern(op_ref, y, perm_ref, d)

    blk = pl.BlockSpec((tm, wb), lambda i, j: (i, j))
    tab = pl.BlockSpec((tm, 128), lambda i, j: (i, 0))
    pm = pl.BlockSpec((tm, tm), lambda i, j: (0, 0))
    outs = pl.pallas_call(
        body, grid=(s // tm, 3), in_specs=[blk, tab, tab, tab] + [pm] * nd,
        out_specs=[blk] + [pl.BlockSpec((d, tm // d, wb), lambda i, j: (0, i, j)) for d in DILATIONS[1:]],
        out_shape=[SDS((s, 3 * wb), BF16)] + [SDS((d, s // d, 3 * wb), BF16) for d in DILATIONS[1:]],
        name=name, compiler_params=_cp("parallel", "parallel"))(proj, *tabs, *[perms[d][0] for d in DILATIONS[1:]])
    return [o.reshape(s, 3 * wb) for o in outs]


def _sum3_rope(ds_, perms, tabs, name, slab=None):
    s, w = ds_[0].shape
    tm = PERM_TILE
    half = ROPE_DIM // 2
    nd = len(DILATIONS) - 1

    def body(*refs):
        x_refs, perm_refs = refs[:nd + 1], refs[nd + 1:2 * nd + 1]
        tab_refs = refs[2 * nd + 1:2 * nd + 4]
        tot = x_refs[0][...].astype(F32)
        for d, x_ref, perm_ref in zip(DILATIONS[1:], x_refs[1:], perm_refs):
            tot = tot + _load_pattern(x_ref, perm_ref, d)
        for j in range(w // 128):
            x = tot[:, j * 128:(j + 1) * 128]
            if tabs is not None:
                x = x * tab_refs[0][...] + pltpu.roll(x, half, 1) * tab_refs[1][...] + pltpu.roll(x, 128 - half, 1) * tab_refs[2][...]
            refs[-1][:, j * 128:(j + 1) * 128] = x.astype(refs[-1].dtype)

    blk = pl.BlockSpec((tm, w), lambda i: (i, 0))
    tab = pl.BlockSpec((tm, 128), lambda i: (i, 0))
    pm = pl.BlockSpec((tm, tm), lambda i: (0, 0))
    extra = [] if tabs is None else list(tabs)
    out_shape, col0, more, more_specs, alias = _slab(s, w, BF16, slab, 2 * nd + 1 + len(extra))
    cb = col0 // w
    args = [ds_[0]] + [x.reshape(d, s // d, w) for d, x in zip(DILATIONS[1:], ds_[1:])] + [perms[d][1] for d in DILATIONS[1:]]
    return pl.pallas_call(
        body, grid=(s // tm,),
        in_specs=[blk] + [_pattern_spec(d, w) for d in DILATIONS[1:]] + [pm] * nd + [tab] * len(extra) + more_specs,
        out_specs=pl.BlockSpec((tm, w), lambda i: (i, cb)), out_shape=out_shape, input_output_aliases=alias,
        name=name, compiler_params=_cp("parallel"))(*args, *extra, *more)


def _band_valid(t, nq, nk, qofs, kofs, seq_len):
    qpos = t * 128 + qofs + lax.broadcasted_iota(jnp.int32, (nq, nk), 0)
    kpos = t * 128 + kofs + lax.broadcasted_iota(jnp.int32, (nq, nk), 1)
    sh = int(math.log2(seq_len))
    same = lax.shift_right_arithmetic(qpos, sh) == lax.shift_right_arithmetic(kpos, sh)
    return same & (jnp.abs(kpos - qpos) <= BAND_HALF)


def _window(r0, r1, r2):
    return jnp.concatenate([r0[128 - BAND_HALF:128], r1[...], r2[0:BAND_HALF]], axis=0)


WIN = 128 + 2 * BAND_HALF


def _first_head_lanes():
    return lax.broadcasted_iota(jnp.int32, (1, 2 * HEAD_DIM), 1) < HEAD_DIM


def _only(x, keep):
    return jnp.where(keep, x, jnp.zeros((), x.dtype))


def _win_specs(width, col, nt):
    return [pl.BlockSpec((128, width), lambda t: (jnp.maximum(t - 1, 0), col)),
            pl.BlockSpec((128, width), lambda t: (t, col)),
            pl.BlockSpec((128, width), lambda t: (jnp.minimum(t + 1, nt - 1), col))]


def _attn_fwd(qk, v_src, v_col, seq_len, name):
    s = qk.shape[0]
    nt = s // 128
    dm = D_MODEL

    def body(q_ref, k0, k1, k2, v0, v1, v2, o_ref, lse_ref):
        t = pl.program_id(0)
        valid = _band_valid(t, 128, WIN, 0, -BAND_HALF, seq_len)
        q = q_ref[...]
        kc = _window(k0, k1, k2)
        vc = _window(v0, v1, v2)
        first = _first_head_lanes()
        outs, lses = [], []
        for pr in range(N_HEADS // 2):
            ps = slice(pr * 128, (pr + 1) * 128)
            qp, kp, vp = q[:, ps], kc[:, ps], vc[:, ps]
            halves = []
            for keep in (first, ~first):
                sc = lax.dot_general(_only(qp, keep), kp, NT, preferred_element_type=F32)
                sc = jnp.where(valid, sc, NEG)
                m = jnp.max(sc, axis=1, keepdims=True)
                e = jnp.exp(sc - m)
                den = jnp.sum(e, axis=1, keepdims=True)
                halves.append(jnp.dot(e.astype(BF16), vp, preferred_element_type=F32) / den)
                lses.append(m + jnp.log(den))
            outs.append(jnp.where(first, halves[0], halves[1]))
        o_ref[...] = jnp.concatenate(outs, axis=1).astype(o_ref.dtype)
        lse_ref[...] = jnp.concatenate(lses, axis=1)

    in_specs = [pl.BlockSpec((128, dm), lambda t: (t, 0))] + _win_specs(dm, 1, nt) + _win_specs(dm, v_col, nt)
    return pl.pallas_call(
        body, grid=(nt,), in_specs=in_specs,
        out_specs=[pl.BlockSpec((128, dm), lambda t: (t, 0)), pl.BlockSpec((128, N_HEADS), lambda t: (t, 0))],
        out_shape=[SDS((s, dm), BF16), SDS((s, N_HEADS), F32)], name=name, compiler_params=_cp("parallel"))(
            qk, qk, qk, qk, v_src, v_src, v_src)


def _attn_combine(os_, lses, perms, expand, name):
    s, dm = os_[0].shape
    tm = PERM_TILE
    nd = len(DILATIONS) - 1

    def body(o1, o2, o3, l1, l2, l3, p2, p3, e_ref, out_ref, out_t_ref, lt_ref):
        ls = [l1[...], l2[...], l3[...]]
        m = jnp.maximum(jnp.maximum(ls[0], ls[1]), ls[2])
        es = [jnp.exp(l - m) for l in ls]
        tot = es[0] + es[1] + es[2]
        lt_ref[...] = m + jnp.log(tot)
        ovs = [o1[...].astype(F32), _load_pattern(o2, p2, DILATIONS[1]), _load_pattern(o3, p3, DILATIONS[2])]
        acc = jnp.zeros((tm, dm), F32)
        for e, o in zip(es, ovs):
            acc = acc + _dot_01(e / tot, e_ref[...]) * o
        out = acc.astype(out_ref.dtype)
        out_ref[...] = out
        out_t_ref[...] = out.T

    row = pl.BlockSpec((tm, dm), lambda i: (i, 0))
    st = pl.BlockSpec((tm, N_HEADS), lambda i: (i, 0))
    pm = pl.BlockSpec((tm, tm), lambda i: (0, 0))
    args = [os_[0]] + [o.reshape(d, s // d, dm) for d, o in zip(DILATIONS[1:], os_[1:])]
    return pl.pallas_call(
        body, grid=(s // tm,),
        in_specs=[row] + [_pattern_spec(d, dm) for d in DILATIONS[1:]] + [st, st, st, pm, pm, pl.BlockSpec((N_HEADS, dm), lambda i: (0, 0))],
        out_specs=[row, pl.BlockSpec((dm, tm), lambda i: (0, i)), st],
        out_shape=[SDS((s, dm + D_INNER), BF16), SDS((dm + D_INNER, s), BF16), SDS((s, N_HEADS), F32)],
        name=name, compiler_params=_cp("parallel"))(*args, *lses, *[perms[d][1] for d in DILATIONS[1:]], expand)


def _attn_delta(dmix, attn, expand_t, perms, name):
    s, dm = attn.shape[0], D_MODEL
    tm = PERM_TILE
    nd = len(DILATIONS) - 1

    def body(d_ref, a_ref, e_ref, *rest):
        perm_refs, o_ref, op_refs = rest[:nd], rest[nd], rest[nd + 1:]
        dv = d_ref[...]
        prod = dv.astype(F32) * a_ref[...].astype(F32)
        o_ref[...] = _dot_01(prod, e_ref[...])
        for d, perm_ref, op_ref in zip(DILATIONS[1:], perm_refs, op_refs):
            _store_pattern(op_ref, dv, perm_ref, d)

    row = pl.BlockSpec((tm, dm), lambda i: (i, 0))
    pm = pl.BlockSpec((tm, tm), lambda i: (0, 0))
    outs = pl.pallas_call(
        body, grid=(s // tm,), in_specs=[row, row, pl.BlockSpec((dm, N_HEADS), lambda i: (0, 0))] + [pm] * nd,
        out_specs=[pl.BlockSpec((tm, N_HEADS), lambda i: (i, 0))] + [_pattern_spec(d, dm) for d in DILATIONS[1:]],
        out_shape=[SDS((s, N_HEADS), F32)] + [SDS((d, s // d, dm), BF16) for d in DILATIONS[1:]],
        name=name, compiler_params=_cp("parallel"))(dmix, attn, expand_t, *[perms[d][0] for d in DILATIONS[1:]])
    return outs[0], [o.reshape(s, dm) for o in outs[1:]]


def _attn_bwd_dq(qk, v_src, v_col, do_src, lse, delta, seq_len, name):
    s = qk.shape[0]
    nt = s // 128
    dm = D_MODEL

    def body(q_ref, k0, k1, k2, v0, v1, v2, do_ref, lse_ref, dl_ref, dq_ref):
        t = pl.program_id(0)
        valid = _band_valid(t, 128, WIN, 0, -BAND_HALF, seq_len)
        q = q_ref[...]
        do = do_ref[...]
        kc = _window(k0, k1, k2)
        vc = _window(v0, v1, v2)
        lse_v, dl_v = lse_ref[...], dl_ref[...]
        first = _first_head_lanes()
        outs = []
        for pr in range(N_HEADS // 2):
            ps = slice(pr * 128, (pr + 1) * 128)
            qp, dop, kp, vp = q[:, ps], do[:, ps], kc[:, ps], vc[:, ps]
            halves = []
            for i, keep in enumerate((first, ~first)):
                h = 2 * pr + i
                sc = lax.dot_general(_only(qp, keep), kp, NT, preferred_element_type=F32)
                p = jnp.exp(jnp.where(valid, sc - lse_v[:, h:h + 1], NEG))
                dp = lax.dot_general(_only(dop, keep), vp, NT, preferred_element_type=F32)
                ds = p * (dp - dl_v[:, h:h + 1])
                halves.append(jnp.dot(ds.astype(BF16), kp, preferred_element_type=F32))
            outs.append(jnp.where(first, halves[0], halves[1]) * (HEAD_DIM ** -0.5))
        dq_ref[...] = jnp.concatenate(outs, axis=1).astype(dq_ref.dtype)

    row = pl.BlockSpec((128, dm), lambda t: (t, 0))
    st = pl.BlockSpec((128, N_HEADS), lambda t: (t, 0))
    in_specs = [row] + _win_specs(dm, 1, nt) + _win_specs(dm, v_col, nt) + [row, st, st]
    return pl.pallas_call(
        body, grid=(nt,), in_specs=in_specs, out_specs=row, out_shape=SDS((s, dm), BF16),
        name=name, compiler_params=_cp("parallel"))(qk, qk, qk, qk, v_src, v_src, v_src, do_src, lse, delta)


def _attn_bwd_dkv(qk, v_src, v_col, do_src, lse_t, delta_t, seq_len, name):
    s = qk.shape[0]
    nt = s // 128
    dm = D_MODEL

    def lane_window(r0, r1, r2):
        return jnp.concatenate([r0[:, 128 - BAND_HALF:128], r1[...], r2[:, 0:BAND_HALF]], axis=1)

    def body(k_ref, v_ref, q0, q1, q2, d0, d1, d2, l0, l1, l2, e0, e1, e2, dk_ref, dv_ref):
        t = pl.program_id(0)
        valid = _band_valid(t, 128, WIN, 0, -BAND_HALF, seq_len)
        k = k_ref[...]
        v = v_ref[...]
        qc = _window(q0, q1, q2)
        dc = _window(d0, d1, d2)
        lse_v = lane_window(l0, l1, l2)
        dl_v = lane_window(e0, e1, e2)
        first = _first_head_lanes()
        dks, dvs = [], []
        for pr in range(N_HEADS // 2):
            ps = slice(pr * 128, (pr + 1) * 128)
            kp, vp, qp, dop = k[:, ps], v[:, ps], qc[:, ps], dc[:, ps]
            dk_h, dv_h = [], []
            for i, keep in enumerate((first, ~first)):
                h = 2 * pr + i
                sc = lax.dot_general(_only(kp, keep), qp, NT, preferred_element_type=F32)
                p = jnp.exp(jnp.where(valid, sc - lse_v[h:h + 1, :], NEG))
                dv_h.append(jnp.dot(p.astype(BF16), dop, preferred_element_type=F32))
                dp = lax.dot_general(_only(vp, keep), dop, NT, preferred_element_type=F32)
                ds = p * (dp - dl_v[h:h + 1, :])
                dk_h.append(jnp.dot(ds.astype(BF16), qp, preferred_element_type=F32))
            dks.append(jnp.where(first, dk_h[0], dk_h[1]))
            dvs.append(jnp.where(first, dv_h[0], dv_h[1]))
        dk_ref[...] = jnp.concatenate(dks, axis=1).astype(dk_ref.dtype)
        dv_ref[...] = jnp.concatenate(dvs, axis=1).astype(dv_ref.dtype)

    row = pl.BlockSpec((128, dm), lambda t: (t, 0))
    stat = [pl.BlockSpec((N_HEADS, 128), lambda t: (0, jnp.maximum(t - 1, 0))), pl.BlockSpec((N_HEADS, 128), lambda t: (0, t)),
            pl.BlockSpec((N_HEADS, 128), lambda t: (0, jnp.minimum(t + 1, nt - 1)))]
    in_specs = ([pl.BlockSpec((128, dm), lambda t: (t, 1)), pl.BlockSpec((128, dm), lambda t: (t, v_col))]
                + _win_specs(dm, 0, nt) + _win_specs(dm, 0, nt) + stat + stat)
    return pl.pallas_call(
        body, grid=(nt,), in_specs=in_specs, out_specs=[row, row], out_shape=[SDS((s, dm), BF16), SDS((s, dm), BF16)],
        name=name, compiler_params=_cp("parallel"))(qk, v_src, qk, qk, qk, do_src, do_src, do_src, lse_t, lse_t, lse_t, delta_t, delta_t, delta_t)


CONV_COLS = (1024, 1408, 512, 256)


def _halo_specs(tm, tc, col0, nrow_blocks):
    r = tm // 16
    return [pl.BlockSpec((16, tc), lambda i, j: (jnp.maximum(i * r - 1, 0), col0 + j)),
            pl.BlockSpec((16, tc), lambda i, j: (jnp.minimum((i + 1) * r, nrow_blocks * r - 1), col0 + j))]


def _shifted(x_ref, hp_ref, hn_ref, i, last):
    x = x_ref[...].astype(F32)
    tm = x.shape[0]
    rows = lax.broadcasted_iota(jnp.int32, x.shape, 0)
    prev_row = jnp.where(i > 0, hp_ref[15:16, :].astype(F32), 0.0)
    next_row = jnp.where(i < last, hn_ref[0:1, :].astype(F32), 0.0)
    xp = jnp.where(rows == 0, prev_row, pltpu.roll(x, 1, 0))
    xn = jnp.where(rows == tm - 1, next_row, pltpu.roll(x, tm - 1, 0))
    return xp, x, xn


def _conv(x_src, col0, width, w3, bias, act, name, out_dtype=BF16, slab=None, transpose=False, wcol0=0):
    s = x_src.shape[0]
    tm = _pick(s, (256, 128))
    tc = _pick(width, CONV_COLS)
    nb = s // tm
    c0 = col0 // tc
    wc0 = wcol0 // tc

    def body(*refs):
        x_ref, hp_ref, hn_ref, w_ref, b_ref = refs[:5]
        o_ref = refs[-1]
        i = pl.program_id(0)
        xp, x, xn = _shifted(x_ref, hp_ref, hn_ref, i, nb - 1)
        w = w_ref[...]
        if transpose:
            y = w[2:3, :] * xp + w[1:2, :] * x + w[0:1, :] * xn
        else:
            y = w[0:1, :] * xp + w[1:2, :] * x + w[2:3, :] * xn + b_ref[...]
        if act:
            y = y * _sigmoid(y)
        o_ref[...] = y.astype(o_ref.dtype)

    in_specs = ([pl.BlockSpec((tm, tc), lambda i, j: (i, c0 + j))] + _halo_specs(tm, tc, c0, nb)
                + [pl.BlockSpec((3, tc), lambda i, j: (0, wc0 + j)), pl.BlockSpec((1, tc), lambda i, j: (0, wc0 + j))])
    out_shape, ocol, more, more_specs, alias = _slab(s, width, out_dtype, slab, 5)
    ob = ocol // tc
    return pl.pallas_call(
        body, grid=(nb, width // tc), in_specs=in_specs + more_specs, out_specs=pl.BlockSpec((tm, tc), lambda i, j: (i, ob + j)),
        out_shape=out_shape, input_output_aliases=alias, name=name, compiler_params=_cp("parallel", "parallel"))(
            x_src, x_src, x_src, w3, bias, *more)


def _conv_silu_bwd(x_src, col0, width, w3, bias, addends, add_widths, name):
    s = x_src.shape[0]
    tm = _pick(s, (256, 128))
    tc = _pick(width, CONV_COLS)
    nb = s // tm
    c0 = col0 // tc
    na = len(addends)

    def body(*refs):
        x_ref, hp_ref, hn_ref, w_ref, b_ref = refs[:5]
        a_refs = refs[5:5 + na]
        dp_ref, dw_ref, db_ref = refs[5 + na:]
        i, j = pl.program_id(1), pl.program_id(0)
        xp, x, xn = _shifted(x_ref, hp_ref, hn_ref, i, nb - 1)
        w = w_ref[...]
        pre = w[0:1, :] * xp + w[1:2, :] * x + w[2:3, :] * xn + b_ref[...]
        g = jnp.zeros_like(pre)
        for a_ref, aw in zip(a_refs, add_widths):
            av = a_ref[...].astype(F32)
            g = g + (av if aw == width else jnp.where(j < aw // tc, av, 0.0))
        sg = _sigmoid(pre)
        dpre = g * (sg * (1.0 + pre * (1.0 - sg)))
        dp_ref[...] = dpre.astype(dp_ref.dtype)

        @pl.when(i == 0)
        def _():
            dw_ref[...] = jnp.zeros_like(dw_ref)
            db_ref[...] = jnp.zeros_like(db_ref)

        dw_ref[...] += jnp.concatenate([jnp.sum(dpre * xp, axis=0, keepdims=True), jnp.sum(dpre * x, axis=0, keepdims=True),
                                        jnp.sum(dpre * xn, axis=0, keepdims=True)], axis=0)
        db_ref[...] += jnp.sum(dpre, axis=0, keepdims=True)

    r = tm // 16
    in_specs = [pl.BlockSpec((tm, tc), lambda j, i: (i, c0 + j)),
                pl.BlockSpec((16, tc), lambda j, i: (jnp.maximum(i * r - 1, 0), c0 + j)),
                pl.BlockSpec((16, tc), lambda j, i: (jnp.minimum((i + 1) * r, nb * r - 1), c0 + j)),
                pl.BlockSpec((3, tc), lambda j, i: (0, j)), pl.BlockSpec((1, tc), lambda j, i: (0, j))]
    for aw in add_widths:
        nblk = aw // tc
        in_specs.append(pl.BlockSpec((tm, tc), lambda j, i, nblk=nblk: (i, jnp.minimum(j, nblk - 1))))
    return pl.pallas_call(
        body, grid=(width // tc, nb), in_specs=in_specs,
        out_specs=[pl.BlockSpec((tm, tc), lambda j, i: (i, j)), pl.BlockSpec((3, tc), lambda j, i: (0, j)), pl.BlockSpec((1, tc), lambda j, i: (0, j))],
        out_shape=[SDS((s, width), BF16), SDS((3, width), F32), SDS((1, width), F32)],
        name=name, compiler_params=_cp("parallel", "arbitrary"))(x_src, x_src, x_src, w3, bias, *addends)


def _ffn_gate_fwd(u, w3, bias, name):
    s = u.shape[0]
    tm = _pick(s, (256, 128))
    tc = _pick(D_FF, CONV_COLS)
    nb = s // tm
    nj = D_FF // tc

    def body(g_ref, gp, gn, u_ref, up, un, wg_ref, wu_ref, bg_ref, bu_ref, o_ref, ot_ref):
        i = pl.program_id(0)
        outs = []
        for (x_ref, hp, hn, w_ref, b_ref) in ((g_ref, gp, gn, wg_ref, bg_ref), (u_ref, up, un, wu_ref, bu_ref)):
            xp, x, xn = _shifted(x_ref, hp, hn, i, nb - 1)
            w = w_ref[...]
            outs.append(w[0:1, :] * xp + w[1:2, :] * x + w[2:3, :] * xn + b_ref[...])
        gate, upv = outs
        out = (gate * _sigmoid(gate) * upv).astype(o_ref.dtype)
        o_ref[...] = out
        ot_ref[...] = out.T

    def xspecs(c0):
        return [pl.BlockSpec((tm, tc), lambda i, j: (i, c0 + j))] + _halo_specs(tm, tc, c0, nb)

    in_specs = (xspecs(0) + xspecs(nj)
                + [pl.BlockSpec((3, tc), lambda i, j: (0, j)), pl.BlockSpec((3, tc), lambda i, j: (0, nj + j)),
                   pl.BlockSpec((1, tc), lambda i, j: (0, j)), pl.BlockSpec((1, tc), lambda i, j: (0, nj + j))])
    return pl.pallas_call(
        body, grid=(nb, nj), in_specs=in_specs,
        out_specs=[pl.BlockSpec((tm, tc), lambda i, j: (i, j)), pl.BlockSpec((tc, tm), lambda i, j: (j, i))],
        out_shape=[SDS((s, D_FF), BF16), SDS((D_FF, s), BF16)], name=name, compiler_params=_cp("parallel", "parallel"))(
            u, u, u, u, u, u, w3, w3, bias, bias)


def _ffn_gate_bwd(u, w3, bias, dact, name):
    s = u.shape[0]
    tm = _pick(s, (256, 128))
    tc = _pick(D_FF, CONV_COLS)
    nb = s // tm
    nj = D_FF // tc

    def body(g_ref, gp, gn, u_ref, up, un, wg_ref, wu_ref, bg_ref, bu_ref, da_ref, dg_ref, du_ref, dwg_ref, dwu_ref, dbg_ref, dbu_ref):
        i = pl.program_id(1)
        sh, pre = [], []
        for (x_ref, hp, hn, w_ref, b_ref) in ((g_ref, gp, gn, wg_ref, bg_ref), (u_ref, up, un, wu_ref, bu_ref)):
            xs3 = _shifted(x_ref, hp, hn, i, nb - 1)
            w = w_ref[...]
            sh.append(xs3)
            pre.append(w[0:1, :] * xs3[0] + w[1:2, :] * xs3[1] + w[2:3, :] * xs3[2] + b_ref[...])
        gate, upv = pre
        da = da_ref[...].astype(F32)
        sg = _sigmoid(gate)
        dgate = da * upv * (sg * (1.0 + gate * (1.0 - sg)))
        dup = da * gate * sg
        dg_ref[...] = dgate.astype(dg_ref.dtype)
        du_ref[...] = dup.astype(du_ref.dtype)

        @pl.when(i == 0)
        def _():
            for r in (dwg_ref, dwu_ref, dbg_ref, dbu_ref):
                r[...] = jnp.zeros_like(r)

        for d, xs3, dw_ref, db_ref in ((dgate, sh[0], dwg_ref, dbg_ref), (dup, sh[1], dwu_ref, dbu_ref)):
            dw_ref[...] += jnp.concatenate([jnp.sum(d * xs3[0], axis=0, keepdims=True), jnp.sum(d * xs3[1], axis=0, keepdims=True),
                                            jnp.sum(d * xs3[2], axis=0, keepdims=True)], axis=0)
            db_ref[...] += jnp.sum(d, axis=0, keepdims=True)

    r = tm // 16

    def xspecs(c0):
        return [pl.BlockSpec((tm, tc), lambda j, i: (i, c0 + j)),
                pl.BlockSpec((16, tc), lambda j, i: (jnp.maximum(i * r - 1, 0), c0 + j)),
                pl.BlockSpec((16, tc), lambda j, i: (jnp.minimum((i + 1) * r, nb * r - 1), c0 + j))]

    in_specs = (xspecs(0) + xspecs(nj)
                + [pl.BlockSpec((3, tc), lambda j, i: (0, j)), pl.BlockSpec((3, tc), lambda j, i: (0, nj + j)),
                   pl.BlockSpec((1, tc), lambda j, i: (0, j)), pl.BlockSpec((1, tc), lambda j, i: (0, nj + j)),
                   pl.BlockSpec((tm, tc), lambda j, i: (i, j))])
    blk = pl.BlockSpec((tm, tc), lambda j, i: (i, j))
    w_o = pl.BlockSpec((3, tc), lambda j, i: (0, j))
    b_o = pl.BlockSpec((1, tc), lambda j, i: (0, j))
    return pl.pallas_call(
        body, grid=(nj, nb), in_specs=in_specs, out_specs=[blk, blk, w_o, w_o, b_o, b_o],
        out_shape=[SDS((s, D_FF), BF16), SDS((s, D_FF), BF16), SDS((3, D_FF), F32), SDS((3, D_FF), F32), SDS((1, D_FF), F32), SDS((1, D_FF), F32)],
        name=name, compiler_params=_cp("parallel", "arbitrary"))(u, u, u, u, u, u, w3, w3, bias, bias, dact)


def _exchange_sems(n):
    return [pltpu.SemaphoreType.DMA((7 * n,)), pltpu.SemaphoreType.DMA((7 * n,)), pltpu.SemaphoreType.DMA((n,))]


def _exchange_copies(srcs, outs, send_sems, recv_sems, local_sems):
    x, y, c = lax.axis_index("x"), lax.axis_index("y"), lax.axis_index("c")
    me = 4 * x + 2 * y + c
    locals_ = [pltpu.make_async_copy(srcs[a].at[me], outs[a].at[me], local_sems.at[a]) for a in range(len(srcs))]
    sends, recvs = [], []
    for k in range(1, N_DEV):
        px, py, pc = x ^ ((k >> 2) & 1), y ^ ((k >> 1) & 1), c ^ (k & 1)
        peer = 4 * px + 2 * py + pc
        for a in range(len(srcs)):
            sems = dict(send_sem=send_sems.at[a * 7 + k - 1], recv_sem=recv_sems.at[a * 7 + k - 1], device_id_type=MESH)
            sends.append(pltpu.make_async_remote_copy(src_ref=srcs[a].at[peer], dst_ref=outs[a].at[me], device_id=(px, py, pc), **sems))
            recvs.append(pltpu.make_async_remote_copy(src_ref=srcs[a].at[peer], dst_ref=outs[a].at[peer], device_id=(x, y, c), **sems))
    return locals_, sends, recvs


def _exchange_start(srcs, outs, send_sems, recv_sems, local_sems):
    locals_, sends, _ = _exchange_copies(srcs, outs, send_sems, recv_sems, local_sems)
    for cp in locals_ + sends:
        cp.start()


def _exchange_wait(srcs, outs, send_sems, recv_sems, local_sems):
    locals_, sends, recvs = _exchange_copies(srcs, outs, send_sems, recv_sems, local_sems)
    for cp in recvs:
        cp.wait_recv()
    for cp in sends:
        cp.wait_send()
    for cp in locals_:
        cp.wait()


def _gather_copies(srcs, outs, send_sems, recv_sems, local_sems):
    x, y, c = lax.axis_index("x"), lax.axis_index("y"), lax.axis_index("c")
    me, sibling = (x, y, c), (x, y, 1 - c)
    chips = [(1 - x, y), (x, 1 - y), (1 - x, 1 - y)]

    def copy(a, k, block, to, src=None):
        dst = outs[a].at[4 * block[0] + 2 * block[1] + block[2]]
        return pltpu.make_async_remote_copy(
            src_ref=dst if src is None else src, dst_ref=dst,
            send_sem=send_sems.at[a * 7 + k], recv_sem=recv_sems.at[a * 7 + k], device_id=to, device_id_type=MESH)

    n = len(srcs)
    locals_ = [pltpu.make_async_copy(srcs[a], outs[a].at[4 * x + 2 * y + c], local_sems.at[a]) for a in range(n)]
    own = [copy(a, 0, me, sibling, src=srcs[a]) for a in range(n)]
    own += [copy(a, 1 + j, me, (*chip, c), src=srcs[a]) for a in range(n) for j, chip in enumerate(chips)]
    landed_ici = [copy(a, 1 + j, (*chip, c), me) for j, chip in enumerate(chips) for a in range(n)]
    passed = [copy(a, 4 + j, (*chip, c), sibling) for j, chip in enumerate(chips) for a in range(n)]
    landed_d2d = [copy(a, 0, sibling, me) for a in range(n)]
    landed_d2d += [copy(a, 4 + j, (*chip, 1 - c), me) for a in range(n) for j, chip in enumerate(chips)]
    return locals_, own, landed_ici, passed, landed_d2d


def _gather_start(*refs):
    locals_, own, _, _, _ = _gather_copies(*refs)
    for cp in locals_ + own:
        cp.start()


def _gather_forward(*refs):
    _, _, landed_ici, passed, _ = _gather_copies(*refs)
    for arrived, onward in zip(landed_ici, passed):
        arrived.wait_recv()
        onward.start()


def _gather_finish(*refs):
    locals_, own, _, passed, landed_d2d = _gather_copies(*refs)
    for cp in landed_d2d:
        cp.wait_recv()
    for cp in own + passed:
        cp.wait_send()
    for cp in locals_:
        cp.wait()


def _ssd_common(dt_ref, dtt_ref, al_r, al_c, bi_r, bi_c, off, rev):
    li = lax.broadcasted_iota(jnp.int32, (CHUNK, CHUNK), 0)
    si = lax.broadcasted_iota(jnp.int32, (CHUNK, CHUNK), 1)
    mask = (li <= si) if rev else (li >= si)
    mask_t = (li >= si) if rev else (li <= si)
    a_r = -jnp.exp(al_r[...])
    a_c = -jnp.exp(al_c[...])
    pre = dt_ref[:, off:off + N_HEADS] + bi_r[...]
    dt = _softplus(pre)
    cs = jnp.dot(mask.astype(F32), dt * a_r, precision=HIGH, preferred_element_type=F32)
    dt_t = _softplus(dtt_ref[off:off + N_HEADS, :] + bi_c[...])
    cs_t = jnp.dot(dt_t * a_c, mask_t.astype(F32), precision=HIGH, preferred_element_type=F32)
    tot = cs[0:1, :] if rev else cs[CHUNK - 1:CHUNK, :]
    return mask, mask_t, a_r, pre, dt, cs, cs_t, tot


def _ssd_fwd(xbc, dt_raw, dt_t, args_f, args_b, name, gather=()):
    s = xbc.shape[0]
    nc = s // CHUNK
    hp = D_INNER // N_HEADS
    hpg = N_HEADS // N_GROUPS
    gs = N_GROUPS * D_STATE
    ng = len(gather)
    fwd_step = (nc * 27) // 32

    def chunk(x_ref, b_ref, c_ref, dt_ref, dtt_ref, alr, alc, bir, bic, y_ref, st_ref, h_scr, off, rev):
        mask, _, _, _, dt, cs, cs_t, tot = _ssd_common(dt_ref, dtt_ref, alr, alc, bir, bic, off, rev)
        first = _first_head_lanes()
        xs = x_ref[...]
        ys = []
        for g in range(N_GROUPS):
            bg = b_ref[:, g * D_STATE:(g + 1) * D_STATE]
            cg = c_ref[:, g * D_STATE:(g + 1) * D_STATE]
            gm = lax.dot_general(cg, bg, NT, preferred_element_type=F32)
            hcat = h_scr[g]
            st_ref[0, g] = hcat
            ch = lax.dot_general(cg, hcat.astype(BF16), NT, preferred_element_type=F32)
            xdd = []
            for pr in range(hpg // 2):
                h0 = g * hpg + 2 * pr
                lanes = slice(h0 * hp, (h0 + 2) * hp)
                cols = [cs[:, h:h + 1] for h in (h0, h0 + 1)]
                pair = lambda a, b: jnp.where(first, a, b)
                xdf = xs[:, lanes].astype(F32) * pair(dt[:, h0:h0 + 1], dt[:, h0 + 1:h0 + 2])
                xdb = xdf.astype(BF16)
                yh = []
                for i, h in enumerate((h0, h0 + 1)):
                    lm = jnp.exp(jnp.where(mask, cols[i] - cs_t[h:h + 1, :], NEG))
                    yh.append(jnp.dot((gm * lm).astype(BF16), xdb, preferred_element_type=F32))
                ecs = pair(jnp.exp(cols[0]), jnp.exp(cols[1]))
                ys.append(pair(yh[0], yh[1]) + ecs * ch[:, 2 * pr * hp:(2 * pr + 2) * hp])
                dec = pair(jnp.exp(tot[:, h0:h0 + 1] - cols[0]), jnp.exp(tot[:, h0 + 1:h0 + 2] - cols[1]))
                xdd.append((xdf * dec).astype(BF16))
            snew = lax.dot_general(jnp.concatenate(xdd, axis=1), bg, TN, preferred_element_type=F32)
            for r in range(hpg):
                rs = slice(r * hp, (r + 1) * hp)
                h_scr[g, rs, :] = jnp.exp(tot[:, g * hpg + r:g * hpg + r + 1]) * hcat[rs] + snew[rs]
        y_ref[...] = jnp.concatenate(ys, axis=1).astype(y_ref.dtype)

    def body(*refs):
        in_f, in_b = refs[0:9], refs[9:18]
        g_src = refs[18:18 + ng]
        out_f, out_b = refs[18 + ng:20 + ng], refs[20 + ng:22 + ng]
        g_dst = refs[22 + ng:22 + 2 * ng]
        hs_f, hs_b = refs[22 + 2 * ng], refs[23 + 2 * ng]
        g_sems = refs[24 + 2 * ng:]
        step = pl.program_id(0)

        @pl.when(step == 0)
        def _():
            hs_f[...] = jnp.zeros_like(hs_f)
            hs_b[...] = jnp.zeros_like(hs_b)
            if ng:
                _gather_start(g_src, g_dst, *g_sems)

        chunk(*in_f, *out_f, hs_f, 0, False)
        chunk(*in_b, *out_b, hs_b, N_HEADS, True)

        if ng:
            @pl.when(step == fwd_step)
            def _():
                _gather_forward(g_src, g_dst, *g_sems)

            @pl.when(step == nc - 1)
            def _():
                _gather_finish(g_src, g_dst, *g_sems)

    small = lambda shape: pl.BlockSpec(shape, lambda c: (0, 0))

    def specs(cm):
        ins = [pl.BlockSpec((CHUNK, D_INNER), lambda c: (cm(c), 0)),
               pl.BlockSpec((CHUNK, gs), lambda c: (cm(c), D_INNER // gs)),
               pl.BlockSpec((CHUNK, gs), lambda c: (cm(c), D_INNER // gs + 1)),
               pl.BlockSpec((CHUNK, 128), lambda c: (cm(c), 0)),
               pl.BlockSpec((2 * N_HEADS, CHUNK), lambda c: (0, cm(c))),
               small((1, N_HEADS)), small((N_HEADS, 1)), small((1, N_HEADS)), small((N_HEADS, 1))]
        outs = [pl.BlockSpec((CHUNK, D_INNER), lambda c: (cm(c), 0)),
                pl.BlockSpec((1, N_GROUPS, hpg * hp, D_STATE), lambda c: (cm(c), 0, 0, 0))]
        return ins, outs

    ins_f, outs_f = specs(lambda c: c)
    ins_b, outs_b = specs(lambda c: nc - 1 - c)
    any_spec = pl.BlockSpec(memory_space=pl.ANY)
    one_dir = [SDS((s, D_INNER), BF16), SDS((nc, N_GROUPS, hpg * hp, D_STATE), F32)]
    state = pltpu.VMEM((N_GROUPS, hpg * hp, D_STATE), F32)
    return pl.pallas_call(
        body, grid=(nc,), in_specs=ins_f + ins_b + [any_spec] * ng, out_specs=outs_f + outs_b + [any_spec] * ng,
        out_shape=one_dir + one_dir + [SDS((N_DEV,) + g.shape, g.dtype) for g in gather],
        scratch_shapes=[state, state] + (_exchange_sems(ng) if ng else []), name=name, compiler_params=_cp("arbitrary"))(
            xbc, xbc, xbc, dt_raw, dt_t, *args_f, xbc, xbc, xbc, dt_raw, dt_t, *args_b, *gather)


def _ssd_bwd(xbc, dt_raw, dt_t, al_r, al_c, bi_r, bi_c, states, dy, off, rev, name, exchange=()):
    s = xbc.shape[0]
    nc = s // CHUNK
    hp = D_INNER // N_HEADS
    gs = N_GROUPS * D_STATE
    hpg = N_HEADS // N_GROUPS
    cm = (lambda c: c) if rev else (lambda c: nc - 1 - c)
    nx = len(exchange)

    def body(*refs):
        x_ref, b_ref, c_ref, dt_ref, dtt_ref, alr, alc, bir, bic, st_ref, dy_ref = refs[:11]
        xch_src = refs[11:11 + nx]
        dx_ref, ddt_ref, dal_ref, dbi_ref = refs[11 + nx:15 + nx]
        xch_dst = refs[15 + nx:15 + 2 * nx]
        dh_scr = refs[15 + 2 * nx]
        xch_sems = refs[16 + 2 * nx:]

        @pl.when(pl.program_id(0) == 0)
        def _():
            dh_scr[...] = jnp.zeros_like(dh_scr)
            dal_ref[...] = jnp.zeros_like(dal_ref)
            dbi_ref[...] = jnp.zeros_like(dbi_ref)
            if nx:
                _exchange_start(xch_src, xch_dst, *xch_sems)

        if nx:
            @pl.when(pl.program_id(0) == nc - 1)
            def _():
                _exchange_wait(xch_src, xch_dst, *xch_sems)

        mask, mask_t, a_r, pre, dt, cs, cs_t, tot = _ssd_common(dt_ref, dtt_ref, alr, alc, bir, bic, off, rev)
        xs = x_ref[...]
        dyv = dy_ref[...]
        rows = lax.broadcasted_iota(jnp.int32, (CHUNK, 1), 0)
        end_row = (rows == 0) if rev else (rows == CHUNK - 1)
        lane_h = lax.broadcasted_iota(jnp.int32, (1, N_HEADS), 1)
        sub_h = lax.broadcasted_iota(jnp.int32, (N_HEADS, 1), 0)
        first = _first_head_lanes()
        dcs_all = jnp.zeros((CHUNK, N_HEADS), F32)
        colw_all = jnp.zeros((N_HEADS, CHUNK), F32)
        dxsum_all = jnp.zeros((CHUNK, N_HEADS), F32)
        dxs, dbs, dcs_out = [], [], []
        for g in range(N_GROUPS):
            bg = b_ref[:, g * D_STATE:(g + 1) * D_STATE]
            cg = c_ref[:, g * D_STATE:(g + 1) * D_STATE]
            gm = lax.dot_general(cg, bg, NT, preferred_element_type=F32)
            gm_t = lax.dot_general(bg, cg, NT, preferred_element_type=F32)
            hcat = st_ref[0, g]
            dhcat = dh_scr[g]
            hb, dhb = hcat.astype(BF16), dhcat.astype(BF16)
            ch = lax.dot_general(cg, hb, NT, preferred_element_type=F32)
            z = lax.dot_general(bg, dhb, NT, preferred_element_type=F32)
            dg_sum = jnp.zeros((CHUNK, CHUNK), F32)
            dchs, xdds, t_hs = [], [], []
            for pr in range(hpg // 2):
                h0 = g * hpg + 2 * pr
                lanes = slice(h0 * hp, (h0 + 2) * hp)
                ps = slice(2 * pr * hp, (2 * pr + 2) * hp)
                pair = lambda a, b: jnp.where(first, a, b)
                cols = [cs[:, h:h + 1] for h in (h0, h0 + 1)]
                tots = [tot[:, h:h + 1] for h in (h0, h0 + 1)]
                xh = xs[:, lanes].astype(F32)
                dtc = pair(dt[:, h0:h0 + 1], dt[:, h0 + 1:h0 + 2])
                xdf = xh * dtc
                xd = xdf.astype(BF16)
                dyh = dyv[:, lanes]
                dyb = dyh.astype(BF16)
                ecs = pair(jnp.exp(cols[0]), jnp.exp(cols[1]))
                dec = pair(jnp.exp(tots[0] - cols[0]), jnp.exp(tots[1] - cols[1]))
                yoff_t = dyh * (ecs * ch[:, ps])
                dchs.append((ecs * dyh).astype(BF16))
                xdd = xdf * dec
                ddec_t = xdd * z[:, ps]
                row_terms = yoff_t - ddec_t
                xdds.append(xdd.astype(BF16))
                dxd_h = []
                for i, (h, keep) in enumerate(((h0, first), (h0 + 1, ~first))):
                    rs = slice((2 * pr + i) * hp, (2 * pr + i + 1) * hp)
                    lm = jnp.exp(jnp.where(mask, cols[i] - cs_t[h:h + 1, :], NEG))
                    m = gm * lm
                    t_h = jnp.exp(tots[i])
                    lm_t = jnp.exp(jnp.where(mask_t, cs_t[h:h + 1, :] - cols[i], NEG))
                    dxd_h.append(jnp.dot((gm_t * lm_t).astype(BF16), dyb, preferred_element_type=F32))
                    dm = lax.dot_general(_only(dyb, keep), xd, NT, preferred_element_type=F32)
                    dg_sum = dg_sum + dm * lm
                    w = dm * m
                    ddec_tot = jnp.sum(jnp.sum(_only(ddec_t, keep), axis=0, keepdims=True), axis=1, keepdims=True)
                    dtot = jnp.sum(jnp.sum(dhcat[rs] * hcat[rs], axis=0, keepdims=True), axis=1, keepdims=True) * t_h + ddec_tot
                    t_hs.append(t_h)
                    dcs = jnp.sum(_only(row_terms, keep) + w, axis=1, keepdims=True)
                    dcs = dcs + jnp.where(end_row, dtot, 0.0)
                    colw_all = colw_all + (sub_h == h).astype(F32) * jnp.sum(w, axis=0, keepdims=True)
                    dcs_all = dcs_all + dcs * (lane_h == h).astype(F32)
                dxd = pair(dxd_h[0], dxd_h[1]) + dec * z[:, ps]
                dxx = dxd * xh
                for i, (h, keep) in enumerate(((h0, first), (h0 + 1, ~first))):
                    dxsum_all = dxsum_all + jnp.sum(_only(dxx, keep), axis=1, keepdims=True) * (lane_h == h).astype(F32)
                dxs.append(dxd * dtc)
            dgs = dg_sum.astype(BF16)
            dchc = jnp.concatenate(dchs, axis=1)
            dcs_out.append(jnp.dot(dgs, bg, preferred_element_type=F32) + jnp.dot(dchc, hb, preferred_element_type=F32))
            dbs.append(lax.dot_general(dgs, cg, TN, preferred_element_type=F32)
                       + jnp.dot(jnp.concatenate(xdds, axis=1), dhb, preferred_element_type=F32))
            dh_in = lax.dot_general(dchc, cg, TN, preferred_element_type=F32)
            for r in range(hpg):
                rs = slice(r * hp, (r + 1) * hp)
                dh_scr[g, rs, :] = dh_in[rs] + t_hs[r] * dhcat[rs]
        dx_ref[...] = jnp.concatenate(dxs + dbs + dcs_out, axis=1).astype(dx_ref.dtype)
        mt = mask_t.astype(F32)
        da = (jnp.dot(mt, dcs_all, precision=HIGH, preferred_element_type=F32)
              - lax.dot_general(mt, colw_all, NT, precision=HIGH, preferred_element_type=F32))
        dal_ref[...] += jnp.sum(da * dt, axis=0, keepdims=True) * a_r
        ddt_raw = (da * a_r + dxsum_all) * _sigmoid(pre)
        ddt_ref[...] = ddt_raw
        dbi_ref[...] += jnp.sum(ddt_raw, axis=0, keepdims=True)

    small = lambda shape: pl.BlockSpec(shape, lambda c: (0, 0))
    in_specs = [pl.BlockSpec((CHUNK, D_INNER), lambda c: (cm(c), 0)),
                pl.BlockSpec((CHUNK, gs), lambda c: (cm(c), D_INNER // gs)),
                pl.BlockSpec((CHUNK, gs), lambda c: (cm(c), D_INNER // gs + 1)),
                pl.BlockSpec((CHUNK, 128), lambda c: (cm(c), 0)),
                pl.BlockSpec((2 * N_HEADS, CHUNK), lambda c: (0, cm(c))),
                small((1, N_HEADS)), small((N_HEADS, 1)), small((1, N_HEADS)), small((N_HEADS, 1)),
                pl.BlockSpec((1, N_GROUPS, hpg * hp, D_STATE), lambda c: (cm(c), 0, 0, 0)),
                pl.BlockSpec((CHUNK, D_INNER), lambda c: (cm(c), 0))]
    any_spec = pl.BlockSpec(memory_space=pl.ANY)
    return pl.pallas_call(
        body, grid=(nc,), in_specs=in_specs + [any_spec] * nx,
        out_specs=[pl.BlockSpec((CHUNK, XBC), lambda c: (cm(c), 0)), pl.BlockSpec((CHUNK, N_HEADS), lambda c: (cm(c), 0)),
                   small((1, N_HEADS)), small((1, N_HEADS))] + [any_spec] * nx,
        out_shape=[SDS((s, XBC), BF16), SDS((s, N_HEADS), F32), SDS((1, N_HEADS), F32), SDS((1, N_HEADS), F32)]
        + [SDS(a.shape, a.dtype) for a in exchange],
        scratch_shapes=[pltpu.VMEM((N_GROUPS, hpg * hp, D_STATE), F32)] + (_exchange_sems(nx) if nx else []),
        name=name, compiler_params=_cp("arbitrary"))(xbc, xbc, xbc, dt_raw, dt_t, al_r, al_c, bi_r, bi_c, states, dy, *exchange)


def _gate_fwd(yf, yb, xbc, proj, dskip_x, norm_w, mix, mix_t, name):
    s = yf.shape[0]
    tm = _pick(s, (256, 128))
    gw = D_INNER // N_GROUPS
    zc = 3 * D_MODEL // D_INNER

    def body(yf_ref, yb_ref, x_ref, z_ref, d_ref, w_ref, _m, _mt, o_ref, ot_ref):
        y = yf_ref[...].astype(F32) + yb_ref[...].astype(F32) + d_ref[...] * x_ref[...].astype(F32)
        z = z_ref[...].astype(F32)
        gt = y * (z * _sigmoid(z))
        outs = []
        for g in range(N_GROUPS):
            gg = gt[:, g * gw:(g + 1) * gw]
            outs.append(gg * lax.rsqrt(jnp.mean(gg * gg, axis=-1, keepdims=True) + EPS))
        out = (jnp.concatenate(outs, axis=1) * w_ref[...]).astype(o_ref.dtype)
        o_ref[...] = out
        ot_ref[...] = out.T

    row = pl.BlockSpec((tm, D_INNER), lambda i: (i, 0))
    vec = pl.BlockSpec((1, D_INNER), lambda i: (0, 0))
    any_spec = pl.BlockSpec(memory_space=pl.ANY)
    cb = (mix.shape[1] - D_INNER) // D_INNER
    return pl.pallas_call(
        body, grid=(s // tm,), in_specs=[row, row, row, pl.BlockSpec((tm, D_INNER), lambda i: (i, zc)), vec, vec, any_spec, any_spec],
        out_specs=[pl.BlockSpec((tm, D_INNER), lambda i: (i, cb)), pl.BlockSpec((D_INNER, tm), lambda i: (cb, i))],
        out_shape=[SDS(mix.shape, BF16), SDS(mix_t.shape, BF16)], input_output_aliases={6: 0, 7: 1},
        name=name, compiler_params=_cp("parallel"))(yf, yb, xbc, proj, dskip_x, norm_w, mix, mix_t)


def _gate_bwd(yf, yb, xbc, proj, dskip_x, norm_w, dmix, dproj, name):
    s = yf.shape[0]
    tm = _pick(s, (256, 128))
    gw = D_INNER // N_GROUPS
    zc = 3 * D_MODEL // D_INNER

    def body(yf_ref, yb_ref, x_ref, z_ref, d_ref, w_ref, do_ref, _, dy_ref, dz_ref, dxs_ref, dw_ref, dd_ref):
        xf = x_ref[...].astype(F32)
        y = yf_ref[...].astype(F32) + yb_ref[...].astype(F32) + d_ref[...] * xf
        z = z_ref[...].astype(F32)
        sg = _sigmoid(z)
        sz = z * sg
        gt = y * sz
        do = do_ref[...].astype(F32)
        dgh = do * w_ref[...]
        ghs, dgts = [], []
        for g in range(N_GROUPS):
            gg = gt[:, g * gw:(g + 1) * gw]
            r = lax.rsqrt(jnp.mean(gg * gg, axis=-1, keepdims=True) + EPS)
            gh = gg * r
            dg = dgh[:, g * gw:(g + 1) * gw]
            ghs.append(gh)
            dgts.append(r * (dg - gh * jnp.mean(dg * gh, axis=-1, keepdims=True)))
        ghat = jnp.concatenate(ghs, axis=1)
        dgt = jnp.concatenate(dgts, axis=1)
        dy = dgt * sz
        dy_ref[...] = dy
        dz_ref[...] = (dgt * y * (sg * (1.0 + z * (1.0 - sg)))).astype(dz_ref.dtype)
        dxs_ref[...] = (dy * d_ref[...]).astype(dxs_ref.dtype)

        @pl.when(pl.program_id(0) == 0)
        def _():
            dw_ref[...] = jnp.zeros_like(dw_ref)
            dd_ref[...] = jnp.zeros_like(dd_ref)

        dw_ref[...] += jnp.sum(do * ghat, axis=0, keepdims=True)
        dd_ref[...] += jnp.sum(dy * xf, axis=0, keepdims=True)

    row = pl.BlockSpec((tm, D_INNER), lambda i: (i, 0))
    vec = pl.BlockSpec((1, D_INNER), lambda i: (0, 0))
    dz_shape, _, more, more_specs, alias = _slab(s, D_INNER, BF16, (dproj.shape[1], zc * D_INNER, dproj), 7, out_idx=1)
    return pl.pallas_call(
        body, grid=(s // tm,),
        in_specs=[row, row, row, pl.BlockSpec((tm, D_INNER), lambda i: (i, zc)), vec, vec, pl.BlockSpec((tm, D_INNER), lambda i: (i, 1))] + more_specs,
        out_specs=[row, pl.BlockSpec((tm, D_INNER), lambda i: (i, zc)), row, vec, vec],
        out_shape=[SDS((s, D_INNER), F32), dz_shape, SDS((s, D_INNER), BF16), SDS((1, D_INNER), F32), SDS((1, D_INNER), F32)],
        input_output_aliases=alias, name=name, compiler_params=_cp("arbitrary"))(yf, yb, xbc, proj, dskip_x, norm_w, dmix, *more)


def _adamw(parts, w, m, v, name, exchange=()):
    r, c = w.shape
    tr = _pick(r, (256, 352, 128))
    n = r // tr
    nx = len(exchange)

    def body(*refs):
        p_ref, w_ref, m_ref, v_ref = refs[:4]
        xch_src = refs[4:4 + nx]
        g_ref, d_ref, nm_ref, nv_ref = refs[4 + nx:8 + nx]
        xch_dst = refs[8 + nx:8 + 2 * nx]
        xch_sems = refs[8 + 2 * nx:]
        if nx:
            @pl.when(pl.program_id(0) == 0)
            def _():
                _exchange_start(xch_src, xch_dst, *xch_sems)

        g = p_ref[0].astype(F32)
        for i in range(1, N_DEV):
            g = g + p_ref[i].astype(F32)
        mn = B1 * m_ref[...] + (1.0 - B1) * g
        vn = B2 * v_ref[...] + (1.0 - B2) * (g * g)
        m_hat = mn / (1.0 - B1 ** STEP)
        v_hat = vn / (1.0 - B2 ** STEP)
        g_ref[...] = g
        d_ref[...] = -LR * (m_hat / (jnp.sqrt(v_hat) + AEPS) + WD * w_ref[...])
        nm_ref[...] = mn
        nv_ref[...] = vn

        if nx:
            @pl.when(pl.program_id(0) == n - 1)
            def _():
                _exchange_wait(xch_src, xch_dst, *xch_sems)

    blk = pl.BlockSpec((tr, c), lambda i: (i, 0))
    any_spec = pl.BlockSpec(memory_space=pl.ANY)
    return pl.pallas_call(
        body, grid=(n,), in_specs=[pl.BlockSpec((N_DEV, tr, c), lambda i: (0, i, 0)), blk, blk, blk] + [any_spec] * nx,
        out_specs=[blk, blk, blk, blk] + [any_spec] * nx,
        out_shape=[SDS((r, c), F32)] * 4 + [SDS(e.shape, e.dtype) for e in exchange],
        scratch_shapes=_exchange_sems(nx) if nx else [],
        name=name, compiler_params=_cp("arbitrary" if nx else "parallel"))(parts, w, m, v, *exchange)


def _sum_parts(parts, name):
    _, r, c = parts.shape

    def body(p_ref, o_ref):
        g = p_ref[0]
        for i in range(1, N_DEV):
            g = g + p_ref[i]
        o_ref[...] = g

    return pl.pallas_call(body, out_shape=SDS((r, c), F32), name=name)(parts)


def _adamw_small(gs, ws, ms, vs, name):
    n = len(gs)

    def body(*refs):
        g_refs, w_refs, m_refs, v_refs = refs[:n], refs[n:2 * n], refs[2 * n:3 * n], refs[3 * n:4 * n]
        d_refs, nm_refs, nv_refs = refs[4 * n:5 * n], refs[5 * n:6 * n], refs[6 * n:7 * n]
        for i in range(n):
            gv = g_refs[i][...]
            mn = B1 * m_refs[i][...] + (1.0 - B1) * gv
            vn = B2 * v_refs[i][...] + (1.0 - B2) * (gv * gv)
            m_hat = mn / (1.0 - B1 ** STEP)
            v_hat = vn / (1.0 - B2 ** STEP)
            d_refs[i][...] = -LR * (m_hat / (jnp.sqrt(v_hat) + AEPS) + WD * w_refs[i][...])
            nm_refs[i][...] = mn
            nv_refs[i][...] = vn

    outs = pl.pallas_call(body, out_shape=[SDS(g.shape, F32) for g in gs] * 3, name=name)(*gs, *ws, *ms, *vs)
    return outs[:n], outs[n:2 * n], outs[2 * n:]


def _my_index():
    return 4 * lax.axis_index("x") + 2 * lax.axis_index("y") + lax.axis_index("c")


def _to_pattern(t, d):
    if d == 1:
        return t
    s, w = t.shape
    return t.reshape(s // d, d, w).transpose(1, 0, 2).reshape(s, w)


def _from_pattern(t, d):
    if d == 1:
        return t
    s, w = t.shape
    return t.reshape(d, s // d, w).transpose(1, 0, 2).reshape(s, w)


def _pad_lanes(t, n):
    return jnp.pad(t, ((0, 0), (0, n - t.shape[1])))


def _to_shards(g, axis):
    r, c = g.shape
    if axis == 0:
        return g.reshape(N_DEV, r // N_DEV, c)
    return g.reshape(r, N_DEV, c // N_DEV).transpose(1, 0, 2)


def _local_step(x, target, p, first_shards=None, late_shards=(), early_exchange=True):
    s = x.shape[0]
    tabs_f, tabs_b = _rope_tables(s)
    expand = jnp.asarray(np.repeat(np.eye(N_HEADS, dtype=np.float32), HEAD_DIM, axis=1))
    al_r = {"f": p["a_log_f"], "b": p["a_log_b"]}
    bi_r = {"f": p["dt_bias_f"], "b": p["dt_bias_b"]}
    dskip_x = jnp.repeat(p["d_skip"], D_INNER // N_HEADS, axis=1)

    h1, h1t, *got = _rmsnorm_fwd(x, p["norm1_w"], "norm1_fwd", gather=first_shards[0] if first_shards else ())
    if first_shards:
        p = dict(p, **first_shards[1](got))
    w_main, w_dt = p["w_in"][:, :MAIN_W], _pad_lanes(p["w_in"][:, MAIN_W:], 128)
    ssm_w3, ffn_w3 = p["ssm_conv_w"].T, p["ffn_conv_w"].T
    proj = _matmul(h1, w_main, name="in_proj")
    dt_raw = _matmul(h1, w_dt, name="in_proj_dt", out_dtype=F32)
    dt_t = dt_raw[:, :2 * N_HEADS].T
    perms = {d: _perm_matrices(d) for d in DILATIONS[1:]}
    qkv = _rope_fwd(proj, tabs_f, perms, "rope_fwd")
    v_col = 2
    os_, lses = [], []
    for d, qkv_p in zip(DILATIONS, qkv):
        o_p, lse_p = _attn_fwd(qkv_p, qkv_p, v_col, s // d, f"attn_fwd_d{d}")
        os_.append(o_p)
        lses.append(_from_pattern(lse_p, d))
    mix, mix_t, lse_tot = _attn_combine(os_, lses, perms, expand, "attn_combine")

    xbc = _conv(proj, 3 * D_MODEL + D_INNER, XBC, ssm_w3, p["ssm_conv_b"], True, "ssm_conv_fwd")
    col = lambda r: r.reshape(N_HEADS, 1)
    ssd_args = {k: (al_r[k], col(al_r[k]), bi_r[k], col(bi_r[k])) for k in ("f", "b")}
    yf, st_f, yb, st_b, *got = _ssd_fwd(xbc, dt_raw, dt_t, ssd_args["f"], ssd_args["b"], "ssd_fwd", gather=late_shards)
    if late_shards:
        p = dict(p, w_out=got[0].reshape(2 * D_MODEL, D_MODEL), w_down=got[2].reshape(D_FF, D_MODEL),
                 w_up=got[1].transpose(1, 0, 2).reshape(D_MODEL, 2 * D_FF))
    mix, mix_t = _gate_fwd(yf, yb, xbc, proj, dskip_x, p["ssm_norm_w"], mix, mix_t, "ssm_gate_fwd")

    x2 =_matmul(mix, p["w_out"], name="out_proj", out_dtype=F32, residual=x)
    h2, h2t = _rmsnorm_fwd(x2, p["norm2_w"], "norm2_fwd")
    u = _matmul(h2, p["w_up"], name="ffn_up")
    act, act_t = _ffn_gate_fwd(u, ffn_w3, p["ffn_conv_b"], "ffn_gate_fwd")
    x3 = _matmul(act, p["w_down"], name="ffn_down", out_dtype=F32, residual=x2)

    dx3, dx3b, g_final, loss = _final_norm_loss(x3, p["final_norm_w"].reshape(1, D_MODEL), target, "final_norm_loss")
    g_w_down = _matmul(act_t, dx3b, name="dw_down")
    dact = _matmul(dx3b, p["w_down"], name="d_act", trans_b=True)
    dug, duu, dwg, dwu, dbg, dbu = _ffn_gate_bwd(u, ffn_w3, p["ffn_conv_b"], dact, "ffn_gate_bwd")
    du = _conv(dug, 0, D_FF, ffn_w3, p["ffn_conv_b"], False, "ffn_conv_bwd_gate", slab=(2 * D_FF, 0, None), transpose=True)
    du = _conv(duu, 0, D_FF, ffn_w3, p["ffn_conv_b"], False, "ffn_conv_bwd_up", slab=(2 * D_FF, D_FF, du), transpose=True, wcol0=D_FF)
    g_w_up = _matmul(h2t, du, name="dw_up")
    none_a, none_w = jnp.zeros((s, 128), BF16), jnp.zeros((D_MODEL, 128), BF16)
    dx2, dx2b, g_norm2 = _proj_norm_bwd(du, p["w_up"], none_a, none_w, x2, p["norm2_w"], dx3, "ffn_up_norm2_bwd")
    g_w_out = _matmul(mix_t, dx2b, name="dw_out")
    dmix = _matmul(dx2b, p["w_out"], name="d_mix", trans_b=True)

    delta, do_pat = _attn_delta(dmix, mix, expand.T, perms, "attn_delta")
    dqs, dks, dvs = [], [], []
    for d, qkv_p, do_p in zip(DILATIONS, qkv, [dmix] + do_pat):
        lse_p, dl_p = _to_pattern(lse_tot, d), _to_pattern(delta, d)
        dqs.append(_attn_bwd_dq(qkv_p, qkv_p, v_col, do_p, lse_p, dl_p, s // d, f"attn_bwd_dq_d{d}"))
        dk, dv = _attn_bwd_dkv(qkv_p, qkv_p, v_col, do_p, lse_p.T, dl_p.T, s // d, f"attn_bwd_dkv_d{d}")
        dks.append(dk)
        dvs.append(dv)
    dproj = _sum3_rope(dqs, perms, tabs_b, "rope_bwd_q", slab=(MAIN_W, 0, None))
    dproj = _sum3_rope(dks, perms, tabs_b, "rope_bwd_k", slab=(MAIN_W, D_MODEL, dproj))
    dproj = _sum3_rope(dvs, perms, None, "sum_dv", slab=(MAIN_W, 2 * D_MODEL, dproj))

    dy, dproj, dxs_skip, g_ssm_norm, g_dskip_lanes = _gate_bwd(yf, yb, xbc, proj, dskip_x, p["ssm_norm_w"], dmix, dproj, "ssm_gate_bwd")
    early_f = [_to_shards(g_w_up, 1), _to_shards(g_w_down, 0)] if early_exchange else []
    early_b = [_to_shards(g_w_out, 0)] if early_exchange else []
    dxbc_f, ddt_f, g_al_f, g_bi_f, *got_f = _ssd_bwd(xbc, dt_raw, dt_t, *ssd_args["f"], st_f, dy, 0, False, "ssd_bwd_f", exchange=early_f)
    dxbc_b, ddt_b, g_al_b, g_bi_b, *got_b = _ssd_bwd(xbc, dt_raw, dt_t, *ssd_args["b"], st_b, dy, N_HEADS, True, "ssd_bwd_b",
                                                     exchange=early_b)
    if early_exchange:
        (g_w_up, g_w_down), (g_w_out,) = got_f, got_b
    dpre, g_ssm_w3, g_ssm_cb = _conv_silu_bwd(proj, 3 * D_MODEL + D_INNER, XBC, ssm_w3, p["ssm_conv_b"],
                                              [dxbc_f, dxbc_b, dxs_skip], [XBC, XBC, D_INNER], "ssm_conv_bwd")
    dproj = _conv(dpre, 0, XBC, ssm_w3, p["ssm_conv_b"], False, "ssm_conv_bwd_x", transpose=True,
                  slab=(MAIN_W, 3 * D_MODEL + D_INNER, dproj))

    ddt =_pad_lanes(jnp.concatenate([ddt_f, ddt_b], axis=1), 128).astype(BF16)
    g_w_main = _matmul(h1t, dproj, name="dw_in")
    g_w_dt = _matmul(h1t, ddt, name="dw_in_dt")
    g_w_in = jnp.concatenate([g_w_main, g_w_dt[:, :2 * N_HEADS]], axis=1)
    late = [_to_shards(g_w_in, 1)] if early_exchange else []
    grad_x, _, g_norm1, *got = _proj_norm_bwd(dproj, w_main, ddt, w_dt, x, p["norm1_w"], dx2, "in_proj_norm1_bwd", exchange=late)
    if early_exchange:
        g_w_in = got[0]

    g_dskip = jnp.sum(g_dskip_lanes.reshape(N_HEADS, D_INNER // N_HEADS), axis=1).reshape(1, N_HEADS)
    small = {
        "norm1_w": g_norm1, "ssm_conv_w": g_ssm_w3.T, "ssm_conv_b": g_ssm_cb, "a_log_f": g_al_f, "a_log_b": g_al_b,
        "dt_bias_f": g_bi_f, "dt_bias_b": g_bi_b, "d_skip": g_dskip, "ssm_norm_w": g_ssm_norm, "norm2_w": g_norm2,
        "ffn_conv_w": jnp.concatenate([dwg, dwu], axis=1).T, "ffn_conv_b": jnp.concatenate([dbg, dbu], axis=1), "final_norm_w": g_final,
    }
    big = {"w_in": g_w_in, "w_out": g_w_out, "w_up": g_w_up, "w_down": g_w_down}
    return loss[0, 0], grad_x, big, small


SMALL_ORDER = ("norm1_w", "ssm_conv_w", "ssm_conv_b", "a_log_f", "a_log_b", "dt_bias_f", "dt_bias_b", "d_skip",
               "ssm_norm_w", "norm2_w", "ffn_conv_w", "ffn_conv_b", "final_norm_w")
SHARDED_SMALL = ("ssm_conv_w", "ffn_conv_w")
BIG_ORDER = ("w_in", "w_out", "w_up", "w_down")


def _pack(vals):
    rows = []
    for v in vals:
        f = v.reshape(-1).astype(F32)
        n = -(-f.shape[0] // 128) * 128
        rows.append(jnp.pad(f, (0, n - f.shape[0])).reshape(-1, 128))
    out = jnp.concatenate(rows, axis=0)
    pad = -out.shape[0] % 8
    return jnp.pad(out, ((0, pad), (0, 0)))


def _unpack(packed, shapes):
    out, r = [], 0
    for shp in shapes:
        n = math.prod(shp)
        nr = -(-n // 128)
        out.append(packed[r:r + nr].reshape(-1)[:n].reshape(shp))
        r += nr
    return out


def kernel(x, norm1_w, w_in, ssm_conv_w, ssm_conv_b, a_log_f, a_log_b, dt_bias_f, dt_bias_b, d_skip, ssm_norm_w, w_out, norm2_w, w_up, ffn_conv_w, ffn_conv_b, w_down, final_norm_w, loss_target, m_norm1_w, m_w_in, m_ssm_conv_w, m_ssm_conv_b, m_a_log_f, m_a_log_b, m_dt_bias_f, m_dt_bias_b, m_d_skip, m_ssm_norm_w, m_w_out, m_norm2_w, m_w_up, m_ffn_conv_w, m_ffn_conv_b, m_w_down, m_final_norm_w, v_norm1_w, v_w_in, v_ssm_conv_w, v_ssm_conv_b, v_a_log_f, v_a_log_b, v_dt_bias_f, v_dt_bias_b, v_d_skip, v_ssm_norm_w, v_w_out, v_norm2_w, v_w_up, v_ffn_conv_w, v_ffn_conv_b, v_w_down, v_final_norm_w):
    w = dict(norm1_w=norm1_w, w_in=w_in, ssm_conv_w=ssm_conv_w, ssm_conv_b=ssm_conv_b, a_log_f=a_log_f, a_log_b=a_log_b,
             dt_bias_f=dt_bias_f, dt_bias_b=dt_bias_b, d_skip=d_skip, ssm_norm_w=ssm_norm_w, w_out=w_out, norm2_w=norm2_w,
             w_up=w_up, ffn_conv_w=ffn_conv_w, ffn_conv_b=ffn_conv_b, w_down=w_down, final_norm_w=final_norm_w)
    mo = dict(norm1_w=m_norm1_w, w_in=m_w_in, ssm_conv_w=m_ssm_conv_w, ssm_conv_b=m_ssm_conv_b, a_log_f=m_a_log_f, a_log_b=m_a_log_b,
              dt_bias_f=m_dt_bias_f, dt_bias_b=m_dt_bias_b, d_skip=m_d_skip, ssm_norm_w=m_ssm_norm_w, w_out=m_w_out, norm2_w=m_norm2_w,
              w_up=m_w_up, ffn_conv_w=m_ffn_conv_w, ffn_conv_b=m_ffn_conv_b, w_down=m_w_down, final_norm_w=m_final_norm_w)
    vo = dict(norm1_w=v_norm1_w, w_in=v_w_in, ssm_conv_w=v_ssm_conv_w, ssm_conv_b=v_ssm_conv_b, a_log_f=v_a_log_f, a_log_b=v_a_log_b,
              dt_bias_f=v_dt_bias_f, dt_bias_b=v_dt_bias_b, d_skip=v_d_skip, ssm_norm_w=v_ssm_norm_w, w_out=v_w_out, norm2_w=v_norm2_w,
              w_up=v_w_up, ffn_conv_w=v_ffn_conv_w, ffn_conv_b=v_ffn_conv_b, w_down=v_w_down, final_norm_w=v_final_norm_w)
    me = _my_index()

    def first_full(got):
        g_in, g_conv = got
        conv_rows = [_unpack(g_conv[i], [ssm_conv_w.shape[1:], ffn_conv_w.shape[1:]]) for i in range(N_DEV)]
        return {"w_in": g_in.transpose(1, 0, 2).reshape(D_MODEL, N_DEV * w_in.shape[2]),
                "ssm_conv_w": jnp.concatenate([c[0] for c in conv_rows], axis=0),
                "ffn_conv_w": jnp.concatenate([c[1] for c in conv_rows], axis=0)}

    first = ([w["w_in"][0].astype(BF16), _pack([w["ssm_conv_w"][0], w["ffn_conv_w"][0]])], first_full)
    full = {k: w[k] for k in ("norm1_w", "ssm_conv_b", "a_log_f", "a_log_b", "dt_bias_f", "dt_bias_b", "d_skip", "ssm_norm_w", "norm2_w",
                              "ffn_conv_b", "final_norm_w")}
    late = [w["w_out"][0].astype(BF16), w["w_up"][0].astype(BF16), w["w_down"][0].astype(BF16)]

    loss_part, grad_x, big, small = _local_step(x[0], loss_target[0], full, first, late)

    small_shapes = [(1,)] + [small[k].shape for k in SMALL_ORDER]
    packed = _pack([loss_part] + [small[k] for k in SMALL_ORDER])
    out_small = jnp.broadcast_to(packed[None], (N_DEV,) + packed.shape)
    r_in, r_out, r_up, r_down = big["w_in"], big["w_out"], big["w_up"], big["w_down"]

    outs_g, outs_d, outs_m, outs_v = {}, {}, {}, {}
    for k, parts in zip(BIG_ORDER, (r_in, r_out, r_up, r_down)):
        host = [out_small] if k == BIG_ORDER[0] else []
        g, dlt, nm, nv, *got = _adamw(parts, w[k][0], mo[k][0], vo[k][0], f"adamw_{k}", exchange=host)
        if host:
            (r_small,) = got
        outs_g[k], outs_d[k], outs_m[k], outs_v[k] = g[None], dlt[None], nm[None], nv[None]
    tot = _unpack(_sum_parts(r_small, "small_grads_sum"), small_shapes)
    loss = tot[0][0]
    gs = dict(zip(SMALL_ORDER, tot[1:]))
    g_own = {}
    for k in SMALL_ORDER:
        if k in SHARDED_SMALL:
            rows = w[k].shape[1]
            g_own[k] = lax.dynamic_slice_in_dim(gs[k], me * rows, rows, axis=0)[None]
        else:
            g_own[k] = gs[k].reshape(w[k].shape)
    two_d = lambda a: a.reshape(-1, a.shape[-1])
    d_s, m_s, v_s = _adamw_small([two_d(g_own[k]) for k in SMALL_ORDER], [two_d(w[k]) for k in SMALL_ORDER],
                                 [two_d(mo[k]) for k in SMALL_ORDER], [two_d(vo[k]) for k in SMALL_ORDER], "adamw_small")
    for k, a, b, c in zip(SMALL_ORDER, d_s, m_s, v_s):
        shp = w[k].shape
        outs_g[k], outs_d[k], outs_m[k], outs_v[k] = g_own[k], a.reshape(shp), b.reshape(shp), c.reshape(shp)

    order = ("norm1_w", "w_in", "ssm_conv_w", "ssm_conv_b", "a_log_f", "a_log_b", "dt_bias_f", "dt_bias_b", "d_skip", "ssm_norm_w",
             "w_out", "norm2_w", "w_up", "ffn_conv_w", "ffn_conv_b", "w_down", "final_norm_w")
    return (loss, grad_x[None], *[outs_g[k] for k in order], *[outs_d[k] for k in order],
            *[outs_m[k] for k in order], *[outs_v[k] for k in order])
```

```python
import math

import numpy as np
import jax
import jax.numpy as jnp
from jax import lax
from jax.experimental import pallas as pl
from jax.experimental.pallas import tpu as pltpu

F32 = jnp.float32
BF16 = jnp.bfloat16
SDS = jax.ShapeDtypeStruct

N_DEV = 8
D_MODEL = 1024
N_HEADS = 16
HEAD_DIM = 64
ROPE_DIM = 16
ROPE_THETA = 500000.0
DILATIONS = (1, 4, 16)
BAND_HALF = 64
D_INNER = 1024
N_GROUPS = 4
D_STATE = 128
CHUNK = 128
XBC = D_INNER + 2 * N_GROUPS * D_STATE
D_FF = 2816
MAIN_W = 3 * D_MODEL + D_INNER + XBC
EPS = 1e-6
LR, B1, B2, AEPS, WD, STEP = 0.001, 0.9, 0.999, 1e-08, 0.01, 10
NEG = -1e30
VMEM_LIMIT = 56 * 1024 * 1024
MESH = pl.DeviceIdType.MESH
HIGH = lax.Precision.HIGHEST
NT = (((1,), (1,)), ((), ()))
TN = (((0,), (0,)), ((), ()))


def _cp(*sem):
    return pltpu.CompilerParams(dimension_semantics=sem, vmem_limit_bytes=VMEM_LIMIT)


def _pick(n, cands):
    for c in cands:
        if n % c == 0:
            return c
    raise ValueError(f"no tile for {n}")


def _sigmoid(x):
    return 1.0 / (1.0 + jnp.exp(-x))


def _dot_01(x, m01):
    mb = m01.astype(BF16)
    out, r = None, x
    for _ in range(3):
        p = r.astype(BF16)
        r = r - p.astype(F32)
        t = jnp.dot(p, mb, preferred_element_type=F32)
        out = t if out is None else out + t
    return out


def _softplus(x):
    return jnp.maximum(x, 0.0) + jnp.log1p(jnp.exp(-jnp.abs(x)))


def _slab(s, width, dtype, slab, n_in, out_idx=0):
    if slab is None:
        return SDS((s, width), dtype), 0, [], [], {}
    total, col0, into = slab
    if into is None:
        return SDS((s, total), dtype), col0, [], [], {}
    return SDS((s, total), dtype), col0, [into], [pl.BlockSpec(memory_space=pl.ANY)], {n_in: out_idx}


def _matmul(a, b, *, name, trans_b=False, out_dtype=BF16, residual=None):
    m, k = a.shape
    n = b.shape[0] if trans_b else b.shape[1]
    tk = k if k <= 2048 else _pick(k, (2048, 1408, 1024, 512))
    nk = k // tk
    if nk == 1:
        tm = _pick(m, (2048, 1408, 1024, 512, 256, 128))
        tn = _pick(n, (512, 256, 128))
    else:
        tm = _pick(m, (1024, 1408, 512, 256, 128))
        tn = _pick(n, (1024, 1408, 512, 256, 128))
    dn = NT if trans_b else (((1,), (0,)), ((), ()))

    def body(*refs):
        a_ref, b_ref = refs[0], refs[1]
        o_ref, acc = refs[-2], refs[-1]
        kk = pl.program_id(2)

        @pl.when(kk == 0)
        def _():
            acc[...] = jnp.zeros_like(acc)

        acc[...] += lax.dot_general(a_ref[...], b_ref[...], dn, preferred_element_type=F32)

        @pl.when(kk == nk - 1)
        def _():
            r = acc[...]
            if residual is not None:
                r = r + refs[2][...].astype(F32)
            o_ref[...] = r.astype(o_ref.dtype)

    in_specs = [pl.BlockSpec((tm, tk), lambda i, j, kk: (i, kk)),
                pl.BlockSpec((tn, tk), lambda i, j, kk: (j, kk)) if trans_b else pl.BlockSpec((tk, tn), lambda i, j, kk: (kk, j))]
    args = [a, b]
    if residual is not None:
        in_specs.append(pl.BlockSpec((tm, tn), lambda i, j, kk: (i, j)))
        args.append(residual)
    return pl.pallas_call(
        body, grid=(m // tm, n // tn, nk), in_specs=in_specs,
        out_specs=pl.BlockSpec((tm, tn), lambda i, j, kk: (i, j)),
        out_shape=SDS((m, n), out_dtype), scratch_shapes=[pltpu.VMEM((tm, tn), F32)],
        name=name, compiler_params=_cp("parallel", "parallel", "arbitrary"))(*args)


def _rmsnorm_fwd(x, w, name, gather=()):
    s, d = x.shape
    tm = _pick(s, (512, 128))
    n = s // tm
    ng = len(gather)

    def body(*refs):
        x_ref, w_ref = refs[:2]
        g_src = refs[2:2 + ng]
        o_ref, ot_ref = refs[2 + ng:4 + ng]
        g_dst = refs[4 + ng:4 + 2 * ng]
        g_sems = refs[4 + 2 * ng:]
        step = pl.program_id(0)
        if ng:
            @pl.when(step == 0)
            def _():
                _gather_start(g_src, g_dst, *g_sems)

        xf = x_ref[...]
        r = lax.rsqrt(jnp.mean(xf * xf, axis=-1, keepdims=True) + EPS)
        out = (xf * r * w_ref[...]).astype(o_ref.dtype)
        o_ref[...] = out
        ot_ref[...] = out.T

        if ng:
            @pl.when(step == n - 1)
            def _():
                _gather_forward(g_src, g_dst, *g_sems)
                _gather_finish(g_src, g_dst, *g_sems)

    any_spec = pl.BlockSpec(memory_space=pl.ANY)
    return pl.pallas_call(
        body, grid=(n,), in_specs=[pl.BlockSpec((tm, d), lambda i: (i, 0)), pl.BlockSpec((1, d), lambda i: (0, 0))] + [any_spec] * ng,
        out_specs=[pl.BlockSpec((tm, d), lambda i: (i, 0)), pl.BlockSpec((d, tm), lambda i: (0, i))] + [any_spec] * ng,
        out_shape=[SDS((s, d), BF16), SDS((d, s), BF16)] + [SDS((N_DEV,) + g.shape, g.dtype) for g in gather],
        scratch_shapes=_exchange_sems(ng) if ng else [],
        name=name, compiler_params=_cp("arbitrary" if ng else "parallel"))(x, w, *gather)


def _proj_norm_bwd(da, wt, da2, wt2, x, w, dres, name, exchange=()):
    s, k = da.shape
    d = x.shape[1]
    tm = _pick(s, (512, 128))
    tk = _pick(k, (1408, 1024, 512))
    nk, ni = k // tk, s // tm
    nx = len(exchange)
    last = nk - 1

    def body(*refs):
        a_ref, b_ref, a2_ref, b2_ref, x_ref, w_ref, dres_ref = refs[:7]
        xch_src = refs[7:7 + nx]
        dx_ref, dxb_ref, dw_ref = refs[7 + nx:10 + nx]
        xch_dst = refs[10 + nx:10 + 2 * nx]
        acc = refs[10 + 2 * nx]
        xch_sems = refs[11 + 2 * nx:]
        kk, i = pl.program_id(0), pl.program_id(1)

        @pl.when((i == 0) & (kk == 0))
        def _():
            dw_ref[...] = jnp.zeros_like(dw_ref)
            if nx:
                _exchange_start(xch_src, xch_dst, *xch_sems)

        @pl.when(kk == 0)
        def _():
            acc[i] = lax.dot_general(a2_ref[...], b2_ref[...], NT, preferred_element_type=F32)

        acc[i] += lax.dot_general(a_ref[...], b_ref[...], NT, preferred_element_type=F32)

        @pl.when(kk == last)
        def _():
            dh = acc[i]
            xf = x_ref[...]
            r = lax.rsqrt(jnp.mean(xf * xf, axis=-1, keepdims=True) + EPS)
            xhat = xf * r
            g = dh * w_ref[...]
            dx = dres_ref[...] + r * (g - xhat * jnp.mean(g * xhat, axis=-1, keepdims=True))
            dx_ref[...] = dx
            dxb_ref[...] = dx.astype(dxb_ref.dtype)
            dw_ref[...] += jnp.sum(dh * xhat, axis=0, keepdims=True)

        if nx:
            @pl.when((i == ni - 1) & (kk == last))
            def _():
                _exchange_wait(xch_src, xch_dst, *xch_sems)

    row_last = pl.BlockSpec((tm, d), lambda kk, i: (jnp.where(kk == last, i, 0), 0))
    vec = pl.BlockSpec((1, d), lambda kk, i: (0, 0))
    any_spec = pl.BlockSpec(memory_space=pl.ANY)
    in_specs = [pl.BlockSpec((tm, tk), lambda kk, i: (i, kk)), pl.BlockSpec((d, tk), lambda kk, i: (0, kk)),
                pl.BlockSpec((tm, da2.shape[1]), lambda kk, i: (jnp.where(kk == 0, i, 0), 0)),
                pl.BlockSpec((d, wt2.shape[1]), lambda kk, i: (0, 0)),
                row_last, vec, row_last]
    return pl.pallas_call(
        body, grid=(nk, ni), in_specs=in_specs + [any_spec] * nx, out_specs=[row_last, row_last, vec] + [any_spec] * nx,
        out_shape=[SDS((s, d), F32), SDS((s, d), BF16), SDS((1, d), F32)] + [SDS(e.shape, e.dtype) for e in exchange],
        scratch_shapes=[pltpu.VMEM((ni, tm, d), F32)] + (_exchange_sems(nx) if nx else []),
        name=name, compiler_params=_cp("arbitrary", "arbitrary"))(da, wt, da2, wt2, x, w, dres, *exchange)


def _final_norm_loss(x, w, target, name):
    s, d = x.shape
    tm = _pick(s, (512, 128))

    def body(x_ref, w_ref, t_ref, dx_ref, dxb_ref, dw_ref, loss_ref):
        xf = x_ref[...]
        r = lax.rsqrt(jnp.mean(xf * xf, axis=-1, keepdims=True) + EPS)
        xhat = xf * r
        wv = w_ref[...]
        e = xhat * wv - t_ref[...]
        dy = e * (1.0 / d)
        g = dy * wv
        dx = r * (g - xhat * jnp.mean(g * xhat, axis=-1, keepdims=True))
        dx_ref[...] = dx
        dxb_ref[...] = dx.astype(dxb_ref.dtype)

        @pl.when(pl.program_id(0) == 0)
        def _():
            dw_ref[...] = jnp.zeros_like(dw_ref)
            loss_ref[...] = jnp.zeros_like(loss_ref)

        dw_ref[...] += jnp.sum(dy * xhat, axis=0, keepdims=True)
        loss_ref[...] += jnp.sum(jnp.sum(e * e, axis=1, keepdims=True), axis=0, keepdims=True) * (0.5 / d)

    row = pl.BlockSpec((tm, d), lambda i: (i, 0))
    vec = pl.BlockSpec((1, d), lambda i: (0, 0))
    return pl.pallas_call(
        body, grid=(s // tm,), in_specs=[row, vec, row], out_specs=[row, row, vec, pl.BlockSpec((1, 128), lambda i: (0, 0))],
        out_shape=[SDS((s, d), F32), SDS((s, d), BF16), SDS((1, d), F32), SDS((1, 128), F32)],
        name=name, compiler_params=_cp("arbitrary"))(x, w, target)


def _rope_tables(s):
    half = ROPE_DIM // 2
    f32 = np.float32
    inv_freq = np.power(f32(ROPE_THETA), -np.arange(half, dtype=f32) * f32(2.0) / f32(ROPE_DIM)).astype(f32)
    ang = (np.arange(s, dtype=f32)[:, None] * inv_freq[None, :]).astype(f32)
    cos, sin = np.cos(ang).astype(f32), np.sin(ang).astype(f32)
    z = np.zeros((s, HEAD_DIM - ROPE_DIM), f32)
    zh = np.zeros((s, half), f32)
    c = np.concatenate([cos, cos, z + 1.0], axis=1)
    sa = np.concatenate([zh, sin, z], axis=1)
    sb = np.concatenate([-sin, zh, z], axis=1)
    two = lambda t: np.concatenate([t, t], axis=1)
    c, sa, sb = two(c), two(sa), two(sb)
    fwd, bwd = (c, sa, sb), (c, np.roll(sb, half, axis=1), np.roll(sa, -half, axis=1))
    return tuple(jnp.asarray(t) for t in fwd), tuple(jnp.asarray(t) for t in bwd)


PERM_TILE = 256


def _perm_matrices(d):
    n = PERM_TILE // d
    o = np.arange(PERM_TILE)
    p = np.zeros((PERM_TILE, PERM_TILE), np.float32)
    p[o, (o % n) * d + o // n] = 1.0
    return jnp.asarray(p, dtype=BF16), jnp.asarray(p.T.copy(), dtype=BF16)


def _store_pattern(o_ref, tile, perm_ref, d, cols=slice(None)):
    n = PERM_TILE // d
    z = jnp.dot(perm_ref[...], tile, preferred_element_type=F32).astype(o_ref.dtype)
    for r in range(d):
        o_ref[r, :, cols] = z[r * n:(r + 1) * n]


def _load_pattern(x_ref, perm_ref, d):
    tile = jnp.concatenate([x_ref[r] for r in range(d)], axis=0)
    return jnp.dot(perm_ref[...], tile, preferred_element_type=F32)


def _pattern_spec(d, w, col=0):
    return pl.BlockSpec((d, PERM_TILE // d, w), lambda i, *_: (0, i, col))


def _rope_fwd(proj, tabs, perms, name):
    s = proj.shape[0]
    tm = PERM_TILE
    half = ROPE_DIM // 2
    wb = D_MODEL
    nd = len(DILATIONS) - 1

    def body(x_ref, c_ref, sa_ref, sb_ref, *rest):
        perm_refs, o_ref, op_refs = rest[:nd], rest[nd], rest[nd + 1:]
        is_v = pl.program_id(1) == 2
        qs = jnp.where(pl.program_id(1) == 0, HEAD_DIM ** -0.5, 1.0)
        c = jnp.where(is_v, 1.0, c_ref[...]) * qs
        sa = jnp.where(is_v, 0.0, sa_ref[...]) * qs
        sb = jnp.where(is_v, 0.0, sb_ref[...]) * qs
        for j in range(wb // 128):
            x = x_ref[:, j * 128:(j + 1) * 128].astype(F32)
            o_ref[:, j * 128:(j + 1) * 128] = (x * c + pltpu.roll(x, half, 1) * sa + pltpu.roll(x, 128 - half, 1) * sb).astype(o_ref.dtype)
        y = o_ref[...]
        for d, perm_ref, op_ref in zip(DILATIONS[1:], perm_refs, op_refs):
            _store_pattern(op_ref, y, perm_ref, d)

    blk = pl.BlockSpec((tm, wb), lambda i, j: (i, j))
    tab = pl.BlockSpec((tm, 128), lambda i, j: (i, 0))
    pm = pl.BlockSpec((tm, tm), lambda i, j: (0, 0))
    outs = pl.pallas_call(
        body, grid=(s // tm, 3), in_specs=[blk, tab, tab, tab] + [pm] * nd,
        out_specs=[blk] + [pl.BlockSpec((d, tm // d, wb), lambda i, j: (0, i, j)) for d in DILATIONS[1:]],
        out_shape=[SDS((s, 3 * wb), BF16)] + [SDS((d, s // d, 3 * wb), BF16) for d in DILATIONS[1:]],
        name=name, compiler_params=_cp("parallel", "parallel"))(proj, *tabs, *[perms[d][0] for d in DILATIONS[1:]])
    return [o.reshape(s, 3 * wb) for o in outs]


def _sum3_rope(ds_, perms, tabs, name, slab=None):
    s, w = ds_[0].shape
    tm = PERM_TILE
    half = ROPE_DIM // 2
    nd = len(DILATIONS) - 1

    def body(*refs):
        x_refs, perm_refs = refs[:nd + 1], refs[nd + 1:2 * nd + 1]
        tab_refs = refs[2 * nd + 1:2 * nd + 4]
        tot = x_refs[0][...].astype(F32)
        for d, x_ref, perm_ref in zip(DILATIONS[1:], x_refs[1:], perm_refs):
            tot = tot + _load_pattern(x_ref, perm_ref, d)
        for j in range(w // 128):
            x = tot[:, j * 128:(j + 1) * 128]
            if tabs is not None:
                x = x * tab_refs[0][...] + pltpu.roll(x, half, 1) * tab_refs[1][...] + pltpu.roll(x, 128 - half, 1) * tab_refs[2][...]
            refs[-1][:, j * 128:(j + 1) * 128] = x.astype(refs[-1].dtype)

    blk = pl.BlockSpec((tm, w), lambda i: (i, 0))
    tab = pl.BlockSpec((tm, 128), lambda i: (i, 0))
    pm = pl.BlockSpec((tm, tm), lambda i: (0, 0))
    extra = [] if tabs is None else list(tabs)
    out_shape, col0, more, more_specs, alias = _slab(s, w, BF16, slab, 2 * nd + 1 + len(extra))
    cb = col0 // w
    args = [ds_[0]] + [x.reshape(d, s // d, w) for d, x in zip(DILATIONS[1:], ds_[1:])] + [perms[d][1] for d in DILATIONS[1:]]
    return pl.pallas_call(
        body, grid=(s // tm,),
        in_specs=[blk] + [_pattern_spec(d, w) for d in DILATIONS[1:]] + [pm] * nd + [tab] * len(extra) + more_specs,
        out_specs=pl.BlockSpec((tm, w), lambda i: (i, cb)), out_shape=out_shape, input_output_aliases=alias,
        name=name, compiler_params=_cp("parallel"))(*args, *extra, *more)


def _band_valid(t, nq, nk, qofs, kofs, seq_len):
    qpos = t * 128 + qofs + lax.broadcasted_iota(jnp.int32, (nq, nk), 0)
    kpos = t * 128 + kofs + lax.broadcasted_iota(jnp.int32, (nq, nk), 1)
    sh = int(math.log2(seq_len))
    same = lax.shift_right_arithmetic(qpos, sh) == lax.shift_right_arithmetic(kpos, sh)
    return same & (jnp.abs(kpos - qpos) <= BAND_HALF)


def _window(r0, r1, r2, cols=slice(None)):
    return jnp.concatenate([r0[128 - BAND_HALF:128, cols], r1[:, cols], r2[0:BAND_HALF, cols]], axis=0)


WIN = 128 + 2 * BAND_HALF


def _first_head_lanes():
    return lax.broadcasted_iota(jnp.int32, (1, 2 * HEAD_DIM), 1) < HEAD_DIM


def _only(x, keep):
    return jnp.where(keep, x, jnp.zeros((), x.dtype))


def _win_specs(width, col, nt):
    return [pl.BlockSpec((128, width), lambda t: (jnp.maximum(t - 1, 0), col)),
            pl.BlockSpec((128, width), lambda t: (t, col)),
            pl.BlockSpec((128, width), lambda t: (jnp.minimum(t + 1, nt - 1), col))]


def _attn_fwd(qk, v_src, v_col, seq_len, name):
    s = qk.shape[0]
    nt = s // 128
    dm = D_MODEL

    def body(q_ref, k0, k1, k2, v0, v1, v2, o_ref, lse_ref):
        t = pl.program_id(0)
        valid = _band_valid(t, 128, WIN, 0, -BAND_HALF, seq_len)
        first = _first_head_lanes()
        outs, lses = [], []
        for pr in range(N_HEADS // 2):
            ps = slice(pr * 128, (pr + 1) * 128)
            qp, kp, vp = q_ref[:, ps], _window(k0, k1, k2, ps), _window(v0, v1, v2, ps)
            halves = []
            for keep in (first, ~first):
                sc = lax.dot_general(_only(qp, keep), kp, NT, preferred_element_type=F32)
                sc = jnp.where(valid, sc, NEG)
                m = jnp.max(sc, axis=1, keepdims=True)
                e = jnp.exp(sc - m)
                den = jnp.sum(e, axis=1, keepdims=True)
                halves.append(jnp.dot(e.astype(BF16), vp, preferred_element_type=F32) / den)
                lses.append(m + jnp.log(den))
            outs.append(jnp.where(first, halves[0], halves[1]))
        o_ref[...] = jnp.concatenate(outs, axis=1).astype(o_ref.dtype)
        lse_ref[...] = jnp.concatenate(lses, axis=1)

    in_specs = [pl.BlockSpec((128, dm), lambda t: (t, 0))] + _win_specs(dm, 1, nt) + _win_specs(dm, v_col, nt)
    return pl.pallas_call(
        body, grid=(nt,), in_specs=in_specs,
        out_specs=[pl.BlockSpec((128, dm), lambda t: (t, 0)), pl.BlockSpec((128, N_HEADS), lambda t: (t, 0))],
        out_shape=[SDS((s, dm), BF16), SDS((s, N_HEADS), F32)], name=name, compiler_params=_cp("parallel"))(
            qk, qk, qk, qk, v_src, v_src, v_src)


def _attn_combine(os_, lses, perms, expand, name):
    s, dm = os_[0].shape
    tm = PERM_TILE
    nd = len(DILATIONS) - 1

    def body(o1, o2, o3, l1, l2, l3, p2, p3, e_ref, out_ref, out_t_ref, lt_ref):
        ls = [l1[...], l2[...], l3[...]]
        m = jnp.maximum(jnp.maximum(ls[0], ls[1]), ls[2])
        es = [jnp.exp(l - m) for l in ls]
        tot = es[0] + es[1] + es[2]
        lt_ref[...] = m + jnp.log(tot)
        ovs = [o1[...].astype(F32), _load_pattern(o2, p2, DILATIONS[1]), _load_pattern(o3, p3, DILATIONS[2])]
        acc = jnp.zeros((tm, dm), F32)
        for e, o in zip(es, ovs):
            acc = acc + _dot_01(e / tot, e_ref[...]) * o
        out = acc.astype(out_ref.dtype)
        out_ref[...] = out
        out_t_ref[...] = out.T

    row = pl.BlockSpec((tm, dm), lambda i: (i, 0))
    st = pl.BlockSpec((tm, N_HEADS), lambda i: (i, 0))
    pm = pl.BlockSpec((tm, tm), lambda i: (0, 0))
    args = [os_[0]] + [o.reshape(d, s // d, dm) for d, o in zip(DILATIONS[1:], os_[1:])]
    return pl.pallas_call(
        body, grid=(s // tm,),
        in_specs=[row] + [_pattern_spec(d, dm) for d in DILATIONS[1:]] + [st, st, st, pm, pm, pl.BlockSpec((N_HEADS, dm), lambda i: (0, 0))],
        out_specs=[row, pl.BlockSpec((dm, tm), lambda i: (0, i)), st],
        out_shape=[SDS((s, dm + D_INNER), BF16), SDS((dm + D_INNER, s), BF16), SDS((s, N_HEADS), F32)],
        name=name, compiler_params=_cp("parallel"))(*args, *lses, *[perms[d][1] for d in DILATIONS[1:]], expand)


def _attn_delta(dmix, attn, expand_t, perms, name):
    s, dm = attn.shape[0], D_MODEL
    tm = PERM_TILE
    nd = len(DILATIONS) - 1

    def body(d_ref, a_ref, e_ref, *rest):
        perm_refs, o_ref, op_refs = rest[:nd], rest[nd], rest[nd + 1:]
        dv = d_ref[...]
        prod = dv.astype(F32) * a_ref[...].astype(F32)
        o_ref[...] = _dot_01(prod, e_ref[...])
        for d, perm_ref, op_ref in zip(DILATIONS[1:], perm_refs, op_refs):
            _store_pattern(op_ref, dv, perm_ref, d)

    row = pl.BlockSpec((tm, dm), lambda i: (i, 0))
    pm = pl.BlockSpec((tm, tm), lambda i: (0, 0))
    outs = pl.pallas_call(
        body, grid=(s // tm,), in_specs=[row, row, pl.BlockSpec((dm, N_HEADS), lambda i: (0, 0))] + [pm] * nd,
        out_specs=[pl.BlockSpec((tm, N_HEADS), lambda i: (i, 0))] + [_pattern_spec(d, dm) for d in DILATIONS[1:]],
        out_shape=[SDS((s, N_HEADS), F32)] + [SDS((d, s // d, dm), BF16) for d in DILATIONS[1:]],
        name=name, compiler_params=_cp("parallel"))(dmix, attn, expand_t, *[perms[d][0] for d in DILATIONS[1:]])
    return outs[0], [o.reshape(s, dm) for o in outs[1:]]


def _attn_bwd_dq(qk, v_src, v_col, do_src, lse, delta, seq_len, name):
    s = qk.shape[0]
    nt = s // 128
    dm = D_MODEL

    def body(q_ref, k0, k1, k2, v0, v1, v2, do_ref, lse_ref, dl_ref, dq_ref):
        t = pl.program_id(0)
        valid = _band_valid(t, 128, WIN, 0, -BAND_HALF, seq_len)
        lse_v, dl_v = lse_ref[...], dl_ref[...]
        first = _first_head_lanes()
        outs = []
        for pr in range(N_HEADS // 2):
            ps = slice(pr * 128, (pr + 1) * 128)
            qp, dop, kp, vp = q_ref[:, ps], do_ref[:, ps], _window(k0, k1, k2, ps), _window(v0, v1, v2, ps)
            halves = []
            for i, keep in enumerate((first, ~first)):
                h = 2 * pr + i
                sc = lax.dot_general(_only(qp, keep), kp, NT, preferred_element_type=F32)
                p = jnp.exp(jnp.where(valid, sc - lse_v[:, h:h + 1], NEG))
                dp = lax.dot_general(_only(dop, keep), vp, NT, preferred_element_type=F32)
                ds = p * (dp - dl_v[:, h:h + 1])
                halves.append(jnp.dot(ds.astype(BF16), kp, preferred_element_type=F32))
            outs.append(jnp.where(first, halves[0], halves[1]) * (HEAD_DIM ** -0.5))
        dq_ref[...] = jnp.concatenate(outs, axis=1).astype(dq_ref.dtype)

    row = pl.BlockSpec((128, dm), lambda t: (t, 0))
    st = pl.BlockSpec((128, N_HEADS), lambda t: (t, 0))
    in_specs = [row] + _win_specs(dm, 1, nt) + _win_specs(dm, v_col, nt) + [row, st, st]
    return pl.pallas_call(
        body, grid=(nt,), in_specs=in_specs, out_specs=row, out_shape=SDS((s, dm), BF16),
        name=name, compiler_params=_cp("parallel"))(qk, qk, qk, qk, v_src, v_src, v_src, do_src, lse, delta)


def _attn_bwd_dkv(qk, v_src, v_col, do_src, lse_t, delta_t, seq_len, name):
    s = qk.shape[0]
    nt = s // 128
    dm = D_MODEL

    def lane_window(r0, r1, r2):
        return jnp.concatenate([r0[:, 128 - BAND_HALF:128], r1[...], r2[:, 0:BAND_HALF]], axis=1)

    def body(k_ref, v_ref, q0, q1, q2, d0, d1, d2, l0, l1, l2, e0, e1, e2, dk_ref, dv_ref):
        t = pl.program_id(0)
        valid = _band_valid(t, 128, WIN, 0, -BAND_HALF, seq_len)
        lse_v = lane_window(l0, l1, l2)
        dl_v = lane_window(e0, e1, e2)
        first = _first_head_lanes()
        dks, dvs = [], []
        for pr in range(N_HEADS // 2):
            ps = slice(pr * 128, (pr + 1) * 128)
            kp, vp, qp, dop = k_ref[:, ps], v_ref[:, ps], _window(q0, q1, q2, ps), _window(d0, d1, d2, ps)
            dk_h, dv_h = [], []
            for i, keep in enumerate((first, ~first)):
                h = 2 * pr + i
                sc = lax.dot_general(_only(kp, keep), qp, NT, preferred_element_type=F32)
                p = jnp.exp(jnp.where(valid, sc - lse_v[h:h + 1, :], NEG))
                dv_h.append(jnp.dot(p.astype(BF16), dop, preferred_element_type=F32))
                dp = lax.dot_general(_only(vp, keep), dop, NT, preferred_element_type=F32)
                ds = p * (dp - dl_v[h:h + 1, :])
                dk_h.append(jnp.dot(ds.astype(BF16), qp, preferred_element_type=F32))
            dks.append(jnp.where(first, dk_h[0], dk_h[1]))
            dvs.append(jnp.where(first, dv_h[0], dv_h[1]))
        dk_ref[...] = jnp.concatenate(dks, axis=1).astype(dk_ref.dtype)
        dv_ref[...] = jnp.concatenate(dvs, axis=1).astype(dv_ref.dtype)

    row = pl.BlockSpec((128, dm), lambda t: (t, 0))
    stat = [pl.BlockSpec((N_HEADS, 128), lambda t: (0, jnp.maximum(t - 1, 0))), pl.BlockSpec((N_HEADS, 128), lambda t: (0, t)),
            pl.BlockSpec((N_HEADS, 128), lambda t: (0, jnp.minimum(t + 1, nt - 1)))]
    in_specs = ([pl.BlockSpec((128, dm), lambda t: (t, 1)), pl.BlockSpec((128, dm), lambda t: (t, v_col))]
                + _win_specs(dm, 0, nt) + _win_specs(dm, 0, nt) + stat + stat)
    return pl.pallas_call(
        body, grid=(nt,), in_specs=in_specs, out_specs=[row, row], out_shape=[SDS((s, dm), BF16), SDS((s, dm), BF16)],
        name=name, compiler_params=_cp("parallel"))(qk, v_src, qk, qk, qk, do_src, do_src, do_src, lse_t, lse_t, lse_t, delta_t, delta_t, delta_t)


CONV_COLS = (1024, 1408, 512, 256)


def _halo_specs(tm, tc, col0, nrow_blocks):
    r = tm // 16
    return [pl.BlockSpec((16, tc), lambda i, j: (jnp.maximum(i * r - 1, 0), col0 + j)),
            pl.BlockSpec((16, tc), lambda i, j: (jnp.minimum((i + 1) * r, nrow_blocks * r - 1), col0 + j))]


def _shifted(x_ref, hp_ref, hn_ref, i, last):
    x = x_ref[...].astype(F32)
    tm = x.shape[0]
    rows = lax.broadcasted_iota(jnp.int32, x.shape, 0)
    prev_row = jnp.where(i > 0, hp_ref[15:16, :].astype(F32), 0.0)
    next_row = jnp.where(i < last, hn_ref[0:1, :].astype(F32), 0.0)
    xp = jnp.where(rows == 0, prev_row, pltpu.roll(x, 1, 0))
    xn = jnp.where(rows == tm - 1, next_row, pltpu.roll(x, tm - 1, 0))
    return xp, x, xn


def _conv(x_src, col0, width, w3, bias, act, name, out_dtype=BF16, slab=None, transpose=False, wcol0=0):
    s = x_src.shape[0]
    tm = _pick(s, (256, 128))
    tc = _pick(width, CONV_COLS)
    nb = s // tm
    c0 = col0 // tc
    wc0 = wcol0 // tc

    def body(*refs):
        x_ref, hp_ref, hn_ref, w_ref, b_ref = refs[:5]
        o_ref = refs[-1]
        i = pl.program_id(0)
        xp, x, xn = _shifted(x_ref, hp_ref, hn_ref, i, nb - 1)
        w = w_ref[...]
        if transpose:
            y = w[2:3, :] * xp + w[1:2, :] * x + w[0:1, :] * xn
        else:
            y = w[0:1, :] * xp + w[1:2, :] * x + w[2:3, :] * xn + b_ref[...]
        if act:
            y = y * _sigmoid(y)
        o_ref[...] = y.astype(o_ref.dtype)

    in_specs = ([pl.BlockSpec((tm, tc), lambda i, j: (i, c0 + j))] + _halo_specs(tm, tc, c0, nb)
                + [pl.BlockSpec((3, tc), lambda i, j: (0, wc0 + j)), pl.BlockSpec((1, tc), lambda i, j: (0, wc0 + j))])
    out_shape, ocol, more, more_specs, alias = _slab(s, width, out_dtype, slab, 5)
    ob = ocol // tc
    return pl.pallas_call(
        body, grid=(nb, width // tc), in_specs=in_specs + more_specs, out_specs=pl.BlockSpec((tm, tc), lambda i, j: (i, ob + j)),
        out_shape=out_shape, input_output_aliases=alias, name=name, compiler_params=_cp("parallel", "parallel"))(
            x_src, x_src, x_src, w3, bias, *more)


def _conv_silu_bwd(x_src, col0, width, w3, bias, addends, add_widths, name):
    s = x_src.shape[0]
    tm = _pick(s, (256, 128))
    tc = _pick(width, CONV_COLS)
    nb = s // tm
    c0 = col0 // tc
    na = len(addends)

    def body(*refs):
        x_ref, hp_ref, hn_ref, w_ref, b_ref = refs[:5]
        a_refs = refs[5:5 + na]
        dp_ref, dw_ref, db_ref = refs[5 + na:]
        i, j = pl.program_id(1), pl.program_id(0)
        xp, x, xn = _shifted(x_ref, hp_ref, hn_ref, i, nb - 1)
        w = w_ref[...]
        pre = w[0:1, :] * xp + w[1:2, :] * x + w[2:3, :] * xn + b_ref[...]
        g = jnp.zeros_like(pre)
        for a_ref, aw in zip(a_refs, add_widths):
            av = a_ref[...].astype(F32)
            g = g + (av if aw == width else jnp.where(j < aw // tc, av, 0.0))
        sg = _sigmoid(pre)
        dpre = g * (sg * (1.0 + pre * (1.0 - sg)))
        dp_ref[...] = dpre.astype(dp_ref.dtype)

        @pl.when(i == 0)
        def _():
            dw_ref[...] = jnp.zeros_like(dw_ref)
            db_ref[...] = jnp.zeros_like(db_ref)

        dw_ref[...] += jnp.concatenate([jnp.sum(dpre * xp, axis=0, keepdims=True), jnp.sum(dpre * x, axis=0, keepdims=True),
                                        jnp.sum(dpre * xn, axis=0, keepdims=True)], axis=0)
        db_ref[...] += jnp.sum(dpre, axis=0, keepdims=True)

    r = tm // 16
    in_specs = [pl.BlockSpec((tm, tc), lambda j, i: (i, c0 + j)),
                pl.BlockSpec((16, tc), lambda j, i: (jnp.maximum(i * r - 1, 0), c0 + j)),
                pl.BlockSpec((16, tc), lambda j, i: (jnp.minimum((i + 1) * r, nb * r - 1), c0 + j)),
                pl.BlockSpec((3, tc), lambda j, i: (0, j)), pl.BlockSpec((1, tc), lambda j, i: (0, j))]
    for aw in add_widths:
        nblk = aw // tc
        in_specs.append(pl.BlockSpec((tm, tc), lambda j, i, nblk=nblk: (i, jnp.minimum(j, nblk - 1))))
    return pl.pallas_call(
        body, grid=(width // tc, nb), in_specs=in_specs,
        out_specs=[pl.BlockSpec((tm, tc), lambda j, i: (i, j)), pl.BlockSpec((3, tc), lambda j, i: (0, j)), pl.BlockSpec((1, tc), lambda j, i: (0, j))],
        out_shape=[SDS((s, width), BF16), SDS((3, width), F32), SDS((1, width), F32)],
        name=name, compiler_params=_cp("parallel", "arbitrary"))(x_src, x_src, x_src, w3, bias, *addends)


def _ffn_gate_fwd(u, w3, bias, name):
    s = u.shape[0]
    tm = _pick(s, (256, 128))
    tc = _pick(D_FF, CONV_COLS)
    nb = s // tm
    nj = D_FF // tc

    def body(g_ref, gp, gn, u_ref, up, un, wg_ref, wu_ref, bg_ref, bu_ref, o_ref, ot_ref):
        i = pl.program_id(0)
        outs = []
        for (x_ref, hp, hn, w_ref, b_ref) in ((g_ref, gp, gn, wg_ref, bg_ref), (u_ref, up, un, wu_ref, bu_ref)):
            xp, x, xn = _shifted(x_ref, hp, hn, i, nb - 1)
            w = w_ref[...]
            outs.append(w[0:1, :] * xp + w[1:2, :] * x + w[2:3, :] * xn + b_ref[...])
        gate, upv = outs
        out = (gate * _sigmoid(gate) * upv).astype(o_ref.dtype)
        o_ref[...] = out
        ot_ref[...] = out.T

    def xspecs(c0):
        return [pl.BlockSpec((tm, tc), lambda i, j: (i, c0 + j))] + _halo_specs(tm, tc, c0, nb)

    in_specs = (xspecs(0) + xspecs(nj)
                + [pl.BlockSpec((3, tc), lambda i, j: (0, j)), pl.BlockSpec((3, tc), lambda i, j: (0, nj + j)),
                   pl.BlockSpec((1, tc), lambda i, j: (0, j)), pl.BlockSpec((1, tc), lambda i, j: (0, nj + j))])
    return pl.pallas_call(
        body, grid=(nb, nj), in_specs=in_specs,
        out_specs=[pl.BlockSpec((tm, tc), lambda i, j: (i, j)), pl.BlockSpec((tc, tm), lambda i, j: (j, i))],
        out_shape=[SDS((s, D_FF), BF16), SDS((D_FF, s), BF16)], name=name, compiler_params=_cp("parallel", "parallel"))(
            u, u, u, u, u, u, w3, w3, bias, bias)


def _ffn_gate_bwd(u, w3, bias, dact, name):
    s = u.shape[0]
    tm = _pick(s, (256, 128))
    tc = _pick(D_FF, CONV_COLS)
    nb = s // tm
    nj = D_FF // tc

    def body(g_ref, gp, gn, u_ref, up, un, wg_ref, wu_ref, bg_ref, bu_ref, da_ref, dg_ref, du_ref, dwg_ref, dwu_ref, dbg_ref, dbu_ref):
        i = pl.program_id(1)
        sh, pre = [], []
        for (x_ref, hp, hn, w_ref, b_ref) in ((g_ref, gp, gn, wg_ref, bg_ref), (u_ref, up, un, wu_ref, bu_ref)):
            xs3 = _shifted(x_ref, hp, hn, i, nb - 1)
            w = w_ref[...]
            sh.append(xs3)
            pre.append(w[0:1, :] * xs3[0] + w[1:2, :] * xs3[1] + w[2:3, :] * xs3[2] + b_ref[...])
        gate, upv = pre
        da = da_ref[...].astype(F32)
        sg = _sigmoid(gate)
        dgate = da * upv * (sg * (1.0 + gate * (1.0 - sg)))
        dup = da * gate * sg
        dg_ref[...] = dgate.astype(dg_ref.dtype)
        du_ref[...] = dup.astype(du_ref.dtype)

        @pl.when(i == 0)
        def _():
            for r in (dwg_ref, dwu_ref, dbg_ref, dbu_ref):
                r[...] = jnp.zeros_like(r)

        for d, xs3, dw_ref, db_ref in ((dgate, sh[0], dwg_ref, dbg_ref), (dup, sh[1], dwu_ref, dbu_ref)):
            dw_ref[...] += jnp.concatenate([jnp.sum(d * xs3[0], axis=0, keepdims=True), jnp.sum(d * xs3[1], axis=0, keepdims=True),
                                            jnp.sum(d * xs3[2], axis=0, keepdims=True)], axis=0)
            db_ref[...] += jnp.sum(d, axis=0, keepdims=True)

    r = tm // 16

    def xspecs(c0):
        return [pl.BlockSpec((tm, tc), lambda j, i: (i, c0 + j)),
                pl.BlockSpec((16, tc), lambda j, i: (jnp.maximum(i * r - 1, 0), c0 + j)),
                pl.BlockSpec((16, tc), lambda j, i: (jnp.minimum((i + 1) * r, nb * r - 1), c0 + j))]

    in_specs = (xspecs(0) + xspecs(nj)
                + [pl.BlockSpec((3, tc), lambda j, i: (0, j)), pl.BlockSpec((3, tc), lambda j, i: (0, nj + j)),
                   pl.BlockSpec((1, tc), lambda j, i: (0, j)), pl.BlockSpec((1, tc), lambda j, i: (0, nj + j)),
                   pl.BlockSpec((tm, tc), lambda j, i: (i, j))])
    blk = pl.BlockSpec((tm, tc), lambda j, i: (i, j))
    w_o = pl.BlockSpec((3, tc), lambda j, i: (0, j))
    b_o = pl.BlockSpec((1, tc), lambda j, i: (0, j))
    return pl.pallas_call(
        body, grid=(nj, nb), in_specs=in_specs, out_specs=[blk, blk, w_o, w_o, b_o, b_o],
        out_shape=[SDS((s, D_FF), BF16), SDS((s, D_FF), BF16), SDS((3, D_FF), F32), SDS((3, D_FF), F32), SDS((1, D_FF), F32), SDS((1, D_FF), F32)],
        name=name, compiler_params=_cp("parallel", "arbitrary"))(u, u, u, u, u, u, w3, w3, bias, bias, dact)


def _exchange_sems(n):
    return [pltpu.SemaphoreType.DMA((7 * n,)), pltpu.SemaphoreType.DMA((7 * n,)), pltpu.SemaphoreType.DMA((n,))]


def _exchange_copies(srcs, outs, send_sems, recv_sems, local_sems):
    x, y, c = lax.axis_index("x"), lax.axis_index("y"), lax.axis_index("c")
    me = 4 * x + 2 * y + c
    locals_ = [pltpu.make_async_copy(srcs[a].at[me], outs[a].at[me], local_sems.at[a]) for a in range(len(srcs))]
    sends, recvs = [], []
    for k in range(1, N_DEV):
        px, py, pc = x ^ ((k >> 2) & 1), y ^ ((k >> 1) & 1), c ^ (k & 1)
        peer = 4 * px + 2 * py + pc
        for a in range(len(srcs)):
            sems = dict(send_sem=send_sems.at[a * 7 + k - 1], recv_sem=recv_sems.at[a * 7 + k - 1], device_id_type=MESH)
            sends.append(pltpu.make_async_remote_copy(src_ref=srcs[a].at[peer], dst_ref=outs[a].at[me], device_id=(px, py, pc), **sems))
            recvs.append(pltpu.make_async_remote_copy(src_ref=srcs[a].at[peer], dst_ref=outs[a].at[peer], device_id=(x, y, c), **sems))
    return locals_, sends, recvs


def _exchange_start(srcs, outs, send_sems, recv_sems, local_sems):
    locals_, sends, _ = _exchange_copies(srcs, outs, send_sems, recv_sems, local_sems)
    for cp in locals_ + sends:
        cp.start()


def _exchange_wait(srcs, outs, send_sems, recv_sems, local_sems):
    locals_, sends, recvs = _exchange_copies(srcs, outs, send_sems, recv_sems, local_sems)
    for cp in recvs:
        cp.wait_recv()
    for cp in sends:
        cp.wait_send()
    for cp in locals_:
        cp.wait()


def _gather_copies(srcs, outs, send_sems, recv_sems, local_sems):
    x, y, c = lax.axis_index("x"), lax.axis_index("y"), lax.axis_index("c")
    me, sibling = (x, y, c), (x, y, 1 - c)
    chips = [(1 - x, y), (x, 1 - y), (1 - x, 1 - y)]

    def copy(a, k, block, to, src=None):
        dst = outs[a].at[4 * block[0] + 2 * block[1] + block[2]]
        return pltpu.make_async_remote_copy(
            src_ref=dst if src is None else src, dst_ref=dst,
            send_sem=send_sems.at[a * 7 + k], recv_sem=recv_sems.at[a * 7 + k], device_id=to, device_id_type=MESH)

    n = len(srcs)
    locals_ = [pltpu.make_async_copy(srcs[a], outs[a].at[4 * x + 2 * y + c], local_sems.at[a]) for a in range(n)]
    own = [copy(a, 0, me, sibling, src=srcs[a]) for a in range(n)]
    own += [copy(a, 1 + j, me, (*chip, c), src=srcs[a]) for a in range(n) for j, chip in enumerate(chips)]
    landed_ici = [copy(a, 1 + j, (*chip, c), me) for j, chip in enumerate(chips) for a in range(n)]
    passed = [copy(a, 4 + j, (*chip, c), sibling) for j, chip in enumerate(chips) for a in range(n)]
    landed_d2d = [copy(a, 0, sibling, me) for a in range(n)]
    landed_d2d += [copy(a, 4 + j, (*chip, 1 - c), me) for a in range(n) for j, chip in enumerate(chips)]
    return locals_, own, landed_ici, passed, landed_d2d


def _gather_start(*refs):
    locals_, own, _, _, _ = _gather_copies(*refs)
    for cp in locals_ + own:
        cp.start()


def _gather_forward(*refs):
    _, _, landed_ici, passed, _ = _gather_copies(*refs)
    for arrived, onward in zip(landed_ici, passed):
        arrived.wait_recv()
        onward.start()


def _gather_finish(*refs):
    locals_, own, _, passed, landed_d2d = _gather_copies(*refs)
    for cp in landed_d2d:
        cp.wait_recv()
    for cp in own + passed:
        cp.wait_send()
    for cp in locals_:
        cp.wait()


def _ssd_common(dt_ref, dtt_ref, al_r, al_c, bi_r, bi_c, off, rev):
    li = lax.broadcasted_iota(jnp.int32, (CHUNK, CHUNK), 0)
    si = lax.broadcasted_iota(jnp.int32, (CHUNK, CHUNK), 1)
    mask = (li <= si) if rev else (li >= si)
    mask_t = (li >= si) if rev else (li <= si)
    a_r = -jnp.exp(al_r[...])
    a_c = -jnp.exp(al_c[...])
    pre = dt_ref[:, off:off + N_HEADS] + bi_r[...]
    dt = _softplus(pre)
    cs = jnp.dot(mask.astype(F32), dt * a_r, precision=HIGH, preferred_element_type=F32)
    dt_t = _softplus(dtt_ref[off:off + N_HEADS, :] + bi_c[...])
    cs_t = jnp.dot(dt_t * a_c, mask_t.astype(F32), precision=HIGH, preferred_element_type=F32)
    tot = cs[0:1, :] if rev else cs[CHUNK - 1:CHUNK, :]
    return mask, mask_t, a_r, pre, dt, cs, cs_t, tot


def _ssd_fwd(xbc, dt_raw, dt_t, args_f, args_b, name, gather=()):
    s = xbc.shape[0]
    nc = s // CHUNK
    hp = D_INNER // N_HEADS
    hpg = N_HEADS // N_GROUPS
    gs = N_GROUPS * D_STATE
    ng = len(gather)
    fwd_step = (nc * 27) // 32

    def chunk(x_ref, b_ref, c_ref, dt_ref, dtt_ref, alr, alc, bir, bic, y_ref, st_ref, h_scr, off, rev):
        mask, _, _, _, dt, cs, cs_t, tot = _ssd_common(dt_ref, dtt_ref, alr, alc, bir, bic, off, rev)
        first = _first_head_lanes()
        ys = []
        for g in range(N_GROUPS):
            bg = b_ref[:, g * D_STATE:(g + 1) * D_STATE]
            cg = c_ref[:, g * D_STATE:(g + 1) * D_STATE]
            gm = lax.dot_general(cg, bg, NT, preferred_element_type=F32)
            hcat = h_scr[g]
            st_ref[0, g] = hcat
            ch = lax.dot_general(cg, hcat.astype(BF16), NT, preferred_element_type=F32)
            xdd = []
            for pr in range(hpg // 2):
                h0 = g * hpg + 2 * pr
                lanes = slice(h0 * hp, (h0 + 2) * hp)
                cols = [cs[:, h:h + 1] for h in (h0, h0 + 1)]
                pair = lambda a, b: jnp.where(first, a, b)
                xdf = x_ref[:, lanes].astype(F32) * pair(dt[:, h0:h0 + 1], dt[:, h0 + 1:h0 + 2])
                xdb = xdf.astype(BF16)
                yh = []
                for i, h in enumerate((h0, h0 + 1)):
                    lm = jnp.exp(jnp.where(mask, cols[i] - cs_t[h:h + 1, :], NEG))
                    yh.append(jnp.dot((gm * lm).astype(BF16), xdb, preferred_element_type=F32))
                ecs = pair(jnp.exp(cols[0]), jnp.exp(cols[1]))
                ys.append(pair(yh[0], yh[1]) + ecs * ch[:, 2 * pr * hp:(2 * pr + 2) * hp])
                dec = pair(jnp.exp(tot[:, h0:h0 + 1] - cols[0]), jnp.exp(tot[:, h0 + 1:h0 + 2] - cols[1]))
                xdd.append((xdf * dec).astype(BF16))
            snew = lax.dot_general(jnp.concatenate(xdd, axis=1), bg, TN, preferred_element_type=F32)
            for r in range(hpg):
                rs = slice(r * hp, (r + 1) * hp)
                h_scr[g, rs, :] = jnp.exp(tot[:, g * hpg + r:g * hpg + r + 1]) * hcat[rs] + snew[rs]
        y_ref[...] = jnp.concatenate(ys, axis=1).astype(y_ref.dtype)

    def body(*refs):
        in_f, in_b = refs[0:9], refs[9:18]
        g_src = refs[18:18 + ng]
        out_f, out_b = refs[18 + ng:20 + ng], refs[20 + ng:22 + ng]
        g_dst = refs[22 + ng:22 + 2 * ng]
        hs_f, hs_b = refs[22 + 2 * ng], refs[23 + 2 * ng]
        g_sems = refs[24 + 2 * ng:]
        step = pl.program_id(0)

        @pl.when(step == 0)
        def _():
            hs_f[...] = jnp.zeros_like(hs_f)
            hs_b[...] = jnp.zeros_like(hs_b)
            if ng:
                _gather_start(g_src, g_dst, *g_sems)

        chunk(*in_f, *out_f, hs_f, 0, False)
        chunk(*in_b, *out_b, hs_b, N_HEADS, True)

        if ng:
            @pl.when(step == fwd_step)
            def _():
                _gather_forward(g_src, g_dst, *g_sems)

            @pl.when(step == nc - 1)
            def _():
                _gather_finish(g_src, g_dst, *g_sems)

    small = lambda shape: pl.BlockSpec(shape, lambda c: (0, 0))

    def specs(cm):
        ins = [pl.BlockSpec((CHUNK, D_INNER), lambda c: (cm(c), 0)),
               pl.BlockSpec((CHUNK, gs), lambda c: (cm(c), D_INNER // gs)),
               pl.BlockSpec((CHUNK, gs), lambda c: (cm(c), D_INNER // gs + 1)),
               pl.BlockSpec((CHUNK, 128), lambda c: (cm(c), 0)),
               pl.BlockSpec((2 * N_HEADS, CHUNK), lambda c: (0, cm(c))),
               small((1, N_HEADS)), small((N_HEADS, 1)), small((1, N_HEADS)), small((N_HEADS, 1))]
        outs = [pl.BlockSpec((CHUNK, D_INNER), lambda c: (cm(c), 0)),
                pl.BlockSpec((1, N_GROUPS, hpg * hp, D_STATE), lambda c: (cm(c), 0, 0, 0))]
        return ins, outs

    ins_f, outs_f = specs(lambda c: c)
    ins_b, outs_b = specs(lambda c: nc - 1 - c)
    any_spec = pl.BlockSpec(memory_space=pl.ANY)
    one_dir = [SDS((s, D_INNER), BF16), SDS((nc, N_GROUPS, hpg * hp, D_STATE), F32)]
    state = pltpu.VMEM((N_GROUPS, hpg * hp, D_STATE), F32)
    return pl.pallas_call(
        body, grid=(nc,), in_specs=ins_f + ins_b + [any_spec] * ng, out_specs=outs_f + outs_b + [any_spec] * ng,
        out_shape=one_dir + one_dir + [SDS((N_DEV,) + g.shape, g.dtype) for g in gather],
        scratch_shapes=[state, state] + (_exchange_sems(ng) if ng else []), name=name, compiler_params=_cp("arbitrary"))(
            xbc, xbc, xbc, dt_raw, dt_t, *args_f, xbc, xbc, xbc, dt_raw, dt_t, *args_b, *gather)


def _ssd_bwd(xbc, dt_raw, dt_t, al_r, al_c, bi_r, bi_c, states, dy, off, rev, name, exchange=()):
    s = xbc.shape[0]
    nc = s // CHUNK
    hp = D_INNER // N_HEADS
    gs = N_GROUPS * D_STATE
    hpg = N_HEADS // N_GROUPS
    cm = (lambda c: c) if rev else (lambda c: nc - 1 - c)
    nx = len(exchange)

    def body(*refs):
        x_ref, b_ref, c_ref, dt_ref, dtt_ref, alr, alc, bir, bic, st_ref, dy_ref = refs[:11]
        xch_src = refs[11:11 + nx]
        dx_ref, ddt_ref, dal_ref, dbi_ref = refs[11 + nx:15 + nx]
        xch_dst = refs[15 + nx:15 + 2 * nx]
        dh_scr = refs[15 + 2 * nx]
        xch_sems = refs[16 + 2 * nx:]

        @pl.when(pl.program_id(0) == 0)
        def _():
            dh_scr[...] = jnp.zeros_like(dh_scr)
            dal_ref[...] = jnp.zeros_like(dal_ref)
            dbi_ref[...] = jnp.zeros_like(dbi_ref)
            if nx:
                _exchange_start(xch_src, xch_dst, *xch_sems)

        if nx:
            @pl.when(pl.program_id(0) == nc - 1)
            def _():
                _exchange_wait(xch_src, xch_dst, *xch_sems)

        mask, mask_t, a_r, pre, dt, cs, cs_t, tot = _ssd_common(dt_ref, dtt_ref, alr, alc, bir, bic, off, rev)
        rows = lax.broadcasted_iota(jnp.int32, (CHUNK, 1), 0)
        end_row = (rows == 0) if rev else (rows == CHUNK - 1)
        lane_h = lax.broadcasted_iota(jnp.int32, (1, N_HEADS), 1)
        sub_h = lax.broadcasted_iota(jnp.int32, (N_HEADS, 1), 0)
        first = _first_head_lanes()
        dcs_all = jnp.zeros((CHUNK, N_HEADS), F32)
        colw_all = jnp.zeros((N_HEADS, CHUNK), F32)
        dxsum_all = jnp.zeros((CHUNK, N_HEADS), F32)
        dxs, dbs, dcs_out = [], [], []
        for g in range(N_GROUPS):
            bg = b_ref[:, g * D_STATE:(g + 1) * D_STATE]
            cg = c_ref[:, g * D_STATE:(g + 1) * D_STATE]
            gm = lax.dot_general(cg, bg, NT, preferred_element_type=F32)
            gm_t = lax.dot_general(bg, cg, NT, preferred_element_type=F32)
            hcat = st_ref[0, g]
            dhcat = dh_scr[g]
            hb, dhb = hcat.astype(BF16), dhcat.astype(BF16)
            ch = lax.dot_general(cg, hb, NT, preferred_element_type=F32)
            z = lax.dot_general(bg, dhb, NT, preferred_element_type=F32)
            dg_sum = jnp.zeros((CHUNK, CHUNK), F32)
            dchs, xdds, t_hs = [], [], []
            for pr in range(hpg // 2):
                h0 = g * hpg + 2 * pr
                lanes = slice(h0 * hp, (h0 + 2) * hp)
                ps = slice(2 * pr * hp, (2 * pr + 2) * hp)
                pair = lambda a, b: jnp.where(first, a, b)
                cols = [cs[:, h:h + 1] for h in (h0, h0 + 1)]
                tots = [tot[:, h:h + 1] for h in (h0, h0 + 1)]
                xh = x_ref[:, lanes].astype(F32)
                dtc = pair(dt[:, h0:h0 + 1], dt[:, h0 + 1:h0 + 2])
                xdf = xh * dtc
                xd = xdf.astype(BF16)
                dyh = dy_ref[:, lanes]
                dyb = dyh.astype(BF16)
                ecs = pair(jnp.exp(cols[0]), jnp.exp(cols[1]))
                dec = pair(jnp.exp(tots[0] - cols[0]), jnp.exp(tots[1] - cols[1]))
                yoff_t = dyh * (ecs * ch[:, ps])
                dchs.append((ecs * dyh).astype(BF16))
                xdd = xdf * dec
                ddec_t = xdd * z[:, ps]
                row_terms = yoff_t - ddec_t
                xdds.append(xdd.astype(BF16))
                dxd_h = []
                for i, (h, keep) in enumerate(((h0, first), (h0 + 1, ~first))):
                    rs = slice((2 * pr + i) * hp, (2 * pr + i + 1) * hp)
                    lm = jnp.exp(jnp.where(mask, cols[i] - cs_t[h:h + 1, :], NEG))
                    m = gm * lm
                    t_h = jnp.exp(tots[i])
                    lm_t = jnp.exp(jnp.where(mask_t, cs_t[h:h + 1, :] - cols[i], NEG))
                    dxd_h.append(jnp.dot((gm_t * lm_t).astype(BF16), dyb, preferred_element_type=F32))
                    dm = lax.dot_general(_only(dyb, keep), xd, NT, preferred_element_type=F32)
                    dg_sum = dg_sum + dm * lm
                    w = dm * m
                    ddec_tot = jnp.sum(jnp.sum(_only(ddec_t, keep), axis=0, keepdims=True), axis=1, keepdims=True)
                    dtot = jnp.sum(jnp.sum(dhcat[rs] * hcat[rs], axis=0, keepdims=True), axis=1, keepdims=True) * t_h + ddec_tot
                    t_hs.append(t_h)
                    dcs = jnp.sum(_only(row_terms, keep) + w, axis=1, keepdims=True)
                    dcs = dcs + jnp.where(end_row, dtot, 0.0)
                    colw_all = colw_all + (sub_h == h).astype(F32) * jnp.sum(w, axis=0, keepdims=True)
                    dcs_all = dcs_all + dcs * (lane_h == h).astype(F32)
                dxd = pair(dxd_h[0], dxd_h[1]) + dec * z[:, ps]
                dxx = dxd * xh
                for i, (h, keep) in enumerate(((h0, first), (h0 + 1, ~first))):
                    dxsum_all = dxsum_all + jnp.sum(_only(dxx, keep), axis=1, keepdims=True) * (lane_h == h).astype(F32)
                dxs.append(dxd * dtc)
            dgs = dg_sum.astype(BF16)
            dchc = jnp.concatenate(dchs, axis=1)
            dcs_out.append(jnp.dot(dgs, bg, preferred_element_type=F32) + jnp.dot(dchc, hb, preferred_element_type=F32))
            dbs.append(lax.dot_general(dgs, cg, TN, preferred_element_type=F32)
                       + jnp.dot(jnp.concatenate(xdds, axis=1), dhb, preferred_element_type=F32))
            dh_in = lax.dot_general(dchc, cg, TN, preferred_element_type=F32)
            for r in range(hpg):
                rs = slice(r * hp, (r + 1) * hp)
                dh_scr[g, rs, :] = dh_in[rs] + t_hs[r] * dhcat[rs]
        dx_ref[...] = jnp.concatenate(dxs + dbs + dcs_out, axis=1).astype(dx_ref.dtype)
        mt = mask_t.astype(F32)
        da = (jnp.dot(mt, dcs_all, precision=HIGH, preferred_element_type=F32)
              - lax.dot_general(mt, colw_all, NT, precision=HIGH, preferred_element_type=F32))
        dal_ref[...] += jnp.sum(da * dt, axis=0, keepdims=True) * a_r
        ddt_raw = (da * a_r + dxsum_all) * _sigmoid(pre)
        ddt_ref[...] = ddt_raw
        dbi_ref[...] += jnp.sum(ddt_raw, axis=0, keepdims=True)

    small = lambda shape: pl.BlockSpec(shape, lambda c: (0, 0))
    in_specs = [pl.BlockSpec((CHUNK, D_INNER), lambda c: (cm(c), 0)),
                pl.BlockSpec((CHUNK, gs), lambda c: (cm(c), D_INNER // gs)),
                pl.BlockSpec((CHUNK, gs), lambda c: (cm(c), D_INNER // gs + 1)),
                pl.BlockSpec((CHUNK, 128), lambda c: (cm(c), 0)),
                pl.BlockSpec((2 * N_HEADS, CHUNK), lambda c: (0, cm(c))),
                small((1, N_HEADS)), small((N_HEADS, 1)), small((1, N_HEADS)), small((N_HEADS, 1)),
                pl.BlockSpec((1, N_GROUPS, hpg * hp, D_STATE), lambda c: (cm(c), 0, 0, 0)),
                pl.BlockSpec((CHUNK, D_INNER), lambda c: (cm(c), 0))]
    any_spec = pl.BlockSpec(memory_space=pl.ANY)
    return pl.pallas_call(
        body, grid=(nc,), in_specs=in_specs + [any_spec] * nx,
        out_specs=[pl.BlockSpec((CHUNK, XBC), lambda c: (cm(c), 0)), pl.BlockSpec((CHUNK, N_HEADS), lambda c: (cm(c), 0)),
                   small((1, N_HEADS)), small((1, N_HEADS))] + [any_spec] * nx,
        out_shape=[SDS((s, XBC), BF16), SDS((s, N_HEADS), F32), SDS((1, N_HEADS), F32), SDS((1, N_HEADS), F32)]
        + [SDS(a.shape, a.dtype) for a in exchange],
        scratch_shapes=[pltpu.VMEM((N_GROUPS, hpg * hp, D_STATE), F32)] + (_exchange_sems(nx) if nx else []),
        name=name, compiler_params=_cp("arbitrary"))(xbc, xbc, xbc, dt_raw, dt_t, al_r, al_c, bi_r, bi_c, states, dy, *exchange)


def _gate_fwd(yf, yb, xbc, proj, dskip_x, norm_w, mix, mix_t, name):
    s = yf.shape[0]
    tm = _pick(s, (256, 128))
    gw = D_INNER // N_GROUPS
    zc = 3 * D_MODEL // D_INNER

    def body(yf_ref, yb_ref, x_ref, z_ref, d_ref, w_ref, _m, _mt, o_ref, ot_ref):
        y = yf_ref[...].astype(F32) + yb_ref[...].astype(F32) + d_ref[...] * x_ref[...].astype(F32)
        z = z_ref[...].astype(F32)
        gt = y * (z * _sigmoid(z))
        outs = []
        for g in range(N_GROUPS):
            gg = gt[:, g * gw:(g + 1) * gw]
            outs.append(gg * lax.rsqrt(jnp.mean(gg * gg, axis=-1, keepdims=True) + EPS))
        out = (jnp.concatenate(outs, axis=1) * w_ref[...]).astype(o_ref.dtype)
        o_ref[...] = out
        ot_ref[...] = out.T

    row = pl.BlockSpec((tm, D_INNER), lambda i: (i, 0))
    vec = pl.BlockSpec((1, D_INNER), lambda i: (0, 0))
    any_spec = pl.BlockSpec(memory_space=pl.ANY)
    cb = (mix.shape[1] - D_INNER) // D_INNER
    return pl.pallas_call(
        body, grid=(s // tm,), in_specs=[row, row, row, pl.BlockSpec((tm, D_INNER), lambda i: (i, zc)), vec, vec, any_spec, any_spec],
        out_specs=[pl.BlockSpec((tm, D_INNER), lambda i: (i, cb)), pl.BlockSpec((D_INNER, tm), lambda i: (cb, i))],
        out_shape=[SDS(mix.shape, BF16), SDS(mix_t.shape, BF16)], input_output_aliases={6: 0, 7: 1},
        name=name, compiler_params=_cp("parallel"))(yf, yb, xbc, proj, dskip_x, norm_w, mix, mix_t)


def _gate_bwd(yf, yb, xbc, proj, dskip_x, norm_w, dmix, dproj, name):
    s = yf.shape[0]
    tm = _pick(s, (256, 128))
    gw = D_INNER // N_GROUPS
    zc = 3 * D_MODEL // D_INNER

    def body(yf_ref, yb_ref, x_ref, z_ref, d_ref, w_ref, do_ref, _, dy_ref, dz_ref, dxs_ref, dw_ref, dd_ref):
        xf = x_ref[...].astype(F32)
        y = yf_ref[...].astype(F32) + yb_ref[...].astype(F32) + d_ref[...] * xf
        z = z_ref[...].astype(F32)
        sg = _sigmoid(z)
        sz = z * sg
        gt = y * sz
        do = do_ref[...].astype(F32)
        dgh = do * w_ref[...]
        ghs, dgts = [], []
        for g in range(N_GROUPS):
            gg = gt[:, g * gw:(g + 1) * gw]
            r = lax.rsqrt(jnp.mean(gg * gg, axis=-1, keepdims=True) + EPS)
            gh = gg * r
            dg = dgh[:, g * gw:(g + 1) * gw]
            ghs.append(gh)
            dgts.append(r * (dg - gh * jnp.mean(dg * gh, axis=-1, keepdims=True)))
        ghat = jnp.concatenate(ghs, axis=1)
        dgt = jnp.concatenate(dgts, axis=1)
        dy = dgt * sz
        dy_ref[...] = dy
        dz_ref[...] = (dgt * y * (sg * (1.0 + z * (1.0 - sg)))).astype(dz_ref.dtype)
        dxs_ref[...] = (dy * d_ref[...]).astype(dxs_ref.dtype)

        @pl.when(pl.program_id(0) == 0)
        def _():
            dw_ref[...] = jnp.zeros_like(dw_ref)
            dd_ref[...] = jnp.zeros_like(dd_ref)

        dw_ref[...] += jnp.sum(do * ghat, axis=0, keepdims=True)
        dd_ref[...] += jnp.sum(dy * xf, axis=0, keepdims=True)

    row = pl.BlockSpec((tm, D_INNER), lambda i: (i, 0))
    vec = pl.BlockSpec((1, D_INNER), lambda i: (0, 0))
    dz_shape, _, more, more_specs, alias = _slab(s, D_INNER, BF16, (dproj.shape[1], zc * D_INNER, dproj), 7, out_idx=1)
    return pl.pallas_call(
        body, grid=(s // tm,),
        in_specs=[row, row, row, pl.BlockSpec((tm, D_INNER), lambda i: (i, zc)), vec, vec, pl.BlockSpec((tm, D_INNER), lambda i: (i, 1))] + more_specs,
        out_specs=[row, pl.BlockSpec((tm, D_INNER), lambda i: (i, zc)), row, vec, vec],
        out_shape=[SDS((s, D_INNER), F32), dz_shape, SDS((s, D_INNER), BF16), SDS((1, D_INNER), F32), SDS((1, D_INNER), F32)],
        input_output_aliases=alias, name=name, compiler_params=_cp("arbitrary"))(yf, yb, xbc, proj, dskip_x, norm_w, dmix, *more)


def _adamw(parts, w, m, v, name, exchange=()):
    r, c = w.shape
    tr = _pick(r, (256, 352, 128))
    n = r // tr
    nx = len(exchange)

    def body(*refs):
        p_ref, w_ref, m_ref, v_ref = refs[:4]
        xch_src = refs[4:4 + nx]
        g_ref, d_ref, nm_ref, nv_ref = refs[4 + nx:8 + nx]
        xch_dst = refs[8 + nx:8 + 2 * nx]
        xch_sems = refs[8 + 2 * nx:]
        if nx:
            @pl.when(pl.program_id(0) == 0)
            def _():
                _exchange_start(xch_src, xch_dst, *xch_sems)

        g = p_ref[0].astype(F32)
        for i in range(1, N_DEV):
            g = g + p_ref[i].astype(F32)
        mn = B1 * m_ref[...] + (1.0 - B1) * g
        vn = B2 * v_ref[...] + (1.0 - B2) * (g * g)
        m_hat = mn / (1.0 - B1 ** STEP)
        v_hat = vn / (1.0 - B2 ** STEP)
        g_ref[...] = g
        d_ref[...] = -LR * (m_hat / (jnp.sqrt(v_hat) + AEPS) + WD * w_ref[...])
        nm_ref[...] = mn
        nv_ref[...] = vn

        if nx:
            @pl.when(pl.program_id(0) == n - 1)
            def _():
                _exchange_wait(xch_src, xch_dst, *xch_sems)

    blk = pl.BlockSpec((tr, c), lambda i: (i, 0))
    any_spec = pl.BlockSpec(memory_space=pl.ANY)
    return pl.pallas_call(
        body, grid=(n,), in_specs=[pl.BlockSpec((N_DEV, tr, c), lambda i: (0, i, 0)), blk, blk, blk] + [any_spec] * nx,
        out_specs=[blk, blk, blk, blk] + [any_spec] * nx,
        out_shape=[SDS((r, c), F32)] * 4 + [SDS(e.shape, e.dtype) for e in exchange],
        scratch_shapes=_exchange_sems(nx) if nx else [],
        name=name, compiler_params=_cp("arbitrary" if nx else "parallel"))(parts, w, m, v, *exchange)


def _sum_parts(parts, name):
    _, r, c = parts.shape

    def body(p_ref, o_ref):
        g = p_ref[0]
        for i in range(1, N_DEV):
            g = g + p_ref[i]
        o_ref[...] = g

    return pl.pallas_call(body, out_shape=SDS((r, c), F32), name=name)(parts)


def _adamw_small(gs, ws, ms, vs, name):
    n = len(gs)

    def body(*refs):
        g_refs, w_refs, m_refs, v_refs = refs[:n], refs[n:2 * n], refs[2 * n:3 * n], refs[3 * n:4 * n]
        d_refs, nm_refs, nv_refs = refs[4 * n:5 * n], refs[5 * n:6 * n], refs[6 * n:7 * n]
        for i in range(n):
            gv = g_refs[i][...]
            mn = B1 * m_refs[i][...] + (1.0 - B1) * gv
            vn = B2 * v_refs[i][...] + (1.0 - B2) * (gv * gv)
            m_hat = mn / (1.0 - B1 ** STEP)
            v_hat = vn / (1.0 - B2 ** STEP)
            d_refs[i][...] = -LR * (m_hat / (jnp.sqrt(v_hat) + AEPS) + WD * w_refs[i][...])
            nm_refs[i][...] = mn
            nv_refs[i][...] = vn

    outs = pl.pallas_call(body, out_shape=[SDS(g.shape, F32) for g in gs] * 3, name=name)(*gs, *ws, *ms, *vs)
    return outs[:n], outs[n:2 * n], outs[2 * n:]


def _my_index():
    return 4 * lax.axis_index("x") + 2 * lax.axis_index("y") + lax.axis_index("c")


def _to_pattern(t, d):
    if d == 1:
        return t
    s, w = t.shape
    return t.reshape(s // d, d, w).transpose(1, 0, 2).reshape(s, w)


def _from_pattern(t, d):
    if d == 1:
        return t
    s, w = t.shape
    return t.reshape(d, s // d, w).transpose(1, 0, 2).reshape(s, w)


def _pad_lanes(t, n):
    return jnp.pad(t, ((0, 0), (0, n - t.shape[1])))


def _to_shards(g, axis):
    r, c = g.shape
    if axis == 0:
        return g.reshape(N_DEV, r // N_DEV, c)
    return g.reshape(r, N_DEV, c // N_DEV).transpose(1, 0, 2)


def _local_step(x, target, p, first_shards=None, late_shards=(), early_exchange=True):
    s = x.shape[0]
    tabs_f, tabs_b = _rope_tables(s)
    expand = jnp.asarray(np.repeat(np.eye(N_HEADS, dtype=np.float32), HEAD_DIM, axis=1))
    al_r = {"f": p["a_log_f"], "b": p["a_log_b"]}
    bi_r = {"f": p["dt_bias_f"], "b": p["dt_bias_b"]}
    dskip_x = jnp.repeat(p["d_skip"], D_INNER // N_HEADS, axis=1)

    h1, h1t, *got = _rmsnorm_fwd(x, p["norm1_w"], "norm1_fwd", gather=first_shards[0] if first_shards else ())
    if first_shards:
        p = dict(p, **first_shards[1](got))
    w_main, w_dt = p["w_in"][:, :MAIN_W], _pad_lanes(p["w_in"][:, MAIN_W:], 128)
    ssm_w3, ffn_w3 = p["ssm_conv_w"].T, p["ffn_conv_w"].T
    proj = _matmul(h1, w_main, name="in_proj")
    dt_raw = _matmul(h1, w_dt, name="in_proj_dt", out_dtype=F32)
    dt_t = dt_raw[:, :2 * N_HEADS].T
    perms = {d: _perm_matrices(d) for d in DILATIONS[1:]}
    qkv = _rope_fwd(proj, tabs_f, perms, "rope_fwd")
    v_col = 2
    os_, lses = [], []
    for d, qkv_p in zip(DILATIONS, qkv):
        o_p, lse_p = _attn_fwd(qkv_p, qkv_p, v_col, s // d, f"attn_fwd_d{d}")
        os_.append(o_p)
        lses.append(_from_pattern(lse_p, d))
    mix, mix_t, lse_tot = _attn_combine(os_, lses, perms, expand, "attn_combine")

    xbc = _conv(proj, 3 * D_MODEL + D_INNER, XBC, ssm_w3, p["ssm_conv_b"], True, "ssm_conv_fwd")
    col = lambda r: r.reshape(N_HEADS, 1)
    ssd_args = {k: (al_r[k], col(al_r[k]), bi_r[k], col(bi_r[k])) for k in ("f", "b")}
    yf, st_f, yb, st_b, *got = _ssd_fwd(xbc, dt_raw, dt_t, ssd_args["f"], ssd_args["b"], "ssd_fwd", gather=late_shards)
    if late_shards:
        p = dict(p, w_out=got[0].reshape(2 * D_MODEL, D_MODEL), w_down=got[2].reshape(D_FF, D_MODEL),
                 w_up=got[1].transpose(1, 0, 2).reshape(D_MODEL, 2 * D_FF))
    mix, mix_t = _gate_fwd(yf, yb, xbc, proj, dskip_x, p["ssm_norm_w"], mix, mix_t, "ssm_gate_fwd")

    x2 =_matmul(mix, p["w_out"], name="out_proj", out_dtype=F32, residual=x)
    h2, h2t = _rmsnorm_fwd(x2, p["norm2_w"], "norm2_fwd")
    u = _matmul(h2, p["w_up"], name="ffn_up")
    act, act_t = _ffn_gate_fwd(u, ffn_w3, p["ffn_conv_b"], "ffn_gate_fwd")
    x3 = _matmul(act, p["w_down"], name="ffn_down", out_dtype=F32, residual=x2)

    dx3, dx3b, g_final, loss = _final_norm_loss(x3, p["final_norm_w"].reshape(1, D_MODEL), target, "final_norm_loss")
    g_w_down = _matmul(act_t, dx3b, name="dw_down")
    dact = _matmul(dx3b, p["w_down"], name="d_act", trans_b=True)
    dug, duu, dwg, dwu, dbg, dbu = _ffn_gate_bwd(u, ffn_w3, p["ffn_conv_b"], dact, "ffn_gate_bwd")
    du = _conv(dug, 0, D_FF, ffn_w3, p["ffn_conv_b"], False, "ffn_conv_bwd_gate", slab=(2 * D_FF, 0, None), transpose=True)
    du = _conv(duu, 0, D_FF, ffn_w3, p["ffn_conv_b"], False, "ffn_conv_bwd_up", slab=(2 * D_FF, D_FF, du), transpose=True, wcol0=D_FF)
    g_w_up = _matmul(h2t, du, name="dw_up")
    none_a, none_w = jnp.zeros((s, 128), BF16), jnp.zeros((D_MODEL, 128), BF16)
    dx2, dx2b, g_norm2 = _proj_norm_bwd(du, p["w_up"], none_a, none_w, x2, p["norm2_w"], dx3, "ffn_up_norm2_bwd")
    g_w_out = _matmul(mix_t, dx2b, name="dw_out")
    dmix = _matmul(dx2b, p["w_out"], name="d_mix", trans_b=True)

    delta, do_pat = _attn_delta(dmix, mix, expand.T, perms, "attn_delta")
    dqs, dks, dvs = [], [], []
    for d, qkv_p, do_p in zip(DILATIONS, qkv, [dmix] + do_pat):
        lse_p, dl_p = _to_pattern(lse_tot, d), _to_pattern(delta, d)
        dqs.append(_attn_bwd_dq(qkv_p, qkv_p, v_col, do_p, lse_p, dl_p, s // d, f"attn_bwd_dq_d{d}"))
        dk, dv = _attn_bwd_dkv(qkv_p, qkv_p, v_col, do_p, lse_p.T, dl_p.T, s // d, f"attn_bwd_dkv_d{d}")
        dks.append(dk)
        dvs.append(dv)
    dproj = _sum3_rope(dqs, perms, tabs_b, "rope_bwd_q", slab=(MAIN_W, 0, None))
    dproj = _sum3_rope(dks, perms, tabs_b, "rope_bwd_k", slab=(MAIN_W, D_MODEL, dproj))
    dproj = _sum3_rope(dvs, perms, None, "sum_dv", slab=(MAIN_W, 2 * D_MODEL, dproj))

    dy, dproj, dxs_skip, g_ssm_norm, g_dskip_lanes = _gate_bwd(yf, yb, xbc, proj, dskip_x, p["ssm_norm_w"], dmix, dproj, "ssm_gate_bwd")
    early_f = [_to_shards(g_w_up, 1), _to_shards(g_w_down, 0)] if early_exchange else []
    early_b = [_to_shards(g_w_out, 0)] if early_exchange else []
    dxbc_f, ddt_f, g_al_f, g_bi_f, *got_f = _ssd_bwd(xbc, dt_raw, dt_t, *ssd_args["f"], st_f, dy, 0, False, "ssd_bwd_f", exchange=early_f)
    dxbc_b, ddt_b, g_al_b, g_bi_b, *got_b = _ssd_bwd(xbc, dt_raw, dt_t, *ssd_args["b"], st_b, dy, N_HEADS, True, "ssd_bwd_b",
                                                     exchange=early_b)
    if early_exchange:
        (g_w_up, g_w_down), (g_w_out,) = got_f, got_b
    dpre, g_ssm_w3, g_ssm_cb = _conv_silu_bwd(proj, 3 * D_MODEL + D_INNER, XBC, ssm_w3, p["ssm_conv_b"],
                                              [dxbc_f, dxbc_b, dxs_skip], [XBC, XBC, D_INNER], "ssm_conv_bwd")
    dproj = _conv(dpre, 0, XBC, ssm_w3, p["ssm_conv_b"], False, "ssm_conv_bwd_x", transpose=True,
                  slab=(MAIN_W, 3 * D_MODEL + D_INNER, dproj))

    ddt =_pad_lanes(jnp.concatenate([ddt_f, ddt_b], axis=1), 128).astype(BF16)
    g_w_main = _matmul(h1t, dproj, name="dw_in")
    g_w_dt = _matmul(h1t, ddt, name="dw_in_dt")
    g_w_in = jnp.concatenate([g_w_main, g_w_dt[:, :2 * N_HEADS]], axis=1)
    late = [_to_shards(g_w_in, 1)] if early_exchange else []
    grad_x, _, g_norm1, *got = _proj_norm_bwd(dproj, w_main, ddt, w_dt, x, p["norm1_w"], dx2, "in_proj_norm1_bwd", exchange=late)
    if early_exchange:
        g_w_in = got[0]

    g_dskip = jnp.sum(g_dskip_lanes.reshape(N_HEADS, D_INNER // N_HEADS), axis=1).reshape(1, N_HEADS)
    small = {
        "norm1_w": g_norm1, "ssm_conv_w": g_ssm_w3.T, "ssm_conv_b": g_ssm_cb, "a_log_f": g_al_f, "a_log_b": g_al_b,
        "dt_bias_f": g_bi_f, "dt_bias_b": g_bi_b, "d_skip": g_dskip, "ssm_norm_w": g_ssm_norm, "norm2_w": g_norm2,
        "ffn_conv_w": jnp.concatenate([dwg, dwu], axis=1).T, "ffn_conv_b": jnp.concatenate([dbg, dbu], axis=1), "final_norm_w": g_final,
    }
    big = {"w_in": g_w_in, "w_out": g_w_out, "w_up": g_w_up, "w_down": g_w_down}
    return loss[0, 0], grad_x, big, small


SMALL_ORDER = ("norm1_w", "ssm_conv_w", "ssm_conv_b", "a_log_f", "a_log_b", "dt_bias_f", "dt_bias_b", "d_skip",
               "ssm_norm_w", "norm2_w", "ffn_conv_w", "ffn_conv_b", "final_norm_w")
SHARDED_SMALL = ("ssm_conv_w", "ffn_conv_w")
BIG_ORDER = ("w_in", "w_out", "w_up", "w_down")


def _pack(vals):
    rows = []
    for v in vals:
        f = v.reshape(-1).astype(F32)
        n = -(-f.shape[0] // 128) * 128
        rows.append(jnp.pad(f, (0, n - f.shape[0])).reshape(-1, 128))
    out = jnp.concatenate(rows, axis=0)
    pad = -out.shape[0] % 8
    return jnp.pad(out, ((0, pad), (0, 0)))


def _unpack(packed, shapes):
    out, r = [], 0
    for shp in shapes:
        n = math.prod(shp)
        nr = -(-n // 128)
        out.append(packed[r:r + nr].reshape(-1)[:n].reshape(shp))
        r += nr
    return out


def kernel(x, norm1_w, w_in, ssm_conv_w, ssm_conv_b, a_log_f, a_log_b, dt_bias_f, dt_bias_b, d_skip, ssm_norm_w, w_out, norm2_w, w_up, ffn_conv_w, ffn_conv_b, w_down, final_norm_w, loss_target, m_norm1_w, m_w_in, m_ssm_conv_w, m_ssm_conv_b, m_a_log_f, m_a_log_b, m_dt_bias_f, m_dt_bias_b, m_d_skip, m_ssm_norm_w, m_w_out, m_norm2_w, m_w_up, m_ffn_conv_w, m_ffn_conv_b, m_w_down, m_final_norm_w, v_norm1_w, v_w_in, v_ssm_conv_w, v_ssm_conv_b, v_a_log_f, v_a_log_b, v_dt_bias_f, v_dt_bias_b, v_d_skip, v_ssm_norm_w, v_w_out, v_norm2_w, v_w_up, v_ffn_conv_w, v_ffn_conv_b, v_w_down, v_final_norm_w):
    w = dict(norm1_w=norm1_w, w_in=w_in, ssm_conv_w=ssm_conv_w, ssm_conv_b=ssm_conv_b, a_log_f=a_log_f, a_log_b=a_log_b,
             dt_bias_f=dt_bias_f, dt_bias_b=dt_bias_b, d_skip=d_skip, ssm_norm_w=ssm_norm_w, w_out=w_out, norm2_w=norm2_w,
             w_up=w_up, ffn_conv_w=ffn_conv_w, ffn_conv_b=ffn_conv_b, w_down=w_down, final_norm_w=final_norm_w)
    mo = dict(norm1_w=m_norm1_w, w_in=m_w_in, ssm_conv_w=m_ssm_conv_w, ssm_conv_b=m_ssm_conv_b, a_log_f=m_a_log_f, a_log_b=m_a_log_b,
              dt_bias_f=m_dt_bias_f, dt_bias_b=m_dt_bias_b, d_skip=m_d_skip, ssm_norm_w=m_ssm_norm_w, w_out=m_w_out, norm2_w=m_norm2_w,
              w_up=m_w_up, ffn_conv_w=m_ffn_conv_w, ffn_conv_b=m_ffn_conv_b, w_down=m_w_down, final_norm_w=m_final_norm_w)
    vo = dict(norm1_w=v_norm1_w, w_in=v_w_in, ssm_conv_w=v_ssm_conv_w, ssm_conv_b=v_ssm_conv_b, a_log_f=v_a_log_f, a_log_b=v_a_log_b,
              dt_bias_f=v_dt_bias_f, dt_bias_b=v_dt_bias_b, d_skip=v_d_skip, ssm_norm_w=v_ssm_norm_w, w_out=v_w_out, norm2_w=v_norm2_w,
              w_up=v_w_up, ffn_conv_w=v_ffn_conv_w, ffn_conv_b=v_ffn_conv_b, w_down=v_w_down, final_norm_w=v_final_norm_w)
    me = _my_index()

    def first_full(got):
        g_in, g_conv = got
        conv_rows = [_unpack(g_conv[i], [ssm_conv_w.shape[1:], ffn_conv_w.shape[1:]]) for i in range(N_DEV)]
        return {"w_in": g_in.transpose(1, 0, 2).reshape(D_MODEL, N_DEV * w_in.shape[2]),
                "ssm_conv_w": jnp.concatenate([c[0] for c in conv_rows], axis=0),
                "ffn_conv_w": jnp.concatenate([c[1] for c in conv_rows], axis=0)}

    first = ([w["w_in"][0].astype(BF16), _pack([w["ssm_conv_w"][0], w["ffn_conv_w"][0]])], first_full)
    full = {k: w[k] for k in ("norm1_w", "ssm_conv_b", "a_log_f", "a_log_b", "dt_bias_f", "dt_bias_b", "d_skip", "ssm_norm_w", "norm2_w",
                              "ffn_conv_b", "final_norm_w")}
    late = [w["w_out"][0].astype(BF16), w["w_up"][0].astype(BF16), w["w_down"][0].astype(BF16)]

    loss_part, grad_x, big, small = _local_step(x[0], loss_target[0], full, first, late)

    small_shapes = [(1,)] + [small[k].shape for k in SMALL_ORDER]
    packed = _pack([loss_part] + [small[k] for k in SMALL_ORDER])
    out_small = jnp.broadcast_to(packed[None], (N_DEV,) + packed.shape)
    r_in, r_out, r_up, r_down = big["w_in"], big["w_out"], big["w_up"], big["w_down"]

    outs_g, outs_d, outs_m, outs_v = {}, {}, {}, {}
    for k, parts in zip(BIG_ORDER, (r_in, r_out, r_up, r_down)):
        host = [out_small] if k == BIG_ORDER[0] else []
        g, dlt, nm, nv, *got = _adamw(parts, w[k][0], mo[k][0], vo[k][0], f"adamw_{k}", exchange=host)
        if host:
            (r_small,) = got
        outs_g[k], outs_d[k], outs_m[k], outs_v[k] = g[None], dlt[None], nm[None], nv[None]
    tot = _unpack(_sum_parts(r_small, "small_grads_sum"), small_shapes)
    loss = tot[0][0]
    gs = dict(zip(SMALL_ORDER, tot[1:]))
    g_own = {}
    for k in SMALL_ORDER:
        if k in SHARDED_SMALL:
            rows = w[k].shape[1]
            g_own[k] = lax.dynamic_slice_in_dim(gs[k], me * rows, rows, axis=0)[None]
        else:
            g_own[k] = gs[k].reshape(w[k].shape)
    two_d = lambda a: a.reshape(-1, a.shape[-1])
    d_s, m_s, v_s = _adamw_small([two_d(g_own[k]) for k in SMALL_ORDER], [two_d(w[k]) for k in SMALL_ORDER],
                                 [two_d(mo[k]) for k in SMALL_ORDER], [two_d(vo[k]) for k in SMALL_ORDER], "adamw_small")
    for k, a, b, c in zip(SMALL_ORDER, d_s, m_s, v_s):
        shp = w[k].shape
        outs_g[k], outs_d[k], outs_m[k], outs_v[k] = g_own[k], a.reshape(shp), b.reshape(shp), c.reshape(shp)

    order = ("norm1_w", "w_in", "ssm_conv_w", "ssm_conv_b", "a_log_f", "a_log_b", "dt_bias_f", "dt_bias_b", "d_skip", "ssm_norm_w",
             "w_out", "norm2_w", "w_up", "ffn_conv_w", "ffn_conv_b", "w_down", "final_norm_w")
    return (loss, grad_x[None], *[outs_g[k] for k in order], *[outs_d[k] for k in order],
            *[outs_m[k] for k in order], *[outs_v[k] for k in order])
```
